```python
import jax
import jax.numpy as jnp
from jax import lax
import numpy as np

D_MODEL = 1024
BATCH = 16
SEQ = 2048
DEPTH = 2

M_HEADS = 4
M_DQK = 128
M_DV = 256
M_CONV = 4
M_CHUNK = 64
GATE_CAP = 15.0
G_HEADS = 4
G_DK = 128
G_DV = 256
G_RANK = 16
G_TAU = 16.0
G_CHUNK = 64
N_EXPERTS = 64
TOP_K = 8
N_GROUPS = 8
TOPK_GROUPS = 4
D_EXPERT = 256
D_SHARED = 256
ROUTED_SCALE = 2.5
MOE_BLOCK = 256
EPS = 1e-6

M_QK_COLS = 2 * M_HEADS * M_DQK
M_V_COLS = M_HEADS * M_DV
G_K_COLS = G_HEADS * G_DK
G_V_COLS = G_HEADS * G_DV
IN_SPLITS = (M_QK_COLS, M_V_COLS, M_V_COLS, M_HEADS, M_HEADS,
             G_K_COLS, G_K_COLS, G_V_COLS, G_RANK, G_V_COLS, D_MODEL, D_MODEL)
D_IN = M_QK_COLS + 2 * M_V_COLS + 2 * M_HEADS + 2 * G_K_COLS + 2 * G_V_COLS + G_RANK + 2 * D_MODEL

kernel_name = 'hybrid_mlstm_gla_moe_adaln'


def _rmsnorm(x, w):
    xf = x.astype(jnp.float32)
    y = xf * lax.rsqrt(jnp.mean(xf * xf, axis=-1, keepdims=True) + EPS)
    return (y * w.astype(jnp.float32)).astype(x.dtype)


def _head_rmsnorm(h, w):
    y = h * lax.rsqrt(jnp.mean(h * h, axis=-1, keepdims=True) + EPS)
    return y.reshape(h.shape[0], h.shape[1], -1) * w.astype(jnp.float32)


def _causal_conv(x, w):
    k = w.shape[0]
    s = x.shape[1]
    xp = jnp.pad(x, ((0, 0), (k - 1, 0), (0, 0)))
    return sum(xp[:, j:j + s] * w[j] for j in range(k))


def _to_chunks(t, size):
    b, s = t.shape[:2]
    t = t.reshape(b, s // size, size, *t.shape[2:])
    return jnp.moveaxis(t, 3, 1)


def _from_chunks(t):
    t = jnp.moveaxis(t, 1, 3)
    b, nc, l = t.shape[:3]
    return t.reshape(b, nc * l, *t.shape[3:])


def _mlstm_chunkwise(q, k, v, li, lf):
    f32 = jnp.float32
    q = _to_chunks(q.astype(f32) * (M_DQK ** -0.5), M_CHUNK)
    k = _to_chunks(k.astype(f32), M_CHUNK)
    v = _to_chunks(v.astype(f32), M_CHUNK)
    li = _to_chunks(li, M_CHUNK)
    b = jnp.cumsum(_to_chunks(lf, M_CHUNK), axis=-1)
    g = b[..., -1]
    a = g[..., None] - b + li

    def step(carry, inp):
        c_st, n_st, m_st = carry
        k_c, v_c, a_c, g_c = inp
        m_new = jnp.maximum(g_c + m_st, jnp.max(a_c, axis=-1))
        decay = jnp.exp(g_c + m_st - m_new)
        wk = jnp.exp(a_c - m_new[..., None])[..., None] * k_c
        c_new = decay[..., None, None] * c_st + jnp.einsum('bhld,bhle->bhde', wk, v_c)
        n_new = decay[..., None] * n_st + jnp.sum(wk, axis=2)
        return (c_new, n_new, m_new), (c_st, n_st, m_st)

    bsz, nh = q.shape[:2]
    init = (jnp.zeros((bsz, nh, M_DQK, M_DV), f32),
            jnp.zeros((bsz, nh, M_DQK), f32),
            jnp.zeros((bsz, nh), f32))
    xs = (jnp.moveaxis(k, 2, 0), jnp.moveaxis(v, 2, 0), jnp.moveaxis(a, 2, 0), jnp.moveaxis(g, 2, 0))
    _, (c_prev, n_prev, m_prev) = lax.scan(step, init, xs)
    c_prev = jnp.moveaxis(c_prev, 0, 2)
    n_prev = jnp.moveaxis(n_prev, 0, 2)
    m_prev = jnp.moveaxis(m_prev, 0, 2)

    causal = jnp.tril(jnp.ones((M_CHUNK, M_CHUNK), dtype=bool))
    d_mat = jnp.where(causal, b[..., :, None] - b[..., None, :] + li[..., None, :], -jnp.inf)
    m_inter = b + m_prev[..., None]
    m_t = jnp.maximum(jnp.max(d_mat, axis=-1), m_inter)
    p = jnp.exp(d_mat - m_t[..., None]) * jnp.einsum('bhcld,bhcsd->bhcls', q, k)
    w_inter = jnp.exp(m_inter - m_t)
    num = (jnp.einsum('bhcls,bhcse->bhcle', p, v)
           + w_inter[..., None] * jnp.einsum('bhcld,bhcde->bhcle', q, c_prev))
    den = jnp.sum(p, axis=-1) + w_inter * jnp.einsum('bhcld,bhcd->bhcl', q, n_prev)
    h = num / jnp.maximum(jnp.abs(den), jnp.exp(-m_t))[..., None]
    return _from_chunks(h)


def _gla_chunked(q, k, v, la):
    f32 = jnp.float32
    q = _to_chunks(q.astype(f32) * (G_DK ** -0.5), G_CHUNK)
    k = _to_chunks(k.astype(f32), G_CHUNK)
    v = _to_chunks(v.astype(f32), G_CHUNK)
    bc = jnp.cumsum(_to_chunks(la, G_CHUNK), axis=3)
    gc = bc[:, :, :, -1]
    q_in = q * jnp.exp(bc)
    k_in = k * jnp.exp(-bc)
    k_out = k * jnp.exp(gc[:, :, :, None, :] - bc)

    def step(s_st, inp):
        k_c, v_c, g_c = inp
        s_new = jnp.exp(g_c)[..., None] * s_st + jnp.einsum('bhld,bhle->bhde', k_c, v_c)
        return s_new, s_st

    bsz, nh = q.shape[:2]
    init = jnp.zeros((bsz, nh, G_DK, G_DV), f32)
    xs = (jnp.moveaxis(k_out, 2, 0), jnp.moveaxis(v, 2, 0), jnp.moveaxis(gc, 2, 0))
    _, s_prev = lax.scan(step, init, xs)
    s_prev = jnp.moveaxis(s_prev, 0, 2)

    causal = jnp.tril(jnp.ones((G_CHUNK, G_CHUNK), dtype=bool))
    att = jnp.where(causal, jnp.einsum('bhcld,bhcsd->bhcls', q_in, k_in), 0.0)
    o = jnp.einsum('bhcls,bhcse->bhcle', att, v) + jnp.einsum('bhcld,bhcde->bhcle', q_in, s_prev)
    return _from_chunks(o)


def _mixer(u, w_in, conv_w, gate_b, m_norm_w, alpha_w, alpha_b, g_norm_w, w_pa, w_pb, w_o):
    f32 = jnp.float32
    dt = u.dtype
    bsz, s = u.shape[:2]
    proj = u @ w_in
    parts = []
    off = 0
    for n in IN_SPLITS:
        parts.append(proj[..., off:off + n])
        off += n
    qk_m, v_m, o_m, i_m, f_m, q_g, k_g, v_g, r_g, z_g, g_a, g_b = parts

    qk_m = jax.nn.silu(_causal_conv(qk_m, conv_w))
    half = M_QK_COLS // 2
    q_m = qk_m[..., :half].reshape(bsz, s, M_HEADS, M_DQK)
    k_m = qk_m[..., half:].reshape(bsz, s, M_HEADS, M_DQK)
    li = GATE_CAP * jnp.tanh((i_m.astype(f32) + gate_b[0]) / GATE_CAP)
    lf = jax.nn.log_sigmoid(GATE_CAP * jnp.tanh((f_m.astype(f32) + gate_b[1]) / GATE_CAP))
    h_a = _mlstm_chunkwise(q_m, k_m, v_m.reshape(bsz, s, M_HEADS, M_DV), li, lf)
    h_a = _head_rmsnorm(h_a, m_norm_w) * jax.nn.sigmoid(o_m.astype(f32))

    la = jax.nn.log_sigmoid((r_g @ alpha_w + alpha_b).astype(f32)) / G_TAU
    h_b = _gla_chunked(q_g.reshape(bsz, s, G_HEADS, G_DK), k_g.reshape(bsz, s, G_HEADS, G_DK),
                       v_g.reshape(bsz, s, G_HEADS, G_DV), la.reshape(bsz, s, G_HEADS, G_DK))
    h_b = _head_rmsnorm(h_b, g_norm_w) * jax.nn.silu(z_g.astype(f32))

    y = (jax.nn.sigmoid(g_a) * (h_a.astype(dt) @ w_pa)
         + jax.nn.sigmoid(g_b) * (h_b.astype(dt) @ w_pb))
    return y @ w_o


def _moe(u, router_w, router_b, w1, w3, w2, s1, s3, s2):
    bsz, s, d = u.shape
    t = bsz * s
    xm = u.reshape(t, d)
    scores = jax.nn.sigmoid((xm @ router_w).astype(jnp.float32))
    sel = scores + router_b.astype(jnp.float32)
    grp = sel.reshape(t, N_GROUPS, N_EXPERTS // N_GROUPS)
    grp_score = jnp.sum(lax.top_k(grp, 2)[0], axis=-1)
    _, gidx = lax.top_k(grp_score, TOPK_GROUPS)
    gmask = jnp.any(gidx[..., None] == jnp.arange(N_GROUPS), axis=1)
    gmask = jnp.repeat(gmask, N_EXPERTS // N_GROUPS, axis=-1)
    _, eidx = lax.top_k(jnp.where(gmask, sel, -jnp.inf), TOP_K)
    wts = jnp.take_along_axis(scores, eidx, axis=-1)
    wts = wts / jnp.sum(wts, axis=-1, keepdims=True) * ROUTED_SCALE

    n_assign = t * TOP_K
    flat_e = eidx.reshape(-1)
    flat_tok = jnp.arange(n_assign, dtype=jnp.int32) // TOP_K
    flat_w = wts.reshape(-1)
    order = jnp.argsort(flat_e)
    se, stok, sw = flat_e[order], flat_tok[order], flat_w[order]
    counts = jnp.bincount(flat_e, length=N_EXPERTS)
    padded = (counts + MOE_BLOCK - 1) // MOE_BLOCK * MOE_BLOCK
    pend = jnp.cumsum(padded)
    pstart = pend - padded
    sstart = jnp.cumsum(counts) - counts
    dest = pstart[se] + jnp.arange(n_assign) - sstart[se]
    n_rows = n_assign + N_EXPERTS * MOE_BLOCK
    n_blocks = n_rows // MOE_BLOCK
    row_tok = jnp.zeros((n_rows,), jnp.int32).at[dest].set(stok)
    row_w = jnp.zeros((n_rows,), jnp.float32).at[dest].set(sw)
    blk_e = jnp.minimum(jnp.searchsorted(pend, jnp.arange(n_blocks) * MOE_BLOCK, side='right'),
                        N_EXPERTS - 1)

    def run_block(args):
        tok_b, w_b, e = args
        xb = xm[tok_b]
        hb = jax.nn.silu(xb @ w1[e]) * (xb @ w3[e])
        return ((hb @ w2[e]) * w_b[:, None]).astype(xm.dtype)

    out = lax.map(run_block, (row_tok.reshape(n_blocks, MOE_BLOCK),
                              row_w.reshape(n_blocks, MOE_BLOCK), blk_e))
    routed = jax.ops.segment_sum(out.reshape(n_rows, d), row_tok, num_segments=t)
    shared = (jax.nn.silu(xm @ s1) * (xm @ s3)) @ s2
    return (routed + shared).astype(u.dtype).reshape(bsz, s, d)


def setup_inputs(seed: int = 0) -> dict:
    key = jax.random.key(seed)
    ks = jax.random.split(key, 26)

    def nrm(k, shape, scale):
        return jax.random.normal(k, shape, jnp.float32) * scale

    L = DEPTH
    D = D_MODEL
    return {
        'x': nrm(ks[0], (BATCH, SEQ, D), 1.0),
        'c': nrm(ks[1], (BATCH, D), 1.0),
        'ada_w': nrm(ks[2], (L, D, 6 * D), 0.5 * D ** -0.5),
        'ada_b': nrm(ks[3], (L, 6 * D), 0.02),
        'norm_mix_w': 1.0 + nrm(ks[4], (L, D), 0.02),
        'norm_moe_w': 1.0 + nrm(ks[5], (L, D), 0.02),
        'w_in': nrm(ks[6], (L, D, D_IN), D ** -0.5),
        'm_conv_w': nrm(ks[7], (L, M_CONV, M_QK_COLS), M_CONV ** -0.5),
        'm_gate_b': jnp.stack([nrm(ks[8], (L, M_HEADS), 0.1),
                               3.0 + nrm(ks[9], (L, M_HEADS), 0.5)], axis=1),
        'm_norm_w': 1.0 + nrm(ks[10], (L, M_V_COLS), 0.02),
        'g_alpha_w': nrm(ks[11], (L, G_RANK, G_K_COLS), G_RANK ** -0.5),
        'g_alpha_b': nrm(ks[12], (L, G_K_COLS), 0.1),
        'g_norm_w': 1.0 + nrm(ks[13], (L, G_V_COLS), 0.02),
        'w_pa': nrm(ks[14], (L, M_V_COLS, D), M_V_COLS ** -0.5),
        'w_pb': nrm(ks[15], (L, G_V_COLS, D), G_V_COLS ** -0.5),
        'w_o': nrm(ks[16], (L, D, D), D ** -0.5),
        'router_w': nrm(ks[17], (L, D, N_EXPERTS), D ** -0.5),
        'router_b': nrm(ks[18], (L, N_EXPERTS), 0.01),
        'exp_w1': nrm(ks[19], (L, N_EXPERTS, D, D_EXPERT), D ** -0.5),
        'exp_w3': nrm(ks[20], (L, N_EXPERTS, D, D_EXPERT), D ** -0.5),
        'exp_w2': nrm(ks[21], (L, N_EXPERTS, D_EXPERT, D), D_EXPERT ** -0.5),
        'sh_w1': nrm(ks[22], (L, D, D_SHARED), D ** -0.5),
        'sh_w3': nrm(ks[23], (L, D, D_SHARED), D ** -0.5),
        'sh_w2': nrm(ks[24], (L, D_SHARED, D), D_SHARED ** -0.5),
        'final_norm_w': 1.0 + nrm(ks[25], (D,), 0.02),
    }


def reference(x, c, ada_w, ada_b, norm_mix_w, norm_moe_w, w_in, m_conv_w, m_gate_b, m_norm_w,
              g_alpha_w, g_alpha_b, g_norm_w, w_pa, w_pb, w_o, router_w, router_b,
              exp_w1, exp_w3, exp_w2, sh_w1, sh_w3, sh_w2, final_norm_w):
    h = x
    cond = jax.nn.silu(c)
    for l in range(DEPTH):
        ada = cond @ ada_w[l] + ada_b[l]
        sh1, sc1, gt1, sh2, sc2, gt2 = jnp.split(ada, 6, axis=-1)
        u = _rmsnorm(h, norm_mix_w[l]) * (1.0 + sc1[:, None, :]) + sh1[:, None, :]
        h = h + gt1[:, None, :] * _mixer(u, w_in[l], m_conv_w[l], m_gate_b[l], m_norm_w[l],
                                         g_alpha_w[l], g_alpha_b[l], g_norm_w[l],
                                         w_pa[l], w_pb[l], w_o[l])
        u = _rmsnorm(h, norm_moe_w[l]) * (1.0 + sc2[:, None, :]) + sh2[:, None, :]
        h = h + gt2[:, None, :] * _moe(u, router_w[l], router_b[l], exp_w1[l], exp_w3[l], exp_w2[l],
                                       sh_w1[l], sh_w3[l], sh_w2[l])
    return _rmsnorm(h, final_norm_w)
```

```python
import functools

import jax
import jax.numpy as jnp
from jax import lax
from jax.experimental import pallas as pl
from jax.experimental.pallas import tpu as pltpu

F32 = jnp.float32
BF16 = jnp.bfloat16
I32 = jnp.int32
HIGHEST = lax.Precision.HIGHEST

D_MODEL = 1024
M_HEADS = 4
M_DQK = 128
M_DV = 256
M_CONV = 4
GATE_CAP = 15.0
G_HEADS = 4
G_DK = 128
G_DV = 256
G_RANK = 16
G_TAU = 16.0
G_CHUNK = 64
N_EXPERTS = 64
TOP_K = 8
N_GROUPS = 8
GROUP_SIZE = N_EXPERTS // N_GROUPS
TOPK_GROUPS = 4
D_EXPERT = 256
D_SHARED = 256
ROUTED_SCALE = 2.5
EPS = 1e-6

M_CHUNK_ROWS = 256
G_BLOCK_ROWS = 256
PROJ_ROWS = 1024
MIX_ROWS = 512
ROUTE_ROWS = 512
DISPATCH_ROWS = 512
EXPERT_ROWS = 256
COMBINE_ROWS = 128
NORM_ROWS = 1024
CONV_HALO = 8
VMEM_LIMIT = 48 * 1024 * 1024

COL_QM, COL_KM, COL_VM, COL_OM = 0, 512, 1024, 2048
COL_QG, COL_KG, COL_VG, COL_ZG, COL_GA, COL_GB = 3072, 3584, 4096, 5120, 6144, 7168
BIG_COLS = 8192
SMALL_COLS = 128
SMALL_I, SMALL_F, SMALL_R = 0, M_HEADS, 2 * M_HEADS


def _cparams(sem, vmem=VMEM_LIMIT):
    return pltpu.CompilerParams(dimension_semantics=sem, vmem_limit_bytes=vmem)


def _silu(x):
    return x * jax.nn.sigmoid(x)


def _log_sigmoid(x):
    return jnp.minimum(x, 0.0) - jnp.log1p(jnp.exp(-jnp.abs(x)))


def _modulated_rmsnorm(x, w, sc, sh):
    y = x * lax.rsqrt(jnp.mean(x * x, axis=-1, keepdims=True) + EPS)
    return (y * w) * (1.0 + sc) + sh


def _lower_tri(n, dtype):
    r = lax.broadcasted_iota(I32, (n, n), 0)
    c = lax.broadcasted_iota(I32, (n, n), 1)
    return (r >= c).astype(dtype)


def _ada_body(c_ref, w_ref, b_ref, o_ref):
    cond = _silu(c_ref[...])
    o_ref[0] = jnp.dot(cond.astype(BF16), w_ref[0].astype(BF16),
                       preferred_element_type=F32) + b_ref[0]


def _ada(c, ada_w, ada_b):
    depth, d, six_d = ada_w.shape
    bsz = c.shape[0]
    nj = six_d // d
    return pl.pallas_call(
        _ada_body,
        grid=(depth, nj),
        in_specs=[pl.BlockSpec((bsz, d), lambda l, j: (0, 0)),
                  pl.BlockSpec((1, d, d), lambda l, j: (l, 0, j)),
                  pl.BlockSpec((1, 1, d), lambda l, j: (l, 0, j))],
        out_specs=pl.BlockSpec((1, bsz, d), lambda l, j: (l, 0, j)),
        out_shape=jax.ShapeDtypeStruct((depth, bsz, six_d), F32),
        compiler_params=_cparams(("parallel", "parallel")),
        name="ada_ln",
    )(c, ada_w, ada_b.reshape(depth, 1, six_d))


def _inproj_body(h_ref, sh_ref, sc_ref, nw_ref, wb_ref, ws_ref, big_ref, small_ref, u_scr):
    @pl.when(pl.program_id(1) == 0)
    def _():
        u = _modulated_rmsnorm(h_ref[...], nw_ref[...], sc_ref[0], sh_ref[0])
        ub = u.astype(BF16)
        u_scr[...] = ub
        small_ref[...] = jnp.dot(ub, ws_ref[...], preferred_element_type=F32)

    big_ref[...] = jnp.dot(u_scr[...], wb_ref[...], preferred_element_type=F32).astype(BF16)


def _inproj(h, sh, sc, nw, w_big, w_small, seq):
    t, d = h.shape
    tm = min(PROJ_ROWS, seq)
    per_b = seq // tm
    tn = 1024
    return pl.pallas_call(
        _inproj_body,
        grid=(t // tm, BIG_COLS // tn),
        in_specs=[pl.BlockSpec((tm, d), lambda i, j: (i, 0)),
                  pl.BlockSpec((1, 1, d), lambda i, j: (i // per_b, 0, 0)),
                  pl.BlockSpec((1, 1, d), lambda i, j: (i // per_b, 0, 0)),
                  pl.BlockSpec((1, d), lambda i, j: (0, 0)),
                  pl.BlockSpec((d, tn), lambda i, j: (0, j)),
                  pl.BlockSpec((d, SMALL_COLS), lambda i, j: (0, 0))],
        out_specs=[pl.BlockSpec((tm, tn), lambda i, j: (i, j)),
                   pl.BlockSpec((tm, SMALL_COLS), lambda i, j: (i, 0))],
        out_shape=[jax.ShapeDtypeStruct((t, BIG_COLS), BF16),
                   jax.ShapeDtypeStruct((t, SMALL_COLS), F32)],
        scratch_shapes=[pltpu.VMEM((tm, d), BF16)],
        compiler_params=_cparams(("parallel", "arbitrary")),
        name="in_proj",
    )(h, sh, sc, nw, w_big, w_small)


def _mlstm_body(q_ref, k_ref, v_ref, o_ref, small_ref, cw_ref, gb_ref, nw_ref, ha_ref,
                xe_scr, c_scr, n_scr, m_scr):
    rows = q_ref.shape[0]
    half = M_HEADS * M_DQK

    @pl.when(pl.program_id(1) == 0)
    def _():
        xe_scr[0:CONV_HALO, :] = jnp.zeros((CONV_HALO, 2 * half), F32)
        c_scr[...] = jnp.zeros(c_scr.shape, F32)
        n_scr[...] = jnp.zeros(n_scr.shape, F32)
        m_scr[...] = jnp.zeros(m_scr.shape, F32)

    xe_scr[CONV_HALO:CONV_HALO + rows, 0:half] = q_ref[...].astype(F32)
    xe_scr[CONV_HALO:CONV_HALO + rows, half:2 * half] = k_ref[...].astype(F32)
    conv = None
    for j in range(M_CONV):
        off = CONV_HALO - (M_CONV - 1) + j
        term = xe_scr[off:off + rows, :] * cw_ref[j:j + 1, :]
        conv = term if conv is None else conv + term
    qk = _silu(conv)
    xe_scr[0:CONV_HALO, :] = xe_scr[rows:rows + CONV_HALO, :]

    capped = GATE_CAP * jnp.tanh((small_ref[...] + gb_ref[...]) / GATE_CAP)
    li_all = capped
    lf_all = _log_sigmoid(capped)
    b_all = jnp.dot(_lower_tri(rows, F32), lf_all, precision=HIGHEST, preferred_element_type=F32)
    li_t = li_all.T
    b_t = b_all.T

    r_io = lax.broadcasted_iota(I32, (rows, rows), 0)
    c_io = lax.broadcasted_iota(I32, (rows, rows), 1)
    causal = r_io >= c_io

    for h in range(M_HEADS):
        q = (qk[:, h * M_DQK:(h + 1) * M_DQK] * (M_DQK ** -0.5)).astype(BF16)
        k = qk[:, half + h * M_DQK:half + (h + 1) * M_DQK]
        kb = k.astype(BF16)
        v = v_ref[:, h * M_DV:(h + 1) * M_DV]
        li_c = li_all[:, SMALL_I + h:SMALL_I + h + 1]
        b_c = b_all[:, SMALL_F + h:SMALL_F + h + 1]
        li_r = li_t[SMALL_I + h:SMALL_I + h + 1, :]
        b_r = b_t[SMALL_F + h:SMALL_F + h + 1, :]
        g = b_c[rows - 1:rows, :]
        m_prev = m_scr[h][:, 0:1]
        c_prev = c_scr[h]
        n_prev = n_scr[h]

        d_mat = jnp.where(causal, b_c - b_r + li_r, -jnp.inf)
        m_inter = b_c + m_prev
        m_t = jnp.maximum(jnp.max(d_mat, axis=1, keepdims=True), m_inter)
        s = lax.dot_general(q, kb, (((1,), (1,)), ((), ())), preferred_element_type=F32)
        p = jnp.exp(d_mat - m_t) * s
        w_inter = jnp.exp(m_inter - m_t)
        num = (jnp.dot(p.astype(BF16), v, preferred_element_type=F32)
               + w_inter * jnp.dot(q, c_prev.astype(BF16), preferred_element_type=F32))
        qn = jnp.sum(q.astype(F32) * n_prev, axis=1, keepdims=True)
        den = jnp.sum(p, axis=1, keepdims=True) + w_inter * qn
        hh = num / jnp.maximum(jnp.abs(den), jnp.exp(-m_t))

        a_r = g - b_r + li_r
        a_c = g - b_c + li_c
        m_new = jnp.maximum(g + m_prev, jnp.max(a_r, axis=1, keepdims=True))
        decay = jnp.exp(g + m_prev - m_new)
        wk = jnp.exp(a_c - m_new) * k
        c_scr[h] = decay * c_prev + lax.dot_general(
            wk.astype(BF16), v, (((0,), (0,)), ((), ())), preferred_element_type=F32)
        n_scr[h] = decay * n_prev + jnp.sum(wk, axis=0, keepdims=True)
        m_scr[h] = jnp.broadcast_to(m_new, m_scr.shape[1:])

        y = hh * lax.rsqrt(jnp.mean(hh * hh, axis=-1, keepdims=True) + EPS)
        y = y * nw_ref[:, h * M_DV:(h + 1) * M_DV]
        gate = jax.nn.sigmoid(o_ref[:, h * M_DV:(h + 1) * M_DV].astype(F32))
        ha_ref[:, h * M_DV:(h + 1) * M_DV] = (y * gate).astype(BF16)


def _mlstm(big, small, conv_w, gate_row, norm_w, bsz, seq):
    t = big.shape[0]
    rows = min(M_CHUNK_ROWS, seq)
    nc = seq // rows
    half = M_HEADS * M_DQK
    vcols = M_HEADS * M_DV
    row_map = lambda b, c: b * nc + c
    return pl.pallas_call(
        _mlstm_body,
        grid=(bsz, nc),
        in_specs=[pl.BlockSpec((rows, half), lambda b, c: (row_map(b, c), COL_QM // half)),
                  pl.BlockSpec((rows, half), lambda b, c: (row_map(b, c), COL_KM // half)),
                  pl.BlockSpec((rows, vcols), lambda b, c: (row_map(b, c), COL_VM // vcols)),
                  pl.BlockSpec((rows, vcols), lambda b, c: (row_map(b, c), COL_OM // vcols)),
                  pl.BlockSpec((rows, SMALL_COLS), lambda b, c: (row_map(b, c), 0)),
                  pl.BlockSpec((M_CONV, 2 * half), lambda b, c: (0, 0)),
                  pl.BlockSpec((1, SMALL_COLS), lambda b, c: (0, 0)),
                  pl.BlockSpec((1, vcols), lambda b, c: (0, 0))],
        out_specs=pl.BlockSpec((rows, vcols), lambda b, c: (row_map(b, c), 0)),
        out_shape=jax.ShapeDtypeStruct((t, vcols), BF16),
        scratch_shapes=[pltpu.VMEM((rows + CONV_HALO, 2 * half), F32),
                        pltpu.VMEM((M_HEADS, M_DQK, M_DV), F32),
                        pltpu.VMEM((M_HEADS, 1, M_DQK), F32),
                        pltpu.VMEM((M_HEADS, 1, 128), F32)],
        compiler_params=_cparams(("parallel", "arbitrary")),
        name="mlstm",
    )(big, big, big, big, small, conv_w, gate_row, norm_w)


def _gla_body(q_ref, k_ref, v_ref, z_ref, small_ref, aw_ref, ab_ref, nw_ref, hb_ref,
              la_scr, s_scr):
    rows = q_ref.shape[0]
    n_chunks = rows // G_CHUNK

    @pl.when(pl.program_id(1) == 0)
    def _():
        s_scr[...] = jnp.zeros(s_scr.shape, F32)

    logits = jnp.dot(small_ref[...].astype(BF16), aw_ref[...], preferred_element_type=F32)
    la_scr[...] = _log_sigmoid(logits + ab_ref[...]) / G_TAU

    tri = _lower_tri(G_CHUNK, F32)
    r_io = lax.broadcasted_iota(I32, (G_CHUNK, G_CHUNK), 0)
    c_io = lax.broadcasted_iota(I32, (G_CHUNK, G_CHUNK), 1)
    causal = r_io >= c_io

    def chunk(ci, carry):
        r0 = pl.multiple_of(ci * G_CHUNK, G_CHUNK)
        rs = pl.ds(r0, G_CHUNK)
        bc = jnp.dot(tri, la_scr[rs, :], precision=HIGHEST, preferred_element_type=F32)
        gc = bc[G_CHUNK - 1:G_CHUNK, :]
        for h in range(G_HEADS):
            ks = slice(h * G_DK, (h + 1) * G_DK)
            vs = slice(h * G_DV, (h + 1) * G_DV)
            q = q_ref[rs, ks].astype(F32) * (G_DK ** -0.5)
            k = k_ref[rs, ks].astype(F32)
            v = v_ref[rs, vs]
            bch = bc[:, ks]
            gch = gc[:, ks]
            q_in = (q * jnp.exp(bch)).astype(BF16)
            k_in = (k * jnp.exp(-bch)).astype(BF16)
            k_out = (k * jnp.exp(gch - bch)).astype(BF16)
            s_prev = s_scr[h]
            att = lax.dot_general(q_in, k_in, (((1,), (1,)), ((), ())), preferred_element_type=F32)
            att = jnp.where(causal, att, 0.0)
            o = (jnp.dot(att.astype(BF16), v, preferred_element_type=F32)
                 + jnp.dot(q_in, s_prev.astype(BF16), preferred_element_type=F32))
            eg_col = jnp.broadcast_to(jnp.exp(gch), (G_DK, G_DK)).T[:, 0:1]
            s_scr[h] = eg_col * s_prev + lax.dot_general(
                k_out, v, (((0,), (0,)), ((), ())), preferred_element_type=F32)
            y = o * lax.rsqrt(jnp.mean(o * o, axis=-1, keepdims=True) + EPS)
            y = y * nw_ref[:, vs]
            hb_ref[rs, vs] = (y * _silu(z_ref[rs, vs].astype(F32))).astype(BF16)
        return carry

    lax.fori_loop(0, n_chunks, chunk, 0)


def _gla(big, small, alpha_full, alpha_b, norm_w, bsz, seq):
    t = big.shape[0]
    rows = min(G_BLOCK_ROWS, seq)
    nb = seq // rows
    kcols = G_HEADS * G_DK
    vcols = G_HEADS * G_DV
    row_map = lambda b, c: b * nb + c
    return pl.pallas_call(
        _gla_body,
        grid=(bsz, nb),
        in_specs=[pl.BlockSpec((rows, kcols), lambda b, c: (row_map(b, c), COL_QG // kcols)),
                  pl.BlockSpec((rows, kcols), lambda b, c: (row_map(b, c), COL_KG // kcols)),
                  pl.BlockSpec((rows, vcols), lambda b, c: (row_map(b, c), COL_VG // vcols)),
                  pl.BlockSpec((rows, vcols), lambda b, c: (row_map(b, c), COL_ZG // vcols)),
                  pl.BlockSpec((rows, SMALL_COLS), lambda b, c: (row_map(b, c), 0)),
                  pl.BlockSpec((SMALL_COLS, kcols), lambda b, c: (0, 0)),
                  pl.BlockSpec((1, kcols), lambda b, c: (0, 0)),
                  pl.BlockSpec((1, vcols), lambda b, c: (0, 0))],
        out_specs=pl.BlockSpec((rows, vcols), lambda b, c: (row_map(b, c), 0)),
        out_shape=jax.ShapeDtypeStruct((t, vcols), BF16),
        scratch_shapes=[pltpu.VMEM((rows, kcols), F32),
                        pltpu.VMEM((G_HEADS, G_DK, G_DV), F32)],
        compiler_params=_cparams(("parallel", "arbitrary")),
        name="gla",
    )(big, big, big, big, small, alpha_full, alpha_b, norm_w)


def _mixout_body(h_ref, ha_ref, hb_ref, ga_ref, gb_ref, gt_ref, wpa_ref, wpb_ref, wo_ref, o_ref):
    a = jnp.dot(ha_ref[...], wpa_ref[...], preferred_element_type=F32)
    b = jnp.dot(hb_ref[...], wpb_ref[...], preferred_element_type=F32)
    y = (jax.nn.sigmoid(ga_ref[...].astype(F32)) * a
         + jax.nn.sigmoid(gb_ref[...].astype(F32)) * b)
    o_ref[...] = h_ref[...] + gt_ref[0] * jnp.dot(y.astype(BF16), wo_ref[...],
                                                  preferred_element_type=F32)


def _mixout(h, ha, hb, big, gt, w_pa, w_pb, w_o, seq):
    t, d = h.shape
    tm = min(MIX_ROWS, seq)
    per_b = seq // tm
    wspec = pl.BlockSpec((d, d), lambda i: (0, 0))
    return pl.pallas_call(
        _mixout_body,
        grid=(t // tm,),
        in_specs=[pl.BlockSpec((tm, d), lambda i: (i, 0)),
                  pl.BlockSpec((tm, d), lambda i: (i, 0)),
                  pl.BlockSpec((tm, d), lambda i: (i, 0)),
                  pl.BlockSpec((tm, d), lambda i: (i, COL_GA // d)),
                  pl.BlockSpec((tm, d), lambda i: (i, COL_GB // d)),
                  pl.BlockSpec((1, 1, d), lambda i: (i // per_b, 0, 0)),
                  wspec, wspec, wspec],
        out_specs=pl.BlockSpec((tm, d), lambda i: (i, 0)),
        out_shape=jax.ShapeDtypeStruct((t, d), F32),
        compiler_params=_cparams(("parallel",)),
        name="mix_out",
    )(h, ha, hb, big, big, gt, w_pa, w_pb, w_o)


def _route_body(h_ref, sh_ref, sc_ref, nw_ref, rwt_ref, rb_ref,
                u_ref, eidx_ref, wts_ref, pos_ref, cnt_ref, carry_scr):
    tm = h_ref.shape[0]

    @pl.when(pl.program_id(0) == 0)
    def _():
        carry_scr[...] = jnp.zeros(carry_scr.shape, F32)

    u = _modulated_rmsnorm(h_ref[...], nw_ref[...], sc_ref[0], sh_ref[0])
    u_ref[...] = u
    logits = lax.dot_general(rwt_ref[...], u, (((1,), (1,)), ((), ())),
                             precision=HIGHEST, preferred_element_type=F32)
    scores = jax.nn.sigmoid(logits)
    sel = scores + rb_ref[...]

    neg = -jnp.inf
    sub_io = lax.broadcasted_iota(I32, (GROUP_SIZE, tm), 0)
    pieces = []
    for g in range(N_GROUPS):
        blk = sel[g * GROUP_SIZE:(g + 1) * GROUP_SIZE, :]
        m1 = jnp.max(blk, axis=0, keepdims=True)
        first = jnp.min(jnp.where(blk == m1, sub_io, GROUP_SIZE), axis=0, keepdims=True)
        m2 = jnp.max(jnp.where(sub_io == first, neg, blk), axis=0, keepdims=True)
        pieces.append(jnp.broadcast_to(m1 + m2, (GROUP_SIZE, tm)))
    gscore = jnp.concatenate(pieces, axis=0)

    e_io = lax.broadcasted_iota(I32, (N_EXPERTS, tm), 0)
    grp_io = e_io // GROUP_SIZE
    gmask = jnp.zeros((N_EXPERTS, tm), jnp.bool_)
    for _ in range(TOPK_GROUPS):
        mx = jnp.max(gscore, axis=0, keepdims=True)
        gi = jnp.min(jnp.where(gscore == mx, grp_io, N_GROUPS), axis=0, keepdims=True)
        hit = grp_io == gi
        gmask = jnp.logical_or(gmask, hit)
        gscore = jnp.where(hit, neg, gscore)

    cur = jnp.where(gmask, sel, neg)
    row_io = lax.broadcasted_iota(I32, (TOP_K, tm), 0)
    eidx = jnp.zeros((TOP_K, tm), I32)
    wraw = jnp.zeros((TOP_K, tm), F32)
    chosen = jnp.zeros((N_EXPERTS, tm), jnp.bool_)
    hits = []
    for kk in range(TOP_K):
        mx = jnp.max(cur, axis=0, keepdims=True)
        ei = jnp.min(jnp.where(cur == mx, e_io, N_EXPERTS), axis=0, keepdims=True)
        hit = e_io == ei
        hits.append(hit)
        sc_k = jnp.sum(jnp.where(hit, scores, 0.0), axis=0, keepdims=True)
        eidx = jnp.where(row_io == kk, ei, eidx)
        wraw = jnp.where(row_io == kk, sc_k, wraw)
        chosen = jnp.logical_or(chosen, hit)
        cur = jnp.where(hit, neg, cur)

    wsum = jnp.sum(wraw, axis=0, keepdims=True)
    wts_ref[...] = wraw / wsum * ROUTED_SCALE
    eidx_ref[...] = eidx

    chosen_f = jnp.where(chosen, 1.0, 0.0)
    r_io = lax.broadcasted_iota(I32, (tm, tm), 0)
    c_io = lax.broadcasted_iota(I32, (tm, tm), 1)
    strict_upper = jnp.where(r_io < c_io, 1.0, 0.0).astype(BF16)
    prefix = jnp.dot(chosen_f.astype(BF16), strict_upper, preferred_element_type=F32)
    rank = prefix + carry_scr[:, 0:1]
    pos = jnp.zeros((TOP_K, tm), F32)
    for kk in range(TOP_K):
        p_k = jnp.sum(jnp.where(hits[kk], rank, 0.0), axis=0, keepdims=True)
        pos = jnp.where(row_io == kk, p_k, pos)
    pos_ref[...] = pos.astype(I32)
    total = carry_scr[...] + jnp.sum(chosen_f, axis=1, keepdims=True)
    carry_scr[...] = total
    cnt_ref[...] = total.astype(I32)


def _route(h, sh, sc, nw, rw_t, rb_col, seq):
    t, d = h.shape
    tm = min(ROUTE_ROWS, seq)
    per_b = seq // tm
    kspec = pl.BlockSpec((TOP_K, tm), lambda i: (0, i))
    return pl.pallas_call(
        _route_body,
        grid=(t // tm,),
        in_specs=[pl.BlockSpec((tm, d), lambda i: (i, 0)),
                  pl.BlockSpec((1, 1, d), lambda i: (i // per_b, 0, 0)),
                  pl.BlockSpec((1, 1, d), lambda i: (i // per_b, 0, 0)),
                  pl.BlockSpec((1, d), lambda i: (0, 0)),
                  pl.BlockSpec((N_EXPERTS, d), lambda i: (0, 0)),
                  pl.BlockSpec((N_EXPERTS, 1), lambda i: (0, 0))],
        out_specs=[pl.BlockSpec((tm, d), lambda i: (i, 0)), kspec, kspec, kspec,
                   pl.BlockSpec((N_EXPERTS, 128), lambda i: (0, 0))],
        out_shape=[jax.ShapeDtypeStruct((t, d), F32),
                   jax.ShapeDtypeStruct((TOP_K, t), I32),
                   jax.ShapeDtypeStruct((TOP_K, t), F32),
                   jax.ShapeDtypeStruct((TOP_K, t), I32),
                   jax.ShapeDtypeStruct((N_EXPERTS, 128), I32)],
        scratch_shapes=[pltpu.VMEM((N_EXPERTS, 128), F32)],
        compiler_params=_cparams(("arbitrary",)),
        name="moe_route",
    )(h, sh, sc, nw, rw_t, rb_col)


def _dispatch_body(dest_ref, u_hbm, xg_in_hbm, xg_hbm, sem):
    del xg_in_hbm
    tm = dest_ref.shape[1]
    base = pl.program_id(0) * tm

    def row_copy(t, kk):
        return pltpu.make_async_copy(u_hbm.at[pl.ds(base + t, 1)],
                                     xg_hbm.at[pl.ds(dest_ref[kk, t], 1)], sem)

    def issue(t, carry):
        for kk in range(TOP_K):
            row_copy(t, kk).start()
        return carry

    def drain(t, carry):
        for kk in range(TOP_K):
            row_copy(t, kk).wait()
        return carry

    lax.fori_loop(0, tm, issue, 0)
    lax.fori_loop(0, tm, drain, 0)


def _dispatch(u, dest, n_rows):
    t, d = u.shape
    tm = min(DISPATCH_ROWS, t)
    xg0 = jnp.zeros((n_rows, d), F32)
    return pl.pallas_call(
        _dispatch_body,
        grid=(t // tm,),
        in_specs=[pl.BlockSpec((TOP_K, tm), lambda i: (0, i), memory_space=pltpu.SMEM),
                  pl.BlockSpec(memory_space=pl.ANY),
                  pl.BlockSpec(memory_space=pl.ANY)],
        out_specs=pl.BlockSpec(memory_space=pl.ANY),
        out_shape=jax.ShapeDtypeStruct((n_rows, d), F32),
        scratch_shapes=[pltpu.SemaphoreType.DMA],
        input_output_aliases={2: 0},
        compiler_params=pltpu.CompilerParams(dimension_semantics=("arbitrary",),
                                             has_side_effects=True),
        name="moe_dispatch",
    )(dest, u, xg0)


def _expert_body(blk_e_ref, blk_first_ref, x_ref, w1_ref, w3_ref, w2_ref, y_ref,
                 w1_scr, w3_scr, w2_scr):
    del blk_e_ref
    j = pl.program_id(0)

    @pl.when(blk_first_ref[j] == 1)
    def _():
        w1_scr[...] = w1_ref[0].astype(BF16)
        w3_scr[...] = w3_ref[0].astype(BF16)
        w2_scr[...] = w2_ref[0].astype(BF16)

    xb = x_ref[...].astype(BF16)
    hid = (_silu(jnp.dot(xb, w1_scr[...], preferred_element_type=F32))
           * jnp.dot(xb, w3_scr[...], preferred_element_type=F32))
    y_ref[...] = jnp.dot(hid.astype(BF16), w2_scr[...], preferred_element_type=F32)


def _experts(xg, blk_e, blk_first, w1, w3, w2):
    n_rows, d = xg.shape
    n_blocks = n_rows // EXPERT_ROWS
    de = w1.shape[-1]
    grid_spec = pltpu.PrefetchScalarGridSpec(
        num_scalar_prefetch=2,
        grid=(n_blocks,),
        in_specs=[pl.BlockSpec((EXPERT_ROWS, d), lambda j, be, bf: (j, 0)),
                  pl.BlockSpec((1, d, de), lambda j, be, bf: (be[j], 0, 0)),
                  pl.BlockSpec((1, d, de), lambda j, be, bf: (be[j], 0, 0)),
                  pl.BlockSpec((1, de, d), lambda j, be, bf: (be[j], 0, 0))],
        out_specs=pl.BlockSpec((EXPERT_ROWS, d), lambda j, be, bf: (j, 0)),
        scratch_shapes=[pltpu.VMEM((d, de), BF16), pltpu.VMEM((d, de), BF16),
                        pltpu.VMEM((de, d), BF16)],
    )
    return pl.pallas_call(
        _expert_body,
        grid_spec=grid_spec,
        out_shape=jax.ShapeDtypeStruct((n_rows, d), F32),
        compiler_params=_cparams(("arbitrary",)),
        name="moe_experts",
    )(blk_e, blk_first, xg, w1, w3, w2)


def _combine_body(dest_ref, h_ref, u_ref, wts_ref, gt_ref, s1_ref, s3_ref, s2_ref, y_hbm,
                  o_ref, buf, sem):
    tm = h_ref.shape[0]

    def row_copy(t, kk):
        return pltpu.make_async_copy(y_hbm.at[pl.ds(dest_ref[kk, t], 1)],
                                     buf.at[kk, pl.ds(t, 1)], sem)

    def issue(t, carry):
        for kk in range(TOP_K):
            row_copy(t, kk).start()
        return carry

    def drain(t, carry):
        for kk in range(TOP_K):
            row_copy(t, kk).wait()
        return carry

    lax.fori_loop(0, tm, issue, 0)

    ub = u_ref[...].astype(BF16)
    hid = (_silu(jnp.dot(ub, s1_ref[...], preferred_element_type=F32))
           * jnp.dot(ub, s3_ref[...], preferred_element_type=F32))
    shared = jnp.dot(hid.astype(BF16), s2_ref[...], preferred_element_type=F32)

    lax.fori_loop(0, tm, drain, 0)
    routed = buf[0] * wts_ref[:, 0:1]
    for kk in range(1, TOP_K):
        routed = routed + buf[kk] * wts_ref[:, kk:kk + 1]
    o_ref[...] = h_ref[...] + gt_ref[0] * (routed + shared)


def _combine(h, u, y, dest, wts_tk, gt, s1, s3, s2, seq):
    t, d = h.shape
    tm = min(COMBINE_ROWS, seq)
    per_b = seq // tm
    ds_ = s1.shape[-1]
    return pl.pallas_call(
        _combine_body,
        grid=(t // tm,),
        in_specs=[pl.BlockSpec((TOP_K, tm), lambda i: (0, i), memory_space=pltpu.SMEM),
                  pl.BlockSpec((tm, d), lambda i: (i, 0)),
                  pl.BlockSpec((tm, d), lambda i: (i, 0)),
                  pl.BlockSpec((tm, TOP_K), lambda i: (i, 0)),
                  pl.BlockSpec((1, 1, d), lambda i: (i // per_b, 0, 0)),
                  pl.BlockSpec((d, ds_), lambda i: (0, 0)),
                  pl.BlockSpec((d, ds_), lambda i: (0, 0)),
                  pl.BlockSpec((ds_, d), lambda i: (0, 0)),
                  pl.BlockSpec(memory_space=pl.ANY)],
        out_specs=pl.BlockSpec((tm, d), lambda i: (i, 0)),
        out_shape=jax.ShapeDtypeStruct((t, d), F32),
        scratch_shapes=[pltpu.VMEM((TOP_K, tm, d), F32), pltpu.SemaphoreType.DMA],
        compiler_params=_cparams(("arbitrary",)),
        name="moe_combine",
    )(dest, h, u, wts_tk, gt, s1, s3, s2, y)


def _final_norm_body(h_ref, w_ref, o_ref):
    x = h_ref[...]
    o_ref[...] = (x * lax.rsqrt(jnp.mean(x * x, axis=-1, keepdims=True) + EPS)) * w_ref[...]


def _final_norm(h, w):
    t, d = h.shape
    tm = min(NORM_ROWS, t)
    return pl.pallas_call(
        _final_norm_body,
        grid=(t // tm,),
        in_specs=[pl.BlockSpec((tm, d), lambda i: (i, 0)), pl.BlockSpec((1, d), lambda i: (0, 0))],
        out_specs=pl.BlockSpec((tm, d), lambda i: (i, 0)),
        out_shape=jax.ShapeDtypeStruct((t, d), F32),
        compiler_params=_cparams(("parallel",)),
        name="final_norm",
    )(h, w)


def _split_w_in(w_in):
    sizes = (2 * M_HEADS * M_DQK, M_HEADS * M_DV, M_HEADS * M_DV, M_HEADS, M_HEADS,
             G_HEADS * G_DK, G_HEADS * G_DK, G_HEADS * G_DV, G_RANK, G_HEADS * G_DV,
             D_MODEL, D_MODEL)
    parts, off = [], 0
    for n in sizes:
        parts.append(w_in[:, off:off + n])
        off += n
    qk_m, v_m, o_m, i_m, f_m, q_g, k_g, v_g, r_g, z_g, g_a, g_b = parts
    big = jnp.concatenate([qk_m, v_m, o_m, q_g, k_g, v_g, z_g, g_a, g_b], axis=1).astype(BF16)
    pad = jnp.zeros((w_in.shape[0], SMALL_COLS - 2 * M_HEADS - G_RANK), w_in.dtype)
    small = jnp.concatenate([i_m, f_m, r_g, pad], axis=1).astype(BF16)
    return big, small


def _moe_layout(counts, eidx, pos, n_blocks):
    padded = (counts + EXPERT_ROWS - 1) // EXPERT_ROWS * EXPERT_ROWS
    pend = jnp.cumsum(padded)
    pstart = pend - padded
    dest = pstart[eidx] + pos
    blk_start = jnp.arange(n_blocks, dtype=I32) * EXPERT_ROWS
    blk_e = jnp.minimum(jnp.searchsorted(pend, blk_start, side='right'), N_EXPERTS - 1).astype(I32)
    prev = jnp.concatenate([jnp.full((1,), -1, I32), blk_e[:-1]])
    blk_first = (blk_e != prev).astype(I32)
    return dest.astype(I32), blk_e, blk_first


def kernel(x, c, ada_w, ada_b, norm_mix_w, norm_moe_w, w_in, m_conv_w, m_gate_b, m_norm_w,
           g_alpha_w, g_alpha_b, g_norm_w, w_pa, w_pb, w_o, router_w, router_b,
           exp_w1, exp_w3, exp_w2, sh_w1, sh_w3, sh_w2, final_norm_w):
    bsz, seq, d = x.shape
    depth = ada_w.shape[0]
    t = bsz * seq
    n_rows = t * TOP_K + N_EXPERTS * EXPERT_ROWS
    n_blocks = n_rows // EXPERT_ROWS

    ada = _ada(c, ada_w, ada_b).reshape(depth, bsz, 6, 1, d)
    h = x.reshape(t, d)
    for l in range(depth):
        sh1, sc1, gt1, sh2, sc2, gt2 = (ada[l, :, i] for i in range(6))

        w_big, w_small = _split_w_in(w_in[l])
        big, small = _inproj(h, sh1, sc1, norm_mix_w[l][None, :], w_big, w_small, seq)
        gate_row = jnp.zeros((1, SMALL_COLS), F32)
        gate_row = gate_row.at[0, SMALL_I:SMALL_I + M_HEADS].set(m_gate_b[l, 0])
        gate_row = gate_row.at[0, SMALL_F:SMALL_F + M_HEADS].set(m_gate_b[l, 1])
        ha = _mlstm(big, small, m_conv_w[l], gate_row, m_norm_w[l][None, :], bsz, seq)
        alpha_full = jnp.zeros((SMALL_COLS, G_HEADS * G_DK), F32)
        alpha_full = alpha_full.at[SMALL_R:SMALL_R + G_RANK].set(g_alpha_w[l]).astype(BF16)
        hb = _gla(big, small, alpha_full, g_alpha_b[l][None, :], g_norm_w[l][None, :], bsz, seq)
        h = _mixout(h, ha, hb, big, gt1, w_pa[l].astype(BF16), w_pb[l].astype(BF16),
                    w_o[l].astype(BF16), seq)

        u, eidx, wts, pos, cnt = _route(h, sh2, sc2, norm_moe_w[l][None, :],
                                        router_w[l].T, router_b[l][:, None], seq)
        dest, blk_e, blk_first = _moe_layout(cnt[:, 0], eidx, pos, n_blocks)
        xg = _dispatch(u, dest, n_rows)
        y = _experts(xg, blk_e, blk_first, exp_w1[l], exp_w3[l], exp_w2[l])
        h = _combine(h, u, y, dest, wts.T, gt2, sh_w1[l].astype(BF16), sh_w3[l].astype(BF16),
                     sh_w2[l].astype(BF16), seq)

    return _final_norm(h, final_norm_w[None, :]).reshape(bsz, seq, d)
```

```python
import functools

import jax
import jax.numpy as jnp
from jax import lax
from jax.experimental import pallas as pl
from jax.experimental.pallas import tpu as pltpu

F32 = jnp.float32
BF16 = jnp.bfloat16
I32 = jnp.int32
HIGHEST = lax.Precision.HIGHEST

D_MODEL = 1024
M_HEADS = 4
M_DQK = 128
M_DV = 256
M_CONV = 4
GATE_CAP = 15.0
G_HEADS = 4
G_DK = 128
G_DV = 256
G_RANK = 16
G_TAU = 16.0
G_CHUNK = 64
N_EXPERTS = 64
TOP_K = 8
N_GROUPS = 8
GROUP_SIZE = N_EXPERTS // N_GROUPS
TOPK_GROUPS = 4
D_EXPERT = 256
D_SHARED = 256
ROUTED_SCALE = 2.5
EPS = 1e-6

M_CHUNK_ROWS = 256
G_BLOCK_ROWS = 256
PROJ_ROWS = 1024
MIX_ROWS = 512
ROUTE_ROWS = 512
DISPATCH_ROWS = 512
EXPERT_ROWS = 256
COMBINE_ROWS = 128
NORM_ROWS = 1024
CONV_HALO = 8
VMEM_LIMIT = 48 * 1024 * 1024

COL_QM, COL_KM, COL_VM, COL_OM = 0, 512, 1024, 2048
COL_QG, COL_KG, COL_VG, COL_ZG, COL_GA, COL_GB = 3072, 3584, 4096, 5120, 6144, 7168
BIG_COLS = 8192
SMALL_COLS = 128
SMALL_I, SMALL_F, SMALL_R = 0, M_HEADS, 2 * M_HEADS


def _cparams(sem, vmem=VMEM_LIMIT):
    return pltpu.CompilerParams(dimension_semantics=sem, vmem_limit_bytes=vmem)


def _silu(x):
    return x * jax.nn.sigmoid(x)


def _log_sigmoid(x):
    return jnp.minimum(x, 0.0) - jnp.log1p(jnp.exp(-jnp.abs(x)))


def _modulated_rmsnorm(x, w, sc, sh):
    y = x * lax.rsqrt(jnp.mean(x * x, axis=-1, keepdims=True) + EPS)
    return (y * w) * (1.0 + sc) + sh


def _lower_tri(n, dtype):
    r = lax.broadcasted_iota(I32, (n, n), 0)
    c = lax.broadcasted_iota(I32, (n, n), 1)
    return (r >= c).astype(dtype)


def _ada_body(c_ref, w_ref, b_ref, o_ref):
    cond = _silu(c_ref[...])
    o_ref[0] = jnp.dot(cond.astype(BF16), w_ref[0].astype(BF16),
                       preferred_element_type=F32) + b_ref[0]


def _ada(c, ada_w, ada_b):
    depth, d, six_d = ada_w.shape
    bsz = c.shape[0]
    nj = six_d // d
    return pl.pallas_call(
        _ada_body,
        grid=(depth, nj),
        in_specs=[pl.BlockSpec((bsz, d), lambda l, j: (0, 0)),
                  pl.BlockSpec((1, d, d), lambda l, j: (l, 0, j)),
                  pl.BlockSpec((1, 1, d), lambda l, j: (l, 0, j))],
        out_specs=pl.BlockSpec((1, bsz, d), lambda l, j: (l, 0, j)),
        out_shape=jax.ShapeDtypeStruct((depth, bsz, six_d), F32),
        compiler_params=_cparams(("parallel", "parallel")),
        name="ada_ln",
    )(c, ada_w, ada_b.reshape(depth, 1, six_d))


def _inproj_body(h_ref, sh_ref, sc_ref, nw_ref, wb_ref, ws_ref, big_ref, small_ref, u_scr):
    @pl.when(pl.program_id(1) == 0)
    def _():
        u = _modulated_rmsnorm(h_ref[...], nw_ref[...], sc_ref[0], sh_ref[0])
        ub = u.astype(BF16)
        u_scr[...] = ub
        small_ref[...] = jnp.dot(ub, ws_ref[...], preferred_element_type=F32)

    big_ref[...] = jnp.dot(u_scr[...], wb_ref[...], preferred_element_type=F32).astype(BF16)


def _inproj(h, sh, sc, nw, w_big, w_small, seq):
    t, d = h.shape
    tm = min(PROJ_ROWS, seq)
    per_b = seq // tm
    tn = 1024
    return pl.pallas_call(
        _inproj_body,
        grid=(t // tm, BIG_COLS // tn),
        in_specs=[pl.BlockSpec((tm, d), lambda i, j: (i, 0)),
                  pl.BlockSpec((1, 1, d), lambda i, j: (i // per_b, 0, 0)),
                  pl.BlockSpec((1, 1, d), lambda i, j: (i // per_b, 0, 0)),
                  pl.BlockSpec((1, d), lambda i, j: (0, 0)),
                  pl.BlockSpec((d, tn), lambda i, j: (0, j)),
                  pl.BlockSpec((d, SMALL_COLS), lambda i, j: (0, 0))],
        out_specs=[pl.BlockSpec((tm, tn), lambda i, j: (i, j)),
                   pl.BlockSpec((tm, SMALL_COLS), lambda i, j: (i, 0))],
        out_shape=[jax.ShapeDtypeStruct((t, BIG_COLS), BF16),
                   jax.ShapeDtypeStruct((t, SMALL_COLS), F32)],
        scratch_shapes=[pltpu.VMEM((tm, d), BF16)],
        compiler_params=_cparams(("parallel", "arbitrary")),
        name="in_proj",
    )(h, sh, sc, nw, w_big, w_small)


def _mlstm_body(q_ref, k_ref, v_ref, o_ref, small_ref, cw_ref, gb_ref, nw_ref, ha_ref,
                xe_scr, c_scr, n_scr, m_scr):
    rows = q_ref.shape[0]
    half = M_HEADS * M_DQK

    @pl.when(pl.program_id(1) == 0)
    def _():
        xe_scr[0:CONV_HALO, :] = jnp.zeros((CONV_HALO, 2 * half), F32)
        c_scr[...] = jnp.zeros(c_scr.shape, F32)
        n_scr[...] = jnp.zeros(n_scr.shape, F32)
        m_scr[...] = jnp.zeros(m_scr.shape, F32)

    xe_scr[CONV_HALO:CONV_HALO + rows, 0:half] = q_ref[...].astype(F32)
    xe_scr[CONV_HALO:CONV_HALO + rows, half:2 * half] = k_ref[...].astype(F32)
    conv = None
    for j in range(M_CONV):
        off = CONV_HALO - (M_CONV - 1) + j
        term = xe_scr[off:off + rows, :] * cw_ref[j:j + 1, :]
        conv = term if conv is None else conv + term
    qk = _silu(conv)
    xe_scr[0:CONV_HALO, :] = xe_scr[rows:rows + CONV_HALO, :]

    capped = GATE_CAP * jnp.tanh((small_ref[...] + gb_ref[...]) / GATE_CAP)
    li_all = capped
    lf_all = _log_sigmoid(capped)
    b_all = jnp.dot(_lower_tri(rows, F32), lf_all, precision=HIGHEST, preferred_element_type=F32)
    li_t = li_all.T
    b_t = b_all.T

    r_io = lax.broadcasted_iota(I32, (rows, rows), 0)
    c_io = lax.broadcasted_iota(I32, (rows, rows), 1)
    causal = r_io >= c_io

    for h in range(M_HEADS):
        q = (qk[:, h * M_DQK:(h + 1) * M_DQK] * (M_DQK ** -0.5)).astype(BF16)
        k = qk[:, half + h * M_DQK:half + (h + 1) * M_DQK]
        kb = k.astype(BF16)
        v = v_ref[:, h * M_DV:(h + 1) * M_DV]
        li_c = li_all[:, SMALL_I + h:SMALL_I + h + 1]
        b_c = b_all[:, SMALL_F + h:SMALL_F + h + 1]
        li_r = li_t[SMALL_I + h:SMALL_I + h + 1, :]
        b_r = b_t[SMALL_F + h:SMALL_F + h + 1, :]
        g = b_c[rows - 1:rows, :]
        m_prev = m_scr[h][:, 0:1]
        c_prev = c_scr[h]
        n_prev = n_scr[h]

        d_mat = jnp.where(causal, b_c - b_r + li_r, -jnp.inf)
        m_inter = b_c + m_prev
        m_t = jnp.maximum(jnp.max(d_mat, axis=1, keepdims=True), m_inter)
        s = lax.dot_general(q, kb, (((1,), (1,)), ((), ())), preferred_element_type=F32)
        p = jnp.exp(d_mat - m_t) * s
        w_inter = jnp.exp(m_inter - m_t)
        num = (jnp.dot(p.astype(BF16), v, preferred_element_type=F32)
               + w_inter * jnp.dot(q, c_prev.astype(BF16), preferred_element_type=F32))
        qn = jnp.sum(q.astype(F32) * n_prev, axis=1, keepdims=True)
        den = jnp.sum(p, axis=1, keepdims=True) + w_inter * qn
        hh = num / jnp.maximum(jnp.abs(den), jnp.exp(-m_t))

        a_r = g - b_r + li_r
        a_c = g - b_c + li_c
        m_new = jnp.maximum(g + m_prev, jnp.max(a_r, axis=1, keepdims=True))
        decay = jnp.exp(g + m_prev - m_new)
        wk = jnp.exp(a_c - m_new) * k
        c_scr[h] = decay * c_prev + lax.dot_general(
            wk.astype(BF16), v, (((0,), (0,)), ((), ())), preferred_element_type=F32)
        n_scr[h] = decay * n_prev + jnp.sum(wk, axis=0, keepdims=True)
        m_scr[h] = jnp.broadcast_to(m_new, m_scr.shape[1:])

        y = hh * lax.rsqrt(jnp.mean(hh * hh, axis=-1, keepdims=True) + EPS)
        y = y * nw_ref[:, h * M_DV:(h + 1) * M_DV]
        gate = jax.nn.sigmoid(o_ref[:, h * M_DV:(h + 1) * M_DV].astype(F32))
        ha_ref[:, h * M_DV:(h + 1) * M_DV] = (y * gate).astype(BF16)


def _mlstm(big, small, conv_w, gate_row, norm_w, bsz, seq):
    t = big.shape[0]
    rows = min(M_CHUNK_ROWS, seq)
    nc = seq // rows
    half = M_HEADS * M_DQK
    vcols = M_HEADS * M_DV
    row_map = lambda b, c: b * nc + c
    return pl.pallas_call(
        _mlstm_body,
        grid=(bsz, nc),
        in_specs=[pl.BlockSpec((rows, half), lambda b, c: (row_map(b, c), COL_QM // half)),
                  pl.BlockSpec((rows, half), lambda b, c: (row_map(b, c), COL_KM // half)),
                  pl.BlockSpec((rows, vcols), lambda b, c: (row_map(b, c), COL_VM // vcols)),
                  pl.BlockSpec((rows, vcols), lambda b, c: (row_map(b, c), COL_OM // vcols)),
                  pl.BlockSpec((rows, SMALL_COLS), lambda b, c: (row_map(b, c), 0)),
                  pl.BlockSpec((M_CONV, 2 * half), lambda b, c: (0, 0)),
                  pl.BlockSpec((1, SMALL_COLS), lambda b, c: (0, 0)),
                  pl.BlockSpec((1, vcols), lambda b, c: (0, 0))],
        out_specs=pl.BlockSpec((rows, vcols), lambda b, c: (row_map(b, c), 0)),
        out_shape=jax.ShapeDtypeStruct((t, vcols), BF16),
        scratch_shapes=[pltpu.VMEM((rows + CONV_HALO, 2 * half), F32),
                        pltpu.VMEM((M_HEADS, M_DQK, M_DV), F32),
                        pltpu.VMEM((M_HEADS, 1, M_DQK), F32),
                        pltpu.VMEM((M_HEADS, 1, 128), F32)],
        compiler_params=_cparams(("parallel", "arbitrary")),
        name="mlstm",
    )(big, big, big, big, small, conv_w, gate_row, norm_w)


def _gla_body(q_ref, k_ref, v_ref, z_ref, small_ref, aw_ref, ab_ref, nw_ref, hb_ref,
              la_scr, s_scr):
    rows = q_ref.shape[0]
    n_chunks = rows // G_CHUNK

    @pl.when(pl.program_id(1) == 0)
    def _():
        s_scr[...] = jnp.zeros(s_scr.shape, F32)

    logits = jnp.dot(small_ref[...].astype(BF16), aw_ref[...], preferred_element_type=F32)
    la_scr[...] = _log_sigmoid(logits + ab_ref[...]) / G_TAU

    tri = _lower_tri(G_CHUNK, F32)
    r_io = lax.broadcasted_iota(I32, (G_CHUNK, G_CHUNK), 0)
    c_io = lax.broadcasted_iota(I32, (G_CHUNK, G_CHUNK), 1)
    causal = r_io >= c_io

    def chunk(ci, carry):
        r0 = pl.multiple_of(ci * G_CHUNK, G_CHUNK)
        rs = pl.ds(r0, G_CHUNK)
        bc = jnp.dot(tri, la_scr[rs, :], precision=HIGHEST, preferred_element_type=F32)
        gc = bc[G_CHUNK - 1:G_CHUNK, :]
        for h in range(G_HEADS):
            ks = slice(h * G_DK, (h + 1) * G_DK)
            vs = slice(h * G_DV, (h + 1) * G_DV)
            q = q_ref[rs, ks].astype(F32) * (G_DK ** -0.5)
            k = k_ref[rs, ks].astype(F32)
            v = v_ref[rs, vs]
            bch = bc[:, ks]
            gch = gc[:, ks]
            q_in = (q * jnp.exp(bch)).astype(BF16)
            k_in = (k * jnp.exp(-bch)).astype(BF16)
            k_out = (k * jnp.exp(gch - bch)).astype(BF16)
            s_prev = s_scr[h]
            att = lax.dot_general(q_in, k_in, (((1,), (1,)), ((), ())), preferred_element_type=F32)
            att = jnp.where(causal, att, 0.0)
            o = (jnp.dot(att.astype(BF16), v, preferred_element_type=F32)
                 + jnp.dot(q_in, s_prev.astype(BF16), preferred_element_type=F32))
            eg_col = jnp.broadcast_to(jnp.exp(gch), (G_DK, G_DK)).T[:, 0:1]
            s_scr[h] = eg_col * s_prev + lax.dot_general(
                k_out, v, (((0,), (0,)), ((), ())), preferred_element_type=F32)
            y = o * lax.rsqrt(jnp.mean(o * o, axis=-1, keepdims=True) + EPS)
            y = y * nw_ref[:, vs]
            hb_ref[rs, vs] = (y * _silu(z_ref[rs, vs].astype(F32))).astype(BF16)
        return carry

    lax.fori_loop(0, n_chunks, chunk, 0)


def _gla(big, small, alpha_full, alpha_b, norm_w, bsz, seq):
    t = big.shape[0]
    rows = min(G_BLOCK_ROWS, seq)
    nb = seq // rows
    kcols = G_HEADS * G_DK
    vcols = G_HEADS * G_DV
    row_map = lambda b, c: b * nb + c
    return pl.pallas_call(
        _gla_body,
        grid=(bsz, nb),
        in_specs=[pl.BlockSpec((rows, kcols), lambda b, c: (row_map(b, c), COL_QG // kcols)),
                  pl.BlockSpec((rows, kcols), lambda b, c: (row_map(b, c), COL_KG // kcols)),
                  pl.BlockSpec((rows, vcols), lambda b, c: (row_map(b, c), COL_VG // vcols)),
                  pl.BlockSpec((rows, vcols), lambda b, c: (row_map(b, c), COL_ZG // vcols)),
                  pl.BlockSpec((rows, SMALL_COLS), lambda b, c: (row_map(b, c), 0)),
                  pl.BlockSpec((SMALL_COLS, kcols), lambda b, c: (0, 0)),
                  pl.BlockSpec((1, kcols), lambda b, c: (0, 0)),
                  pl.BlockSpec((1, vcols), lambda b, c: (0, 0))],
        out_specs=pl.BlockSpec((rows, vcols), lambda b, c: (row_map(b, c), 0)),
        out_shape=jax.ShapeDtypeStruct((t, vcols), BF16),
        scratch_shapes=[pltpu.VMEM((rows, kcols), F32),
                        pltpu.VMEM((G_HEADS, G_DK, G_DV), F32)],
        compiler_params=_cparams(("parallel", "arbitrary")),
        name="gla",
    )(big, big, big, big, small, alpha_full, alpha_b, norm_w)


def _mixout_body(h_ref, ha_ref, hb_ref, ga_ref, gb_ref, gt_ref, wpa_ref, wpb_ref, wo_ref, o_ref):
    a = jnp.dot(ha_ref[...], wpa_ref[...], preferred_element_type=F32)
    b = jnp.dot(hb_ref[...], wpb_ref[...], preferred_element_type=F32)
    y = (jax.nn.sigmoid(ga_ref[...].astype(F32)) * a
         + jax.nn.sigmoid(gb_ref[...].astype(F32)) * b)
    o_ref[...] = h_ref[...] + gt_ref[0] * jnp.dot(y.astype(BF16), wo_ref[...],
                                                  preferred_element_type=F32)


def _mixout(h, ha, hb, big, gt, w_pa, w_pb, w_o, seq):
    t, d = h.shape
    tm = min(MIX_ROWS, seq)
    per_b = seq // tm
    wspec = pl.BlockSpec((d, d), lambda i: (0, 0))
    return pl.pallas_call(
        _mixout_body,
        grid=(t // tm,),
        in_specs=[pl.BlockSpec((tm, d), lambda i: (i, 0)),
                  pl.BlockSpec((tm, d), lambda i: (i, 0)),
                  pl.BlockSpec((tm, d), lambda i: (i, 0)),
                  pl.BlockSpec((tm, d), lambda i: (i, COL_GA // d)),
                  pl.BlockSpec((tm, d), lambda i: (i, COL_GB // d)),
                  pl.BlockSpec((1, 1, d), lambda i: (i // per_b, 0, 0)),
                  wspec, wspec, wspec],
        out_specs=pl.BlockSpec((tm, d), lambda i: (i, 0)),
        out_shape=jax.ShapeDtypeStruct((t, d), F32),
        compiler_params=_cparams(("parallel",)),
        name="mix_out",
    )(h, ha, hb, big, big, gt, w_pa, w_pb, w_o)


def _route_body(h_ref, sh_ref, sc_ref, nw_ref, rwt_ref, rb_ref,
                u_ref, eidx_ref, wts_ref, pos_ref, cnt_ref, carry_scr):
    tm = h_ref.shape[0]

    @pl.when(pl.program_id(0) == 0)
    def _():
        carry_scr[...] = jnp.zeros(carry_scr.shape, F32)

    u = _modulated_rmsnorm(h_ref[...], nw_ref[...], sc_ref[0], sh_ref[0])
    u_ref[...] = u
    logits = lax.dot_general(rwt_ref[...], u, (((1,), (1,)), ((), ())),
                             precision=HIGHEST, preferred_element_type=F32)
    scores = jax.nn.sigmoid(logits)
    sel = scores + rb_ref[...]

    neg = -jnp.inf
    sub_io = lax.broadcasted_iota(I32, (GROUP_SIZE, tm), 0)
    pieces = []
    for g in range(N_GROUPS):
        blk = sel[g * GROUP_SIZE:(g + 1) * GROUP_SIZE, :]
        m1 = jnp.max(blk, axis=0, keepdims=True)
        first = jnp.min(jnp.where(blk == m1, sub_io, GROUP_SIZE), axis=0, keepdims=True)
        m2 = jnp.max(jnp.where(sub_io == first, neg, blk), axis=0, keepdims=True)
        pieces.append(jnp.broadcast_to(m1 + m2, (GROUP_SIZE, tm)))
    gscore = jnp.concatenate(pieces, axis=0)

    e_io = lax.broadcasted_iota(I32, (N_EXPERTS, tm), 0)
    grp_io = e_io // GROUP_SIZE
    gmask = jnp.zeros((N_EXPERTS, tm), jnp.bool_)
    for _ in range(TOPK_GROUPS):
        mx = jnp.max(gscore, axis=0, keepdims=True)
        gi = jnp.min(jnp.where(gscore == mx, grp_io, N_GROUPS), axis=0, keepdims=True)
        hit = grp_io == gi
        gmask = jnp.logical_or(gmask, hit)
        gscore = jnp.where(hit, neg, gscore)

    cur = jnp.where(gmask, sel, neg)
    row_io = lax.broadcasted_iota(I32, (TOP_K, tm), 0)
    eidx = jnp.zeros((TOP_K, tm), I32)
    wraw = jnp.zeros((TOP_K, tm), F32)
    chosen = jnp.zeros((N_EXPERTS, tm), jnp.bool_)
    hits = []
    for kk in range(TOP_K):
        mx = jnp.max(cur, axis=0, keepdims=True)
        ei = jnp.min(jnp.where(cur == mx, e_io, N_EXPERTS), axis=0, keepdims=True)
        hit = e_io == ei
        hits.append(hit)
        sc_k = jnp.sum(jnp.where(hit, scores, 0.0), axis=0, keepdims=True)
        eidx = jnp.where(row_io == kk, ei, eidx)
        wraw = jnp.where(row_io == kk, sc_k, wraw)
        chosen = jnp.logical_or(chosen, hit)
        cur = jnp.where(hit, neg, cur)

    wsum = jnp.sum(wraw, axis=0, keepdims=True)
    wts_ref[...] = wraw / wsum * ROUTED_SCALE
    eidx_ref[...] = eidx

    chosen_f = jnp.where(chosen, 1.0, 0.0)
    r_io = lax.broadcasted_iota(I32, (tm, tm), 0)
    c_io = lax.broadcasted_iota(I32, (tm, tm), 1)
    strict_upper = jnp.where(r_io < c_io, 1.0, 0.0).astype(BF16)
    prefix = jnp.dot(chosen_f.astype(BF16), strict_upper, preferred_element_type=F32)
    rank = prefix + carry_scr[:, 0:1]
    pos = jnp.zeros((TOP_K, tm), F32)
    for kk in range(TOP_K):
        p_k = jnp.sum(jnp.where(hits[kk], rank, 0.0), axis=0, keepdims=True)
        pos = jnp.where(row_io == kk, p_k, pos)
    pos_ref[...] = pos.astype(I32)
    total = carry_scr[...] + jnp.sum(chosen_f, axis=1, keepdims=True)
    carry_scr[...] = total
    cnt_ref[...] = total.astype(I32)


def _route(h, sh, sc, nw, rw_t, rb_col, seq):
    t, d = h.shape
    tm = min(ROUTE_ROWS, seq)
    per_b = seq // tm
    kspec = pl.BlockSpec((TOP_K, tm), lambda i: (0, i))
    return pl.pallas_call(
        _route_body,
        grid=(t // tm,),
        in_specs=[pl.BlockSpec((tm, d), lambda i: (i, 0)),
                  pl.BlockSpec((1, 1, d), lambda i: (i // per_b, 0, 0)),
                  pl.BlockSpec((1, 1, d), lambda i: (i // per_b, 0, 0)),
                  pl.BlockSpec((1, d), lambda i: (0, 0)),
                  pl.BlockSpec((N_EXPERTS, d), lambda i: (0, 0)),
                  pl.BlockSpec((N_EXPERTS, 1), lambda i: (0, 0))],
        out_specs=[pl.BlockSpec((tm, d), lambda i: (i, 0)), kspec, kspec, kspec,
                   pl.BlockSpec((N_EXPERTS, 128), lambda i: (0, 0))],
        out_shape=[jax.ShapeDtypeStruct((t, d), F32),
                   jax.ShapeDtypeStruct((TOP_K, t), I32),
                   jax.ShapeDtypeStruct((TOP_K, t), F32),
                   jax.ShapeDtypeStruct((TOP_K, t), I32),
                   jax.ShapeDtypeStruct((N_EXPERTS, 128), I32)],
        scratch_shapes=[pltpu.VMEM((N_EXPERTS, 128), F32)],
        compiler_params=_cparams(("arbitrary",)),
        name="moe_route",
    )(h, sh, sc, nw, rw_t, rb_col)


def _dispatch_body(pstart_ref, eidx_ref, pos_ref, u_ref, xg_hbm, sem):
    tm = u_ref.shape[0]

    def row_copy(t, kk):
        dest = pstart_ref[eidx_ref[kk, t]] + pos_ref[kk, t]
        return pltpu.make_async_copy(u_ref.at[pl.ds(t, 1)], xg_hbm.at[pl.ds(dest, 1)], sem)

    def issue(t, carry):
        for kk in range(TOP_K):
            row_copy(t, kk).start()
        return carry

    def drain(t, carry):
        for kk in range(TOP_K):
            row_copy(t, kk).wait()
        return carry

    lax.fori_loop(0, tm, issue, 0)
    lax.fori_loop(0, tm, drain, 0)


def _dispatch(u, pstart, eidx, pos, n_rows):
    t, d = u.shape
    tm = min(DISPATCH_ROWS, t)
    kspec = pl.BlockSpec((TOP_K, tm), lambda i, ps: (0, i), memory_space=pltpu.SMEM)
    grid_spec = pltpu.PrefetchScalarGridSpec(
        num_scalar_prefetch=1,
        grid=(t // tm,),
        in_specs=[kspec, kspec, pl.BlockSpec((tm, d), lambda i, ps: (i, 0))],
        out_specs=pl.BlockSpec(memory_space=pl.ANY),
        scratch_shapes=[pltpu.SemaphoreType.DMA],
    )
    return pl.pallas_call(
        _dispatch_body,
        grid_spec=grid_spec,
        out_shape=jax.ShapeDtypeStruct((n_rows, d), F32),
        compiler_params=pltpu.CompilerParams(dimension_semantics=("arbitrary",),
                                             has_side_effects=True),
        name="moe_dispatch",
    )(pstart, eidx, pos, u)


def _expert_body(blk_e_ref, blk_first_ref, blk_valid_ref, x_ref, w1_ref, w3_ref, w2_ref, y_ref,
                 w1_scr, w3_scr, w2_scr):
    del blk_e_ref
    j = pl.program_id(0)
    valid = blk_valid_ref[j]

    @pl.when(blk_first_ref[j] == 1)
    def _():
        w1_scr[...] = w1_ref[0].astype(BF16)
        w3_scr[...] = w3_ref[0].astype(BF16)
        w2_scr[...] = w2_ref[0].astype(BF16)

    @pl.when(valid > 0)
    def _():
        rows = lax.broadcasted_iota(I32, x_ref.shape, 0)
        xb = jnp.where(rows < valid, x_ref[...], 0.0).astype(BF16)
        hid = (_silu(jnp.dot(xb, w1_scr[...], preferred_element_type=F32))
               * jnp.dot(xb, w3_scr[...], preferred_element_type=F32))
        y_ref[...] = jnp.dot(hid.astype(BF16), w2_scr[...], preferred_element_type=F32)

    @pl.when(valid == 0)
    def _():
        y_ref[...] = jnp.zeros(y_ref.shape, F32)


def _experts(xg, blk_e, blk_first, blk_valid, w1, w3, w2):
    n_rows, d = xg.shape
    n_blocks = n_rows // EXPERT_ROWS
    de = w1.shape[-1]
    grid_spec = pltpu.PrefetchScalarGridSpec(
        num_scalar_prefetch=3,
        grid=(n_blocks,),
        in_specs=[pl.BlockSpec((EXPERT_ROWS, d), lambda j, be, bf, bv: (j, 0)),
                  pl.BlockSpec((1, d, de), lambda j, be, bf, bv: (be[j], 0, 0)),
                  pl.BlockSpec((1, d, de), lambda j, be, bf, bv: (be[j], 0, 0)),
                  pl.BlockSpec((1, de, d), lambda j, be, bf, bv: (be[j], 0, 0))],
        out_specs=pl.BlockSpec((EXPERT_ROWS, d), lambda j, be, bf, bv: (j, 0)),
        scratch_shapes=[pltpu.VMEM((d, de), BF16), pltpu.VMEM((d, de), BF16),
                        pltpu.VMEM((de, d), BF16)],
    )
    return pl.pallas_call(
        _expert_body,
        grid_spec=grid_spec,
        out_shape=jax.ShapeDtypeStruct((n_rows, d), F32),
        compiler_params=_cparams(("arbitrary",)),
        name="moe_experts",
    )(blk_e, blk_first, blk_valid, xg, w1, w3, w2)


def _combine_body(pstart_ref, eidx_ref, pos_ref, h_ref, u_ref, wts_ref, gt_ref, s1_ref, s3_ref,
                  s2_ref, y_hbm, o_ref, buf, sem):
    tm = h_ref.shape[0]

    def row_copy(t, kk):
        dest = pstart_ref[eidx_ref[kk, t]] + pos_ref[kk, t]
        return pltpu.make_async_copy(y_hbm.at[pl.ds(dest, 1)], buf.at[kk, pl.ds(t, 1)], sem)

    def issue(t, carry):
        for kk in range(TOP_K):
            row_copy(t, kk).start()
        return carry

    def drain(t, carry):
        for kk in range(TOP_K):
            row_copy(t, kk).wait()
        return carry

    lax.fori_loop(0, tm, issue, 0)

    ub = u_ref[...].astype(BF16)
    hid = (_silu(jnp.dot(ub, s1_ref[...], preferred_element_type=F32))
           * jnp.dot(ub, s3_ref[...], preferred_element_type=F32))
    shared = jnp.dot(hid.astype(BF16), s2_ref[...], preferred_element_type=F32)

    lax.fori_loop(0, tm, drain, 0)
    routed = buf[0] * wts_ref[:, 0:1]
    for kk in range(1, TOP_K):
        routed = routed + buf[kk] * wts_ref[:, kk:kk + 1]
    o_ref[...] = h_ref[...] + gt_ref[0] * (routed + shared)


def _combine(h, u, y, pstart, eidx, pos, wts_tk, gt, s1, s3, s2, seq):
    t, d = h.shape
    tm = min(COMBINE_ROWS, seq)
    per_b = seq // tm
    ds_ = s1.shape[-1]
    kspec = pl.BlockSpec((TOP_K, tm), lambda i, ps: (0, i), memory_space=pltpu.SMEM)
    grid_spec = pltpu.PrefetchScalarGridSpec(
        num_scalar_prefetch=1,
        grid=(t // tm,),
        in_specs=[kspec, kspec,
                  pl.BlockSpec((tm, d), lambda i, ps: (i, 0)),
                  pl.BlockSpec((tm, d), lambda i, ps: (i, 0)),
                  pl.BlockSpec((tm, TOP_K), lambda i, ps: (i, 0)),
                  pl.BlockSpec((1, 1, d), lambda i, ps: (i // per_b, 0, 0)),
                  pl.BlockSpec((d, ds_), lambda i, ps: (0, 0)),
                  pl.BlockSpec((d, ds_), lambda i, ps: (0, 0)),
                  pl.BlockSpec((ds_, d), lambda i, ps: (0, 0)),
                  pl.BlockSpec(memory_space=pl.ANY)],
        out_specs=pl.BlockSpec((tm, d), lambda i, ps: (i, 0)),
        scratch_shapes=[pltpu.VMEM((TOP_K, tm, d), F32), pltpu.SemaphoreType.DMA],
    )
    return pl.pallas_call(
        _combine_body,
        grid_spec=grid_spec,
        out_shape=jax.ShapeDtypeStruct((t, d), F32),
        compiler_params=_cparams(("arbitrary",)),
        name="moe_combine",
    )(pstart, eidx, pos, h, u, wts_tk, gt, s1, s3, s2, y)


def _final_norm_body(h_ref, w_ref, o_ref):
    x = h_ref[...]
    o_ref[...] = (x * lax.rsqrt(jnp.mean(x * x, axis=-1, keepdims=True) + EPS)) * w_ref[...]


def _final_norm(h, w):
    t, d = h.shape
    tm = min(NORM_ROWS, t)
    return pl.pallas_call(
        _final_norm_body,
        grid=(t // tm,),
        in_specs=[pl.BlockSpec((tm, d), lambda i: (i, 0)), pl.BlockSpec((1, d), lambda i: (0, 0))],
        out_specs=pl.BlockSpec((tm, d), lambda i: (i, 0)),
        out_shape=jax.ShapeDtypeStruct((t, d), F32),
        compiler_params=_cparams(("parallel",)),
        name="final_norm",
    )(h, w)


def _split_w_in(w_in):
    sizes = (2 * M_HEADS * M_DQK, M_HEADS * M_DV, M_HEADS * M_DV, M_HEADS, M_HEADS,
             G_HEADS * G_DK, G_HEADS * G_DK, G_HEADS * G_DV, G_RANK, G_HEADS * G_DV,
             D_MODEL, D_MODEL)
    parts, off = [], 0
    for n in sizes:
        parts.append(w_in[:, off:off + n])
        off += n
    qk_m, v_m, o_m, i_m, f_m, q_g, k_g, v_g, r_g, z_g, g_a, g_b = parts
    big = jnp.concatenate([qk_m, v_m, o_m, q_g, k_g, v_g, z_g, g_a, g_b], axis=1).astype(BF16)
    pad = jnp.zeros((w_in.shape[0], SMALL_COLS - 2 * M_HEADS - G_RANK), w_in.dtype)
    small = jnp.concatenate([i_m, f_m, r_g, pad], axis=1).astype(BF16)
    return big, small


def _moe_layout(counts, n_blocks):
    padded = (counts + EXPERT_ROWS - 1) // EXPERT_ROWS * EXPERT_ROWS
    pend = jnp.cumsum(padded)
    pstart = pend - padded
    blk_start = jnp.arange(n_blocks, dtype=I32) * EXPERT_ROWS
    owner = jnp.sum((pend[None, :] <= blk_start[:, None]).astype(I32), axis=1)
    blk_e = jnp.minimum(owner, N_EXPERTS - 1)
    prev = jnp.concatenate([jnp.full((1,), -1, I32), blk_e[:-1]])
    blk_first = (blk_e != prev).astype(I32)
    used = jnp.where(owner < N_EXPERTS, counts[blk_e] - (blk_start - pstart[blk_e]), 0)
    blk_valid = jnp.clip(used, 0, EXPERT_ROWS)
    return pstart.astype(I32), blk_e.astype(I32), blk_first, blk_valid.astype(I32)


def kernel(x, c, ada_w, ada_b, norm_mix_w, norm_moe_w, w_in, m_conv_w, m_gate_b, m_norm_w,
           g_alpha_w, g_alpha_b, g_norm_w, w_pa, w_pb, w_o, router_w, router_b,
           exp_w1, exp_w3, exp_w2, sh_w1, sh_w3, sh_w2, final_norm_w):
    bsz, seq, d = x.shape
    depth = ada_w.shape[0]
    t = bsz * seq
    n_rows = t * TOP_K + N_EXPERTS * EXPERT_ROWS
    n_blocks = n_rows // EXPERT_ROWS

    ada = _ada(c, ada_w, ada_b).reshape(depth, bsz, 6, 1, d)
    h = x.reshape(t, d)
    for l in range(depth):
        sh1, sc1, gt1, sh2, sc2, gt2 = (ada[l, :, i] for i in range(6))

        w_big, w_small = _split_w_in(w_in[l])
        big, small = _inproj(h, sh1, sc1, norm_mix_w[l][None, :], w_big, w_small, seq)
        gate_row = jnp.zeros((1, SMALL_COLS), F32)
        gate_row = gate_row.at[0, SMALL_I:SMALL_I + M_HEADS].set(m_gate_b[l, 0])
        gate_row = gate_row.at[0, SMALL_F:SMALL_F + M_HEADS].set(m_gate_b[l, 1])
        ha = _mlstm(big, small, m_conv_w[l], gate_row, m_norm_w[l][None, :], bsz, seq)
        alpha_full = jnp.zeros((SMALL_COLS, G_HEADS * G_DK), F32)
        alpha_full = alpha_full.at[SMALL_R:SMALL_R + G_RANK].set(g_alpha_w[l]).astype(BF16)
        hb = _gla(big, small, alpha_full, g_alpha_b[l][None, :], g_norm_w[l][None, :], bsz, seq)
        h = _mixout(h, ha, hb, big, gt1, w_pa[l].astype(BF16), w_pb[l].astype(BF16),
                    w_o[l].astype(BF16), seq)

        u, eidx, wts, pos, cnt = _route(h, sh2, sc2, norm_moe_w[l][None, :],
                                        router_w[l].T, router_b[l][:, None], seq)
        pstart, blk_e, blk_first, blk_valid = _moe_layout(cnt[:, 0], n_blocks)
        xg = _dispatch(u, pstart, eidx, pos, n_rows)
        y = _experts(xg, blk_e, blk_first, blk_valid, exp_w1[l], exp_w3[l], exp_w2[l])
        h = _combine(h, u, y, pstart, eidx, pos, wts.T, gt2, sh_w1[l].astype(BF16),
                     sh_w3[l].astype(BF16), sh_w2[l].astype(BF16), seq)

    return _final_norm(h, final_norm_w[None, :]).reshape(bsz, seq, d)
```

```python
import functools

import jax
import jax.numpy as jnp
import numpy as np
from jax import lax
from jax.experimental import pallas as pl
from jax.experimental.pallas import tpu as pltpu
from jax.experimental.pallas import tpu_sc as plsc

F32 = jnp.float32
BF16 = jnp.bfloat16
I32 = jnp.int32
U32 = jnp.uint32
HI_MASK = np.uint32(0xFFFF0000)
HIGHEST = lax.Precision.HIGHEST

SC_CORES = 2
SC_SUBCORES = 16
SC_WORKERS = SC_CORES * SC_SUBCORES
SC_CHUNK = 64

D_MODEL = 1024
M_HEADS = 4
M_DQK = 128
M_DV = 256
M_CONV = 4
GATE_CAP = 15.0
G_HEADS = 4
G_DK = 128
G_DV = 256
G_RANK = 16
G_TAU = 16.0
G_CHUNK = 64
N_EXPERTS = 64
TOP_K = 8
N_GROUPS = 8
GROUP_SIZE = N_EXPERTS // N_GROUPS
TOPK_GROUPS = 4
D_EXPERT = 256
D_SHARED = 256
ROUTED_SCALE = 2.5
EPS = 1e-6

M_CHUNK_ROWS = 256
G_BLOCK_ROWS = 256
PROJ_ROWS = 1024
MIX_ROWS = 512
ROUTE_ROWS = 512
DISPATCH_ROWS = 512
EXPERT_ROWS = 256
COMBINE_ROWS = 512
NORM_ROWS = 1024
CONV_HALO = 8
VMEM_LIMIT = 48 * 1024 * 1024

COL_QM, COL_KM, COL_VM, COL_OM = 0, 512, 1024, 2048
COL_QG, COL_KG, COL_VG, COL_ZG, COL_GA, COL_GB = 3072, 3584, 4096, 5120, 6144, 7168
BIG_COLS = 8192
SMALL_COLS = 128
SMALL_I, SMALL_F, SMALL_R = 0, M_HEADS, 2 * M_HEADS


def _cparams(sem, vmem=VMEM_LIMIT):
    return pltpu.CompilerParams(dimension_semantics=sem, vmem_limit_bytes=vmem)


def _silu(x):
    return x * jax.nn.sigmoid(x)


def _log_sigmoid(x):
    return jnp.minimum(x, 0.0) - jnp.log1p(jnp.exp(-jnp.abs(x)))


def _modulated_rmsnorm(x, w, sc, sh):
    y = x * lax.rsqrt(jnp.mean(x * x, axis=-1, keepdims=True) + EPS)
    return (y * w) * (1.0 + sc) + sh


def _pack_bf16_pair(lo, hi):
    lo_bits = lax.bitcast_convert_type(lo.astype(BF16).astype(F32), U32)
    hi_bits = lax.bitcast_convert_type(hi.astype(BF16).astype(F32), U32)
    return (lo_bits >> 16) | (hi_bits & HI_MASK)


def _unpack_bf16_pair(packed):
    lo = lax.bitcast_convert_type(packed << 16, F32)
    hi = lax.bitcast_convert_type(packed & HI_MASK, F32)
    return lo, hi


def _lower_tri(n, dtype):
    r = lax.broadcasted_iota(I32, (n, n), 0)
    c = lax.broadcasted_iota(I32, (n, n), 1)
    return (r >= c).astype(dtype)


def _ada_body(c_ref, w_ref, b_ref, o_ref):
    cond = _silu(c_ref[...])
    o_ref[0] = jnp.dot(cond.astype(BF16), w_ref[0].astype(BF16),
                       preferred_element_type=F32) + b_ref[0]


def _ada(c, ada_w, ada_b):
    depth, d, six_d = ada_w.shape
    bsz = c.shape[0]
    nj = six_d // d
    return pl.pallas_call(
        _ada_body,
        grid=(depth, nj),
        in_specs=[pl.BlockSpec((bsz, d), lambda l, j: (0, 0)),
                  pl.BlockSpec((1, d, d), lambda l, j: (l, 0, j)),
                  pl.BlockSpec((1, 1, d), lambda l, j: (l, 0, j))],
        out_specs=pl.BlockSpec((1, bsz, d), lambda l, j: (l, 0, j)),
        out_shape=jax.ShapeDtypeStruct((depth, bsz, six_d), F32),
        compiler_params=_cparams(("parallel", "parallel")),
        name="ada_ln",
    )(c, ada_w, ada_b.reshape(depth, 1, six_d))


def _inproj_body(h_ref, sh_ref, sc_ref, nw_ref, wb_ref, ws_ref, big_ref, small_ref, u_scr):
    @pl.when(pl.program_id(1) == 0)
    def _():
        u = _modulated_rmsnorm(h_ref[...], nw_ref[...], sc_ref[0], sh_ref[0])
        ub = u.astype(BF16)
        u_scr[...] = ub
        small_ref[...] = jnp.dot(ub, ws_ref[...], preferred_element_type=F32)

    big_ref[...] = jnp.dot(u_scr[...], wb_ref[...], preferred_element_type=F32).astype(BF16)


def _inproj(h, sh, sc, nw, w_big, w_small, seq):
    t, d = h.shape
    tm = min(PROJ_ROWS, seq)
    per_b = seq // tm
    tn = 1024
    return pl.pallas_call(
        _inproj_body,
        grid=(t // tm, BIG_COLS // tn),
        in_specs=[pl.BlockSpec((tm, d), lambda i, j: (i, 0)),
                  pl.BlockSpec((1, 1, d), lambda i, j: (i // per_b, 0, 0)),
                  pl.BlockSpec((1, 1, d), lambda i, j: (i // per_b, 0, 0)),
                  pl.BlockSpec((1, d), lambda i, j: (0, 0)),
                  pl.BlockSpec((d, tn), lambda i, j: (0, j)),
                  pl.BlockSpec((d, SMALL_COLS), lambda i, j: (0, 0))],
        out_specs=[pl.BlockSpec((tm, tn), lambda i, j: (i, j)),
                   pl.BlockSpec((tm, SMALL_COLS), lambda i, j: (i, 0))],
        out_shape=[jax.ShapeDtypeStruct((t, BIG_COLS), BF16),
                   jax.ShapeDtypeStruct((t, SMALL_COLS), F32)],
        scratch_shapes=[pltpu.VMEM((tm, d), BF16)],
        compiler_params=_cparams(("parallel", "arbitrary")),
        name="in_proj",
    )(h, sh, sc, nw, w_big, w_small)


def _mlstm_body(q_ref, k_ref, v_ref, o_ref, small_ref, cw_ref, gb_ref, nw_ref, ha_ref,
                xe_scr, c_scr, n_scr, m_scr):
    rows = q_ref.shape[0]
    half = M_HEADS * M_DQK

    @pl.when(pl.program_id(1) == 0)
    def _():
        xe_scr[0:CONV_HALO, :] = jnp.zeros((CONV_HALO, 2 * half), F32)
        c_scr[...] = jnp.zeros(c_scr.shape, F32)
        n_scr[...] = jnp.zeros(n_scr.shape, F32)
        m_scr[...] = jnp.zeros(m_scr.shape, F32)

    xe_scr[CONV_HALO:CONV_HALO + rows, 0:half] = q_ref[...].astype(F32)
    xe_scr[CONV_HALO:CONV_HALO + rows, half:2 * half] = k_ref[...].astype(F32)
    conv = None
    for j in range(M_CONV):
        off = CONV_HALO - (M_CONV - 1) + j
        term = xe_scr[off:off + rows, :] * cw_ref[j:j + 1, :]
        conv = term if conv is None else conv + term
    qk = _silu(conv)
    xe_scr[0:CONV_HALO, :] = xe_scr[rows:rows + CONV_HALO, :]

    capped = GATE_CAP * jnp.tanh((small_ref[...] + gb_ref[...]) / GATE_CAP)
    li_all = capped
    lf_all = _log_sigmoid(capped)
    b_all = jnp.dot(_lower_tri(rows, F32), lf_all, precision=HIGHEST, preferred_element_type=F32)
    li_t = li_all.T
    b_t = b_all.T

    r_io = lax.broadcasted_iota(I32, (rows, rows), 0)
    c_io = lax.broadcasted_iota(I32, (rows, rows), 1)
    causal = r_io >= c_io

    for h in range(M_HEADS):
        q = (qk[:, h * M_DQK:(h + 1) * M_DQK] * (M_DQK ** -0.5)).astype(BF16)
        k = qk[:, half + h * M_DQK:half + (h + 1) * M_DQK]
        kb = k.astype(BF16)
        v = v_ref[:, h * M_DV:(h + 1) * M_DV]
        li_c = li_all[:, SMALL_I + h:SMALL_I + h + 1]
        b_c = b_all[:, SMALL_F + h:SMALL_F + h + 1]
        li_r = li_t[SMALL_I + h:SMALL_I + h + 1, :]
        b_r = b_t[SMALL_F + h:SMALL_F + h + 1, :]
        g = b_c[rows - 1:rows, :]
        m_prev = m_scr[h][:, 0:1]
        c_prev = c_scr[h]
        n_prev = n_scr[h]

        d_mat = jnp.where(causal, b_c - b_r + li_r, -jnp.inf)
        m_inter = b_c + m_prev
        m_t = jnp.maximum(jnp.max(d_mat, axis=1, keepdims=True), m_inter)
        s = lax.dot_general(q, kb, (((1,), (1,)), ((), ())), preferred_element_type=F32)
        p = jnp.exp(d_mat - m_t) * s
        w_inter = jnp.exp(m_inter - m_t)
        num = (jnp.dot(p.astype(BF16), v, preferred_element_type=F32)
               + w_inter * jnp.dot(q, c_prev.astype(BF16), preferred_element_type=F32))
        qn = jnp.sum(q.astype(F32) * n_prev, axis=1, keepdims=True)
        den = jnp.sum(p, axis=1, keepdims=True) + w_inter * qn
        hh = num / jnp.maximum(jnp.abs(den), jnp.exp(-m_t))

        a_r = g - b_r + li_r
        a_c = g - b_c + li_c
        m_new = jnp.maximum(g + m_prev, jnp.max(a_r, axis=1, keepdims=True))
        decay = jnp.exp(g + m_prev - m_new)
        wk = jnp.exp(a_c - m_new) * k
        c_scr[h] = decay * c_prev + lax.dot_general(
            wk.astype(BF16), v, (((0,), (0,)), ((), ())), preferred_element_type=F32)
        n_scr[h] = decay * n_prev + jnp.sum(wk, axis=0, keepdims=True)
        m_scr[h] = jnp.broadcast_to(m_new, m_scr.shape[1:])

        y = hh * lax.rsqrt(jnp.mean(hh * hh, axis=-1, keepdims=True) + EPS)
        y = y * nw_ref[:, h * M_DV:(h + 1) * M_DV]
        gate = jax.nn.sigmoid(o_ref[:, h * M_DV:(h + 1) * M_DV].astype(F32))
        ha_ref[:, h * M_DV:(h + 1) * M_DV] = (y * gate).astype(BF16)


def _mlstm(big, small, conv_w, gate_row, norm_w, bsz, seq):
    t = big.shape[0]
    rows = min(M_CHUNK_ROWS, seq)
    nc = seq // rows
    half = M_HEADS * M_DQK
    vcols = M_HEADS * M_DV
    row_map = lambda b, c: b * nc + c
    return pl.pallas_call(
        _mlstm_body,
        grid=(bsz, nc),
        in_specs=[pl.BlockSpec((rows, half), lambda b, c: (row_map(b, c), COL_QM // half)),
                  pl.BlockSpec((rows, half), lambda b, c: (row_map(b, c), COL_KM // half)),
                  pl.BlockSpec((rows, vcols), lambda b, c: (row_map(b, c), COL_VM // vcols)),
                  pl.BlockSpec((rows, vcols), lambda b, c: (row_map(b, c), COL_OM // vcols)),
                  pl.BlockSpec((rows, SMALL_COLS), lambda b, c: (row_map(b, c), 0)),
                  pl.BlockSpec((M_CONV, 2 * half), lambda b, c: (0, 0)),
                  pl.BlockSpec((1, SMALL_COLS), lambda b, c: (0, 0)),
                  pl.BlockSpec((1, vcols), lambda b, c: (0, 0))],
        out_specs=pl.BlockSpec((rows, vcols), lambda b, c: (row_map(b, c), 0)),
        out_shape=jax.ShapeDtypeStruct((t, vcols), BF16),
        scratch_shapes=[pltpu.VMEM((rows + CONV_HALO, 2 * half), F32),
                        pltpu.VMEM((M_HEADS, M_DQK, M_DV), F32),
                        pltpu.VMEM((M_HEADS, 1, M_DQK), F32),
                        pltpu.VMEM((M_HEADS, 1, 128), F32)],
        compiler_params=_cparams(("parallel", "arbitrary")),
        name="mlstm",
    )(big, big, big, big, small, conv_w, gate_row, norm_w)


def _gla_body(q_ref, k_ref, v_ref, z_ref, small_ref, aw_ref, ab_ref, nw_ref, hb_ref,
              la_scr, s_scr):
    rows = q_ref.shape[0]
    n_chunks = rows // G_CHUNK

    @pl.when(pl.program_id(1) == 0)
    def _():
        s_scr[...] = jnp.zeros(s_scr.shape, F32)

    logits = jnp.dot(small_ref[...].astype(BF16), aw_ref[...], preferred_element_type=F32)
    la_scr[...] = _log_sigmoid(logits + ab_ref[...]) / G_TAU

    tri = _lower_tri(G_CHUNK, F32)
    r_io = lax.broadcasted_iota(I32, (G_CHUNK, G_CHUNK), 0)
    c_io = lax.broadcasted_iota(I32, (G_CHUNK, G_CHUNK), 1)
    causal = r_io >= c_io

    def chunk(ci, carry):
        r0 = pl.multiple_of(ci * G_CHUNK, G_CHUNK)
        rs = pl.ds(r0, G_CHUNK)
        bc = jnp.dot(tri, la_scr[rs, :], precision=HIGHEST, preferred_element_type=F32)
        gc = bc[G_CHUNK - 1:G_CHUNK, :]
        for h in range(G_HEADS):
            ks = slice(h * G_DK, (h + 1) * G_DK)
            vs = slice(h * G_DV, (h + 1) * G_DV)
            q = q_ref[rs, ks].astype(F32) * (G_DK ** -0.5)
            k = k_ref[rs, ks].astype(F32)
            v = v_ref[rs, vs]
            bch = bc[:, ks]
            gch = gc[:, ks]
            q_in = (q * jnp.exp(bch)).astype(BF16)
            k_in = (k * jnp.exp(-bch)).astype(BF16)
            k_out = (k * jnp.exp(gch - bch)).astype(BF16)
            s_prev = s_scr[h]
            att = lax.dot_general(q_in, k_in, (((1,), (1,)), ((), ())), preferred_element_type=F32)
            att = jnp.where(causal, att, 0.0)
            o = (jnp.dot(att.astype(BF16), v, preferred_element_type=F32)
                 + jnp.dot(q_in, s_prev.astype(BF16), preferred_element_type=F32))
            eg_col = jnp.broadcast_to(jnp.exp(gch), (G_DK, G_DK)).T[:, 0:1]
            s_scr[h] = eg_col * s_prev + lax.dot_general(
                k_out, v, (((0,), (0,)), ((), ())), preferred_element_type=F32)
            y = o * lax.rsqrt(jnp.mean(o * o, axis=-1, keepdims=True) + EPS)
            y = y * nw_ref[:, vs]
            hb_ref[rs, vs] = (y * _silu(z_ref[rs, vs].astype(F32))).astype(BF16)
        return carry

    lax.fori_loop(0, n_chunks, chunk, 0)


def _gla(big, small, alpha_full, alpha_b, norm_w, bsz, seq):
    t = big.shape[0]
    rows = min(G_BLOCK_ROWS, seq)
    nb = seq // rows
    kcols = G_HEADS * G_DK
    vcols = G_HEADS * G_DV
    row_map = lambda b, c: b * nb + c
    return pl.pallas_call(
        _gla_body,
        grid=(bsz, nb),
        in_specs=[pl.BlockSpec((rows, kcols), lambda b, c: (row_map(b, c), COL_QG // kcols)),
                  pl.BlockSpec((rows, kcols), lambda b, c: (row_map(b, c), COL_KG // kcols)),
                  pl.BlockSpec((rows, vcols), lambda b, c: (row_map(b, c), COL_VG // vcols)),
                  pl.BlockSpec((rows, vcols), lambda b, c: (row_map(b, c), COL_ZG // vcols)),
                  pl.BlockSpec((rows, SMALL_COLS), lambda b, c: (row_map(b, c), 0)),
                  pl.BlockSpec((SMALL_COLS, kcols), lambda b, c: (0, 0)),
                  pl.BlockSpec((1, kcols), lambda b, c: (0, 0)),
                  pl.BlockSpec((1, vcols), lambda b, c: (0, 0))],
        out_specs=pl.BlockSpec((rows, vcols), lambda b, c: (row_map(b, c), 0)),
        out_shape=jax.ShapeDtypeStruct((t, vcols), BF16),
        scratch_shapes=[pltpu.VMEM((rows, kcols), F32),
                        pltpu.VMEM((G_HEADS, G_DK, G_DV), F32)],
        compiler_params=_cparams(("parallel", "arbitrary")),
        name="gla",
    )(big, big, big, big, small, alpha_full, alpha_b, norm_w)


def _mixout_body(h_ref, ha_ref, hb_ref, ga_ref, gb_ref, gt_ref, wpa_ref, wpb_ref, wo_ref, o_ref):
    a = jnp.dot(ha_ref[...], wpa_ref[...], preferred_element_type=F32)
    b = jnp.dot(hb_ref[...], wpb_ref[...], preferred_element_type=F32)
    y = (jax.nn.sigmoid(ga_ref[...].astype(F32)) * a
         + jax.nn.sigmoid(gb_ref[...].astype(F32)) * b)
    o_ref[...] = h_ref[...] + gt_ref[0] * jnp.dot(y.astype(BF16), wo_ref[...],
                                                  preferred_element_type=F32)


def _mixout(h, ha, hb, big, gt, w_pa, w_pb, w_o, seq):
    t, d = h.shape
    tm = min(MIX_ROWS, seq)
    per_b = seq // tm
    wspec = pl.BlockSpec((d, d), lambda i: (0, 0))
    return pl.pallas_call(
        _mixout_body,
        grid=(t // tm,),
        in_specs=[pl.BlockSpec((tm, d), lambda i: (i, 0)),
                  pl.BlockSpec((tm, d), lambda i: (i, 0)),
                  pl.BlockSpec((tm, d), lambda i: (i, 0)),
                  pl.BlockSpec((tm, d), lambda i: (i, COL_GA // d)),
                  pl.BlockSpec((tm, d), lambda i: (i, COL_GB // d)),
                  pl.BlockSpec((1, 1, d), lambda i: (i // per_b, 0, 0)),
                  wspec, wspec, wspec],
        out_specs=pl.BlockSpec((tm, d), lambda i: (i, 0)),
        out_shape=jax.ShapeDtypeStruct((t, d), F32),
        compiler_params=_cparams(("parallel",)),
        name="mix_out",
    )(h, ha, hb, big, big, gt, w_pa, w_pb, w_o)


def _route_body(h_ref, sh_ref, sc_ref, nw_ref, rwt_ref, rb_ref,
                u_ref, eidx_ref, wts_ref, pos_ref, cnt_ref, carry_scr):
    tm = h_ref.shape[0]

    @pl.when(pl.program_id(0) == 0)
    def _():
        carry_scr[...] = jnp.zeros(carry_scr.shape, F32)

    u = _modulated_rmsnorm(h_ref[...], nw_ref[...], sc_ref[0], sh_ref[0])
    half = u.shape[1] // 2
    u_ref[...] = _pack_bf16_pair(u[:, :half], u[:, half:])
    logits = lax.dot_general(rwt_ref[...], u, (((1,), (1,)), ((), ())),
                             precision=HIGHEST, preferred_element_type=F32)
    scores = jax.nn.sigmoid(logits)
    sel = scores + rb_ref[...]

    neg = -jnp.inf
    sub_io = lax.broadcasted_iota(I32, (GROUP_SIZE, tm), 0)
    pieces = []
    for g in range(N_GROUPS):
        blk = sel[g * GROUP_SIZE:(g + 1) * GROUP_SIZE, :]
        m1 = jnp.max(blk, axis=0, keepdims=True)
        first = jnp.min(jnp.where(blk == m1, sub_io, GROUP_SIZE), axis=0, keepdims=True)
        m2 = jnp.max(jnp.where(sub_io == first, neg, blk), axis=0, keepdims=True)
        pieces.append(jnp.broadcast_to(m1 + m2, (GROUP_SIZE, tm)))
    gscore = jnp.concatenate(pieces, axis=0)

    e_io = lax.broadcasted_iota(I32, (N_EXPERTS, tm), 0)
    grp_io = e_io // GROUP_SIZE
    gmask = jnp.zeros((N_EXPERTS, tm), jnp.bool_)
    for _ in range(TOPK_GROUPS):
        mx = jnp.max(gscore, axis=0, keepdims=True)
        gi = jnp.min(jnp.where(gscore == mx, grp_io, N_GROUPS), axis=0, keepdims=True)
        hit = grp_io == gi
        gmask = jnp.logical_or(gmask, hit)
        gscore = jnp.where(hit, neg, gscore)

    cur = jnp.where(gmask, sel, neg)
    row_io = lax.broadcasted_iota(I32, (TOP_K, tm), 0)
    eidx = jnp.zeros((TOP_K, tm), I32)
    wraw = jnp.zeros((TOP_K, tm), F32)
    chosen = jnp.zeros((N_EXPERTS, tm), jnp.bool_)
    hits = []
    for kk in range(TOP_K):
        mx = jnp.max(cur, axis=0, keepdims=True)
        ei = jnp.min(jnp.where(cur == mx, e_io, N_EXPERTS), axis=0, keepdims=True)
        hit = e_io == ei
        hits.append(hit)
        sc_k = jnp.sum(jnp.where(hit, scores, 0.0), axis=0, keepdims=True)
        eidx = jnp.where(row_io == kk, ei, eidx)
        wraw = jnp.where(row_io == kk, sc_k, wraw)
        chosen = jnp.logical_or(chosen, hit)
        cur = jnp.where(hit, neg, cur)

    wsum = jnp.sum(wraw, axis=0, keepdims=True)
    wts_ref[...] = wraw / wsum * ROUTED_SCALE
    eidx_ref[...] = eidx

    chosen_f = jnp.where(chosen, 1.0, 0.0)
    r_io = lax.broadcasted_iota(I32, (tm, tm), 0)
    c_io = lax.broadcasted_iota(I32, (tm, tm), 1)
    strict_upper = jnp.where(r_io < c_io, 1.0, 0.0).astype(BF16)
    prefix = jnp.dot(chosen_f.astype(BF16), strict_upper, preferred_element_type=F32)
    rank = prefix + carry_scr[:, 0:1]
    pos = jnp.zeros((TOP_K, tm), F32)
    for kk in range(TOP_K):
        p_k = jnp.sum(jnp.where(hits[kk], rank, 0.0), axis=0, keepdims=True)
        pos = jnp.where(row_io == kk, p_k, pos)
    pos_ref[...] = pos.astype(I32)
    total = carry_scr[...] + jnp.sum(chosen_f, axis=1, keepdims=True)
    carry_scr[...] = total
    cnt_ref[...] = total.astype(I32)


def _route(h, sh, sc, nw, rw_t, rb_col, seq):
    t, d = h.shape
    tm = min(ROUTE_ROWS, seq)
    per_b = seq // tm
    kspec = pl.BlockSpec((TOP_K, tm), lambda i: (0, i))
    return pl.pallas_call(
        _route_body,
        grid=(t // tm,),
        in_specs=[pl.BlockSpec((tm, d), lambda i: (i, 0)),
                  pl.BlockSpec((1, 1, d), lambda i: (i // per_b, 0, 0)),
                  pl.BlockSpec((1, 1, d), lambda i: (i // per_b, 0, 0)),
                  pl.BlockSpec((1, d), lambda i: (0, 0)),
                  pl.BlockSpec((N_EXPERTS, d), lambda i: (0, 0)),
                  pl.BlockSpec((N_EXPERTS, 1), lambda i: (0, 0))],
        out_specs=[pl.BlockSpec((tm, d // 2), lambda i: (i, 0)), kspec, kspec, kspec,
                   pl.BlockSpec((N_EXPERTS, 128), lambda i: (0, 0))],
        out_shape=[jax.ShapeDtypeStruct((t, d // 2), U32),
                   jax.ShapeDtypeStruct((TOP_K, t), I32),
                   jax.ShapeDtypeStruct((TOP_K, t), F32),
                   jax.ShapeDtypeStruct((TOP_K, t), I32),
                   jax.ShapeDtypeStruct((N_EXPERTS, 128), I32)],
        scratch_shapes=[pltpu.VMEM((N_EXPERTS, 128), F32)],
        compiler_params=_cparams(("arbitrary",)),
        name="moe_route",
    )(h, sh, sc, nw, rw_t, rb_col)


def _sc_worker_id():
    return lax.axis_index("s") * SC_CORES + lax.axis_index("c")


def _sc_scatter_rows(x, dest, n_rows):
    t, d = x.shape
    n_k = dest.shape[0]
    assert t % (SC_WORKERS * 2 * SC_CHUNK) == 0
    per_w = t // SC_WORKERS
    n_ch = per_w // SC_CHUNK
    dest4 = dest.reshape(n_k, SC_WORKERS, n_ch, SC_CHUNK).transpose(1, 2, 0, 3)
    mesh = plsc.VectorSubcoreMesh(core_axis_name="c", subcore_axis_name="s")

    @functools.partial(
        pl.kernel, mesh=mesh,
        out_type=jax.ShapeDtypeStruct((n_rows, d), x.dtype),
        scratch_types=[pltpu.VMEM((n_ch, n_k, SC_CHUNK), I32),
                       pltpu.VMEM((SC_CHUNK, d), x.dtype),
                       pltpu.VMEM((SC_CHUNK, d), x.dtype)] + [pltpu.SemaphoreType.DMA] * 4,
        name="moe_dispatch_sc",
    )
    def scatter_kernel(x_hbm, dest_hbm, out_hbm, idx_v, rows0, rows1, l0, l1, s0, s1):
        wid = _sc_worker_id()
        base = wid * per_w
        pltpu.sync_copy(dest_hbm.at[wid], idx_v)
        bufs = ((rows0, l0, s0), (rows1, l1, s1))

        def load(ci, b):
            rows, load_sem, _ = bufs[b]
            return pltpu.make_async_copy(x_hbm.at[pl.ds(base + ci * SC_CHUNK, SC_CHUNK)], rows,
                                         load_sem)

        def scatters(ci, b):
            rows, _, scatter_sem = bufs[b]
            return [pltpu.make_async_copy(rows, out_hbm.at[idx_v.at[ci, j]], scatter_sem)
                    for j in range(n_k)]

        load(0, 0).start()

        @pl.loop(0, n_ch, step=2)
        def _(ci):
            load(ci, 0).wait()
            for cp in scatters(ci, 0):
                cp.start()

            @pl.when(ci > 0)
            def _():
                for cp in scatters(ci - 1, 1):
                    cp.wait()

            load(ci + 1, 1).start()
            load(ci + 1, 1).wait()
            for cp in scatters(ci + 1, 1):
                cp.start()
            for cp in scatters(ci, 0):
                cp.wait()

            @pl.when(ci + 2 < n_ch)
            def _():
                load(ci + 2, 0).start()

        for cp in scatters(n_ch - 1, 1):
            cp.wait()

    return scatter_kernel(x, dest4)


def _sc_gather_rows(table, idx):
    n, d = idx.shape[0], table.shape[1]
    assert n % (SC_WORKERS * 2 * SC_CHUNK) == 0
    per_w = n // SC_WORKERS
    n_ch = per_w // SC_CHUNK
    idx3 = idx.reshape(SC_WORKERS, n_ch, SC_CHUNK)
    mesh = plsc.VectorSubcoreMesh(core_axis_name="c", subcore_axis_name="s")

    @functools.partial(
        pl.kernel, mesh=mesh,
        out_type=jax.ShapeDtypeStruct((n, d), table.dtype),
        scratch_types=[pltpu.VMEM((n_ch, SC_CHUNK), I32),
                       pltpu.VMEM((SC_CHUNK, d), table.dtype),
                       pltpu.VMEM((SC_CHUNK, d), table.dtype)] + [pltpu.SemaphoreType.DMA] * 4,
        name="moe_combine_sc",
    )
    def gather_kernel(table_hbm, idx_hbm, out_hbm, idx_v, rows0, rows1, g0, g1, w0, w1):
        wid = _sc_worker_id()
        base = wid * per_w
        pltpu.sync_copy(idx_hbm.at[wid], idx_v)
        bufs = ((rows0, g0, w0), (rows1, g1, w1))

        def gather(ci, b):
            rows, gather_sem, _ = bufs[b]
            return pltpu.make_async_copy(table_hbm.at[idx_v.at[ci]], rows, gather_sem)

        def write(ci, b):
            rows, _, write_sem = bufs[b]
            return pltpu.make_async_copy(rows, out_hbm.at[pl.ds(base + ci * SC_CHUNK, SC_CHUNK)],
                                         write_sem)

        gather(0, 0).start()

        @pl.loop(0, n_ch, step=2)
        def _(ci):
            gather(ci, 0).wait()
            write(ci, 0).start()

            @pl.when(ci > 0)
            def _():
                write(ci - 1, 1).wait()

            gather(ci + 1, 1).start()
            gather(ci + 1, 1).wait()
            write(ci + 1, 1).start()
            write(ci, 0).wait()

            @pl.when(ci + 2 < n_ch)
            def _():
                gather(ci + 2, 0).start()

        write(n_ch - 1, 1).wait()

    return gather_kernel(table, idx3)


def _expert_body(blk_e_ref, blk_first_ref, blk_valid_ref, x_ref, w1_ref, w3_ref, w2_ref, y_ref,
                 w1_scr, w3_scr, w2_scr):
    del blk_e_ref
    j = pl.program_id(0)
    valid = blk_valid_ref[j]

    @pl.when(blk_first_ref[j] == 1)
    def _():
        w1_scr[...] = w1_ref[0].astype(BF16)
        w3_scr[...] = w3_ref[0].astype(BF16)
        w2_scr[...] = w2_ref[0].astype(BF16)

    @pl.when(valid > 0)
    def _():
        half = x_ref.shape[1]
        rows = lax.broadcasted_iota(I32, x_ref.shape, 0)
        lo, hi = _unpack_bf16_pair(jnp.where(rows < valid, x_ref[...], 0))
        lo, hi = lo.astype(BF16), hi.astype(BF16)

        def proj(w_scr):
            return (jnp.dot(lo, w_scr[0:half, :], preferred_element_type=F32)
                    + jnp.dot(hi, w_scr[half:2 * half, :], preferred_element_type=F32))

        hid = _silu(proj(w1_scr)) * proj(w3_scr)
        y = jnp.dot(hid.astype(BF16), w2_scr[...], preferred_element_type=F32)
        y_ref[...] = _pack_bf16_pair(y[:, :half], y[:, half:])

    @pl.when(valid == 0)
    def _():
        y_ref[...] = jnp.zeros(y_ref.shape, U32)


def _experts(xg, blk_e, blk_first, blk_valid, w1, w3, w2):
    n_rows, half = xg.shape
    n_blocks = n_rows // EXPERT_ROWS
    d, de = w1.shape[-2:]
    grid_spec = pltpu.PrefetchScalarGridSpec(
        num_scalar_prefetch=3,
        grid=(n_blocks,),
        in_specs=[pl.BlockSpec((EXPERT_ROWS, half), lambda j, be, bf, bv: (j, 0)),
                  pl.BlockSpec((1, d, de), lambda j, be, bf, bv: (be[j], 0, 0)),
                  pl.BlockSpec((1, d, de), lambda j, be, bf, bv: (be[j], 0, 0)),
                  pl.BlockSpec((1, de, d), lambda j, be, bf, bv: (be[j], 0, 0))],
        out_specs=pl.BlockSpec((EXPERT_ROWS, half), lambda j, be, bf, bv: (j, 0)),
        scratch_shapes=[pltpu.VMEM((d, de), BF16), pltpu.VMEM((d, de), BF16),
                        pltpu.VMEM((de, d), BF16)],
    )
    return pl.pallas_call(
        _expert_body,
        grid_spec=grid_spec,
        out_shape=jax.ShapeDtypeStruct((n_rows, half), U32),
        compiler_params=_cparams(("arbitrary",)),
        name="moe_experts",
    )(blk_e, blk_first, blk_valid, xg, w1, w3, w2)


def _combine_body(h_ref, u_ref, yg_ref, wts_ref, gt_ref, s1_ref, s3_ref, s2_ref, o_ref):
    half = u_ref.shape[1]
    lo, hi = _unpack_bf16_pair(u_ref[...])
    lo, hi = lo.astype(BF16), hi.astype(BF16)

    def proj(w_ref):
        return (jnp.dot(lo, w_ref[0:half, :], preferred_element_type=F32)
                + jnp.dot(hi, w_ref[half:2 * half, :], preferred_element_type=F32))

    hid = _silu(proj(s1_ref)) * proj(s3_ref)
    shared = jnp.dot(hid.astype(BF16), s2_ref[...], preferred_element_type=F32)

    routed_lo = routed_hi = None
    for kk in range(TOP_K):
        y_lo, y_hi = _unpack_bf16_pair(yg_ref[kk])
        w = wts_ref[:, kk:kk + 1]
        routed_lo = y_lo * w if routed_lo is None else routed_lo + y_lo * w
        routed_hi = y_hi * w if routed_hi is None else routed_hi + y_hi * w
    gt = gt_ref[0]
    o_ref[:, 0:half] = h_ref[:, 0:half] + gt[:, 0:half] * (routed_lo + shared[:, 0:half])
    o_ref[:, half:2 * half] = (h_ref[:, half:2 * half]
                               + gt[:, half:2 * half] * (routed_hi + shared[:, half:2 * half]))


def _combine(h, u, yg, wts_tk, gt, s1, s3, s2, seq):
    t, d = h.shape
    half = d // 2
    tm = min(COMBINE_ROWS, seq)
    per_b = seq // tm
    ds_ = s1.shape[-1]
    return pl.pallas_call(
        _combine_body,
        grid=(t // tm,),
        in_specs=[pl.BlockSpec((tm, d), lambda i: (i, 0)),
                  pl.BlockSpec((tm, half), lambda i: (i, 0)),
                  pl.BlockSpec((TOP_K, tm, half), lambda i: (0, i, 0)),
                  pl.BlockSpec((tm, TOP_K), lambda i: (i, 0)),
                  pl.BlockSpec((1, 1, d), lambda i: (i // per_b, 0, 0)),
                  pl.BlockSpec((d, ds_), lambda i: (0, 0)),
                  pl.BlockSpec((d, ds_), lambda i: (0, 0)),
                  pl.BlockSpec((ds_, d), lambda i: (0, 0))],
        out_specs=pl.BlockSpec((tm, d), lambda i: (i, 0)),
        out_shape=jax.ShapeDtypeStruct((t, d), F32),
        compiler_params=_cparams(("parallel",)),
        name="moe_combine",
    )(h, u, yg, wts_tk, gt, s1, s3, s2)


def _final_norm_body(h_ref, w_ref, o_ref):
    x = h_ref[...]
    o_ref[...] = (x * lax.rsqrt(jnp.mean(x * x, axis=-1, keepdims=True) + EPS)) * w_ref[...]


def _final_norm(h, w):
    t, d = h.shape
    tm = min(NORM_ROWS, t)
    return pl.pallas_call(
        _final_norm_body,
        grid=(t // tm,),
        in_specs=[pl.BlockSpec((tm, d), lambda i: (i, 0)), pl.BlockSpec((1, d), lambda i: (0, 0))],
        out_specs=pl.BlockSpec((tm, d), lambda i: (i, 0)),
        out_shape=jax.ShapeDtypeStruct((t, d), F32),
        compiler_params=_cparams(("parallel",)),
        name="final_norm",
    )(h, w)


def _split_w_in(w_in):
    sizes = (2 * M_HEADS * M_DQK, M_HEADS * M_DV, M_HEADS * M_DV, M_HEADS, M_HEADS,
             G_HEADS * G_DK, G_HEADS * G_DK, G_HEADS * G_DV, G_RANK, G_HEADS * G_DV,
             D_MODEL, D_MODEL)
    parts, off = [], 0
    for n in sizes:
        parts.append(w_in[:, off:off + n])
        off += n
    qk_m, v_m, o_m, i_m, f_m, q_g, k_g, v_g, r_g, z_g, g_a, g_b = parts
    big = jnp.concatenate([qk_m, v_m, o_m, q_g, k_g, v_g, z_g, g_a, g_b], axis=1).astype(BF16)
    pad = jnp.zeros((w_in.shape[0], SMALL_COLS - 2 * M_HEADS - G_RANK), w_in.dtype)
    small = jnp.concatenate([i_m, f_m, r_g, pad], axis=1).astype(BF16)
    return big, small


def _moe_layout(counts, eidx, pos, n_blocks):
    padded = (counts + EXPERT_ROWS - 1) // EXPERT_ROWS * EXPERT_ROWS
    pend = jnp.cumsum(padded)
    pstart = pend - padded
    experts = jnp.arange(N_EXPERTS, dtype=I32)
    dest = pos + jnp.sum(jnp.where(eidx[..., None] == experts, pstart, 0), axis=-1)
    blk_start = jnp.arange(n_blocks, dtype=I32) * EXPERT_ROWS
    owner = jnp.sum((pend[None, :] <= blk_start[:, None]).astype(I32), axis=1)
    blk_e = jnp.minimum(owner, N_EXPERTS - 1)
    prev = jnp.concatenate([jnp.full((1,), -1, I32), blk_e[:-1]])
    blk_first = (blk_e != prev).astype(I32)
    used = jnp.where(owner < N_EXPERTS, counts[blk_e] - (blk_start - pstart[blk_e]), 0)
    blk_valid = jnp.clip(used, 0, EXPERT_ROWS)
    return dest.astype(I32), blk_e.astype(I32), blk_first, blk_valid.astype(I32)


def kernel(x, c, ada_w, ada_b, norm_mix_w, norm_moe_w, w_in, m_conv_w, m_gate_b, m_norm_w,
           g_alpha_w, g_alpha_b, g_norm_w, w_pa, w_pb, w_o, router_w, router_b,
           exp_w1, exp_w3, exp_w2, sh_w1, sh_w3, sh_w2, final_norm_w):
    bsz, seq, d = x.shape
    depth = ada_w.shape[0]
    t = bsz * seq
    n_rows = t * TOP_K + N_EXPERTS * EXPERT_ROWS
    n_blocks = n_rows // EXPERT_ROWS

    ada = _ada(c, ada_w, ada_b).reshape(depth, bsz, 6, 1, d)
    h = x.reshape(t, d)
    for l in range(depth):
        sh1, sc1, gt1, sh2, sc2, gt2 = (ada[l, :, i] for i in range(6))

        w_big, w_small = _split_w_in(w_in[l])
        big, small = _inproj(h, sh1, sc1, norm_mix_w[l][None, :], w_big, w_small, seq)
        gate_row = jnp.zeros((1, SMALL_COLS), F32)
        gate_row = gate_row.at[0, SMALL_I:SMALL_I + M_HEADS].set(m_gate_b[l, 0])
        gate_row = gate_row.at[0, SMALL_F:SMALL_F + M_HEADS].set(m_gate_b[l, 1])
        ha = _mlstm(big, small, m_conv_w[l], gate_row, m_norm_w[l][None, :], bsz, seq)
        alpha_full = jnp.zeros((SMALL_COLS, G_HEADS * G_DK), F32)
        alpha_full = alpha_full.at[SMALL_R:SMALL_R + G_RANK].set(g_alpha_w[l]).astype(BF16)
        hb = _gla(big, small, alpha_full, g_alpha_b[l][None, :], g_norm_w[l][None, :], bsz, seq)
        h = _mixout(h, ha, hb, big, gt1, w_pa[l].astype(BF16), w_pb[l].astype(BF16),
                    w_o[l].astype(BF16), seq)

        u, eidx, wts, pos, cnt = _route(h, sh2, sc2, norm_moe_w[l][None, :],
                                        router_w[l].T, router_b[l][:, None], seq)
        dest, blk_e, blk_first, blk_valid = _moe_layout(cnt[:, 0], eidx, pos, n_blocks)
        xg = _sc_scatter_rows(u, dest, n_rows)
        y = _experts(xg, blk_e, blk_first, blk_valid, exp_w1[l], exp_w3[l], exp_w2[l])
        yg = _sc_gather_rows(y, dest.reshape(-1)).reshape(TOP_K, t, d // 2)
        h = _combine(h, u, yg, wts.T, gt2, sh_w1[l].astype(BF16), sh_w3[l].astype(BF16),
                     sh_w2[l].astype(BF16), seq)

    return _final_norm(h, final_norm_w[None, :]).reshape(bsz, seq, d)
```

```python
import functools

import jax
import jax.numpy as jnp
import numpy as np
from jax import lax
from jax.experimental import pallas as pl
from jax.experimental.pallas import tpu as pltpu
from jax.experimental.pallas import tpu_sc as plsc

F32 = jnp.float32
BF16 = jnp.bfloat16
I32 = jnp.int32
U32 = jnp.uint32
HI_MASK = np.uint32(0xFFFF0000)
HIGHEST = lax.Precision.HIGHEST

SC_CORES = 2
SC_SUBCORES = 16
SC_WORKERS = SC_CORES * SC_SUBCORES
SC_CHUNK = 64

D_MODEL = 1024
M_HEADS = 4
M_DQK = 128
M_DV = 256
M_CONV = 4
GATE_CAP = 15.0
G_HEADS = 4
G_DK = 128
G_DV = 256
G_RANK = 16
G_TAU = 16.0
G_CHUNK = 64
N_EXPERTS = 64
TOP_K = 8
N_GROUPS = 8
GROUP_SIZE = N_EXPERTS // N_GROUPS
TOPK_GROUPS = 4
D_EXPERT = 256
D_SHARED = 256
ROUTED_SCALE = 2.5
EPS = 1e-6

M_CHUNK_ROWS = 256
G_BLOCK_ROWS = 256
PROJ_ROWS = 1024
MIX_ROWS = 512
ROUTE_ROWS = 512
EXPERT_ROWS = 512
COMBINE_ROWS = 512
CONV_HALO = 8
VMEM_LIMIT = 48 * 1024 * 1024

COL_QM, COL_KM, COL_VM, COL_OM = 0, 512, 1024, 2048
COL_QG, COL_KG, COL_VG, COL_ZG, COL_GA, COL_GB = 3072, 3584, 4096, 5120, 6144, 7168
BIG_COLS = 8192
SMALL_COLS = 128
SMALL_I, SMALL_F, SMALL_R = 0, M_HEADS, 2 * M_HEADS


def _cparams(sem, vmem=VMEM_LIMIT):
    return pltpu.CompilerParams(dimension_semantics=sem, vmem_limit_bytes=vmem)


def _silu(x):
    return x * jax.nn.sigmoid(x)


def _log_sigmoid(x):
    return jnp.minimum(x, 0.0) - jnp.log1p(jnp.exp(-jnp.abs(x)))


def _modulated_rmsnorm(x, w, sc, sh):
    y = x * lax.rsqrt(jnp.mean(x * x, axis=-1, keepdims=True) + EPS)
    return (y * w) * (1.0 + sc) + sh


def _pack_bf16_pair(lo, hi):
    lo_bits = lax.bitcast_convert_type(lo.astype(BF16).astype(F32), U32)
    hi_bits = lax.bitcast_convert_type(hi.astype(BF16).astype(F32), U32)
    return (lo_bits >> 16) | (hi_bits & HI_MASK)


def _unpack_bf16_pair(packed):
    lo = lax.bitcast_convert_type(packed << 16, F32)
    hi = lax.bitcast_convert_type(packed & HI_MASK, F32)
    return lo, hi


def _lower_tri(n, dtype):
    r = lax.broadcasted_iota(I32, (n, n), 0)
    c = lax.broadcasted_iota(I32, (n, n), 1)
    return (r >= c).astype(dtype)


def _ada_body(c_ref, w_ref, b_ref, o_ref):
    cond = _silu(c_ref[...])
    o_ref[0] = jnp.dot(cond.astype(BF16), w_ref[0].astype(BF16),
                       preferred_element_type=F32) + b_ref[0]


def _ada(c, ada_w, ada_b):
    depth, d, six_d = ada_w.shape
    bsz = c.shape[0]
    nj = six_d // d
    return pl.pallas_call(
        _ada_body,
        grid=(depth, nj),
        in_specs=[pl.BlockSpec((bsz, d), lambda l, j: (0, 0)),
                  pl.BlockSpec((1, d, d), lambda l, j: (l, 0, j)),
                  pl.BlockSpec((1, 1, d), lambda l, j: (l, 0, j))],
        out_specs=pl.BlockSpec((1, bsz, d), lambda l, j: (l, 0, j)),
        out_shape=jax.ShapeDtypeStruct((depth, bsz, six_d), F32),
        compiler_params=_cparams(("parallel", "parallel")),
        name="ada_ln",
    )(c, ada_w, ada_b.reshape(depth, 1, six_d))


def _inproj_body(h_ref, sh_ref, sc_ref, nw_ref, wb_ref, ws_ref, big_ref, small_ref, u_scr):
    @pl.when(pl.program_id(1) == 0)
    def _():
        u = _modulated_rmsnorm(h_ref[...], nw_ref[...], sc_ref[0], sh_ref[0])
        ub = u.astype(BF16)
        u_scr[...] = ub
        small_ref[...] = jnp.dot(ub, ws_ref[...], preferred_element_type=F32)

    big_ref[...] = jnp.dot(u_scr[...], wb_ref[...], preferred_element_type=F32).astype(BF16)


def _inproj(h, sh, sc, nw, w_big, w_small, seq):
    t, d = h.shape
    tm = min(PROJ_ROWS, seq)
    per_b = seq // tm
    tn = 2048
    return pl.pallas_call(
        _inproj_body,
        grid=(t // tm, BIG_COLS // tn),
        in_specs=[pl.BlockSpec((tm, d), lambda i, j: (i, 0)),
                  pl.BlockSpec((1, 1, d), lambda i, j: (i // per_b, 0, 0)),
                  pl.BlockSpec((1, 1, d), lambda i, j: (i // per_b, 0, 0)),
                  pl.BlockSpec((1, d), lambda i, j: (0, 0)),
                  pl.BlockSpec((d, tn), lambda i, j: (0, j)),
                  pl.BlockSpec((d, SMALL_COLS), lambda i, j: (0, 0))],
        out_specs=[pl.BlockSpec((tm, tn), lambda i, j: (i, j)),
                   pl.BlockSpec((tm, SMALL_COLS), lambda i, j: (i, 0))],
        out_shape=[jax.ShapeDtypeStruct((t, BIG_COLS), BF16),
                   jax.ShapeDtypeStruct((t, SMALL_COLS), F32)],
        scratch_shapes=[pltpu.VMEM((tm, d), BF16)],
        compiler_params=_cparams(("parallel", "arbitrary")),
        name="in_proj",
    )(h, sh, sc, nw, w_big, w_small)


def _mlstm_body(q_ref, k_ref, v_ref, o_ref, small_ref, cw_ref, gb_ref, nw_ref, ha_ref,
                xe_scr, c_scr, n_scr, m_scr):
    rows = q_ref.shape[0]
    half = M_HEADS * M_DQK

    @pl.when(pl.program_id(1) == 0)
    def _():
        xe_scr[0:CONV_HALO, :] = jnp.zeros((CONV_HALO, 2 * half), F32)
        c_scr[...] = jnp.zeros(c_scr.shape, F32)
        n_scr[...] = jnp.zeros(n_scr.shape, F32)
        m_scr[...] = jnp.zeros(m_scr.shape, F32)

    xe_scr[CONV_HALO:CONV_HALO + rows, 0:half] = q_ref[...].astype(F32)
    xe_scr[CONV_HALO:CONV_HALO + rows, half:2 * half] = k_ref[...].astype(F32)
    conv = None
    for j in range(M_CONV):
        off = CONV_HALO - (M_CONV - 1) + j
        term = xe_scr[off:off + rows, :] * cw_ref[j:j + 1, :]
        conv = term if conv is None else conv + term
    qk = _silu(conv)
    xe_scr[0:CONV_HALO, :] = xe_scr[rows:rows + CONV_HALO, :]

    capped = GATE_CAP * jnp.tanh((small_ref[...] + gb_ref[...]) / GATE_CAP)
    li_all = capped
    lf_all = _log_sigmoid(capped)
    b_all = jnp.dot(_lower_tri(rows, F32), lf_all, precision=HIGHEST, preferred_element_type=F32)
    li_t = li_all.T
    b_t = b_all.T

    r_io = lax.broadcasted_iota(I32, (rows, rows), 0)
    c_io = lax.broadcasted_iota(I32, (rows, rows), 1)
    causal = r_io >= c_io

    for h in range(M_HEADS):
        q = (qk[:, h * M_DQK:(h + 1) * M_DQK] * (M_DQK ** -0.5)).astype(BF16)
        k = qk[:, half + h * M_DQK:half + (h + 1) * M_DQK]
        kb = k.astype(BF16)
        v = v_ref[:, h * M_DV:(h + 1) * M_DV]
        li_c = li_all[:, SMALL_I + h:SMALL_I + h + 1]
        b_c = b_all[:, SMALL_F + h:SMALL_F + h + 1]
        li_r = li_t[SMALL_I + h:SMALL_I + h + 1, :]
        b_r = b_t[SMALL_F + h:SMALL_F + h + 1, :]
        g = b_c[rows - 1:rows, :]
        m_prev = m_scr[h][:, 0:1]
        c_prev = c_scr[h]
        n_prev = n_scr[h]

        d_mat = jnp.where(causal, b_c - b_r + li_r, -jnp.inf)
        m_inter = b_c + m_prev
        m_t = jnp.maximum(jnp.max(d_mat, axis=1, keepdims=True), m_inter)
        s = lax.dot_general(q, kb, (((1,), (1,)), ((), ())), preferred_element_type=F32)
        p = jnp.exp(d_mat - m_t) * s
        w_inter = jnp.exp(m_inter - m_t)
        num = (jnp.dot(p.astype(BF16), v, preferred_element_type=F32)
               + w_inter * jnp.dot(q, c_prev.astype(BF16), preferred_element_type=F32))
        qn = jnp.sum(q.astype(F32) * n_prev, axis=1, keepdims=True)
        den = jnp.sum(p, axis=1, keepdims=True) + w_inter * qn
        hh = num / jnp.maximum(jnp.abs(den), jnp.exp(-m_t))

        a_r = g - b_r + li_r
        a_c = g - b_c + li_c
        m_new = jnp.maximum(g + m_prev, jnp.max(a_r, axis=1, keepdims=True))
        decay = jnp.exp(g + m_prev - m_new)
        wk = jnp.exp(a_c - m_new) * k
        c_scr[h] = decay * c_prev + lax.dot_general(
            wk.astype(BF16), v, (((0,), (0,)), ((), ())), preferred_element_type=F32)
        n_scr[h] = decay * n_prev + jnp.sum(wk, axis=0, keepdims=True)
        m_scr[h] = jnp.broadcast_to(m_new, m_scr.shape[1:])

        y = hh * lax.rsqrt(jnp.mean(hh * hh, axis=-1, keepdims=True) + EPS)
        y = y * nw_ref[:, h * M_DV:(h + 1) * M_DV]
        gate = jax.nn.sigmoid(o_ref[:, h * M_DV:(h + 1) * M_DV].astype(F32))
        ha_ref[:, h * M_DV:(h + 1) * M_DV] = (y * gate).astype(BF16)


def _mlstm(big, small, conv_w, gate_row, norm_w, bsz, seq):
    t = big.shape[0]
    rows = min(M_CHUNK_ROWS, seq)
    nc = seq // rows
    half = M_HEADS * M_DQK
    vcols = M_HEADS * M_DV
    row_map = lambda b, c: b * nc + c
    return pl.pallas_call(
        _mlstm_body,
        grid=(bsz, nc),
        in_specs=[pl.BlockSpec((rows, half), lambda b, c: (row_map(b, c), COL_QM // half)),
                  pl.BlockSpec((rows, half), lambda b, c: (row_map(b, c), COL_KM // half)),
                  pl.BlockSpec((rows, vcols), lambda b, c: (row_map(b, c), COL_VM // vcols)),
                  pl.BlockSpec((rows, vcols), lambda b, c: (row_map(b, c), COL_OM // vcols)),
                  pl.BlockSpec((rows, SMALL_COLS), lambda b, c: (row_map(b, c), 0)),
                  pl.BlockSpec((M_CONV, 2 * half), lambda b, c: (0, 0)),
                  pl.BlockSpec((1, SMALL_COLS), lambda b, c: (0, 0)),
                  pl.BlockSpec((1, vcols), lambda b, c: (0, 0))],
        out_specs=pl.BlockSpec((rows, vcols), lambda b, c: (row_map(b, c), 0)),
        out_shape=jax.ShapeDtypeStruct((t, vcols), BF16),
        scratch_shapes=[pltpu.VMEM((rows + CONV_HALO, 2 * half), F32),
                        pltpu.VMEM((M_HEADS, M_DQK, M_DV), F32),
                        pltpu.VMEM((M_HEADS, 1, M_DQK), F32),
                        pltpu.VMEM((M_HEADS, 1, 128), F32)],
        compiler_params=_cparams(("parallel", "arbitrary")),
        name="mlstm",
    )(big, big, big, big, small, conv_w, gate_row, norm_w)


def _gla_body(q_ref, k_ref, v_ref, z_ref, small_ref, aw_ref, ab_ref, nw_ref, hb_ref,
              la_scr, s_scr):
    rows = q_ref.shape[0]
    n_chunks = rows // G_CHUNK

    @pl.when(pl.program_id(1) == 0)
    def _():
        s_scr[...] = jnp.zeros(s_scr.shape, F32)

    logits = jnp.dot(small_ref[...].astype(BF16), aw_ref[...], preferred_element_type=F32)
    la_scr[...] = _log_sigmoid(logits + ab_ref[...]) / G_TAU

    tri = _lower_tri(G_CHUNK, F32)
    r_io = lax.broadcasted_iota(I32, (G_CHUNK, G_CHUNK), 0)
    c_io = lax.broadcasted_iota(I32, (G_CHUNK, G_CHUNK), 1)
    causal = r_io >= c_io

    def chunk(ci, carry):
        r0 = pl.multiple_of(ci * G_CHUNK, G_CHUNK)
        rs = pl.ds(r0, G_CHUNK)
        bc = jnp.dot(tri, la_scr[rs, :], precision=HIGHEST, preferred_element_type=F32)
        gc = bc[G_CHUNK - 1:G_CHUNK, :]
        for h in range(G_HEADS):
            ks = slice(h * G_DK, (h + 1) * G_DK)
            vs = slice(h * G_DV, (h + 1) * G_DV)
            q = q_ref[rs, ks].astype(F32) * (G_DK ** -0.5)
            k = k_ref[rs, ks].astype(F32)
            v = v_ref[rs, vs]
            bch = bc[:, ks]
            gch = gc[:, ks]
            q_in = (q * jnp.exp(bch)).astype(BF16)
            k_in = (k * jnp.exp(-bch)).astype(BF16)
            k_out = (k * jnp.exp(gch - bch)).astype(BF16)
            s_prev = s_scr[h]
            att = lax.dot_general(q_in, k_in, (((1,), (1,)), ((), ())), preferred_element_type=F32)
            att = jnp.where(causal, att, 0.0)
            o = (jnp.dot(att.astype(BF16), v, preferred_element_type=F32)
                 + jnp.dot(q_in, s_prev.astype(BF16), preferred_element_type=F32))
            eg_col = jnp.broadcast_to(jnp.exp(gch), (G_DK, G_DK)).T[:, 0:1]
            s_scr[h] = eg_col * s_prev + lax.dot_general(
                k_out, v, (((0,), (0,)), ((), ())), preferred_element_type=F32)
            y = o * lax.rsqrt(jnp.mean(o * o, axis=-1, keepdims=True) + EPS)
            y = y * nw_ref[:, vs]
            hb_ref[rs, vs] = (y * _silu(z_ref[rs, vs].astype(F32))).astype(BF16)
        return carry

    lax.fori_loop(0, n_chunks, chunk, 0)


def _gla(big, small, alpha_full, alpha_b, norm_w, bsz, seq):
    t = big.shape[0]
    rows = min(G_BLOCK_ROWS, seq)
    nb = seq // rows
    kcols = G_HEADS * G_DK
    vcols = G_HEADS * G_DV
    row_map = lambda b, c: b * nb + c
    return pl.pallas_call(
        _gla_body,
        grid=(bsz, nb),
        in_specs=[pl.BlockSpec((rows, kcols), lambda b, c: (row_map(b, c), COL_QG // kcols)),
                  pl.BlockSpec((rows, kcols), lambda b, c: (row_map(b, c), COL_KG // kcols)),
                  pl.BlockSpec((rows, vcols), lambda b, c: (row_map(b, c), COL_VG // vcols)),
                  pl.BlockSpec((rows, vcols), lambda b, c: (row_map(b, c), COL_ZG // vcols)),
                  pl.BlockSpec((rows, SMALL_COLS), lambda b, c: (row_map(b, c), 0)),
                  pl.BlockSpec((SMALL_COLS, kcols), lambda b, c: (0, 0)),
                  pl.BlockSpec((1, kcols), lambda b, c: (0, 0)),
                  pl.BlockSpec((1, vcols), lambda b, c: (0, 0))],
        out_specs=pl.BlockSpec((rows, vcols), lambda b, c: (row_map(b, c), 0)),
        out_shape=jax.ShapeDtypeStruct((t, vcols), BF16),
        scratch_shapes=[pltpu.VMEM((rows, kcols), F32),
                        pltpu.VMEM((G_HEADS, G_DK, G_DV), F32)],
        compiler_params=_cparams(("parallel", "arbitrary")),
        name="gla",
    )(big, big, big, big, small, alpha_full, alpha_b, norm_w)


def _mixout_body(h_ref, ha_ref, hb_ref, ga_ref, gb_ref, gt_ref, wpa_ref, wpb_ref, wo_ref, o_ref):
    a = jnp.dot(ha_ref[...], wpa_ref[...], preferred_element_type=F32)
    b = jnp.dot(hb_ref[...], wpb_ref[...], preferred_element_type=F32)
    y = (jax.nn.sigmoid(ga_ref[...].astype(F32)) * a
         + jax.nn.sigmoid(gb_ref[...].astype(F32)) * b)
    o_ref[...] = h_ref[...] + gt_ref[0] * jnp.dot(y.astype(BF16), wo_ref[...],
                                                  preferred_element_type=F32)


def _mixout(h, ha, hb, big, gt, w_pa, w_pb, w_o, seq):
    t, d = h.shape
    tm = min(MIX_ROWS, seq)
    per_b = seq // tm
    wspec = pl.BlockSpec((d, d), lambda i: (0, 0))
    return pl.pallas_call(
        _mixout_body,
        grid=(t // tm,),
        in_specs=[pl.BlockSpec((tm, d), lambda i: (i, 0)),
                  pl.BlockSpec((tm, d), lambda i: (i, 0)),
                  pl.BlockSpec((tm, d), lambda i: (i, 0)),
                  pl.BlockSpec((tm, d), lambda i: (i, COL_GA // d)),
                  pl.BlockSpec((tm, d), lambda i: (i, COL_GB // d)),
                  pl.BlockSpec((1, 1, d), lambda i: (i // per_b, 0, 0)),
                  wspec, wspec, wspec],
        out_specs=pl.BlockSpec((tm, d), lambda i: (i, 0)),
        out_shape=jax.ShapeDtypeStruct((t, d), F32),
        compiler_params=_cparams(("parallel",)),
        name="mix_out",
    )(h, ha, hb, big, big, gt, w_pa, w_pb, w_o)


def _route_body(h_ref, sh_ref, sc_ref, nw_ref, rwt_ref, rb_ref,
                u_ref, eidx_ref, wts_ref, pos_ref, cnt_ref, carry_scr):
    tm = h_ref.shape[0]

    @pl.when(pl.program_id(0) == 0)
    def _():
        carry_scr[...] = jnp.zeros(carry_scr.shape, F32)

    u = _modulated_rmsnorm(h_ref[...], nw_ref[...], sc_ref[0], sh_ref[0])
    half = u.shape[1] // 2
    u_ref[...] = _pack_bf16_pair(u[:, :half], u[:, half:])
    logits = lax.dot_general(rwt_ref[...], u, (((1,), (1,)), ((), ())),
                             precision=HIGHEST, preferred_element_type=F32)
    scores = jax.nn.sigmoid(logits)
    sel = scores + rb_ref[...]

    neg = -jnp.inf
    sub_io = lax.broadcasted_iota(I32, (GROUP_SIZE, tm), 0)
    pieces = []
    for g in range(N_GROUPS):
        blk = sel[g * GROUP_SIZE:(g + 1) * GROUP_SIZE, :]
        m1 = jnp.max(blk, axis=0, keepdims=True)
        first = jnp.min(jnp.where(blk == m1, sub_io, GROUP_SIZE), axis=0, keepdims=True)
        m2 = jnp.max(jnp.where(sub_io == first, neg, blk), axis=0, keepdims=True)
        pieces.append(jnp.broadcast_to(m1 + m2, (GROUP_SIZE, tm)))
    gscore = jnp.concatenate(pieces, axis=0)

    e_io = lax.broadcasted_iota(I32, (N_EXPERTS, tm), 0)
    grp_io = e_io // GROUP_SIZE
    gmask = jnp.zeros((N_EXPERTS, tm), jnp.bool_)
    for _ in range(TOPK_GROUPS):
        mx = jnp.max(gscore, axis=0, keepdims=True)
        gi = jnp.min(jnp.where(gscore == mx, grp_io, N_GROUPS), axis=0, keepdims=True)
        hit = grp_io == gi
        gmask = jnp.logical_or(gmask, hit)
        gscore = jnp.where(hit, neg, gscore)

    cur = jnp.where(gmask, sel, neg)
    row_io = lax.broadcasted_iota(I32, (TOP_K, tm), 0)
    eidx = jnp.zeros((TOP_K, tm), I32)
    wraw = jnp.zeros((TOP_K, tm), F32)
    chosen = jnp.zeros((N_EXPERTS, tm), jnp.bool_)
    hits = []
    for kk in range(TOP_K):
        mx = jnp.max(cur, axis=0, keepdims=True)
        ei = jnp.min(jnp.where(cur == mx, e_io, N_EXPERTS), axis=0, keepdims=True)
        hit = e_io == ei
        hits.append(hit)
        sc_k = jnp.sum(jnp.where(hit, scores, 0.0), axis=0, keepdims=True)
        eidx = jnp.where(row_io == kk, ei, eidx)
        wraw = jnp.where(row_io == kk, sc_k, wraw)
        chosen = jnp.logical_or(chosen, hit)
        cur = jnp.where(hit, neg, cur)

    wsum = jnp.sum(wraw, axis=0, keepdims=True)
    wts_ref[...] = wraw / wsum * ROUTED_SCALE
    eidx_ref[...] = eidx

    chosen_f = jnp.where(chosen, 1.0, 0.0)
    r_io = lax.broadcasted_iota(I32, (tm, tm), 0)
    c_io = lax.broadcasted_iota(I32, (tm, tm), 1)
    strict_upper = jnp.where(r_io < c_io, 1.0, 0.0).astype(BF16)
    prefix = jnp.dot(chosen_f.astype(BF16), strict_upper, preferred_element_type=F32)
    rank = prefix + carry_scr[:, 0:1]
    pos = jnp.zeros((TOP_K, tm), F32)
    for kk in range(TOP_K):
        p_k = jnp.sum(jnp.where(hits[kk], rank, 0.0), axis=0, keepdims=True)
        pos = jnp.where(row_io == kk, p_k, pos)
    pos_ref[...] = pos.astype(I32)
    total = carry_scr[...] + jnp.sum(chosen_f, axis=1, keepdims=True)
    carry_scr[...] = total
    cnt_ref[...] = total.astype(I32)


def _route(h, sh, sc, nw, rw_t, rb_col, seq):
    t, d = h.shape
    tm = min(ROUTE_ROWS, seq)
    per_b = seq // tm
    kspec = pl.BlockSpec((TOP_K, tm), lambda i: (0, i))
    return pl.pallas_call(
        _route_body,
        grid=(t // tm,),
        in_specs=[pl.BlockSpec((tm, d), lambda i: (i, 0)),
                  pl.BlockSpec((1, 1, d), lambda i: (i // per_b, 0, 0)),
                  pl.BlockSpec((1, 1, d), lambda i: (i // per_b, 0, 0)),
                  pl.BlockSpec((1, d), lambda i: (0, 0)),
                  pl.BlockSpec((N_EXPERTS, d), lambda i: (0, 0)),
                  pl.BlockSpec((N_EXPERTS, 1), lambda i: (0, 0))],
        out_specs=[pl.BlockSpec((tm, d // 2), lambda i: (i, 0)), kspec, kspec, kspec,
                   pl.BlockSpec((N_EXPERTS, 128), lambda i: (0, 0))],
        out_shape=[jax.ShapeDtypeStruct((t, d // 2), U32),
                   jax.ShapeDtypeStruct((TOP_K, t), I32),
                   jax.ShapeDtypeStruct((TOP_K, t), F32),
                   jax.ShapeDtypeStruct((TOP_K, t), I32),
                   jax.ShapeDtypeStruct((N_EXPERTS, 128), I32)],
        scratch_shapes=[pltpu.VMEM((N_EXPERTS, 128), F32)],
        compiler_params=_cparams(("arbitrary",)),
        name="moe_route",
    )(h, sh, sc, nw, rw_t, rb_col)


def _sc_worker_id():
    return lax.axis_index("s") * SC_CORES + lax.axis_index("c")


def _sc_scatter_rows(x, dest, n_rows):
    t, d = x.shape
    n_k = dest.shape[0]
    assert t % (SC_WORKERS * 2 * SC_CHUNK) == 0
    per_w = t // SC_WORKERS
    n_ch = per_w // SC_CHUNK
    dest4 = dest.reshape(n_k, SC_WORKERS, n_ch, SC_CHUNK).transpose(1, 2, 0, 3)
    mesh = plsc.VectorSubcoreMesh(core_axis_name="c", subcore_axis_name="s")

    @functools.partial(
        pl.kernel, mesh=mesh,
        out_type=jax.ShapeDtypeStruct((n_rows, d), x.dtype),
        scratch_types=[pltpu.VMEM((n_ch, n_k, SC_CHUNK), I32),
                       pltpu.VMEM((SC_CHUNK, d), x.dtype),
                       pltpu.VMEM((SC_CHUNK, d), x.dtype)] + [pltpu.SemaphoreType.DMA] * 4,
        name="moe_dispatch_sc",
    )
    def scatter_kernel(x_hbm, dest_hbm, out_hbm, idx_v, rows0, rows1, l0, l1, s0, s1):
        wid = _sc_worker_id()
        base = wid * per_w
        pltpu.sync_copy(dest_hbm.at[wid], idx_v)
        bufs = ((rows0, l0, s0), (rows1, l1, s1))

        def load(ci, b):
            rows, load_sem, _ = bufs[b]
            return pltpu.make_async_copy(x_hbm.at[pl.ds(base + ci * SC_CHUNK, SC_CHUNK)], rows,
                                         load_sem)

        def scatters(ci, b):
            rows, _, scatter_sem = bufs[b]
            return [pltpu.make_async_copy(rows, out_hbm.at[idx_v.at[ci, j]], scatter_sem)
                    for j in range(n_k)]

        load(0, 0).start()

        @pl.loop(0, n_ch, step=2)
        def _(ci):
            load(ci, 0).wait()
            for cp in scatters(ci, 0):
                cp.start()

            @pl.when(ci > 0)
            def _():
                for cp in scatters(ci - 1, 1):
                    cp.wait()

            load(ci + 1, 1).start()
            load(ci + 1, 1).wait()
            for cp in scatters(ci + 1, 1):
                cp.start()
            for cp in scatters(ci, 0):
                cp.wait()

            @pl.when(ci + 2 < n_ch)
            def _():
                load(ci + 2, 0).start()

        for cp in scatters(n_ch - 1, 1):
            cp.wait()

    return scatter_kernel(x, dest4)


def _sc_gather_rows(table, idx):
    n, d = idx.shape[0], table.shape[1]
    assert n % (SC_WORKERS * 2 * SC_CHUNK) == 0
    per_w = n // SC_WORKERS
    n_ch = per_w // SC_CHUNK
    idx3 = idx.reshape(SC_WORKERS, n_ch, SC_CHUNK)
    mesh = plsc.VectorSubcoreMesh(core_axis_name="c", subcore_axis_name="s")

    @functools.partial(
        pl.kernel, mesh=mesh,
        out_type=jax.ShapeDtypeStruct((n, d), table.dtype),
        scratch_types=[pltpu.VMEM((n_ch, SC_CHUNK), I32),
                       pltpu.VMEM((SC_CHUNK, d), table.dtype),
                       pltpu.VMEM((SC_CHUNK, d), table.dtype)] + [pltpu.SemaphoreType.DMA] * 4,
        name="moe_combine_sc",
    )
    def gather_kernel(table_hbm, idx_hbm, out_hbm, idx_v, rows0, rows1, g0, g1, w0, w1):
        wid = _sc_worker_id()
        base = wid * per_w
        pltpu.sync_copy(idx_hbm.at[wid], idx_v)
        bufs = ((rows0, g0, w0), (rows1, g1, w1))

        def gather(ci, b):
            rows, gather_sem, _ = bufs[b]
            return pltpu.make_async_copy(table_hbm.at[idx_v.at[ci]], rows, gather_sem)

        def write(ci, b):
            rows, _, write_sem = bufs[b]
            return pltpu.make_async_copy(rows, out_hbm.at[pl.ds(base + ci * SC_CHUNK, SC_CHUNK)],
                                         write_sem)

        gather(0, 0).start()

        @pl.loop(0, n_ch, step=2)
        def _(ci):
            gather(ci, 0).wait()
            write(ci, 0).start()

            @pl.when(ci > 0)
            def _():
                write(ci - 1, 1).wait()

            gather(ci + 1, 1).start()
            gather(ci + 1, 1).wait()
            write(ci + 1, 1).start()
            write(ci, 0).wait()

            @pl.when(ci + 2 < n_ch)
            def _():
                gather(ci + 2, 0).start()

        write(n_ch - 1, 1).wait()

    return gather_kernel(table, idx3)


def _expert_body(blk_e_ref, blk_first_ref, blk_valid_ref, x_ref, w1_ref, w3_ref, w2_ref, y_ref,
                 w1_scr, w3_scr, w2_scr):
    del blk_e_ref
    j = pl.program_id(0)
    valid = blk_valid_ref[j]

    @pl.when(blk_first_ref[j] == 1)
    def _():
        w1_scr[...] = w1_ref[0].astype(BF16)
        w3_scr[...] = w3_ref[0].astype(BF16)
        w2_scr[...] = w2_ref[0].astype(BF16)

    @pl.when(valid > 0)
    def _():
        half = x_ref.shape[1]
        rows = lax.broadcasted_iota(I32, x_ref.shape, 0)
        lo, hi = _unpack_bf16_pair(jnp.where(rows < valid, x_ref[...], 0))
        lo, hi = lo.astype(BF16), hi.astype(BF16)

        def proj(w_scr):
            return (jnp.dot(lo, w_scr[0:half, :], preferred_element_type=F32)
                    + jnp.dot(hi, w_scr[half:2 * half, :], preferred_element_type=F32))

        hid = _silu(proj(w1_scr)) * proj(w3_scr)
        y = jnp.dot(hid.astype(BF16), w2_scr[...], preferred_element_type=F32)
        y_ref[...] = _pack_bf16_pair(y[:, :half], y[:, half:])

    @pl.when(valid == 0)
    def _():
        y_ref[...] = jnp.zeros(y_ref.shape, U32)


def _experts(xg, blk_e, blk_first, blk_valid, w1, w3, w2):
    n_rows, half = xg.shape
    n_blocks = n_rows // EXPERT_ROWS
    d, de = w1.shape[-2:]
    grid_spec = pltpu.PrefetchScalarGridSpec(
        num_scalar_prefetch=3,
        grid=(n_blocks,),
        in_specs=[pl.BlockSpec((EXPERT_ROWS, half), lambda j, be, bf, bv: (j, 0)),
                  pl.BlockSpec((1, d, de), lambda j, be, bf, bv: (be[j], 0, 0)),
                  pl.BlockSpec((1, d, de), lambda j, be, bf, bv: (be[j], 0, 0)),
                  pl.BlockSpec((1, de, d), lambda j, be, bf, bv: (be[j], 0, 0))],
        out_specs=pl.BlockSpec((EXPERT_ROWS, half), lambda j, be, bf, bv: (j, 0)),
        scratch_shapes=[pltpu.VMEM((d, de), BF16), pltpu.VMEM((d, de), BF16),
                        pltpu.VMEM((de, d), BF16)],
    )
    return pl.pallas_call(
        _expert_body,
        grid_spec=grid_spec,
        out_shape=jax.ShapeDtypeStruct((n_rows, half), U32),
        compiler_params=_cparams(("arbitrary",)),
        name="moe_experts",
    )(blk_e, blk_first, blk_valid, xg, w1, w3, w2)


def _combine_body(h_ref, u_ref, yg_ref, wts_ref, gt_ref, s1_ref, s3_ref, s2_ref, fw_ref, o_ref, *,
                  final_norm):
    half = u_ref.shape[1]
    lo, hi = _unpack_bf16_pair(u_ref[...])
    lo, hi = lo.astype(BF16), hi.astype(BF16)

    def proj(w_ref):
        return (jnp.dot(lo, w_ref[0:half, :], preferred_element_type=F32)
                + jnp.dot(hi, w_ref[half:2 * half, :], preferred_element_type=F32))

    hid = _silu(proj(s1_ref)) * proj(s3_ref)
    shared = jnp.dot(hid.astype(BF16), s2_ref[...], preferred_element_type=F32)

    routed_lo = routed_hi = None
    for kk in range(TOP_K):
        y_lo, y_hi = _unpack_bf16_pair(yg_ref[kk])
        w = wts_ref[:, kk:kk + 1]
        routed_lo = y_lo * w if routed_lo is None else routed_lo + y_lo * w
        routed_hi = y_hi * w if routed_hi is None else routed_hi + y_hi * w
    gt = gt_ref[0]
    out_lo = h_ref[:, 0:half] + gt[:, 0:half] * (routed_lo + shared[:, 0:half])
    out_hi = (h_ref[:, half:2 * half]
              + gt[:, half:2 * half] * (routed_hi + shared[:, half:2 * half]))
    if final_norm:
        ssq = (jnp.sum(out_lo * out_lo, axis=-1, keepdims=True)
               + jnp.sum(out_hi * out_hi, axis=-1, keepdims=True))
        inv = lax.rsqrt(ssq / (2 * half) + EPS)
        out_lo = (out_lo * inv) * fw_ref[:, 0:half]
        out_hi = (out_hi * inv) * fw_ref[:, half:2 * half]
    o_ref[:, 0:half] = out_lo
    o_ref[:, half:2 * half] = out_hi


def _combine(h, u, yg, wts_tk, gt, s1, s3, s2, final_w, seq, final_norm):
    t, d = h.shape
    half = d // 2
    tm = min(COMBINE_ROWS, seq)
    per_b = seq // tm
    ds_ = s1.shape[-1]
    return pl.pallas_call(
        functools.partial(_combine_body, final_norm=final_norm),
        grid=(t // tm,),
        in_specs=[pl.BlockSpec((tm, d), lambda i: (i, 0)),
                  pl.BlockSpec((tm, half), lambda i: (i, 0)),
                  pl.BlockSpec((TOP_K, tm, half), lambda i: (0, i, 0)),
                  pl.BlockSpec((tm, TOP_K), lambda i: (i, 0)),
                  pl.BlockSpec((1, 1, d), lambda i: (i // per_b, 0, 0)),
                  pl.BlockSpec((d, ds_), lambda i: (0, 0)),
                  pl.BlockSpec((d, ds_), lambda i: (0, 0)),
                  pl.BlockSpec((ds_, d), lambda i: (0, 0)),
                  pl.BlockSpec((1, d), lambda i: (0, 0))],
        out_specs=pl.BlockSpec((tm, d), lambda i: (i, 0)),
        out_shape=jax.ShapeDtypeStruct((t, d), F32),
        compiler_params=_cparams(("parallel",)),
        name="moe_combine",
    )(h, u, yg, wts_tk, gt, s1, s3, s2, final_w)


def _split_w_in(w_in):
    sizes = (2 * M_HEADS * M_DQK, M_HEADS * M_DV, M_HEADS * M_DV, M_HEADS, M_HEADS,
             G_HEADS * G_DK, G_HEADS * G_DK, G_HEADS * G_DV, G_RANK, G_HEADS * G_DV,
             D_MODEL, D_MODEL)
    parts, off = [], 0
    for n in sizes:
        parts.append(w_in[:, off:off + n])
        off += n
    qk_m, v_m, o_m, i_m, f_m, q_g, k_g, v_g, r_g, z_g, g_a, g_b = parts
    big = jnp.concatenate([qk_m, v_m, o_m, q_g, k_g, v_g, z_g, g_a, g_b], axis=1).astype(BF16)
    pad = jnp.zeros((w_in.shape[0], SMALL_COLS - 2 * M_HEADS - G_RANK), w_in.dtype)
    small = jnp.concatenate([i_m, f_m, r_g, pad], axis=1).astype(BF16)
    return big, small


def _moe_layout(counts, eidx, pos, n_blocks):
    padded = (counts + EXPERT_ROWS - 1) // EXPERT_ROWS * EXPERT_ROWS
    pend = jnp.cumsum(padded)
    pstart = pend - padded
    experts = jnp.arange(N_EXPERTS, dtype=I32)
    dest = pos + jnp.sum(jnp.where(eidx[..., None] == experts, pstart, 0), axis=-1)
    blk_start = jnp.arange(n_blocks, dtype=I32) * EXPERT_ROWS
    owner = jnp.sum((pend[None, :] <= blk_start[:, None]).astype(I32), axis=1)
    blk_e = jnp.minimum(owner, N_EXPERTS - 1)
    prev = jnp.concatenate([jnp.full((1,), -1, I32), blk_e[:-1]])
    blk_first = (blk_e != prev).astype(I32)
    used = jnp.where(owner < N_EXPERTS, counts[blk_e] - (blk_start - pstart[blk_e]), 0)
    blk_valid = jnp.clip(used, 0, EXPERT_ROWS)
    return dest.astype(I32), blk_e.astype(I32), blk_first, blk_valid.astype(I32)


def kernel(x, c, ada_w, ada_b, norm_mix_w, norm_moe_w, w_in, m_conv_w, m_gate_b, m_norm_w,
           g_alpha_w, g_alpha_b, g_norm_w, w_pa, w_pb, w_o, router_w, router_b,
           exp_w1, exp_w3, exp_w2, sh_w1, sh_w3, sh_w2, final_norm_w):
    bsz, seq, d = x.shape
    depth = ada_w.shape[0]
    t = bsz * seq
    n_rows = t * TOP_K + N_EXPERTS * EXPERT_ROWS
    n_blocks = n_rows // EXPERT_ROWS

    ada = _ada(c, ada_w, ada_b).reshape(depth, bsz, 6, 1, d)
    h = x.reshape(t, d)
    for l in range(depth):
        sh1, sc1, gt1, sh2, sc2, gt2 = (ada[l, :, i] for i in range(6))

        w_big, w_small = _split_w_in(w_in[l])
        big, small = _inproj(h, sh1, sc1, norm_mix_w[l][None, :], w_big, w_small, seq)
        gate_row = jnp.zeros((1, SMALL_COLS), F32)
        gate_row = gate_row.at[0, SMALL_I:SMALL_I + M_HEADS].set(m_gate_b[l, 0])
        gate_row = gate_row.at[0, SMALL_F:SMALL_F + M_HEADS].set(m_gate_b[l, 1])
        ha = _mlstm(big, small, m_conv_w[l], gate_row, m_norm_w[l][None, :], bsz, seq)
        alpha_full = jnp.zeros((SMALL_COLS, G_HEADS * G_DK), F32)
        alpha_full = alpha_full.at[SMALL_R:SMALL_R + G_RANK].set(g_alpha_w[l]).astype(BF16)
        hb = _gla(big, small, alpha_full, g_alpha_b[l][None, :], g_norm_w[l][None, :], bsz, seq)
        h = _mixout(h, ha, hb, big, gt1, w_pa[l].astype(BF16), w_pb[l].astype(BF16),
                    w_o[l].astype(BF16), seq)

        u, eidx, wts, pos, cnt = _route(h, sh2, sc2, norm_moe_w[l][None, :],
                                        router_w[l].T, router_b[l][:, None], seq)
        dest, blk_e, blk_first, blk_valid = _moe_layout(cnt[:, 0], eidx, pos, n_blocks)
        xg = _sc_scatter_rows(u, dest, n_rows)
        y = _experts(xg, blk_e, blk_first, blk_valid, exp_w1[l], exp_w3[l], exp_w2[l])
        yg = _sc_gather_rows(y, dest.reshape(-1)).reshape(TOP_K, t, d // 2)
        h = _combine(h, u, yg, wts.T, gt2, sh_w1[l].astype(BF16), sh_w3[l].astype(BF16),
                     sh_w2[l].astype(BF16), final_norm_w[None, :], seq,
                     final_norm=(l == depth - 1))

    return h.reshape(bsz, seq, d)
```

```python
import functools

import jax
import jax.numpy as jnp
import numpy as np
from jax import lax
from jax.experimental import pallas as pl
from jax.experimental.pallas import tpu as pltpu
from jax.experimental.pallas import tpu_sc as plsc

F32 = jnp.float32
BF16 = jnp.bfloat16
I32 = jnp.int32
U32 = jnp.uint32
HI_MASK = np.uint32(0xFFFF0000)
HIGHEST = lax.Precision.HIGHEST

SC_CORES = 2
SC_SUBCORES = 16
SC_WORKERS = SC_CORES * SC_SUBCORES
SC_CHUNK = 64

D_MODEL = 1024
M_HEADS = 4
M_DQK = 128
M_DV = 256
M_CONV = 4
GATE_CAP = 15.0
G_HEADS = 4
G_DK = 128
G_DV = 256
G_RANK = 16
G_TAU = 16.0
G_CHUNK = 64
N_EXPERTS = 64
TOP_K = 8
N_GROUPS = 8
GROUP_SIZE = N_EXPERTS // N_GROUPS
TOPK_GROUPS = 4
D_EXPERT = 256
D_SHARED = 256
ROUTED_SCALE = 2.5
EPS = 1e-6

M_CHUNK_ROWS = 256
G_BLOCK_ROWS = 256
PROJ_ROWS = 1024
MIX_ROWS = 512
ROUTE_ROWS = 512
EXPERT_ROWS = 512
COMBINE_ROWS = 512
CONV_HALO = 8
VMEM_LIMIT = 48 * 1024 * 1024

COL_QM, COL_KM, COL_VM, COL_OM = 0, 512, 1024, 2048
COL_QG, COL_KG, COL_VG, COL_ZG, COL_GA, COL_GB = 3072, 3584, 4096, 5120, 6144, 7168
BIG_COLS = 8192
SMALL_COLS = 128
SMALL_I, SMALL_F, SMALL_R = 0, M_HEADS, 2 * M_HEADS


def _cparams(sem, vmem=VMEM_LIMIT):
    return pltpu.CompilerParams(dimension_semantics=sem, vmem_limit_bytes=vmem)


def _silu(x):
    return x * jax.nn.sigmoid(x)


def _log_sigmoid(x):
    return jnp.minimum(x, 0.0) - jnp.log1p(jnp.exp(-jnp.abs(x)))


def _modulated_rmsnorm(x, w, sc, sh):
    y = x * lax.rsqrt(jnp.mean(x * x, axis=-1, keepdims=True) + EPS)
    return (y * w) * (1.0 + sc) + sh


def _pack_bf16_pair(lo, hi):
    lo_bits = lax.bitcast_convert_type(lo.astype(BF16).astype(F32), U32)
    hi_bits = lax.bitcast_convert_type(hi.astype(BF16).astype(F32), U32)
    return (lo_bits >> 16) | (hi_bits & HI_MASK)


def _unpack_bf16_pair(packed):
    lo = lax.bitcast_convert_type(packed << 16, F32)
    hi = lax.bitcast_convert_type(packed & HI_MASK, F32)
    return lo, hi


def _lower_tri(n, dtype):
    r = lax.broadcasted_iota(I32, (n, n), 0)
    c = lax.broadcasted_iota(I32, (n, n), 1)
    return (r >= c).astype(dtype)


def _ada_body(c_ref, w_ref, b_ref, o_ref):
    cond = _silu(c_ref[...])
    o_ref[0] = jnp.dot(cond.astype(BF16), w_ref[0].astype(BF16),
                       preferred_element_type=F32) + b_ref[0]


def _ada(c, ada_w, ada_b):
    depth, d, six_d = ada_w.shape
    bsz = c.shape[0]
    nj = six_d // d
    return pl.pallas_call(
        _ada_body,
        grid=(depth, nj),
        in_specs=[pl.BlockSpec((bsz, d), lambda l, j: (0, 0)),
                  pl.BlockSpec((1, d, d), lambda l, j: (l, 0, j)),
                  pl.BlockSpec((1, 1, d), lambda l, j: (l, 0, j))],
        out_specs=pl.BlockSpec((1, bsz, d), lambda l, j: (l, 0, j)),
        out_shape=jax.ShapeDtypeStruct((depth, bsz, six_d), F32),
        compiler_params=_cparams(("parallel", "parallel")),
        name="ada_ln",
    )(c, ada_w, ada_b.reshape(depth, 1, six_d))


def _inproj_body(h_ref, sh_ref, sc_ref, nw_ref, wb_ref, ws_ref, big_ref, small_ref, u_scr):
    @pl.when(pl.program_id(1) == 0)
    def _():
        u = _modulated_rmsnorm(h_ref[...], nw_ref[...], sc_ref[0], sh_ref[0])
        ub = u.astype(BF16)
        u_scr[...] = ub
        small_ref[...] = jnp.dot(ub, ws_ref[...], preferred_element_type=F32)

    big_ref[...] = jnp.dot(u_scr[...], wb_ref[...], preferred_element_type=F32).astype(BF16)


def _inproj(h, sh, sc, nw, w_big, w_small, seq):
    t, d = h.shape
    tm = min(PROJ_ROWS, seq)
    per_b = seq // tm
    tn = 2048
    return pl.pallas_call(
        _inproj_body,
        grid=(t // tm, BIG_COLS // tn),
        in_specs=[pl.BlockSpec((tm, d), lambda i, j: (i, 0)),
                  pl.BlockSpec((1, 1, d), lambda i, j: (i // per_b, 0, 0)),
                  pl.BlockSpec((1, 1, d), lambda i, j: (i // per_b, 0, 0)),
                  pl.BlockSpec((1, d), lambda i, j: (0, 0)),
                  pl.BlockSpec((d, tn), lambda i, j: (0, j)),
                  pl.BlockSpec((d, SMALL_COLS), lambda i, j: (0, 0))],
        out_specs=[pl.BlockSpec((tm, tn), lambda i, j: (i, j)),
                   pl.BlockSpec((tm, SMALL_COLS), lambda i, j: (i, 0))],
        out_shape=[jax.ShapeDtypeStruct((t, BIG_COLS), BF16),
                   jax.ShapeDtypeStruct((t, SMALL_COLS), F32)],
        scratch_shapes=[pltpu.VMEM((tm, d), BF16)],
        compiler_params=_cparams(("parallel", "arbitrary")),
        name="in_proj",
    )(h, sh, sc, nw, w_big, w_small)


def _mlstm_body(q_ref, k_ref, v_ref, o_ref, small_ref, cw_ref, gb_ref, nw_ref, ha_ref,
                xe_scr, c_scr, n_scr, m_scr):
    rows = q_ref.shape[0]
    half = M_HEADS * M_DQK

    @pl.when(pl.program_id(1) == 0)
    def _():
        xe_scr[0:CONV_HALO, :] = jnp.zeros((CONV_HALO, 2 * half), F32)
        c_scr[...] = jnp.zeros(c_scr.shape, F32)
        n_scr[...] = jnp.zeros(n_scr.shape, F32)
        m_scr[...] = jnp.zeros(m_scr.shape, F32)

    xe_scr[CONV_HALO:CONV_HALO + rows, 0:half] = q_ref[...].astype(F32)
    xe_scr[CONV_HALO:CONV_HALO + rows, half:2 * half] = k_ref[...].astype(F32)
    conv = None
    for j in range(M_CONV):
        off = CONV_HALO - (M_CONV - 1) + j
        term = xe_scr[off:off + rows, :] * cw_ref[j:j + 1, :]
        conv = term if conv is None else conv + term
    qk = _silu(conv)
    xe_scr[0:CONV_HALO, :] = xe_scr[rows:rows + CONV_HALO, :]

    capped = GATE_CAP * jnp.tanh((small_ref[...] + gb_ref[...]) / GATE_CAP)
    li_all = capped
    lf_all = _log_sigmoid(capped)
    b_all = _split3_dot(_lower_tri(rows, BF16), lf_all)
    li_t = li_all.T
    b_t = b_all.T

    r_io = lax.broadcasted_iota(I32, (rows, rows), 0)
    c_io = lax.broadcasted_iota(I32, (rows, rows), 1)
    causal = r_io >= c_io

    for h in range(M_HEADS):
        q = (qk[:, h * M_DQK:(h + 1) * M_DQK] * (M_DQK ** -0.5)).astype(BF16)
        k = qk[:, half + h * M_DQK:half + (h + 1) * M_DQK]
        kb = k.astype(BF16)
        v = v_ref[:, h * M_DV:(h + 1) * M_DV]
        li_c = li_all[:, SMALL_I + h:SMALL_I + h + 1]
        b_c = b_all[:, SMALL_F + h:SMALL_F + h + 1]
        li_r = li_t[SMALL_I + h:SMALL_I + h + 1, :]
        b_r = b_t[SMALL_F + h:SMALL_F + h + 1, :]
        g = b_c[rows - 1:rows, :]
        m_prev = m_scr[h][:, 0:1]
        c_prev = c_scr[h]
        n_prev = n_scr[h]

        d_mat = jnp.where(causal, b_c - b_r + li_r, -jnp.inf)
        m_inter = b_c + m_prev
        m_t = jnp.maximum(jnp.max(d_mat, axis=1, keepdims=True), m_inter)
        s = lax.dot_general(q, kb, (((1,), (1,)), ((), ())), preferred_element_type=F32)
        p = jnp.exp(d_mat - m_t) * s
        w_inter = jnp.exp(m_inter - m_t)
        num = (jnp.dot(p.astype(BF16), v, preferred_element_type=F32)
               + w_inter * jnp.dot(q, c_prev.astype(BF16), preferred_element_type=F32))
        qn = jnp.sum(q.astype(F32) * n_prev, axis=1, keepdims=True)
        den = jnp.sum(p, axis=1, keepdims=True) + w_inter * qn
        hh = num / jnp.maximum(jnp.abs(den), jnp.exp(-m_t))

        a_r = g - b_r + li_r
        a_c = g - b_c + li_c
        m_new = jnp.maximum(g + m_prev, jnp.max(a_r, axis=1, keepdims=True))
        decay = jnp.exp(g + m_prev - m_new)
        wk = jnp.exp(a_c - m_new) * k
        c_scr[h] = decay * c_prev + lax.dot_general(
            wk.astype(BF16), v, (((0,), (0,)), ((), ())), preferred_element_type=F32)
        n_scr[h] = decay * n_prev + jnp.sum(wk, axis=0, keepdims=True)
        m_scr[h] = jnp.broadcast_to(m_new, m_scr.shape[1:])

        y = hh * lax.rsqrt(jnp.mean(hh * hh, axis=-1, keepdims=True) + EPS)
        y = y * nw_ref[:, h * M_DV:(h + 1) * M_DV]
        gate = jax.nn.sigmoid(o_ref[:, h * M_DV:(h + 1) * M_DV].astype(F32))
        ha_ref[:, h * M_DV:(h + 1) * M_DV] = (y * gate).astype(BF16)


def _mlstm(big, small, conv_w, gate_row, norm_w, bsz, seq):
    t = big.shape[0]
    rows = min(M_CHUNK_ROWS, seq)
    nc = seq // rows
    half = M_HEADS * M_DQK
    vcols = M_HEADS * M_DV
    row_map = lambda b, c: b * nc + c
    return pl.pallas_call(
        _mlstm_body,
        grid=(bsz, nc),
        in_specs=[pl.BlockSpec((rows, half), lambda b, c: (row_map(b, c), COL_QM // half)),
                  pl.BlockSpec((rows, half), lambda b, c: (row_map(b, c), COL_KM // half)),
                  pl.BlockSpec((rows, vcols), lambda b, c: (row_map(b, c), COL_VM // vcols)),
                  pl.BlockSpec((rows, vcols), lambda b, c: (row_map(b, c), COL_OM // vcols)),
                  pl.BlockSpec((rows, SMALL_COLS), lambda b, c: (row_map(b, c), 0)),
                  pl.BlockSpec((M_CONV, 2 * half), lambda b, c: (0, 0)),
                  pl.BlockSpec((1, SMALL_COLS), lambda b, c: (0, 0)),
                  pl.BlockSpec((1, vcols), lambda b, c: (0, 0))],
        out_specs=pl.BlockSpec((rows, vcols), lambda b, c: (row_map(b, c), 0)),
        out_shape=jax.ShapeDtypeStruct((t, vcols), BF16),
        scratch_shapes=[pltpu.VMEM((rows + CONV_HALO, 2 * half), F32),
                        pltpu.VMEM((M_HEADS, M_DQK, M_DV), F32),
                        pltpu.VMEM((M_HEADS, 1, M_DQK), F32),
                        pltpu.VMEM((M_HEADS, 1, 128), F32)],
        compiler_params=_cparams(("parallel", "arbitrary")),
        name="mlstm",
    )(big, big, big, big, small, conv_w, gate_row, norm_w)


def _split3_dot(lhs01, x):
    hi = x.astype(BF16)
    r1 = x - hi.astype(F32)
    mid = r1.astype(BF16)
    lo = (r1 - mid.astype(F32)).astype(BF16)
    return (jnp.dot(lhs01, hi, preferred_element_type=F32)
            + jnp.dot(lhs01, mid, preferred_element_type=F32)
            + jnp.dot(lhs01, lo, preferred_element_type=F32))


def _gla_body(q_ref, k_ref, v_ref, z_ref, small_ref, aw_ref, ab_ref, nw_ref, hb_ref, st_scr):
    rows = q_ref.shape[0]
    n_chunks = rows // G_CHUNK
    kcols = G_HEADS * G_DK

    @pl.when(pl.program_id(1) == 0)
    def _():
        st_scr[...] = jnp.zeros(st_scr.shape, F32)

    logits = jnp.dot(small_ref[...].astype(BF16), aw_ref[...], preferred_element_type=F32)
    la = _log_sigmoid(logits + ab_ref[...]) / G_TAU
    r_io = lax.broadcasted_iota(I32, (rows, rows), 0)
    c_io = lax.broadcasted_iota(I32, (rows, rows), 1)
    chunk_causal = jnp.logical_and(r_io >= c_io, r_io // G_CHUNK == c_io // G_CHUNK)
    bc = _split3_dot(jnp.where(chunk_causal, 1.0, 0.0).astype(BF16), la)
    gcs = [bc[(ci + 1) * G_CHUNK - 1:(ci + 1) * G_CHUNK, :] for ci in range(n_chunks)]
    gc_rows = jnp.concatenate([jnp.broadcast_to(g, (G_CHUNK, kcols)) for g in gcs], axis=0)

    q = q_ref[...].astype(F32) * (G_DK ** -0.5)
    k = k_ref[...].astype(F32)
    q_in = (q * jnp.exp(bc)).astype(BF16)
    k_in = (k * jnp.exp(-bc)).astype(BF16)
    k_out = (k * jnp.exp(gc_rows - bc)).astype(BF16)

    for h in range(G_HEADS):
        ks = slice(h * G_DK, (h + 1) * G_DK)
        vs = slice(h * G_DV, (h + 1) * G_DV)
        v = v_ref[:, vs]
        att = lax.dot_general(q_in[:, ks], k_in[:, ks], (((1,), (1,)), ((), ())),
                              preferred_element_type=F32)
        att = jnp.where(chunk_causal, att, 0.0).astype(BF16)
        o_intra = jnp.dot(att, v, preferred_element_type=F32)
        st = st_scr[h]
        outs = []
        for ci in range(n_chunks):
            rs = slice(ci * G_CHUNK, (ci + 1) * G_CHUNK)
            o_inter = lax.dot_general(q_in[rs, ks], st.astype(BF16), (((1,), (1,)), ((), ())),
                                      preferred_element_type=F32)
            outs.append(o_intra[rs, :] + o_inter)
            st = jnp.exp(gcs[ci][:, ks]) * st + lax.dot_general(
                v[rs, :], k_out[rs, ks], (((0,), (0,)), ((), ())), preferred_element_type=F32)
        st_scr[h] = st
        o = jnp.concatenate(outs, axis=0)
        y = o * lax.rsqrt(jnp.mean(o * o, axis=-1, keepdims=True) + EPS)
        y = y * nw_ref[:, vs]
        hb_ref[:, vs] = (y * _silu(z_ref[:, vs].astype(F32))).astype(BF16)


def _gla(big, small, alpha_full, alpha_b, norm_w, bsz, seq):
    t = big.shape[0]
    rows = min(G_BLOCK_ROWS, seq)
    nb = seq // rows
    kcols = G_HEADS * G_DK
    vcols = G_HEADS * G_DV
    row_map = lambda b, c: b * nb + c
    return pl.pallas_call(
        _gla_body,
        grid=(bsz, nb),
        in_specs=[pl.BlockSpec((rows, kcols), lambda b, c: (row_map(b, c), COL_QG // kcols)),
                  pl.BlockSpec((rows, kcols), lambda b, c: (row_map(b, c), COL_KG // kcols)),
                  pl.BlockSpec((rows, vcols), lambda b, c: (row_map(b, c), COL_VG // vcols)),
                  pl.BlockSpec((rows, vcols), lambda b, c: (row_map(b, c), COL_ZG // vcols)),
                  pl.BlockSpec((rows, SMALL_COLS), lambda b, c: (row_map(b, c), 0)),
                  pl.BlockSpec((SMALL_COLS, kcols), lambda b, c: (0, 0)),
                  pl.BlockSpec((1, kcols), lambda b, c: (0, 0)),
                  pl.BlockSpec((1, vcols), lambda b, c: (0, 0))],
        out_specs=pl.BlockSpec((rows, vcols), lambda b, c: (row_map(b, c), 0)),
        out_shape=jax.ShapeDtypeStruct((t, vcols), BF16),
        scratch_shapes=[pltpu.VMEM((G_HEADS, G_DV, G_DK), F32)],
        compiler_params=_cparams(("parallel", "arbitrary")),
        name="gla",
    )(big, big, big, big, small, alpha_full, alpha_b, norm_w)


def _mixout_body(h_ref, ha_ref, hb_ref, ga_ref, gb_ref, gt_ref, wpa_ref, wpb_ref, wo_ref, o_ref):
    a = jnp.dot(ha_ref[...], wpa_ref[...], preferred_element_type=F32)
    b = jnp.dot(hb_ref[...], wpb_ref[...], preferred_element_type=F32)
    y = (jax.nn.sigmoid(ga_ref[...].astype(F32)) * a
         + jax.nn.sigmoid(gb_ref[...].astype(F32)) * b)
    o_ref[...] = h_ref[...] + gt_ref[0] * jnp.dot(y.astype(BF16), wo_ref[...],
                                                  preferred_element_type=F32)


def _mixout(h, ha, hb, big, gt, w_pa, w_pb, w_o, seq):
    t, d = h.shape
    tm = min(MIX_ROWS, seq)
    per_b = seq // tm
    wspec = pl.BlockSpec((d, d), lambda i: (0, 0))
    return pl.pallas_call(
        _mixout_body,
        grid=(t // tm,),
        in_specs=[pl.BlockSpec((tm, d), lambda i: (i, 0)),
                  pl.BlockSpec((tm, d), lambda i: (i, 0)),
                  pl.BlockSpec((tm, d), lambda i: (i, 0)),
                  pl.BlockSpec((tm, d), lambda i: (i, COL_GA // d)),
                  pl.BlockSpec((tm, d), lambda i: (i, COL_GB // d)),
                  pl.BlockSpec((1, 1, d), lambda i: (i // per_b, 0, 0)),
                  wspec, wspec, wspec],
        out_specs=pl.BlockSpec((tm, d), lambda i: (i, 0)),
        out_shape=jax.ShapeDtypeStruct((t, d), F32),
        compiler_params=_cparams(("parallel",)),
        name="mix_out",
    )(h, ha, hb, big, big, gt, w_pa, w_pb, w_o)


def _route_body(h_ref, sh_ref, sc_ref, nw_ref, rwt_ref, rb_ref,
                u_ref, eidx_ref, wts_ref, pos_ref, cnt_ref, carry_scr):
    tm = h_ref.shape[0]

    @pl.when(pl.program_id(0) == 0)
    def _():
        carry_scr[...] = jnp.zeros(carry_scr.shape, F32)

    u = _modulated_rmsnorm(h_ref[...], nw_ref[...], sc_ref[0], sh_ref[0])
    half = u.shape[1] // 2
    u_ref[...] = _pack_bf16_pair(u[:, :half], u[:, half:])
    logits = lax.dot_general(rwt_ref[...], u, (((1,), (1,)), ((), ())),
                             precision=HIGHEST, preferred_element_type=F32)
    scores = jax.nn.sigmoid(logits)
    sel = scores + rb_ref[...]

    neg = -jnp.inf
    sub_io = lax.broadcasted_iota(I32, (GROUP_SIZE, tm), 0)
    pieces = []
    for g in range(N_GROUPS):
        blk = sel[g * GROUP_SIZE:(g + 1) * GROUP_SIZE, :]
        m1 = jnp.max(blk, axis=0, keepdims=True)
        first = jnp.min(jnp.where(blk == m1, sub_io, GROUP_SIZE), axis=0, keepdims=True)
        m2 = jnp.max(jnp.where(sub_io == first, neg, blk), axis=0, keepdims=True)
        pieces.append(jnp.broadcast_to(m1 + m2, (GROUP_SIZE, tm)))
    gscore = jnp.concatenate(pieces, axis=0)

    e_io = lax.broadcasted_iota(I32, (N_EXPERTS, tm), 0)
    grp_io = e_io // GROUP_SIZE
    gmask = jnp.zeros((N_EXPERTS, tm), jnp.bool_)
    for _ in range(TOPK_GROUPS):
        mx = jnp.max(gscore, axis=0, keepdims=True)
        gi = jnp.min(jnp.where(gscore == mx, grp_io, N_GROUPS), axis=0, keepdims=True)
        hit = grp_io == gi
        gmask = jnp.logical_or(gmask, hit)
        gscore = jnp.where(hit, neg, gscore)

    cur = jnp.where(gmask, sel, neg)
    row_io = lax.broadcasted_iota(I32, (TOP_K, tm), 0)
    eidx = jnp.zeros((TOP_K, tm), I32)
    wraw = jnp.zeros((TOP_K, tm), F32)
    chosen = jnp.zeros((N_EXPERTS, tm), jnp.bool_)
    hits = []
    for kk in range(TOP_K):
        mx = jnp.max(cur, axis=0, keepdims=True)
        ei = jnp.min(jnp.where(cur == mx, e_io, N_EXPERTS), axis=0, keepdims=True)
        hit = e_io == ei
        hits.append(hit)
        sc_k = jnp.sum(jnp.where(hit, scores, 0.0), axis=0, keepdims=True)
        eidx = jnp.where(row_io == kk, ei, eidx)
        wraw = jnp.where(row_io == kk, sc_k, wraw)
        chosen = jnp.logical_or(chosen, hit)
        cur = jnp.where(hit, neg, cur)

    wsum = jnp.sum(wraw, axis=0, keepdims=True)
    wts_ref[...] = wraw / wsum * ROUTED_SCALE
    eidx_ref[...] = eidx

    chosen_f = jnp.where(chosen, 1.0, 0.0)
    r_io = lax.broadcasted_iota(I32, (tm, tm), 0)
    c_io = lax.broadcasted_iota(I32, (tm, tm), 1)
    strict_upper = jnp.where(r_io < c_io, 1.0, 0.0).astype(BF16)
    prefix = jnp.dot(chosen_f.astype(BF16), strict_upper, preferred_element_type=F32)
    rank = prefix + carry_scr[:, 0:1]
    pos = jnp.zeros((TOP_K, tm), F32)
    for kk in range(TOP_K):
        p_k = jnp.sum(jnp.where(hits[kk], rank, 0.0), axis=0, keepdims=True)
        pos = jnp.where(row_io == kk, p_k, pos)
    pos_ref[...] = pos.astype(I32)
    total = carry_scr[...] + jnp.sum(chosen_f, axis=1, keepdims=True)
    carry_scr[...] = total
    cnt_ref[...] = total.astype(I32)


def _route(h, sh, sc, nw, rw_t, rb_col, seq):
    t, d = h.shape
    tm = min(ROUTE_ROWS, seq)
    per_b = seq // tm
    kspec = pl.BlockSpec((TOP_K, tm), lambda i: (0, i))
    return pl.pallas_call(
        _route_body,
        grid=(t // tm,),
        in_specs=[pl.BlockSpec((tm, d), lambda i: (i, 0)),
                  pl.BlockSpec((1, 1, d), lambda i: (i // per_b, 0, 0)),
                  pl.BlockSpec((1, 1, d), lambda i: (i // per_b, 0, 0)),
                  pl.BlockSpec((1, d), lambda i: (0, 0)),
                  pl.BlockSpec((N_EXPERTS, d), lambda i: (0, 0)),
                  pl.BlockSpec((N_EXPERTS, 1), lambda i: (0, 0))],
        out_specs=[pl.BlockSpec((tm, d // 2), lambda i: (i, 0)), kspec, kspec, kspec,
                   pl.BlockSpec((N_EXPERTS, 128), lambda i: (0, 0))],
        out_shape=[jax.ShapeDtypeStruct((t, d // 2), U32),
                   jax.ShapeDtypeStruct((TOP_K, t), I32),
                   jax.ShapeDtypeStruct((TOP_K, t), F32),
                   jax.ShapeDtypeStruct((TOP_K, t), I32),
                   jax.ShapeDtypeStruct((N_EXPERTS, 128), I32)],
        scratch_shapes=[pltpu.VMEM((N_EXPERTS, 128), F32)],
        compiler_params=_cparams(("arbitrary",)),
        name="moe_route",
    )(h, sh, sc, nw, rw_t, rb_col)


def _sc_worker_id():
    return lax.axis_index("s") * SC_CORES + lax.axis_index("c")


def _sc_scatter_rows(x, dest, n_rows):
    t, d = x.shape
    n_k = dest.shape[0]
    assert t % (SC_WORKERS * 2 * SC_CHUNK) == 0
    per_w = t // SC_WORKERS
    n_ch = per_w // SC_CHUNK
    dest4 = dest.reshape(n_k, SC_WORKERS, n_ch, SC_CHUNK).transpose(1, 2, 0, 3)
    mesh = plsc.VectorSubcoreMesh(core_axis_name="c", subcore_axis_name="s")

    @functools.partial(
        pl.kernel, mesh=mesh,
        out_type=jax.ShapeDtypeStruct((n_rows, d), x.dtype),
        scratch_types=[pltpu.VMEM((n_ch, n_k, SC_CHUNK), I32),
                       pltpu.VMEM((SC_CHUNK, d), x.dtype),
                       pltpu.VMEM((SC_CHUNK, d), x.dtype)] + [pltpu.SemaphoreType.DMA] * 4,
        name="moe_dispatch_sc",
    )
    def scatter_kernel(x_hbm, dest_hbm, out_hbm, idx_v, rows0, rows1, l0, l1, s0, s1):
        wid = _sc_worker_id()
        base = wid * per_w
        pltpu.sync_copy(dest_hbm.at[wid], idx_v)
        bufs = ((rows0, l0, s0), (rows1, l1, s1))

        def load(ci, b):
            rows, load_sem, _ = bufs[b]
            return pltpu.make_async_copy(x_hbm.at[pl.ds(base + ci * SC_CHUNK, SC_CHUNK)], rows,
                                         load_sem)

        def scatters(ci, b):
            rows, _, scatter_sem = bufs[b]
            return [pltpu.make_async_copy(rows, out_hbm.at[idx_v.at[ci, j]], scatter_sem)
                    for j in range(n_k)]

        load(0, 0).start()

        @pl.loop(0, n_ch, step=2)
        def _(ci):
            load(ci, 0).wait()
            for cp in scatters(ci, 0):
                cp.start()

            @pl.when(ci > 0)
            def _():
                for cp in scatters(ci - 1, 1):
                    cp.wait()

            load(ci + 1, 1).start()
            load(ci + 1, 1).wait()
            for cp in scatters(ci + 1, 1):
                cp.start()
            for cp in scatters(ci, 0):
                cp.wait()

            @pl.when(ci + 2 < n_ch)
            def _():
                load(ci + 2, 0).start()

        for cp in scatters(n_ch - 1, 1):
            cp.wait()

    return scatter_kernel(x, dest4)


def _sc_gather_rows(table, idx):
    n, d = idx.shape[0], table.shape[1]
    assert n % (SC_WORKERS * 2 * SC_CHUNK) == 0
    per_w = n // SC_WORKERS
    n_ch = per_w // SC_CHUNK
    idx3 = idx.reshape(SC_WORKERS, n_ch, SC_CHUNK)
    mesh = plsc.VectorSubcoreMesh(core_axis_name="c", subcore_axis_name="s")

    @functools.partial(
        pl.kernel, mesh=mesh,
        out_type=jax.ShapeDtypeStruct((n, d), table.dtype),
        scratch_types=[pltpu.VMEM((n_ch, SC_CHUNK), I32),
                       pltpu.VMEM((SC_CHUNK, d), table.dtype),
                       pltpu.VMEM((SC_CHUNK, d), table.dtype)] + [pltpu.SemaphoreType.DMA] * 4,
        name="moe_combine_sc",
    )
    def gather_kernel(table_hbm, idx_hbm, out_hbm, idx_v, rows0, rows1, g0, g1, w0, w1):
        wid = _sc_worker_id()
        base = wid * per_w
        pltpu.sync_copy(idx_hbm.at[wid], idx_v)
        bufs = ((rows0, g0, w0), (rows1, g1, w1))

        def gather(ci, b):
            rows, gather_sem, _ = bufs[b]
            return pltpu.make_async_copy(table_hbm.at[idx_v.at[ci]], rows, gather_sem)

        def write(ci, b):
            rows, _, write_sem = bufs[b]
            return pltpu.make_async_copy(rows, out_hbm.at[pl.ds(base + ci * SC_CHUNK, SC_CHUNK)],
                                         write_sem)

        gather(0, 0).start()

        @pl.loop(0, n_ch, step=2)
        def _(ci):
            gather(ci, 0).wait()
            write(ci, 0).start()

            @pl.when(ci > 0)
            def _():
                write(ci - 1, 1).wait()

            gather(ci + 1, 1).start()
            gather(ci + 1, 1).wait()
            write(ci + 1, 1).start()
            write(ci, 0).wait()

            @pl.when(ci + 2 < n_ch)
            def _():
                gather(ci + 2, 0).start()

        write(n_ch - 1, 1).wait()

    return gather_kernel(table, idx3)


def _expert_body(blk_e_ref, blk_first_ref, blk_valid_ref, x_ref, w1_ref, w3_ref, w2_ref, y_ref,
                 w1_scr, w3_scr, w2_scr):
    del blk_e_ref
    j = pl.program_id(0)
    valid = blk_valid_ref[j]

    @pl.when(blk_first_ref[j] == 1)
    def _():
        w1_scr[...] = w1_ref[0].astype(BF16)
        w3_scr[...] = w3_ref[0].astype(BF16)
        w2_scr[...] = w2_ref[0].astype(BF16)

    @pl.when(valid > 0)
    def _():
        half = x_ref.shape[1]
        rows = lax.broadcasted_iota(I32, x_ref.shape, 0)
        lo, hi = _unpack_bf16_pair(jnp.where(rows < valid, x_ref[...], 0))
        lo, hi = lo.astype(BF16), hi.astype(BF16)

        def proj(w_scr):
            return (jnp.dot(lo, w_scr[0:half, :], preferred_element_type=F32)
                    + jnp.dot(hi, w_scr[half:2 * half, :], preferred_element_type=F32))

        hid = _silu(proj(w1_scr)) * proj(w3_scr)
        y = jnp.dot(hid.astype(BF16), w2_scr[...], preferred_element_type=F32)
        y_ref[...] = _pack_bf16_pair(y[:, :half], y[:, half:])

    @pl.when(valid == 0)
    def _():
        y_ref[...] = jnp.zeros(y_ref.shape, U32)


def _experts(xg, blk_e, blk_first, blk_valid, w1, w3, w2):
    n_rows, half = xg.shape
    n_blocks = n_rows // EXPERT_ROWS
    d, de = w1.shape[-2:]
    grid_spec = pltpu.PrefetchScalarGridSpec(
        num_scalar_prefetch=3,
        grid=(n_blocks,),
        in_specs=[pl.BlockSpec((EXPERT_ROWS, half), lambda j, be, bf, bv: (j, 0)),
                  pl.BlockSpec((1, d, de), lambda j, be, bf, bv: (be[j], 0, 0)),
                  pl.BlockSpec((1, d, de), lambda j, be, bf, bv: (be[j], 0, 0)),
                  pl.BlockSpec((1, de, d), lambda j, be, bf, bv: (be[j], 0, 0))],
        out_specs=pl.BlockSpec((EXPERT_ROWS, half), lambda j, be, bf, bv: (j, 0)),
        scratch_shapes=[pltpu.VMEM((d, de), BF16), pltpu.VMEM((d, de), BF16),
                        pltpu.VMEM((de, d), BF16)],
    )
    return pl.pallas_call(
        _expert_body,
        grid_spec=grid_spec,
        out_shape=jax.ShapeDtypeStruct((n_rows, half), U32),
        compiler_params=_cparams(("arbitrary",)),
        name="moe_experts",
    )(blk_e, blk_first, blk_valid, xg, w1, w3, w2)


def _combine_body(h_ref, u_ref, yg_ref, wts_ref, gt_ref, s1_ref, s3_ref, s2_ref, fw_ref, o_ref, *,
                  final_norm):
    half = u_ref.shape[1]
    lo, hi = _unpack_bf16_pair(u_ref[...])
    lo, hi = lo.astype(BF16), hi.astype(BF16)

    def proj(w_ref):
        return (jnp.dot(lo, w_ref[0:half, :], preferred_element_type=F32)
                + jnp.dot(hi, w_ref[half:2 * half, :], preferred_element_type=F32))

    hid = _silu(proj(s1_ref)) * proj(s3_ref)
    shared = jnp.dot(hid.astype(BF16), s2_ref[...], preferred_element_type=F32)

    routed_lo = routed_hi = None
    for kk in range(TOP_K):
        y_lo, y_hi = _unpack_bf16_pair(yg_ref[kk])
        w = wts_ref[:, kk:kk + 1]
        routed_lo = y_lo * w if routed_lo is None else routed_lo + y_lo * w
        routed_hi = y_hi * w if routed_hi is None else routed_hi + y_hi * w
    gt = gt_ref[0]
    out_lo = h_ref[:, 0:half] + gt[:, 0:half] * (routed_lo + shared[:, 0:half])
    out_hi = (h_ref[:, half:2 * half]
              + gt[:, half:2 * half] * (routed_hi + shared[:, half:2 * half]))
    if final_norm:
        ssq = (jnp.sum(out_lo * out_lo, axis=-1, keepdims=True)
               + jnp.sum(out_hi * out_hi, axis=-1, keepdims=True))
        inv = lax.rsqrt(ssq / (2 * half) + EPS)
        out_lo = (out_lo * inv) * fw_ref[:, 0:half]
        out_hi = (out_hi * inv) * fw_ref[:, half:2 * half]
    o_ref[:, 0:half] = out_lo
    o_ref[:, half:2 * half] = out_hi


def _combine(h, u, yg, wts_tk, gt, s1, s3, s2, final_w, seq, final_norm):
    t, d = h.shape
    half = d // 2
    tm = min(COMBINE_ROWS, seq)
    per_b = seq // tm
    ds_ = s1.shape[-1]
    return pl.pallas_call(
        functools.partial(_combine_body, final_norm=final_norm),
        grid=(t // tm,),
        in_specs=[pl.BlockSpec((tm, d), lambda i: (i, 0)),
                  pl.BlockSpec((tm, half), lambda i: (i, 0)),
                  pl.BlockSpec((TOP_K, tm, half), lambda i: (0, i, 0)),
                  pl.BlockSpec((tm, TOP_K), lambda i: (i, 0)),
                  pl.BlockSpec((1, 1, d), lambda i: (i // per_b, 0, 0)),
                  pl.BlockSpec((d, ds_), lambda i: (0, 0)),
                  pl.BlockSpec((d, ds_), lambda i: (0, 0)),
                  pl.BlockSpec((ds_, d), lambda i: (0, 0)),
                  pl.BlockSpec((1, d), lambda i: (0, 0))],
        out_specs=pl.BlockSpec((tm, d), lambda i: (i, 0)),
        out_shape=jax.ShapeDtypeStruct((t, d), F32),
        compiler_params=_cparams(("parallel",)),
        name="moe_combine",
    )(h, u, yg, wts_tk, gt, s1, s3, s2, final_w)


def _split_w_in(w_in):
    sizes = (2 * M_HEADS * M_DQK, M_HEADS * M_DV, M_HEADS * M_DV, M_HEADS, M_HEADS,
             G_HEADS * G_DK, G_HEADS * G_DK, G_HEADS * G_DV, G_RANK, G_HEADS * G_DV,
             D_MODEL, D_MODEL)
    offs = [0]
    for n in sizes:
        offs.append(offs[-1] + n)
    w16 = w_in.astype(BF16)
    big = jnp.concatenate([w16[:, offs[0]:offs[3]], w16[:, offs[5]:offs[8]], w16[:, offs[9]:offs[12]]],
                          axis=1)
    pad = jnp.zeros((w_in.shape[0], SMALL_COLS - 2 * M_HEADS - G_RANK), BF16)
    small = jnp.concatenate([w16[:, offs[3]:offs[5]], w16[:, offs[8]:offs[9]], pad], axis=1)
    return big, small


def _moe_layout(counts, eidx, pos, n_blocks):
    padded = (counts + EXPERT_ROWS - 1) // EXPERT_ROWS * EXPERT_ROWS
    pend = jnp.cumsum(padded)
    pstart = pend - padded
    experts = jnp.arange(N_EXPERTS, dtype=I32)
    dest = pos + jnp.sum(jnp.where(eidx[..., None] == experts, pstart, 0), axis=-1)
    blk_start = jnp.arange(n_blocks, dtype=I32) * EXPERT_ROWS
    owner = jnp.sum((pend[None, :] <= blk_start[:, None]).astype(I32), axis=1)
    blk_e = jnp.minimum(owner, N_EXPERTS - 1)
    prev = jnp.concatenate([jnp.full((1,), -1, I32), blk_e[:-1]])
    blk_first = (blk_e != prev).astype(I32)
    used = jnp.where(owner < N_EXPERTS, counts[blk_e] - (blk_start - pstart[blk_e]), 0)
    blk_valid = jnp.clip(used, 0, EXPERT_ROWS)
    return dest.astype(I32), blk_e.astype(I32), blk_first, blk_valid.astype(I32)


def kernel(x, c, ada_w, ada_b, norm_mix_w, norm_moe_w, w_in, m_conv_w, m_gate_b, m_norm_w,
           g_alpha_w, g_alpha_b, g_norm_w, w_pa, w_pb, w_o, router_w, router_b,
           exp_w1, exp_w3, exp_w2, sh_w1, sh_w3, sh_w2, final_norm_w):
    bsz, seq, d = x.shape
    depth = ada_w.shape[0]
    t = bsz * seq
    n_rows = t * TOP_K + N_EXPERTS * EXPERT_ROWS
    n_blocks = n_rows // EXPERT_ROWS

    ada = _ada(c, ada_w, ada_b).reshape(depth, bsz, 6, 1, d)
    h = x.reshape(t, d)
    for l in range(depth):
        sh1, sc1, gt1, sh2, sc2, gt2 = (ada[l, :, i] for i in range(6))

        w_big, w_small = _split_w_in(w_in[l])
        big, small = _inproj(h, sh1, sc1, norm_mix_w[l][None, :], w_big, w_small, seq)
        gate_row = jnp.zeros((1, SMALL_COLS), F32)
        gate_row = gate_row.at[0, SMALL_I:SMALL_I + M_HEADS].set(m_gate_b[l, 0])
        gate_row = gate_row.at[0, SMALL_F:SMALL_F + M_HEADS].set(m_gate_b[l, 1])
        ha = _mlstm(big, small, m_conv_w[l], gate_row, m_norm_w[l][None, :], bsz, seq)
        alpha_full = jnp.zeros((SMALL_COLS, G_HEADS * G_DK), F32)
        alpha_full = alpha_full.at[SMALL_R:SMALL_R + G_RANK].set(g_alpha_w[l]).astype(BF16)
        hb = _gla(big, small, alpha_full, g_alpha_b[l][None, :], g_norm_w[l][None, :], bsz, seq)
        h = _mixout(h, ha, hb, big, gt1, w_pa[l].astype(BF16), w_pb[l].astype(BF16),
                    w_o[l].astype(BF16), seq)

        u, eidx, wts, pos, cnt = _route(h, sh2, sc2, norm_moe_w[l][None, :],
                                        router_w[l].T, router_b[l][:, None], seq)
        dest, blk_e, blk_first, blk_valid = _moe_layout(cnt[:, 0], eidx, pos, n_blocks)
        xg = _sc_scatter_rows(u, dest, n_rows)
        y = _experts(xg, blk_e, blk_first, blk_valid, exp_w1[l], exp_w3[l], exp_w2[l])
        yg = _sc_gather_rows(y, dest.reshape(-1)).reshape(TOP_K, t, d // 2)
        h = _combine(h, u, yg, wts.T, gt2, sh_w1[l].astype(BF16), sh_w3[l].astype(BF16),
                     sh_w2[l].astype(BF16), final_norm_w[None, :], seq,
                     final_norm=(l == depth - 1))

    return h.reshape(bsz, seq, d)
```

```python
import functools

import jax
import jax.numpy as jnp
import numpy as np
from jax import lax
from jax.experimental import pallas as pl
from jax.experimental.pallas import tpu as pltpu
from jax.experimental.pallas import tpu_sc as plsc

F32 = jnp.float32
BF16 = jnp.bfloat16
I32 = jnp.int32
U32 = jnp.uint32
HI_MASK = np.uint32(0xFFFF0000)
HIGHEST = lax.Precision.HIGHEST

SC_CORES = 2
SC_SUBCORES = 16
SC_WORKERS = SC_CORES * SC_SUBCORES
SC_CHUNK = 64

D_MODEL = 1024
M_HEADS = 4
M_DQK = 128
M_DV = 256
M_CONV = 4
GATE_CAP = 15.0
G_HEADS = 4
G_DK = 128
G_DV = 256
G_RANK = 16
G_TAU = 16.0
G_CHUNK = 64
N_EXPERTS = 64
TOP_K = 8
N_GROUPS = 8
GROUP_SIZE = N_EXPERTS // N_GROUPS
TOPK_GROUPS = 4
D_EXPERT = 256
D_SHARED = 256
ROUTED_SCALE = 2.5
EPS = 1e-6

M_CHUNK_ROWS = 256
G_BLOCK_ROWS = 256
PROJ_ROWS = 1024
MIX_ROWS = 512
ROUTE_ROWS = 512
EXPERT_ROWS = 512
COMBINE_ROWS = 512
COMBINE_PARTS = 2
CONV_HALO = 8
VMEM_LIMIT = 48 * 1024 * 1024

COL_QM, COL_KM, COL_VM, COL_OM = 0, 512, 1024, 2048
COL_QG, COL_KG, COL_VG, COL_ZG, COL_GA, COL_GB = 3072, 3584, 4096, 5120, 6144, 7168
BIG_COLS = 8192
SMALL_COLS = 128
SMALL_I, SMALL_F, SMALL_R = 0, M_HEADS, 2 * M_HEADS


def _cparams(sem, vmem=VMEM_LIMIT):
    return pltpu.CompilerParams(dimension_semantics=sem, vmem_limit_bytes=vmem)


def _silu(x):
    return x * jax.nn.sigmoid(x)


def _log_sigmoid(x):
    return jnp.minimum(x, 0.0) - jnp.log1p(jnp.exp(-jnp.abs(x)))


def _modulated_rmsnorm(x, w, sc, sh):
    y = x * lax.rsqrt(jnp.mean(x * x, axis=-1, keepdims=True) + EPS)
    return (y * w) * (1.0 + sc) + sh


def _pack_bf16_pair(lo, hi):
    lo_bits = lax.bitcast_convert_type(lo.astype(BF16).astype(F32), U32)
    hi_bits = lax.bitcast_convert_type(hi.astype(BF16).astype(F32), U32)
    return (lo_bits >> 16) | (hi_bits & HI_MASK)


def _unpack_bf16_pair(packed):
    lo = lax.bitcast_convert_type(packed << 16, F32)
    hi = lax.bitcast_convert_type(packed & HI_MASK, F32)
    return lo, hi


def _lower_tri(n, dtype):
    r = lax.broadcasted_iota(I32, (n, n), 0)
    c = lax.broadcasted_iota(I32, (n, n), 1)
    return (r >= c).astype(dtype)


def _ada_body(c_ref, w_ref, b_ref, o_ref):
    cond = _silu(c_ref[...])
    o_ref[0] = jnp.dot(cond.astype(BF16), w_ref[0].astype(BF16),
                       preferred_element_type=F32) + b_ref[0]


def _ada(c, ada_w, ada_b):
    depth, d, six_d = ada_w.shape
    bsz = c.shape[0]
    nj = six_d // d
    return pl.pallas_call(
        _ada_body,
        grid=(depth, nj),
        in_specs=[pl.BlockSpec((bsz, d), lambda l, j: (0, 0)),
                  pl.BlockSpec((1, d, d), lambda l, j: (l, 0, j)),
                  pl.BlockSpec((1, 1, d), lambda l, j: (l, 0, j))],
        out_specs=pl.BlockSpec((1, bsz, d), lambda l, j: (l, 0, j)),
        out_shape=jax.ShapeDtypeStruct((depth, bsz, six_d), F32),
        compiler_params=_cparams(("parallel", "parallel")),
        name="ada_ln",
    )(c, ada_w, ada_b.reshape(depth, 1, six_d))


def _inproj_body(h_ref, sh_ref, sc_ref, nw_ref, wb_ref, ws_ref, big_ref, small_ref, u_scr):
    @pl.when(pl.program_id(1) == 0)
    def _():
        u = _modulated_rmsnorm(h_ref[...], nw_ref[...], sc_ref[0], sh_ref[0])
        ub = u.astype(BF16)
        u_scr[...] = ub
        small_ref[...] = jnp.dot(ub, ws_ref[...], preferred_element_type=F32)

    big_ref[...] = jnp.dot(u_scr[...], wb_ref[...], preferred_element_type=F32).astype(BF16)


def _inproj(h, sh, sc, nw, w_big, w_small, seq):
    t, d = h.shape
    tm = min(PROJ_ROWS, seq)
    per_b = seq // tm
    tn = 2048
    return pl.pallas_call(
        _inproj_body,
        grid=(t // tm, BIG_COLS // tn),
        in_specs=[pl.BlockSpec((tm, d), lambda i, j: (i, 0)),
                  pl.BlockSpec((1, 1, d), lambda i, j: (i // per_b, 0, 0)),
                  pl.BlockSpec((1, 1, d), lambda i, j: (i // per_b, 0, 0)),
                  pl.BlockSpec((1, d), lambda i, j: (0, 0)),
                  pl.BlockSpec((d, tn), lambda i, j: (0, j)),
                  pl.BlockSpec((d, SMALL_COLS), lambda i, j: (0, 0))],
        out_specs=[pl.BlockSpec((tm, tn), lambda i, j: (i, j)),
                   pl.BlockSpec((tm, SMALL_COLS), lambda i, j: (i, 0))],
        out_shape=[jax.ShapeDtypeStruct((t, BIG_COLS), BF16),
                   jax.ShapeDtypeStruct((t, SMALL_COLS), F32)],
        scratch_shapes=[pltpu.VMEM((tm, d), BF16)],
        compiler_params=_cparams(("parallel", "arbitrary")),
        name="in_proj",
    )(h, sh, sc, nw, w_big, w_small)


def _mlstm_body(q_ref, k_ref, v_ref, o_ref, small_ref, cw_ref, gb_ref, nw_ref, ha_ref,
                xe_scr, c_scr, n_scr, m_scr):
    rows = q_ref.shape[0]
    half = M_HEADS * M_DQK

    @pl.when(pl.program_id(1) == 0)
    def _():
        xe_scr[0:CONV_HALO, :] = jnp.zeros((CONV_HALO, 2 * half), F32)
        c_scr[...] = jnp.zeros(c_scr.shape, F32)
        n_scr[...] = jnp.zeros(n_scr.shape, F32)
        m_scr[...] = jnp.zeros(m_scr.shape, F32)

    xe_scr[CONV_HALO:CONV_HALO + rows, 0:half] = q_ref[...].astype(F32)
    xe_scr[CONV_HALO:CONV_HALO + rows, half:2 * half] = k_ref[...].astype(F32)
    conv = None
    for j in range(M_CONV):
        off = CONV_HALO - (M_CONV - 1) + j
        term = xe_scr[off:off + rows, :] * cw_ref[j:j + 1, :]
        conv = term if conv is None else conv + term
    qk = _silu(conv)
    xe_scr[0:CONV_HALO, :] = xe_scr[rows:rows + CONV_HALO, :]

    capped = GATE_CAP * jnp.tanh((small_ref[...] + gb_ref[...]) / GATE_CAP)
    li_all = capped
    lf_all = _log_sigmoid(capped)
    b_all = _split3_dot(_lower_tri(rows, BF16), lf_all)
    li_t = li_all.T
    b_t = b_all.T

    r_io = lax.broadcasted_iota(I32, (rows, rows), 0)
    c_io = lax.broadcasted_iota(I32, (rows, rows), 1)
    causal = r_io >= c_io

    for h in range(M_HEADS):
        q = (qk[:, h * M_DQK:(h + 1) * M_DQK] * (M_DQK ** -0.5)).astype(BF16)
        k = qk[:, half + h * M_DQK:half + (h + 1) * M_DQK]
        kb = k.astype(BF16)
        v = v_ref[:, h * M_DV:(h + 1) * M_DV]
        li_c = li_all[:, SMALL_I + h:SMALL_I + h + 1]
        b_c = b_all[:, SMALL_F + h:SMALL_F + h + 1]
        li_r = li_t[SMALL_I + h:SMALL_I + h + 1, :]
        b_r = b_t[SMALL_F + h:SMALL_F + h + 1, :]
        g = b_c[rows - 1:rows, :]
        m_prev = m_scr[h][:, 0:1]
        c_prev = c_scr[h]
        n_prev = n_scr[h]

        d_mat = jnp.where(causal, b_c - b_r + li_r, -jnp.inf)
        m_inter = b_c + m_prev
        m_t = jnp.maximum(jnp.max(d_mat, axis=1, keepdims=True), m_inter)
        s = lax.dot_general(q, kb, (((1,), (1,)), ((), ())), preferred_element_type=F32)
        p = jnp.exp(d_mat - m_t) * s
        w_inter = jnp.exp(m_inter - m_t)
        num = (jnp.dot(p.astype(BF16), v, preferred_element_type=F32)
               + w_inter * jnp.dot(q, c_prev.astype(BF16), preferred_element_type=F32))
        qn = jnp.sum(q.astype(F32) * n_prev, axis=1, keepdims=True)
        den = jnp.sum(p, axis=1, keepdims=True) + w_inter * qn
        hh = num / jnp.maximum(jnp.abs(den), jnp.exp(-m_t))

        a_r = g - b_r + li_r
        a_c = g - b_c + li_c
        m_new = jnp.maximum(g + m_prev, jnp.max(a_r, axis=1, keepdims=True))
        decay = jnp.exp(g + m_prev - m_new)
        wk = jnp.exp(a_c - m_new) * k
        c_scr[h] = decay * c_prev + lax.dot_general(
            wk.astype(BF16), v, (((0,), (0,)), ((), ())), preferred_element_type=F32)
        n_scr[h] = decay * n_prev + jnp.sum(wk, axis=0, keepdims=True)
        m_scr[h] = jnp.broadcast_to(m_new, m_scr.shape[1:])

        y = hh * lax.rsqrt(jnp.mean(hh * hh, axis=-1, keepdims=True) + EPS)
        y = y * nw_ref[:, h * M_DV:(h + 1) * M_DV]
        gate = jax.nn.sigmoid(o_ref[:, h * M_DV:(h + 1) * M_DV].astype(F32))
        ha_ref[:, h * M_DV:(h + 1) * M_DV] = (y * gate).astype(BF16)


def _mlstm(big, small, conv_w, gate_row, norm_w, bsz, seq):
    t = big.shape[0]
    rows = min(M_CHUNK_ROWS, seq)
    nc = seq // rows
    half = M_HEADS * M_DQK
    vcols = M_HEADS * M_DV
    row_map = lambda b, c: b * nc + c
    return pl.pallas_call(
        _mlstm_body,
        grid=(bsz, nc),
        in_specs=[pl.BlockSpec((rows, half), lambda b, c: (row_map(b, c), COL_QM // half)),
                  pl.BlockSpec((rows, half), lambda b, c: (row_map(b, c), COL_KM // half)),
                  pl.BlockSpec((rows, vcols), lambda b, c: (row_map(b, c), COL_VM // vcols)),
                  pl.BlockSpec((rows, vcols), lambda b, c: (row_map(b, c), COL_OM // vcols)),
                  pl.BlockSpec((rows, SMALL_COLS), lambda b, c: (row_map(b, c), 0)),
                  pl.BlockSpec((M_CONV, 2 * half), lambda b, c: (0, 0)),
                  pl.BlockSpec((1, SMALL_COLS), lambda b, c: (0, 0)),
                  pl.BlockSpec((1, vcols), lambda b, c: (0, 0))],
        out_specs=pl.BlockSpec((rows, vcols), lambda b, c: (row_map(b, c), 0)),
        out_shape=jax.ShapeDtypeStruct((t, vcols), BF16),
        scratch_shapes=[pltpu.VMEM((rows + CONV_HALO, 2 * half), F32),
                        pltpu.VMEM((M_HEADS, M_DQK, M_DV), F32),
                        pltpu.VMEM((M_HEADS, 1, M_DQK), F32),
                        pltpu.VMEM((M_HEADS, 1, 128), F32)],
        compiler_params=_cparams(("parallel", "arbitrary")),
        name="mlstm",
    )(big, big, big, big, small, conv_w, gate_row, norm_w)


def _split3_dot(lhs01, x):
    hi = x.astype(BF16)
    r1 = x - hi.astype(F32)
    mid = r1.astype(BF16)
    lo = (r1 - mid.astype(F32)).astype(BF16)
    return (jnp.dot(lhs01, hi, preferred_element_type=F32)
            + jnp.dot(lhs01, mid, preferred_element_type=F32)
            + jnp.dot(lhs01, lo, preferred_element_type=F32))


def _gla_body(q_ref, k_ref, v_ref, z_ref, small_ref, aw_ref, ab_ref, nw_ref, hb_ref, st_scr):
    rows = q_ref.shape[0]
    n_chunks = rows // G_CHUNK
    kcols = G_HEADS * G_DK

    @pl.when(pl.program_id(1) == 0)
    def _():
        st_scr[...] = jnp.zeros(st_scr.shape, F32)

    logits = jnp.dot(small_ref[...].astype(BF16), aw_ref[...], preferred_element_type=F32)
    la = _log_sigmoid(logits + ab_ref[...]) / G_TAU
    r_io = lax.broadcasted_iota(I32, (rows, rows), 0)
    c_io = lax.broadcasted_iota(I32, (rows, rows), 1)
    chunk_causal = jnp.logical_and(r_io >= c_io, r_io // G_CHUNK == c_io // G_CHUNK)
    bc = _split3_dot(jnp.where(chunk_causal, 1.0, 0.0).astype(BF16), la)
    gcs = [bc[(ci + 1) * G_CHUNK - 1:(ci + 1) * G_CHUNK, :] for ci in range(n_chunks)]
    gc_rows = jnp.concatenate([jnp.broadcast_to(g, (G_CHUNK, kcols)) for g in gcs], axis=0)

    q = q_ref[...].astype(F32) * (G_DK ** -0.5)
    k = k_ref[...].astype(F32)
    q_in = (q * jnp.exp(bc)).astype(BF16)
    k_in = (k * jnp.exp(-bc)).astype(BF16)
    k_out = (k * jnp.exp(gc_rows - bc)).astype(BF16)

    for h in range(G_HEADS):
        ks = slice(h * G_DK, (h + 1) * G_DK)
        vs = slice(h * G_DV, (h + 1) * G_DV)
        v = v_ref[:, vs]
        att = lax.dot_general(q_in[:, ks], k_in[:, ks], (((1,), (1,)), ((), ())),
                              preferred_element_type=F32)
        att = jnp.where(chunk_causal, att, 0.0).astype(BF16)
        o_intra = jnp.dot(att, v, preferred_element_type=F32)
        st = st_scr[h]
        outs = []
        for ci in range(n_chunks):
            rs = slice(ci * G_CHUNK, (ci + 1) * G_CHUNK)
            o_inter = lax.dot_general(q_in[rs, ks], st.astype(BF16), (((1,), (1,)), ((), ())),
                                      preferred_element_type=F32)
            outs.append(o_intra[rs, :] + o_inter)
            st = jnp.exp(gcs[ci][:, ks]) * st + lax.dot_general(
                v[rs, :], k_out[rs, ks], (((0,), (0,)), ((), ())), preferred_element_type=F32)
        st_scr[h] = st
        o = jnp.concatenate(outs, axis=0)
        y = o * lax.rsqrt(jnp.mean(o * o, axis=-1, keepdims=True) + EPS)
        y = y * nw_ref[:, vs]
        hb_ref[:, vs] = (y * _silu(z_ref[:, vs].astype(F32))).astype(BF16)


def _gla(big, small, alpha_full, alpha_b, norm_w, bsz, seq):
    t = big.shape[0]
    rows = min(G_BLOCK_ROWS, seq)
    nb = seq // rows
    kcols = G_HEADS * G_DK
    vcols = G_HEADS * G_DV
    row_map = lambda b, c: b * nb + c
    return pl.pallas_call(
        _gla_body,
        grid=(bsz, nb),
        in_specs=[pl.BlockSpec((rows, kcols), lambda b, c: (row_map(b, c), COL_QG // kcols)),
                  pl.BlockSpec((rows, kcols), lambda b, c: (row_map(b, c), COL_KG // kcols)),
                  pl.BlockSpec((rows, vcols), lambda b, c: (row_map(b, c), COL_VG // vcols)),
                  pl.BlockSpec((rows, vcols), lambda b, c: (row_map(b, c), COL_ZG // vcols)),
                  pl.BlockSpec((rows, SMALL_COLS), lambda b, c: (row_map(b, c), 0)),
                  pl.BlockSpec((SMALL_COLS, kcols), lambda b, c: (0, 0)),
                  pl.BlockSpec((1, kcols), lambda b, c: (0, 0)),
                  pl.BlockSpec((1, vcols), lambda b, c: (0, 0))],
        out_specs=pl.BlockSpec((rows, vcols), lambda b, c: (row_map(b, c), 0)),
        out_shape=jax.ShapeDtypeStruct((t, vcols), BF16),
        scratch_shapes=[pltpu.VMEM((G_HEADS, G_DV, G_DK), F32)],
        compiler_params=_cparams(("parallel", "arbitrary")),
        name="gla",
    )(big, big, big, big, small, alpha_full, alpha_b, norm_w)


def _mixout_body(h_ref, ha_ref, hb_ref, ga_ref, gb_ref, gt_ref, wpa_ref, wpb_ref, wo_ref, o_ref):
    a = jnp.dot(ha_ref[...], wpa_ref[...], preferred_element_type=F32)
    b = jnp.dot(hb_ref[...], wpb_ref[...], preferred_element_type=F32)
    y = (jax.nn.sigmoid(ga_ref[...].astype(F32)) * a
         + jax.nn.sigmoid(gb_ref[...].astype(F32)) * b)
    o_ref[...] = h_ref[...] + gt_ref[0] * jnp.dot(y.astype(BF16), wo_ref[...],
                                                  preferred_element_type=F32)


def _mixout(h, ha, hb, big, gt, w_pa, w_pb, w_o, seq):
    t, d = h.shape
    tm = min(MIX_ROWS, seq)
    per_b = seq // tm
    wspec = pl.BlockSpec((d, d), lambda i: (0, 0))
    return pl.pallas_call(
        _mixout_body,
        grid=(t // tm,),
        in_specs=[pl.BlockSpec((tm, d), lambda i: (i, 0)),
                  pl.BlockSpec((tm, d), lambda i: (i, 0)),
                  pl.BlockSpec((tm, d), lambda i: (i, 0)),
                  pl.BlockSpec((tm, d), lambda i: (i, COL_GA // d)),
                  pl.BlockSpec((tm, d), lambda i: (i, COL_GB // d)),
                  pl.BlockSpec((1, 1, d), lambda i: (i // per_b, 0, 0)),
                  wspec, wspec, wspec],
        out_specs=pl.BlockSpec((tm, d), lambda i: (i, 0)),
        out_shape=jax.ShapeDtypeStruct((t, d), F32),
        compiler_params=_cparams(("parallel",)),
        name="mix_out",
    )(h, ha, hb, big, big, gt, w_pa, w_pb, w_o)


def _route_body(h_ref, sh_ref, sc_ref, nw_ref, rwt_ref, rb_ref,
                u_ref, eidx_ref, wts_ref, pos_ref, cnt_ref, carry_scr):
    tm = h_ref.shape[0]

    @pl.when(pl.program_id(0) == 0)
    def _():
        carry_scr[...] = jnp.zeros(carry_scr.shape, F32)

    u = _modulated_rmsnorm(h_ref[...], nw_ref[...], sc_ref[0], sh_ref[0])
    half = u.shape[1] // 2
    u_ref[...] = _pack_bf16_pair(u[:, :half], u[:, half:])
    logits = lax.dot_general(rwt_ref[...], u, (((1,), (1,)), ((), ())),
                             precision=HIGHEST, preferred_element_type=F32)
    scores = jax.nn.sigmoid(logits)
    sel = scores + rb_ref[...]

    neg = -jnp.inf
    sub_io = lax.broadcasted_iota(I32, (GROUP_SIZE, tm), 0)
    pieces = []
    for g in range(N_GROUPS):
        blk = sel[g * GROUP_SIZE:(g + 1) * GROUP_SIZE, :]
        m1 = jnp.max(blk, axis=0, keepdims=True)
        first = jnp.min(jnp.where(blk == m1, sub_io, GROUP_SIZE), axis=0, keepdims=True)
        m2 = jnp.max(jnp.where(sub_io == first, neg, blk), axis=0, keepdims=True)
        pieces.append(jnp.broadcast_to(m1 + m2, (GROUP_SIZE, tm)))
    gscore = jnp.concatenate(pieces, axis=0)

    e_io = lax.broadcasted_iota(I32, (N_EXPERTS, tm), 0)
    grp_io = e_io // GROUP_SIZE
    gmask = jnp.zeros((N_EXPERTS, tm), jnp.bool_)
    for _ in range(TOPK_GROUPS):
        mx = jnp.max(gscore, axis=0, keepdims=True)
        gi = jnp.min(jnp.where(gscore == mx, grp_io, N_GROUPS), axis=0, keepdims=True)
        hit = grp_io == gi
        gmask = jnp.logical_or(gmask, hit)
        gscore = jnp.where(hit, neg, gscore)

    cur = jnp.where(gmask, sel, neg)
    row_io = lax.broadcasted_iota(I32, (TOP_K, tm), 0)
    eidx = jnp.zeros((TOP_K, tm), I32)
    wraw = jnp.zeros((TOP_K, tm), F32)
    chosen = jnp.zeros((N_EXPERTS, tm), jnp.bool_)
    hits = []
    for kk in range(TOP_K):
        mx = jnp.max(cur, axis=0, keepdims=True)
        ei = jnp.min(jnp.where(cur == mx, e_io, N_EXPERTS), axis=0, keepdims=True)
        hit = e_io == ei
        hits.append(hit)
        sc_k = jnp.sum(jnp.where(hit, scores, 0.0), axis=0, keepdims=True)
        eidx = jnp.where(row_io == kk, ei, eidx)
        wraw = jnp.where(row_io == kk, sc_k, wraw)
        chosen = jnp.logical_or(chosen, hit)
        cur = jnp.where(hit, neg, cur)

    wsum = jnp.sum(wraw, axis=0, keepdims=True)
    wts_ref[...] = wraw / wsum * ROUTED_SCALE
    eidx_ref[...] = eidx

    chosen_f = jnp.where(chosen, 1.0, 0.0)
    r_io = lax.broadcasted_iota(I32, (tm, tm), 0)
    c_io = lax.broadcasted_iota(I32, (tm, tm), 1)
    strict_upper = jnp.where(r_io < c_io, 1.0, 0.0).astype(BF16)
    prefix = jnp.dot(chosen_f.astype(BF16), strict_upper, preferred_element_type=F32)
    rank = prefix + carry_scr[:, 0:1]
    pos = jnp.zeros((TOP_K, tm), F32)
    for kk in range(TOP_K):
        p_k = jnp.sum(jnp.where(hits[kk], rank, 0.0), axis=0, keepdims=True)
        pos = jnp.where(row_io == kk, p_k, pos)
    pos_ref[...] = pos.astype(I32)
    total = carry_scr[...] + jnp.sum(chosen_f, axis=1, keepdims=True)
    carry_scr[...] = total
    cnt_ref[...] = total.astype(I32)


def _route(h, sh, sc, nw, rw_t, rb_col, seq):
    t, d = h.shape
    tm = min(ROUTE_ROWS, seq)
    per_b = seq // tm
    kspec = pl.BlockSpec((TOP_K, tm), lambda i: (0, i))
    return pl.pallas_call(
        _route_body,
        grid=(t // tm,),
        in_specs=[pl.BlockSpec((tm, d), lambda i: (i, 0)),
                  pl.BlockSpec((1, 1, d), lambda i: (i // per_b, 0, 0)),
                  pl.BlockSpec((1, 1, d), lambda i: (i // per_b, 0, 0)),
                  pl.BlockSpec((1, d), lambda i: (0, 0)),
                  pl.BlockSpec((N_EXPERTS, d), lambda i: (0, 0)),
                  pl.BlockSpec((N_EXPERTS, 1), lambda i: (0, 0))],
        out_specs=[pl.BlockSpec((tm, d // 2), lambda i: (i, 0)), kspec, kspec, kspec,
                   pl.BlockSpec((N_EXPERTS, 128), lambda i: (0, 0))],
        out_shape=[jax.ShapeDtypeStruct((t, d // 2), U32),
                   jax.ShapeDtypeStruct((TOP_K, t), I32),
                   jax.ShapeDtypeStruct((TOP_K, t), F32),
                   jax.ShapeDtypeStruct((TOP_K, t), I32),
                   jax.ShapeDtypeStruct((N_EXPERTS, 128), I32)],
        scratch_shapes=[pltpu.VMEM((N_EXPERTS, 128), F32)],
        compiler_params=_cparams(("arbitrary",)),
        name="moe_route",
    )(h, sh, sc, nw, rw_t, rb_col)


def _sc_worker_id():
    return lax.axis_index("s") * SC_CORES + lax.axis_index("c")


def _sc_scatter_rows(x, dest, n_rows):
    t, d = x.shape
    n_k = dest.shape[0]
    assert t % (SC_WORKERS * 2 * SC_CHUNK) == 0
    per_w = t // SC_WORKERS
    n_ch = per_w // SC_CHUNK
    dest4 = dest.reshape(n_k, SC_WORKERS, n_ch, SC_CHUNK).transpose(1, 2, 0, 3)
    mesh = plsc.VectorSubcoreMesh(core_axis_name="c", subcore_axis_name="s")

    @functools.partial(
        pl.kernel, mesh=mesh,
        out_type=jax.ShapeDtypeStruct((n_rows, d), x.dtype),
        scratch_types=[pltpu.VMEM((n_ch, n_k, SC_CHUNK), I32),
                       pltpu.VMEM((SC_CHUNK, d), x.dtype),
                       pltpu.VMEM((SC_CHUNK, d), x.dtype)] + [pltpu.SemaphoreType.DMA] * 4,
        name="moe_dispatch_sc",
    )
    def scatter_kernel(x_hbm, dest_hbm, out_hbm, idx_v, rows0, rows1, l0, l1, s0, s1):
        wid = _sc_worker_id()
        base = wid * per_w
        pltpu.sync_copy(dest_hbm.at[wid], idx_v)
        bufs = ((rows0, l0, s0), (rows1, l1, s1))

        def load(ci, b):
            rows, load_sem, _ = bufs[b]
            return pltpu.make_async_copy(x_hbm.at[pl.ds(base + ci * SC_CHUNK, SC_CHUNK)], rows,
                                         load_sem)

        def scatters(ci, b):
            rows, _, scatter_sem = bufs[b]
            return [pltpu.make_async_copy(rows, out_hbm.at[idx_v.at[ci, j]], scatter_sem)
                    for j in range(n_k)]

        load(0, 0).start()

        @pl.loop(0, n_ch, step=2)
        def _(ci):
            load(ci, 0).wait()
            for cp in scatters(ci, 0):
                cp.start()

            @pl.when(ci > 0)
            def _():
                for cp in scatters(ci - 1, 1):
                    cp.wait()

            load(ci + 1, 1).start()
            load(ci + 1, 1).wait()
            for cp in scatters(ci + 1, 1):
                cp.start()
            for cp in scatters(ci, 0):
                cp.wait()

            @pl.when(ci + 2 < n_ch)
            def _():
                load(ci + 2, 0).start()

        for cp in scatters(n_ch - 1, 1):
            cp.wait()

    return scatter_kernel(x, dest4)


def _sc_gather_rows(table, idx):
    n, d = idx.shape[0], table.shape[1]
    assert n % (SC_WORKERS * 2 * SC_CHUNK) == 0
    per_w = n // SC_WORKERS
    n_ch = per_w // SC_CHUNK
    idx3 = idx.reshape(SC_WORKERS, n_ch, SC_CHUNK)
    mesh = plsc.VectorSubcoreMesh(core_axis_name="c", subcore_axis_name="s")

    @functools.partial(
        pl.kernel, mesh=mesh,
        out_type=jax.ShapeDtypeStruct((n, d), table.dtype),
        scratch_types=[pltpu.VMEM((n_ch, SC_CHUNK), I32),
                       pltpu.VMEM((SC_CHUNK, d), table.dtype),
                       pltpu.VMEM((SC_CHUNK, d), table.dtype)] + [pltpu.SemaphoreType.DMA] * 4,
        name="moe_combine_sc",
    )
    def gather_kernel(table_hbm, idx_hbm, out_hbm, idx_v, rows0, rows1, g0, g1, w0, w1):
        wid = _sc_worker_id()
        base = wid * per_w
        pltpu.sync_copy(idx_hbm.at[wid], idx_v)
        bufs = ((rows0, g0, w0), (rows1, g1, w1))

        def gather(ci, b):
            rows, gather_sem, _ = bufs[b]
            return pltpu.make_async_copy(table_hbm.at[idx_v.at[ci]], rows, gather_sem)

        def write(ci, b):
            rows, _, write_sem = bufs[b]
            return pltpu.make_async_copy(rows, out_hbm.at[pl.ds(base + ci * SC_CHUNK, SC_CHUNK)],
                                         write_sem)

        gather(0, 0).start()

        @pl.loop(0, n_ch, step=2)
        def _(ci):
            gather(ci, 0).wait()
            write(ci, 0).start()

            @pl.when(ci > 0)
            def _():
                write(ci - 1, 1).wait()

            gather(ci + 1, 1).start()
            gather(ci + 1, 1).wait()
            write(ci + 1, 1).start()
            write(ci, 0).wait()

            @pl.when(ci + 2 < n_ch)
            def _():
                gather(ci + 2, 0).start()

        write(n_ch - 1, 1).wait()

    return gather_kernel(table, idx3)


def _expert_body(blk_e_ref, blk_first_ref, blk_valid_ref, x_ref, w1_ref, w3_ref, w2_ref, y_ref,
                 w1_scr, w3_scr, w2_scr):
    del blk_e_ref
    j = pl.program_id(0)
    valid = blk_valid_ref[j]

    @pl.when(blk_first_ref[j] == 1)
    def _():
        w1_scr[...] = w1_ref[0].astype(BF16)
        w3_scr[...] = w3_ref[0].astype(BF16)
        w2_scr[...] = w2_ref[0].astype(BF16)

    @pl.when(valid > 0)
    def _():
        half = x_ref.shape[1]
        rows = lax.broadcasted_iota(I32, x_ref.shape, 0)
        lo, hi = _unpack_bf16_pair(jnp.where(rows < valid, x_ref[...], 0))
        lo, hi = lo.astype(BF16), hi.astype(BF16)

        def proj(w_scr):
            return (jnp.dot(lo, w_scr[0:half, :], preferred_element_type=F32)
                    + jnp.dot(hi, w_scr[half:2 * half, :], preferred_element_type=F32))

        hid = _silu(proj(w1_scr)) * proj(w3_scr)
        y = jnp.dot(hid.astype(BF16), w2_scr[...], preferred_element_type=F32)
        y_ref[...] = _pack_bf16_pair(y[:, :half], y[:, half:])

    @pl.when(valid == 0)
    def _():
        y_ref[...] = jnp.zeros(y_ref.shape, U32)


def _experts(xg, blk_e, blk_first, blk_valid, w1, w3, w2, layer):
    n_rows, half = xg.shape
    n_blocks = n_rows // EXPERT_ROWS
    d, de = w1.shape[-2:]
    grid_spec = pltpu.PrefetchScalarGridSpec(
        num_scalar_prefetch=3,
        grid=(n_blocks,),
        in_specs=[pl.BlockSpec((EXPERT_ROWS, half), lambda j, be, bf, bv: (j, 0)),
                  pl.BlockSpec((None, 1, d, de), lambda j, be, bf, bv: (layer, be[j], 0, 0)),
                  pl.BlockSpec((None, 1, d, de), lambda j, be, bf, bv: (layer, be[j], 0, 0)),
                  pl.BlockSpec((None, 1, de, d), lambda j, be, bf, bv: (layer, be[j], 0, 0))],
        out_specs=pl.BlockSpec((EXPERT_ROWS, half), lambda j, be, bf, bv: (j, 0)),
        scratch_shapes=[pltpu.VMEM((d, de), BF16), pltpu.VMEM((d, de), BF16),
                        pltpu.VMEM((de, d), BF16)],
    )
    return pl.pallas_call(
        _expert_body,
        grid_spec=grid_spec,
        out_shape=jax.ShapeDtypeStruct((n_rows, half), U32),
        compiler_params=_cparams(("arbitrary",)),
        name="moe_experts",
    )(blk_e, blk_first, blk_valid, xg, w1, w3, w2)


def _combine_body(h_ref, u_ref, yg_ref, wts_ref, gt_ref, s1_ref, s3_ref, s2_ref, fw_ref, o_ref, *,
                  final_norm):
    half = u_ref.shape[1]
    lo, hi = _unpack_bf16_pair(u_ref[...])
    lo, hi = lo.astype(BF16), hi.astype(BF16)

    def proj(w_ref):
        return (jnp.dot(lo, w_ref[0:half, :], preferred_element_type=F32)
                + jnp.dot(hi, w_ref[half:2 * half, :], preferred_element_type=F32))

    hid = _silu(proj(s1_ref)) * proj(s3_ref)
    shared = jnp.dot(hid.astype(BF16), s2_ref[...], preferred_element_type=F32)

    routed_lo = routed_hi = None
    for kk in range(TOP_K):
        y_lo, y_hi = _unpack_bf16_pair(yg_ref[kk])
        w = wts_ref[:, kk:kk + 1]
        routed_lo = y_lo * w if routed_lo is None else routed_lo + y_lo * w
        routed_hi = y_hi * w if routed_hi is None else routed_hi + y_hi * w
    gt = gt_ref[0]
    out_lo = h_ref[:, 0:half] + gt[:, 0:half] * (routed_lo + shared[:, 0:half])
    out_hi = (h_ref[:, half:2 * half]
              + gt[:, half:2 * half] * (routed_hi + shared[:, half:2 * half]))
    if final_norm:
        ssq = (jnp.sum(out_lo * out_lo, axis=-1, keepdims=True)
               + jnp.sum(out_hi * out_hi, axis=-1, keepdims=True))
        inv = lax.rsqrt(ssq / (2 * half) + EPS)
        out_lo = (out_lo * inv) * fw_ref[:, 0:half]
        out_hi = (out_hi * inv) * fw_ref[:, half:2 * half]
    o_ref[:, 0:half] = out_lo
    o_ref[:, half:2 * half] = out_hi


def _combine(h, u, yg, wts_tk, gt, s1, s3, s2, final_w, seq, final_norm, part, n_parts):
    t, d = h.shape
    half = d // 2
    tm = min(COMBINE_ROWS, seq)
    per_b = seq // tm
    ds_ = s1.shape[-1]
    steps = t // n_parts // tm
    off = part * steps
    return pl.pallas_call(
        functools.partial(_combine_body, final_norm=final_norm),
        grid=(steps,),
        in_specs=[pl.BlockSpec((tm, d), lambda i: (i + off, 0)),
                  pl.BlockSpec((tm, half), lambda i: (i + off, 0)),
                  pl.BlockSpec((TOP_K, tm, half), lambda i: (0, i, 0)),
                  pl.BlockSpec((tm, TOP_K), lambda i: (i + off, 0)),
                  pl.BlockSpec((1, 1, d), lambda i: ((i + off) // per_b, 0, 0)),
                  pl.BlockSpec((d, ds_), lambda i: (0, 0)),
                  pl.BlockSpec((d, ds_), lambda i: (0, 0)),
                  pl.BlockSpec((ds_, d), lambda i: (0, 0)),
                  pl.BlockSpec((1, d), lambda i: (0, 0))],
        out_specs=pl.BlockSpec((tm, d), lambda i: (i + off, 0)),
        out_shape=jax.ShapeDtypeStruct((t, d), F32),
        input_output_aliases={0: 0},
        compiler_params=_cparams(("parallel",)),
        name="moe_combine",
    )(h, u, yg, wts_tk, gt, s1, s3, s2, final_w)


def _split_w_in(w_in):
    sizes = (2 * M_HEADS * M_DQK, M_HEADS * M_DV, M_HEADS * M_DV, M_HEADS, M_HEADS,
             G_HEADS * G_DK, G_HEADS * G_DK, G_HEADS * G_DV, G_RANK, G_HEADS * G_DV,
             D_MODEL, D_MODEL)
    offs = [0]
    for n in sizes:
        offs.append(offs[-1] + n)
    w16 = w_in.astype(BF16)
    big = jnp.concatenate([w16[:, offs[0]:offs[3]], w16[:, offs[5]:offs[8]], w16[:, offs[9]:offs[12]]],
                          axis=1)
    pad = jnp.zeros((w_in.shape[0], SMALL_COLS - 2 * M_HEADS - G_RANK), BF16)
    small = jnp.concatenate([w16[:, offs[3]:offs[5]], w16[:, offs[8]:offs[9]], pad], axis=1)
    return big, small


def _moe_layout(counts, eidx, pos, n_blocks):
    padded = (counts + EXPERT_ROWS - 1) // EXPERT_ROWS * EXPERT_ROWS
    pend = jnp.cumsum(padded)
    pstart = pend - padded
    experts = jnp.arange(N_EXPERTS, dtype=I32)
    dest = pos + jnp.sum(jnp.where(eidx[..., None] == experts, pstart, 0), axis=-1)
    blk_start = jnp.arange(n_blocks, dtype=I32) * EXPERT_ROWS
    owner = jnp.sum((pend[None, :] <= blk_start[:, None]).astype(I32), axis=1)
    blk_e = jnp.minimum(owner, N_EXPERTS - 1)
    prev = jnp.concatenate([jnp.full((1,), -1, I32), blk_e[:-1]])
    blk_first = (blk_e != prev).astype(I32)
    own = blk_e[:, None] == experts[None, :]
    rows_left = jnp.sum(jnp.where(own, (pstart + counts)[None, :], 0), axis=1) - blk_start
    blk_valid = jnp.clip(jnp.where(owner < N_EXPERTS, rows_left, 0), 0, EXPERT_ROWS)
    return dest.astype(I32), blk_e.astype(I32), blk_first, blk_valid.astype(I32)


def kernel(x, c, ada_w, ada_b, norm_mix_w, norm_moe_w, w_in, m_conv_w, m_gate_b, m_norm_w,
           g_alpha_w, g_alpha_b, g_norm_w, w_pa, w_pb, w_o, router_w, router_b,
           exp_w1, exp_w3, exp_w2, sh_w1, sh_w3, sh_w2, final_norm_w):
    bsz, seq, d = x.shape
    depth = ada_w.shape[0]
    t = bsz * seq
    n_rows = t * TOP_K + N_EXPERTS * EXPERT_ROWS
    n_blocks = n_rows // EXPERT_ROWS

    ada = _ada(c, ada_w, ada_b).reshape(depth, bsz, 6, 1, d)
    h = x.reshape(t, d)
    for l in range(depth):
        sh1, sc1, gt1, sh2, sc2, gt2 = (ada[l, :, i] for i in range(6))

        w_big, w_small = _split_w_in(w_in[l])
        big, small = _inproj(h, sh1, sc1, norm_mix_w[l][None, :], w_big, w_small, seq)
        gate_row = jnp.zeros((1, SMALL_COLS), F32)
        gate_row = gate_row.at[0, SMALL_I:SMALL_I + M_HEADS].set(m_gate_b[l, 0])
        gate_row = gate_row.at[0, SMALL_F:SMALL_F + M_HEADS].set(m_gate_b[l, 1])
        ha = _mlstm(big, small, m_conv_w[l], gate_row, m_norm_w[l][None, :], bsz, seq)
        alpha_full = jnp.zeros((SMALL_COLS, G_HEADS * G_DK), F32)
        alpha_full = alpha_full.at[SMALL_R:SMALL_R + G_RANK].set(g_alpha_w[l]).astype(BF16)
        hb = _gla(big, small, alpha_full, g_alpha_b[l][None, :], g_norm_w[l][None, :], bsz, seq)
        h = _mixout(h, ha, hb, big, gt1, w_pa[l].astype(BF16), w_pb[l].astype(BF16),
                    w_o[l].astype(BF16), seq)

        u, eidx, wts, pos, cnt = _route(h, sh2, sc2, norm_moe_w[l][None, :],
                                        router_w[l].T, router_b[l][:, None], seq)
        dest, blk_e, blk_first, blk_valid = _moe_layout(cnt[:, 0], eidx, pos, n_blocks)
        xg = _sc_scatter_rows(u, dest, n_rows)
        y = _experts(xg, blk_e, blk_first, blk_valid, exp_w1, exp_w3, exp_w2, l)
        tp = t // COMBINE_PARTS
        for p in range(COMBINE_PARTS):
            dest_p = dest[:, p * tp:(p + 1) * tp].reshape(-1)
            yg = _sc_gather_rows(y, dest_p).reshape(TOP_K, tp, d // 2)
            h = _combine(h, u, yg, wts.T, gt2, sh_w1[l].astype(BF16), sh_w3[l].astype(BF16),
                         sh_w2[l].astype(BF16), final_norm_w[None, :], seq,
                         final_norm=(l == depth - 1), part=p, n_parts=COMBINE_PARTS)

    return h.reshape(bsz, seq, d)
```

```python
import functools

import jax
import jax.numpy as jnp
import numpy as np
from jax import lax
from jax.experimental import pallas as pl
from jax.experimental.pallas import tpu as pltpu
from jax.experimental.pallas import tpu_sc as plsc

F32 = jnp.float32
BF16 = jnp.bfloat16
I32 = jnp.int32
U32 = jnp.uint32
HI_MASK = np.uint32(0xFFFF0000)
HIGHEST = lax.Precision.HIGHEST

SC_CORES = 2
SC_SUBCORES = 16
SC_WORKERS = SC_CORES * SC_SUBCORES
SC_CHUNK = 64

D_MODEL = 1024
M_HEADS = 4
M_DQK = 128
M_DV = 256
M_CONV = 4
GATE_CAP = 15.0
G_HEADS = 4
G_DK = 128
G_DV = 256
G_RANK = 16
G_TAU = 16.0
G_CHUNK = 64
N_EXPERTS = 64
TOP_K = 8
N_GROUPS = 8
GROUP_SIZE = N_EXPERTS // N_GROUPS
TOPK_GROUPS = 4
D_EXPERT = 256
D_SHARED = 256
ROUTED_SCALE = 2.5
EPS = 1e-6

M_CHUNK_ROWS = 256
G_BLOCK_ROWS = 256
PROJ_ROWS = 512
PROJ_COL_CHUNK = 1024
MIX_ROWS = 512
ROUTE_ROWS = 512
EXPERT_ROWS = 1024
COMBINE_ROWS = 512
COMBINE_PARTS = 2
CONV_HALO = 8
VMEM_LIMIT = 48 * 1024 * 1024

COL_QM, COL_KM, COL_VM, COL_OM = 0, 512, 1024, 2048
COL_QG, COL_KG, COL_VG, COL_ZG, COL_GA, COL_GB = 3072, 3584, 4096, 5120, 6144, 7168
BIG_COLS = 8192
SMALL_COLS = 128
SMALL_I, SMALL_F, SMALL_R = 0, M_HEADS, 2 * M_HEADS


def _cparams(sem, vmem=VMEM_LIMIT):
    return pltpu.CompilerParams(dimension_semantics=sem, vmem_limit_bytes=vmem)


def _silu(x):
    return x * jax.nn.sigmoid(x)


def _log_sigmoid(x):
    return jnp.minimum(x, 0.0) - jnp.log1p(jnp.exp(-jnp.abs(x)))


def _modulated_rmsnorm(x, w, sc, sh):
    y = x * lax.rsqrt(jnp.mean(x * x, axis=-1, keepdims=True) + EPS)
    return (y * w) * (1.0 + sc) + sh


def _pack_bf16_pair(lo, hi):
    lo_bits = lax.bitcast_convert_type(lo.astype(BF16).astype(F32), U32)
    hi_bits = lax.bitcast_convert_type(hi.astype(BF16).astype(F32), U32)
    return (lo_bits >> 16) | (hi_bits & HI_MASK)


def _unpack_bf16_pair(packed):
    lo = lax.bitcast_convert_type(packed << 16, F32)
    hi = lax.bitcast_convert_type(packed & HI_MASK, F32)
    return lo, hi


def _lower_tri(n, dtype):
    r = lax.broadcasted_iota(I32, (n, n), 0)
    c = lax.broadcasted_iota(I32, (n, n), 1)
    return (r >= c).astype(dtype)


def _ada_body(c_ref, w_ref, b_ref, o_ref):
    cond = _silu(c_ref[...])
    o_ref[0] = jnp.dot(cond.astype(BF16), w_ref[0].astype(BF16),
                       preferred_element_type=F32) + b_ref[0]


def _ada(c, ada_w, ada_b):
    depth, d, six_d = ada_w.shape
    bsz = c.shape[0]
    nj = six_d // d
    return pl.pallas_call(
        _ada_body,
        grid=(depth, nj),
        in_specs=[pl.BlockSpec((bsz, d), lambda l, j: (0, 0)),
                  pl.BlockSpec((1, d, d), lambda l, j: (l, 0, j)),
                  pl.BlockSpec((1, 1, d), lambda l, j: (l, 0, j))],
        out_specs=pl.BlockSpec((1, bsz, d), lambda l, j: (l, 0, j)),
        out_shape=jax.ShapeDtypeStruct((depth, bsz, six_d), F32),
        compiler_params=_cparams(("parallel", "parallel")),
        name="ada_ln",
    )(c, ada_w, ada_b.reshape(depth, 1, six_d))


def _inproj_body(h_ref, sh_ref, sc_ref, nw_ref, wb_ref, ws_ref, big_ref, small_ref):
    rows = h_ref.shape[0]
    group = rows // 2
    for g in range(2):
        rs = slice(g * group, (g + 1) * group)
        u = _modulated_rmsnorm(h_ref[rs, :], nw_ref[...], sc_ref[0], sh_ref[0]).astype(BF16)
        small_ref[rs, :] = jnp.dot(u, ws_ref[...], preferred_element_type=F32)
        for j in range(BIG_COLS // PROJ_COL_CHUNK):
            cs = slice(j * PROJ_COL_CHUNK, (j + 1) * PROJ_COL_CHUNK)
            big_ref[rs, cs] = jnp.dot(u, wb_ref[:, cs], preferred_element_type=F32).astype(BF16)


def _inproj(h, sh, sc, nw, w_big, w_small, seq):
    t, d = h.shape
    tm = min(PROJ_ROWS, seq)
    per_b = seq // tm
    resident = pl.Buffered(1)
    return pl.pallas_call(
        _inproj_body,
        grid=(t // tm,),
        in_specs=[pl.BlockSpec((tm, d), lambda i: (i, 0)),
                  pl.BlockSpec((1, 1, d), lambda i: (i // per_b, 0, 0)),
                  pl.BlockSpec((1, 1, d), lambda i: (i // per_b, 0, 0)),
                  pl.BlockSpec((1, d), lambda i: (0, 0)),
                  pl.BlockSpec((d, BIG_COLS), lambda i: (0, 0), pipeline_mode=resident),
                  pl.BlockSpec((d, SMALL_COLS), lambda i: (0, 0), pipeline_mode=resident)],
        out_specs=[pl.BlockSpec((tm, BIG_COLS), lambda i: (i, 0)),
                   pl.BlockSpec((tm, SMALL_COLS), lambda i: (i, 0))],
        out_shape=[jax.ShapeDtypeStruct((t, BIG_COLS), BF16),
                   jax.ShapeDtypeStruct((t, SMALL_COLS), F32)],
        compiler_params=_cparams(("parallel",)),
        name="in_proj",
    )(h, sh, sc, nw, w_big, w_small)


def _mlstm_body(q_ref, k_ref, v_ref, o_ref, small_ref, cw_ref, gb_ref, nw_ref, ha_ref,
                xe_scr, c_scr, n_scr, m_scr):
    rows = q_ref.shape[0]
    half = M_HEADS * M_DQK

    @pl.when(pl.program_id(1) == 0)
    def _():
        xe_scr[0:CONV_HALO, :] = jnp.zeros((CONV_HALO, 2 * half), F32)
        c_scr[...] = jnp.zeros(c_scr.shape, F32)
        n_scr[...] = jnp.zeros(n_scr.shape, F32)
        m_scr[...] = jnp.zeros(m_scr.shape, F32)

    xe_scr[CONV_HALO:CONV_HALO + rows, 0:half] = q_ref[...].astype(F32)
    xe_scr[CONV_HALO:CONV_HALO + rows, half:2 * half] = k_ref[...].astype(F32)
    conv = None
    for j in range(M_CONV):
        off = CONV_HALO - (M_CONV - 1) + j
        term = xe_scr[off:off + rows, :] * cw_ref[j:j + 1, :]
        conv = term if conv is None else conv + term
    qk = _silu(conv)
    xe_scr[0:CONV_HALO, :] = xe_scr[rows:rows + CONV_HALO, :]

    capped = GATE_CAP * jnp.tanh((small_ref[...] + gb_ref[...]) / GATE_CAP)
    li_all = capped
    lf_all = _log_sigmoid(capped)
    b_all = _split3_dot(_lower_tri(rows, BF16), lf_all)
    li_t = li_all.T
    b_t = b_all.T

    r_io = lax.broadcasted_iota(I32, (rows, rows), 0)
    c_io = lax.broadcasted_iota(I32, (rows, rows), 1)
    causal = r_io >= c_io

    for h in range(M_HEADS):
        q = (qk[:, h * M_DQK:(h + 1) * M_DQK] * (M_DQK ** -0.5)).astype(BF16)
        k = qk[:, half + h * M_DQK:half + (h + 1) * M_DQK]
        kb = k.astype(BF16)
        v = v_ref[:, h * M_DV:(h + 1) * M_DV]
        li_c = li_all[:, SMALL_I + h:SMALL_I + h + 1]
        b_c = b_all[:, SMALL_F + h:SMALL_F + h + 1]
        li_r = li_t[SMALL_I + h:SMALL_I + h + 1, :]
        b_r = b_t[SMALL_F + h:SMALL_F + h + 1, :]
        g = b_c[rows - 1:rows, :]
        m_prev = m_scr[h][:, 0:1]
        c_prev = c_scr[h]
        n_prev = n_scr[h]

        d_mat = jnp.where(causal, b_c - b_r + li_r, -jnp.inf)
        m_inter = b_c + m_prev
        m_t = jnp.maximum(jnp.max(d_mat, axis=1, keepdims=True), m_inter)
        s = lax.dot_general(q, kb, (((1,), (1,)), ((), ())), preferred_element_type=F32)
        p = jnp.exp(d_mat - m_t) * s
        w_inter = jnp.exp(m_inter - m_t)
        num = (jnp.dot(p.astype(BF16), v, preferred_element_type=F32)
               + w_inter * jnp.dot(q, c_prev.astype(BF16), preferred_element_type=F32))
        qn = jnp.sum(q.astype(F32) * n_prev, axis=1, keepdims=True)
        den = jnp.sum(p, axis=1, keepdims=True) + w_inter * qn
        hh = num / jnp.maximum(jnp.abs(den), jnp.exp(-m_t))

        a_r = g - b_r + li_r
        a_c = g - b_c + li_c
        m_new = jnp.maximum(g + m_prev, jnp.max(a_r, axis=1, keepdims=True))
        decay = jnp.exp(g + m_prev - m_new)
        wk = jnp.exp(a_c - m_new) * k
        c_scr[h] = decay * c_prev + lax.dot_general(
            wk.astype(BF16), v, (((0,), (0,)), ((), ())), preferred_element_type=F32)
        n_scr[h] = decay * n_prev + jnp.sum(wk, axis=0, keepdims=True)
        m_scr[h] = jnp.broadcast_to(m_new, m_scr.shape[1:])

        y = hh * lax.rsqrt(jnp.mean(hh * hh, axis=-1, keepdims=True) + EPS)
        y = y * nw_ref[:, h * M_DV:(h + 1) * M_DV]
        gate = jax.nn.sigmoid(o_ref[:, h * M_DV:(h + 1) * M_DV].astype(F32))
        ha_ref[:, h * M_DV:(h + 1) * M_DV] = (y * gate).astype(BF16)


def _mlstm(big, small, conv_w, gate_row, norm_w, bsz, seq):
    t = big.shape[0]
    rows = min(M_CHUNK_ROWS, seq)
    nc = seq // rows
    half = M_HEADS * M_DQK
    vcols = M_HEADS * M_DV
    row_map = lambda b, c: b * nc + c
    return pl.pallas_call(
        _mlstm_body,
        grid=(bsz, nc),
        in_specs=[pl.BlockSpec((rows, half), lambda b, c: (row_map(b, c), COL_QM // half)),
                  pl.BlockSpec((rows, half), lambda b, c: (row_map(b, c), COL_KM // half)),
                  pl.BlockSpec((rows, vcols), lambda b, c: (row_map(b, c), COL_VM // vcols)),
                  pl.BlockSpec((rows, vcols), lambda b, c: (row_map(b, c), COL_OM // vcols)),
                  pl.BlockSpec((rows, SMALL_COLS), lambda b, c: (row_map(b, c), 0)),
                  pl.BlockSpec((M_CONV, 2 * half), lambda b, c: (0, 0)),
                  pl.BlockSpec((1, SMALL_COLS), lambda b, c: (0, 0)),
                  pl.BlockSpec((1, vcols), lambda b, c: (0, 0))],
        out_specs=pl.BlockSpec((rows, vcols), lambda b, c: (row_map(b, c), 0)),
        out_shape=jax.ShapeDtypeStruct((t, vcols), BF16),
        scratch_shapes=[pltpu.VMEM((rows + CONV_HALO, 2 * half), F32),
                        pltpu.VMEM((M_HEADS, M_DQK, M_DV), F32),
                        pltpu.VMEM((M_HEADS, 1, M_DQK), F32),
                        pltpu.VMEM((M_HEADS, 1, 128), F32)],
        compiler_params=_cparams(("parallel", "arbitrary")),
        name="mlstm",
    )(big, big, big, big, small, conv_w, gate_row, norm_w)


def _split3_dot(lhs01, x):
    hi = x.astype(BF16)
    r1 = x - hi.astype(F32)
    mid = r1.astype(BF16)
    lo = (r1 - mid.astype(F32)).astype(BF16)
    return (jnp.dot(lhs01, hi, preferred_element_type=F32)
            + jnp.dot(lhs01, mid, preferred_element_type=F32)
            + jnp.dot(lhs01, lo, preferred_element_type=F32))


def _gla_body(q_ref, k_ref, v_ref, z_ref, small_ref, aw_ref, ab_ref, nw_ref, hb_ref, st_scr):
    rows = q_ref.shape[0]
    n_chunks = rows // G_CHUNK
    kcols = G_HEADS * G_DK

    @pl.when(pl.program_id(1) == 0)
    def _():
        st_scr[...] = jnp.zeros(st_scr.shape, F32)

    logits = jnp.dot(small_ref[...].astype(BF16), aw_ref[...], preferred_element_type=F32)
    la = _log_sigmoid(logits + ab_ref[...]) / G_TAU
    r_io = lax.broadcasted_iota(I32, (rows, rows), 0)
    c_io = lax.broadcasted_iota(I32, (rows, rows), 1)
    chunk_causal = jnp.logical_and(r_io >= c_io, r_io // G_CHUNK == c_io // G_CHUNK)
    bc = _split3_dot(jnp.where(chunk_causal, 1.0, 0.0).astype(BF16), la)
    gcs = [bc[(ci + 1) * G_CHUNK - 1:(ci + 1) * G_CHUNK, :] for ci in range(n_chunks)]
    gc_rows = jnp.concatenate([jnp.broadcast_to(g, (G_CHUNK, kcols)) for g in gcs], axis=0)

    q = q_ref[...].astype(F32) * (G_DK ** -0.5)
    k = k_ref[...].astype(F32)
    q_in = (q * jnp.exp(bc)).astype(BF16)
    k_in = (k * jnp.exp(-bc)).astype(BF16)
    k_out = (k * jnp.exp(gc_rows - bc)).astype(BF16)

    for h in range(G_HEADS):
        ks = slice(h * G_DK, (h + 1) * G_DK)
        vs = slice(h * G_DV, (h + 1) * G_DV)
        v = v_ref[:, vs]
        att = lax.dot_general(q_in[:, ks], k_in[:, ks], (((1,), (1,)), ((), ())),
                              preferred_element_type=F32)
        att = jnp.where(chunk_causal, att, 0.0).astype(BF16)
        o_intra = jnp.dot(att, v, preferred_element_type=F32)
        st = st_scr[h]
        outs = []
        for ci in range(n_chunks):
            rs = slice(ci * G_CHUNK, (ci + 1) * G_CHUNK)
            o_inter = lax.dot_general(q_in[rs, ks], st.astype(BF16), (((1,), (1,)), ((), ())),
                                      preferred_element_type=F32)
            outs.append(o_intra[rs, :] + o_inter)
            st = jnp.exp(gcs[ci][:, ks]) * st + lax.dot_general(
                v[rs, :], k_out[rs, ks], (((0,), (0,)), ((), ())), preferred_element_type=F32)
        st_scr[h] = st
        o = jnp.concatenate(outs, axis=0)
        y = o * lax.rsqrt(jnp.mean(o * o, axis=-1, keepdims=True) + EPS)
        y = y * nw_ref[:, vs]
        hb_ref[:, vs] = (y * _silu(z_ref[:, vs].astype(F32))).astype(BF16)


def _gla(big, small, alpha_full, alpha_b, norm_w, bsz, seq):
    t = big.shape[0]
    rows = min(G_BLOCK_ROWS, seq)
    nb = seq // rows
    kcols = G_HEADS * G_DK
    vcols = G_HEADS * G_DV
    row_map = lambda b, c: b * nb + c
    return pl.pallas_call(
        _gla_body,
        grid=(bsz, nb),
        in_specs=[pl.BlockSpec((rows, kcols), lambda b, c: (row_map(b, c), COL_QG // kcols)),
                  pl.BlockSpec((rows, kcols), lambda b, c: (row_map(b, c), COL_KG // kcols)),
                  pl.BlockSpec((rows, vcols), lambda b, c: (row_map(b, c), COL_VG // vcols)),
                  pl.BlockSpec((rows, vcols), lambda b, c: (row_map(b, c), COL_ZG // vcols)),
                  pl.BlockSpec((rows, SMALL_COLS), lambda b, c: (row_map(b, c), 0)),
                  pl.BlockSpec((SMALL_COLS, kcols), lambda b, c: (0, 0)),
                  pl.BlockSpec((1, kcols), lambda b, c: (0, 0)),
                  pl.BlockSpec((1, vcols), lambda b, c: (0, 0))],
        out_specs=pl.BlockSpec((rows, vcols), lambda b, c: (row_map(b, c), 0)),
        out_shape=jax.ShapeDtypeStruct((t, vcols), BF16),
        scratch_shapes=[pltpu.VMEM((G_HEADS, G_DV, G_DK), F32)],
        compiler_params=_cparams(("parallel", "arbitrary")),
        name="gla",
    )(big, big, big, big, small, alpha_full, alpha_b, norm_w)


def _mixout_body(h_ref, ha_ref, hb_ref, ga_ref, gb_ref, gt_ref, wpa_ref, wpb_ref, wo_ref, o_ref):
    a = jnp.dot(ha_ref[...], wpa_ref[...], preferred_element_type=F32)
    b = jnp.dot(hb_ref[...], wpb_ref[...], preferred_element_type=F32)
    y = (jax.nn.sigmoid(ga_ref[...].astype(F32)) * a
         + jax.nn.sigmoid(gb_ref[...].astype(F32)) * b)
    o_ref[...] = h_ref[...] + gt_ref[0] * jnp.dot(y.astype(BF16), wo_ref[...],
                                                  preferred_element_type=F32)


def _mixout(h, ha, hb, big, gt, w_pa, w_pb, w_o, seq):
    t, d = h.shape
    tm = min(MIX_ROWS, seq)
    per_b = seq // tm
    wspec = pl.BlockSpec((d, d), lambda i: (0, 0))
    return pl.pallas_call(
        _mixout_body,
        grid=(t // tm,),
        in_specs=[pl.BlockSpec((tm, d), lambda i: (i, 0)),
                  pl.BlockSpec((tm, d), lambda i: (i, 0)),
                  pl.BlockSpec((tm, d), lambda i: (i, 0)),
                  pl.BlockSpec((tm, d), lambda i: (i, COL_GA // d)),
                  pl.BlockSpec((tm, d), lambda i: (i, COL_GB // d)),
                  pl.BlockSpec((1, 1, d), lambda i: (i // per_b, 0, 0)),
                  wspec, wspec, wspec],
        out_specs=pl.BlockSpec((tm, d), lambda i: (i, 0)),
        out_shape=jax.ShapeDtypeStruct((t, d), F32),
        compiler_params=_cparams(("parallel",)),
        name="mix_out",
    )(h, ha, hb, big, big, gt, w_pa, w_pb, w_o)


def _route_body(h_ref, sh_ref, sc_ref, nw_ref, rwt_ref, rb_ref,
                u_ref, eidx_ref, wts_ref, pos_ref, cnt_ref, carry_scr):
    tm = h_ref.shape[0]

    @pl.when(pl.program_id(0) == 0)
    def _():
        carry_scr[...] = jnp.zeros(carry_scr.shape, F32)

    u = _modulated_rmsnorm(h_ref[...], nw_ref[...], sc_ref[0], sh_ref[0])
    half = u.shape[1] // 2
    u_ref[...] = _pack_bf16_pair(u[:, :half], u[:, half:])
    logits = lax.dot_general(rwt_ref[...], u, (((1,), (1,)), ((), ())),
                             precision=HIGHEST, preferred_element_type=F32)
    scores = jax.nn.sigmoid(logits)
    sel = scores + rb_ref[...]

    neg = -jnp.inf
    sub_io = lax.broadcasted_iota(I32, (GROUP_SIZE, tm), 0)
    pieces = []
    for g in range(N_GROUPS):
        blk = sel[g * GROUP_SIZE:(g + 1) * GROUP_SIZE, :]
        m1 = jnp.max(blk, axis=0, keepdims=True)
        first = jnp.min(jnp.where(blk == m1, sub_io, GROUP_SIZE), axis=0, keepdims=True)
        m2 = jnp.max(jnp.where(sub_io == first, neg, blk), axis=0, keepdims=True)
        pieces.append(jnp.broadcast_to(m1 + m2, (GROUP_SIZE, tm)))
    gscore = jnp.concatenate(pieces, axis=0)

    e_io = lax.broadcasted_iota(I32, (N_EXPERTS, tm), 0)
    grp_io = e_io // GROUP_SIZE
    gmask = jnp.zeros((N_EXPERTS, tm), jnp.bool_)
    for _ in range(TOPK_GROUPS):
        mx = jnp.max(gscore, axis=0, keepdims=True)
        gi = jnp.min(jnp.where(gscore == mx, grp_io, N_GROUPS), axis=0, keepdims=True)
        hit = grp_io == gi
        gmask = jnp.logical_or(gmask, hit)
        gscore = jnp.where(hit, neg, gscore)

    cur = jnp.where(gmask, sel, neg)
    row_io = lax.broadcasted_iota(I32, (TOP_K, tm), 0)
    eidx = jnp.zeros((TOP_K, tm), I32)
    wraw = jnp.zeros((TOP_K, tm), F32)
    chosen = jnp.zeros((N_EXPERTS, tm), jnp.bool_)
    hits = []
    for kk in range(TOP_K):
        mx = jnp.max(cur, axis=0, keepdims=True)
        ei = jnp.min(jnp.where(cur == mx, e_io, N_EXPERTS), axis=0, keepdims=True)
        hit = e_io == ei
        hits.append(hit)
        sc_k = jnp.sum(jnp.where(hit, scores, 0.0), axis=0, keepdims=True)
        eidx = jnp.where(row_io == kk, ei, eidx)
        wraw = jnp.where(row_io == kk, sc_k, wraw)
        chosen = jnp.logical_or(chosen, hit)
        cur = jnp.where(hit, neg, cur)

    wsum = jnp.sum(wraw, axis=0, keepdims=True)
    wts_ref[...] = wraw / wsum * ROUTED_SCALE
    eidx_ref[...] = eidx

    chosen_f = jnp.where(chosen, 1.0, 0.0)
    r_io = lax.broadcasted_iota(I32, (tm, tm), 0)
    c_io = lax.broadcasted_iota(I32, (tm, tm), 1)
    strict_upper = jnp.where(r_io < c_io, 1.0, 0.0).astype(BF16)
    prefix = jnp.dot(chosen_f.astype(BF16), strict_upper, preferred_element_type=F32)
    rank = prefix + carry_scr[:, 0:1]
    pos = jnp.zeros((TOP_K, tm), F32)
    for kk in range(TOP_K):
        p_k = jnp.sum(jnp.where(hits[kk], rank, 0.0), axis=0, keepdims=True)
        pos = jnp.where(row_io == kk, p_k, pos)
    pos_ref[...] = pos.astype(I32)
    total = carry_scr[...] + jnp.sum(chosen_f, axis=1, keepdims=True)
    carry_scr[...] = total
    cnt_ref[...] = total.astype(I32)


def _route(h, sh, sc, nw, rw_t, rb_col, seq):
    t, d = h.shape
    tm = min(ROUTE_ROWS, seq)
    per_b = seq // tm
    kspec = pl.BlockSpec((TOP_K, tm), lambda i: (0, i))
    return pl.pallas_call(
        _route_body,
        grid=(t // tm,),
        in_specs=[pl.BlockSpec((tm, d), lambda i: (i, 0)),
                  pl.BlockSpec((1, 1, d), lambda i: (i // per_b, 0, 0)),
                  pl.BlockSpec((1, 1, d), lambda i: (i // per_b, 0, 0)),
                  pl.BlockSpec((1, d), lambda i: (0, 0)),
                  pl.BlockSpec((N_EXPERTS, d), lambda i: (0, 0)),
                  pl.BlockSpec((N_EXPERTS, 1), lambda i: (0, 0))],
        out_specs=[pl.BlockSpec((tm, d // 2), lambda i: (i, 0)), kspec, kspec, kspec,
                   pl.BlockSpec((N_EXPERTS, 128), lambda i: (0, 0))],
        out_shape=[jax.ShapeDtypeStruct((t, d // 2), U32),
                   jax.ShapeDtypeStruct((TOP_K, t), I32),
                   jax.ShapeDtypeStruct((TOP_K, t), F32),
                   jax.ShapeDtypeStruct((TOP_K, t), I32),
                   jax.ShapeDtypeStruct((N_EXPERTS, 128), I32)],
        scratch_shapes=[pltpu.VMEM((N_EXPERTS, 128), F32)],
        compiler_params=_cparams(("arbitrary",)),
        name="moe_route",
    )(h, sh, sc, nw, rw_t, rb_col)


def _sc_worker_id():
    return lax.axis_index("s") * SC_CORES + lax.axis_index("c")


def _sc_scatter_rows(x, dest, n_rows):
    t, d = x.shape
    n_k = dest.shape[0]
    assert t % (SC_WORKERS * 2 * SC_CHUNK) == 0
    per_w = t // SC_WORKERS
    n_ch = per_w // SC_CHUNK
    dest4 = dest.reshape(n_k, SC_WORKERS, n_ch, SC_CHUNK).transpose(1, 2, 0, 3)
    mesh = plsc.VectorSubcoreMesh(core_axis_name="c", subcore_axis_name="s")

    @functools.partial(
        pl.kernel, mesh=mesh,
        out_type=jax.ShapeDtypeStruct((n_rows, d), x.dtype),
        scratch_types=[pltpu.VMEM((n_ch, n_k, SC_CHUNK), I32),
                       pltpu.VMEM((SC_CHUNK, d), x.dtype),
                       pltpu.VMEM((SC_CHUNK, d), x.dtype)] + [pltpu.SemaphoreType.DMA] * 4,
        name="moe_dispatch_sc",
    )
    def scatter_kernel(x_hbm, dest_hbm, out_hbm, idx_v, rows0, rows1, l0, l1, s0, s1):
        wid = _sc_worker_id()
        base = wid * per_w
        pltpu.sync_copy(dest_hbm.at[wid], idx_v)
        bufs = ((rows0, l0, s0), (rows1, l1, s1))

        def load(ci, b):
            rows, load_sem, _ = bufs[b]
            return pltpu.make_async_copy(x_hbm.at[pl.ds(base + ci * SC_CHUNK, SC_CHUNK)], rows,
                                         load_sem)

        def scatters(ci, b):
            rows, _, scatter_sem = bufs[b]
            return [pltpu.make_async_copy(rows, out_hbm.at[idx_v.at[ci, j]], scatter_sem)
                    for j in range(n_k)]

        load(0, 0).start()

        @pl.loop(0, n_ch, step=2)
        def _(ci):
            load(ci, 0).wait()
            for cp in scatters(ci, 0):
                cp.start()

            @pl.when(ci > 0)
            def _():
                for cp in scatters(ci - 1, 1):
                    cp.wait()

            load(ci + 1, 1).start()
            load(ci + 1, 1).wait()
            for cp in scatters(ci + 1, 1):
                cp.start()
            for cp in scatters(ci, 0):
                cp.wait()

            @pl.when(ci + 2 < n_ch)
            def _():
                load(ci + 2, 0).start()

        for cp in scatters(n_ch - 1, 1):
            cp.wait()

    return scatter_kernel(x, dest4)


def _sc_gather_rows(table, idx):
    n, d = idx.shape[0], table.shape[1]
    assert n % (SC_WORKERS * 2 * SC_CHUNK) == 0
    per_w = n // SC_WORKERS
    n_ch = per_w // SC_CHUNK
    idx3 = idx.reshape(SC_WORKERS, n_ch, SC_CHUNK)
    mesh = plsc.VectorSubcoreMesh(core_axis_name="c", subcore_axis_name="s")

    @functools.partial(
        pl.kernel, mesh=mesh,
        out_type=jax.ShapeDtypeStruct((n, d), table.dtype),
        scratch_types=[pltpu.VMEM((n_ch, SC_CHUNK), I32),
                       pltpu.VMEM((SC_CHUNK, d), table.dtype),
                       pltpu.VMEM((SC_CHUNK, d), table.dtype)] + [pltpu.SemaphoreType.DMA] * 4,
        name="moe_combine_sc",
    )
    def gather_kernel(table_hbm, idx_hbm, out_hbm, idx_v, rows0, rows1, g0, g1, w0, w1):
        wid = _sc_worker_id()
        base = wid * per_w
        pltpu.sync_copy(idx_hbm.at[wid], idx_v)
        bufs = ((rows0, g0, w0), (rows1, g1, w1))

        def gather(ci, b):
            rows, gather_sem, _ = bufs[b]
            return pltpu.make_async_copy(table_hbm.at[idx_v.at[ci]], rows, gather_sem)

        def write(ci, b):
            rows, _, write_sem = bufs[b]
            return pltpu.make_async_copy(rows, out_hbm.at[pl.ds(base + ci * SC_CHUNK, SC_CHUNK)],
                                         write_sem)

        gather(0, 0).start()

        @pl.loop(0, n_ch, step=2)
        def _(ci):
            gather(ci, 0).wait()
            write(ci, 0).start()

            @pl.when(ci > 0)
            def _():
                write(ci - 1, 1).wait()

            gather(ci + 1, 1).start()
            gather(ci + 1, 1).wait()
            write(ci + 1, 1).start()
            write(ci, 0).wait()

            @pl.when(ci + 2 < n_ch)
            def _():
                gather(ci + 2, 0).start()

        write(n_ch - 1, 1).wait()

    return gather_kernel(table, idx3)


def _expert_body(blk_e_ref, blk_first_ref, blk_valid_ref, x_ref, w1_ref, w3_ref, w2_ref, y_ref,
                 w1_scr, w3_scr, w2_scr):
    del blk_e_ref
    j = pl.program_id(0)
    valid = blk_valid_ref[j]

    @pl.when(blk_first_ref[j] == 1)
    def _():
        w1_scr[...] = w1_ref[0].astype(BF16)
        w3_scr[...] = w3_ref[0].astype(BF16)
        w2_scr[...] = w2_ref[0].astype(BF16)

    @pl.when(valid > 0)
    def _():
        half = x_ref.shape[1]
        rows = lax.broadcasted_iota(I32, x_ref.shape, 0)
        lo, hi = _unpack_bf16_pair(jnp.where(rows < valid, x_ref[...], 0))
        lo, hi = lo.astype(BF16), hi.astype(BF16)

        def proj(w_scr):
            return (jnp.dot(lo, w_scr[0:half, :], preferred_element_type=F32)
                    + jnp.dot(hi, w_scr[half:2 * half, :], preferred_element_type=F32))

        hid = _silu(proj(w1_scr)) * proj(w3_scr)
        y = jnp.dot(hid.astype(BF16), w2_scr[...], preferred_element_type=F32)
        y_ref[...] = _pack_bf16_pair(y[:, :half], y[:, half:])

    @pl.when(valid == 0)
    def _():
        y_ref[...] = jnp.zeros(y_ref.shape, U32)


def _experts(xg, blk_e, blk_first, blk_valid, w1, w3, w2, layer):
    n_rows, half = xg.shape
    n_blocks = n_rows // EXPERT_ROWS
    d, de = w1.shape[-2:]
    grid_spec = pltpu.PrefetchScalarGridSpec(
        num_scalar_prefetch=3,
        grid=(n_blocks,),
        in_specs=[pl.BlockSpec((EXPERT_ROWS, half), lambda j, be, bf, bv: (j, 0)),
                  pl.BlockSpec((None, 1, d, de), lambda j, be, bf, bv: (layer, be[j], 0, 0)),
                  pl.BlockSpec((None, 1, d, de), lambda j, be, bf, bv: (layer, be[j], 0, 0)),
                  pl.BlockSpec((None, 1, de, d), lambda j, be, bf, bv: (layer, be[j], 0, 0))],
        out_specs=pl.BlockSpec((EXPERT_ROWS, half), lambda j, be, bf, bv: (j, 0)),
        scratch_shapes=[pltpu.VMEM((d, de), BF16), pltpu.VMEM((d, de), BF16),
                        pltpu.VMEM((de, d), BF16)],
    )
    return pl.pallas_call(
        _expert_body,
        grid_spec=grid_spec,
        out_shape=jax.ShapeDtypeStruct((n_rows, half), U32),
        compiler_params=_cparams(("arbitrary",)),
        name="moe_experts",
    )(blk_e, blk_first, blk_valid, xg, w1, w3, w2)


def _combine_body(h_ref, u_ref, yg_ref, wts_ref, gt_ref, s1_ref, s3_ref, s2_ref, fw_ref, o_ref, *,
                  final_norm):
    half = u_ref.shape[1]
    lo, hi = _unpack_bf16_pair(u_ref[...])
    lo, hi = lo.astype(BF16), hi.astype(BF16)

    def proj(w_ref):
        return (jnp.dot(lo, w_ref[0:half, :], preferred_element_type=F32)
                + jnp.dot(hi, w_ref[half:2 * half, :], preferred_element_type=F32))

    hid = _silu(proj(s1_ref)) * proj(s3_ref)
    shared = jnp.dot(hid.astype(BF16), s2_ref[...], preferred_element_type=F32)

    routed_lo = routed_hi = None
    for kk in range(TOP_K):
        y_lo, y_hi = _unpack_bf16_pair(yg_ref[kk])
        w = wts_ref[:, kk:kk + 1]
        routed_lo = y_lo * w if routed_lo is None else routed_lo + y_lo * w
        routed_hi = y_hi * w if routed_hi is None else routed_hi + y_hi * w
    gt = gt_ref[0]
    out_lo = h_ref[:, 0:half] + gt[:, 0:half] * (routed_lo + shared[:, 0:half])
    out_hi = (h_ref[:, half:2 * half]
              + gt[:, half:2 * half] * (routed_hi + shared[:, half:2 * half]))
    if final_norm:
        ssq = (jnp.sum(out_lo * out_lo, axis=-1, keepdims=True)
               + jnp.sum(out_hi * out_hi, axis=-1, keepdims=True))
        inv = lax.rsqrt(ssq / (2 * half) + EPS)
        out_lo = (out_lo * inv) * fw_ref[:, 0:half]
        out_hi = (out_hi * inv) * fw_ref[:, half:2 * half]
    o_ref[:, 0:half] = out_lo
    o_ref[:, half:2 * half] = out_hi


def _combine(h, u, yg, wts_tk, gt, s1, s3, s2, final_w, seq, final_norm, part, n_parts):
    t, d = h.shape
    half = d // 2
    tm = min(COMBINE_ROWS, seq)
    per_b = seq // tm
    ds_ = s1.shape[-1]
    steps = t // n_parts // tm
    off = part * steps
    return pl.pallas_call(
        functools.partial(_combine_body, final_norm=final_norm),
        grid=(steps,),
        in_specs=[pl.BlockSpec((tm, d), lambda i: (i + off, 0)),
                  pl.BlockSpec((tm, half), lambda i: (i + off, 0)),
                  pl.BlockSpec((TOP_K, tm, half), lambda i: (0, i, 0)),
                  pl.BlockSpec((tm, TOP_K), lambda i: (i + off, 0)),
                  pl.BlockSpec((1, 1, d), lambda i: ((i + off) // per_b, 0, 0)),
                  pl.BlockSpec((d, ds_), lambda i: (0, 0)),
                  pl.BlockSpec((d, ds_), lambda i: (0, 0)),
                  pl.BlockSpec((ds_, d), lambda i: (0, 0)),
                  pl.BlockSpec((1, d), lambda i: (0, 0))],
        out_specs=pl.BlockSpec((tm, d), lambda i: (i + off, 0)),
        out_shape=jax.ShapeDtypeStruct((t, d), F32),
        input_output_aliases={0: 0},
        compiler_params=_cparams(("parallel",)),
        name="moe_combine",
    )(h, u, yg, wts_tk, gt, s1, s3, s2, final_w)


def _split_w_in(w_in):
    sizes = (2 * M_HEADS * M_DQK, M_HEADS * M_DV, M_HEADS * M_DV, M_HEADS, M_HEADS,
             G_HEADS * G_DK, G_HEADS * G_DK, G_HEADS * G_DV, G_RANK, G_HEADS * G_DV,
             D_MODEL, D_MODEL)
    offs = [0]
    for n in sizes:
        offs.append(offs[-1] + n)
    w16 = w_in.astype(BF16)
    big = jnp.concatenate([w16[:, offs[0]:offs[3]], w16[:, offs[5]:offs[8]], w16[:, offs[9]:offs[12]]],
                          axis=1)
    pad = jnp.zeros((w_in.shape[0], SMALL_COLS - 2 * M_HEADS - G_RANK), BF16)
    small = jnp.concatenate([w16[:, offs[3]:offs[5]], w16[:, offs[8]:offs[9]], pad], axis=1)
    return big, small


def _moe_layout(counts, eidx, pos, n_blocks):
    padded = (counts + EXPERT_ROWS - 1) // EXPERT_ROWS * EXPERT_ROWS
    pend = jnp.cumsum(padded)
    pstart = pend - padded
    experts = jnp.arange(N_EXPERTS, dtype=I32)
    dest = pos + jnp.sum(jnp.where(eidx[..., None] == experts, pstart, 0), axis=-1)
    blk_start = jnp.arange(n_blocks, dtype=I32) * EXPERT_ROWS
    owner = jnp.sum((pend[None, :] <= blk_start[:, None]).astype(I32), axis=1)
    blk_e = jnp.minimum(owner, N_EXPERTS - 1)
    prev = jnp.concatenate([jnp.full((1,), -1, I32), blk_e[:-1]])
    blk_first = (blk_e != prev).astype(I32)
    own = blk_e[:, None] == experts[None, :]
    rows_left = jnp.sum(jnp.where(own, (pstart + counts)[None, :], 0), axis=1) - blk_start
    blk_valid = jnp.clip(jnp.where(owner < N_EXPERTS, rows_left, 0), 0, EXPERT_ROWS)
    return dest.astype(I32), blk_e.astype(I32), blk_first, blk_valid.astype(I32)


def kernel(x, c, ada_w, ada_b, norm_mix_w, norm_moe_w, w_in, m_conv_w, m_gate_b, m_norm_w,
           g_alpha_w, g_alpha_b, g_norm_w, w_pa, w_pb, w_o, router_w, router_b,
           exp_w1, exp_w3, exp_w2, sh_w1, sh_w3, sh_w2, final_norm_w):
    bsz, seq, d = x.shape
    depth = ada_w.shape[0]
    t = bsz * seq
    n_rows = t * TOP_K + N_EXPERTS * EXPERT_ROWS
    n_blocks = n_rows // EXPERT_ROWS

    ada = _ada(c, ada_w, ada_b).reshape(depth, bsz, 6, 1, d)
    h = x.reshape(t, d)
    for l in range(depth):
        sh1, sc1, gt1, sh2, sc2, gt2 = (ada[l, :, i] for i in range(6))

        w_big, w_small = _split_w_in(w_in[l])
        big, small = _inproj(h, sh1, sc1, norm_mix_w[l][None, :], w_big, w_small, seq)
        gate_row = jnp.zeros((1, SMALL_COLS), F32)
        gate_row = gate_row.at[0, SMALL_I:SMALL_I + M_HEADS].set(m_gate_b[l, 0])
        gate_row = gate_row.at[0, SMALL_F:SMALL_F + M_HEADS].set(m_gate_b[l, 1])
        ha = _mlstm(big, small, m_conv_w[l], gate_row, m_norm_w[l][None, :], bsz, seq)
        alpha_full = jnp.zeros((SMALL_COLS, G_HEADS * G_DK), F32)
        alpha_full = alpha_full.at[SMALL_R:SMALL_R + G_RANK].set(g_alpha_w[l]).astype(BF16)
        hb = _gla(big, small, alpha_full, g_alpha_b[l][None, :], g_norm_w[l][None, :], bsz, seq)
        h = _mixout(h, ha, hb, big, gt1, w_pa[l].astype(BF16), w_pb[l].astype(BF16),
                    w_o[l].astype(BF16), seq)

        u, eidx, wts, pos, cnt = _route(h, sh2, sc2, norm_moe_w[l][None, :],
                                        router_w[l].T, router_b[l][:, None], seq)
        dest, blk_e, blk_first, blk_valid = _moe_layout(cnt[:, 0], eidx, pos, n_blocks)
        xg = _sc_scatter_rows(u, dest, n_rows)
        y = _experts(xg, blk_e, blk_first, blk_valid, exp_w1, exp_w3, exp_w2, l)
        tp = t // COMBINE_PARTS
        for p in range(COMBINE_PARTS):
            dest_p = dest[:, p * tp:(p + 1) * tp].reshape(-1)
            yg = _sc_gather_rows(y, dest_p).reshape(TOP_K, tp, d // 2)
            h = _combine(h, u, yg, wts.T, gt2, sh_w1[l].astype(BF16), sh_w3[l].astype(BF16),
                         sh_w2[l].astype(BF16), final_norm_w[None, :], seq,
                         final_norm=(l == depth - 1), part=p, n_parts=COMBINE_PARTS)

    return h.reshape(bsz, seq, d)
```

```python
import functools

import jax
import jax.numpy as jnp
import numpy as np
from jax import lax
from jax.experimental import pallas as pl
from jax.experimental.pallas import tpu as pltpu
from jax.experimental.pallas import tpu_sc as plsc

F32 = jnp.float32
BF16 = jnp.bfloat16
I32 = jnp.int32
U32 = jnp.uint32
HI_MASK = np.uint32(0xFFFF0000)
HIGHEST = lax.Precision.HIGHEST

SC_CORES = 2
SC_SUBCORES = 16
SC_WORKERS = SC_CORES * SC_SUBCORES
SC_CHUNK = 64

D_MODEL = 1024
M_HEADS = 4
M_DQK = 128
M_DV = 256
M_CONV = 4
GATE_CAP = 15.0
G_HEADS = 4
G_DK = 128
G_DV = 256
G_RANK = 16
G_TAU = 16.0
G_CHUNK = 64
N_EXPERTS = 64
TOP_K = 8
N_GROUPS = 8
GROUP_SIZE = N_EXPERTS // N_GROUPS
TOPK_GROUPS = 4
D_EXPERT = 256
D_SHARED = 256
ROUTED_SCALE = 2.5
EPS = 1e-6

M_CHUNK_ROWS = 256
G_BLOCK_ROWS = 256
MIX_ROWS = 512
ROUTE_ROWS = 512
EXPERT_ROWS = 1024
COMBINE_ROWS = 512
COMBINE_PARTS = 2
CONV_HALO = 8
VMEM_LIMIT = 48 * 1024 * 1024

COL_QM, COL_OM = 0, 2048
COL_QG, COL_ZG, COL_GA, COL_GB = 3072, 5120, 6144, 7168
BIG_COLS = 8192
SMALL_COLS = 128
SMALL_I, SMALL_F, SMALL_R = 0, M_HEADS, 2 * M_HEADS


def _cparams(sem, vmem=VMEM_LIMIT):
    return pltpu.CompilerParams(dimension_semantics=sem, vmem_limit_bytes=vmem)


def _silu(x):
    return x * jax.nn.sigmoid(x)


def _log_sigmoid(x):
    return jnp.minimum(x, 0.0) - jnp.log1p(jnp.exp(-jnp.abs(x)))


def _modulated_rmsnorm(x, w, sc, sh):
    y = x * lax.rsqrt(jnp.mean(x * x, axis=-1, keepdims=True) + EPS)
    return (y * w) * (1.0 + sc) + sh


def _pack_bf16_pair(lo, hi):
    lo_bits = lax.bitcast_convert_type(lo.astype(BF16).astype(F32), U32)
    hi_bits = lax.bitcast_convert_type(hi.astype(BF16).astype(F32), U32)
    return (lo_bits >> 16) | (hi_bits & HI_MASK)


def _unpack_bf16_pair(packed):
    lo = lax.bitcast_convert_type(packed << 16, F32)
    hi = lax.bitcast_convert_type(packed & HI_MASK, F32)
    return lo, hi


def _lower_tri(n, dtype):
    r = lax.broadcasted_iota(I32, (n, n), 0)
    c = lax.broadcasted_iota(I32, (n, n), 1)
    return (r >= c).astype(dtype)


def _split3_dot(lhs01, x):
    hi = x.astype(BF16)
    r1 = x - hi.astype(F32)
    mid = r1.astype(BF16)
    lo = (r1 - mid.astype(F32)).astype(BF16)
    return (jnp.dot(lhs01, hi, preferred_element_type=F32)
            + jnp.dot(lhs01, mid, preferred_element_type=F32)
            + jnp.dot(lhs01, lo, preferred_element_type=F32))


def _ada_body(c_ref, w_ref, b_ref, o_ref):
    cond = _silu(c_ref[...])
    o_ref[0] = jnp.dot(cond.astype(BF16), w_ref[0].astype(BF16),
                       preferred_element_type=F32) + b_ref[0]


def _ada(c, ada_w, ada_b):
    depth, d, six_d = ada_w.shape
    bsz = c.shape[0]
    nj = six_d // d
    return pl.pallas_call(
        _ada_body,
        grid=(depth, nj),
        in_specs=[pl.BlockSpec((bsz, d), lambda l, j: (0, 0)),
                  pl.BlockSpec((1, d, d), lambda l, j: (l, 0, j)),
                  pl.BlockSpec((1, 1, d), lambda l, j: (l, 0, j))],
        out_specs=pl.BlockSpec((1, bsz, d), lambda l, j: (l, 0, j)),
        out_shape=jax.ShapeDtypeStruct((depth, bsz, six_d), F32),
        compiler_params=_cparams(("parallel", "parallel")),
        name="ada_ln",
    )(c, ada_w, ada_b.reshape(depth, 1, six_d))


def _mlstm_body(h_ref, sh_ref, sc_ref, mw_ref, w_ref, ws_ref, cw_ref, gb_ref, nw_ref, ha_ref,
                xe_scr, c_scr, n_scr, m_scr):
    rows = h_ref.shape[0]
    half = M_HEADS * M_DQK
    vcols = M_HEADS * M_DV

    @pl.when(pl.program_id(1) == 0)
    def _():
        xe_scr[0:CONV_HALO, :] = jnp.zeros((CONV_HALO, 2 * half), F32)
        c_scr[...] = jnp.zeros(c_scr.shape, F32)
        n_scr[...] = jnp.zeros(n_scr.shape, F32)
        m_scr[...] = jnp.zeros(m_scr.shape, F32)

    u = _modulated_rmsnorm(h_ref[...], mw_ref[...], sc_ref[0], sh_ref[0]).astype(BF16)
    v_all = jnp.dot(u, w_ref[:, 2 * half:2 * half + vcols], preferred_element_type=F32).astype(BF16)
    o_all = jnp.dot(u, w_ref[:, 2 * half + vcols:2 * half + 2 * vcols], preferred_element_type=F32)
    small = jnp.dot(u, ws_ref[...], preferred_element_type=F32)

    xe_scr[CONV_HALO:CONV_HALO + rows, :] = jnp.dot(u, w_ref[:, 0:2 * half],
                                                     preferred_element_type=F32)
    conv = None
    for j in range(M_CONV):
        off = CONV_HALO - (M_CONV - 1) + j
        term = xe_scr[off:off + rows, :] * cw_ref[j:j + 1, :]
        conv = term if conv is None else conv + term
    qk = _silu(conv)
    xe_scr[0:CONV_HALO, :] = xe_scr[rows:rows + CONV_HALO, :]

    capped = GATE_CAP * jnp.tanh((small + gb_ref[...]) / GATE_CAP)
    li_all = capped
    lf_all = _log_sigmoid(capped)
    b_all = _split3_dot(_lower_tri(rows, BF16), lf_all)
    li_t = li_all.T
    b_t = b_all.T

    r_io = lax.broadcasted_iota(I32, (rows, rows), 0)
    c_io = lax.broadcasted_iota(I32, (rows, rows), 1)
    causal = r_io >= c_io

    for h in range(M_HEADS):
        q = (qk[:, h * M_DQK:(h + 1) * M_DQK] * (M_DQK ** -0.5)).astype(BF16)
        k = qk[:, half + h * M_DQK:half + (h + 1) * M_DQK]
        kb = k.astype(BF16)
        v = v_all[:, h * M_DV:(h + 1) * M_DV]
        li_c = li_all[:, SMALL_I + h:SMALL_I + h + 1]
        b_c = b_all[:, SMALL_F + h:SMALL_F + h + 1]
        li_r = li_t[SMALL_I + h:SMALL_I + h + 1, :]
        b_r = b_t[SMALL_F + h:SMALL_F + h + 1, :]
        g = b_c[rows - 1:rows, :]
        m_prev = m_scr[h][:, 0:1]
        c_prev = c_scr[h]
        n_prev = n_scr[h]

        d_mat = jnp.where(causal, b_c - b_r + li_r, -jnp.inf)
        m_inter = b_c + m_prev
        m_t = jnp.maximum(jnp.max(d_mat, axis=1, keepdims=True), m_inter)
        s = lax.dot_general(q, kb, (((1,), (1,)), ((), ())), preferred_element_type=F32)
        p = jnp.exp(d_mat - m_t) * s
        w_inter = jnp.exp(m_inter - m_t)
        num = (jnp.dot(p.astype(BF16), v, preferred_element_type=F32)
               + w_inter * jnp.dot(q, c_prev.astype(BF16), preferred_element_type=F32))
        qn = jnp.sum(q.astype(F32) * n_prev, axis=1, keepdims=True)
        den = jnp.sum(p, axis=1, keepdims=True) + w_inter * qn
        hh = num / jnp.maximum(jnp.abs(den), jnp.exp(-m_t))

        a_r = g - b_r + li_r
        a_c = g - b_c + li_c
        m_new = jnp.maximum(g + m_prev, jnp.max(a_r, axis=1, keepdims=True))
        decay = jnp.exp(g + m_prev - m_new)
        wk = jnp.exp(a_c - m_new) * k
        c_scr[h] = decay * c_prev + lax.dot_general(
            wk.astype(BF16), v, (((0,), (0,)), ((), ())), preferred_element_type=F32)
        n_scr[h] = decay * n_prev + jnp.sum(wk, axis=0, keepdims=True)
        m_scr[h] = jnp.broadcast_to(m_new, m_scr.shape[1:])

        y = hh * lax.rsqrt(jnp.mean(hh * hh, axis=-1, keepdims=True) + EPS)
        y = y * nw_ref[:, h * M_DV:(h + 1) * M_DV]
        gate = jax.nn.sigmoid(o_all[:, h * M_DV:(h + 1) * M_DV])
        ha_ref[:, h * M_DV:(h + 1) * M_DV] = (y * gate).astype(BF16)


def _mlstm(h, sh, sc, mix_w, w_big, w_small, conv_w, gate_row, norm_w, bsz, seq):
    t, d = h.shape
    rows = min(M_CHUNK_ROWS, seq)
    nc = seq // rows
    half = M_HEADS * M_DQK
    vcols = M_HEADS * M_DV
    wcols = 2 * half + 2 * vcols
    assert COL_QM == 0 and COL_OM + vcols == wcols
    resident = pl.Buffered(1)
    return pl.pallas_call(
        _mlstm_body,
        grid=(bsz, nc),
        in_specs=[pl.BlockSpec((rows, d), lambda b, c: (b * nc + c, 0)),
                  pl.BlockSpec((1, 1, d), lambda b, c: (b, 0, 0)),
                  pl.BlockSpec((1, 1, d), lambda b, c: (b, 0, 0)),
                  pl.BlockSpec((1, d), lambda b, c: (0, 0)),
                  pl.BlockSpec((d, wcols), lambda b, c: (0, 0), pipeline_mode=resident),
                  pl.BlockSpec((d, SMALL_COLS), lambda b, c: (0, 0), pipeline_mode=resident),
                  pl.BlockSpec((M_CONV, 2 * half), lambda b, c: (0, 0)),
                  pl.BlockSpec((1, SMALL_COLS), lambda b, c: (0, 0)),
                  pl.BlockSpec((1, vcols), lambda b, c: (0, 0))],
        out_specs=pl.BlockSpec((rows, vcols), lambda b, c: (b * nc + c, 0)),
        out_shape=jax.ShapeDtypeStruct((t, vcols), BF16),
        scratch_shapes=[pltpu.VMEM((rows + CONV_HALO, 2 * half), F32),
                        pltpu.VMEM((M_HEADS, M_DQK, M_DV), F32),
                        pltpu.VMEM((M_HEADS, 1, M_DQK), F32),
                        pltpu.VMEM((M_HEADS, 1, 128), F32)],
        compiler_params=_cparams(("parallel", "arbitrary")),
        name="mlstm",
    )(h, sh, sc, mix_w, w_big, w_small, conv_w, gate_row, norm_w)


def _gla_body(h_ref, sh_ref, sc_ref, mw_ref, w_ref, ws_ref, aw_ref, ab_ref, nw_ref, hb_ref, st_scr):
    rows = h_ref.shape[0]
    n_chunks = rows // G_CHUNK
    kcols = G_HEADS * G_DK
    vcols = G_HEADS * G_DV

    @pl.when(pl.program_id(1) == 0)
    def _():
        st_scr[...] = jnp.zeros(st_scr.shape, F32)

    u = _modulated_rmsnorm(h_ref[...], mw_ref[...], sc_ref[0], sh_ref[0]).astype(BF16)
    q = jnp.dot(u, w_ref[:, 0:kcols], preferred_element_type=F32) * (G_DK ** -0.5)
    k = jnp.dot(u, w_ref[:, kcols:2 * kcols], preferred_element_type=F32)
    v_all = jnp.dot(u, w_ref[:, 2 * kcols:2 * kcols + vcols],
                    preferred_element_type=F32).astype(BF16)
    z_all = jnp.dot(u, w_ref[:, 2 * kcols + vcols:2 * kcols + 2 * vcols],
                    preferred_element_type=F32)
    small = jnp.dot(u, ws_ref[...], preferred_element_type=F32)

    logits = jnp.dot(small.astype(BF16), aw_ref[...], preferred_element_type=F32)
    la = _log_sigmoid(logits + ab_ref[...]) / G_TAU
    r_io = lax.broadcasted_iota(I32, (rows, rows), 0)
    c_io = lax.broadcasted_iota(I32, (rows, rows), 1)
    chunk_causal = jnp.logical_and(r_io >= c_io, r_io // G_CHUNK == c_io // G_CHUNK)
    bc = _split3_dot(jnp.where(chunk_causal, 1.0, 0.0).astype(BF16), la)
    gcs = [bc[(ci + 1) * G_CHUNK - 1:(ci + 1) * G_CHUNK, :] for ci in range(n_chunks)]
    gc_rows = jnp.concatenate([jnp.broadcast_to(g, (G_CHUNK, kcols)) for g in gcs], axis=0)

    q_in = (q * jnp.exp(bc)).astype(BF16)
    k_in = (k * jnp.exp(-bc)).astype(BF16)
    k_out = (k * jnp.exp(gc_rows - bc)).astype(BF16)

    for h in range(G_HEADS):
        ks = slice(h * G_DK, (h + 1) * G_DK)
        vs = slice(h * G_DV, (h + 1) * G_DV)
        v = v_all[:, vs]
        att = lax.dot_general(q_in[:, ks], k_in[:, ks], (((1,), (1,)), ((), ())),
                              preferred_element_type=F32)
        att = jnp.where(chunk_causal, att, 0.0).astype(BF16)
        o_intra = jnp.dot(att, v, preferred_element_type=F32)
        st = st_scr[h]
        outs = []
        for ci in range(n_chunks):
            rs = slice(ci * G_CHUNK, (ci + 1) * G_CHUNK)
            o_inter = lax.dot_general(q_in[rs, ks], st.astype(BF16), (((1,), (1,)), ((), ())),
                                      preferred_element_type=F32)
            outs.append(o_intra[rs, :] + o_inter)
            st = jnp.exp(gcs[ci][:, ks]) * st + lax.dot_general(
                v[rs, :], k_out[rs, ks], (((0,), (0,)), ((), ())), preferred_element_type=F32)
        st_scr[h] = st
        o = jnp.concatenate(outs, axis=0)
        y = o * lax.rsqrt(jnp.mean(o * o, axis=-1, keepdims=True) + EPS)
        y = y * nw_ref[:, vs]
        hb_ref[:, vs] = (y * _silu(z_all[:, vs])).astype(BF16)


def _gla(h, sh, sc, mix_w, w_big, w_small, alpha_full, alpha_b, norm_w, bsz, seq):
    t, d = h.shape
    rows = min(G_BLOCK_ROWS, seq)
    nb = seq // rows
    kcols = G_HEADS * G_DK
    vcols = G_HEADS * G_DV
    wcols = 2 * kcols + 2 * vcols
    assert COL_QG % wcols == 0 and COL_ZG + vcols == COL_QG + wcols
    resident = pl.Buffered(1)
    return pl.pallas_call(
        _gla_body,
        grid=(bsz, nb),
        in_specs=[pl.BlockSpec((rows, d), lambda b, c: (b * nb + c, 0)),
                  pl.BlockSpec((1, 1, d), lambda b, c: (b, 0, 0)),
                  pl.BlockSpec((1, 1, d), lambda b, c: (b, 0, 0)),
                  pl.BlockSpec((1, d), lambda b, c: (0, 0)),
                  pl.BlockSpec((d, wcols), lambda b, c: (0, COL_QG // wcols), pipeline_mode=resident),
                  pl.BlockSpec((d, SMALL_COLS), lambda b, c: (0, 0), pipeline_mode=resident),
                  pl.BlockSpec((SMALL_COLS, kcols), lambda b, c: (0, 0)),
                  pl.BlockSpec((1, kcols), lambda b, c: (0, 0)),
                  pl.BlockSpec((1, vcols), lambda b, c: (0, 0))],
        out_specs=pl.BlockSpec((rows, vcols), lambda b, c: (b * nb + c, 0)),
        out_shape=jax.ShapeDtypeStruct((t, vcols), BF16),
        scratch_shapes=[pltpu.VMEM((G_HEADS, G_DV, G_DK), F32)],
        compiler_params=_cparams(("parallel", "arbitrary")),
        name="gla",
    )(h, sh, sc, mix_w, w_big, w_small, alpha_full, alpha_b, norm_w)


def _mixout_body(h_ref, sh_ref, sc_ref, mw_ref, wg_ref, ha_ref, hb_ref, gt_ref, wpa_ref, wpb_ref,
                 wo_ref, o_ref):
    d = h_ref.shape[1]
    h = h_ref[...]
    u = _modulated_rmsnorm(h, mw_ref[...], sc_ref[0], sh_ref[0]).astype(BF16)
    ga = jnp.dot(u, wg_ref[:, 0:d], preferred_element_type=F32)
    gb = jnp.dot(u, wg_ref[:, d:2 * d], preferred_element_type=F32)
    a = jnp.dot(ha_ref[...], wpa_ref[...], preferred_element_type=F32)
    b = jnp.dot(hb_ref[...], wpb_ref[...], preferred_element_type=F32)
    y = jax.nn.sigmoid(ga) * a + jax.nn.sigmoid(gb) * b
    o_ref[...] = h + gt_ref[0] * jnp.dot(y.astype(BF16), wo_ref[...], preferred_element_type=F32)


def _mixout(h, sh, sc, mix_w, w_big, ha, hb, gt, w_pa, w_pb, w_o, seq):
    t, d = h.shape
    tm = min(MIX_ROWS, seq)
    per_b = seq // tm
    assert COL_GA % (2 * d) == 0 and COL_GB == COL_GA + d
    resident = pl.Buffered(1)
    wspec = pl.BlockSpec((d, d), lambda i: (0, 0), pipeline_mode=resident)
    bspec = pl.BlockSpec((1, 1, d), lambda i: (i // per_b, 0, 0))
    return pl.pallas_call(
        _mixout_body,
        grid=(t // tm,),
        in_specs=[pl.BlockSpec((tm, d), lambda i: (i, 0)),
                  bspec, bspec,
                  pl.BlockSpec((1, d), lambda i: (0, 0)),
                  pl.BlockSpec((d, 2 * d), lambda i: (0, COL_GA // (2 * d)), pipeline_mode=resident),
                  pl.BlockSpec((tm, d), lambda i: (i, 0)),
                  pl.BlockSpec((tm, d), lambda i: (i, 0)),
                  bspec,
                  wspec, wspec, wspec],
        out_specs=pl.BlockSpec((tm, d), lambda i: (i, 0)),
        out_shape=jax.ShapeDtypeStruct((t, d), F32),
        compiler_params=_cparams(("parallel",)),
        name="mix_out",
    )(h, sh, sc, mix_w, w_big, ha, hb, gt, w_pa, w_pb, w_o)


def _route_body(h_ref, sh_ref, sc_ref, nw_ref, rwt_ref, rb_ref,
                u_ref, eidx_ref, wts_ref, pos_ref, cnt_ref, carry_scr):
    tm = h_ref.shape[0]

    @pl.when(pl.program_id(0) == 0)
    def _():
        carry_scr[...] = jnp.zeros(carry_scr.shape, F32)

    u = _modulated_rmsnorm(h_ref[...], nw_ref[...], sc_ref[0], sh_ref[0])
    half = u.shape[1] // 2
    u_ref[...] = _pack_bf16_pair(u[:, :half], u[:, half:])
    logits = lax.dot_general(rwt_ref[...], u, (((1,), (1,)), ((), ())),
                             precision=HIGHEST, preferred_element_type=F32)
    scores = jax.nn.sigmoid(logits)
    sel = scores + rb_ref[...]

    neg = -jnp.inf
    sub_io = lax.broadcasted_iota(I32, (GROUP_SIZE, tm), 0)
    pieces = []
    for g in range(N_GROUPS):
        blk = sel[g * GROUP_SIZE:(g + 1) * GROUP_SIZE, :]
        m1 = jnp.max(blk, axis=0, keepdims=True)
        first = jnp.min(jnp.where(blk == m1, sub_io, GROUP_SIZE), axis=0, keepdims=True)
        m2 = jnp.max(jnp.where(sub_io == first, neg, blk), axis=0, keepdims=True)
        pieces.append(jnp.broadcast_to(m1 + m2, (GROUP_SIZE, tm)))
    gscore = jnp.concatenate(pieces, axis=0)

    e_io = lax.broadcasted_iota(I32, (N_EXPERTS, tm), 0)
    grp_io = e_io // GROUP_SIZE
    gmask = jnp.zeros((N_EXPERTS, tm), jnp.bool_)
    for _ in range(TOPK_GROUPS):
        mx = jnp.max(gscore, axis=0, keepdims=True)
        gi = jnp.min(jnp.where(gscore == mx, grp_io, N_GROUPS), axis=0, keepdims=True)
        hit = grp_io == gi
        gmask = jnp.logical_or(gmask, hit)
        gscore = jnp.where(hit, neg, gscore)

    cur = jnp.where(gmask, sel, neg)
    row_io = lax.broadcasted_iota(I32, (TOP_K, tm), 0)
    eidx = jnp.zeros((TOP_K, tm), I32)
    wraw = jnp.zeros((TOP_K, tm), F32)
    chosen = jnp.zeros((N_EXPERTS, tm), jnp.bool_)
    hits = []
    for kk in range(TOP_K):
        mx = jnp.max(cur, axis=0, keepdims=True)
        ei = jnp.min(jnp.where(cur == mx, e_io, N_EXPERTS), axis=0, keepdims=True)
        hit = e_io == ei
        hits.append(hit)
        sc_k = jnp.sum(jnp.where(hit, scores, 0.0), axis=0, keepdims=True)
        eidx = jnp.where(row_io == kk, ei, eidx)
        wraw = jnp.where(row_io == kk, sc_k, wraw)
        chosen = jnp.logical_or(chosen, hit)
        cur = jnp.where(hit, neg, cur)

    wsum = jnp.sum(wraw, axis=0, keepdims=True)
    wts_ref[...] = wraw / wsum * ROUTED_SCALE
    eidx_ref[...] = eidx

    chosen_f = jnp.where(chosen, 1.0, 0.0)
    r_io = lax.broadcasted_iota(I32, (tm, tm), 0)
    c_io = lax.broadcasted_iota(I32, (tm, tm), 1)
    strict_upper = jnp.where(r_io < c_io, 1.0, 0.0).astype(BF16)
    prefix = jnp.dot(chosen_f.astype(BF16), strict_upper, preferred_element_type=F32)
    rank = prefix + carry_scr[:, 0:1]
    pos = jnp.zeros((TOP_K, tm), F32)
    for kk in range(TOP_K):
        p_k = jnp.sum(jnp.where(hits[kk], rank, 0.0), axis=0, keepdims=True)
        pos = jnp.where(row_io == kk, p_k, pos)
    pos_ref[...] = pos.astype(I32)
    total = carry_scr[...] + jnp.sum(chosen_f, axis=1, keepdims=True)
    carry_scr[...] = total
    cnt_ref[...] = total.astype(I32)


def _route(h, sh, sc, nw, rw_t, rb_col, seq):
    t, d = h.shape
    tm = min(ROUTE_ROWS, seq)
    per_b = seq // tm
    kspec = pl.BlockSpec((TOP_K, tm), lambda i: (0, i))
    return pl.pallas_call(
        _route_body,
        grid=(t // tm,),
        in_specs=[pl.BlockSpec((tm, d), lambda i: (i, 0)),
                  pl.BlockSpec((1, 1, d), lambda i: (i // per_b, 0, 0)),
                  pl.BlockSpec((1, 1, d), lambda i: (i // per_b, 0, 0)),
                  pl.BlockSpec((1, d), lambda i: (0, 0)),
                  pl.BlockSpec((N_EXPERTS, d), lambda i: (0, 0)),
                  pl.BlockSpec((N_EXPERTS, 1), lambda i: (0, 0))],
        out_specs=[pl.BlockSpec((tm, d // 2), lambda i: (i, 0)), kspec, kspec, kspec,
                   pl.BlockSpec((N_EXPERTS, 128), lambda i: (0, 0))],
        out_shape=[jax.ShapeDtypeStruct((t, d // 2), U32),
                   jax.ShapeDtypeStruct((TOP_K, t), I32),
                   jax.ShapeDtypeStruct((TOP_K, t), F32),
                   jax.ShapeDtypeStruct((TOP_K, t), I32),
                   jax.ShapeDtypeStruct((N_EXPERTS, 128), I32)],
        scratch_shapes=[pltpu.VMEM((N_EXPERTS, 128), F32)],
        compiler_params=_cparams(("arbitrary",)),
        name="moe_route",
    )(h, sh, sc, nw, rw_t, rb_col)


def _sc_worker_id():
    return lax.axis_index("s") * SC_CORES + lax.axis_index("c")


def _sc_scatter_rows(x, dest, n_rows):
    t, d = x.shape
    n_k = dest.shape[0]
    assert t % (SC_WORKERS * 2 * SC_CHUNK) == 0
    per_w = t // SC_WORKERS
    n_ch = per_w // SC_CHUNK
    dest4 = dest.reshape(n_k, SC_WORKERS, n_ch, SC_CHUNK).transpose(1, 2, 0, 3)
    mesh = plsc.VectorSubcoreMesh(core_axis_name="c", subcore_axis_name="s")

    @functools.partial(
        pl.kernel, mesh=mesh,
        out_type=jax.ShapeDtypeStruct((n_rows, d), x.dtype),
        scratch_types=[pltpu.VMEM((n_ch, n_k, SC_CHUNK), I32),
                       pltpu.VMEM((SC_CHUNK, d), x.dtype),
                       pltpu.VMEM((SC_CHUNK, d), x.dtype)] + [pltpu.SemaphoreType.DMA] * 4,
        name="moe_dispatch_sc",
    )
    def scatter_kernel(x_hbm, dest_hbm, out_hbm, idx_v, rows0, rows1, l0, l1, s0, s1):
        wid = _sc_worker_id()
        base = wid * per_w
        pltpu.sync_copy(dest_hbm.at[wid], idx_v)
        bufs = ((rows0, l0, s0), (rows1, l1, s1))

        def load(ci, b):
            rows, load_sem, _ = bufs[b]
            return pltpu.make_async_copy(x_hbm.at[pl.ds(base + ci * SC_CHUNK, SC_CHUNK)], rows,
                                         load_sem)

        def scatters(ci, b):
            rows, _, scatter_sem = bufs[b]
            return [pltpu.make_async_copy(rows, out_hbm.at[idx_v.at[ci, j]], scatter_sem)
                    for j in range(n_k)]

        load(0, 0).start()

        @pl.loop(0, n_ch, step=2)
        def _(ci):
            load(ci, 0).wait()
            for cp in scatters(ci, 0):
                cp.start()

            @pl.when(ci > 0)
            def _():
                for cp in scatters(ci - 1, 1):
                    cp.wait()

            load(ci + 1, 1).start()
            load(ci + 1, 1).wait()
            for cp in scatters(ci + 1, 1):
                cp.start()
            for cp in scatters(ci, 0):
                cp.wait()

            @pl.when(ci + 2 < n_ch)
            def _():
                load(ci + 2, 0).start()

        for cp in scatters(n_ch - 1, 1):
            cp.wait()

    return scatter_kernel(x, dest4)


def _sc_gather_rows(table, idx):
    n, d = idx.shape[0], table.shape[1]
    assert n % (SC_WORKERS * 2 * SC_CHUNK) == 0
    per_w = n // SC_WORKERS
    n_ch = per_w // SC_CHUNK
    idx3 = idx.reshape(SC_WORKERS, n_ch, SC_CHUNK)
    mesh = plsc.VectorSubcoreMesh(core_axis_name="c", subcore_axis_name="s")

    @functools.partial(
        pl.kernel, mesh=mesh,
        out_type=jax.ShapeDtypeStruct((n, d), table.dtype),
        scratch_types=[pltpu.VMEM((n_ch, SC_CHUNK), I32),
                       pltpu.VMEM((SC_CHUNK, d), table.dtype),
                       pltpu.VMEM((SC_CHUNK, d), table.dtype)] + [pltpu.SemaphoreType.DMA] * 4,
        name="moe_combine_sc",
    )
    def gather_kernel(table_hbm, idx_hbm, out_hbm, idx_v, rows0, rows1, g0, g1, w0, w1):
        wid = _sc_worker_id()
        base = wid * per_w
        pltpu.sync_copy(idx_hbm.at[wid], idx_v)
        bufs = ((rows0, g0, w0), (rows1, g1, w1))

        def gather(ci, b):
            rows, gather_sem, _ = bufs[b]
            return pltpu.make_async_copy(table_hbm.at[idx_v.at[ci]], rows, gather_sem)

        def write(ci, b):
            rows, _, write_sem = bufs[b]
            return pltpu.make_async_copy(rows, out_hbm.at[pl.ds(base + ci * SC_CHUNK, SC_CHUNK)],
                                         write_sem)

        gather(0, 0).start()

        @pl.loop(0, n_ch, step=2)
        def _(ci):
            gather(ci, 0).wait()
            write(ci, 0).start()

            @pl.when(ci > 0)
            def _():
                write(ci - 1, 1).wait()

            gather(ci + 1, 1).start()
            gather(ci + 1, 1).wait()
            write(ci + 1, 1).start()
            write(ci, 0).wait()

            @pl.when(ci + 2 < n_ch)
            def _():
                gather(ci + 2, 0).start()

        write(n_ch - 1, 1).wait()

    return gather_kernel(table, idx3)


def _expert_body(blk_e_ref, blk_first_ref, blk_valid_ref, x_ref, w1_ref, w3_ref, w2_ref, y_ref,
                 w1_scr, w3_scr, w2_scr):
    del blk_e_ref
    j = pl.program_id(0)
    valid = blk_valid_ref[j]

    @pl.when(blk_first_ref[j] == 1)
    def _():
        w1_scr[...] = w1_ref[0].astype(BF16)
        w3_scr[...] = w3_ref[0].astype(BF16)
        w2_scr[...] = w2_ref[0].astype(BF16)

    @pl.when(valid > 0)
    def _():
        half = x_ref.shape[1]
        rows = lax.broadcasted_iota(I32, x_ref.shape, 0)
        lo, hi = _unpack_bf16_pair(jnp.where(rows < valid, x_ref[...], 0))
        lo, hi = lo.astype(BF16), hi.astype(BF16)

        def proj(w_scr):
            return (jnp.dot(lo, w_scr[0:half, :], preferred_element_type=F32)
                    + jnp.dot(hi, w_scr[half:2 * half, :], preferred_element_type=F32))

        hid = _silu(proj(w1_scr)) * proj(w3_scr)
        y = jnp.dot(hid.astype(BF16), w2_scr[...], preferred_element_type=F32)
        y_ref[...] = _pack_bf16_pair(y[:, :half], y[:, half:])

    @pl.when(valid == 0)
    def _():
        y_ref[...] = jnp.zeros(y_ref.shape, U32)


def _experts(xg, blk_e, blk_first, blk_valid, w1, w3, w2, layer):
    n_rows, half = xg.shape
    n_blocks = n_rows // EXPERT_ROWS
    d, de = w1.shape[-2:]
    grid_spec = pltpu.PrefetchScalarGridSpec(
        num_scalar_prefetch=3,
        grid=(n_blocks,),
        in_specs=[pl.BlockSpec((EXPERT_ROWS, half), lambda j, be, bf, bv: (j, 0)),
                  pl.BlockSpec((None, 1, d, de), lambda j, be, bf, bv: (layer, be[j], 0, 0)),
                  pl.BlockSpec((None, 1, d, de), lambda j, be, bf, bv: (layer, be[j], 0, 0)),
                  pl.BlockSpec((None, 1, de, d), lambda j, be, bf, bv: (layer, be[j], 0, 0))],
        out_specs=pl.BlockSpec((EXPERT_ROWS, half), lambda j, be, bf, bv: (j, 0)),
        scratch_shapes=[pltpu.VMEM((d, de), BF16), pltpu.VMEM((d, de), BF16),
                        pltpu.VMEM((de, d), BF16)],
    )
    return pl.pallas_call(
        _expert_body,
        grid_spec=grid_spec,
        out_shape=jax.ShapeDtypeStruct((n_rows, half), U32),
        compiler_params=_cparams(("arbitrary",)),
        name="moe_experts",
    )(blk_e, blk_first, blk_valid, xg, w1, w3, w2)


def _combine_body(h_ref, u_ref, yg_ref, wts_ref, gt_ref, s1_ref, s3_ref, s2_ref, fw_ref, o_ref, *,
                  final_norm):
    half = u_ref.shape[1]
    lo, hi = _unpack_bf16_pair(u_ref[...])
    lo, hi = lo.astype(BF16), hi.astype(BF16)

    def proj(w_ref):
        return (jnp.dot(lo, w_ref[0:half, :], preferred_element_type=F32)
                + jnp.dot(hi, w_ref[half:2 * half, :], preferred_element_type=F32))

    hid = _silu(proj(s1_ref)) * proj(s3_ref)
    shared = jnp.dot(hid.astype(BF16), s2_ref[...], preferred_element_type=F32)

    routed_lo = routed_hi = None
    for kk in range(TOP_K):
        y_lo, y_hi = _unpack_bf16_pair(yg_ref[kk])
        w = wts_ref[:, kk:kk + 1]
        routed_lo = y_lo * w if routed_lo is None else routed_lo + y_lo * w
        routed_hi = y_hi * w if routed_hi is None else routed_hi + y_hi * w
    gt = gt_ref[0]
    out_lo = h_ref[:, 0:half] + gt[:, 0:half] * (routed_lo + shared[:, 0:half])
    out_hi = (h_ref[:, half:2 * half]
              + gt[:, half:2 * half] * (routed_hi + shared[:, half:2 * half]))
    if final_norm:
        ssq = (jnp.sum(out_lo * out_lo, axis=-1, keepdims=True)
               + jnp.sum(out_hi * out_hi, axis=-1, keepdims=True))
        inv = lax.rsqrt(ssq / (2 * half) + EPS)
        out_lo = (out_lo * inv) * fw_ref[:, 0:half]
        out_hi = (out_hi * inv) * fw_ref[:, half:2 * half]
    o_ref[:, 0:half] = out_lo
    o_ref[:, half:2 * half] = out_hi


def _combine(h, u, yg, wts_tk, gt, s1, s3, s2, final_w, seq, final_norm, part, n_parts):
    t, d = h.shape
    half = d // 2
    tm = min(COMBINE_ROWS, seq)
    per_b = seq // tm
    ds_ = s1.shape[-1]
    steps = t // n_parts // tm
    off = part * steps
    return pl.pallas_call(
        functools.partial(_combine_body, final_norm=final_norm),
        grid=(steps,),
        in_specs=[pl.BlockSpec((tm, d), lambda i: (i + off, 0)),
                  pl.BlockSpec((tm, half), lambda i: (i + off, 0)),
                  pl.BlockSpec((TOP_K, tm, half), lambda i: (0, i, 0)),
                  pl.BlockSpec((tm, TOP_K), lambda i: (i + off, 0)),
                  pl.BlockSpec((1, 1, d), lambda i: ((i + off) // per_b, 0, 0)),
                  pl.BlockSpec((d, ds_), lambda i: (0, 0)),
                  pl.BlockSpec((d, ds_), lambda i: (0, 0)),
                  pl.BlockSpec((ds_, d), lambda i: (0, 0)),
                  pl.BlockSpec((1, d), lambda i: (0, 0))],
        out_specs=pl.BlockSpec((tm, d), lambda i: (i + off, 0)),
        out_shape=jax.ShapeDtypeStruct((t, d), F32),
        input_output_aliases={0: 0},
        compiler_params=_cparams(("parallel",)),
        name="moe_combine",
    )(h, u, yg, wts_tk, gt, s1, s3, s2, final_w)


def _split_w_in(w_in):
    sizes = (2 * M_HEADS * M_DQK, M_HEADS * M_DV, M_HEADS * M_DV, M_HEADS, M_HEADS,
             G_HEADS * G_DK, G_HEADS * G_DK, G_HEADS * G_DV, G_RANK, G_HEADS * G_DV,
             D_MODEL, D_MODEL)
    offs = [0]
    for n in sizes:
        offs.append(offs[-1] + n)
    w16 = w_in.astype(BF16)
    big = jnp.concatenate([w16[:, offs[0]:offs[3]], w16[:, offs[5]:offs[8]], w16[:, offs[9]:offs[12]]],
                          axis=1)
    pad = jnp.zeros((w_in.shape[0], SMALL_COLS - 2 * M_HEADS - G_RANK), BF16)
    small = jnp.concatenate([w16[:, offs[3]:offs[5]], w16[:, offs[8]:offs[9]], pad], axis=1)
    return big, small


def _moe_layout(counts, eidx, pos, n_blocks):
    padded = (counts + EXPERT_ROWS - 1) // EXPERT_ROWS * EXPERT_ROWS
    pend = jnp.cumsum(padded)
    pstart = pend - padded
    experts = jnp.arange(N_EXPERTS, dtype=I32)
    dest = pos + jnp.sum(jnp.where(eidx[..., None] == experts, pstart, 0), axis=-1)
    blk_start = jnp.arange(n_blocks, dtype=I32) * EXPERT_ROWS
    owner = jnp.sum((pend[None, :] <= blk_start[:, None]).astype(I32), axis=1)
    blk_e = jnp.minimum(owner, N_EXPERTS - 1)
    prev = jnp.concatenate([jnp.full((1,), -1, I32), blk_e[:-1]])
    blk_first = (blk_e != prev).astype(I32)
    own = blk_e[:, None] == experts[None, :]
    rows_left = jnp.sum(jnp.where(own, (pstart + counts)[None, :], 0), axis=1) - blk_start
    blk_valid = jnp.clip(jnp.where(owner < N_EXPERTS, rows_left, 0), 0, EXPERT_ROWS)
    return dest.astype(I32), blk_e.astype(I32), blk_first, blk_valid.astype(I32)


def kernel(x, c, ada_w, ada_b, norm_mix_w, norm_moe_w, w_in, m_conv_w, m_gate_b, m_norm_w,
           g_alpha_w, g_alpha_b, g_norm_w, w_pa, w_pb, w_o, router_w, router_b,
           exp_w1, exp_w3, exp_w2, sh_w1, sh_w3, sh_w2, final_norm_w):
    bsz, seq, d = x.shape
    depth = ada_w.shape[0]
    t = bsz * seq
    n_rows = t * TOP_K + N_EXPERTS * EXPERT_ROWS
    n_blocks = n_rows // EXPERT_ROWS

    ada = _ada(c, ada_w, ada_b).reshape(depth, bsz, 6, 1, d)
    h = x.reshape(t, d)
    for l in range(depth):
        sh1, sc1, gt1, sh2, sc2, gt2 = (ada[l, :, i] for i in range(6))

        w_big, w_small = _split_w_in(w_in[l])
        mix_w = norm_mix_w[l][None, :]
        gate_row = jnp.zeros((1, SMALL_COLS), F32)
        gate_row = gate_row.at[0, SMALL_I:SMALL_I + M_HEADS].set(m_gate_b[l, 0])
        gate_row = gate_row.at[0, SMALL_F:SMALL_F + M_HEADS].set(m_gate_b[l, 1])
        ha = _mlstm(h, sh1, sc1, mix_w, w_big, w_small, m_conv_w[l], gate_row, m_norm_w[l][None, :],
                    bsz, seq)
        alpha_full = jnp.zeros((SMALL_COLS, G_HEADS * G_DK), F32)
        alpha_full = alpha_full.at[SMALL_R:SMALL_R + G_RANK].set(g_alpha_w[l]).astype(BF16)
        hb = _gla(h, sh1, sc1, mix_w, w_big, w_small, alpha_full, g_alpha_b[l][None, :],
                  g_norm_w[l][None, :], bsz, seq)
        h = _mixout(h, sh1, sc1, mix_w, w_big, ha, hb, gt1, w_pa[l].astype(BF16),
                    w_pb[l].astype(BF16), w_o[l].astype(BF16), seq)

        u, eidx, wts, pos, cnt = _route(h, sh2, sc2, norm_moe_w[l][None, :],
                                        router_w[l].T, router_b[l][:, None], seq)
        dest, blk_e, blk_first, blk_valid = _moe_layout(cnt[:, 0], eidx, pos, n_blocks)
        xg = _sc_scatter_rows(u, dest, n_rows)
        y = _experts(xg, blk_e, blk_first, blk_valid, exp_w1, exp_w3, exp_w2, l)
        tp = t // COMBINE_PARTS
        for p in range(COMBINE_PARTS):
            dest_p = dest[:, p * tp:(p + 1) * tp].reshape(-1)
            yg = _sc_gather_rows(y, dest_p).reshape(TOP_K, tp, d // 2)
            h = _combine(h, u, yg, wts.T, gt2, sh_w1[l].astype(BF16), sh_w3[l].astype(BF16),
                         sh_w2[l].astype(BF16), final_norm_w[None, :], seq,
                         final_norm=(l == depth - 1), part=p, n_parts=COMBINE_PARTS)

    return h.reshape(bsz, seq, d)
```

```python
import functools

import jax
import jax.numpy as jnp
import numpy as np
from jax import lax
from jax.experimental import pallas as pl
from jax.experimental.pallas import tpu as pltpu
from jax.experimental.pallas import tpu_sc as plsc

F32 = jnp.float32
BF16 = jnp.bfloat16
I32 = jnp.int32
U32 = jnp.uint32
HI_MASK = np.uint32(0xFFFF0000)
HIGHEST = lax.Precision.HIGHEST

SC_CORES = 2
SC_SUBCORES = 16
SC_WORKERS = SC_CORES * SC_SUBCORES
SC_CHUNK = 64

D_MODEL = 1024
M_HEADS = 4
M_DQK = 128
M_DV = 256
M_CONV = 4
GATE_CAP = 15.0
G_HEADS = 4
G_DK = 128
G_DV = 256
G_RANK = 16
G_TAU = 16.0
G_CHUNK = 64
N_EXPERTS = 64
TOP_K = 8
N_GROUPS = 8
GROUP_SIZE = N_EXPERTS // N_GROUPS
TOPK_GROUPS = 4
D_EXPERT = 256
D_SHARED = 256
ROUTED_SCALE = 2.5
EPS = 1e-6

M_CHUNK_ROWS = 256
G_BLOCK_ROWS = 256
SEQ_GROUP = 2
MIX_ROWS = 512
ROUTE_ROWS = 512
EXPERT_ROWS = 1024
COMBINE_ROWS = 512
COMBINE_PARTS = 2
CONV_HALO = 8
VMEM_LIMIT = 48 * 1024 * 1024

COL_QM, COL_OM = 0, 2048
COL_QG, COL_ZG, COL_GA, COL_GB = 3072, 5120, 6144, 7168
BIG_COLS = 8192
SMALL_COLS = 128
SMALL_I, SMALL_F, SMALL_R = 0, M_HEADS, 2 * M_HEADS


def _cparams(sem, vmem=VMEM_LIMIT):
    return pltpu.CompilerParams(dimension_semantics=sem, vmem_limit_bytes=vmem)


def _silu(x):
    return x * jax.nn.sigmoid(x)


def _log_sigmoid(x):
    return jnp.minimum(x, 0.0) - jnp.log1p(jnp.exp(-jnp.abs(x)))


def _modulated_rmsnorm(x, w, sc, sh):
    y = x * lax.rsqrt(jnp.mean(x * x, axis=-1, keepdims=True) + EPS)
    return (y * w) * (1.0 + sc) + sh


def _pack_bf16_pair(lo, hi):
    lo_bits = lax.bitcast_convert_type(lo.astype(BF16).astype(F32), U32)
    hi_bits = lax.bitcast_convert_type(hi.astype(BF16).astype(F32), U32)
    return (lo_bits >> 16) | (hi_bits & HI_MASK)


def _unpack_bf16_pair(packed):
    lo = lax.bitcast_convert_type(packed << 16, F32)
    hi = lax.bitcast_convert_type(packed & HI_MASK, F32)
    return lo, hi


def _lower_tri(n, dtype):
    r = lax.broadcasted_iota(I32, (n, n), 0)
    c = lax.broadcasted_iota(I32, (n, n), 1)
    return (r >= c).astype(dtype)


def _split3_dot(lhs01, x):
    hi = x.astype(BF16)
    r1 = x - hi.astype(F32)
    mid = r1.astype(BF16)
    lo = (r1 - mid.astype(F32)).astype(BF16)
    return (jnp.dot(lhs01, hi, preferred_element_type=F32)
            + jnp.dot(lhs01, mid, preferred_element_type=F32)
            + jnp.dot(lhs01, lo, preferred_element_type=F32))


def _ada_body(c_ref, w_ref, b_ref, o_ref):
    cond = _silu(c_ref[...])
    o_ref[0] = jnp.dot(cond.astype(BF16), w_ref[0].astype(BF16),
                       preferred_element_type=F32) + b_ref[0]


def _ada(c, ada_w, ada_b):
    depth, d, six_d = ada_w.shape
    bsz = c.shape[0]
    nj = six_d // d
    return pl.pallas_call(
        _ada_body,
        grid=(depth, nj),
        in_specs=[pl.BlockSpec((bsz, d), lambda l, j: (0, 0)),
                  pl.BlockSpec((1, d, d), lambda l, j: (l, 0, j)),
                  pl.BlockSpec((1, 1, d), lambda l, j: (l, 0, j))],
        out_specs=pl.BlockSpec((1, bsz, d), lambda l, j: (l, 0, j)),
        out_shape=jax.ShapeDtypeStruct((depth, bsz, six_d), F32),
        compiler_params=_cparams(("parallel", "parallel")),
        name="ada_ln",
    )(c, ada_w, ada_b.reshape(depth, 1, six_d))


def _mlstm_body(h0_ref, hn_ref, sh_ref, sc_ref, mw_ref, w_ref, ws_ref, cw_ref, gb_ref, nw_ref,
                ha_ref, qk_nxt, v_nxt, o_nxt, sm_nxt, xe_scr, v_scr, o_scr, sm_scr,
                c_scr, n_scr, m_scr):
    n_seq, rows = hn_ref.shape[0], hn_ref.shape[1]
    half = M_HEADS * M_DQK
    vcols = M_HEADS * M_DV

    def project(h_ref):
        u = jnp.concatenate(
            [_modulated_rmsnorm(h_ref[g], mw_ref[...], sc_ref[g], sh_ref[g]) for g in range(n_seq)],
            axis=0).astype(BF16)
        qk_nxt[...] = jnp.dot(u, w_ref[:, 0:2 * half], preferred_element_type=F32)
        v_nxt[...] = jnp.dot(u, w_ref[:, 2 * half:2 * half + vcols],
                             preferred_element_type=F32).astype(BF16)
        o_nxt[...] = jnp.dot(u, w_ref[:, 2 * half + vcols:2 * half + 2 * vcols],
                             preferred_element_type=F32)
        sm_nxt[...] = jnp.dot(u, ws_ref[...], preferred_element_type=F32)

    @pl.when(pl.program_id(1) == 0)
    def _():
        xe_scr[:, 0:CONV_HALO, :] = jnp.zeros((n_seq, CONV_HALO, 2 * half), F32)
        c_scr[...] = jnp.zeros(c_scr.shape, F32)
        n_scr[...] = jnp.zeros(n_scr.shape, F32)
        m_scr[...] = jnp.zeros(m_scr.shape, F32)
        project(h0_ref)

    for g in range(n_seq):
        xe_scr[g, CONV_HALO:CONV_HALO + rows, :] = qk_nxt[g * rows:(g + 1) * rows, :]
    v_scr[...] = v_nxt[...]
    o_scr[...] = o_nxt[...]
    sm_scr[...] = sm_nxt[...]
    project(hn_ref)

    r_io = lax.broadcasted_iota(I32, (rows, rows), 0)
    c_io = lax.broadcasted_iota(I32, (rows, rows), 1)
    causal = r_io >= c_io
    tri = _lower_tri(rows, BF16)

    for g in range(n_seq):
        gr = slice(g * rows, (g + 1) * rows)
        conv = None
        for j in range(M_CONV):
            off = CONV_HALO - (M_CONV - 1) + j
            term = xe_scr[g, off:off + rows, :] * cw_ref[j:j + 1, :]
            conv = term if conv is None else conv + term
        qk = _silu(conv)
        xe_scr[g, 0:CONV_HALO, :] = xe_scr[g, rows:rows + CONV_HALO, :]

        capped = GATE_CAP * jnp.tanh((sm_scr[gr, :] + gb_ref[...]) / GATE_CAP)
        li_all = capped
        lf_all = _log_sigmoid(capped)
        b_all = _split3_dot(tri, lf_all)
        li_t = li_all.T
        b_t = b_all.T

        for h in range(M_HEADS):
            sidx = g * M_HEADS + h
            q = (qk[:, h * M_DQK:(h + 1) * M_DQK] * (M_DQK ** -0.5)).astype(BF16)
            k = qk[:, half + h * M_DQK:half + (h + 1) * M_DQK]
            kb = k.astype(BF16)
            v = v_scr[gr, h * M_DV:(h + 1) * M_DV]
            li_c = li_all[:, SMALL_I + h:SMALL_I + h + 1]
            b_c = b_all[:, SMALL_F + h:SMALL_F + h + 1]
            li_r = li_t[SMALL_I + h:SMALL_I + h + 1, :]
            b_r = b_t[SMALL_F + h:SMALL_F + h + 1, :]
            gsum = b_c[rows - 1:rows, :]
            m_prev = m_scr[sidx][:, 0:1]
            c_prev = c_scr[sidx]
            n_prev = n_scr[sidx]

            d_mat = jnp.where(causal, b_c - b_r + li_r, -jnp.inf)
            m_inter = b_c + m_prev
            m_t = jnp.maximum(jnp.max(d_mat, axis=1, keepdims=True), m_inter)
            s = lax.dot_general(q, kb, (((1,), (1,)), ((), ())), preferred_element_type=F32)
            p = jnp.exp(d_mat - m_t) * s
            w_inter = jnp.exp(m_inter - m_t)
            num = (jnp.dot(p.astype(BF16), v, preferred_element_type=F32)
                   + w_inter * jnp.dot(q, c_prev.astype(BF16), preferred_element_type=F32))
            qn = jnp.sum(q.astype(F32) * n_prev, axis=1, keepdims=True)
            den = jnp.sum(p, axis=1, keepdims=True) + w_inter * qn
            hh = num / jnp.maximum(jnp.abs(den), jnp.exp(-m_t))

            a_r = gsum - b_r + li_r
            a_c = gsum - b_c + li_c
            m_new = jnp.maximum(gsum + m_prev, jnp.max(a_r, axis=1, keepdims=True))
            decay = jnp.exp(gsum + m_prev - m_new)
            wk = jnp.exp(a_c - m_new) * k
            c_scr[sidx] = decay * c_prev + lax.dot_general(
                wk.astype(BF16), v, (((0,), (0,)), ((), ())), preferred_element_type=F32)
            n_scr[sidx] = decay * n_prev + jnp.sum(wk, axis=0, keepdims=True)
            m_scr[sidx] = jnp.broadcast_to(m_new, m_scr.shape[1:])

            y = hh * lax.rsqrt(jnp.mean(hh * hh, axis=-1, keepdims=True) + EPS)
            y = y * nw_ref[:, h * M_DV:(h + 1) * M_DV]
            gate = jax.nn.sigmoid(o_scr[gr, h * M_DV:(h + 1) * M_DV])
            ha_ref[g, :, h * M_DV:(h + 1) * M_DV] = (y * gate).astype(BF16)


def _mlstm(h, sh, sc, mix_w, w_big, w_small, conv_w, gate_row, norm_w, bsz, seq):
    t, d = h.shape
    rows = min(M_CHUNK_ROWS, seq)
    nc = seq // rows
    ng = SEQ_GROUP if bsz % SEQ_GROUP == 0 else 1
    half = M_HEADS * M_DQK
    vcols = M_HEADS * M_DV
    wcols = 2 * half + 2 * vcols
    assert COL_QM == 0 and COL_OM + vcols == wcols
    resident = pl.Buffered(1)
    nxt = [pltpu.VMEM((ng * rows, 2 * half), F32), pltpu.VMEM((ng * rows, vcols), BF16),
           pltpu.VMEM((ng * rows, vcols), F32), pltpu.VMEM((ng * rows, SMALL_COLS), F32)]
    out = pl.pallas_call(
        _mlstm_body,
        grid=(bsz // ng, nc),
        in_specs=[pl.BlockSpec((ng, rows, d), lambda b, c: (b, 0, 0)),
                  pl.BlockSpec((ng, rows, d), lambda b, c: (b, jnp.minimum(c + 1, nc - 1), 0)),
                  pl.BlockSpec((ng, 1, d), lambda b, c: (b, 0, 0)),
                  pl.BlockSpec((ng, 1, d), lambda b, c: (b, 0, 0)),
                  pl.BlockSpec((1, d), lambda b, c: (0, 0)),
                  pl.BlockSpec((d, wcols), lambda b, c: (0, 0), pipeline_mode=resident),
                  pl.BlockSpec((d, SMALL_COLS), lambda b, c: (0, 0), pipeline_mode=resident),
                  pl.BlockSpec((M_CONV, 2 * half), lambda b, c: (0, 0)),
                  pl.BlockSpec((1, SMALL_COLS), lambda b, c: (0, 0)),
                  pl.BlockSpec((1, vcols), lambda b, c: (0, 0))],
        out_specs=pl.BlockSpec((ng, rows, vcols), lambda b, c: (b, c, 0)),
        out_shape=jax.ShapeDtypeStruct((bsz, seq, vcols), BF16),
        scratch_shapes=nxt + [pltpu.VMEM((ng, rows + CONV_HALO, 2 * half), F32)] + nxt[1:]
        + [pltpu.VMEM((ng * M_HEADS, M_DQK, M_DV), F32),
           pltpu.VMEM((ng * M_HEADS, 1, M_DQK), F32),
           pltpu.VMEM((ng * M_HEADS, 1, 128), F32)],
        compiler_params=_cparams(("parallel", "arbitrary")),
        name="mlstm",
    )(h.reshape(bsz, seq, d), h.reshape(bsz, seq, d), sh, sc, mix_w, w_big, w_small, conv_w,
      gate_row, norm_w)
    return out.reshape(t, vcols)


def _gla_body(h0_ref, hn_ref, sh_ref, sc_ref, mw_ref, w_ref, ws_ref, aw_ref, ab_ref, nw_ref, hb_ref,
              qk_nxt, v_nxt, z_nxt, sm_nxt, qk_scr, v_scr, z_scr, sm_scr, st_scr):
    n_seq, rows = hn_ref.shape[0], hn_ref.shape[1]
    n_chunks = rows // G_CHUNK
    kcols = G_HEADS * G_DK
    vcols = G_HEADS * G_DV

    def project(h_ref):
        u = jnp.concatenate(
            [_modulated_rmsnorm(h_ref[g], mw_ref[...], sc_ref[g], sh_ref[g]) for g in range(n_seq)],
            axis=0).astype(BF16)
        qk_nxt[...] = jnp.dot(u, w_ref[:, 0:2 * kcols], preferred_element_type=F32)
        v_nxt[...] = jnp.dot(u, w_ref[:, 2 * kcols:2 * kcols + vcols],
                             preferred_element_type=F32).astype(BF16)
        z_nxt[...] = jnp.dot(u, w_ref[:, 2 * kcols + vcols:2 * kcols + 2 * vcols],
                             preferred_element_type=F32)
        sm_nxt[...] = jnp.dot(u, ws_ref[...], preferred_element_type=F32)

    @pl.when(pl.program_id(1) == 0)
    def _():
        st_scr[...] = jnp.zeros(st_scr.shape, F32)
        project(h0_ref)

    qk_scr[...] = qk_nxt[...]
    v_scr[...] = v_nxt[...]
    z_scr[...] = z_nxt[...]
    sm_scr[...] = sm_nxt[...]
    project(hn_ref)

    r_io = lax.broadcasted_iota(I32, (rows, rows), 0)
    c_io = lax.broadcasted_iota(I32, (rows, rows), 1)
    chunk_causal = jnp.logical_and(r_io >= c_io, r_io // G_CHUNK == c_io // G_CHUNK)
    chunk_tri = jnp.where(chunk_causal, 1.0, 0.0).astype(BF16)

    for g in range(n_seq):
        gr = slice(g * rows, (g + 1) * rows)
        q = qk_scr[gr, 0:kcols] * (G_DK ** -0.5)
        k = qk_scr[gr, kcols:2 * kcols]

        logits = jnp.dot(sm_scr[gr, :].astype(BF16), aw_ref[...], preferred_element_type=F32)
        la = _log_sigmoid(logits + ab_ref[...]) / G_TAU
        bc = _split3_dot(chunk_tri, la)
        gcs = [bc[(ci + 1) * G_CHUNK - 1:(ci + 1) * G_CHUNK, :] for ci in range(n_chunks)]
        gc_rows = jnp.concatenate([jnp.broadcast_to(gc, (G_CHUNK, kcols)) for gc in gcs], axis=0)

        q_in = (q * jnp.exp(bc)).astype(BF16)
        k_in = (k * jnp.exp(-bc)).astype(BF16)
        k_out = (k * jnp.exp(gc_rows - bc)).astype(BF16)

        for h in range(G_HEADS):
            ks = slice(h * G_DK, (h + 1) * G_DK)
            vs = slice(h * G_DV, (h + 1) * G_DV)
            v = v_scr[gr, vs]
            att = lax.dot_general(q_in[:, ks], k_in[:, ks], (((1,), (1,)), ((), ())),
                                  preferred_element_type=F32)
            att = jnp.where(chunk_causal, att, 0.0).astype(BF16)
            o_intra = jnp.dot(att, v, preferred_element_type=F32)
            st = st_scr[g * G_HEADS + h]
            outs = []
            for ci in range(n_chunks):
                rs = slice(ci * G_CHUNK, (ci + 1) * G_CHUNK)
                o_inter = lax.dot_general(q_in[rs, ks], st.astype(BF16),
                                          (((1,), (1,)), ((), ())), preferred_element_type=F32)
                outs.append(o_intra[rs, :] + o_inter)
                st = jnp.exp(gcs[ci][:, ks]) * st + lax.dot_general(
                    v[rs, :], k_out[rs, ks], (((0,), (0,)), ((), ())), preferred_element_type=F32)
            st_scr[g * G_HEADS + h] = st
            o = jnp.concatenate(outs, axis=0)
            y = o * lax.rsqrt(jnp.mean(o * o, axis=-1, keepdims=True) + EPS)
            y = y * nw_ref[:, vs]
            hb_ref[g, :, vs] = (y * _silu(z_scr[gr, vs])).astype(BF16)


def _gla(h, sh, sc, mix_w, w_big, w_small, alpha_full, alpha_b, norm_w, bsz, seq):
    t, d = h.shape
    rows = min(G_BLOCK_ROWS, seq)
    nb = seq // rows
    ng = SEQ_GROUP if bsz % SEQ_GROUP == 0 else 1
    kcols = G_HEADS * G_DK
    vcols = G_HEADS * G_DV
    wcols = 2 * kcols + 2 * vcols
    assert COL_QG % wcols == 0 and COL_ZG + vcols == COL_QG + wcols
    resident = pl.Buffered(1)
    nxt = [pltpu.VMEM((ng * rows, 2 * kcols), F32), pltpu.VMEM((ng * rows, vcols), BF16),
           pltpu.VMEM((ng * rows, vcols), F32), pltpu.VMEM((ng * rows, SMALL_COLS), F32)]
    out = pl.pallas_call(
        _gla_body,
        grid=(bsz // ng, nb),
        in_specs=[pl.BlockSpec((ng, rows, d), lambda b, c: (b, 0, 0)),
                  pl.BlockSpec((ng, rows, d), lambda b, c: (b, jnp.minimum(c + 1, nb - 1), 0)),
                  pl.BlockSpec((ng, 1, d), lambda b, c: (b, 0, 0)),
                  pl.BlockSpec((ng, 1, d), lambda b, c: (b, 0, 0)),
                  pl.BlockSpec((1, d), lambda b, c: (0, 0)),
                  pl.BlockSpec((d, wcols), lambda b, c: (0, COL_QG // wcols), pipeline_mode=resident),
                  pl.BlockSpec((d, SMALL_COLS), lambda b, c: (0, 0), pipeline_mode=resident),
                  pl.BlockSpec((SMALL_COLS, kcols), lambda b, c: (0, 0)),
                  pl.BlockSpec((1, kcols), lambda b, c: (0, 0)),
                  pl.BlockSpec((1, vcols), lambda b, c: (0, 0))],
        out_specs=pl.BlockSpec((ng, rows, vcols), lambda b, c: (b, c, 0)),
        out_shape=jax.ShapeDtypeStruct((bsz, seq, vcols), BF16),
        scratch_shapes=nxt + nxt + [pltpu.VMEM((ng * G_HEADS, G_DV, G_DK), F32)],
        compiler_params=_cparams(("parallel", "arbitrary")),
        name="gla",
    )(h.reshape(bsz, seq, d), h.reshape(bsz, seq, d), sh, sc, mix_w, w_big, w_small, alpha_full,
      alpha_b, norm_w)
    return out.reshape(t, vcols)


def _mixout_body(h_ref, sh_ref, sc_ref, mw_ref, wg_ref, ha_ref, hb_ref, gt_ref, wpa_ref, wpb_ref,
                 wo_ref, o_ref):
    d = h_ref.shape[1]
    h = h_ref[...]
    u = _modulated_rmsnorm(h, mw_ref[...], sc_ref[0], sh_ref[0]).astype(BF16)
    ga = jnp.dot(u, wg_ref[:, 0:d], preferred_element_type=F32)
    gb = jnp.dot(u, wg_ref[:, d:2 * d], preferred_element_type=F32)
    a = jnp.dot(ha_ref[...], wpa_ref[...], preferred_element_type=F32)
    b = jnp.dot(hb_ref[...], wpb_ref[...], preferred_element_type=F32)
    y = jax.nn.sigmoid(ga) * a + jax.nn.sigmoid(gb) * b
    o_ref[...] = h + gt_ref[0] * jnp.dot(y.astype(BF16), wo_ref[...], preferred_element_type=F32)


def _mixout(h, sh, sc, mix_w, w_big, ha, hb, gt, w_pa, w_pb, w_o, seq):
    t, d = h.shape
    tm = min(MIX_ROWS, seq)
    per_b = seq // tm
    assert COL_GA % (2 * d) == 0 and COL_GB == COL_GA + d
    resident = pl.Buffered(1)
    wspec = pl.BlockSpec((d, d), lambda i: (0, 0), pipeline_mode=resident)
    bspec = pl.BlockSpec((1, 1, d), lambda i: (i // per_b, 0, 0))
    return pl.pallas_call(
        _mixout_body,
        grid=(t // tm,),
        in_specs=[pl.BlockSpec((tm, d), lambda i: (i, 0)),
                  bspec, bspec,
                  pl.BlockSpec((1, d), lambda i: (0, 0)),
                  pl.BlockSpec((d, 2 * d), lambda i: (0, COL_GA // (2 * d)), pipeline_mode=resident),
                  pl.BlockSpec((tm, d), lambda i: (i, 0)),
                  pl.BlockSpec((tm, d), lambda i: (i, 0)),
                  bspec,
                  wspec, wspec, wspec],
        out_specs=pl.BlockSpec((tm, d), lambda i: (i, 0)),
        out_shape=jax.ShapeDtypeStruct((t, d), F32),
        compiler_params=_cparams(("parallel",)),
        name="mix_out",
    )(h, sh, sc, mix_w, w_big, ha, hb, gt, w_pa, w_pb, w_o)


def _route_body(h_ref, sh_ref, sc_ref, nw_ref, rwt_ref, rb_ref,
                u_ref, eidx_ref, wts_ref, pos_ref, cnt_ref, carry_scr):
    tm = h_ref.shape[0]

    @pl.when(pl.program_id(0) == 0)
    def _():
        carry_scr[...] = jnp.zeros(carry_scr.shape, F32)

    u = _modulated_rmsnorm(h_ref[...], nw_ref[...], sc_ref[0], sh_ref[0])
    half = u.shape[1] // 2
    u_ref[...] = _pack_bf16_pair(u[:, :half], u[:, half:])
    logits = lax.dot_general(rwt_ref[...], u, (((1,), (1,)), ((), ())),
                             precision=HIGHEST, preferred_element_type=F32)
    scores = jax.nn.sigmoid(logits)
    sel = scores + rb_ref[...]

    neg = -jnp.inf
    sub_io = lax.broadcasted_iota(I32, (GROUP_SIZE, tm), 0)
    pieces = []
    for g in range(N_GROUPS):
        blk = sel[g * GROUP_SIZE:(g + 1) * GROUP_SIZE, :]
        m1 = jnp.max(blk, axis=0, keepdims=True)
        first = jnp.min(jnp.where(blk == m1, sub_io, GROUP_SIZE), axis=0, keepdims=True)
        m2 = jnp.max(jnp.where(sub_io == first, neg, blk), axis=0, keepdims=True)
        pieces.append(jnp.broadcast_to(m1 + m2, (GROUP_SIZE, tm)))
    gscore = jnp.concatenate(pieces, axis=0)

    e_io = lax.broadcasted_iota(I32, (N_EXPERTS, tm), 0)
    grp_io = e_io // GROUP_SIZE
    gmask = jnp.zeros((N_EXPERTS, tm), jnp.bool_)
    for _ in range(TOPK_GROUPS):
        mx = jnp.max(gscore, axis=0, keepdims=True)
        gi = jnp.min(jnp.where(gscore == mx, grp_io, N_GROUPS), axis=0, keepdims=True)
        hit = grp_io == gi
        gmask = jnp.logical_or(gmask, hit)
        gscore = jnp.where(hit, neg, gscore)

    cur = jnp.where(gmask, sel, neg)
    row_io = lax.broadcasted_iota(I32, (TOP_K, tm), 0)
    eidx = jnp.zeros((TOP_K, tm), I32)
    wraw = jnp.zeros((TOP_K, tm), F32)
    chosen = jnp.zeros((N_EXPERTS, tm), jnp.bool_)
    hits = []
    for kk in range(TOP_K):
        mx = jnp.max(cur, axis=0, keepdims=True)
        ei = jnp.min(jnp.where(cur == mx, e_io, N_EXPERTS), axis=0, keepdims=True)
        hit = e_io == ei
        hits.append(hit)
        sc_k = jnp.sum(jnp.where(hit, scores, 0.0), axis=0, keepdims=True)
        eidx = jnp.where(row_io == kk, ei, eidx)
        wraw = jnp.where(row_io == kk, sc_k, wraw)
        chosen = jnp.logical_or(chosen, hit)
        cur = jnp.where(hit, neg, cur)

    wsum = jnp.sum(wraw, axis=0, keepdims=True)
    wts_ref[...] = wraw / wsum * ROUTED_SCALE
    eidx_ref[...] = eidx

    chosen_f = jnp.where(chosen, 1.0, 0.0)
    r_io = lax.broadcasted_iota(I32, (tm, tm), 0)
    c_io = lax.broadcasted_iota(I32, (tm, tm), 1)
    strict_upper = jnp.where(r_io < c_io, 1.0, 0.0).astype(BF16)
    prefix = jnp.dot(chosen_f.astype(BF16), strict_upper, preferred_element_type=F32)
    rank = prefix + carry_scr[:, 0:1]
    pos = jnp.zeros((TOP_K, tm), F32)
    for kk in range(TOP_K):
        p_k = jnp.sum(jnp.where(hits[kk], rank, 0.0), axis=0, keepdims=True)
        pos = jnp.where(row_io == kk, p_k, pos)
    pos_ref[...] = pos.astype(I32)
    total = carry_scr[...] + jnp.sum(chosen_f, axis=1, keepdims=True)
    carry_scr[...] = total
    cnt_ref[...] = total.astype(I32)


def _route(h, sh, sc, nw, rw_t, rb_col, seq):
    t, d = h.shape
    tm = min(ROUTE_ROWS, seq)
    per_b = seq // tm
    kspec = pl.BlockSpec((TOP_K, tm), lambda i: (0, i))
    return pl.pallas_call(
        _route_body,
        grid=(t // tm,),
        in_specs=[pl.BlockSpec((tm, d), lambda i: (i, 0)),
                  pl.BlockSpec((1, 1, d), lambda i: (i // per_b, 0, 0)),
                  pl.BlockSpec((1, 1, d), lambda i: (i // per_b, 0, 0)),
                  pl.BlockSpec((1, d), lambda i: (0, 0)),
                  pl.BlockSpec((N_EXPERTS, d), lambda i: (0, 0)),
                  pl.BlockSpec((N_EXPERTS, 1), lambda i: (0, 0))],
        out_specs=[pl.BlockSpec((tm, d // 2), lambda i: (i, 0)), kspec, kspec, kspec,
                   pl.BlockSpec((N_EXPERTS, 128), lambda i: (0, 0))],
        out_shape=[jax.ShapeDtypeStruct((t, d // 2), U32),
                   jax.ShapeDtypeStruct((TOP_K, t), I32),
                   jax.ShapeDtypeStruct((TOP_K, t), F32),
                   jax.ShapeDtypeStruct((TOP_K, t), I32),
                   jax.ShapeDtypeStruct((N_EXPERTS, 128), I32)],
        scratch_shapes=[pltpu.VMEM((N_EXPERTS, 128), F32)],
        compiler_params=_cparams(("arbitrary",)),
        name="moe_route",
    )(h, sh, sc, nw, rw_t, rb_col)


def _sc_worker_id():
    return lax.axis_index("s") * SC_CORES + lax.axis_index("c")


def _sc_scatter_rows(x, dest, n_rows):
    t, d = x.shape
    n_k = dest.shape[0]
    assert t % (SC_WORKERS * 2 * SC_CHUNK) == 0
    per_w = t // SC_WORKERS
    n_ch = per_w // SC_CHUNK
    dest4 = dest.reshape(n_k, SC_WORKERS, n_ch, SC_CHUNK).transpose(1, 2, 0, 3)
    mesh = plsc.VectorSubcoreMesh(core_axis_name="c", subcore_axis_name="s")

    @functools.partial(
        pl.kernel, mesh=mesh,
        out_type=jax.ShapeDtypeStruct((n_rows, d), x.dtype),
        scratch_types=[pltpu.VMEM((n_ch, n_k, SC_CHUNK), I32),
                       pltpu.VMEM((SC_CHUNK, d), x.dtype),
                       pltpu.VMEM((SC_CHUNK, d), x.dtype)] + [pltpu.SemaphoreType.DMA] * 4,
        name="moe_dispatch_sc",
    )
    def scatter_kernel(x_hbm, dest_hbm, out_hbm, idx_v, rows0, rows1, l0, l1, s0, s1):
        wid = _sc_worker_id()
        base = wid * per_w
        pltpu.sync_copy(dest_hbm.at[wid], idx_v)
        bufs = ((rows0, l0, s0), (rows1, l1, s1))

        def load(ci, b):
            rows, load_sem, _ = bufs[b]
            return pltpu.make_async_copy(x_hbm.at[pl.ds(base + ci * SC_CHUNK, SC_CHUNK)], rows,
                                         load_sem)

        def scatters(ci, b):
            rows, _, scatter_sem = bufs[b]
            return [pltpu.make_async_copy(rows, out_hbm.at[idx_v.at[ci, j]], scatter_sem)
                    for j in range(n_k)]

        load(0, 0).start()

        @pl.loop(0, n_ch, step=2)
        def _(ci):
            load(ci, 0).wait()
            for cp in scatters(ci, 0):
                cp.start()

            @pl.when(ci > 0)
            def _():
                for cp in scatters(ci - 1, 1):
                    cp.wait()

            load(ci + 1, 1).start()
            load(ci + 1, 1).wait()
            for cp in scatters(ci + 1, 1):
                cp.start()
            for cp in scatters(ci, 0):
                cp.wait()

            @pl.when(ci + 2 < n_ch)
            def _():
                load(ci + 2, 0).start()

        for cp in scatters(n_ch - 1, 1):
            cp.wait()

    return scatter_kernel(x, dest4)


def _sc_gather_rows(table, idx):
    n, d = idx.shape[0], table.shape[1]
    assert n % (SC_WORKERS * 2 * SC_CHUNK) == 0
    per_w = n // SC_WORKERS
    n_ch = per_w // SC_CHUNK
    idx3 = idx.reshape(SC_WORKERS, n_ch, SC_CHUNK)
    mesh = plsc.VectorSubcoreMesh(core_axis_name="c", subcore_axis_name="s")

    @functools.partial(
        pl.kernel, mesh=mesh,
        out_type=jax.ShapeDtypeStruct((n, d), table.dtype),
        scratch_types=[pltpu.VMEM((n_ch, SC_CHUNK), I32),
                       pltpu.VMEM((SC_CHUNK, d), table.dtype),
                       pltpu.VMEM((SC_CHUNK, d), table.dtype)] + [pltpu.SemaphoreType.DMA] * 4,
        name="moe_combine_sc",
    )
    def gather_kernel(table_hbm, idx_hbm, out_hbm, idx_v, rows0, rows1, g0, g1, w0, w1):
        wid = _sc_worker_id()
        base = wid * per_w
        pltpu.sync_copy(idx_hbm.at[wid], idx_v)
        bufs = ((rows0, g0, w0), (rows1, g1, w1))

        def gather(ci, b):
            rows, gather_sem, _ = bufs[b]
            return pltpu.make_async_copy(table_hbm.at[idx_v.at[ci]], rows, gather_sem)

        def write(ci, b):
            rows, _, write_sem = bufs[b]
            return pltpu.make_async_copy(rows, out_hbm.at[pl.ds(base + ci * SC_CHUNK, SC_CHUNK)],
                                         write_sem)

        gather(0, 0).start()

        @pl.loop(0, n_ch, step=2)
        def _(ci):
            gather(ci, 0).wait()
            write(ci, 0).start()

            @pl.when(ci > 0)
            def _():
                write(ci - 1, 1).wait()

            gather(ci + 1, 1).start()
            gather(ci + 1, 1).wait()
            write(ci + 1, 1).start()
            write(ci, 0).wait()

            @pl.when(ci + 2 < n_ch)
            def _():
                gather(ci + 2, 0).start()

        write(n_ch - 1, 1).wait()

    return gather_kernel(table, idx3)


def _expert_body(blk_e_ref, blk_first_ref, blk_valid_ref, x_ref, w1_ref, w3_ref, w2_ref, y_ref,
                 w1_scr, w3_scr, w2_scr):
    del blk_e_ref
    j = pl.program_id(0)
    valid = blk_valid_ref[j]

    @pl.when(blk_first_ref[j] == 1)
    def _():
        w1_scr[...] = w1_ref[0].astype(BF16)
        w3_scr[...] = w3_ref[0].astype(BF16)
        w2_scr[...] = w2_ref[0].astype(BF16)

    @pl.when(valid > 0)
    def _():
        half = x_ref.shape[1]
        rows = lax.broadcasted_iota(I32, x_ref.shape, 0)
        lo, hi = _unpack_bf16_pair(jnp.where(rows < valid, x_ref[...], 0))
        lo, hi = lo.astype(BF16), hi.astype(BF16)

        def proj(w_scr):
            return (jnp.dot(lo, w_scr[0:half, :], preferred_element_type=F32)
                    + jnp.dot(hi, w_scr[half:2 * half, :], preferred_element_type=F32))

        hid = _silu(proj(w1_scr)) * proj(w3_scr)
        y = jnp.dot(hid.astype(BF16), w2_scr[...], preferred_element_type=F32)
        y_ref[...] = _pack_bf16_pair(y[:, :half], y[:, half:])

    @pl.when(valid == 0)
    def _():
        y_ref[...] = jnp.zeros(y_ref.shape, U32)


def _experts(xg, blk_e, blk_first, blk_valid, w1, w3, w2, layer):
    n_rows, half = xg.shape
    n_blocks = n_rows // EXPERT_ROWS
    d, de = w1.shape[-2:]
    grid_spec = pltpu.PrefetchScalarGridSpec(
        num_scalar_prefetch=3,
        grid=(n_blocks,),
        in_specs=[pl.BlockSpec((EXPERT_ROWS, half), lambda j, be, bf, bv: (j, 0)),
                  pl.BlockSpec((None, 1, d, de), lambda j, be, bf, bv: (layer, be[j], 0, 0)),
                  pl.BlockSpec((None, 1, d, de), lambda j, be, bf, bv: (layer, be[j], 0, 0)),
                  pl.BlockSpec((None, 1, de, d), lambda j, be, bf, bv: (layer, be[j], 0, 0))],
        out_specs=pl.BlockSpec((EXPERT_ROWS, half), lambda j, be, bf, bv: (j, 0)),
        scratch_shapes=[pltpu.VMEM((d, de), BF16), pltpu.VMEM((d, de), BF16),
                        pltpu.VMEM((de, d), BF16)],
    )
    return pl.pallas_call(
        _expert_body,
        grid_spec=grid_spec,
        out_shape=jax.ShapeDtypeStruct((n_rows, half), U32),
        compiler_params=_cparams(("arbitrary",)),
        name="moe_experts",
    )(blk_e, blk_first, blk_valid, xg, w1, w3, w2)


def _combine_body(h_ref, u_ref, yg_ref, wts_ref, gt_ref, s1_ref, s3_ref, s2_ref, fw_ref, o_ref, *,
                  final_norm):
    half = u_ref.shape[1]
    lo, hi = _unpack_bf16_pair(u_ref[...])
    lo, hi = lo.astype(BF16), hi.astype(BF16)

    def proj(w_ref):
        return (jnp.dot(lo, w_ref[0:half, :], preferred_element_type=F32)
                + jnp.dot(hi, w_ref[half:2 * half, :], preferred_element_type=F32))

    hid = _silu(proj(s1_ref)) * proj(s3_ref)
    shared = jnp.dot(hid.astype(BF16), s2_ref[...], preferred_element_type=F32)

    routed_lo = routed_hi = None
    for kk in range(TOP_K):
        y_lo, y_hi = _unpack_bf16_pair(yg_ref[kk])
        w = wts_ref[:, kk:kk + 1]
        routed_lo = y_lo * w if routed_lo is None else routed_lo + y_lo * w
        routed_hi = y_hi * w if routed_hi is None else routed_hi + y_hi * w
    gt = gt_ref[0]
    out_lo = h_ref[:, 0:half] + gt[:, 0:half] * (routed_lo + shared[:, 0:half])
    out_hi = (h_ref[:, half:2 * half]
              + gt[:, half:2 * half] * (routed_hi + shared[:, half:2 * half]))
    if final_norm:
        ssq = (jnp.sum(out_lo * out_lo, axis=-1, keepdims=True)
               + jnp.sum(out_hi * out_hi, axis=-1, keepdims=True))
        inv = lax.rsqrt(ssq / (2 * half) + EPS)
        out_lo = (out_lo * inv) * fw_ref[:, 0:half]
        out_hi = (out_hi * inv) * fw_ref[:, half:2 * half]
    o_ref[:, 0:half] = out_lo
    o_ref[:, half:2 * half] = out_hi


def _combine(h, u, yg, wts_tk, gt, s1, s3, s2, final_w, seq, final_norm, part, n_parts):
    t, d = h.shape
    half = d // 2
    tm = min(COMBINE_ROWS, seq)
    per_b = seq // tm
    ds_ = s1.shape[-1]
    steps = t // n_parts // tm
    off = part * steps
    return pl.pallas_call(
        functools.partial(_combine_body, final_norm=final_norm),
        grid=(steps,),
        in_specs=[pl.BlockSpec((tm, d), lambda i: (i + off, 0)),
                  pl.BlockSpec((tm, half), lambda i: (i + off, 0)),
                  pl.BlockSpec((TOP_K, tm, half), lambda i: (0, i, 0)),
                  pl.BlockSpec((tm, TOP_K), lambda i: (i + off, 0)),
                  pl.BlockSpec((1, 1, d), lambda i: ((i + off) // per_b, 0, 0)),
                  pl.BlockSpec((d, ds_), lambda i: (0, 0)),
                  pl.BlockSpec((d, ds_), lambda i: (0, 0)),
                  pl.BlockSpec((ds_, d), lambda i: (0, 0)),
                  pl.BlockSpec((1, d), lambda i: (0, 0))],
        out_specs=pl.BlockSpec((tm, d), lambda i: (i + off, 0)),
        out_shape=jax.ShapeDtypeStruct((t, d), F32),
        input_output_aliases={0: 0},
        compiler_params=_cparams(("parallel",)),
        name="moe_combine",
    )(h, u, yg, wts_tk, gt, s1, s3, s2, final_w)


def _split_w_in(w_in):
    sizes = (2 * M_HEADS * M_DQK, M_HEADS * M_DV, M_HEADS * M_DV, M_HEADS, M_HEADS,
             G_HEADS * G_DK, G_HEADS * G_DK, G_HEADS * G_DV, G_RANK, G_HEADS * G_DV,
             D_MODEL, D_MODEL)
    offs = [0]
    for n in sizes:
        offs.append(offs[-1] + n)
    w16 = w_in.astype(BF16)
    big = jnp.concatenate([w16[:, offs[0]:offs[3]], w16[:, offs[5]:offs[8]], w16[:, offs[9]:offs[12]]],
                          axis=1)
    pad = jnp.zeros((w_in.shape[0], SMALL_COLS - 2 * M_HEADS - G_RANK), BF16)
    small = jnp.concatenate([w16[:, offs[3]:offs[5]], w16[:, offs[8]:offs[9]], pad], axis=1)
    return big, small


def _moe_layout(counts, eidx, pos, n_blocks):
    padded = (counts + EXPERT_ROWS - 1) // EXPERT_ROWS * EXPERT_ROWS
    pend = jnp.cumsum(padded)
    pstart = pend - padded
    experts = jnp.arange(N_EXPERTS, dtype=I32)
    dest = pos + jnp.sum(jnp.where(eidx[..., None] == experts, pstart, 0), axis=-1)
    blk_start = jnp.arange(n_blocks, dtype=I32) * EXPERT_ROWS
    owner = jnp.sum((pend[None, :] <= blk_start[:, None]).astype(I32), axis=1)
    blk_e = jnp.minimum(owner, N_EXPERTS - 1)
    prev = jnp.concatenate([jnp.full((1,), -1, I32), blk_e[:-1]])
    blk_first = (blk_e != prev).astype(I32)
    own = blk_e[:, None] == experts[None, :]
    rows_left = jnp.sum(jnp.where(own, (pstart + counts)[None, :], 0), axis=1) - blk_start
    blk_valid = jnp.clip(jnp.where(owner < N_EXPERTS, rows_left, 0), 0, EXPERT_ROWS)
    return dest.astype(I32), blk_e.astype(I32), blk_first, blk_valid.astype(I32)


def kernel(x, c, ada_w, ada_b, norm_mix_w, norm_moe_w, w_in, m_conv_w, m_gate_b, m_norm_w,
           g_alpha_w, g_alpha_b, g_norm_w, w_pa, w_pb, w_o, router_w, router_b,
           exp_w1, exp_w3, exp_w2, sh_w1, sh_w3, sh_w2, final_norm_w):
    bsz, seq, d = x.shape
    depth = ada_w.shape[0]
    t = bsz * seq
    n_rows = t * TOP_K + N_EXPERTS * EXPERT_ROWS
    n_blocks = n_rows // EXPERT_ROWS

    ada = _ada(c, ada_w, ada_b).reshape(depth, bsz, 6, 1, d)
    h = x.reshape(t, d)
    for l in range(depth):
        sh1, sc1, gt1, sh2, sc2, gt2 = (ada[l, :, i] for i in range(6))

        w_big, w_small = _split_w_in(w_in[l])
        mix_w = norm_mix_w[l][None, :]
        gate_row = jnp.zeros((1, SMALL_COLS), F32)
        gate_row = gate_row.at[0, SMALL_I:SMALL_I + M_HEADS].set(m_gate_b[l, 0])
        gate_row = gate_row.at[0, SMALL_F:SMALL_F + M_HEADS].set(m_gate_b[l, 1])
        ha = _mlstm(h, sh1, sc1, mix_w, w_big, w_small, m_conv_w[l], gate_row, m_norm_w[l][None, :],
                    bsz, seq)
        alpha_full = jnp.zeros((SMALL_COLS, G_HEADS * G_DK), F32)
        alpha_full = alpha_full.at[SMALL_R:SMALL_R + G_RANK].set(g_alpha_w[l]).astype(BF16)
        hb = _gla(h, sh1, sc1, mix_w, w_big, w_small, alpha_full, g_alpha_b[l][None, :],
                  g_norm_w[l][None, :], bsz, seq)
        h = _mixout(h, sh1, sc1, mix_w, w_big, ha, hb, gt1, w_pa[l].astype(BF16),
                    w_pb[l].astype(BF16), w_o[l].astype(BF16), seq)

        u, eidx, wts, pos, cnt = _route(h, sh2, sc2, norm_moe_w[l][None, :],
                                        router_w[l].T, router_b[l][:, None], seq)
        dest, blk_e, blk_first, blk_valid = _moe_layout(cnt[:, 0], eidx, pos, n_blocks)
        xg = _sc_scatter_rows(u, dest, n_rows)
        y = _experts(xg, blk_e, blk_first, blk_valid, exp_w1, exp_w3, exp_w2, l)
        tp = t // COMBINE_PARTS
        for p in range(COMBINE_PARTS):
            dest_p = dest[:, p * tp:(p + 1) * tp].reshape(-1)
            yg = _sc_gather_rows(y, dest_p).reshape(TOP_K, tp, d // 2)
            h = _combine(h, u, yg, wts.T, gt2, sh_w1[l].astype(BF16), sh_w3[l].astype(BF16),
                         sh_w2[l].astype(BF16), final_norm_w[None, :], seq,
                         final_norm=(l == depth - 1), part=p, n_parts=COMBINE_PARTS)

    return h.reshape(bsz, seq, d)
```

```python
import functools

import jax
import jax.numpy as jnp
import numpy as np
from jax import lax
from jax.experimental import pallas as pl
from jax.experimental.pallas import tpu as pltpu
from jax.experimental.pallas import tpu_sc as plsc

F32 = jnp.float32
BF16 = jnp.bfloat16
I32 = jnp.int32
U32 = jnp.uint32
HI_MASK = np.uint32(0xFFFF0000)
HIGHEST = lax.Precision.HIGHEST

SC_CORES = 2
SC_SUBCORES = 16
SC_WORKERS = SC_CORES * SC_SUBCORES
SC_CHUNK = 64

D_MODEL = 1024
M_HEADS = 4
M_DQK = 128
M_DV = 256
M_CONV = 4
GATE_CAP = 15.0
G_HEADS = 4
G_DK = 128
G_DV = 256
G_RANK = 16
G_TAU = 16.0
G_CHUNK = 64
N_EXPERTS = 64
TOP_K = 8
N_GROUPS = 8
GROUP_SIZE = N_EXPERTS // N_GROUPS
TOPK_GROUPS = 4
D_EXPERT = 256
D_SHARED = 256
ROUTED_SCALE = 2.5
EPS = 1e-6

M_CHUNK_ROWS = 256
G_BLOCK_ROWS = 256
SEQ_GROUP = 2
PROJ_PIECE_COLS = 256
MIX_ROWS = 512
ROUTE_ROWS = 512
EXPERT_ROWS = 1024
COMBINE_ROWS = 512
COMBINE_PARTS = 2
CONV_HALO = 8
VMEM_LIMIT = 48 * 1024 * 1024

COL_QM, COL_OM = 0, 2048
COL_QG, COL_ZG, COL_GA, COL_GB = 3072, 5120, 6144, 7168
BIG_COLS = 8192
SMALL_COLS = 128
SMALL_I, SMALL_F, SMALL_R = 0, M_HEADS, 2 * M_HEADS


def _cparams(sem, vmem=VMEM_LIMIT):
    return pltpu.CompilerParams(dimension_semantics=sem, vmem_limit_bytes=vmem)


def _silu(x):
    return x * jax.nn.sigmoid(x)


def _log_sigmoid(x):
    return jnp.minimum(x, 0.0) - jnp.log1p(jnp.exp(-jnp.abs(x)))


def _modulated_rmsnorm(x, w, sc, sh):
    y = x * lax.rsqrt(jnp.mean(x * x, axis=-1, keepdims=True) + EPS)
    return (y * w) * (1.0 + sc) + sh


def _pack_bf16_pair(lo, hi):
    lo_bits = lax.bitcast_convert_type(lo.astype(BF16).astype(F32), U32)
    hi_bits = lax.bitcast_convert_type(hi.astype(BF16).astype(F32), U32)
    return (lo_bits >> 16) | (hi_bits & HI_MASK)


def _unpack_bf16_pair(packed):
    lo = lax.bitcast_convert_type(packed << 16, F32)
    hi = lax.bitcast_convert_type(packed & HI_MASK, F32)
    return lo, hi


def _lower_tri(n, dtype):
    r = lax.broadcasted_iota(I32, (n, n), 0)
    c = lax.broadcasted_iota(I32, (n, n), 1)
    return (r >= c).astype(dtype)


def _split3_dot(lhs01, x):
    hi = x.astype(BF16)
    r1 = x - hi.astype(F32)
    mid = r1.astype(BF16)
    lo = (r1 - mid.astype(F32)).astype(BF16)
    return (jnp.dot(lhs01, hi, preferred_element_type=F32)
            + jnp.dot(lhs01, mid, preferred_element_type=F32)
            + jnp.dot(lhs01, lo, preferred_element_type=F32))


def _ada_body(c_ref, w_ref, b_ref, o_ref):
    cond = _silu(c_ref[...])
    o_ref[0] = jnp.dot(cond.astype(BF16), w_ref[0].astype(BF16),
                       preferred_element_type=F32) + b_ref[0]


def _ada(c, ada_w, ada_b):
    depth, d, six_d = ada_w.shape
    bsz = c.shape[0]
    nj = six_d // d
    return pl.pallas_call(
        _ada_body,
        grid=(depth, nj),
        in_specs=[pl.BlockSpec((bsz, d), lambda l, j: (0, 0)),
                  pl.BlockSpec((1, d, d), lambda l, j: (l, 0, j)),
                  pl.BlockSpec((1, 1, d), lambda l, j: (l, 0, j))],
        out_specs=pl.BlockSpec((1, bsz, d), lambda l, j: (l, 0, j)),
        out_shape=jax.ShapeDtypeStruct((depth, bsz, six_d), F32),
        compiler_params=_cparams(("parallel", "parallel")),
        name="ada_ln",
    )(c, ada_w, ada_b.reshape(depth, 1, six_d))


def _mlstm_body(h0_ref, hn_ref, sh_ref, sc_ref, mw_ref, w_ref, ws_ref, cw_ref, gb_ref, nw_ref,
                ha_ref, qk_nxt, v_nxt, o_nxt, sm_nxt, xe_scr, v_scr, o_scr, sm_scr,
                c_scr, n_scr, m_scr):
    n_seq, rows = hn_ref.shape[0], hn_ref.shape[1]
    half = M_HEADS * M_DQK
    vcols = M_HEADS * M_DV

    def projection_pieces(h_ref):
        u = jnp.concatenate(
            [_modulated_rmsnorm(h_ref[g], mw_ref[...], sc_ref[g], sh_ref[g]) for g in range(n_seq)],
            axis=0).astype(BF16)

        def piece(dst, w_lo, lo, width):
            def run():
                dst[:, lo:lo + width] = jnp.dot(u, w_ref[:, w_lo + lo:w_lo + lo + width],
                                                preferred_element_type=F32).astype(dst.dtype)
            return run

        def small_piece():
            sm_nxt[...] = jnp.dot(u, ws_ref[...], preferred_element_type=F32)

        pieces = [small_piece]
        for dst, w_lo in ((qk_nxt, 0), (v_nxt, 2 * half), (o_nxt, 2 * half + vcols)):
            for lo in range(0, vcols, PROJ_PIECE_COLS):
                pieces.append(piece(dst, w_lo, lo, PROJ_PIECE_COLS))
        return pieces

    @pl.when(pl.program_id(1) == 0)
    def _():
        xe_scr[:, 0:CONV_HALO, :] = jnp.zeros((n_seq, CONV_HALO, 2 * half), F32)
        c_scr[...] = jnp.zeros(c_scr.shape, F32)
        n_scr[...] = jnp.zeros(n_scr.shape, F32)
        m_scr[...] = jnp.zeros(m_scr.shape, F32)
        for run in projection_pieces(h0_ref):
            run()

    for g in range(n_seq):
        xe_scr[g, CONV_HALO:CONV_HALO + rows, :] = qk_nxt[g * rows:(g + 1) * rows, :]
    v_scr[...] = v_nxt[...]
    o_scr[...] = o_nxt[...]
    sm_scr[...] = sm_nxt[...]
    pending = projection_pieces(hn_ref)

    r_io = lax.broadcasted_iota(I32, (rows, rows), 0)
    c_io = lax.broadcasted_iota(I32, (rows, rows), 1)
    causal = r_io >= c_io
    tri = _lower_tri(rows, BF16)

    for g in range(n_seq):
        gr = slice(g * rows, (g + 1) * rows)
        conv = None
        for j in range(M_CONV):
            off = CONV_HALO - (M_CONV - 1) + j
            term = xe_scr[g, off:off + rows, :] * cw_ref[j:j + 1, :]
            conv = term if conv is None else conv + term
        qk = _silu(conv)
        xe_scr[g, 0:CONV_HALO, :] = xe_scr[g, rows:rows + CONV_HALO, :]

        capped = GATE_CAP * jnp.tanh((sm_scr[gr, :] + gb_ref[...]) / GATE_CAP)
        li_all = capped
        lf_all = _log_sigmoid(capped)
        b_all = _split3_dot(tri, lf_all)
        li_t = li_all.T
        b_t = b_all.T

        for h in range(M_HEADS):
            if pending:
                pending.pop(0)()
            sidx = g * M_HEADS + h
            q = (qk[:, h * M_DQK:(h + 1) * M_DQK] * (M_DQK ** -0.5)).astype(BF16)
            k = qk[:, half + h * M_DQK:half + (h + 1) * M_DQK]
            kb = k.astype(BF16)
            v = v_scr[gr, h * M_DV:(h + 1) * M_DV]
            li_c = li_all[:, SMALL_I + h:SMALL_I + h + 1]
            b_c = b_all[:, SMALL_F + h:SMALL_F + h + 1]
            li_r = li_t[SMALL_I + h:SMALL_I + h + 1, :]
            b_r = b_t[SMALL_F + h:SMALL_F + h + 1, :]
            gsum = b_c[rows - 1:rows, :]
            m_prev = m_scr[sidx][:, 0:1]
            c_prev = c_scr[sidx]
            n_prev = n_scr[sidx]

            d_mat = jnp.where(causal, b_c - b_r + li_r, -jnp.inf)
            m_inter = b_c + m_prev
            m_t = jnp.maximum(jnp.max(d_mat, axis=1, keepdims=True), m_inter)
            s = lax.dot_general(q, kb, (((1,), (1,)), ((), ())), preferred_element_type=F32)
            p = jnp.exp(d_mat - m_t) * s
            w_inter = jnp.exp(m_inter - m_t)
            num = (jnp.dot(p.astype(BF16), v, preferred_element_type=F32)
                   + w_inter * jnp.dot(q, c_prev.astype(BF16), preferred_element_type=F32))
            qn = jnp.sum(q.astype(F32) * n_prev, axis=1, keepdims=True)
            den = jnp.sum(p, axis=1, keepdims=True) + w_inter * qn
            hh = num / jnp.maximum(jnp.abs(den), jnp.exp(-m_t))

            if pending:
                pending.pop(0)()

            a_r = gsum - b_r + li_r
            a_c = gsum - b_c + li_c
            m_new = jnp.maximum(gsum + m_prev, jnp.max(a_r, axis=1, keepdims=True))
            decay = jnp.exp(gsum + m_prev - m_new)
            wk = jnp.exp(a_c - m_new) * k
            c_scr[sidx] = decay * c_prev + lax.dot_general(
                wk.astype(BF16), v, (((0,), (0,)), ((), ())), preferred_element_type=F32)
            n_scr[sidx] = decay * n_prev + jnp.sum(wk, axis=0, keepdims=True)
            m_scr[sidx] = jnp.broadcast_to(m_new, m_scr.shape[1:])

            y = hh * lax.rsqrt(jnp.mean(hh * hh, axis=-1, keepdims=True) + EPS)
            y = y * nw_ref[:, h * M_DV:(h + 1) * M_DV]
            gate = jax.nn.sigmoid(o_scr[gr, h * M_DV:(h + 1) * M_DV])
            ha_ref[g, :, h * M_DV:(h + 1) * M_DV] = (y * gate).astype(BF16)
    for run in pending:
        run()


def _mlstm(h, sh, sc, mix_w, w_big, w_small, conv_w, gate_row, norm_w, bsz, seq):
    t, d = h.shape
    rows = min(M_CHUNK_ROWS, seq)
    nc = seq // rows
    ng = SEQ_GROUP if bsz % SEQ_GROUP == 0 else 1
    half = M_HEADS * M_DQK
    vcols = M_HEADS * M_DV
    wcols = 2 * half + 2 * vcols
    assert COL_QM == 0 and COL_OM + vcols == wcols
    resident = pl.Buffered(1)
    nxt = [pltpu.VMEM((ng * rows, 2 * half), F32), pltpu.VMEM((ng * rows, vcols), BF16),
           pltpu.VMEM((ng * rows, vcols), F32), pltpu.VMEM((ng * rows, SMALL_COLS), F32)]
    out = pl.pallas_call(
        _mlstm_body,
        grid=(bsz // ng, nc),
        in_specs=[pl.BlockSpec((ng, rows, d), lambda b, c: (b, 0, 0)),
                  pl.BlockSpec((ng, rows, d), lambda b, c: (b, jnp.minimum(c + 1, nc - 1), 0)),
                  pl.BlockSpec((ng, 1, d), lambda b, c: (b, 0, 0)),
                  pl.BlockSpec((ng, 1, d), lambda b, c: (b, 0, 0)),
                  pl.BlockSpec((1, d), lambda b, c: (0, 0)),
                  pl.BlockSpec((d, wcols), lambda b, c: (0, 0), pipeline_mode=resident),
                  pl.BlockSpec((d, SMALL_COLS), lambda b, c: (0, 0), pipeline_mode=resident),
                  pl.BlockSpec((M_CONV, 2 * half), lambda b, c: (0, 0)),
                  pl.BlockSpec((1, SMALL_COLS), lambda b, c: (0, 0)),
                  pl.BlockSpec((1, vcols), lambda b, c: (0, 0))],
        out_specs=pl.BlockSpec((ng, rows, vcols), lambda b, c: (b, c, 0)),
        out_shape=jax.ShapeDtypeStruct((bsz, seq, vcols), BF16),
        scratch_shapes=nxt + [pltpu.VMEM((ng, rows + CONV_HALO, 2 * half), F32)] + nxt[1:]
        + [pltpu.VMEM((ng * M_HEADS, M_DQK, M_DV), F32),
           pltpu.VMEM((ng * M_HEADS, 1, M_DQK), F32),
           pltpu.VMEM((ng * M_HEADS, 1, 128), F32)],
        compiler_params=_cparams(("parallel", "arbitrary")),
        name="mlstm",
    )(h.reshape(bsz, seq, d), h.reshape(bsz, seq, d), sh, sc, mix_w, w_big, w_small, conv_w,
      gate_row, norm_w)
    return out.reshape(t, vcols)


def _gla_body(h0_ref, hn_ref, sh_ref, sc_ref, mw_ref, w_ref, ws_ref, aw_ref, ab_ref, nw_ref, hb_ref,
              qk_nxt, v_nxt, z_nxt, sm_nxt, qk_scr, v_scr, z_scr, sm_scr, st_scr):
    n_seq, rows = hn_ref.shape[0], hn_ref.shape[1]
    n_chunks = rows // G_CHUNK
    kcols = G_HEADS * G_DK
    vcols = G_HEADS * G_DV

    def projection_pieces(h_ref):
        u = jnp.concatenate(
            [_modulated_rmsnorm(h_ref[g], mw_ref[...], sc_ref[g], sh_ref[g]) for g in range(n_seq)],
            axis=0).astype(BF16)

        def piece(dst, w_lo, lo, width):
            def run():
                dst[:, lo:lo + width] = jnp.dot(u, w_ref[:, w_lo + lo:w_lo + lo + width],
                                                preferred_element_type=F32).astype(dst.dtype)
            return run

        def small_piece():
            sm_nxt[...] = jnp.dot(u, ws_ref[...], preferred_element_type=F32)

        pieces = [small_piece]
        for dst, w_lo in ((qk_nxt, 0), (v_nxt, 2 * kcols), (z_nxt, 2 * kcols + vcols)):
            for lo in range(0, vcols, PROJ_PIECE_COLS):
                pieces.append(piece(dst, w_lo, lo, PROJ_PIECE_COLS))
        return pieces

    @pl.when(pl.program_id(1) == 0)
    def _():
        st_scr[...] = jnp.zeros(st_scr.shape, F32)
        for run in projection_pieces(h0_ref):
            run()

    qk_scr[...] = qk_nxt[...]
    v_scr[...] = v_nxt[...]
    z_scr[...] = z_nxt[...]
    sm_scr[...] = sm_nxt[...]
    for run in projection_pieces(hn_ref):
        run()

    r_io = lax.broadcasted_iota(I32, (rows, rows), 0)
    c_io = lax.broadcasted_iota(I32, (rows, rows), 1)
    chunk_causal = jnp.logical_and(r_io >= c_io, r_io // G_CHUNK == c_io // G_CHUNK)
    chunk_tri = jnp.where(chunk_causal, 1.0, 0.0).astype(BF16)

    for g in range(n_seq):
        gr = slice(g * rows, (g + 1) * rows)
        q = qk_scr[gr, 0:kcols] * (G_DK ** -0.5)
        k = qk_scr[gr, kcols:2 * kcols]

        logits = jnp.dot(sm_scr[gr, :].astype(BF16), aw_ref[...], preferred_element_type=F32)
        la = _log_sigmoid(logits + ab_ref[...]) / G_TAU
        bc = _split3_dot(chunk_tri, la)
        gcs = [bc[(ci + 1) * G_CHUNK - 1:(ci + 1) * G_CHUNK, :] for ci in range(n_chunks)]
        gc_rows = jnp.concatenate([jnp.broadcast_to(gc, (G_CHUNK, kcols)) for gc in gcs], axis=0)

        q_in = (q * jnp.exp(bc)).astype(BF16)
        k_in = (k * jnp.exp(-bc)).astype(BF16)
        k_out = (k * jnp.exp(gc_rows - bc)).astype(BF16)

        for h in range(G_HEADS):
            ks = slice(h * G_DK, (h + 1) * G_DK)
            vs = slice(h * G_DV, (h + 1) * G_DV)
            v = v_scr[gr, vs]
            att = lax.dot_general(q_in[:, ks], k_in[:, ks], (((1,), (1,)), ((), ())),
                                  preferred_element_type=F32)
            att = jnp.where(chunk_causal, att, 0.0).astype(BF16)
            o_intra = jnp.dot(att, v, preferred_element_type=F32)
            st = st_scr[g * G_HEADS + h]
            outs = []
            for ci in range(n_chunks):
                rs = slice(ci * G_CHUNK, (ci + 1) * G_CHUNK)
                o_inter = lax.dot_general(q_in[rs, ks], st.astype(BF16),
                                          (((1,), (1,)), ((), ())), preferred_element_type=F32)
                outs.append(o_intra[rs, :] + o_inter)
                st = jnp.exp(gcs[ci][:, ks]) * st + lax.dot_general(
                    v[rs, :], k_out[rs, ks], (((0,), (0,)), ((), ())), preferred_element_type=F32)
            st_scr[g * G_HEADS + h] = st
            o = jnp.concatenate(outs, axis=0)
            y = o * lax.rsqrt(jnp.mean(o * o, axis=-1, keepdims=True) + EPS)
            y = y * nw_ref[:, vs]
            hb_ref[g, :, vs] = (y * _silu(z_scr[gr, vs])).astype(BF16)


def _gla(h, sh, sc, mix_w, w_big, w_small, alpha_full, alpha_b, norm_w, bsz, seq):
    t, d = h.shape
    rows = min(G_BLOCK_ROWS, seq)
    nb = seq // rows
    ng = SEQ_GROUP if bsz % SEQ_GROUP == 0 else 1
    kcols = G_HEADS * G_DK
    vcols = G_HEADS * G_DV
    wcols = 2 * kcols + 2 * vcols
    assert COL_QG % wcols == 0 and COL_ZG + vcols == COL_QG + wcols
    resident = pl.Buffered(1)
    nxt = [pltpu.VMEM((ng * rows, 2 * kcols), F32), pltpu.VMEM((ng * rows, vcols), BF16),
           pltpu.VMEM((ng * rows, vcols), F32), pltpu.VMEM((ng * rows, SMALL_COLS), F32)]
    out = pl.pallas_call(
        _gla_body,
        grid=(bsz // ng, nb),
        in_specs=[pl.BlockSpec((ng, rows, d), lambda b, c: (b, 0, 0)),
                  pl.BlockSpec((ng, rows, d), lambda b, c: (b, jnp.minimum(c + 1, nb - 1), 0)),
                  pl.BlockSpec((ng, 1, d), lambda b, c: (b, 0, 0)),
                  pl.BlockSpec((ng, 1, d), lambda b, c: (b, 0, 0)),
                  pl.BlockSpec((1, d), lambda b, c: (0, 0)),
                  pl.BlockSpec((d, wcols), lambda b, c: (0, COL_QG // wcols), pipeline_mode=resident),
                  pl.BlockSpec((d, SMALL_COLS), lambda b, c: (0, 0), pipeline_mode=resident),
                  pl.BlockSpec((SMALL_COLS, kcols), lambda b, c: (0, 0)),
                  pl.BlockSpec((1, kcols), lambda b, c: (0, 0)),
                  pl.BlockSpec((1, vcols), lambda b, c: (0, 0))],
        out_specs=pl.BlockSpec((ng, rows, vcols), lambda b, c: (b, c, 0)),
        out_shape=jax.ShapeDtypeStruct((bsz, seq, vcols), BF16),
        scratch_shapes=nxt + nxt + [pltpu.VMEM((ng * G_HEADS, G_DV, G_DK), F32)],
        compiler_params=_cparams(("parallel", "arbitrary")),
        name="gla",
    )(h.reshape(bsz, seq, d), h.reshape(bsz, seq, d), sh, sc, mix_w, w_big, w_small, alpha_full,
      alpha_b, norm_w)
    return out.reshape(t, vcols)


def _mixout_body(h_ref, sh_ref, sc_ref, mw_ref, wg_ref, ha_ref, hb_ref, gt_ref, wpa_ref, wpb_ref,
                 wo_ref, o_ref):
    d = h_ref.shape[1]
    h = h_ref[...]
    u = _modulated_rmsnorm(h, mw_ref[...], sc_ref[0], sh_ref[0]).astype(BF16)
    ga = jnp.dot(u, wg_ref[:, 0:d], preferred_element_type=F32)
    gb = jnp.dot(u, wg_ref[:, d:2 * d], preferred_element_type=F32)
    a = jnp.dot(ha_ref[...], wpa_ref[...], preferred_element_type=F32)
    b = jnp.dot(hb_ref[...], wpb_ref[...], preferred_element_type=F32)
    y = jax.nn.sigmoid(ga) * a + jax.nn.sigmoid(gb) * b
    o_ref[...] = h + gt_ref[0] * jnp.dot(y.astype(BF16), wo_ref[...], preferred_element_type=F32)


def _mixout(h, sh, sc, mix_w, w_big, ha, hb, gt, w_pa, w_pb, w_o, seq):
    t, d = h.shape
    tm = min(MIX_ROWS, seq)
    per_b = seq // tm
    assert COL_GA % (2 * d) == 0 and COL_GB == COL_GA + d
    resident = pl.Buffered(1)
    wspec = pl.BlockSpec((d, d), lambda i: (0, 0), pipeline_mode=resident)
    bspec = pl.BlockSpec((1, 1, d), lambda i: (i // per_b, 0, 0))
    return pl.pallas_call(
        _mixout_body,
        grid=(t // tm,),
        in_specs=[pl.BlockSpec((tm, d), lambda i: (i, 0)),
                  bspec, bspec,
                  pl.BlockSpec((1, d), lambda i: (0, 0)),
                  pl.BlockSpec((d, 2 * d), lambda i: (0, COL_GA // (2 * d)), pipeline_mode=resident),
                  pl.BlockSpec((tm, d), lambda i: (i, 0)),
                  pl.BlockSpec((tm, d), lambda i: (i, 0)),
                  bspec,
                  wspec, wspec, wspec],
        out_specs=pl.BlockSpec((tm, d), lambda i: (i, 0)),
        out_shape=jax.ShapeDtypeStruct((t, d), F32),
        compiler_params=_cparams(("parallel",)),
        name="mix_out",
    )(h, sh, sc, mix_w, w_big, ha, hb, gt, w_pa, w_pb, w_o)


def _route_body(h_ref, sh_ref, sc_ref, nw_ref, rwt_ref, rb_ref,
                u_ref, eidx_ref, wts_ref, pos_ref, cnt_ref, carry_scr):
    tm = h_ref.shape[0]

    @pl.when(pl.program_id(0) == 0)
    def _():
        carry_scr[...] = jnp.zeros(carry_scr.shape, F32)

    u = _modulated_rmsnorm(h_ref[...], nw_ref[...], sc_ref[0], sh_ref[0])
    half = u.shape[1] // 2
    u_ref[...] = _pack_bf16_pair(u[:, :half], u[:, half:])
    logits = lax.dot_general(rwt_ref[...], u, (((1,), (1,)), ((), ())),
                             precision=HIGHEST, preferred_element_type=F32)
    scores = jax.nn.sigmoid(logits)
    sel = scores + rb_ref[...]

    neg = -jnp.inf
    sub_io = lax.broadcasted_iota(I32, (GROUP_SIZE, tm), 0)
    pieces = []
    for g in range(N_GROUPS):
        blk = sel[g * GROUP_SIZE:(g + 1) * GROUP_SIZE, :]
        m1 = jnp.max(blk, axis=0, keepdims=True)
        first = jnp.min(jnp.where(blk == m1, sub_io, GROUP_SIZE), axis=0, keepdims=True)
        m2 = jnp.max(jnp.where(sub_io == first, neg, blk), axis=0, keepdims=True)
        pieces.append(jnp.broadcast_to(m1 + m2, (GROUP_SIZE, tm)))
    gscore = jnp.concatenate(pieces, axis=0)

    e_io = lax.broadcasted_iota(I32, (N_EXPERTS, tm), 0)
    grp_io = e_io // GROUP_SIZE
    gmask = jnp.zeros((N_EXPERTS, tm), jnp.bool_)
    for _ in range(TOPK_GROUPS):
        mx = jnp.max(gscore, axis=0, keepdims=True)
        gi = jnp.min(jnp.where(gscore == mx, grp_io, N_GROUPS), axis=0, keepdims=True)
        hit = grp_io == gi
        gmask = jnp.logical_or(gmask, hit)
        gscore = jnp.where(hit, neg, gscore)

    cur = jnp.where(gmask, sel, neg)
    row_io = lax.broadcasted_iota(I32, (TOP_K, tm), 0)
    eidx = jnp.zeros((TOP_K, tm), I32)
    wraw = jnp.zeros((TOP_K, tm), F32)
    chosen = jnp.zeros((N_EXPERTS, tm), jnp.bool_)
    hits = []
    for kk in range(TOP_K):
        mx = jnp.max(cur, axis=0, keepdims=True)
        ei = jnp.min(jnp.where(cur == mx, e_io, N_EXPERTS), axis=0, keepdims=True)
        hit = e_io == ei
        hits.append(hit)
        sc_k = jnp.sum(jnp.where(hit, scores, 0.0), axis=0, keepdims=True)
        eidx = jnp.where(row_io == kk, ei, eidx)
        wraw = jnp.where(row_io == kk, sc_k, wraw)
        chosen = jnp.logical_or(chosen, hit)
        cur = jnp.where(hit, neg, cur)

    wsum = jnp.sum(wraw, axis=0, keepdims=True)
    wts_ref[...] = wraw / wsum * ROUTED_SCALE
    eidx_ref[...] = eidx

    chosen_f = jnp.where(chosen, 1.0, 0.0)
    r_io = lax.broadcasted_iota(I32, (tm, tm), 0)
    c_io = lax.broadcasted_iota(I32, (tm, tm), 1)
    strict_upper = jnp.where(r_io < c_io, 1.0, 0.0).astype(BF16)
    prefix = jnp.dot(chosen_f.astype(BF16), strict_upper, preferred_element_type=F32)
    rank = prefix + carry_scr[:, 0:1]
    pos = jnp.zeros((TOP_K, tm), F32)
    for kk in range(TOP_K):
        p_k = jnp.sum(jnp.where(hits[kk], rank, 0.0), axis=0, keepdims=True)
        pos = jnp.where(row_io == kk, p_k, pos)
    pos_ref[...] = pos.astype(I32)
    total = carry_scr[...] + jnp.sum(chosen_f, axis=1, keepdims=True)
    carry_scr[...] = total
    cnt_ref[...] = total.astype(I32)


def _route(h, sh, sc, nw, rw_t, rb_col, seq):
    t, d = h.shape
    tm = min(ROUTE_ROWS, seq)
    per_b = seq // tm
    kspec = pl.BlockSpec((TOP_K, tm), lambda i: (0, i))
    return pl.pallas_call(
        _route_body,
        grid=(t // tm,),
        in_specs=[pl.BlockSpec((tm, d), lambda i: (i, 0)),
                  pl.BlockSpec((1, 1, d), lambda i: (i // per_b, 0, 0)),
                  pl.BlockSpec((1, 1, d), lambda i: (i // per_b, 0, 0)),
                  pl.BlockSpec((1, d), lambda i: (0, 0)),
                  pl.BlockSpec((N_EXPERTS, d), lambda i: (0, 0)),
                  pl.BlockSpec((N_EXPERTS, 1), lambda i: (0, 0))],
        out_specs=[pl.BlockSpec((tm, d // 2), lambda i: (i, 0)), kspec, kspec, kspec,
                   pl.BlockSpec((N_EXPERTS, 128), lambda i: (0, 0))],
        out_shape=[jax.ShapeDtypeStruct((t, d // 2), U32),
                   jax.ShapeDtypeStruct((TOP_K, t), I32),
                   jax.ShapeDtypeStruct((TOP_K, t), F32),
                   jax.ShapeDtypeStruct((TOP_K, t), I32),
                   jax.ShapeDtypeStruct((N_EXPERTS, 128), I32)],
        scratch_shapes=[pltpu.VMEM((N_EXPERTS, 128), F32)],
        compiler_params=_cparams(("arbitrary",)),
        name="moe_route",
    )(h, sh, sc, nw, rw_t, rb_col)


def _sc_worker_id():
    return lax.axis_index("s") * SC_CORES + lax.axis_index("c")


def _sc_scatter_rows(x, dest, n_rows):
    t, d = x.shape
    n_k = dest.shape[0]
    assert t % (SC_WORKERS * 2 * SC_CHUNK) == 0
    per_w = t // SC_WORKERS
    n_ch = per_w // SC_CHUNK
    dest4 = dest.reshape(n_k, SC_WORKERS, n_ch, SC_CHUNK).transpose(1, 2, 0, 3)
    mesh = plsc.VectorSubcoreMesh(core_axis_name="c", subcore_axis_name="s")

    @functools.partial(
        pl.kernel, mesh=mesh,
        out_type=jax.ShapeDtypeStruct((n_rows, d), x.dtype),
        scratch_types=[pltpu.VMEM((n_ch, n_k, SC_CHUNK), I32),
                       pltpu.VMEM((SC_CHUNK, d), x.dtype),
                       pltpu.VMEM((SC_CHUNK, d), x.dtype)] + [pltpu.SemaphoreType.DMA] * 4,
        name="moe_dispatch_sc",
    )
    def scatter_kernel(x_hbm, dest_hbm, out_hbm, idx_v, rows0, rows1, l0, l1, s0, s1):
        wid = _sc_worker_id()
        base = wid * per_w
        pltpu.sync_copy(dest_hbm.at[wid], idx_v)
        bufs = ((rows0, l0, s0), (rows1, l1, s1))

        def load(ci, b):
            rows, load_sem, _ = bufs[b]
            return pltpu.make_async_copy(x_hbm.at[pl.ds(base + ci * SC_CHUNK, SC_CHUNK)], rows,
                                         load_sem)

        def scatters(ci, b):
            rows, _, scatter_sem = bufs[b]
            return [pltpu.make_async_copy(rows, out_hbm.at[idx_v.at[ci, j]], scatter_sem)
                    for j in range(n_k)]

        load(0, 0).start()

        @pl.loop(0, n_ch, step=2)
        def _(ci):
            load(ci, 0).wait()
            for cp in scatters(ci, 0):
                cp.start()

            @pl.when(ci > 0)
            def _():
                for cp in scatters(ci - 1, 1):
                    cp.wait()

            load(ci + 1, 1).start()
            load(ci + 1, 1).wait()
            for cp in scatters(ci + 1, 1):
                cp.start()
            for cp in scatters(ci, 0):
                cp.wait()

            @pl.when(ci + 2 < n_ch)
            def _():
                load(ci + 2, 0).start()

        for cp in scatters(n_ch - 1, 1):
            cp.wait()

    return scatter_kernel(x, dest4)


def _sc_gather_rows(table, idx):
    n, d = idx.shape[0], table.shape[1]
    assert n % (SC_WORKERS * 2 * SC_CHUNK) == 0
    per_w = n // SC_WORKERS
    n_ch = per_w // SC_CHUNK
    idx3 = idx.reshape(SC_WORKERS, n_ch, SC_CHUNK)
    mesh = plsc.VectorSubcoreMesh(core_axis_name="c", subcore_axis_name="s")

    @functools.partial(
        pl.kernel, mesh=mesh,
        out_type=jax.ShapeDtypeStruct((n, d), table.dtype),
        scratch_types=[pltpu.VMEM((n_ch, SC_CHUNK), I32),
                       pltpu.VMEM((SC_CHUNK, d), table.dtype),
                       pltpu.VMEM((SC_CHUNK, d), table.dtype)] + [pltpu.SemaphoreType.DMA] * 4,
        name="moe_combine_sc",
    )
    def gather_kernel(table_hbm, idx_hbm, out_hbm, idx_v, rows0, rows1, g0, g1, w0, w1):
        wid = _sc_worker_id()
        base = wid * per_w
        pltpu.sync_copy(idx_hbm.at[wid], idx_v)
        bufs = ((rows0, g0, w0), (rows1, g1, w1))

        def gather(ci, b):
            rows, gather_sem, _ = bufs[b]
            return pltpu.make_async_copy(table_hbm.at[idx_v.at[ci]], rows, gather_sem)

        def write(ci, b):
            rows, _, write_sem = bufs[b]
            return pltpu.make_async_copy(rows, out_hbm.at[pl.ds(base + ci * SC_CHUNK, SC_CHUNK)],
                                         write_sem)

        gather(0, 0).start()

        @pl.loop(0, n_ch, step=2)
        def _(ci):
            gather(ci, 0).wait()
            write(ci, 0).start()

            @pl.when(ci > 0)
            def _():
                write(ci - 1, 1).wait()

            gather(ci + 1, 1).start()
            gather(ci + 1, 1).wait()
            write(ci + 1, 1).start()
            write(ci, 0).wait()

            @pl.when(ci + 2 < n_ch)
            def _():
                gather(ci + 2, 0).start()

        write(n_ch - 1, 1).wait()

    return gather_kernel(table, idx3)


def _expert_body(blk_e_ref, blk_first_ref, blk_valid_ref, x_ref, w1_ref, w3_ref, w2_ref, y_ref,
                 w1_scr, w3_scr, w2_scr):
    del blk_e_ref
    j = pl.program_id(0)
    valid = blk_valid_ref[j]

    @pl.when(blk_first_ref[j] == 1)
    def _():
        w1_scr[...] = w1_ref[0].astype(BF16)
        w3_scr[...] = w3_ref[0].astype(BF16)
        w2_scr[...] = w2_ref[0].astype(BF16)

    @pl.when(valid > 0)
    def _():
        half = x_ref.shape[1]
        rows = lax.broadcasted_iota(I32, x_ref.shape, 0)
        lo, hi = _unpack_bf16_pair(jnp.where(rows < valid, x_ref[...], 0))
        lo, hi = lo.astype(BF16), hi.astype(BF16)

        def proj(w_scr):
            return (jnp.dot(lo, w_scr[0:half, :], preferred_element_type=F32)
                    + jnp.dot(hi, w_scr[half:2 * half, :], preferred_element_type=F32))

        hid = _silu(proj(w1_scr)) * proj(w3_scr)
        y = jnp.dot(hid.astype(BF16), w2_scr[...], preferred_element_type=F32)
        y_ref[...] = _pack_bf16_pair(y[:, :half], y[:, half:])

    @pl.when(valid == 0)
    def _():
        y_ref[...] = jnp.zeros(y_ref.shape, U32)


def _experts(xg, blk_e, blk_first, blk_valid, w1, w3, w2, layer):
    n_rows, half = xg.shape
    n_blocks = n_rows // EXPERT_ROWS
    d, de = w1.shape[-2:]
    grid_spec = pltpu.PrefetchScalarGridSpec(
        num_scalar_prefetch=3,
        grid=(n_blocks,),
        in_specs=[pl.BlockSpec((EXPERT_ROWS, half), lambda j, be, bf, bv: (j, 0)),
                  pl.BlockSpec((None, 1, d, de), lambda j, be, bf, bv: (layer, be[j], 0, 0)),
                  pl.BlockSpec((None, 1, d, de), lambda j, be, bf, bv: (layer, be[j], 0, 0)),
                  pl.BlockSpec((None, 1, de, d), lambda j, be, bf, bv: (layer, be[j], 0, 0))],
        out_specs=pl.BlockSpec((EXPERT_ROWS, half), lambda j, be, bf, bv: (j, 0)),
        scratch_shapes=[pltpu.VMEM((d, de), BF16), pltpu.VMEM((d, de), BF16),
                        pltpu.VMEM((de, d), BF16)],
    )
    return pl.pallas_call(
        _expert_body,
        grid_spec=grid_spec,
        out_shape=jax.ShapeDtypeStruct((n_rows, half), U32),
        compiler_params=_cparams(("arbitrary",)),
        name="moe_experts",
    )(blk_e, blk_first, blk_valid, xg, w1, w3, w2)


def _combine_body(h_ref, u_ref, yg_ref, wts_ref, gt_ref, s1_ref, s3_ref, s2_ref, fw_ref, o_ref, *,
                  final_norm):
    half = u_ref.shape[1]
    lo, hi = _unpack_bf16_pair(u_ref[...])
    lo, hi = lo.astype(BF16), hi.astype(BF16)

    def proj(w_ref):
        return (jnp.dot(lo, w_ref[0:half, :], preferred_element_type=F32)
                + jnp.dot(hi, w_ref[half:2 * half, :], preferred_element_type=F32))

    hid = _silu(proj(s1_ref)) * proj(s3_ref)
    shared = jnp.dot(hid.astype(BF16), s2_ref[...], preferred_element_type=F32)

    routed_lo = routed_hi = None
    for kk in range(TOP_K):
        y_lo, y_hi = _unpack_bf16_pair(yg_ref[kk])
        w = wts_ref[:, kk:kk + 1]
        routed_lo = y_lo * w if routed_lo is None else routed_lo + y_lo * w
        routed_hi = y_hi * w if routed_hi is None else routed_hi + y_hi * w
    gt = gt_ref[0]
    out_lo = h_ref[:, 0:half] + gt[:, 0:half] * (routed_lo + shared[:, 0:half])
    out_hi = (h_ref[:, half:2 * half]
              + gt[:, half:2 * half] * (routed_hi + shared[:, half:2 * half]))
    if final_norm:
        ssq = (jnp.sum(out_lo * out_lo, axis=-1, keepdims=True)
               + jnp.sum(out_hi * out_hi, axis=-1, keepdims=True))
        inv = lax.rsqrt(ssq / (2 * half) + EPS)
        out_lo = (out_lo * inv) * fw_ref[:, 0:half]
        out_hi = (out_hi * inv) * fw_ref[:, half:2 * half]
    o_ref[:, 0:half] = out_lo
    o_ref[:, half:2 * half] = out_hi


def _combine(h, u, yg, wts_tk, gt, s1, s3, s2, final_w, seq, final_norm, part, n_parts):
    t, d = h.shape
    half = d // 2
    tm = min(COMBINE_ROWS, seq)
    per_b = seq // tm
    ds_ = s1.shape[-1]
    steps = t // n_parts // tm
    off = part * steps
    return pl.pallas_call(
        functools.partial(_combine_body, final_norm=final_norm),
        grid=(steps,),
        in_specs=[pl.BlockSpec((tm, d), lambda i: (i + off, 0)),
                  pl.BlockSpec((tm, half), lambda i: (i + off, 0)),
                  pl.BlockSpec((TOP_K, tm, half), lambda i: (0, i, 0)),
                  pl.BlockSpec((tm, TOP_K), lambda i: (i + off, 0)),
                  pl.BlockSpec((1, 1, d), lambda i: ((i + off) // per_b, 0, 0)),
                  pl.BlockSpec((d, ds_), lambda i: (0, 0)),
                  pl.BlockSpec((d, ds_), lambda i: (0, 0)),
                  pl.BlockSpec((ds_, d), lambda i: (0, 0)),
                  pl.BlockSpec((1, d), lambda i: (0, 0))],
        out_specs=pl.BlockSpec((tm, d), lambda i: (i + off, 0)),
        out_shape=jax.ShapeDtypeStruct((t, d), F32),
        input_output_aliases={0: 0},
        compiler_params=_cparams(("parallel",)),
        name="moe_combine",
    )(h, u, yg, wts_tk, gt, s1, s3, s2, final_w)


def _split_w_in(w_in):
    sizes = (2 * M_HEADS * M_DQK, M_HEADS * M_DV, M_HEADS * M_DV, M_HEADS, M_HEADS,
             G_HEADS * G_DK, G_HEADS * G_DK, G_HEADS * G_DV, G_RANK, G_HEADS * G_DV,
             D_MODEL, D_MODEL)
    offs = [0]
    for n in sizes:
        offs.append(offs[-1] + n)
    w16 = w_in.astype(BF16)
    big = jnp.concatenate([w16[:, offs[0]:offs[3]], w16[:, offs[5]:offs[8]], w16[:, offs[9]:offs[12]]],
                          axis=1)
    pad = jnp.zeros((w_in.shape[0], SMALL_COLS - 2 * M_HEADS - G_RANK), BF16)
    small = jnp.concatenate([w16[:, offs[3]:offs[5]], w16[:, offs[8]:offs[9]], pad], axis=1)
    return big, small


def _moe_layout(counts, eidx, pos, n_blocks):
    padded = (counts + EXPERT_ROWS - 1) // EXPERT_ROWS * EXPERT_ROWS
    pend = jnp.cumsum(padded)
    pstart = pend - padded
    experts = jnp.arange(N_EXPERTS, dtype=I32)
    dest = pos + jnp.sum(jnp.where(eidx[..., None] == experts, pstart, 0), axis=-1)
    blk_start = jnp.arange(n_blocks, dtype=I32) * EXPERT_ROWS
    owner = jnp.sum((pend[None, :] <= blk_start[:, None]).astype(I32), axis=1)
    blk_e = jnp.minimum(owner, N_EXPERTS - 1)
    prev = jnp.concatenate([jnp.full((1,), -1, I32), blk_e[:-1]])
    blk_first = (blk_e != prev).astype(I32)
    own = blk_e[:, None] == experts[None, :]
    rows_left = jnp.sum(jnp.where(own, (pstart + counts)[None, :], 0), axis=1) - blk_start
    blk_valid = jnp.clip(jnp.where(owner < N_EXPERTS, rows_left, 0), 0, EXPERT_ROWS)
    return dest.astype(I32), blk_e.astype(I32), blk_first, blk_valid.astype(I32)


def kernel(x, c, ada_w, ada_b, norm_mix_w, norm_moe_w, w_in, m_conv_w, m_gate_b, m_norm_w,
           g_alpha_w, g_alpha_b, g_norm_w, w_pa, w_pb, w_o, router_w, router_b,
           exp_w1, exp_w3, exp_w2, sh_w1, sh_w3, sh_w2, final_norm_w):
    bsz, seq, d = x.shape
    depth = ada_w.shape[0]
    t = bsz * seq
    n_rows = t * TOP_K + N_EXPERTS * EXPERT_ROWS
    n_blocks = n_rows // EXPERT_ROWS

    ada = _ada(c, ada_w, ada_b).reshape(depth, bsz, 6, 1, d)
    h = x.reshape(t, d)
    for l in range(depth):
        sh1, sc1, gt1, sh2, sc2, gt2 = (ada[l, :, i] for i in range(6))

        w_big, w_small = _split_w_in(w_in[l])
        mix_w = norm_mix_w[l][None, :]
        gate_row = jnp.zeros((1, SMALL_COLS), F32)
        gate_row = gate_row.at[0, SMALL_I:SMALL_I + M_HEADS].set(m_gate_b[l, 0])
        gate_row = gate_row.at[0, SMALL_F:SMALL_F + M_HEADS].set(m_gate_b[l, 1])
        ha = _mlstm(h, sh1, sc1, mix_w, w_big, w_small, m_conv_w[l], gate_row, m_norm_w[l][None, :],
                    bsz, seq)
        alpha_full = jnp.zeros((SMALL_COLS, G_HEADS * G_DK), F32)
        alpha_full = alpha_full.at[SMALL_R:SMALL_R + G_RANK].set(g_alpha_w[l]).astype(BF16)
        hb = _gla(h, sh1, sc1, mix_w, w_big, w_small, alpha_full, g_alpha_b[l][None, :],
                  g_norm_w[l][None, :], bsz, seq)
        h = _mixout(h, sh1, sc1, mix_w, w_big, ha, hb, gt1, w_pa[l].astype(BF16),
                    w_pb[l].astype(BF16), w_o[l].astype(BF16), seq)

        u, eidx, wts, pos, cnt = _route(h, sh2, sc2, norm_moe_w[l][None, :],
                                        router_w[l].T, router_b[l][:, None], seq)
        dest, blk_e, blk_first, blk_valid = _moe_layout(cnt[:, 0], eidx, pos, n_blocks)
        xg = _sc_scatter_rows(u, dest, n_rows)
        y = _experts(xg, blk_e, blk_first, blk_valid, exp_w1, exp_w3, exp_w2, l)
        tp = t // COMBINE_PARTS
        for p in range(COMBINE_PARTS):
            dest_p = dest[:, p * tp:(p + 1) * tp].reshape(-1)
            yg = _sc_gather_rows(y, dest_p).reshape(TOP_K, tp, d // 2)
            h = _combine(h, u, yg, wts.T, gt2, sh_w1[l].astype(BF16), sh_w3[l].astype(BF16),
                         sh_w2[l].astype(BF16), final_norm_w[None, :], seq,
                         final_norm=(l == depth - 1), part=p, n_parts=COMBINE_PARTS)

    return h.reshape(bsz, seq, d)
```

```python
import functools

import jax
import jax.numpy as jnp
import numpy as np
from jax import lax
from jax.experimental import pallas as pl
from jax.experimental.pallas import tpu as pltpu
from jax.experimental.pallas import tpu_sc as plsc

F32 = jnp.float32
BF16 = jnp.bfloat16
I32 = jnp.int32
U32 = jnp.uint32
HI_MASK = np.uint32(0xFFFF0000)

SC_CORES = 2
SC_SUBCORES = 16
SC_WORKERS = SC_CORES * SC_SUBCORES
SC_CHUNK = 64

D_MODEL = 1024
M_HEADS = 4
M_DQK = 128
M_DV = 256
M_CONV = 4
GATE_CAP = 15.0
G_HEADS = 4
G_DK = 128
G_DV = 256
G_RANK = 16
G_TAU = 16.0
G_CHUNK = 64
N_EXPERTS = 64
TOP_K = 8
N_GROUPS = 8
GROUP_SIZE = N_EXPERTS // N_GROUPS
TOPK_GROUPS = 4
D_EXPERT = 256
D_SHARED = 256
ROUTED_SCALE = 2.5
EPS = 1e-6

M_CHUNK_ROWS = 256
G_BLOCK_ROWS = 256
SEQ_GROUP = 2
PROJ_PIECE_COLS = 256
MIX_ROWS = 512
ROUTE_ROWS = 512
EXPERT_ROWS = 1024
COMBINE_ROWS = 512
COMBINE_PARTS = 2
CONV_HALO = 8
VMEM_LIMIT = 48 * 1024 * 1024

COL_QM, COL_OM = 0, 2048
COL_QG, COL_ZG, COL_GA, COL_GB = 3072, 5120, 6144, 7168
BIG_COLS = 8192
SMALL_COLS = 128
SMALL_I, SMALL_F, SMALL_R = 0, M_HEADS, 2 * M_HEADS


def _cparams(sem, vmem=VMEM_LIMIT):
    return pltpu.CompilerParams(dimension_semantics=sem, vmem_limit_bytes=vmem)


def _silu(x):
    return x * jax.nn.sigmoid(x)


def _log_sigmoid(x):
    return jnp.minimum(x, 0.0) - jnp.log1p(jnp.exp(-jnp.abs(x)))


def _modulated_rmsnorm(x, w, sc, sh):
    y = x * lax.rsqrt(jnp.mean(x * x, axis=-1, keepdims=True) + EPS)
    return (y * w) * (1.0 + sc) + sh


def _pack_bf16_pair(lo, hi):
    lo_bits = lax.bitcast_convert_type(lo.astype(BF16).astype(F32), U32)
    hi_bits = lax.bitcast_convert_type(hi.astype(BF16).astype(F32), U32)
    return (lo_bits >> 16) | (hi_bits & HI_MASK)


def _unpack_bf16_pair(packed):
    lo = lax.bitcast_convert_type(packed << 16, F32)
    hi = lax.bitcast_convert_type(packed & HI_MASK, F32)
    return lo, hi


def _lower_tri(n, dtype):
    r = lax.broadcasted_iota(I32, (n, n), 0)
    c = lax.broadcasted_iota(I32, (n, n), 1)
    return (r >= c).astype(dtype)


def _split3_dot(lhs01, x):
    hi = x.astype(BF16)
    r1 = x - hi.astype(F32)
    mid = r1.astype(BF16)
    lo = (r1 - mid.astype(F32)).astype(BF16)
    return (jnp.dot(lhs01, hi, preferred_element_type=F32)
            + jnp.dot(lhs01, mid, preferred_element_type=F32)
            + jnp.dot(lhs01, lo, preferred_element_type=F32))


def _split2_dot_nt(a, b):
    nt = (((1,), (1,)), ((), ()))
    a_hi = a.astype(BF16)
    a_lo = (a - a_hi.astype(F32)).astype(BF16)
    b_hi = b.astype(BF16)
    b_lo = (b - b_hi.astype(F32)).astype(BF16)
    return (lax.dot_general(a_hi, b_hi, nt, preferred_element_type=F32)
            + lax.dot_general(a_hi, b_lo, nt, preferred_element_type=F32)
            + lax.dot_general(a_lo, b_hi, nt, preferred_element_type=F32))


def _ada_body(c_ref, w_ref, b_ref, o_ref):
    cond = _silu(c_ref[...])
    o_ref[0] = jnp.dot(cond.astype(BF16), w_ref[0].astype(BF16),
                       preferred_element_type=F32) + b_ref[0]


def _ada(c, ada_w, ada_b):
    depth, d, six_d = ada_w.shape
    bsz = c.shape[0]
    nj = six_d // d
    return pl.pallas_call(
        _ada_body,
        grid=(depth, nj),
        in_specs=[pl.BlockSpec((bsz, d), lambda l, j: (0, 0)),
                  pl.BlockSpec((1, d, d), lambda l, j: (l, 0, j)),
                  pl.BlockSpec((1, 1, d), lambda l, j: (l, 0, j))],
        out_specs=pl.BlockSpec((1, bsz, d), lambda l, j: (l, 0, j)),
        out_shape=jax.ShapeDtypeStruct((depth, bsz, six_d), F32),
        compiler_params=_cparams(("parallel", "parallel")),
        name="ada_ln",
    )(c, ada_w, ada_b.reshape(depth, 1, six_d))


def _mlstm_body(h0_ref, hn_ref, sh_ref, sc_ref, mw_ref, w_ref, ws_ref, cw_ref, gb_ref, nw_ref,
                ha_ref, qk_nxt, v_nxt, o_nxt, sm_nxt, xe_scr, v_scr, o_scr, sm_scr,
                c_scr, n_scr, m_scr):
    n_seq, rows = hn_ref.shape[0], hn_ref.shape[1]
    half = M_HEADS * M_DQK
    vcols = M_HEADS * M_DV

    def projection_pieces(h_ref):
        u = jnp.concatenate(
            [_modulated_rmsnorm(h_ref[g], mw_ref[...], sc_ref[g], sh_ref[g]) for g in range(n_seq)],
            axis=0).astype(BF16)

        def piece(dst, w_lo, lo, width):
            def run():
                dst[:, lo:lo + width] = jnp.dot(u, w_ref[:, w_lo + lo:w_lo + lo + width],
                                                preferred_element_type=F32).astype(dst.dtype)
            return run

        def small_piece():
            sm_nxt[...] = jnp.dot(u, ws_ref[...], preferred_element_type=F32)

        pieces = [small_piece]
        for dst, w_lo in ((qk_nxt, 0), (v_nxt, 2 * half), (o_nxt, 2 * half + vcols)):
            for lo in range(0, vcols, PROJ_PIECE_COLS):
                pieces.append(piece(dst, w_lo, lo, PROJ_PIECE_COLS))
        return pieces

    @pl.when(pl.program_id(1) == 0)
    def _():
        xe_scr[:, 0:CONV_HALO, :] = jnp.zeros((n_seq, CONV_HALO, 2 * half), F32)
        c_scr[...] = jnp.zeros(c_scr.shape, F32)
        n_scr[...] = jnp.zeros(n_scr.shape, F32)
        m_scr[...] = jnp.zeros(m_scr.shape, F32)
        for run in projection_pieces(h0_ref):
            run()

    for g in range(n_seq):
        xe_scr[g, CONV_HALO:CONV_HALO + rows, :] = qk_nxt[g * rows:(g + 1) * rows, :]
    v_scr[...] = v_nxt[...]
    o_scr[...] = o_nxt[...]
    sm_scr[...] = sm_nxt[...]
    pending = projection_pieces(hn_ref)

    r_io = lax.broadcasted_iota(I32, (rows, rows), 0)
    c_io = lax.broadcasted_iota(I32, (rows, rows), 1)
    causal = r_io >= c_io
    tri = _lower_tri(rows, BF16)

    for g in range(n_seq):
        gr = slice(g * rows, (g + 1) * rows)
        conv = None
        for j in range(M_CONV):
            off = CONV_HALO - (M_CONV - 1) + j
            term = xe_scr[g, off:off + rows, :] * cw_ref[j:j + 1, :]
            conv = term if conv is None else conv + term
        qk = _silu(conv)
        xe_scr[g, 0:CONV_HALO, :] = xe_scr[g, rows:rows + CONV_HALO, :]

        capped = GATE_CAP * jnp.tanh((sm_scr[gr, :] + gb_ref[...]) / GATE_CAP)
        li_all = capped
        lf_all = _log_sigmoid(capped)
        b_all = _split3_dot(tri, lf_all)
        li_t = li_all.T
        b_t = b_all.T

        for h in range(M_HEADS):
            if pending:
                pending.pop(0)()
            sidx = g * M_HEADS + h
            q = (qk[:, h * M_DQK:(h + 1) * M_DQK] * (M_DQK ** -0.5)).astype(BF16)
            k = qk[:, half + h * M_DQK:half + (h + 1) * M_DQK]
            kb = k.astype(BF16)
            v = v_scr[gr, h * M_DV:(h + 1) * M_DV]
            li_c = li_all[:, SMALL_I + h:SMALL_I + h + 1]
            b_c = b_all[:, SMALL_F + h:SMALL_F + h + 1]
            li_r = li_t[SMALL_I + h:SMALL_I + h + 1, :]
            b_r = b_t[SMALL_F + h:SMALL_F + h + 1, :]
            gsum = b_c[rows - 1:rows, :]
            m_prev = m_scr[sidx][:, 0:1]
            c_prev = c_scr[sidx]
            n_prev = n_scr[sidx]

            d_mat = jnp.where(causal, b_c - b_r + li_r, -jnp.inf)
            m_inter = b_c + m_prev
            m_t = jnp.maximum(jnp.max(d_mat, axis=1, keepdims=True), m_inter)
            s = lax.dot_general(q, kb, (((1,), (1,)), ((), ())), preferred_element_type=F32)
            p = jnp.exp(d_mat - m_t) * s
            w_inter = jnp.exp(m_inter - m_t)
            num = (jnp.dot(p.astype(BF16), v, preferred_element_type=F32)
                   + w_inter * jnp.dot(q, c_prev.astype(BF16), preferred_element_type=F32))
            qn = jnp.sum(q.astype(F32) * n_prev, axis=1, keepdims=True)
            den = jnp.sum(p, axis=1, keepdims=True) + w_inter * qn
            hh = num / jnp.maximum(jnp.abs(den), jnp.exp(-m_t))

            if pending:
                pending.pop(0)()

            a_r = gsum - b_r + li_r
            a_c = gsum - b_c + li_c
            m_new = jnp.maximum(gsum + m_prev, jnp.max(a_r, axis=1, keepdims=True))
            decay = jnp.exp(gsum + m_prev - m_new)
            wk = jnp.exp(a_c - m_new) * k
            c_scr[sidx] = decay * c_prev + lax.dot_general(
                wk.astype(BF16), v, (((0,), (0,)), ((), ())), preferred_element_type=F32)
            n_scr[sidx] = decay * n_prev + jnp.sum(wk, axis=0, keepdims=True)
            m_scr[sidx] = jnp.broadcast_to(m_new, m_scr.shape[1:])

            y = hh * lax.rsqrt(jnp.mean(hh * hh, axis=-1, keepdims=True) + EPS)
            y = y * nw_ref[:, h * M_DV:(h + 1) * M_DV]
            gate = jax.nn.sigmoid(o_scr[gr, h * M_DV:(h + 1) * M_DV])
            ha_ref[g, :, h * M_DV:(h + 1) * M_DV] = (y * gate).astype(BF16)
    for run in pending:
        run()


def _mlstm(h, sh, sc, mix_w, w_big, w_small, conv_w, gate_row, norm_w, bsz, seq):
    t, d = h.shape
    rows = min(M_CHUNK_ROWS, seq)
    nc = seq // rows
    ng = SEQ_GROUP if bsz % SEQ_GROUP == 0 else 1
    half = M_HEADS * M_DQK
    vcols = M_HEADS * M_DV
    wcols = 2 * half + 2 * vcols
    assert COL_QM == 0 and COL_OM + vcols == wcols
    resident = pl.Buffered(1)
    nxt = [pltpu.VMEM((ng * rows, 2 * half), F32), pltpu.VMEM((ng * rows, vcols), BF16),
           pltpu.VMEM((ng * rows, vcols), F32), pltpu.VMEM((ng * rows, SMALL_COLS), F32)]
    out = pl.pallas_call(
        _mlstm_body,
        grid=(bsz // ng, nc),
        in_specs=[pl.BlockSpec((ng, rows, d), lambda b, c: (b, 0, 0)),
                  pl.BlockSpec((ng, rows, d), lambda b, c: (b, jnp.minimum(c + 1, nc - 1), 0)),
                  pl.BlockSpec((ng, 1, d), lambda b, c: (b, 0, 0)),
                  pl.BlockSpec((ng, 1, d), lambda b, c: (b, 0, 0)),
                  pl.BlockSpec((1, d), lambda b, c: (0, 0)),
                  pl.BlockSpec((d, wcols), lambda b, c: (0, 0), pipeline_mode=resident),
                  pl.BlockSpec((d, SMALL_COLS), lambda b, c: (0, 0), pipeline_mode=resident),
                  pl.BlockSpec((M_CONV, 2 * half), lambda b, c: (0, 0)),
                  pl.BlockSpec((1, SMALL_COLS), lambda b, c: (0, 0)),
                  pl.BlockSpec((1, vcols), lambda b, c: (0, 0))],
        out_specs=pl.BlockSpec((ng, rows, vcols), lambda b, c: (b, c, 0)),
        out_shape=jax.ShapeDtypeStruct((bsz, seq, vcols), BF16),
        scratch_shapes=nxt + [pltpu.VMEM((ng, rows + CONV_HALO, 2 * half), F32)] + nxt[1:]
        + [pltpu.VMEM((ng * M_HEADS, M_DQK, M_DV), F32),
           pltpu.VMEM((ng * M_HEADS, 1, M_DQK), F32),
           pltpu.VMEM((ng * M_HEADS, 1, 128), F32)],
        compiler_params=_cparams(("parallel", "arbitrary")),
        name="mlstm",
    )(h.reshape(bsz, seq, d), h.reshape(bsz, seq, d), sh, sc, mix_w, w_big, w_small, conv_w,
      gate_row, norm_w)
    return out.reshape(t, vcols)


def _gla_body(h0_ref, hn_ref, sh_ref, sc_ref, mw_ref, w_ref, ws_ref, aw_ref, ab_ref, nw_ref, hb_ref,
              qk_nxt, v_nxt, z_nxt, sm_nxt, qk_scr, v_scr, z_scr, sm_scr, st_scr):
    n_seq, rows = hn_ref.shape[0], hn_ref.shape[1]
    n_chunks = rows // G_CHUNK
    kcols = G_HEADS * G_DK
    vcols = G_HEADS * G_DV

    def projection_pieces(h_ref):
        u = jnp.concatenate(
            [_modulated_rmsnorm(h_ref[g], mw_ref[...], sc_ref[g], sh_ref[g]) for g in range(n_seq)],
            axis=0).astype(BF16)

        def piece(dst, w_lo, lo, width):
            def run():
                dst[:, lo:lo + width] = jnp.dot(u, w_ref[:, w_lo + lo:w_lo + lo + width],
                                                preferred_element_type=F32).astype(dst.dtype)
            return run

        def small_piece():
            sm_nxt[...] = jnp.dot(u, ws_ref[...], preferred_element_type=F32)

        pieces = [small_piece]
        for dst, w_lo in ((qk_nxt, 0), (v_nxt, 2 * kcols), (z_nxt, 2 * kcols + vcols)):
            for lo in range(0, vcols, PROJ_PIECE_COLS):
                pieces.append(piece(dst, w_lo, lo, PROJ_PIECE_COLS))
        return pieces

    @pl.when(pl.program_id(1) == 0)
    def _():
        st_scr[...] = jnp.zeros(st_scr.shape, F32)
        for run in projection_pieces(h0_ref):
            run()

    qk_scr[...] = qk_nxt[...]
    v_scr[...] = v_nxt[...]
    z_scr[...] = z_nxt[...]
    sm_scr[...] = sm_nxt[...]
    pending = projection_pieces(hn_ref)

    r_io = lax.broadcasted_iota(I32, (rows, rows), 0)
    c_io = lax.broadcasted_iota(I32, (rows, rows), 1)
    chunk_causal = jnp.logical_and(r_io >= c_io, r_io // G_CHUNK == c_io // G_CHUNK)
    chunk_tri = jnp.where(chunk_causal, 1.0, 0.0).astype(BF16)

    for g in range(n_seq):
        gr = slice(g * rows, (g + 1) * rows)
        for _ in range((len(pending) + n_seq - 1 - g) // (n_seq - g)):
            pending.pop(0)()
        q = qk_scr[gr, 0:kcols] * (G_DK ** -0.5)
        k = qk_scr[gr, kcols:2 * kcols]

        logits = jnp.dot(sm_scr[gr, :].astype(BF16), aw_ref[...], preferred_element_type=F32)
        la = _log_sigmoid(logits + ab_ref[...]) / G_TAU
        bc = _split3_dot(chunk_tri, la)
        gcs = [bc[(ci + 1) * G_CHUNK - 1:(ci + 1) * G_CHUNK, :] for ci in range(n_chunks)]
        gc_rows = jnp.concatenate([jnp.broadcast_to(gc, (G_CHUNK, kcols)) for gc in gcs], axis=0)

        q_in = (q * jnp.exp(bc)).astype(BF16)
        k_in = (k * jnp.exp(-bc)).astype(BF16)
        k_out = (k * jnp.exp(gc_rows - bc)).astype(BF16)

        for h in range(G_HEADS):
            ks = slice(h * G_DK, (h + 1) * G_DK)
            vs = slice(h * G_DV, (h + 1) * G_DV)
            v = v_scr[gr, vs]
            att = lax.dot_general(q_in[:, ks], k_in[:, ks], (((1,), (1,)), ((), ())),
                                  preferred_element_type=F32)
            att = jnp.where(chunk_causal, att, 0.0).astype(BF16)
            o_intra = jnp.dot(att, v, preferred_element_type=F32)
            st = st_scr[g * G_HEADS + h]
            outs = []
            for ci in range(n_chunks):
                rs = slice(ci * G_CHUNK, (ci + 1) * G_CHUNK)
                o_inter = lax.dot_general(q_in[rs, ks], st.astype(BF16),
                                          (((1,), (1,)), ((), ())), preferred_element_type=F32)
                outs.append(o_intra[rs, :] + o_inter)
                st = jnp.exp(gcs[ci][:, ks]) * st + lax.dot_general(
                    v[rs, :], k_out[rs, ks], (((0,), (0,)), ((), ())), preferred_element_type=F32)
            st_scr[g * G_HEADS + h] = st
            o = jnp.concatenate(outs, axis=0)
            y = o * lax.rsqrt(jnp.mean(o * o, axis=-1, keepdims=True) + EPS)
            y = y * nw_ref[:, vs]
            hb_ref[g, :, vs] = (y * _silu(z_scr[gr, vs])).astype(BF16)


def _gla(h, sh, sc, mix_w, w_big, w_small, alpha_full, alpha_b, norm_w, bsz, seq):
    t, d = h.shape
    rows = min(G_BLOCK_ROWS, seq)
    nb = seq // rows
    ng = SEQ_GROUP if bsz % SEQ_GROUP == 0 else 1
    kcols = G_HEADS * G_DK
    vcols = G_HEADS * G_DV
    wcols = 2 * kcols + 2 * vcols
    assert COL_QG % wcols == 0 and COL_ZG + vcols == COL_QG + wcols
    resident = pl.Buffered(1)
    nxt = [pltpu.VMEM((ng * rows, 2 * kcols), F32), pltpu.VMEM((ng * rows, vcols), BF16),
           pltpu.VMEM((ng * rows, vcols), F32), pltpu.VMEM((ng * rows, SMALL_COLS), F32)]
    out = pl.pallas_call(
        _gla_body,
        grid=(bsz // ng, nb),
        in_specs=[pl.BlockSpec((ng, rows, d), lambda b, c: (b, 0, 0)),
                  pl.BlockSpec((ng, rows, d), lambda b, c: (b, jnp.minimum(c + 1, nb - 1), 0)),
                  pl.BlockSpec((ng, 1, d), lambda b, c: (b, 0, 0)),
                  pl.BlockSpec((ng, 1, d), lambda b, c: (b, 0, 0)),
                  pl.BlockSpec((1, d), lambda b, c: (0, 0)),
                  pl.BlockSpec((d, wcols), lambda b, c: (0, COL_QG // wcols), pipeline_mode=resident),
                  pl.BlockSpec((d, SMALL_COLS), lambda b, c: (0, 0), pipeline_mode=resident),
                  pl.BlockSpec((SMALL_COLS, kcols), lambda b, c: (0, 0)),
                  pl.BlockSpec((1, kcols), lambda b, c: (0, 0)),
                  pl.BlockSpec((1, vcols), lambda b, c: (0, 0))],
        out_specs=pl.BlockSpec((ng, rows, vcols), lambda b, c: (b, c, 0)),
        out_shape=jax.ShapeDtypeStruct((bsz, seq, vcols), BF16),
        scratch_shapes=nxt + nxt + [pltpu.VMEM((ng * G_HEADS, G_DV, G_DK), F32)],
        compiler_params=_cparams(("parallel", "arbitrary")),
        name="gla",
    )(h.reshape(bsz, seq, d), h.reshape(bsz, seq, d), sh, sc, mix_w, w_big, w_small, alpha_full,
      alpha_b, norm_w)
    return out.reshape(t, vcols)


def _mixout_body(h_ref, sh_ref, sc_ref, mw_ref, wg_ref, ha_ref, hb_ref, gt_ref, wpa_ref, wpb_ref,
                 wo_ref, o_ref):
    d = h_ref.shape[1]
    h = h_ref[...]
    u = _modulated_rmsnorm(h, mw_ref[...], sc_ref[0], sh_ref[0]).astype(BF16)
    ga = jnp.dot(u, wg_ref[:, 0:d], preferred_element_type=F32)
    gb = jnp.dot(u, wg_ref[:, d:2 * d], preferred_element_type=F32)
    a = jnp.dot(ha_ref[...], wpa_ref[...], preferred_element_type=F32)
    b = jnp.dot(hb_ref[...], wpb_ref[...], preferred_element_type=F32)
    y = jax.nn.sigmoid(ga) * a + jax.nn.sigmoid(gb) * b
    o_ref[...] = h + gt_ref[0] * jnp.dot(y.astype(BF16), wo_ref[...], preferred_element_type=F32)


def _mixout(h, sh, sc, mix_w, w_big, ha, hb, gt, w_pa, w_pb, w_o, seq):
    t, d = h.shape
    tm = min(MIX_ROWS, seq)
    per_b = seq // tm
    assert COL_GA % (2 * d) == 0 and COL_GB == COL_GA + d
    resident = pl.Buffered(1)
    wspec = pl.BlockSpec((d, d), lambda i: (0, 0), pipeline_mode=resident)
    bspec = pl.BlockSpec((1, 1, d), lambda i: (i // per_b, 0, 0))
    return pl.pallas_call(
        _mixout_body,
        grid=(t // tm,),
        in_specs=[pl.BlockSpec((tm, d), lambda i: (i, 0)),
                  bspec, bspec,
                  pl.BlockSpec((1, d), lambda i: (0, 0)),
                  pl.BlockSpec((d, 2 * d), lambda i: (0, COL_GA // (2 * d)), pipeline_mode=resident),
                  pl.BlockSpec((tm, d), lambda i: (i, 0)),
                  pl.BlockSpec((tm, d), lambda i: (i, 0)),
                  bspec,
                  wspec, wspec, wspec],
        out_specs=pl.BlockSpec((tm, d), lambda i: (i, 0)),
        out_shape=jax.ShapeDtypeStruct((t, d), F32),
        compiler_params=_cparams(("parallel",)),
        name="mix_out",
    )(h, sh, sc, mix_w, w_big, ha, hb, gt, w_pa, w_pb, w_o)


def _route_body(h_ref, sh_ref, sc_ref, nw_ref, rwt_ref, rb_ref,
                u_ref, eidx_ref, wts_ref, pos_ref, cnt_ref, carry_scr):
    tm = h_ref.shape[0]

    @pl.when(pl.program_id(0) == 0)
    def _():
        carry_scr[...] = jnp.zeros(carry_scr.shape, F32)

    u = _modulated_rmsnorm(h_ref[...], nw_ref[...], sc_ref[0], sh_ref[0])
    half = u.shape[1] // 2
    u_ref[...] = _pack_bf16_pair(u[:, :half], u[:, half:])
    logits = _split2_dot_nt(rwt_ref[...], u)
    scores = jax.nn.sigmoid(logits)
    sel = scores + rb_ref[...]

    neg = -jnp.inf
    sub_io = lax.broadcasted_iota(I32, (GROUP_SIZE, tm), 0)
    pieces = []
    for g in range(N_GROUPS):
        blk = sel[g * GROUP_SIZE:(g + 1) * GROUP_SIZE, :]
        m1 = jnp.max(blk, axis=0, keepdims=True)
        first = jnp.min(jnp.where(blk == m1, sub_io, GROUP_SIZE), axis=0, keepdims=True)
        m2 = jnp.max(jnp.where(sub_io == first, neg, blk), axis=0, keepdims=True)
        pieces.append(jnp.broadcast_to(m1 + m2, (GROUP_SIZE, tm)))
    gscore = jnp.concatenate(pieces, axis=0)

    e_io = lax.broadcasted_iota(I32, (N_EXPERTS, tm), 0)
    grp_io = e_io // GROUP_SIZE
    gmask = jnp.zeros((N_EXPERTS, tm), jnp.bool_)
    for _ in range(TOPK_GROUPS):
        mx = jnp.max(gscore, axis=0, keepdims=True)
        gi = jnp.min(jnp.where(gscore == mx, grp_io, N_GROUPS), axis=0, keepdims=True)
        hit = grp_io == gi
        gmask = jnp.logical_or(gmask, hit)
        gscore = jnp.where(hit, neg, gscore)

    cur = jnp.where(gmask, sel, neg)
    row_io = lax.broadcasted_iota(I32, (TOP_K, tm), 0)
    eidx = jnp.zeros((TOP_K, tm), I32)
    wraw = jnp.zeros((TOP_K, tm), F32)
    chosen = jnp.zeros((N_EXPERTS, tm), jnp.bool_)
    hits = []
    for kk in range(TOP_K):
        mx = jnp.max(cur, axis=0, keepdims=True)
        ei = jnp.min(jnp.where(cur == mx, e_io, N_EXPERTS), axis=0, keepdims=True)
        hit = e_io == ei
        hits.append(hit)
        sc_k = jnp.sum(jnp.where(hit, scores, 0.0), axis=0, keepdims=True)
        eidx = jnp.where(row_io == kk, ei, eidx)
        wraw = jnp.where(row_io == kk, sc_k, wraw)
        chosen = jnp.logical_or(chosen, hit)
        cur = jnp.where(hit, neg, cur)

    wsum = jnp.sum(wraw, axis=0, keepdims=True)
    wts_ref[...] = wraw / wsum * ROUTED_SCALE
    eidx_ref[...] = eidx

    chosen_f = jnp.where(chosen, 1.0, 0.0)
    r_io = lax.broadcasted_iota(I32, (tm, tm), 0)
    c_io = lax.broadcasted_iota(I32, (tm, tm), 1)
    strict_upper = jnp.where(r_io < c_io, 1.0, 0.0).astype(BF16)
    prefix = jnp.dot(chosen_f.astype(BF16), strict_upper, preferred_element_type=F32)
    rank = prefix + carry_scr[:, 0:1]
    pos = jnp.zeros((TOP_K, tm), F32)
    for kk in range(TOP_K):
        p_k = jnp.sum(jnp.where(hits[kk], rank, 0.0), axis=0, keepdims=True)
        pos = jnp.where(row_io == kk, p_k, pos)
    pos_ref[...] = pos.astype(I32)
    total = carry_scr[...] + jnp.sum(chosen_f, axis=1, keepdims=True)
    carry_scr[...] = total
    cnt_ref[...] = total.astype(I32)


def _route(h, sh, sc, nw, rw_t, rb_col, seq):
    t, d = h.shape
    tm = min(ROUTE_ROWS, seq)
    per_b = seq // tm
    kspec = pl.BlockSpec((TOP_K, tm), lambda i: (0, i))
    return pl.pallas_call(
        _route_body,
        grid=(t // tm,),
        in_specs=[pl.BlockSpec((tm, d), lambda i: (i, 0)),
                  pl.BlockSpec((1, 1, d), lambda i: (i // per_b, 0, 0)),
                  pl.BlockSpec((1, 1, d), lambda i: (i // per_b, 0, 0)),
                  pl.BlockSpec((1, d), lambda i: (0, 0)),
                  pl.BlockSpec((N_EXPERTS, d), lambda i: (0, 0)),
                  pl.BlockSpec((N_EXPERTS, 1), lambda i: (0, 0))],
        out_specs=[pl.BlockSpec((tm, d // 2), lambda i: (i, 0)), kspec, kspec, kspec,
                   pl.BlockSpec((N_EXPERTS, 128), lambda i: (0, 0))],
        out_shape=[jax.ShapeDtypeStruct((t, d // 2), U32),
                   jax.ShapeDtypeStruct((TOP_K, t), I32),
                   jax.ShapeDtypeStruct((TOP_K, t), F32),
                   jax.ShapeDtypeStruct((TOP_K, t), I32),
                   jax.ShapeDtypeStruct((N_EXPERTS, 128), I32)],
        scratch_shapes=[pltpu.VMEM((N_EXPERTS, 128), F32)],
        compiler_params=_cparams(("arbitrary",)),
        name="moe_route",
    )(h, sh, sc, nw, rw_t, rb_col)


def _sc_worker_id():
    return lax.axis_index("s") * SC_CORES + lax.axis_index("c")


def _sc_scatter_rows(x, dest, n_rows):
    t, d = x.shape
    n_k = dest.shape[0]
    assert t % (SC_WORKERS * 2 * SC_CHUNK) == 0
    per_w = t // SC_WORKERS
    n_ch = per_w // SC_CHUNK
    dest4 = dest.reshape(n_k, SC_WORKERS, n_ch, SC_CHUNK).transpose(1, 2, 0, 3)
    mesh = plsc.VectorSubcoreMesh(core_axis_name="c", subcore_axis_name="s")

    @functools.partial(
        pl.kernel, mesh=mesh,
        out_type=jax.ShapeDtypeStruct((n_rows, d), x.dtype),
        scratch_types=[pltpu.VMEM((n_ch, n_k, SC_CHUNK), I32),
                       pltpu.VMEM((SC_CHUNK, d), x.dtype),
                       pltpu.VMEM((SC_CHUNK, d), x.dtype)] + [pltpu.SemaphoreType.DMA] * 4,
        name="moe_dispatch_sc",
    )
    def scatter_kernel(x_hbm, dest_hbm, out_hbm, idx_v, rows0, rows1, l0, l1, s0, s1):
        wid = _sc_worker_id()
        base = wid * per_w
        pltpu.sync_copy(dest_hbm.at[wid], idx_v)
        bufs = ((rows0, l0, s0), (rows1, l1, s1))

        def load(ci, b):
            rows, load_sem, _ = bufs[b]
            return pltpu.make_async_copy(x_hbm.at[pl.ds(base + ci * SC_CHUNK, SC_CHUNK)], rows,
                                         load_sem)

        def scatters(ci, b):
            rows, _, scatter_sem = bufs[b]
            return [pltpu.make_async_copy(rows, out_hbm.at[idx_v.at[ci, j]], scatter_sem)
                    for j in range(n_k)]

        load(0, 0).start()

        @pl.loop(0, n_ch, step=2)
        def _(ci):
            load(ci, 0).wait()
            for cp in scatters(ci, 0):
                cp.start()

            @pl.when(ci > 0)
            def _():
                for cp in scatters(ci - 1, 1):
                    cp.wait()

            load(ci + 1, 1).start()
            load(ci + 1, 1).wait()
            for cp in scatters(ci + 1, 1):
                cp.start()
            for cp in scatters(ci, 0):
                cp.wait()

            @pl.when(ci + 2 < n_ch)
            def _():
                load(ci + 2, 0).start()

        for cp in scatters(n_ch - 1, 1):
            cp.wait()

    return scatter_kernel(x, dest4)


def _sc_gather_rows(table, idx):
    n, d = idx.shape[0], table.shape[1]
    assert n % (SC_WORKERS * 2 * SC_CHUNK) == 0
    per_w = n // SC_WORKERS
    n_ch = per_w // SC_CHUNK
    idx3 = idx.reshape(SC_WORKERS, n_ch, SC_CHUNK)
    mesh = plsc.VectorSubcoreMesh(core_axis_name="c", subcore_axis_name="s")

    @functools.partial(
        pl.kernel, mesh=mesh,
        out_type=jax.ShapeDtypeStruct((n, d), table.dtype),
        scratch_types=[pltpu.VMEM((n_ch, SC_CHUNK), I32),
                       pltpu.VMEM((SC_CHUNK, d), table.dtype),
                       pltpu.VMEM((SC_CHUNK, d), table.dtype)] + [pltpu.SemaphoreType.DMA] * 4,
        name="moe_combine_sc",
    )
    def gather_kernel(table_hbm, idx_hbm, out_hbm, idx_v, rows0, rows1, g0, g1, w0, w1):
        wid = _sc_worker_id()
        base = wid * per_w
        pltpu.sync_copy(idx_hbm.at[wid], idx_v)
        bufs = ((rows0, g0, w0), (rows1, g1, w1))

        def gather(ci, b):
            rows, gather_sem, _ = bufs[b]
            return pltpu.make_async_copy(table_hbm.at[idx_v.at[ci]], rows, gather_sem)

        def write(ci, b):
            rows, _, write_sem = bufs[b]
            return pltpu.make_async_copy(rows, out_hbm.at[pl.ds(base + ci * SC_CHUNK, SC_CHUNK)],
                                         write_sem)

        gather(0, 0).start()

        @pl.loop(0, n_ch, step=2)
        def _(ci):
            gather(ci, 0).wait()
            write(ci, 0).start()

            @pl.when(ci > 0)
            def _():
                write(ci - 1, 1).wait()

            gather(ci + 1, 1).start()
            gather(ci + 1, 1).wait()
            write(ci + 1, 1).start()
            write(ci, 0).wait()

            @pl.when(ci + 2 < n_ch)
            def _():
                gather(ci + 2, 0).start()

        write(n_ch - 1, 1).wait()

    return gather_kernel(table, idx3)


def _expert_body(blk_e_ref, blk_first_ref, blk_valid_ref, x_ref, w1_ref, w3_ref, w2_ref, y_ref,
                 w1_scr, w3_scr, w2_scr):
    del blk_e_ref
    j = pl.program_id(0)
    valid = blk_valid_ref[j]

    @pl.when(blk_first_ref[j] == 1)
    def _():
        w1_scr[...] = w1_ref[0].astype(BF16)
        w3_scr[...] = w3_ref[0].astype(BF16)
        w2_scr[...] = w2_ref[0].astype(BF16)

    @pl.when(valid > 0)
    def _():
        half = x_ref.shape[1]
        rows = lax.broadcasted_iota(I32, x_ref.shape, 0)
        lo, hi = _unpack_bf16_pair(jnp.where(rows < valid, x_ref[...], 0))
        lo, hi = lo.astype(BF16), hi.astype(BF16)

        def proj(w_scr):
            return (jnp.dot(lo, w_scr[0:half, :], preferred_element_type=F32)
                    + jnp.dot(hi, w_scr[half:2 * half, :], preferred_element_type=F32))

        hid = _silu(proj(w1_scr)) * proj(w3_scr)
        y = jnp.dot(hid.astype(BF16), w2_scr[...], preferred_element_type=F32)
        y_ref[...] = _pack_bf16_pair(y[:, :half], y[:, half:])

    @pl.when(valid == 0)
    def _():
        y_ref[...] = jnp.zeros(y_ref.shape, U32)


def _experts(xg, blk_e, blk_first, blk_valid, w1, w3, w2, layer):
    n_rows, half = xg.shape
    n_blocks = n_rows // EXPERT_ROWS
    d, de = w1.shape[-2:]
    grid_spec = pltpu.PrefetchScalarGridSpec(
        num_scalar_prefetch=3,
        grid=(n_blocks,),
        in_specs=[pl.BlockSpec((EXPERT_ROWS, half), lambda j, be, bf, bv: (j, 0)),
                  pl.BlockSpec((None, 1, d, de), lambda j, be, bf, bv: (layer, be[j], 0, 0)),
                  pl.BlockSpec((None, 1, d, de), lambda j, be, bf, bv: (layer, be[j], 0, 0)),
                  pl.BlockSpec((None, 1, de, d), lambda j, be, bf, bv: (layer, be[j], 0, 0))],
        out_specs=pl.BlockSpec((EXPERT_ROWS, half), lambda j, be, bf, bv: (j, 0)),
        scratch_shapes=[pltpu.VMEM((d, de), BF16), pltpu.VMEM((d, de), BF16),
                        pltpu.VMEM((de, d), BF16)],
    )
    return pl.pallas_call(
        _expert_body,
        grid_spec=grid_spec,
        out_shape=jax.ShapeDtypeStruct((n_rows, half), U32),
        compiler_params=_cparams(("arbitrary",)),
        name="moe_experts",
    )(blk_e, blk_first, blk_valid, xg, w1, w3, w2)


def _combine_body(h_ref, u_ref, yg_ref, wts_ref, gt_ref, s1_ref, s3_ref, s2_ref, fw_ref, o_ref, *,
                  final_norm):
    half = u_ref.shape[1]
    lo, hi = _unpack_bf16_pair(u_ref[...])
    lo, hi = lo.astype(BF16), hi.astype(BF16)

    def proj(w_ref):
        return (jnp.dot(lo, w_ref[0:half, :], preferred_element_type=F32)
                + jnp.dot(hi, w_ref[half:2 * half, :], preferred_element_type=F32))

    hid = _silu(proj(s1_ref)) * proj(s3_ref)
    shared = jnp.dot(hid.astype(BF16), s2_ref[...], preferred_element_type=F32)

    routed_lo = routed_hi = None
    for kk in range(TOP_K):
        y_lo, y_hi = _unpack_bf16_pair(yg_ref[kk])
        w = wts_ref[:, kk:kk + 1]
        routed_lo = y_lo * w if routed_lo is None else routed_lo + y_lo * w
        routed_hi = y_hi * w if routed_hi is None else routed_hi + y_hi * w
    gt = gt_ref[0]
    out_lo = h_ref[:, 0:half] + gt[:, 0:half] * (routed_lo + shared[:, 0:half])
    out_hi = (h_ref[:, half:2 * half]
              + gt[:, half:2 * half] * (routed_hi + shared[:, half:2 * half]))
    if final_norm:
        ssq = (jnp.sum(out_lo * out_lo, axis=-1, keepdims=True)
               + jnp.sum(out_hi * out_hi, axis=-1, keepdims=True))
        inv = lax.rsqrt(ssq / (2 * half) + EPS)
        out_lo = (out_lo * inv) * fw_ref[:, 0:half]
        out_hi = (out_hi * inv) * fw_ref[:, half:2 * half]
    o_ref[:, 0:half] = out_lo
    o_ref[:, half:2 * half] = out_hi


def _combine(h, u, yg, wts_tk, gt, s1, s3, s2, final_w, seq, final_norm, part, n_parts):
    t, d = h.shape
    half = d // 2
    tm = min(COMBINE_ROWS, seq)
    per_b = seq // tm
    ds_ = s1.shape[-1]
    steps = t // n_parts // tm
    off = part * steps
    return pl.pallas_call(
        functools.partial(_combine_body, final_norm=final_norm),
        grid=(steps,),
        in_specs=[pl.BlockSpec((tm, d), lambda i: (i + off, 0)),
                  pl.BlockSpec((tm, half), lambda i: (i + off, 0)),
                  pl.BlockSpec((TOP_K, tm, half), lambda i: (0, i, 0)),
                  pl.BlockSpec((tm, TOP_K), lambda i: (i + off, 0)),
                  pl.BlockSpec((1, 1, d), lambda i: ((i + off) // per_b, 0, 0)),
                  pl.BlockSpec((d, ds_), lambda i: (0, 0)),
                  pl.BlockSpec((d, ds_), lambda i: (0, 0)),
                  pl.BlockSpec((ds_, d), lambda i: (0, 0)),
                  pl.BlockSpec((1, d), lambda i: (0, 0))],
        out_specs=pl.BlockSpec((tm, d), lambda i: (i + off, 0)),
        out_shape=jax.ShapeDtypeStruct((t, d), F32),
        input_output_aliases={0: 0},
        compiler_params=_cparams(("parallel",)),
        name="moe_combine",
    )(h, u, yg, wts_tk, gt, s1, s3, s2, final_w)


def _split_w_in(w_in):
    sizes = (2 * M_HEADS * M_DQK, M_HEADS * M_DV, M_HEADS * M_DV, M_HEADS, M_HEADS,
             G_HEADS * G_DK, G_HEADS * G_DK, G_HEADS * G_DV, G_RANK, G_HEADS * G_DV,
             D_MODEL, D_MODEL)
    offs = [0]
    for n in sizes:
        offs.append(offs[-1] + n)
    w16 = w_in.astype(BF16)
    big = jnp.concatenate([w16[:, offs[0]:offs[3]], w16[:, offs[5]:offs[8]], w16[:, offs[9]:offs[12]]],
                          axis=1)
    pad = jnp.zeros((w_in.shape[0], SMALL_COLS - 2 * M_HEADS - G_RANK), BF16)
    small = jnp.concatenate([w16[:, offs[3]:offs[5]], w16[:, offs[8]:offs[9]], pad], axis=1)
    return big, small


def _moe_layout(counts, eidx, pos, n_blocks):
    padded = (counts + EXPERT_ROWS - 1) // EXPERT_ROWS * EXPERT_ROWS
    pend = jnp.cumsum(padded)
    pstart = pend - padded
    experts = jnp.arange(N_EXPERTS, dtype=I32)
    dest = pos + jnp.sum(jnp.where(eidx[..., None] == experts, pstart, 0), axis=-1)
    blk_start = jnp.arange(n_blocks, dtype=I32) * EXPERT_ROWS
    owner = jnp.sum((pend[None, :] <= blk_start[:, None]).astype(I32), axis=1)
    blk_e = jnp.minimum(owner, N_EXPERTS - 1)
    prev = jnp.concatenate([jnp.full((1,), -1, I32), blk_e[:-1]])
    blk_first = (blk_e != prev).astype(I32)
    own = blk_e[:, None] == experts[None, :]
    rows_left = jnp.sum(jnp.where(own, (pstart + counts)[None, :], 0), axis=1) - blk_start
    blk_valid = jnp.clip(jnp.where(owner < N_EXPERTS, rows_left, 0), 0, EXPERT_ROWS)
    return dest.astype(I32), blk_e.astype(I32), blk_first, blk_valid.astype(I32)


def kernel(x, c, ada_w, ada_b, norm_mix_w, norm_moe_w, w_in, m_conv_w, m_gate_b, m_norm_w,
           g_alpha_w, g_alpha_b, g_norm_w, w_pa, w_pb, w_o, router_w, router_b,
           exp_w1, exp_w3, exp_w2, sh_w1, sh_w3, sh_w2, final_norm_w):
    bsz, seq, d = x.shape
    depth = ada_w.shape[0]
    t = bsz * seq
    n_rows = t * TOP_K + N_EXPERTS * EXPERT_ROWS
    n_blocks = n_rows // EXPERT_ROWS

    ada = _ada(c, ada_w, ada_b).reshape(depth, bsz, 6, 1, d)
    h = x.reshape(t, d)
    for l in range(depth):
        sh1, sc1, gt1, sh2, sc2, gt2 = (ada[l, :, i] for i in range(6))

        w_big, w_small = _split_w_in(w_in[l])
        mix_w = norm_mix_w[l][None, :]
        gate_row = jnp.zeros((1, SMALL_COLS), F32)
        gate_row = gate_row.at[0, SMALL_I:SMALL_I + M_HEADS].set(m_gate_b[l, 0])
        gate_row = gate_row.at[0, SMALL_F:SMALL_F + M_HEADS].set(m_gate_b[l, 1])
        ha = _mlstm(h, sh1, sc1, mix_w, w_big, w_small, m_conv_w[l], gate_row, m_norm_w[l][None, :],
                    bsz, seq)
        alpha_full = jnp.zeros((SMALL_COLS, G_HEADS * G_DK), F32)
        alpha_full = alpha_full.at[SMALL_R:SMALL_R + G_RANK].set(g_alpha_w[l]).astype(BF16)
        hb = _gla(h, sh1, sc1, mix_w, w_big, w_small, alpha_full, g_alpha_b[l][None, :],
                  g_norm_w[l][None, :], bsz, seq)
        h = _mixout(h, sh1, sc1, mix_w, w_big, ha, hb, gt1, w_pa[l].astype(BF16),
                    w_pb[l].astype(BF16), w_o[l].astype(BF16), seq)

        u, eidx, wts, pos, cnt = _route(h, sh2, sc2, norm_moe_w[l][None, :],
                                        router_w[l].T, router_b[l][:, None], seq)
        dest, blk_e, blk_first, blk_valid = _moe_layout(cnt[:, 0], eidx, pos, n_blocks)
        xg = _sc_scatter_rows(u, dest, n_rows)
        y = _experts(xg, blk_e, blk_first, blk_valid, exp_w1, exp_w3, exp_w2, l)
        tp = t // COMBINE_PARTS
        for p in range(COMBINE_PARTS):
            dest_p = dest[:, p * tp:(p + 1) * tp].reshape(-1)
            yg = _sc_gather_rows(y, dest_p).reshape(TOP_K, tp, d // 2)
            h = _combine(h, u, yg, wts.T, gt2, sh_w1[l].astype(BF16), sh_w3[l].astype(BF16),
                         sh_w2[l].astype(BF16), final_norm_w[None, :], seq,
                         final_norm=(l == depth - 1), part=p, n_parts=COMBINE_PARTS)

    return h.reshape(bsz, seq, d)
```

```python
import functools

import jax
import jax.numpy as jnp
import numpy as np
from jax import lax
from jax.experimental import pallas as pl
from jax.experimental.pallas import tpu as pltpu
from jax.experimental.pallas import tpu_sc as plsc

F32 = jnp.float32
BF16 = jnp.bfloat16
I32 = jnp.int32
U32 = jnp.uint32
HI_MASK = np.uint32(0xFFFF0000)

SC_CORES = 2
SC_SUBCORES = 16
SC_WORKERS = SC_CORES * SC_SUBCORES
SC_CHUNK = 64

D_MODEL = 1024
M_HEADS = 4
M_DQK = 128
M_DV = 256
M_CONV = 4
GATE_CAP = 15.0
G_HEADS = 4
G_DK = 128
G_DV = 256
G_RANK = 16
G_TAU = 16.0
G_CHUNK = 64
N_EXPERTS = 64
TOP_K = 8
N_GROUPS = 8
GROUP_SIZE = N_EXPERTS // N_GROUPS
TOPK_GROUPS = 4
ROUTED_SCALE = 2.5
EPS = 1e-6

M_CHUNK_ROWS = 256
G_BLOCK_ROWS = 256
SEQ_GROUP = 2
PROJ_PIECE_COLS = 256
MIX_ROWS = 512
ROUTE_ROWS = 512
EXPERT_ROWS = 1024
COMBINE_ROWS = 512
COMBINE_PARTS = 2
CONV_HALO = 8
VMEM_LIMIT = 48 * 1024 * 1024

M_GROUP_COLS = 2 * M_HEADS * M_DQK + 2 * M_HEADS * M_DV
G_GROUP_COLS = 2 * G_HEADS * G_DK + 2 * G_HEADS * G_DV
GATE_GROUP_COLS = 2 * D_MODEL
COL_M_GROUP = 0
COL_G_GROUP = COL_M_GROUP + M_GROUP_COLS
COL_GATE_GROUP = COL_G_GROUP + G_GROUP_COLS
SMALL_COLS = 128
SMALL_I, SMALL_F, SMALL_R = 0, M_HEADS, 2 * M_HEADS


def _cparams(sem, vmem=VMEM_LIMIT):
    return pltpu.CompilerParams(dimension_semantics=sem, vmem_limit_bytes=vmem)


def _silu(x):
    return x * jax.nn.sigmoid(x)


def _log_sigmoid(x):
    return jnp.minimum(x, 0.0) - jnp.log1p(jnp.exp(-jnp.abs(x)))


def _modulated_rmsnorm(x, w, sc, sh):
    y = x * lax.rsqrt(jnp.mean(x * x, axis=-1, keepdims=True) + EPS)
    return (y * w) * (1.0 + sc) + sh


def _pack_bf16_pair(lo, hi):
    lo_bits = lax.bitcast_convert_type(lo.astype(BF16).astype(F32), U32)
    hi_bits = lax.bitcast_convert_type(hi.astype(BF16).astype(F32), U32)
    return (lo_bits >> 16) | (hi_bits & HI_MASK)


def _unpack_bf16_pair(packed):
    lo = lax.bitcast_convert_type(packed << 16, F32)
    hi = lax.bitcast_convert_type(packed & HI_MASK, F32)
    return lo, hi


def _lower_tri(n, dtype):
    r = lax.broadcasted_iota(I32, (n, n), 0)
    c = lax.broadcasted_iota(I32, (n, n), 1)
    return (r >= c).astype(dtype)


def _split3_dot(lhs01, x):
    hi = x.astype(BF16)
    r1 = x - hi.astype(F32)
    mid = r1.astype(BF16)
    lo = (r1 - mid.astype(F32)).astype(BF16)
    return (jnp.dot(lhs01, hi, preferred_element_type=F32)
            + jnp.dot(lhs01, mid, preferred_element_type=F32)
            + jnp.dot(lhs01, lo, preferred_element_type=F32))


def _split2_dot_nt(a, b):
    nt = (((1,), (1,)), ((), ()))
    a_hi = a.astype(BF16)
    a_lo = (a - a_hi.astype(F32)).astype(BF16)
    b_hi = b.astype(BF16)
    b_lo = (b - b_hi.astype(F32)).astype(BF16)
    return (lax.dot_general(a_hi, b_hi, nt, preferred_element_type=F32)
            + lax.dot_general(a_hi, b_lo, nt, preferred_element_type=F32)
            + lax.dot_general(a_lo, b_hi, nt, preferred_element_type=F32))


def _ada_body(c_ref, w_ref, b_ref, o_ref):
    cond = _silu(c_ref[...])
    o_ref[0] = jnp.dot(cond.astype(BF16), w_ref[0].astype(BF16),
                       preferred_element_type=F32) + b_ref[0]


def _ada(c, ada_w, ada_b):
    depth, d, six_d = ada_w.shape
    bsz = c.shape[0]
    nj = six_d // d
    return pl.pallas_call(
        _ada_body,
        grid=(depth, nj),
        in_specs=[pl.BlockSpec((bsz, d), lambda l, j: (0, 0)),
                  pl.BlockSpec((1, d, d), lambda l, j: (l, 0, j)),
                  pl.BlockSpec((1, 1, d), lambda l, j: (l, 0, j))],
        out_specs=pl.BlockSpec((1, bsz, d), lambda l, j: (l, 0, j)),
        out_shape=jax.ShapeDtypeStruct((depth, bsz, six_d), F32),
        compiler_params=_cparams(("parallel", "parallel")),
        name="ada_ln",
    )(c, ada_w, ada_b.reshape(depth, 1, six_d))


def _mlstm_body(h0_ref, hn_ref, sh_ref, sc_ref, mw_ref, w_ref, ws_ref, cw_ref, gb_ref, nw_ref,
                ha_ref, qk_nxt, v_nxt, o_nxt, sm_nxt, xe_scr, v_scr, o_scr, sm_scr,
                c_scr, n_scr, m_scr):
    n_seq, rows = hn_ref.shape[0], hn_ref.shape[1]
    half = M_HEADS * M_DQK
    vcols = M_HEADS * M_DV

    def projection_pieces(h_ref):
        u = jnp.concatenate(
            [_modulated_rmsnorm(h_ref[g], mw_ref[...], sc_ref[g], sh_ref[g]) for g in range(n_seq)],
            axis=0).astype(BF16)

        def piece(dst, w_lo, lo, width):
            def run():
                dst[:, lo:lo + width] = jnp.dot(u, w_ref[:, w_lo + lo:w_lo + lo + width],
                                                preferred_element_type=F32).astype(dst.dtype)
            return run

        def small_piece():
            sm_nxt[...] = jnp.dot(u, ws_ref[...], preferred_element_type=F32)

        pieces = [small_piece]
        for dst, w_lo in ((qk_nxt, 0), (v_nxt, 2 * half), (o_nxt, 2 * half + vcols)):
            for lo in range(0, vcols, PROJ_PIECE_COLS):
                pieces.append(piece(dst, w_lo, lo, PROJ_PIECE_COLS))
        return pieces

    @pl.when(pl.program_id(1) == 0)
    def _():
        xe_scr[:, 0:CONV_HALO, :] = jnp.zeros((n_seq, CONV_HALO, 2 * half), F32)
        c_scr[...] = jnp.zeros(c_scr.shape, F32)
        n_scr[...] = jnp.zeros(n_scr.shape, F32)
        m_scr[...] = jnp.zeros(m_scr.shape, F32)
        for run in projection_pieces(h0_ref):
            run()

    for g in range(n_seq):
        xe_scr[g, CONV_HALO:CONV_HALO + rows, :] = qk_nxt[g * rows:(g + 1) * rows, :]
    v_scr[...] = v_nxt[...]
    o_scr[...] = o_nxt[...]
    sm_scr[...] = sm_nxt[...]
    pending = projection_pieces(hn_ref)

    r_io = lax.broadcasted_iota(I32, (rows, rows), 0)
    c_io = lax.broadcasted_iota(I32, (rows, rows), 1)
    causal = r_io >= c_io
    tri = _lower_tri(rows, BF16)

    for g in range(n_seq):
        gr = slice(g * rows, (g + 1) * rows)
        conv = None
        for j in range(M_CONV):
            off = CONV_HALO - (M_CONV - 1) + j
            term = xe_scr[g, off:off + rows, :] * cw_ref[j:j + 1, :]
            conv = term if conv is None else conv + term
        qk = _silu(conv)
        xe_scr[g, 0:CONV_HALO, :] = xe_scr[g, rows:rows + CONV_HALO, :]

        capped = GATE_CAP * jnp.tanh((sm_scr[gr, :] + gb_ref[...]) / GATE_CAP)
        li_all = capped
        lf_all = _log_sigmoid(capped)
        b_all = _split3_dot(tri, lf_all)
        li_t = li_all.T
        b_t = b_all.T

        for h in range(M_HEADS):
            if pending:
                pending.pop(0)()
            sidx = g * M_HEADS + h
            q = (qk[:, h * M_DQK:(h + 1) * M_DQK] * (M_DQK ** -0.5)).astype(BF16)
            k = qk[:, half + h * M_DQK:half + (h + 1) * M_DQK]
            kb = k.astype(BF16)
            v = v_scr[gr, h * M_DV:(h + 1) * M_DV]
            li_c = li_all[:, SMALL_I + h:SMALL_I + h + 1]
            b_c = b_all[:, SMALL_F + h:SMALL_F + h + 1]
            li_r = li_t[SMALL_I + h:SMALL_I + h + 1, :]
            b_r = b_t[SMALL_F + h:SMALL_F + h + 1, :]
            gsum = b_c[rows - 1:rows, :]
            m_prev = m_scr[sidx][:, 0:1]
            c_prev = c_scr[sidx]
            n_prev = n_scr[sidx]

            d_mat = jnp.where(causal, b_c - b_r + li_r, -jnp.inf)
            m_inter = b_c + m_prev
            m_t = jnp.maximum(jnp.max(d_mat, axis=1, keepdims=True), m_inter)
            s = lax.dot_general(q, kb, (((1,), (1,)), ((), ())), preferred_element_type=F32)
            p = jnp.exp(d_mat - m_t) * s
            w_inter = jnp.exp(m_inter - m_t)
            num = (jnp.dot(p.astype(BF16), v, preferred_element_type=F32)
                   + w_inter * jnp.dot(q, c_prev.astype(BF16), preferred_element_type=F32))
            qn = jnp.sum(q.astype(F32) * n_prev, axis=1, keepdims=True)
            den = jnp.sum(p, axis=1, keepdims=True) + w_inter * qn
            hh = num / jnp.maximum(jnp.abs(den), jnp.exp(-m_t))

            if pending:
                pending.pop(0)()

            a_r = gsum - b_r + li_r
            a_c = gsum - b_c + li_c
            m_new = jnp.maximum(gsum + m_prev, jnp.max(a_r, axis=1, keepdims=True))
            decay = jnp.exp(gsum + m_prev - m_new)
            wk = jnp.exp(a_c - m_new) * k
            c_scr[sidx] = decay * c_prev + lax.dot_general(
                wk.astype(BF16), v, (((0,), (0,)), ((), ())), preferred_element_type=F32)
            n_scr[sidx] = decay * n_prev + jnp.sum(wk, axis=0, keepdims=True)
            m_scr[sidx] = jnp.broadcast_to(m_new, m_scr.shape[1:])

            y = hh * lax.rsqrt(jnp.mean(hh * hh, axis=-1, keepdims=True) + EPS)
            y = y * nw_ref[:, h * M_DV:(h + 1) * M_DV]
            gate = jax.nn.sigmoid(o_scr[gr, h * M_DV:(h + 1) * M_DV])
            ha_ref[g, :, h * M_DV:(h + 1) * M_DV] = (y * gate).astype(BF16)
    for run in pending:
        run()


def _mlstm(h, sh, sc, mix_w, w_big, w_small, conv_w, gate_row, norm_w, bsz, seq):
    t, d = h.shape
    rows = min(M_CHUNK_ROWS, seq)
    nc = seq // rows
    ng = SEQ_GROUP if bsz % SEQ_GROUP == 0 else 1
    half = M_HEADS * M_DQK
    vcols = M_HEADS * M_DV
    wcols = M_GROUP_COLS
    assert COL_M_GROUP % wcols == 0
    resident = pl.Buffered(1)
    nxt = [pltpu.VMEM((ng * rows, 2 * half), F32), pltpu.VMEM((ng * rows, vcols), BF16),
           pltpu.VMEM((ng * rows, vcols), F32), pltpu.VMEM((ng * rows, SMALL_COLS), F32)]
    out = pl.pallas_call(
        _mlstm_body,
        grid=(bsz // ng, nc),
        in_specs=[pl.BlockSpec((ng, rows, d), lambda b, c: (b, 0, 0)),
                  pl.BlockSpec((ng, rows, d), lambda b, c: (b, jnp.minimum(c + 1, nc - 1), 0)),
                  pl.BlockSpec((ng, 1, d), lambda b, c: (b, 0, 0)),
                  pl.BlockSpec((ng, 1, d), lambda b, c: (b, 0, 0)),
                  pl.BlockSpec((1, d), lambda b, c: (0, 0)),
                  pl.BlockSpec((d, wcols), lambda b, c: (0, 0), pipeline_mode=resident),
                  pl.BlockSpec((d, SMALL_COLS), lambda b, c: (0, 0), pipeline_mode=resident),
                  pl.BlockSpec((M_CONV, 2 * half), lambda b, c: (0, 0)),
                  pl.BlockSpec((1, SMALL_COLS), lambda b, c: (0, 0)),
                  pl.BlockSpec((1, vcols), lambda b, c: (0, 0))],
        out_specs=pl.BlockSpec((ng, rows, vcols), lambda b, c: (b, c, 0)),
        out_shape=jax.ShapeDtypeStruct((bsz, seq, vcols), BF16),
        scratch_shapes=nxt + [pltpu.VMEM((ng, rows + CONV_HALO, 2 * half), F32)] + nxt[1:]
        + [pltpu.VMEM((ng * M_HEADS, M_DQK, M_DV), F32),
           pltpu.VMEM((ng * M_HEADS, 1, M_DQK), F32),
           pltpu.VMEM((ng * M_HEADS, 1, 128), F32)],
        compiler_params=_cparams(("parallel", "arbitrary")),
        name="mlstm",
    )(h.reshape(bsz, seq, d), h.reshape(bsz, seq, d), sh, sc, mix_w, w_big, w_small, conv_w,
      gate_row, norm_w)
    return out.reshape(t, vcols)


def _gla_body(h0_ref, hn_ref, sh_ref, sc_ref, mw_ref, w_ref, ws_ref, aw_ref, ab_ref, nw_ref, hb_ref,
              qk_nxt, v_nxt, z_nxt, sm_nxt, qk_scr, v_scr, z_scr, sm_scr, st_scr):
    n_seq, rows = hn_ref.shape[0], hn_ref.shape[1]
    n_chunks = rows // G_CHUNK
    kcols = G_HEADS * G_DK
    vcols = G_HEADS * G_DV

    def projection_pieces(h_ref):
        u = jnp.concatenate(
            [_modulated_rmsnorm(h_ref[g], mw_ref[...], sc_ref[g], sh_ref[g]) for g in range(n_seq)],
            axis=0).astype(BF16)

        def piece(dst, w_lo, lo, width):
            def run():
                dst[:, lo:lo + width] = jnp.dot(u, w_ref[:, w_lo + lo:w_lo + lo + width],
                                                preferred_element_type=F32).astype(dst.dtype)
            return run

        def small_piece():
            sm_nxt[...] = jnp.dot(u, ws_ref[...], preferred_element_type=F32)

        pieces = [small_piece]
        for dst, w_lo in ((qk_nxt, 0), (v_nxt, 2 * kcols), (z_nxt, 2 * kcols + vcols)):
            for lo in range(0, vcols, PROJ_PIECE_COLS):
                pieces.append(piece(dst, w_lo, lo, PROJ_PIECE_COLS))
        return pieces

    @pl.when(pl.program_id(1) == 0)
    def _():
        st_scr[...] = jnp.zeros(st_scr.shape, F32)
        for run in projection_pieces(h0_ref):
            run()

    qk_scr[...] = qk_nxt[...]
    v_scr[...] = v_nxt[...]
    z_scr[...] = z_nxt[...]
    sm_scr[...] = sm_nxt[...]
    pending = projection_pieces(hn_ref)

    r_io = lax.broadcasted_iota(I32, (rows, rows), 0)
    c_io = lax.broadcasted_iota(I32, (rows, rows), 1)
    chunk_causal = jnp.logical_and(r_io >= c_io, r_io // G_CHUNK == c_io // G_CHUNK)
    chunk_tri = jnp.where(chunk_causal, 1.0, 0.0).astype(BF16)

    for g in range(n_seq):
        gr = slice(g * rows, (g + 1) * rows)
        for _ in range((len(pending) + n_seq - 1 - g) // (n_seq - g)):
            pending.pop(0)()
        q = qk_scr[gr, 0:kcols] * (G_DK ** -0.5)
        k = qk_scr[gr, kcols:2 * kcols]

        logits = jnp.dot(sm_scr[gr, :].astype(BF16), aw_ref[...], preferred_element_type=F32)
        la = _log_sigmoid(logits + ab_ref[...]) / G_TAU
        bc = _split3_dot(chunk_tri, la)
        gcs = [bc[(ci + 1) * G_CHUNK - 1:(ci + 1) * G_CHUNK, :] for ci in range(n_chunks)]
        gc_rows = jnp.concatenate([jnp.broadcast_to(gc, (G_CHUNK, kcols)) for gc in gcs], axis=0)

        q_in = (q * jnp.exp(bc)).astype(BF16)
        k_in = (k * jnp.exp(-bc)).astype(BF16)
        k_out = (k * jnp.exp(gc_rows - bc)).astype(BF16)

        for h in range(G_HEADS):
            ks = slice(h * G_DK, (h + 1) * G_DK)
            vs = slice(h * G_DV, (h + 1) * G_DV)
            v = v_scr[gr, vs]
            att = lax.dot_general(q_in[:, ks], k_in[:, ks], (((1,), (1,)), ((), ())),
                                  preferred_element_type=F32)
            att = jnp.where(chunk_causal, att, 0.0).astype(BF16)
            o_intra = jnp.dot(att, v, preferred_element_type=F32)
            st = st_scr[g * G_HEADS + h]
            outs = []
            for ci in range(n_chunks):
                rs = slice(ci * G_CHUNK, (ci + 1) * G_CHUNK)
                o_inter = lax.dot_general(q_in[rs, ks], st.astype(BF16),
                                          (((1,), (1,)), ((), ())), preferred_element_type=F32)
                outs.append(o_intra[rs, :] + o_inter)
                st = jnp.exp(gcs[ci][:, ks]) * st + lax.dot_general(
                    v[rs, :], k_out[rs, ks], (((0,), (0,)), ((), ())), preferred_element_type=F32)
            st_scr[g * G_HEADS + h] = st
            o = jnp.concatenate(outs, axis=0)
            y = o * lax.rsqrt(jnp.mean(o * o, axis=-1, keepdims=True) + EPS)
            y = y * nw_ref[:, vs]
            hb_ref[g, :, vs] = (y * _silu(z_scr[gr, vs])).astype(BF16)


def _gla(h, sh, sc, mix_w, w_big, w_small, alpha_full, alpha_b, norm_w, bsz, seq):
    t, d = h.shape
    rows = min(G_BLOCK_ROWS, seq)
    nb = seq // rows
    ng = SEQ_GROUP if bsz % SEQ_GROUP == 0 else 1
    kcols = G_HEADS * G_DK
    vcols = G_HEADS * G_DV
    wcols = G_GROUP_COLS
    assert COL_G_GROUP % wcols == 0
    resident = pl.Buffered(1)
    nxt = [pltpu.VMEM((ng * rows, 2 * kcols), F32), pltpu.VMEM((ng * rows, vcols), BF16),
           pltpu.VMEM((ng * rows, vcols), F32), pltpu.VMEM((ng * rows, SMALL_COLS), F32)]
    out = pl.pallas_call(
        _gla_body,
        grid=(bsz // ng, nb),
        in_specs=[pl.BlockSpec((ng, rows, d), lambda b, c: (b, 0, 0)),
                  pl.BlockSpec((ng, rows, d), lambda b, c: (b, jnp.minimum(c + 1, nb - 1), 0)),
                  pl.BlockSpec((ng, 1, d), lambda b, c: (b, 0, 0)),
                  pl.BlockSpec((ng, 1, d), lambda b, c: (b, 0, 0)),
                  pl.BlockSpec((1, d), lambda b, c: (0, 0)),
                  pl.BlockSpec((d, wcols), lambda b, c: (0, COL_G_GROUP // wcols),
                               pipeline_mode=resident),
                  pl.BlockSpec((d, SMALL_COLS), lambda b, c: (0, 0), pipeline_mode=resident),
                  pl.BlockSpec((SMALL_COLS, kcols), lambda b, c: (0, 0)),
                  pl.BlockSpec((1, kcols), lambda b, c: (0, 0)),
                  pl.BlockSpec((1, vcols), lambda b, c: (0, 0))],
        out_specs=pl.BlockSpec((ng, rows, vcols), lambda b, c: (b, c, 0)),
        out_shape=jax.ShapeDtypeStruct((bsz, seq, vcols), BF16),
        scratch_shapes=nxt + nxt + [pltpu.VMEM((ng * G_HEADS, G_DV, G_DK), F32)],
        compiler_params=_cparams(("parallel", "arbitrary")),
        name="gla",
    )(h.reshape(bsz, seq, d), h.reshape(bsz, seq, d), sh, sc, mix_w, w_big, w_small, alpha_full,
      alpha_b, norm_w)
    return out.reshape(t, vcols)


def _mixout_body(h_ref, sh_ref, sc_ref, mw_ref, wg_ref, ha_ref, hb_ref, gt_ref, wpa_ref, wpb_ref,
                 wo_ref, o_ref):
    d = h_ref.shape[1]
    h = h_ref[...]
    u = _modulated_rmsnorm(h, mw_ref[...], sc_ref[0], sh_ref[0]).astype(BF16)
    ga = jnp.dot(u, wg_ref[:, 0:d], preferred_element_type=F32)
    gb = jnp.dot(u, wg_ref[:, d:2 * d], preferred_element_type=F32)
    a = jnp.dot(ha_ref[...], wpa_ref[...], preferred_element_type=F32)
    b = jnp.dot(hb_ref[...], wpb_ref[...], preferred_element_type=F32)
    y = jax.nn.sigmoid(ga) * a + jax.nn.sigmoid(gb) * b
    o_ref[...] = h + gt_ref[0] * jnp.dot(y.astype(BF16), wo_ref[...], preferred_element_type=F32)


def _mixout(h, sh, sc, mix_w, w_big, ha, hb, gt, w_pa, w_pb, w_o, seq):
    t, d = h.shape
    tm = min(MIX_ROWS, seq)
    per_b = seq // tm
    assert GATE_GROUP_COLS == 2 * d and COL_GATE_GROUP % GATE_GROUP_COLS == 0
    resident = pl.Buffered(1)
    wspec = pl.BlockSpec((d, d), lambda i: (0, 0), pipeline_mode=resident)
    bspec = pl.BlockSpec((1, 1, d), lambda i: (i // per_b, 0, 0))
    return pl.pallas_call(
        _mixout_body,
        grid=(t // tm,),
        in_specs=[pl.BlockSpec((tm, d), lambda i: (i, 0)),
                  bspec, bspec,
                  pl.BlockSpec((1, d), lambda i: (0, 0)),
                  pl.BlockSpec((d, 2 * d), lambda i: (0, COL_GATE_GROUP // GATE_GROUP_COLS),
                               pipeline_mode=resident),
                  pl.BlockSpec((tm, d), lambda i: (i, 0)),
                  pl.BlockSpec((tm, d), lambda i: (i, 0)),
                  bspec,
                  wspec, wspec, wspec],
        out_specs=pl.BlockSpec((tm, d), lambda i: (i, 0)),
        out_shape=jax.ShapeDtypeStruct((t, d), F32),
        compiler_params=_cparams(("parallel",)),
        name="mix_out",
    )(h, sh, sc, mix_w, w_big, ha, hb, gt, w_pa, w_pb, w_o)


def _route_body(h_ref, sh_ref, sc_ref, nw_ref, rwt_ref, rb_ref,
                u_ref, eidx_ref, wts_ref, pos_ref, cnt_ref, carry_scr):
    tm = h_ref.shape[0]

    @pl.when(pl.program_id(0) == 0)
    def _():
        carry_scr[...] = jnp.zeros(carry_scr.shape, F32)

    u = _modulated_rmsnorm(h_ref[...], nw_ref[...], sc_ref[0], sh_ref[0])
    half = u.shape[1] // 2
    u_ref[...] = _pack_bf16_pair(u[:, :half], u[:, half:])
    logits = _split2_dot_nt(rwt_ref[...], u)
    scores = jax.nn.sigmoid(logits)
    sel = scores + rb_ref[...]

    neg = -jnp.inf
    sub_io = lax.broadcasted_iota(I32, (GROUP_SIZE, tm), 0)
    pieces = []
    for g in range(N_GROUPS):
        blk = sel[g * GROUP_SIZE:(g + 1) * GROUP_SIZE, :]
        m1 = jnp.max(blk, axis=0, keepdims=True)
        first = jnp.min(jnp.where(blk == m1, sub_io, GROUP_SIZE), axis=0, keepdims=True)
        m2 = jnp.max(jnp.where(sub_io == first, neg, blk), axis=0, keepdims=True)
        pieces.append(jnp.broadcast_to(m1 + m2, (GROUP_SIZE, tm)))
    gscore = jnp.concatenate(pieces, axis=0)

    e_io = lax.broadcasted_iota(I32, (N_EXPERTS, tm), 0)
    grp_io = e_io // GROUP_SIZE
    gmask = jnp.zeros((N_EXPERTS, tm), jnp.bool_)
    for _ in range(TOPK_GROUPS):
        mx = jnp.max(gscore, axis=0, keepdims=True)
        gi = jnp.min(jnp.where(gscore == mx, grp_io, N_GROUPS), axis=0, keepdims=True)
        hit = grp_io == gi
        gmask = jnp.logical_or(gmask, hit)
        gscore = jnp.where(hit, neg, gscore)

    cur = jnp.where(gmask, sel, neg)
    row_io = lax.broadcasted_iota(I32, (TOP_K, tm), 0)
    eidx = jnp.zeros((TOP_K, tm), I32)
    wraw = jnp.zeros((TOP_K, tm), F32)
    chosen = jnp.zeros((N_EXPERTS, tm), jnp.bool_)
    hits = []
    for kk in range(TOP_K):
        mx = jnp.max(cur, axis=0, keepdims=True)
        ei = jnp.min(jnp.where(cur == mx, e_io, N_EXPERTS), axis=0, keepdims=True)
        hit = e_io == ei
        hits.append(hit)
        sc_k = jnp.sum(jnp.where(hit, scores, 0.0), axis=0, keepdims=True)
        eidx = jnp.where(row_io == kk, ei, eidx)
        wraw = jnp.where(row_io == kk, sc_k, wraw)
        chosen = jnp.logical_or(chosen, hit)
        cur = jnp.where(hit, neg, cur)

    wsum = jnp.sum(wraw, axis=0, keepdims=True)
    wts_ref[...] = wraw / wsum * ROUTED_SCALE
    eidx_ref[...] = eidx

    chosen_f = jnp.where(chosen, 1.0, 0.0)
    r_io = lax.broadcasted_iota(I32, (tm, tm), 0)
    c_io = lax.broadcasted_iota(I32, (tm, tm), 1)
    strict_upper = jnp.where(r_io < c_io, 1.0, 0.0).astype(BF16)
    prefix = jnp.dot(chosen_f.astype(BF16), strict_upper, preferred_element_type=F32)
    rank = prefix + carry_scr[:, 0:1]
    pos = jnp.zeros((TOP_K, tm), F32)
    for kk in range(TOP_K):
        p_k = jnp.sum(jnp.where(hits[kk], rank, 0.0), axis=0, keepdims=True)
        pos = jnp.where(row_io == kk, p_k, pos)
    pos_ref[...] = pos.astype(I32)
    total = carry_scr[...] + jnp.sum(chosen_f, axis=1, keepdims=True)
    carry_scr[...] = total
    cnt_ref[...] = total.astype(I32)


def _route(h, sh, sc, nw, rw_t, rb_col, seq):
    t, d = h.shape
    tm = min(ROUTE_ROWS, seq)
    per_b = seq // tm
    kspec = pl.BlockSpec((TOP_K, tm), lambda i: (0, i))
    return pl.pallas_call(
        _route_body,
        grid=(t // tm,),
        in_specs=[pl.BlockSpec((tm, d), lambda i: (i, 0)),
                  pl.BlockSpec((1, 1, d), lambda i: (i // per_b, 0, 0)),
                  pl.BlockSpec((1, 1, d), lambda i: (i // per_b, 0, 0)),
                  pl.BlockSpec((1, d), lambda i: (0, 0)),
                  pl.BlockSpec((N_EXPERTS, d), lambda i: (0, 0)),
                  pl.BlockSpec((N_EXPERTS, 1), lambda i: (0, 0))],
        out_specs=[pl.BlockSpec((tm, d // 2), lambda i: (i, 0)), kspec, kspec, kspec,
                   pl.BlockSpec((N_EXPERTS, 128), lambda i: (0, 0))],
        out_shape=[jax.ShapeDtypeStruct((t, d // 2), U32),
                   jax.ShapeDtypeStruct((TOP_K, t), I32),
                   jax.ShapeDtypeStruct((TOP_K, t), F32),
                   jax.ShapeDtypeStruct((TOP_K, t), I32),
                   jax.ShapeDtypeStruct((N_EXPERTS, 128), I32)],
        scratch_shapes=[pltpu.VMEM((N_EXPERTS, 128), F32)],
        compiler_params=_cparams(("arbitrary",)),
        name="moe_route",
    )(h, sh, sc, nw, rw_t, rb_col)


def _sc_worker_id():
    return lax.axis_index("s") * SC_CORES + lax.axis_index("c")


def _sc_scatter_rows(x, dest, n_rows):
    t, d = x.shape
    n_k = dest.shape[0]
    assert t % (SC_WORKERS * 2 * SC_CHUNK) == 0
    per_w = t // SC_WORKERS
    n_ch = per_w // SC_CHUNK
    dest4 = dest.reshape(n_k, SC_WORKERS, n_ch, SC_CHUNK).transpose(1, 2, 0, 3)
    mesh = plsc.VectorSubcoreMesh(core_axis_name="c", subcore_axis_name="s")

    @functools.partial(
        pl.kernel, mesh=mesh,
        out_type=jax.ShapeDtypeStruct((n_rows, d), x.dtype),
        scratch_types=[pltpu.VMEM((n_ch, n_k, SC_CHUNK), I32),
                       pltpu.VMEM((SC_CHUNK, d), x.dtype),
                       pltpu.VMEM((SC_CHUNK, d), x.dtype)] + [pltpu.SemaphoreType.DMA] * 4,
        name="moe_dispatch_sc",
    )
    def scatter_kernel(x_hbm, dest_hbm, out_hbm, idx_v, rows0, rows1, l0, l1, s0, s1):
        wid = _sc_worker_id()
        base = wid * per_w
        pltpu.sync_copy(dest_hbm.at[wid], idx_v)
        bufs = ((rows0, l0, s0), (rows1, l1, s1))

        def load(ci, b):
            rows, load_sem, _ = bufs[b]
            return pltpu.make_async_copy(x_hbm.at[pl.ds(base + ci * SC_CHUNK, SC_CHUNK)], rows,
                                         load_sem)

        def scatters(ci, b):
            rows, _, scatter_sem = bufs[b]
            return [pltpu.make_async_copy(rows, out_hbm.at[idx_v.at[ci, j]], scatter_sem)
                    for j in range(n_k)]

        load(0, 0).start()

        @pl.loop(0, n_ch, step=2)
        def _(ci):
            load(ci, 0).wait()
            for cp in scatters(ci, 0):
                cp.start()

            @pl.when(ci > 0)
            def _():
                for cp in scatters(ci - 1, 1):
                    cp.wait()

            load(ci + 1, 1).start()
            load(ci + 1, 1).wait()
            for cp in scatters(ci + 1, 1):
                cp.start()
            for cp in scatters(ci, 0):
                cp.wait()

            @pl.when(ci + 2 < n_ch)
            def _():
                load(ci + 2, 0).start()

        for cp in scatters(n_ch - 1, 1):
            cp.wait()

    return scatter_kernel(x, dest4)


def _sc_gather_rows(table, idx):
    n, d = idx.shape[0], table.shape[1]
    assert n % (SC_WORKERS * 2 * SC_CHUNK) == 0
    per_w = n // SC_WORKERS
    n_ch = per_w // SC_CHUNK
    idx3 = idx.reshape(SC_WORKERS, n_ch, SC_CHUNK)
    mesh = plsc.VectorSubcoreMesh(core_axis_name="c", subcore_axis_name="s")

    @functools.partial(
        pl.kernel, mesh=mesh,
        out_type=jax.ShapeDtypeStruct((n, d), table.dtype),
        scratch_types=[pltpu.VMEM((n_ch, SC_CHUNK), I32),
                       pltpu.VMEM((SC_CHUNK, d), table.dtype),
                       pltpu.VMEM((SC_CHUNK, d), table.dtype)] + [pltpu.SemaphoreType.DMA] * 4,
        name="moe_combine_sc",
    )
    def gather_kernel(table_hbm, idx_hbm, out_hbm, idx_v, rows0, rows1, g0, g1, w0, w1):
        wid = _sc_worker_id()
        base = wid * per_w
        pltpu.sync_copy(idx_hbm.at[wid], idx_v)
        bufs = ((rows0, g0, w0), (rows1, g1, w1))

        def gather(ci, b):
            rows, gather_sem, _ = bufs[b]
            return pltpu.make_async_copy(table_hbm.at[idx_v.at[ci]], rows, gather_sem)

        def write(ci, b):
            rows, _, write_sem = bufs[b]
            return pltpu.make_async_copy(rows, out_hbm.at[pl.ds(base + ci * SC_CHUNK, SC_CHUNK)],
                                         write_sem)

        gather(0, 0).start()

        @pl.loop(0, n_ch, step=2)
        def _(ci):
            gather(ci, 0).wait()
            write(ci, 0).start()

            @pl.when(ci > 0)
            def _():
                write(ci - 1, 1).wait()

            gather(ci + 1, 1).start()
            gather(ci + 1, 1).wait()
            write(ci + 1, 1).start()
            write(ci, 0).wait()

            @pl.when(ci + 2 < n_ch)
            def _():
                gather(ci + 2, 0).start()

        write(n_ch - 1, 1).wait()

    return gather_kernel(table, idx3)


def _expert_body(blk_e_ref, blk_first_ref, blk_valid_ref, xa_ref, xb_ref, w1_ref, w3_ref, w2_ref,
                 y_ref, w1_scr, w3_scr, w2_scr):
    del blk_e_ref
    j = pl.program_id(0)
    valid = blk_valid_ref[j]
    sub, half = xa_ref.shape

    @pl.when(blk_first_ref[j] == 1)
    def _():
        w1_scr[...] = w1_ref[0].astype(BF16)
        w3_scr[...] = w3_ref[0].astype(BF16)
        w2_scr[...] = w2_ref[0].astype(BF16)

    @pl.when(valid > 0)
    def _():
        for s, x_ref in enumerate((xa_ref, xb_ref)):
            rows = lax.broadcasted_iota(I32, x_ref.shape, 0) + s * sub
            lo, hi = _unpack_bf16_pair(jnp.where(rows < valid, x_ref[...], 0))
            lo, hi = lo.astype(BF16), hi.astype(BF16)

            def proj(w_scr):
                return (jnp.dot(lo, w_scr[0:half, :], preferred_element_type=F32)
                        + jnp.dot(hi, w_scr[half:2 * half, :], preferred_element_type=F32))

            hid = _silu(proj(w1_scr)) * proj(w3_scr)
            y = jnp.dot(hid.astype(BF16), w2_scr[...], preferred_element_type=F32)
            y_ref[s * sub:(s + 1) * sub, :] = _pack_bf16_pair(y[:, :half], y[:, half:])

    @pl.when(valid == 0)
    def _():
        y_ref[...] = jnp.zeros(y_ref.shape, U32)


def _experts(xg, blk_e, blk_first, blk_valid, w1, w3, w2, layer):
    n_rows, half = xg.shape
    n_blocks = n_rows // EXPERT_ROWS
    d, de = w1.shape[-2:]
    grid_spec = pltpu.PrefetchScalarGridSpec(
        num_scalar_prefetch=3,
        grid=(n_blocks,),
        in_specs=[pl.BlockSpec((EXPERT_ROWS // 2, half), lambda j, be, bf, bv: (2 * j, 0)),
                  pl.BlockSpec((EXPERT_ROWS // 2, half), lambda j, be, bf, bv: (2 * j + 1, 0)),
                  pl.BlockSpec((None, 1, d, de), lambda j, be, bf, bv: (layer, be[j], 0, 0)),
                  pl.BlockSpec((None, 1, d, de), lambda j, be, bf, bv: (layer, be[j], 0, 0)),
                  pl.BlockSpec((None, 1, de, d), lambda j, be, bf, bv: (layer, be[j], 0, 0))],
        out_specs=pl.BlockSpec((EXPERT_ROWS, half), lambda j, be, bf, bv: (j, 0)),
        scratch_shapes=[pltpu.VMEM((d, de), BF16), pltpu.VMEM((d, de), BF16),
                        pltpu.VMEM((de, d), BF16)],
    )
    return pl.pallas_call(
        _expert_body,
        grid_spec=grid_spec,
        out_shape=jax.ShapeDtypeStruct((n_rows, half), U32),
        compiler_params=_cparams(("arbitrary",)),
        name="moe_experts",
    )(blk_e, blk_first, blk_valid, xg, xg, w1, w3, w2)


def _combine_body(h_ref, u_ref, yg_ref, wts_ref, gt_ref, s1_ref, s3_ref, s2_ref, fw_ref, o_ref, *,
                  final_norm):
    half = u_ref.shape[1]
    lo, hi = _unpack_bf16_pair(u_ref[...])
    lo, hi = lo.astype(BF16), hi.astype(BF16)

    def proj(w_ref):
        return (jnp.dot(lo, w_ref[0:half, :], preferred_element_type=F32)
                + jnp.dot(hi, w_ref[half:2 * half, :], preferred_element_type=F32))

    hid = _silu(proj(s1_ref)) * proj(s3_ref)
    shared = jnp.dot(hid.astype(BF16), s2_ref[...], preferred_element_type=F32)

    routed_lo = routed_hi = None
    for kk in range(TOP_K):
        y_lo, y_hi = _unpack_bf16_pair(yg_ref[kk])
        w = wts_ref[:, kk:kk + 1]
        routed_lo = y_lo * w if routed_lo is None else routed_lo + y_lo * w
        routed_hi = y_hi * w if routed_hi is None else routed_hi + y_hi * w
    gt = gt_ref[0]
    out_lo = h_ref[:, 0:half] + gt[:, 0:half] * (routed_lo + shared[:, 0:half])
    out_hi = (h_ref[:, half:2 * half]
              + gt[:, half:2 * half] * (routed_hi + shared[:, half:2 * half]))
    if final_norm:
        ssq = (jnp.sum(out_lo * out_lo, axis=-1, keepdims=True)
               + jnp.sum(out_hi * out_hi, axis=-1, keepdims=True))
        inv = lax.rsqrt(ssq / (2 * half) + EPS)
        out_lo = (out_lo * inv) * fw_ref[:, 0:half]
        out_hi = (out_hi * inv) * fw_ref[:, half:2 * half]
    o_ref[:, 0:half] = out_lo
    o_ref[:, half:2 * half] = out_hi


def _combine(h, u, yg, wts_tk, gt, s1, s3, s2, final_w, seq, final_norm, part, n_parts):
    t, d = h.shape
    half = d // 2
    tm = min(COMBINE_ROWS, seq)
    per_b = seq // tm
    ds_ = s1.shape[-1]
    steps = t // n_parts // tm
    off = part * steps
    return pl.pallas_call(
        functools.partial(_combine_body, final_norm=final_norm),
        grid=(steps,),
        in_specs=[pl.BlockSpec((tm, d), lambda i: (i + off, 0)),
                  pl.BlockSpec((tm, half), lambda i: (i + off, 0)),
                  pl.BlockSpec((TOP_K, tm, half), lambda i: (0, i, 0)),
                  pl.BlockSpec((tm, TOP_K), lambda i: (i + off, 0)),
                  pl.BlockSpec((1, 1, d), lambda i: ((i + off) // per_b, 0, 0)),
                  pl.BlockSpec((d, ds_), lambda i: (0, 0)),
                  pl.BlockSpec((d, ds_), lambda i: (0, 0)),
                  pl.BlockSpec((ds_, d), lambda i: (0, 0)),
                  pl.BlockSpec((1, d), lambda i: (0, 0))],
        out_specs=pl.BlockSpec((tm, d), lambda i: (i + off, 0)),
        out_shape=jax.ShapeDtypeStruct((t, d), F32),
        input_output_aliases={0: 0},
        compiler_params=_cparams(("parallel",)),
        name="moe_combine",
    )(h, u, yg, wts_tk, gt, s1, s3, s2, final_w)


def _split_w_in(w_in):
    sizes = (2 * M_HEADS * M_DQK, M_HEADS * M_DV, M_HEADS * M_DV, M_HEADS, M_HEADS,
             G_HEADS * G_DK, G_HEADS * G_DK, G_HEADS * G_DV, G_RANK, G_HEADS * G_DV,
             D_MODEL, D_MODEL)
    offs = [0]
    for n in sizes:
        offs.append(offs[-1] + n)
    w16 = w_in.astype(BF16)
    big = jnp.concatenate([w16[:, offs[0]:offs[3]], w16[:, offs[5]:offs[8]], w16[:, offs[9]:offs[12]]],
                          axis=1)
    pad = jnp.zeros((w_in.shape[0], SMALL_COLS - 2 * M_HEADS - G_RANK), BF16)
    small = jnp.concatenate([w16[:, offs[3]:offs[5]], w16[:, offs[8]:offs[9]], pad], axis=1)
    return big, small


def _moe_layout(counts, eidx, pos, n_blocks):
    padded = (counts + EXPERT_ROWS - 1) // EXPERT_ROWS * EXPERT_ROWS
    pend = jnp.cumsum(padded)
    pstart = pend - padded
    experts = jnp.arange(N_EXPERTS, dtype=I32)
    dest = pos + jnp.sum(jnp.where(eidx[..., None] == experts, pstart, 0), axis=-1)
    blk_start = jnp.arange(n_blocks, dtype=I32) * EXPERT_ROWS
    owner = jnp.sum((pend[None, :] <= blk_start[:, None]).astype(I32), axis=1)
    blk_e = jnp.minimum(owner, N_EXPERTS - 1)
    prev = jnp.concatenate([jnp.full((1,), -1, I32), blk_e[:-1]])
    blk_first = (blk_e != prev).astype(I32)
    own = blk_e[:, None] == experts[None, :]
    rows_left = jnp.sum(jnp.where(own, (pstart + counts)[None, :], 0), axis=1) - blk_start
    blk_valid = jnp.clip(jnp.where(owner < N_EXPERTS, rows_left, 0), 0, EXPERT_ROWS)
    return dest.astype(I32), blk_e.astype(I32), blk_first, blk_valid.astype(I32)


def kernel(x, c, ada_w, ada_b, norm_mix_w, norm_moe_w, w_in, m_conv_w, m_gate_b, m_norm_w,
           g_alpha_w, g_alpha_b, g_norm_w, w_pa, w_pb, w_o, router_w, router_b,
           exp_w1, exp_w3, exp_w2, sh_w1, sh_w3, sh_w2, final_norm_w):
    bsz, seq, d = x.shape
    depth = ada_w.shape[0]
    t = bsz * seq
    n_rows = t * TOP_K + N_EXPERTS * EXPERT_ROWS
    n_blocks = n_rows // EXPERT_ROWS

    ada = _ada(c, ada_w, ada_b).reshape(depth, bsz, 6, 1, d)
    h = x.reshape(t, d)
    for l in range(depth):
        sh1, sc1, gt1, sh2, sc2, gt2 = (ada[l, :, i] for i in range(6))

        w_big, w_small = _split_w_in(w_in[l])
        mix_w = norm_mix_w[l][None, :]
        gate_row = jnp.zeros((1, SMALL_COLS), F32)
        gate_row = gate_row.at[0, SMALL_I:SMALL_I + M_HEADS].set(m_gate_b[l, 0])
        gate_row = gate_row.at[0, SMALL_F:SMALL_F + M_HEADS].set(m_gate_b[l, 1])
        ha = _mlstm(h, sh1, sc1, mix_w, w_big, w_small, m_conv_w[l], gate_row, m_norm_w[l][None, :],
                    bsz, seq)
        alpha_full = jnp.zeros((SMALL_COLS, G_HEADS * G_DK), F32)
        alpha_full = alpha_full.at[SMALL_R:SMALL_R + G_RANK].set(g_alpha_w[l]).astype(BF16)
        hb = _gla(h, sh1, sc1, mix_w, w_big, w_small, alpha_full, g_alpha_b[l][None, :],
                  g_norm_w[l][None, :], bsz, seq)
        h = _mixout(h, sh1, sc1, mix_w, w_big, ha, hb, gt1, w_pa[l].astype(BF16),
                    w_pb[l].astype(BF16), w_o[l].astype(BF16), seq)

        u, eidx, wts, pos, cnt = _route(h, sh2, sc2, norm_moe_w[l][None, :],
                                        router_w[l].T, router_b[l][:, None], seq)
        dest, blk_e, blk_first, blk_valid = _moe_layout(cnt[:, 0], eidx, pos, n_blocks)
        xg = _sc_scatter_rows(u, dest, n_rows)
        y = _experts(xg, blk_e, blk_first, blk_valid, exp_w1, exp_w3, exp_w2, l)
        tp = t // COMBINE_PARTS
        for p in range(COMBINE_PARTS):
            dest_p = dest[:, p * tp:(p + 1) * tp].reshape(-1)
            yg = _sc_gather_rows(y, dest_p).reshape(TOP_K, tp, d // 2)
            h = _combine(h, u, yg, wts.T, gt2, sh_w1[l].astype(BF16), sh_w3[l].astype(BF16),
                         sh_w2[l].astype(BF16), final_norm_w[None, :], seq,
                         final_norm=(l == depth - 1), part=p, n_parts=COMBINE_PARTS)

    return h.reshape(bsz, seq, d)
```

```python
import functools

import jax
import jax.numpy as jnp
import numpy as np
from jax import lax
from jax.experimental import pallas as pl
from jax.experimental.pallas import tpu as pltpu
from jax.experimental.pallas import tpu_sc as plsc

F32 = jnp.float32
BF16 = jnp.bfloat16
I32 = jnp.int32
U32 = jnp.uint32
HI_MASK = np.uint32(0xFFFF0000)

LANES = 128

SC_CORES = 2
SC_SUBCORES = 16
SC_WORKERS = SC_CORES * SC_SUBCORES
SC_CHUNK = 64

D_MODEL = 1024
M_HEADS = 4
M_DQK = 128
M_DV = 256
M_CONV = 4
GATE_CAP = 15.0
G_HEADS = 4
G_DK = 128
G_DV = 256
G_RANK = 16
G_TAU = 16.0
G_CHUNK = 64
N_EXPERTS = 64
TOP_K = 8
N_GROUPS = 8
GROUP_SIZE = N_EXPERTS // N_GROUPS
TOPK_GROUPS = 4
ROUTED_SCALE = 2.5
EPS = 1e-6

M_CHUNK_ROWS = 256
G_BLOCK_ROWS = 256
SEQ_GROUP = 2
PROJ_PIECE_COLS = 256
MIX_ROWS = 512
ROUTE_ROWS = 512
EXPERT_ROWS = 1024
COMBINE_ROWS = 512
COMBINE_PARTS = 2
CONV_HALO = 8
VMEM_LIMIT = 48 * 1024 * 1024

M_GROUP_COLS = 2 * M_HEADS * M_DQK + 2 * M_HEADS * M_DV
G_GROUP_COLS = 2 * G_HEADS * G_DK + 2 * G_HEADS * G_DV
GATE_GROUP_COLS = 2 * D_MODEL
COL_M_GROUP = 0
COL_G_GROUP = COL_M_GROUP + M_GROUP_COLS
COL_GATE_GROUP = COL_G_GROUP + G_GROUP_COLS
SMALL_COLS = LANES
SMALL_I, SMALL_F, SMALL_R = 0, M_HEADS, 2 * M_HEADS


def _cparams(sem, vmem=VMEM_LIMIT):
    return pltpu.CompilerParams(dimension_semantics=sem, vmem_limit_bytes=vmem)


def _silu(x):
    return x * jax.nn.sigmoid(x)


def _log_sigmoid(x):
    return jnp.minimum(x, 0.0) - jnp.log1p(jnp.exp(-jnp.abs(x)))


def _modulated_rmsnorm(x, w, sc, sh):
    y = x * lax.rsqrt(jnp.mean(x * x, axis=-1, keepdims=True) + EPS)
    return (y * w) * (1.0 + sc) + sh


def _pack_bf16_pair(lo, hi):
    lo_bits = lax.bitcast_convert_type(lo.astype(BF16).astype(F32), U32)
    hi_bits = lax.bitcast_convert_type(hi.astype(BF16).astype(F32), U32)
    return (lo_bits >> 16) | (hi_bits & HI_MASK)


def _unpack_bf16_pair(packed):
    lo = lax.bitcast_convert_type(packed << 16, F32)
    hi = lax.bitcast_convert_type(packed & HI_MASK, F32)
    return lo, hi


def _lower_tri(n, dtype):
    r = lax.broadcasted_iota(I32, (n, n), 0)
    c = lax.broadcasted_iota(I32, (n, n), 1)
    return (r >= c).astype(dtype)


def _split3_dot(lhs01, x):
    hi = x.astype(BF16)
    r1 = x - hi.astype(F32)
    mid = r1.astype(BF16)
    lo = (r1 - mid.astype(F32)).astype(BF16)
    return (jnp.dot(lhs01, hi, preferred_element_type=F32)
            + jnp.dot(lhs01, mid, preferred_element_type=F32)
            + jnp.dot(lhs01, lo, preferred_element_type=F32))


def _split2_dot_nt(a, b):
    nt = (((1,), (1,)), ((), ()))
    a_hi = a.astype(BF16)
    a_lo = (a - a_hi.astype(F32)).astype(BF16)
    b_hi = b.astype(BF16)
    b_lo = (b - b_hi.astype(F32)).astype(BF16)
    return (lax.dot_general(a_hi, b_hi, nt, preferred_element_type=F32)
            + lax.dot_general(a_hi, b_lo, nt, preferred_element_type=F32)
            + lax.dot_general(a_lo, b_hi, nt, preferred_element_type=F32))


def _ada_body(c_ref, w_ref, b_ref, o_ref):
    cond = _silu(c_ref[...])
    o_ref[0] = jnp.dot(cond.astype(BF16), w_ref[0].astype(BF16),
                       preferred_element_type=F32) + b_ref[0]


def _ada(c, ada_w, ada_b):
    depth, d, six_d = ada_w.shape
    bsz = c.shape[0]
    nj = six_d // d
    return pl.pallas_call(
        _ada_body,
        grid=(depth, nj),
        in_specs=[pl.BlockSpec((bsz, d), lambda l, j: (0, 0)),
                  pl.BlockSpec((1, d, d), lambda l, j: (l, 0, j)),
                  pl.BlockSpec((1, 1, d), lambda l, j: (l, 0, j))],
        out_specs=pl.BlockSpec((1, bsz, d), lambda l, j: (l, 0, j)),
        out_shape=jax.ShapeDtypeStruct((depth, bsz, six_d), F32),
        compiler_params=_cparams(("parallel", "parallel")),
        name="ada_ln",
    )(c, ada_w, ada_b.reshape(depth, 1, six_d))


def _mlstm_body(h0_ref, hn_ref, sh_ref, sc_ref, mw_ref, w_ref, ws_ref, cw_ref, gb_ref, nw_ref,
                ha_ref, qk_nxt, v_nxt, o_nxt, sm_nxt, xe_scr, v_scr, o_scr, sm_scr,
                c_scr, n_scr, m_scr):
    n_seq, rows = hn_ref.shape[0], hn_ref.shape[1]
    half = M_HEADS * M_DQK
    vcols = M_HEADS * M_DV

    def projection_pieces(h_ref):
        u = jnp.concatenate(
            [_modulated_rmsnorm(h_ref[g], mw_ref[...], sc_ref[g], sh_ref[g]) for g in range(n_seq)],
            axis=0).astype(BF16)

        def piece(dst, w_lo, lo, width):
            def run():
                dst[:, lo:lo + width] = jnp.dot(u, w_ref[:, w_lo + lo:w_lo + lo + width],
                                                preferred_element_type=F32).astype(dst.dtype)
            return run

        def small_piece():
            sm_nxt[...] = jnp.dot(u, ws_ref[...], preferred_element_type=F32)

        pieces = [small_piece]
        for dst, w_lo in ((qk_nxt, 0), (v_nxt, 2 * half), (o_nxt, 2 * half + vcols)):
            for lo in range(0, vcols, PROJ_PIECE_COLS):
                pieces.append(piece(dst, w_lo, lo, PROJ_PIECE_COLS))
        return pieces

    @pl.when(pl.program_id(1) == 0)
    def _():
        xe_scr[:, 0:CONV_HALO, :] = jnp.zeros((n_seq, CONV_HALO, 2 * half), F32)
        c_scr[...] = jnp.zeros(c_scr.shape, F32)
        n_scr[...] = jnp.zeros(n_scr.shape, F32)
        m_scr[...] = jnp.zeros(m_scr.shape, F32)
        for run in projection_pieces(h0_ref):
            run()

    for g in range(n_seq):
        xe_scr[g, CONV_HALO:CONV_HALO + rows, :] = qk_nxt[g * rows:(g + 1) * rows, :]
    v_scr[...] = v_nxt[...]
    o_scr[...] = o_nxt[...]
    sm_scr[...] = sm_nxt[...]
    pending = projection_pieces(hn_ref)

    r_io = lax.broadcasted_iota(I32, (rows, rows), 0)
    c_io = lax.broadcasted_iota(I32, (rows, rows), 1)
    causal = r_io >= c_io
    tri = _lower_tri(rows, BF16)

    for g in range(n_seq):
        gr = slice(g * rows, (g + 1) * rows)
        conv = None
        for j in range(M_CONV):
            off = CONV_HALO - (M_CONV - 1) + j
            term = xe_scr[g, off:off + rows, :] * cw_ref[j:j + 1, :]
            conv = term if conv is None else conv + term
        qk = _silu(conv)
        xe_scr[g, 0:CONV_HALO, :] = xe_scr[g, rows:rows + CONV_HALO, :]

        capped = GATE_CAP * jnp.tanh((sm_scr[gr, :] + gb_ref[...]) / GATE_CAP)
        li_all = capped
        lf_all = _log_sigmoid(capped)
        b_all = _split3_dot(tri, lf_all)
        li_t = li_all.T
        b_t = b_all.T

        for h in range(M_HEADS):
            if pending:
                pending.pop(0)()
            sidx = g * M_HEADS + h
            q = (qk[:, h * M_DQK:(h + 1) * M_DQK] * (M_DQK ** -0.5)).astype(BF16)
            k = qk[:, half + h * M_DQK:half + (h + 1) * M_DQK]
            kb = k.astype(BF16)
            v = v_scr[gr, h * M_DV:(h + 1) * M_DV]
            li_c = li_all[:, SMALL_I + h:SMALL_I + h + 1]
            b_c = b_all[:, SMALL_F + h:SMALL_F + h + 1]
            li_r = li_t[SMALL_I + h:SMALL_I + h + 1, :]
            b_r = b_t[SMALL_F + h:SMALL_F + h + 1, :]
            gsum = b_c[rows - 1:rows, :]
            m_prev = m_scr[sidx][:, 0:1]
            c_prev = c_scr[sidx]
            n_prev = n_scr[sidx]

            d_mat = jnp.where(causal, b_c - b_r + li_r, -jnp.inf)
            m_inter = b_c + m_prev
            m_t = jnp.maximum(jnp.max(d_mat, axis=1, keepdims=True), m_inter)
            s = lax.dot_general(q, kb, (((1,), (1,)), ((), ())), preferred_element_type=F32)
            p = jnp.exp(d_mat - m_t) * s
            w_inter = jnp.exp(m_inter - m_t)
            num = (jnp.dot(p.astype(BF16), v, preferred_element_type=F32)
                   + w_inter * jnp.dot(q, c_prev.astype(BF16), preferred_element_type=F32))
            qn = jnp.sum(q.astype(F32) * n_prev, axis=1, keepdims=True)
            den = jnp.sum(p, axis=1, keepdims=True) + w_inter * qn
            hh = num / jnp.maximum(jnp.abs(den), jnp.exp(-m_t))

            if pending:
                pending.pop(0)()

            a_r = gsum - b_r + li_r
            a_c = gsum - b_c + li_c
            m_new = jnp.maximum(gsum + m_prev, jnp.max(a_r, axis=1, keepdims=True))
            decay = jnp.exp(gsum + m_prev - m_new)
            wk = jnp.exp(a_c - m_new) * k
            c_scr[sidx] = decay * c_prev + lax.dot_general(
                wk.astype(BF16), v, (((0,), (0,)), ((), ())), preferred_element_type=F32)
            n_scr[sidx] = decay * n_prev + jnp.sum(wk, axis=0, keepdims=True)
            m_scr[sidx] = jnp.broadcast_to(m_new, m_scr.shape[1:])

            y = hh * lax.rsqrt(jnp.mean(hh * hh, axis=-1, keepdims=True) + EPS)
            y = y * nw_ref[:, h * M_DV:(h + 1) * M_DV]
            gate = jax.nn.sigmoid(o_scr[gr, h * M_DV:(h + 1) * M_DV])
            ha_ref[g, :, h * M_DV:(h + 1) * M_DV] = (y * gate).astype(BF16)
    for run in pending:
        run()


def _mlstm(h, sh, sc, mix_w, w_big, w_small, conv_w, gate_row, norm_w, bsz, seq):
    t, d = h.shape
    rows = min(M_CHUNK_ROWS, seq)
    nc = seq // rows
    ng = SEQ_GROUP if bsz % SEQ_GROUP == 0 else 1
    half = M_HEADS * M_DQK
    vcols = M_HEADS * M_DV
    wcols = M_GROUP_COLS
    assert COL_M_GROUP % wcols == 0
    resident = pl.Buffered(1)
    nxt = [pltpu.VMEM((ng * rows, 2 * half), F32), pltpu.VMEM((ng * rows, vcols), BF16),
           pltpu.VMEM((ng * rows, vcols), F32), pltpu.VMEM((ng * rows, SMALL_COLS), F32)]
    out = pl.pallas_call(
        _mlstm_body,
        grid=(bsz // ng, nc),
        in_specs=[pl.BlockSpec((ng, rows, d), lambda b, c: (b, 0, 0)),
                  pl.BlockSpec((ng, rows, d), lambda b, c: (b, jnp.minimum(c + 1, nc - 1), 0)),
                  pl.BlockSpec((ng, 1, d), lambda b, c: (b, 0, 0)),
                  pl.BlockSpec((ng, 1, d), lambda b, c: (b, 0, 0)),
                  pl.BlockSpec((1, d), lambda b, c: (0, 0)),
                  pl.BlockSpec((d, wcols), lambda b, c: (0, 0), pipeline_mode=resident),
                  pl.BlockSpec((d, SMALL_COLS), lambda b, c: (0, 0), pipeline_mode=resident),
                  pl.BlockSpec((M_CONV, 2 * half), lambda b, c: (0, 0)),
                  pl.BlockSpec((1, SMALL_COLS), lambda b, c: (0, 0)),
                  pl.BlockSpec((1, vcols), lambda b, c: (0, 0))],
        out_specs=pl.BlockSpec((ng, rows, vcols), lambda b, c: (b, c, 0)),
        out_shape=jax.ShapeDtypeStruct((bsz, seq, vcols), BF16),
        scratch_shapes=nxt + [pltpu.VMEM((ng, rows + CONV_HALO, 2 * half), F32)] + nxt[1:]
        + [pltpu.VMEM((ng * M_HEADS, M_DQK, M_DV), F32),
           pltpu.VMEM((ng * M_HEADS, 1, M_DQK), F32),
           pltpu.VMEM((ng * M_HEADS, 1, LANES), F32)],
        compiler_params=_cparams(("parallel", "arbitrary")),
        name="mlstm",
    )(h.reshape(bsz, seq, d), h.reshape(bsz, seq, d), sh, sc, mix_w, w_big, w_small, conv_w,
      gate_row, norm_w)
    return out.reshape(t, vcols)


def _gla_body(h0_ref, hn_ref, sh_ref, sc_ref, mw_ref, w_ref, ws_ref, aw_ref, ab_ref, nw_ref, hb_ref,
              qk_nxt, v_nxt, z_nxt, sm_nxt, qk_scr, v_scr, z_scr, sm_scr, st_scr):
    n_seq, rows = hn_ref.shape[0], hn_ref.shape[1]
    n_chunks = rows // G_CHUNK
    kcols = G_HEADS * G_DK
    vcols = G_HEADS * G_DV

    def projection_pieces(h_ref):
        u = jnp.concatenate(
            [_modulated_rmsnorm(h_ref[g], mw_ref[...], sc_ref[g], sh_ref[g]) for g in range(n_seq)],
            axis=0).astype(BF16)

        def piece(dst, w_lo, lo, width):
            def run():
                dst[:, lo:lo + width] = jnp.dot(u, w_ref[:, w_lo + lo:w_lo + lo + width],
                                                preferred_element_type=F32).astype(dst.dtype)
            return run

        def small_piece():
            sm_nxt[...] = jnp.dot(u, ws_ref[...], preferred_element_type=F32)

        pieces = [small_piece]
        for dst, w_lo in ((qk_nxt, 0), (v_nxt, 2 * kcols), (z_nxt, 2 * kcols + vcols)):
            for lo in range(0, vcols, PROJ_PIECE_COLS):
                pieces.append(piece(dst, w_lo, lo, PROJ_PIECE_COLS))
        return pieces

    @pl.when(pl.program_id(1) == 0)
    def _():
        st_scr[...] = jnp.zeros(st_scr.shape, F32)
        for run in projection_pieces(h0_ref):
            run()

    qk_scr[...] = qk_nxt[...]
    v_scr[...] = v_nxt[...]
    z_scr[...] = z_nxt[...]
    sm_scr[...] = sm_nxt[...]
    pending = projection_pieces(hn_ref)

    r_io = lax.broadcasted_iota(I32, (rows, rows), 0)
    c_io = lax.broadcasted_iota(I32, (rows, rows), 1)
    chunk_causal = jnp.logical_and(r_io >= c_io, r_io // G_CHUNK == c_io // G_CHUNK)
    chunk_tri = jnp.where(chunk_causal, 1.0, 0.0).astype(BF16)

    for g in range(n_seq):
        gr = slice(g * rows, (g + 1) * rows)
        for _ in range((len(pending) + n_seq - 1 - g) // (n_seq - g)):
            pending.pop(0)()
        q = qk_scr[gr, 0:kcols] * (G_DK ** -0.5)
        k = qk_scr[gr, kcols:2 * kcols]

        logits = jnp.dot(sm_scr[gr, :].astype(BF16), aw_ref[...], preferred_element_type=F32)
        la = _log_sigmoid(logits + ab_ref[...]) / G_TAU
        bc = _split3_dot(chunk_tri, la)
        gcs = [bc[(ci + 1) * G_CHUNK - 1:(ci + 1) * G_CHUNK, :] for ci in range(n_chunks)]
        gc_rows = jnp.concatenate([jnp.broadcast_to(gc, (G_CHUNK, kcols)) for gc in gcs], axis=0)

        q_in = (q * jnp.exp(bc)).astype(BF16)
        k_in = (k * jnp.exp(-bc)).astype(BF16)
        k_out = (k * jnp.exp(gc_rows - bc)).astype(BF16)

        for h in range(G_HEADS):
            ks = slice(h * G_DK, (h + 1) * G_DK)
            vs = slice(h * G_DV, (h + 1) * G_DV)
            v = v_scr[gr, vs]
            att = lax.dot_general(q_in[:, ks], k_in[:, ks], (((1,), (1,)), ((), ())),
                                  preferred_element_type=F32)
            att = jnp.where(chunk_causal, att, 0.0).astype(BF16)
            o_intra = jnp.dot(att, v, preferred_element_type=F32)
            st = st_scr[g * G_HEADS + h]
            outs = []
            for ci in range(n_chunks):
                rs = slice(ci * G_CHUNK, (ci + 1) * G_CHUNK)
                o_inter = lax.dot_general(q_in[rs, ks], st.astype(BF16),
                                          (((1,), (1,)), ((), ())), preferred_element_type=F32)
                outs.append(o_intra[rs, :] + o_inter)
                st = jnp.exp(gcs[ci][:, ks]) * st + lax.dot_general(
                    v[rs, :], k_out[rs, ks], (((0,), (0,)), ((), ())), preferred_element_type=F32)
            st_scr[g * G_HEADS + h] = st
            o = jnp.concatenate(outs, axis=0)
            y = o * lax.rsqrt(jnp.mean(o * o, axis=-1, keepdims=True) + EPS)
            y = y * nw_ref[:, vs]
            hb_ref[g, :, vs] = (y * _silu(z_scr[gr, vs])).astype(BF16)


def _gla(h, sh, sc, mix_w, w_big, w_small, alpha_full, alpha_b, norm_w, bsz, seq):
    t, d = h.shape
    rows = min(G_BLOCK_ROWS, seq)
    nb = seq // rows
    ng = SEQ_GROUP if bsz % SEQ_GROUP == 0 else 1
    kcols = G_HEADS * G_DK
    vcols = G_HEADS * G_DV
    wcols = G_GROUP_COLS
    assert COL_G_GROUP % wcols == 0
    resident = pl.Buffered(1)
    nxt = [pltpu.VMEM((ng * rows, 2 * kcols), F32), pltpu.VMEM((ng * rows, vcols), BF16),
           pltpu.VMEM((ng * rows, vcols), F32), pltpu.VMEM((ng * rows, SMALL_COLS), F32)]
    out = pl.pallas_call(
        _gla_body,
        grid=(bsz // ng, nb),
        in_specs=[pl.BlockSpec((ng, rows, d), lambda b, c: (b, 0, 0)),
                  pl.BlockSpec((ng, rows, d), lambda b, c: (b, jnp.minimum(c + 1, nb - 1), 0)),
                  pl.BlockSpec((ng, 1, d), lambda b, c: (b, 0, 0)),
                  pl.BlockSpec((ng, 1, d), lambda b, c: (b, 0, 0)),
                  pl.BlockSpec((1, d), lambda b, c: (0, 0)),
                  pl.BlockSpec((d, wcols), lambda b, c: (0, COL_G_GROUP // wcols),
                               pipeline_mode=resident),
                  pl.BlockSpec((d, SMALL_COLS), lambda b, c: (0, 0), pipeline_mode=resident),
                  pl.BlockSpec((SMALL_COLS, kcols), lambda b, c: (0, 0)),
                  pl.BlockSpec((1, kcols), lambda b, c: (0, 0)),
                  pl.BlockSpec((1, vcols), lambda b, c: (0, 0))],
        out_specs=pl.BlockSpec((ng, rows, vcols), lambda b, c: (b, c, 0)),
        out_shape=jax.ShapeDtypeStruct((bsz, seq, vcols), BF16),
        scratch_shapes=nxt + nxt + [pltpu.VMEM((ng * G_HEADS, G_DV, G_DK), F32)],
        compiler_params=_cparams(("parallel", "arbitrary")),
        name="gla",
    )(h.reshape(bsz, seq, d), h.reshape(bsz, seq, d), sh, sc, mix_w, w_big, w_small, alpha_full,
      alpha_b, norm_w)
    return out.reshape(t, vcols)


def _mixout_body(h_ref, sh_ref, sc_ref, mw_ref, wg_ref, ha_ref, hb_ref, gt_ref, wpa_ref, wpb_ref,
                 wo_ref, o_ref):
    d = h_ref.shape[1]
    h = h_ref[...]
    u = _modulated_rmsnorm(h, mw_ref[...], sc_ref[0], sh_ref[0]).astype(BF16)
    ga = jnp.dot(u, wg_ref[:, 0:d], preferred_element_type=F32)
    gb = jnp.dot(u, wg_ref[:, d:2 * d], preferred_element_type=F32)
    a = jnp.dot(ha_ref[...], wpa_ref[...], preferred_element_type=F32)
    b = jnp.dot(hb_ref[...], wpb_ref[...], preferred_element_type=F32)
    y = jax.nn.sigmoid(ga) * a + jax.nn.sigmoid(gb) * b
    o_ref[...] = h + gt_ref[0] * jnp.dot(y.astype(BF16), wo_ref[...], preferred_element_type=F32)


def _mixout(h, sh, sc, mix_w, w_big, ha, hb, gt, w_pa, w_pb, w_o, seq):
    t, d = h.shape
    tm = min(MIX_ROWS, seq)
    per_b = seq // tm
    assert GATE_GROUP_COLS == 2 * d and COL_GATE_GROUP % GATE_GROUP_COLS == 0
    resident = pl.Buffered(1)
    wspec = pl.BlockSpec((d, d), lambda i: (0, 0), pipeline_mode=resident)
    bspec = pl.BlockSpec((1, 1, d), lambda i: (i // per_b, 0, 0))
    return pl.pallas_call(
        _mixout_body,
        grid=(t // tm,),
        in_specs=[pl.BlockSpec((tm, d), lambda i: (i, 0)),
                  bspec, bspec,
                  pl.BlockSpec((1, d), lambda i: (0, 0)),
                  pl.BlockSpec((d, 2 * d), lambda i: (0, COL_GATE_GROUP // GATE_GROUP_COLS),
                               pipeline_mode=resident),
                  pl.BlockSpec((tm, d), lambda i: (i, 0)),
                  pl.BlockSpec((tm, d), lambda i: (i, 0)),
                  bspec,
                  wspec, wspec, wspec],
        out_specs=pl.BlockSpec((tm, d), lambda i: (i, 0)),
        out_shape=jax.ShapeDtypeStruct((t, d), F32),
        compiler_params=_cparams(("parallel",)),
        name="mix_out",
    )(h, sh, sc, mix_w, w_big, ha, hb, gt, w_pa, w_pb, w_o)


def _route_body(h_ref, sh_ref, sc_ref, nw_ref, rwt_ref, rb_ref,
                u_ref, eidx_ref, wts_ref, pos_ref, cnt_ref, carry_scr):
    tm = h_ref.shape[0]

    @pl.when(pl.program_id(0) == 0)
    def _():
        carry_scr[...] = jnp.zeros(carry_scr.shape, F32)

    u = _modulated_rmsnorm(h_ref[...], nw_ref[...], sc_ref[0], sh_ref[0])
    half = u.shape[1] // 2
    u_ref[...] = _pack_bf16_pair(u[:, :half], u[:, half:])
    logits = _split2_dot_nt(rwt_ref[...], u)
    scores = jax.nn.sigmoid(logits)
    sel = scores + rb_ref[...]

    neg = -jnp.inf
    sub_io = lax.broadcasted_iota(I32, (GROUP_SIZE, tm), 0)
    pieces = []
    for g in range(N_GROUPS):
        blk = sel[g * GROUP_SIZE:(g + 1) * GROUP_SIZE, :]
        m1 = jnp.max(blk, axis=0, keepdims=True)
        first = jnp.min(jnp.where(blk == m1, sub_io, GROUP_SIZE), axis=0, keepdims=True)
        m2 = jnp.max(jnp.where(sub_io == first, neg, blk), axis=0, keepdims=True)
        pieces.append(jnp.broadcast_to(m1 + m2, (GROUP_SIZE, tm)))
    gscore = jnp.concatenate(pieces, axis=0)

    e_io = lax.broadcasted_iota(I32, (N_EXPERTS, tm), 0)
    grp_io = e_io // GROUP_SIZE
    gmask = jnp.zeros((N_EXPERTS, tm), jnp.bool_)
    for _ in range(TOPK_GROUPS):
        mx = jnp.max(gscore, axis=0, keepdims=True)
        gi = jnp.min(jnp.where(gscore == mx, grp_io, N_GROUPS), axis=0, keepdims=True)
        hit = grp_io == gi
        gmask = jnp.logical_or(gmask, hit)
        gscore = jnp.where(hit, neg, gscore)

    cur = jnp.where(gmask, sel, neg)
    row_io = lax.broadcasted_iota(I32, (TOP_K, tm), 0)
    eidx = jnp.zeros((TOP_K, tm), I32)
    wraw = jnp.zeros((TOP_K, tm), F32)
    chosen = jnp.zeros((N_EXPERTS, tm), jnp.bool_)
    hits = []
    for kk in range(TOP_K):
        mx = jnp.max(cur, axis=0, keepdims=True)
        ei = jnp.min(jnp.where(cur == mx, e_io, N_EXPERTS), axis=0, keepdims=True)
        hit = e_io == ei
        hits.append(hit)
        sc_k = jnp.sum(jnp.where(hit, scores, 0.0), axis=0, keepdims=True)
        eidx = jnp.where(row_io == kk, ei, eidx)
        wraw = jnp.where(row_io == kk, sc_k, wraw)
        chosen = jnp.logical_or(chosen, hit)
        cur = jnp.where(hit, neg, cur)

    wsum = jnp.sum(wraw, axis=0, keepdims=True)
    wts_ref[...] = wraw / wsum * ROUTED_SCALE
    eidx_ref[...] = eidx

    chosen_f = jnp.where(chosen, 1.0, 0.0)
    r_io = lax.broadcasted_iota(I32, (tm, tm), 0)
    c_io = lax.broadcasted_iota(I32, (tm, tm), 1)
    strict_upper = jnp.where(r_io < c_io, 1.0, 0.0).astype(BF16)
    prefix = jnp.dot(chosen_f.astype(BF16), strict_upper, preferred_element_type=F32)
    rank = prefix + carry_scr[:, 0:1]
    pos = jnp.zeros((TOP_K, tm), F32)
    for kk in range(TOP_K):
        p_k = jnp.sum(jnp.where(hits[kk], rank, 0.0), axis=0, keepdims=True)
        pos = jnp.where(row_io == kk, p_k, pos)
    pos_ref[...] = pos.astype(I32)
    total = carry_scr[...] + jnp.sum(chosen_f, axis=1, keepdims=True)
    carry_scr[...] = total
    cnt_ref[...] = total.astype(I32)


def _route(h, sh, sc, nw, rw_t, rb_col, seq):
    t, d = h.shape
    tm = min(ROUTE_ROWS, seq)
    per_b = seq // tm
    kspec = pl.BlockSpec((TOP_K, tm), lambda i: (0, i))
    return pl.pallas_call(
        _route_body,
        grid=(t // tm,),
        in_specs=[pl.BlockSpec((tm, d), lambda i: (i, 0)),
                  pl.BlockSpec((1, 1, d), lambda i: (i // per_b, 0, 0)),
                  pl.BlockSpec((1, 1, d), lambda i: (i // per_b, 0, 0)),
                  pl.BlockSpec((1, d), lambda i: (0, 0)),
                  pl.BlockSpec((N_EXPERTS, d), lambda i: (0, 0)),
                  pl.BlockSpec((N_EXPERTS, 1), lambda i: (0, 0))],
        out_specs=[pl.BlockSpec((tm, d // 2), lambda i: (i, 0)), kspec, kspec, kspec,
                   pl.BlockSpec((N_EXPERTS, LANES), lambda i: (0, 0))],
        out_shape=[jax.ShapeDtypeStruct((t, d // 2), U32),
                   jax.ShapeDtypeStruct((TOP_K, t), I32),
                   jax.ShapeDtypeStruct((TOP_K, t), F32),
                   jax.ShapeDtypeStruct((TOP_K, t), I32),
                   jax.ShapeDtypeStruct((N_EXPERTS, LANES), I32)],
        scratch_shapes=[pltpu.VMEM((N_EXPERTS, LANES), F32)],
        compiler_params=_cparams(("arbitrary",)),
        name="moe_route",
    )(h, sh, sc, nw, rw_t, rb_col)


def _sc_worker_id():
    return lax.axis_index("s") * SC_CORES + lax.axis_index("c")


def _sc_scatter_rows(x, dest, n_rows):
    t, d = x.shape
    n_k = dest.shape[0]
    assert t % (SC_WORKERS * 2 * SC_CHUNK) == 0
    per_w = t // SC_WORKERS
    n_ch = per_w // SC_CHUNK
    dest4 = dest.reshape(n_k, SC_WORKERS, n_ch, SC_CHUNK).transpose(1, 2, 0, 3)
    mesh = plsc.VectorSubcoreMesh(core_axis_name="c", subcore_axis_name="s")

    @functools.partial(
        pl.kernel, mesh=mesh,
        out_type=jax.ShapeDtypeStruct((n_rows, d), x.dtype),
        scratch_types=[pltpu.VMEM((n_ch, n_k, SC_CHUNK), I32),
                       pltpu.VMEM((SC_CHUNK, d), x.dtype),
                       pltpu.VMEM((SC_CHUNK, d), x.dtype)] + [pltpu.SemaphoreType.DMA] * 4,
        name="moe_dispatch_sc",
    )
    def scatter_kernel(x_hbm, dest_hbm, out_hbm, idx_v, rows0, rows1, l0, l1, s0, s1):
        wid = _sc_worker_id()
        base = wid * per_w
        pltpu.sync_copy(dest_hbm.at[wid], idx_v)
        bufs = ((rows0, l0, s0), (rows1, l1, s1))

        def load(ci, b):
            rows, load_sem, _ = bufs[b]
            return pltpu.make_async_copy(x_hbm.at[pl.ds(base + ci * SC_CHUNK, SC_CHUNK)], rows,
                                         load_sem)

        def scatters(ci, b):
            rows, _, scatter_sem = bufs[b]
            return [pltpu.make_async_copy(rows, out_hbm.at[idx_v.at[ci, j]], scatter_sem)
                    for j in range(n_k)]

        load(0, 0).start()

        @pl.loop(0, n_ch, step=2)
        def _(ci):
            load(ci, 0).wait()
            for cp in scatters(ci, 0):
                cp.start()

            @pl.when(ci > 0)
            def _():
                for cp in scatters(ci - 1, 1):
                    cp.wait()

            load(ci + 1, 1).start()
            load(ci + 1, 1).wait()
            for cp in scatters(ci + 1, 1):
                cp.start()
            for cp in scatters(ci, 0):
                cp.wait()

            @pl.when(ci + 2 < n_ch)
            def _():
                load(ci + 2, 0).start()

        for cp in scatters(n_ch - 1, 1):
            cp.wait()

    return scatter_kernel(x, dest4)


def _sc_gather_rows(table, idx):
    n, d = idx.shape[0], table.shape[1]
    assert n % (SC_WORKERS * 2 * SC_CHUNK) == 0
    per_w = n // SC_WORKERS
    n_ch = per_w // SC_CHUNK
    idx3 = idx.reshape(SC_WORKERS, n_ch, SC_CHUNK)
    mesh = plsc.VectorSubcoreMesh(core_axis_name="c", subcore_axis_name="s")

    @functools.partial(
        pl.kernel, mesh=mesh,
        out_type=jax.ShapeDtypeStruct((n, d), table.dtype),
        scratch_types=[pltpu.VMEM((n_ch, SC_CHUNK), I32),
                       pltpu.VMEM((SC_CHUNK, d), table.dtype),
                       pltpu.VMEM((SC_CHUNK, d), table.dtype)] + [pltpu.SemaphoreType.DMA] * 4,
        name="moe_combine_sc",
    )
    def gather_kernel(table_hbm, idx_hbm, out_hbm, idx_v, rows0, rows1, g0, g1, w0, w1):
        wid = _sc_worker_id()
        base = wid * per_w
        pltpu.sync_copy(idx_hbm.at[wid], idx_v)
        bufs = ((rows0, g0, w0), (rows1, g1, w1))

        def gather(ci, b):
            rows, gather_sem, _ = bufs[b]
            return pltpu.make_async_copy(table_hbm.at[idx_v.at[ci]], rows, gather_sem)

        def write(ci, b):
            rows, _, write_sem = bufs[b]
            return pltpu.make_async_copy(rows, out_hbm.at[pl.ds(base + ci * SC_CHUNK, SC_CHUNK)],
                                         write_sem)

        gather(0, 0).start()

        @pl.loop(0, n_ch, step=2)
        def _(ci):
            gather(ci, 0).wait()
            write(ci, 0).start()

            @pl.when(ci > 0)
            def _():
                write(ci - 1, 1).wait()

            gather(ci + 1, 1).start()
            gather(ci + 1, 1).wait()
            write(ci + 1, 1).start()
            write(ci, 0).wait()

            @pl.when(ci + 2 < n_ch)
            def _():
                gather(ci + 2, 0).start()

        write(n_ch - 1, 1).wait()

    return gather_kernel(table, idx3)


def _expert_body(blk_e_ref, blk_first_ref, blk_valid_ref, x_ref, w1_ref, w3_ref, w2_ref, y_ref,
                 w1_scr, w3_scr, w2_scr):
    del blk_e_ref
    j = pl.program_id(0)
    valid = blk_valid_ref[j]

    @pl.when(blk_first_ref[j] == 1)
    def _():
        w1_scr[...] = w1_ref[0].astype(BF16)
        w3_scr[...] = w3_ref[0].astype(BF16)
        w2_scr[...] = w2_ref[0].astype(BF16)

    @pl.when(valid > 0)
    def _():
        half = x_ref.shape[1]
        rows = lax.broadcasted_iota(I32, x_ref.shape, 0)
        lo, hi = _unpack_bf16_pair(jnp.where(rows < valid, x_ref[...], 0))
        lo, hi = lo.astype(BF16), hi.astype(BF16)

        def proj(w_scr):
            return (jnp.dot(lo, w_scr[0:half, :], preferred_element_type=F32)
                    + jnp.dot(hi, w_scr[half:2 * half, :], preferred_element_type=F32))

        hid = _silu(proj(w1_scr)) * proj(w3_scr)
        y = jnp.dot(hid.astype(BF16), w2_scr[...], preferred_element_type=F32)
        y_ref[...] = _pack_bf16_pair(y[:, :half], y[:, half:])

    @pl.when(valid == 0)
    def _():
        y_ref[...] = jnp.zeros(y_ref.shape, U32)


def _experts(xg, blk_e, blk_first, blk_valid, w1, w3, w2, layer):
    n_rows, half = xg.shape
    n_blocks = n_rows // EXPERT_ROWS
    d, de = w1.shape[-2:]
    grid_spec = pltpu.PrefetchScalarGridSpec(
        num_scalar_prefetch=3,
        grid=(n_blocks,),
        in_specs=[pl.BlockSpec((EXPERT_ROWS, half), lambda j, be, bf, bv: (j, 0)),
                  pl.BlockSpec((None, 1, d, de), lambda j, be, bf, bv: (layer, be[j], 0, 0)),
                  pl.BlockSpec((None, 1, d, de), lambda j, be, bf, bv: (layer, be[j], 0, 0)),
                  pl.BlockSpec((None, 1, de, d), lambda j, be, bf, bv: (layer, be[j], 0, 0))],
        out_specs=pl.BlockSpec((EXPERT_ROWS, half), lambda j, be, bf, bv: (j, 0)),
        scratch_shapes=[pltpu.VMEM((d, de), BF16), pltpu.VMEM((d, de), BF16),
                        pltpu.VMEM((de, d), BF16)],
    )
    return pl.pallas_call(
        _expert_body,
        grid_spec=grid_spec,
        out_shape=jax.ShapeDtypeStruct((n_rows, half), U32),
        compiler_params=_cparams(("arbitrary",)),
        name="moe_experts",
    )(blk_e, blk_first, blk_valid, xg, w1, w3, w2)


def _combine_body(h_ref, u_ref, yg_ref, wts_ref, gt_ref, s1_ref, s3_ref, s2_ref, fw_ref, o_ref, *,
                  final_norm):
    half = u_ref.shape[1]
    lo, hi = _unpack_bf16_pair(u_ref[...])
    lo, hi = lo.astype(BF16), hi.astype(BF16)

    def proj(w_ref):
        return (jnp.dot(lo, w_ref[0:half, :], preferred_element_type=F32)
                + jnp.dot(hi, w_ref[half:2 * half, :], preferred_element_type=F32))

    hid = _silu(proj(s1_ref)) * proj(s3_ref)
    shared = jnp.dot(hid.astype(BF16), s2_ref[...], preferred_element_type=F32)

    routed_lo = routed_hi = None
    for kk in range(TOP_K):
        y_lo, y_hi = _unpack_bf16_pair(yg_ref[kk])
        w = wts_ref[:, kk:kk + 1]
        routed_lo = y_lo * w if routed_lo is None else routed_lo + y_lo * w
        routed_hi = y_hi * w if routed_hi is None else routed_hi + y_hi * w
    gt = gt_ref[0]
    out_lo = h_ref[:, 0:half] + gt[:, 0:half] * (routed_lo + shared[:, 0:half])
    out_hi = (h_ref[:, half:2 * half]
              + gt[:, half:2 * half] * (routed_hi + shared[:, half:2 * half]))
    if final_norm:
        ssq = (jnp.sum(out_lo * out_lo, axis=-1, keepdims=True)
               + jnp.sum(out_hi * out_hi, axis=-1, keepdims=True))
        inv = lax.rsqrt(ssq / (2 * half) + EPS)
        out_lo = (out_lo * inv) * fw_ref[:, 0:half]
        out_hi = (out_hi * inv) * fw_ref[:, half:2 * half]
    o_ref[:, 0:half] = out_lo
    o_ref[:, half:2 * half] = out_hi


def _combine(h, u, yg, wts_tk, gt, s1, s3, s2, final_w, seq, final_norm, part, n_parts):
    t, d = h.shape
    half = d // 2
    tm = min(COMBINE_ROWS, seq)
    per_b = seq // tm
    ds_ = s1.shape[-1]
    steps = t // n_parts // tm
    off = part * steps
    return pl.pallas_call(
        functools.partial(_combine_body, final_norm=final_norm),
        grid=(steps,),
        in_specs=[pl.BlockSpec((tm, d), lambda i: (i + off, 0)),
                  pl.BlockSpec((tm, half), lambda i: (i + off, 0)),
                  pl.BlockSpec((TOP_K, tm, half), lambda i: (0, i, 0)),
                  pl.BlockSpec((tm, TOP_K), lambda i: (i + off, 0)),
                  pl.BlockSpec((1, 1, d), lambda i: ((i + off) // per_b, 0, 0)),
                  pl.BlockSpec((d, ds_), lambda i: (0, 0)),
                  pl.BlockSpec((d, ds_), lambda i: (0, 0)),
                  pl.BlockSpec((ds_, d), lambda i: (0, 0)),
                  pl.BlockSpec((1, d), lambda i: (0, 0))],
        out_specs=pl.BlockSpec((tm, d), lambda i: (i + off, 0)),
        out_shape=jax.ShapeDtypeStruct((t, d), F32),
        input_output_aliases={0: 0},
        compiler_params=_cparams(("parallel",)),
        name="moe_combine",
    )(h, u, yg, wts_tk, gt, s1, s3, s2, final_w)


def _split_w_in(w_in):
    sizes = (2 * M_HEADS * M_DQK, M_HEADS * M_DV, M_HEADS * M_DV, M_HEADS, M_HEADS,
             G_HEADS * G_DK, G_HEADS * G_DK, G_HEADS * G_DV, G_RANK, G_HEADS * G_DV,
             D_MODEL, D_MODEL)
    offs = [0]
    for n in sizes:
        offs.append(offs[-1] + n)
    w16 = w_in.astype(BF16)
    big = jnp.concatenate([w16[:, offs[0]:offs[3]], w16[:, offs[5]:offs[8]], w16[:, offs[9]:offs[12]]],
                          axis=1)
    pad = jnp.zeros((w_in.shape[0], SMALL_COLS - 2 * M_HEADS - G_RANK), BF16)
    small = jnp.concatenate([w16[:, offs[3]:offs[5]], w16[:, offs[8]:offs[9]], pad], axis=1)
    return big, small


def _moe_layout(counts, eidx, pos, n_blocks):
    padded = (counts + EXPERT_ROWS - 1) // EXPERT_ROWS * EXPERT_ROWS
    pend = jnp.cumsum(padded)
    pstart = pend - padded
    experts = jnp.arange(N_EXPERTS, dtype=I32)
    dest = pos + jnp.sum(jnp.where(eidx[..., None] == experts, pstart, 0), axis=-1)
    blk_start = jnp.arange(n_blocks, dtype=I32) * EXPERT_ROWS
    owner = jnp.sum((pend[None, :] <= blk_start[:, None]).astype(I32), axis=1)
    blk_e = jnp.minimum(owner, N_EXPERTS - 1)
    prev = jnp.concatenate([jnp.full((1,), -1, I32), blk_e[:-1]])
    blk_first = (blk_e != prev).astype(I32)
    own = blk_e[:, None] == experts[None, :]
    rows_left = jnp.sum(jnp.where(own, (pstart + counts)[None, :], 0), axis=1) - blk_start
    blk_valid = jnp.clip(jnp.where(owner < N_EXPERTS, rows_left, 0), 0, EXPERT_ROWS)
    return dest.astype(I32), blk_e.astype(I32), blk_first, blk_valid.astype(I32)


def kernel(x, c, ada_w, ada_b, norm_mix_w, norm_moe_w, w_in, m_conv_w, m_gate_b, m_norm_w,
           g_alpha_w, g_alpha_b, g_norm_w, w_pa, w_pb, w_o, router_w, router_b,
           exp_w1, exp_w3, exp_w2, sh_w1, sh_w3, sh_w2, final_norm_w):
    bsz, seq, d = x.shape
    depth = ada_w.shape[0]
    t = bsz * seq
    n_rows = t * TOP_K + N_EXPERTS * EXPERT_ROWS
    n_blocks = n_rows // EXPERT_ROWS

    ada = _ada(c, ada_w, ada_b).reshape(depth, bsz, 6, 1, d)
    h = x.reshape(t, d)
    for l in range(depth):
        sh1, sc1, gt1, sh2, sc2, gt2 = (ada[l, :, i] for i in range(6))

        w_big, w_small = _split_w_in(w_in[l])
        mix_w = norm_mix_w[l][None, :]
        gate_row = jnp.zeros((1, SMALL_COLS), F32)
        gate_row = gate_row.at[0, SMALL_I:SMALL_I + M_HEADS].set(m_gate_b[l, 0])
        gate_row = gate_row.at[0, SMALL_F:SMALL_F + M_HEADS].set(m_gate_b[l, 1])
        ha = _mlstm(h, sh1, sc1, mix_w, w_big, w_small, m_conv_w[l], gate_row, m_norm_w[l][None, :],
                    bsz, seq)
        alpha_full = jnp.zeros((SMALL_COLS, G_HEADS * G_DK), F32)
        alpha_full = alpha_full.at[SMALL_R:SMALL_R + G_RANK].set(g_alpha_w[l]).astype(BF16)
        hb = _gla(h, sh1, sc1, mix_w, w_big, w_small, alpha_full, g_alpha_b[l][None, :],
                  g_norm_w[l][None, :], bsz, seq)
        h = _mixout(h, sh1, sc1, mix_w, w_big, ha, hb, gt1, w_pa[l].astype(BF16),
                    w_pb[l].astype(BF16), w_o[l].astype(BF16), seq)

        u, eidx, wts, pos, cnt = _route(h, sh2, sc2, norm_moe_w[l][None, :],
                                        router_w[l].T, router_b[l][:, None], seq)
        dest, blk_e, blk_first, blk_valid = _moe_layout(cnt[:, 0], eidx, pos, n_blocks)
        xg = _sc_scatter_rows(u, dest, n_rows)
        y = _experts(xg, blk_e, blk_first, blk_valid, exp_w1, exp_w3, exp_w2, l)
        tp = t // COMBINE_PARTS
        for p in range(COMBINE_PARTS):
            dest_p = dest[:, p * tp:(p + 1) * tp].reshape(-1)
            yg = _sc_gather_rows(y, dest_p).reshape(TOP_K, tp, d // 2)
            h = _combine(h, u, yg, wts.T, gt2, sh_w1[l].astype(BF16), sh_w3[l].astype(BF16),
                         sh_w2[l].astype(BF16), final_norm_w[None, :], seq,
                         final_norm=(l == depth - 1), part=p, n_parts=COMBINE_PARTS)

    return h.reshape(bsz, seq, d)
```

```python
import functools

import jax
import jax.numpy as jnp
import numpy as np
from jax import lax
from jax.experimental import pallas as pl
from jax.experimental.pallas import tpu as pltpu
from jax.experimental.pallas import tpu_sc as plsc

F32 = jnp.float32
BF16 = jnp.bfloat16
I32 = jnp.int32
U32 = jnp.uint32
HI_MASK = np.uint32(0xFFFF0000)

LANES = 128

SC_CORES = 2
SC_SUBCORES = 16
SC_WORKERS = SC_CORES * SC_SUBCORES
SC_CHUNK = 64

D_MODEL = 1024
M_HEADS = 4
M_DQK = 128
M_DV = 256
M_CONV = 4
GATE_CAP = 15.0
G_HEADS = 4
G_DK = 128
G_DV = 256
G_RANK = 16
G_TAU = 16.0
G_CHUNK = 64
N_EXPERTS = 64
TOP_K = 8
N_GROUPS = 8
GROUP_SIZE = N_EXPERTS // N_GROUPS
TOPK_GROUPS = 4
ROUTED_SCALE = 2.5
EPS = 1e-6

M_CHUNK_ROWS = 256
G_BLOCK_ROWS = 256
SEQ_GROUP = 2
PROJ_PIECE_COLS = 256
MIX_ROWS = 512
ROUTE_ROWS = 1024
EXPERT_ROWS = 1024
COMBINE_ROWS = 512
COMBINE_PARTS = 4
CONV_HALO = 8
VMEM_LIMIT = 48 * 1024 * 1024

M_GROUP_COLS = 2 * M_HEADS * M_DQK + 2 * M_HEADS * M_DV
G_GROUP_COLS = 2 * G_HEADS * G_DK + 2 * G_HEADS * G_DV
GATE_GROUP_COLS = 2 * D_MODEL
COL_M_GROUP = 0
COL_G_GROUP = COL_M_GROUP + M_GROUP_COLS
COL_GATE_GROUP = COL_G_GROUP + G_GROUP_COLS
SMALL_COLS = LANES
SMALL_I, SMALL_F, SMALL_R = 0, M_HEADS, 2 * M_HEADS


def _cparams(sem, vmem=VMEM_LIMIT):
    return pltpu.CompilerParams(dimension_semantics=sem, vmem_limit_bytes=vmem)


def _silu(x):
    return x * jax.nn.sigmoid(x)


def _log_sigmoid(x):
    return jnp.minimum(x, 0.0) - jnp.log1p(jnp.exp(-jnp.abs(x)))


def _modulated_rmsnorm(x, w, sc, sh):
    y = x * lax.rsqrt(jnp.mean(x * x, axis=-1, keepdims=True) + EPS)
    return (y * w) * (1.0 + sc) + sh


def _pack_bf16_pair(lo, hi):
    lo_bits = lax.bitcast_convert_type(lo.astype(BF16).astype(F32), U32)
    hi_bits = lax.bitcast_convert_type(hi.astype(BF16).astype(F32), U32)
    return (lo_bits >> 16) | (hi_bits & HI_MASK)


def _unpack_bf16_pair(packed):
    lo = lax.bitcast_convert_type(packed << 16, F32)
    hi = lax.bitcast_convert_type(packed & HI_MASK, F32)
    return lo, hi


def _lower_tri(n, dtype):
    r = lax.broadcasted_iota(I32, (n, n), 0)
    c = lax.broadcasted_iota(I32, (n, n), 1)
    return (r >= c).astype(dtype)


def _split3_dot(lhs01, x):
    hi = x.astype(BF16)
    r1 = x - hi.astype(F32)
    mid = r1.astype(BF16)
    lo = (r1 - mid.astype(F32)).astype(BF16)
    return (jnp.dot(lhs01, hi, preferred_element_type=F32)
            + jnp.dot(lhs01, mid, preferred_element_type=F32)
            + jnp.dot(lhs01, lo, preferred_element_type=F32))


def _split2_dot_nt(a, b):
    nt = (((1,), (1,)), ((), ()))
    a_hi = a.astype(BF16)
    a_lo = (a - a_hi.astype(F32)).astype(BF16)
    b_hi = b.astype(BF16)
    b_lo = (b - b_hi.astype(F32)).astype(BF16)
    return (lax.dot_general(a_hi, b_hi, nt, preferred_element_type=F32)
            + lax.dot_general(a_hi, b_lo, nt, preferred_element_type=F32)
            + lax.dot_general(a_lo, b_hi, nt, preferred_element_type=F32))


def _ada_body(c_ref, w_ref, b_ref, o_ref):
    cond = _silu(c_ref[...])
    o_ref[0] = jnp.dot(cond.astype(BF16), w_ref[0].astype(BF16),
                       preferred_element_type=F32) + b_ref[0]


def _ada(c, ada_w, ada_b):
    depth, d, six_d = ada_w.shape
    bsz = c.shape[0]
    nj = six_d // d
    return pl.pallas_call(
        _ada_body,
        grid=(depth, nj),
        in_specs=[pl.BlockSpec((bsz, d), lambda l, j: (0, 0)),
                  pl.BlockSpec((1, d, d), lambda l, j: (l, 0, j)),
                  pl.BlockSpec((1, 1, d), lambda l, j: (l, 0, j))],
        out_specs=pl.BlockSpec((1, bsz, d), lambda l, j: (l, 0, j)),
        out_shape=jax.ShapeDtypeStruct((depth, bsz, six_d), F32),
        compiler_params=_cparams(("parallel", "parallel")),
        name="ada_ln",
    )(c, ada_w, ada_b.reshape(depth, 1, six_d))


def _mlstm_body(h0_ref, hn_ref, sh_ref, sc_ref, mw_ref, w_ref, ws_ref, cw_ref, gb_ref, nw_ref,
                ha_ref, qk_nxt, v_nxt, o_nxt, sm_nxt, xe_scr, v_scr, o_scr, sm_scr,
                c_scr, n_scr, m_scr):
    n_seq, rows = hn_ref.shape[0], hn_ref.shape[1]
    half = M_HEADS * M_DQK
    vcols = M_HEADS * M_DV

    def projection_pieces(h_ref):
        u = jnp.concatenate(
            [_modulated_rmsnorm(h_ref[g], mw_ref[...], sc_ref[g], sh_ref[g]) for g in range(n_seq)],
            axis=0).astype(BF16)

        def piece(dst, w_lo, lo, width):
            def run():
                dst[:, lo:lo + width] = jnp.dot(u, w_ref[:, w_lo + lo:w_lo + lo + width],
                                                preferred_element_type=F32).astype(dst.dtype)
            return run

        def small_piece():
            sm_nxt[...] = jnp.dot(u, ws_ref[...], preferred_element_type=F32)

        pieces = [small_piece]
        for dst, w_lo in ((qk_nxt, 0), (v_nxt, 2 * half), (o_nxt, 2 * half + vcols)):
            for lo in range(0, vcols, PROJ_PIECE_COLS):
                pieces.append(piece(dst, w_lo, lo, PROJ_PIECE_COLS))
        return pieces

    @pl.when(pl.program_id(1) == 0)
    def _():
        xe_scr[:, 0:CONV_HALO, :] = jnp.zeros((n_seq, CONV_HALO, 2 * half), F32)
        c_scr[...] = jnp.zeros(c_scr.shape, F32)
        n_scr[...] = jnp.zeros(n_scr.shape, F32)
        m_scr[...] = jnp.zeros(m_scr.shape, F32)
        for run in projection_pieces(h0_ref):
            run()

    for g in range(n_seq):
        xe_scr[g, CONV_HALO:CONV_HALO + rows, :] = qk_nxt[g * rows:(g + 1) * rows, :]
    v_scr[...] = v_nxt[...]
    o_scr[...] = o_nxt[...]
    sm_scr[...] = sm_nxt[...]
    pending = projection_pieces(hn_ref)

    r_io = lax.broadcasted_iota(I32, (rows, rows), 0)
    c_io = lax.broadcasted_iota(I32, (rows, rows), 1)
    causal = r_io >= c_io
    tri = _lower_tri(rows, BF16)

    for g in range(n_seq):
        gr = slice(g * rows, (g + 1) * rows)
        conv = None
        for j in range(M_CONV):
            off = CONV_HALO - (M_CONV - 1) + j
            term = xe_scr[g, off:off + rows, :] * cw_ref[j:j + 1, :]
            conv = term if conv is None else conv + term
        qk = _silu(conv)
        xe_scr[g, 0:CONV_HALO, :] = xe_scr[g, rows:rows + CONV_HALO, :]

        capped = GATE_CAP * jnp.tanh((sm_scr[gr, :] + gb_ref[...]) / GATE_CAP)
        li_all = capped
        lf_all = _log_sigmoid(capped)
        b_all = _split3_dot(tri, lf_all)
        li_t = li_all.T
        b_t = b_all.T

        for h in range(M_HEADS):
            if pending:
                pending.pop(0)()
            sidx = g * M_HEADS + h
            q = (qk[:, h * M_DQK:(h + 1) * M_DQK] * (M_DQK ** -0.5)).astype(BF16)
            k = qk[:, half + h * M_DQK:half + (h + 1) * M_DQK]
            kb = k.astype(BF16)
            v = v_scr[gr, h * M_DV:(h + 1) * M_DV]
            li_c = li_all[:, SMALL_I + h:SMALL_I + h + 1]
            b_c = b_all[:, SMALL_F + h:SMALL_F + h + 1]
            li_r = li_t[SMALL_I + h:SMALL_I + h + 1, :]
            b_r = b_t[SMALL_F + h:SMALL_F + h + 1, :]
            gsum = b_c[rows - 1:rows, :]
            m_prev = m_scr[sidx][:, 0:1]
            c_prev = c_scr[sidx]
            n_prev = n_scr[sidx]

            d_mat = jnp.where(causal, b_c - b_r + li_r, -jnp.inf)
            m_inter = b_c + m_prev
            m_t = jnp.maximum(jnp.max(d_mat, axis=1, keepdims=True), m_inter)
            s = lax.dot_general(q, kb, (((1,), (1,)), ((), ())), preferred_element_type=F32)
            p = jnp.exp(d_mat - m_t) * s
            w_inter = jnp.exp(m_inter - m_t)
            num = (jnp.dot(p.astype(BF16), v, preferred_element_type=F32)
                   + w_inter * jnp.dot(q, c_prev.astype(BF16), preferred_element_type=F32))
            qn = jnp.sum(q.astype(F32) * n_prev, axis=1, keepdims=True)
            den = jnp.sum(p, axis=1, keepdims=True) + w_inter * qn
            hh = num / jnp.maximum(jnp.abs(den), jnp.exp(-m_t))

            if pending:
                pending.pop(0)()

            a_r = gsum - b_r + li_r
            a_c = gsum - b_c + li_c
            m_new = jnp.maximum(gsum + m_prev, jnp.max(a_r, axis=1, keepdims=True))
            decay = jnp.exp(gsum + m_prev - m_new)
            wk = jnp.exp(a_c - m_new) * k
            c_scr[sidx] = decay * c_prev + lax.dot_general(
                wk.astype(BF16), v, (((0,), (0,)), ((), ())), preferred_element_type=F32)
            n_scr[sidx] = decay * n_prev + jnp.sum(wk, axis=0, keepdims=True)
            m_scr[sidx] = jnp.broadcast_to(m_new, m_scr.shape[1:])

            y = hh * lax.rsqrt(jnp.mean(hh * hh, axis=-1, keepdims=True) + EPS)
            y = y * nw_ref[:, h * M_DV:(h + 1) * M_DV]
            gate = jax.nn.sigmoid(o_scr[gr, h * M_DV:(h + 1) * M_DV])
            ha_ref[g, :, h * M_DV:(h + 1) * M_DV] = (y * gate).astype(BF16)
    for run in pending:
        run()


def _mlstm(h, sh, sc, mix_w, w_big, w_small, conv_w, gate_row, norm_w, bsz, seq):
    t, d = h.shape
    rows = min(M_CHUNK_ROWS, seq)
    nc = seq // rows
    ng = SEQ_GROUP if bsz % SEQ_GROUP == 0 else 1
    half = M_HEADS * M_DQK
    vcols = M_HEADS * M_DV
    wcols = M_GROUP_COLS
    assert COL_M_GROUP % wcols == 0
    resident = pl.Buffered(1)
    nxt = [pltpu.VMEM((ng * rows, 2 * half), F32), pltpu.VMEM((ng * rows, vcols), BF16),
           pltpu.VMEM((ng * rows, vcols), F32), pltpu.VMEM((ng * rows, SMALL_COLS), F32)]
    out = pl.pallas_call(
        _mlstm_body,
        grid=(bsz // ng, nc),
        in_specs=[pl.BlockSpec((ng, rows, d), lambda b, c: (b, 0, 0)),
                  pl.BlockSpec((ng, rows, d), lambda b, c: (b, jnp.minimum(c + 1, nc - 1), 0)),
                  pl.BlockSpec((ng, 1, d), lambda b, c: (b, 0, 0)),
                  pl.BlockSpec((ng, 1, d), lambda b, c: (b, 0, 0)),
                  pl.BlockSpec((1, d), lambda b, c: (0, 0)),
                  pl.BlockSpec((d, wcols), lambda b, c: (0, 0), pipeline_mode=resident),
                  pl.BlockSpec((d, SMALL_COLS), lambda b, c: (0, 0), pipeline_mode=resident),
                  pl.BlockSpec((M_CONV, 2 * half), lambda b, c: (0, 0)),
                  pl.BlockSpec((1, SMALL_COLS), lambda b, c: (0, 0)),
                  pl.BlockSpec((1, vcols), lambda b, c: (0, 0))],
        out_specs=pl.BlockSpec((ng, rows, vcols), lambda b, c: (b, c, 0)),
        out_shape=jax.ShapeDtypeStruct((bsz, seq, vcols), BF16),
        scratch_shapes=nxt + [pltpu.VMEM((ng, rows + CONV_HALO, 2 * half), F32)] + nxt[1:]
        + [pltpu.VMEM((ng * M_HEADS, M_DQK, M_DV), F32),
           pltpu.VMEM((ng * M_HEADS, 1, M_DQK), F32),
           pltpu.VMEM((ng * M_HEADS, 1, LANES), F32)],
        compiler_params=_cparams(("parallel", "arbitrary")),
        name="mlstm",
    )(h.reshape(bsz, seq, d), h.reshape(bsz, seq, d), sh, sc, mix_w, w_big, w_small, conv_w,
      gate_row, norm_w)
    return out.reshape(t, vcols)


def _gla_body(h0_ref, hn_ref, sh_ref, sc_ref, mw_ref, w_ref, ws_ref, aw_ref, ab_ref, nw_ref, hb_ref,
              qk_nxt, v_nxt, z_nxt, sm_nxt, qk_scr, v_scr, z_scr, sm_scr, st_scr):
    n_seq, rows = hn_ref.shape[0], hn_ref.shape[1]
    n_chunks = rows // G_CHUNK
    kcols = G_HEADS * G_DK
    vcols = G_HEADS * G_DV

    def projection_pieces(h_ref):
        u = jnp.concatenate(
            [_modulated_rmsnorm(h_ref[g], mw_ref[...], sc_ref[g], sh_ref[g]) for g in range(n_seq)],
            axis=0).astype(BF16)

        def piece(dst, w_lo, lo, width):
            def run():
                dst[:, lo:lo + width] = jnp.dot(u, w_ref[:, w_lo + lo:w_lo + lo + width],
                                                preferred_element_type=F32).astype(dst.dtype)
            return run

        def small_piece():
            sm_nxt[...] = jnp.dot(u, ws_ref[...], preferred_element_type=F32)

        pieces = [small_piece]
        for dst, w_lo in ((qk_nxt, 0), (v_nxt, 2 * kcols), (z_nxt, 2 * kcols + vcols)):
            for lo in range(0, vcols, PROJ_PIECE_COLS):
                pieces.append(piece(dst, w_lo, lo, PROJ_PIECE_COLS))
        return pieces

    @pl.when(pl.program_id(1) == 0)
    def _():
        st_scr[...] = jnp.zeros(st_scr.shape, F32)
        for run in projection_pieces(h0_ref):
            run()

    qk_scr[...] = qk_nxt[...]
    v_scr[...] = v_nxt[...]
    z_scr[...] = z_nxt[...]
    sm_scr[...] = sm_nxt[...]
    pending = projection_pieces(hn_ref)

    r_io = lax.broadcasted_iota(I32, (rows, rows), 0)
    c_io = lax.broadcasted_iota(I32, (rows, rows), 1)
    chunk_causal = jnp.logical_and(r_io >= c_io, r_io // G_CHUNK == c_io // G_CHUNK)
    chunk_tri = jnp.where(chunk_causal, 1.0, 0.0).astype(BF16)

    for g in range(n_seq):
        gr = slice(g * rows, (g + 1) * rows)
        for _ in range((len(pending) + n_seq - 1 - g) // (n_seq - g)):
            pending.pop(0)()
        q = qk_scr[gr, 0:kcols] * (G_DK ** -0.5)
        k = qk_scr[gr, kcols:2 * kcols]

        logits = jnp.dot(sm_scr[gr, :].astype(BF16), aw_ref[...], preferred_element_type=F32)
        la = _log_sigmoid(logits + ab_ref[...]) / G_TAU
        bc = _split3_dot(chunk_tri, la)
        gcs = [bc[(ci + 1) * G_CHUNK - 1:(ci + 1) * G_CHUNK, :] for ci in range(n_chunks)]
        gc_rows = jnp.concatenate([jnp.broadcast_to(gc, (G_CHUNK, kcols)) for gc in gcs], axis=0)

        q_in = (q * jnp.exp(bc)).astype(BF16)
        k_in = (k * jnp.exp(-bc)).astype(BF16)
        k_out = (k * jnp.exp(gc_rows - bc)).astype(BF16)

        for h in range(G_HEADS):
            ks = slice(h * G_DK, (h + 1) * G_DK)
            vs = slice(h * G_DV, (h + 1) * G_DV)
            v = v_scr[gr, vs]
            att = lax.dot_general(q_in[:, ks], k_in[:, ks], (((1,), (1,)), ((), ())),
                                  preferred_element_type=F32)
            att = jnp.where(chunk_causal, att, 0.0).astype(BF16)
            o_intra = jnp.dot(att, v, preferred_element_type=F32)
            st = st_scr[g * G_HEADS + h]
            outs = []
            for ci in range(n_chunks):
                rs = slice(ci * G_CHUNK, (ci + 1) * G_CHUNK)
                o_inter = lax.dot_general(q_in[rs, ks], st.astype(BF16),
                                          (((1,), (1,)), ((), ())), preferred_element_type=F32)
                outs.append(o_intra[rs, :] + o_inter)
                st = jnp.exp(gcs[ci][:, ks]) * st + lax.dot_general(
                    v[rs, :], k_out[rs, ks], (((0,), (0,)), ((), ())), preferred_element_type=F32)
            st_scr[g * G_HEADS + h] = st
            o = jnp.concatenate(outs, axis=0)
            y = o * lax.rsqrt(jnp.mean(o * o, axis=-1, keepdims=True) + EPS)
            y = y * nw_ref[:, vs]
            hb_ref[g, :, vs] = (y * _silu(z_scr[gr, vs])).astype(BF16)


def _gla(h, sh, sc, mix_w, w_big, w_small, alpha_full, alpha_b, norm_w, bsz, seq):
    t, d = h.shape
    rows = min(G_BLOCK_ROWS, seq)
    nb = seq // rows
    ng = SEQ_GROUP if bsz % SEQ_GROUP == 0 else 1
    kcols = G_HEADS * G_DK
    vcols = G_HEADS * G_DV
    wcols = G_GROUP_COLS
    assert COL_G_GROUP % wcols == 0
    resident = pl.Buffered(1)
    nxt = [pltpu.VMEM((ng * rows, 2 * kcols), F32), pltpu.VMEM((ng * rows, vcols), BF16),
           pltpu.VMEM((ng * rows, vcols), F32), pltpu.VMEM((ng * rows, SMALL_COLS), F32)]
    out = pl.pallas_call(
        _gla_body,
        grid=(bsz // ng, nb),
        in_specs=[pl.BlockSpec((ng, rows, d), lambda b, c: (b, 0, 0)),
                  pl.BlockSpec((ng, rows, d), lambda b, c: (b, jnp.minimum(c + 1, nb - 1), 0)),
                  pl.BlockSpec((ng, 1, d), lambda b, c: (b, 0, 0)),
                  pl.BlockSpec((ng, 1, d), lambda b, c: (b, 0, 0)),
                  pl.BlockSpec((1, d), lambda b, c: (0, 0)),
                  pl.BlockSpec((d, wcols), lambda b, c: (0, COL_G_GROUP // wcols),
                               pipeline_mode=resident),
                  pl.BlockSpec((d, SMALL_COLS), lambda b, c: (0, 0), pipeline_mode=resident),
                  pl.BlockSpec((SMALL_COLS, kcols), lambda b, c: (0, 0)),
                  pl.BlockSpec((1, kcols), lambda b, c: (0, 0)),
                  pl.BlockSpec((1, vcols), lambda b, c: (0, 0))],
        out_specs=pl.BlockSpec((ng, rows, vcols), lambda b, c: (b, c, 0)),
        out_shape=jax.ShapeDtypeStruct((bsz, seq, vcols), BF16),
        scratch_shapes=nxt + nxt + [pltpu.VMEM((ng * G_HEADS, G_DV, G_DK), F32)],
        compiler_params=_cparams(("parallel", "arbitrary")),
        name="gla",
    )(h.reshape(bsz, seq, d), h.reshape(bsz, seq, d), sh, sc, mix_w, w_big, w_small, alpha_full,
      alpha_b, norm_w)
    return out.reshape(t, vcols)


def _mixout_body(h_ref, sh_ref, sc_ref, mw_ref, wg_ref, ha_ref, hb_ref, gt_ref, wpa_ref, wpb_ref,
                 wo_ref, o_ref):
    d = h_ref.shape[1]
    h = h_ref[...]
    u = _modulated_rmsnorm(h, mw_ref[...], sc_ref[0], sh_ref[0]).astype(BF16)
    ga = jnp.dot(u, wg_ref[:, 0:d], preferred_element_type=F32)
    gb = jnp.dot(u, wg_ref[:, d:2 * d], preferred_element_type=F32)
    a = jnp.dot(ha_ref[...], wpa_ref[...], preferred_element_type=F32)
    b = jnp.dot(hb_ref[...], wpb_ref[...], preferred_element_type=F32)
    y = jax.nn.sigmoid(ga) * a + jax.nn.sigmoid(gb) * b
    o_ref[...] = h + gt_ref[0] * jnp.dot(y.astype(BF16), wo_ref[...], preferred_element_type=F32)


def _mixout(h, sh, sc, mix_w, w_big, ha, hb, gt, w_pa, w_pb, w_o, seq):
    t, d = h.shape
    tm = min(MIX_ROWS, seq)
    per_b = seq // tm
    assert GATE_GROUP_COLS == 2 * d and COL_GATE_GROUP % GATE_GROUP_COLS == 0
    resident = pl.Buffered(1)
    wspec = pl.BlockSpec((d, d), lambda i: (0, 0), pipeline_mode=resident)
    bspec = pl.BlockSpec((1, 1, d), lambda i: (i // per_b, 0, 0))
    return pl.pallas_call(
        _mixout_body,
        grid=(t // tm,),
        in_specs=[pl.BlockSpec((tm, d), lambda i: (i, 0)),
                  bspec, bspec,
                  pl.BlockSpec((1, d), lambda i: (0, 0)),
                  pl.BlockSpec((d, 2 * d), lambda i: (0, COL_GATE_GROUP // GATE_GROUP_COLS),
                               pipeline_mode=resident),
                  pl.BlockSpec((tm, d), lambda i: (i, 0)),
                  pl.BlockSpec((tm, d), lambda i: (i, 0)),
                  bspec,
                  wspec, wspec, wspec],
        out_specs=pl.BlockSpec((tm, d), lambda i: (i, 0)),
        out_shape=jax.ShapeDtypeStruct((t, d), F32),
        compiler_params=_cparams(("parallel",)),
        name="mix_out",
    )(h, sh, sc, mix_w, w_big, ha, hb, gt, w_pa, w_pb, w_o)


def _route_body(h_ref, sh_ref, sc_ref, nw_ref, rwt_ref, rb_ref,
                u_ref, eidx_ref, wts_ref, pos_ref, cnt_ref, carry_scr):
    tm = h_ref.shape[0]

    @pl.when(pl.program_id(0) == 0)
    def _():
        carry_scr[...] = jnp.zeros(carry_scr.shape, F32)

    u = _modulated_rmsnorm(h_ref[...], nw_ref[...], sc_ref[0], sh_ref[0])
    half = u.shape[1] // 2
    u_ref[...] = _pack_bf16_pair(u[:, :half], u[:, half:])
    logits = _split2_dot_nt(rwt_ref[...], u)
    scores = jax.nn.sigmoid(logits)
    sel = scores + rb_ref[...]

    neg = -jnp.inf
    sub_io = lax.broadcasted_iota(I32, (GROUP_SIZE, tm), 0)
    pieces = []
    for g in range(N_GROUPS):
        blk = sel[g * GROUP_SIZE:(g + 1) * GROUP_SIZE, :]
        m1 = jnp.max(blk, axis=0, keepdims=True)
        first = jnp.min(jnp.where(blk == m1, sub_io, GROUP_SIZE), axis=0, keepdims=True)
        m2 = jnp.max(jnp.where(sub_io == first, neg, blk), axis=0, keepdims=True)
        pieces.append(jnp.broadcast_to(m1 + m2, (GROUP_SIZE, tm)))
    gscore = jnp.concatenate(pieces, axis=0)

    e_io = lax.broadcasted_iota(I32, (N_EXPERTS, tm), 0)
    grp_io = e_io // GROUP_SIZE
    gmask = jnp.zeros((N_EXPERTS, tm), jnp.bool_)
    for _ in range(TOPK_GROUPS):
        mx = jnp.max(gscore, axis=0, keepdims=True)
        gi = jnp.min(jnp.where(gscore == mx, grp_io, N_GROUPS), axis=0, keepdims=True)
        hit = grp_io == gi
        gmask = jnp.logical_or(gmask, hit)
        gscore = jnp.where(hit, neg, gscore)

    cur = jnp.where(gmask, sel, neg)
    row_io = lax.broadcasted_iota(I32, (TOP_K, tm), 0)
    eidx = jnp.zeros((TOP_K, tm), I32)
    wraw = jnp.zeros((TOP_K, tm), F32)
    chosen = jnp.zeros((N_EXPERTS, tm), jnp.bool_)
    hits = []
    for kk in range(TOP_K):
        mx = jnp.max(cur, axis=0, keepdims=True)
        ei = jnp.min(jnp.where(cur == mx, e_io, N_EXPERTS), axis=0, keepdims=True)
        hit = e_io == ei
        hits.append(hit)
        sc_k = jnp.sum(jnp.where(hit, scores, 0.0), axis=0, keepdims=True)
        eidx = jnp.where(row_io == kk, ei, eidx)
        wraw = jnp.where(row_io == kk, sc_k, wraw)
        chosen = jnp.logical_or(chosen, hit)
        cur = jnp.where(hit, neg, cur)

    wsum = jnp.sum(wraw, axis=0, keepdims=True)
    wts_ref[...] = wraw / wsum * ROUTED_SCALE
    eidx_ref[...] = eidx

    chosen_f = jnp.where(chosen, 1.0, 0.0)
    r_io = lax.broadcasted_iota(I32, (tm, tm), 0)
    c_io = lax.broadcasted_iota(I32, (tm, tm), 1)
    strict_upper = jnp.where(r_io < c_io, 1.0, 0.0).astype(BF16)
    prefix = jnp.dot(chosen_f.astype(BF16), strict_upper, preferred_element_type=F32)
    rank = prefix + carry_scr[:, 0:1]
    pos = jnp.zeros((TOP_K, tm), F32)
    for kk in range(TOP_K):
        p_k = jnp.sum(jnp.where(hits[kk], rank, 0.0), axis=0, keepdims=True)
        pos = jnp.where(row_io == kk, p_k, pos)
    pos_ref[...] = pos.astype(I32)
    total = carry_scr[...] + jnp.sum(chosen_f, axis=1, keepdims=True)
    carry_scr[...] = total
    cnt_ref[...] = total.astype(I32)


def _route(h, sh, sc, nw, rw_t, rb_col, seq):
    t, d = h.shape
    tm = min(ROUTE_ROWS, seq)
    per_b = seq // tm
    kspec = pl.BlockSpec((TOP_K, tm), lambda i: (0, i))
    return pl.pallas_call(
        _route_body,
        grid=(t // tm,),
        in_specs=[pl.BlockSpec((tm, d), lambda i: (i, 0)),
                  pl.BlockSpec((1, 1, d), lambda i: (i // per_b, 0, 0)),
                  pl.BlockSpec((1, 1, d), lambda i: (i // per_b, 0, 0)),
                  pl.BlockSpec((1, d), lambda i: (0, 0)),
                  pl.BlockSpec((N_EXPERTS, d), lambda i: (0, 0)),
                  pl.BlockSpec((N_EXPERTS, 1), lambda i: (0, 0))],
        out_specs=[pl.BlockSpec((tm, d // 2), lambda i: (i, 0)), kspec, kspec, kspec,
                   pl.BlockSpec((N_EXPERTS, LANES), lambda i: (0, 0))],
        out_shape=[jax.ShapeDtypeStruct((t, d // 2), U32),
                   jax.ShapeDtypeStruct((TOP_K, t), I32),
                   jax.ShapeDtypeStruct((TOP_K, t), F32),
                   jax.ShapeDtypeStruct((TOP_K, t), I32),
                   jax.ShapeDtypeStruct((N_EXPERTS, LANES), I32)],
        scratch_shapes=[pltpu.VMEM((N_EXPERTS, LANES), F32)],
        compiler_params=_cparams(("arbitrary",)),
        name="moe_route",
    )(h, sh, sc, nw, rw_t, rb_col)


def _sc_worker_id():
    return lax.axis_index("s") * SC_CORES + lax.axis_index("c")


def _sc_scatter_rows(x, dest, n_rows):
    t, d = x.shape
    n_k = dest.shape[0]
    assert t % (SC_WORKERS * 2 * SC_CHUNK) == 0
    per_w = t // SC_WORKERS
    n_ch = per_w // SC_CHUNK
    dest4 = dest.reshape(n_k, SC_WORKERS, n_ch, SC_CHUNK).transpose(1, 2, 0, 3)
    mesh = plsc.VectorSubcoreMesh(core_axis_name="c", subcore_axis_name="s")

    @functools.partial(
        pl.kernel, mesh=mesh,
        out_type=jax.ShapeDtypeStruct((n_rows, d), x.dtype),
        scratch_types=[pltpu.VMEM((n_ch, n_k, SC_CHUNK), I32),
                       pltpu.VMEM((SC_CHUNK, d), x.dtype),
                       pltpu.VMEM((SC_CHUNK, d), x.dtype)] + [pltpu.SemaphoreType.DMA] * 4,
        name="moe_dispatch_sc",
    )
    def scatter_kernel(x_hbm, dest_hbm, out_hbm, idx_v, rows0, rows1, l0, l1, s0, s1):
        wid = _sc_worker_id()
        base = wid * per_w
        pltpu.sync_copy(dest_hbm.at[wid], idx_v)
        bufs = ((rows0, l0, s0), (rows1, l1, s1))

        def load(ci, b):
            rows, load_sem, _ = bufs[b]
            return pltpu.make_async_copy(x_hbm.at[pl.ds(base + ci * SC_CHUNK, SC_CHUNK)], rows,
                                         load_sem)

        def scatters(ci, b):
            rows, _, scatter_sem = bufs[b]
            return [pltpu.make_async_copy(rows, out_hbm.at[idx_v.at[ci, j]], scatter_sem)
                    for j in range(n_k)]

        load(0, 0).start()

        @pl.loop(0, n_ch, step=2)
        def _(ci):
            load(ci, 0).wait()
            for cp in scatters(ci, 0):
                cp.start()

            @pl.when(ci > 0)
            def _():
                for cp in scatters(ci - 1, 1):
                    cp.wait()

            load(ci + 1, 1).start()
            load(ci + 1, 1).wait()
            for cp in scatters(ci + 1, 1):
                cp.start()
            for cp in scatters(ci, 0):
                cp.wait()

            @pl.when(ci + 2 < n_ch)
            def _():
                load(ci + 2, 0).start()

        for cp in scatters(n_ch - 1, 1):
            cp.wait()

    return scatter_kernel(x, dest4)


def _sc_gather_rows(table, idx):
    n, d = idx.shape[0], table.shape[1]
    assert n % (SC_WORKERS * 2 * SC_CHUNK) == 0
    per_w = n // SC_WORKERS
    n_ch = per_w // SC_CHUNK
    idx3 = idx.reshape(SC_WORKERS, n_ch, SC_CHUNK)
    mesh = plsc.VectorSubcoreMesh(core_axis_name="c", subcore_axis_name="s")

    @functools.partial(
        pl.kernel, mesh=mesh,
        out_type=jax.ShapeDtypeStruct((n, d), table.dtype),
        scratch_types=[pltpu.VMEM((n_ch, SC_CHUNK), I32),
                       pltpu.VMEM((SC_CHUNK, d), table.dtype),
                       pltpu.VMEM((SC_CHUNK, d), table.dtype)] + [pltpu.SemaphoreType.DMA] * 4,
        name="moe_combine_sc",
    )
    def gather_kernel(table_hbm, idx_hbm, out_hbm, idx_v, rows0, rows1, g0, g1, w0, w1):
        wid = _sc_worker_id()
        base = wid * per_w
        pltpu.sync_copy(idx_hbm.at[wid], idx_v)
        bufs = ((rows0, g0, w0), (rows1, g1, w1))

        def gather(ci, b):
            rows, gather_sem, _ = bufs[b]
            return pltpu.make_async_copy(table_hbm.at[idx_v.at[ci]], rows, gather_sem)

        def write(ci, b):
            rows, _, write_sem = bufs[b]
            return pltpu.make_async_copy(rows, out_hbm.at[pl.ds(base + ci * SC_CHUNK, SC_CHUNK)],
                                         write_sem)

        gather(0, 0).start()

        @pl.loop(0, n_ch, step=2)
        def _(ci):
            gather(ci, 0).wait()
            write(ci, 0).start()

            @pl.when(ci > 0)
            def _():
                write(ci - 1, 1).wait()

            gather(ci + 1, 1).start()
            gather(ci + 1, 1).wait()
            write(ci + 1, 1).start()
            write(ci, 0).wait()

            @pl.when(ci + 2 < n_ch)
            def _():
                gather(ci + 2, 0).start()

        write(n_ch - 1, 1).wait()

    return gather_kernel(table, idx3)


def _expert_body(blk_e_ref, blk_first_ref, blk_valid_ref, x_ref, w1_ref, w3_ref, w2_ref, y_ref,
                 w1_scr, w3_scr, w2_scr):
    del blk_e_ref
    j = pl.program_id(0)
    valid = blk_valid_ref[j]

    @pl.when(blk_first_ref[j] == 1)
    def _():
        w1_scr[...] = w1_ref[0].astype(BF16)
        w3_scr[...] = w3_ref[0].astype(BF16)
        w2_scr[...] = w2_ref[0].astype(BF16)

    @pl.when(valid > 0)
    def _():
        half = x_ref.shape[1]
        rows = lax.broadcasted_iota(I32, x_ref.shape, 0)
        lo, hi = _unpack_bf16_pair(jnp.where(rows < valid, x_ref[...], 0))
        lo, hi = lo.astype(BF16), hi.astype(BF16)

        def proj(w_scr):
            return (jnp.dot(lo, w_scr[0:half, :], preferred_element_type=F32)
                    + jnp.dot(hi, w_scr[half:2 * half, :], preferred_element_type=F32))

        hid = _silu(proj(w1_scr)) * proj(w3_scr)
        y = jnp.dot(hid.astype(BF16), w2_scr[...], preferred_element_type=F32)
        y_ref[...] = _pack_bf16_pair(y[:, :half], y[:, half:])

    @pl.when(valid == 0)
    def _():
        y_ref[...] = jnp.zeros(y_ref.shape, U32)


def _experts(xg, blk_e, blk_first, blk_valid, w1, w3, w2, layer):
    n_rows, half = xg.shape
    n_blocks = n_rows // EXPERT_ROWS
    d, de = w1.shape[-2:]
    grid_spec = pltpu.PrefetchScalarGridSpec(
        num_scalar_prefetch=3,
        grid=(n_blocks,),
        in_specs=[pl.BlockSpec((EXPERT_ROWS, half), lambda j, be, bf, bv: (j, 0)),
                  pl.BlockSpec((None, 1, d, de), lambda j, be, bf, bv: (layer, be[j], 0, 0)),
                  pl.BlockSpec((None, 1, d, de), lambda j, be, bf, bv: (layer, be[j], 0, 0)),
                  pl.BlockSpec((None, 1, de, d), lambda j, be, bf, bv: (layer, be[j], 0, 0))],
        out_specs=pl.BlockSpec((EXPERT_ROWS, half), lambda j, be, bf, bv: (j, 0)),
        scratch_shapes=[pltpu.VMEM((d, de), BF16), pltpu.VMEM((d, de), BF16),
                        pltpu.VMEM((de, d), BF16)],
    )
    return pl.pallas_call(
        _expert_body,
        grid_spec=grid_spec,
        out_shape=jax.ShapeDtypeStruct((n_rows, half), U32),
        compiler_params=_cparams(("arbitrary",)),
        name="moe_experts",
    )(blk_e, blk_first, blk_valid, xg, w1, w3, w2)


def _combine_body(h_ref, u_ref, yg_ref, wts_ref, gt_ref, s1_ref, s3_ref, s2_ref, fw_ref, o_ref, *,
                  final_norm):
    half = u_ref.shape[1]
    lo, hi = _unpack_bf16_pair(u_ref[...])
    lo, hi = lo.astype(BF16), hi.astype(BF16)

    def proj(w_ref):
        return (jnp.dot(lo, w_ref[0:half, :], preferred_element_type=F32)
                + jnp.dot(hi, w_ref[half:2 * half, :], preferred_element_type=F32))

    hid = _silu(proj(s1_ref)) * proj(s3_ref)
    shared = jnp.dot(hid.astype(BF16), s2_ref[...], preferred_element_type=F32)

    routed_lo = routed_hi = None
    for kk in range(TOP_K):
        y_lo, y_hi = _unpack_bf16_pair(yg_ref[kk])
        w = wts_ref[:, kk:kk + 1]
        routed_lo = y_lo * w if routed_lo is None else routed_lo + y_lo * w
        routed_hi = y_hi * w if routed_hi is None else routed_hi + y_hi * w
    gt = gt_ref[0]
    out_lo = h_ref[:, 0:half] + gt[:, 0:half] * (routed_lo + shared[:, 0:half])
    out_hi = (h_ref[:, half:2 * half]
              + gt[:, half:2 * half] * (routed_hi + shared[:, half:2 * half]))
    if final_norm:
        ssq = (jnp.sum(out_lo * out_lo, axis=-1, keepdims=True)
               + jnp.sum(out_hi * out_hi, axis=-1, keepdims=True))
        inv = lax.rsqrt(ssq / (2 * half) + EPS)
        out_lo = (out_lo * inv) * fw_ref[:, 0:half]
        out_hi = (out_hi * inv) * fw_ref[:, half:2 * half]
    o_ref[:, 0:half] = out_lo
    o_ref[:, half:2 * half] = out_hi


def _combine(h, u, yg, wts_tk, gt, s1, s3, s2, final_w, seq, final_norm, part, n_parts):
    t, d = h.shape
    half = d // 2
    tm = min(COMBINE_ROWS, seq)
    per_b = seq // tm
    ds_ = s1.shape[-1]
    steps = t // n_parts // tm
    off = part * steps
    return pl.pallas_call(
        functools.partial(_combine_body, final_norm=final_norm),
        grid=(steps,),
        in_specs=[pl.BlockSpec((tm, d), lambda i: (i + off, 0)),
                  pl.BlockSpec((tm, half), lambda i: (i + off, 0)),
                  pl.BlockSpec((TOP_K, tm, half), lambda i: (0, i, 0)),
                  pl.BlockSpec((tm, TOP_K), lambda i: (i + off, 0)),
                  pl.BlockSpec((1, 1, d), lambda i: ((i + off) // per_b, 0, 0)),
                  pl.BlockSpec((d, ds_), lambda i: (0, 0)),
                  pl.BlockSpec((d, ds_), lambda i: (0, 0)),
                  pl.BlockSpec((ds_, d), lambda i: (0, 0)),
                  pl.BlockSpec((1, d), lambda i: (0, 0))],
        out_specs=pl.BlockSpec((tm, d), lambda i: (i + off, 0)),
        out_shape=jax.ShapeDtypeStruct((t, d), F32),
        input_output_aliases={0: 0},
        compiler_params=_cparams(("parallel",)),
        name="moe_combine",
    )(h, u, yg, wts_tk, gt, s1, s3, s2, final_w)


def _split_w_in(w_in):
    sizes = (2 * M_HEADS * M_DQK, M_HEADS * M_DV, M_HEADS * M_DV, M_HEADS, M_HEADS,
             G_HEADS * G_DK, G_HEADS * G_DK, G_HEADS * G_DV, G_RANK, G_HEADS * G_DV,
             D_MODEL, D_MODEL)
    offs = [0]
    for n in sizes:
        offs.append(offs[-1] + n)
    w16 = w_in.astype(BF16)
    big = jnp.concatenate([w16[:, offs[0]:offs[3]], w16[:, offs[5]:offs[8]], w16[:, offs[9]:offs[12]]],
                          axis=1)
    pad = jnp.zeros((w_in.shape[0], SMALL_COLS - 2 * M_HEADS - G_RANK), BF16)
    small = jnp.concatenate([w16[:, offs[3]:offs[5]], w16[:, offs[8]:offs[9]], pad], axis=1)
    return big, small


def _moe_layout(counts, eidx, pos, n_blocks):
    padded = (counts + EXPERT_ROWS - 1) // EXPERT_ROWS * EXPERT_ROWS
    pend = jnp.cumsum(padded)
    pstart = pend - padded
    experts = jnp.arange(N_EXPERTS, dtype=I32)
    dest = pos + jnp.sum(jnp.where(eidx[..., None] == experts, pstart, 0), axis=-1)
    blk_start = jnp.arange(n_blocks, dtype=I32) * EXPERT_ROWS
    owner = jnp.sum((pend[None, :] <= blk_start[:, None]).astype(I32), axis=1)
    blk_e = jnp.minimum(owner, N_EXPERTS - 1)
    prev = jnp.concatenate([jnp.full((1,), -1, I32), blk_e[:-1]])
    blk_first = (blk_e != prev).astype(I32)
    own = blk_e[:, None] == experts[None, :]
    rows_left = jnp.sum(jnp.where(own, (pstart + counts)[None, :], 0), axis=1) - blk_start
    blk_valid = jnp.clip(jnp.where(owner < N_EXPERTS, rows_left, 0), 0, EXPERT_ROWS)
    return dest.astype(I32), blk_e.astype(I32), blk_first, blk_valid.astype(I32)


def kernel(x, c, ada_w, ada_b, norm_mix_w, norm_moe_w, w_in, m_conv_w, m_gate_b, m_norm_w,
           g_alpha_w, g_alpha_b, g_norm_w, w_pa, w_pb, w_o, router_w, router_b,
           exp_w1, exp_w3, exp_w2, sh_w1, sh_w3, sh_w2, final_norm_w):
    bsz, seq, d = x.shape
    depth = ada_w.shape[0]
    t = bsz * seq
    n_rows = t * TOP_K + N_EXPERTS * EXPERT_ROWS
    n_blocks = n_rows // EXPERT_ROWS

    ada = _ada(c, ada_w, ada_b).reshape(depth, bsz, 6, 1, d)
    h = x.reshape(t, d)
    for l in range(depth):
        sh1, sc1, gt1, sh2, sc2, gt2 = (ada[l, :, i] for i in range(6))

        w_big, w_small = _split_w_in(w_in[l])
        mix_w = norm_mix_w[l][None, :]
        gate_row = jnp.zeros((1, SMALL_COLS), F32)
        gate_row = gate_row.at[0, SMALL_I:SMALL_I + M_HEADS].set(m_gate_b[l, 0])
        gate_row = gate_row.at[0, SMALL_F:SMALL_F + M_HEADS].set(m_gate_b[l, 1])
        ha = _mlstm(h, sh1, sc1, mix_w, w_big, w_small, m_conv_w[l], gate_row, m_norm_w[l][None, :],
                    bsz, seq)
        alpha_full = jnp.zeros((SMALL_COLS, G_HEADS * G_DK), F32)
        alpha_full = alpha_full.at[SMALL_R:SMALL_R + G_RANK].set(g_alpha_w[l]).astype(BF16)
        hb = _gla(h, sh1, sc1, mix_w, w_big, w_small, alpha_full, g_alpha_b[l][None, :],
                  g_norm_w[l][None, :], bsz, seq)
        h = _mixout(h, sh1, sc1, mix_w, w_big, ha, hb, gt1, w_pa[l].astype(BF16),
                    w_pb[l].astype(BF16), w_o[l].astype(BF16), seq)

        u, eidx, wts, pos, cnt = _route(h, sh2, sc2, norm_moe_w[l][None, :],
                                        router_w[l].T, router_b[l][:, None], seq)
        dest, blk_e, blk_first, blk_valid = _moe_layout(cnt[:, 0], eidx, pos, n_blocks)
        xg = _sc_scatter_rows(u, dest, n_rows)
        y = _experts(xg, blk_e, blk_first, blk_valid, exp_w1, exp_w3, exp_w2, l)
        tp = t // COMBINE_PARTS
        for p in range(COMBINE_PARTS):
            dest_p = dest[:, p * tp:(p + 1) * tp].reshape(-1)
            yg = _sc_gather_rows(y, dest_p).reshape(TOP_K, tp, d // 2)
            h = _combine(h, u, yg, wts.T, gt2, sh_w1[l].astype(BF16), sh_w3[l].astype(BF16),
                         sh_w2[l].astype(BF16), final_norm_w[None, :], seq,
                         final_norm=(l == depth - 1), part=p, n_parts=COMBINE_PARTS)

    return h.reshape(bsz, seq, d)
```

```python
import functools

import jax
import jax.numpy as jnp
import numpy as np
from jax import lax
from jax.experimental import pallas as pl
from jax.experimental.pallas import tpu as pltpu
from jax.experimental.pallas import tpu_sc as plsc

F32 = jnp.float32
BF16 = jnp.bfloat16
I32 = jnp.int32
U32 = jnp.uint32
HI_MASK = np.uint32(0xFFFF0000)

LANES = 128

SC_CORES = 2
SC_SUBCORES = 16
SC_WORKERS = SC_CORES * SC_SUBCORES
SC_CHUNK = 64

D_MODEL = 1024
M_HEADS = 4
M_DQK = 128
M_DV = 256
M_CONV = 4
GATE_CAP = 15.0
G_HEADS = 4
G_DK = 128
G_DV = 256
G_RANK = 16
G_TAU = 16.0
G_CHUNK = 64
N_EXPERTS = 64
TOP_K = 8
N_GROUPS = 8
GROUP_SIZE = N_EXPERTS // N_GROUPS
TOPK_GROUPS = 4
ROUTED_SCALE = 2.5
EPS = 1e-6

M_CHUNK_ROWS = 256
G_BLOCK_ROWS = 256
SEQ_GROUP = 2
PROJ_PIECE_COLS = 256
MIX_ROWS = 512
ROUTE_ROWS = 1024
EXPERT_ROWS = 512
COMBINE_ROWS = 512
COMBINE_PARTS = 4
CONV_HALO = 8
VMEM_LIMIT = 48 * 1024 * 1024

M_GROUP_COLS = 2 * M_HEADS * M_DQK + 2 * M_HEADS * M_DV
G_GROUP_COLS = 2 * G_HEADS * G_DK + 2 * G_HEADS * G_DV
GATE_GROUP_COLS = 2 * D_MODEL
COL_M_GROUP = 0
COL_G_GROUP = COL_M_GROUP + M_GROUP_COLS
COL_GATE_GROUP = COL_G_GROUP + G_GROUP_COLS
SMALL_COLS = LANES
SMALL_I, SMALL_F, SMALL_R = 0, M_HEADS, 2 * M_HEADS


def _cparams(sem, vmem=VMEM_LIMIT):
    return pltpu.CompilerParams(dimension_semantics=sem, vmem_limit_bytes=vmem)


def _silu(x):
    return x * jax.nn.sigmoid(x)


def _log_sigmoid(x):
    return jnp.minimum(x, 0.0) - jnp.log1p(jnp.exp(-jnp.abs(x)))


def _modulated_rmsnorm(x, w, sc, sh):
    y = x * lax.rsqrt(jnp.mean(x * x, axis=-1, keepdims=True) + EPS)
    return (y * w) * (1.0 + sc) + sh


def _pack_bf16_pair(lo, hi):
    lo_bits = lax.bitcast_convert_type(lo.astype(BF16).astype(F32), U32)
    hi_bits = lax.bitcast_convert_type(hi.astype(BF16).astype(F32), U32)
    return (lo_bits >> 16) | (hi_bits & HI_MASK)


def _unpack_bf16_pair(packed):
    lo = lax.bitcast_convert_type(packed << 16, F32)
    hi = lax.bitcast_convert_type(packed & HI_MASK, F32)
    return lo, hi


def _lower_tri(n, dtype):
    r = lax.broadcasted_iota(I32, (n, n), 0)
    c = lax.broadcasted_iota(I32, (n, n), 1)
    return (r >= c).astype(dtype)


def _split3_dot(lhs01, x):
    hi = x.astype(BF16)
    r1 = x - hi.astype(F32)
    mid = r1.astype(BF16)
    lo = (r1 - mid.astype(F32)).astype(BF16)
    return (jnp.dot(lhs01, hi, preferred_element_type=F32)
            + jnp.dot(lhs01, mid, preferred_element_type=F32)
            + jnp.dot(lhs01, lo, preferred_element_type=F32))


def _split2_dot_nt(a, b):
    nt = (((1,), (1,)), ((), ()))
    a_hi = a.astype(BF16)
    a_lo = (a - a_hi.astype(F32)).astype(BF16)
    b_hi = b.astype(BF16)
    b_lo = (b - b_hi.astype(F32)).astype(BF16)
    return (lax.dot_general(a_hi, b_hi, nt, preferred_element_type=F32)
            + lax.dot_general(a_hi, b_lo, nt, preferred_element_type=F32)
            + lax.dot_general(a_lo, b_hi, nt, preferred_element_type=F32))


def _ada_body(c_ref, w_ref, b_ref, o_ref):
    cond = _silu(c_ref[...])
    o_ref[0] = jnp.dot(cond.astype(BF16), w_ref[0].astype(BF16),
                       preferred_element_type=F32) + b_ref[0]


def _ada(c, ada_w, ada_b):
    depth, d, six_d = ada_w.shape
    bsz = c.shape[0]
    nj = six_d // d
    return pl.pallas_call(
        _ada_body,
        grid=(depth, nj),
        in_specs=[pl.BlockSpec((bsz, d), lambda l, j: (0, 0)),
                  pl.BlockSpec((1, d, d), lambda l, j: (l, 0, j)),
                  pl.BlockSpec((1, 1, d), lambda l, j: (l, 0, j))],
        out_specs=pl.BlockSpec((1, bsz, d), lambda l, j: (l, 0, j)),
        out_shape=jax.ShapeDtypeStruct((depth, bsz, six_d), F32),
        compiler_params=_cparams(("parallel", "parallel")),
        name="ada_ln",
    )(c, ada_w, ada_b.reshape(depth, 1, six_d))


def _mlstm_body(h0_ref, hn_ref, sh_ref, sc_ref, mw_ref, w_ref, ws_ref, cw_ref, gb_ref, nw_ref,
                ha_ref, qk_nxt, v_nxt, o_nxt, sm_nxt, xe_scr, v_scr, o_scr, sm_scr,
                c_scr, n_scr, m_scr):
    n_seq, rows = hn_ref.shape[0], hn_ref.shape[1]
    half = M_HEADS * M_DQK
    vcols = M_HEADS * M_DV

    def projection_pieces(h_ref):
        u = jnp.concatenate(
            [_modulated_rmsnorm(h_ref[g], mw_ref[...], sc_ref[g], sh_ref[g]) for g in range(n_seq)],
            axis=0).astype(BF16)

        def piece(dst, w_lo, lo, width):
            def run():
                dst[:, lo:lo + width] = jnp.dot(u, w_ref[:, w_lo + lo:w_lo + lo + width],
                                                preferred_element_type=F32).astype(dst.dtype)
            return run

        def small_piece():
            sm_nxt[...] = jnp.dot(u, ws_ref[...], preferred_element_type=F32)

        pieces = [small_piece]
        for dst, w_lo in ((qk_nxt, 0), (v_nxt, 2 * half), (o_nxt, 2 * half + vcols)):
            for lo in range(0, vcols, PROJ_PIECE_COLS):
                pieces.append(piece(dst, w_lo, lo, PROJ_PIECE_COLS))
        return pieces

    @pl.when(pl.program_id(1) == 0)
    def _():
        xe_scr[:, 0:CONV_HALO, :] = jnp.zeros((n_seq, CONV_HALO, 2 * half), F32)
        c_scr[...] = jnp.zeros(c_scr.shape, F32)
        n_scr[...] = jnp.zeros(n_scr.shape, F32)
        m_scr[...] = jnp.zeros(m_scr.shape, F32)
        for run in projection_pieces(h0_ref):
            run()

    for g in range(n_seq):
        xe_scr[g, CONV_HALO:CONV_HALO + rows, :] = qk_nxt[g * rows:(g + 1) * rows, :]
    v_scr[...] = v_nxt[...]
    o_scr[...] = o_nxt[...]
    sm_scr[...] = sm_nxt[...]
    pending = projection_pieces(hn_ref)

    r_io = lax.broadcasted_iota(I32, (rows, rows), 0)
    c_io = lax.broadcasted_iota(I32, (rows, rows), 1)
    causal = r_io >= c_io
    tri = _lower_tri(rows, BF16)

    for g in range(n_seq):
        gr = slice(g * rows, (g + 1) * rows)
        conv = None
        for j in range(M_CONV):
            off = CONV_HALO - (M_CONV - 1) + j
            term = xe_scr[g, off:off + rows, :] * cw_ref[j:j + 1, :]
            conv = term if conv is None else conv + term
        qk = _silu(conv)
        xe_scr[g, 0:CONV_HALO, :] = xe_scr[g, rows:rows + CONV_HALO, :]

        capped = GATE_CAP * jnp.tanh((sm_scr[gr, :] + gb_ref[...]) / GATE_CAP)
        li_all = capped
        lf_all = _log_sigmoid(capped)
        b_all = _split3_dot(tri, lf_all)
        li_t = li_all.T
        b_t = b_all.T

        for h in range(M_HEADS):
            if pending:
                pending.pop(0)()
            sidx = g * M_HEADS + h
            q = (qk[:, h * M_DQK:(h + 1) * M_DQK] * (M_DQK ** -0.5)).astype(BF16)
            k = qk[:, half + h * M_DQK:half + (h + 1) * M_DQK]
            kb = k.astype(BF16)
            v = v_scr[gr, h * M_DV:(h + 1) * M_DV]
            li_c = li_all[:, SMALL_I + h:SMALL_I + h + 1]
            b_c = b_all[:, SMALL_F + h:SMALL_F + h + 1]
            li_r = li_t[SMALL_I + h:SMALL_I + h + 1, :]
            b_r = b_t[SMALL_F + h:SMALL_F + h + 1, :]
            gsum = b_c[rows - 1:rows, :]
            m_prev = m_scr[sidx][:, 0:1]
            c_prev = c_scr[sidx]
            n_prev = n_scr[sidx]

            d_mat = jnp.where(causal, b_c - b_r + li_r, -jnp.inf)
            m_inter = b_c + m_prev
            m_t = jnp.maximum(jnp.max(d_mat, axis=1, keepdims=True), m_inter)
            s = lax.dot_general(q, kb, (((1,), (1,)), ((), ())), preferred_element_type=F32)
            p = jnp.exp(d_mat - m_t) * s
            w_inter = jnp.exp(m_inter - m_t)
            num = (jnp.dot(p.astype(BF16), v, preferred_element_type=F32)
                   + w_inter * jnp.dot(q, c_prev.astype(BF16), preferred_element_type=F32))
            qn = jnp.sum(q.astype(F32) * n_prev, axis=1, keepdims=True)
            den = jnp.sum(p, axis=1, keepdims=True) + w_inter * qn
            hh = num / jnp.maximum(jnp.abs(den), jnp.exp(-m_t))

            if pending:
                pending.pop(0)()

            a_r = gsum - b_r + li_r
            a_c = gsum - b_c + li_c
            m_new = jnp.maximum(gsum + m_prev, jnp.max(a_r, axis=1, keepdims=True))
            decay = jnp.exp(gsum + m_prev - m_new)
            wk = jnp.exp(a_c - m_new) * k
            c_scr[sidx] = decay * c_prev + lax.dot_general(
                wk.astype(BF16), v, (((0,), (0,)), ((), ())), preferred_element_type=F32)
            n_scr[sidx] = decay * n_prev + jnp.sum(wk, axis=0, keepdims=True)
            m_scr[sidx] = jnp.broadcast_to(m_new, m_scr.shape[1:])

            y = hh * lax.rsqrt(jnp.mean(hh * hh, axis=-1, keepdims=True) + EPS)
            y = y * nw_ref[:, h * M_DV:(h + 1) * M_DV]
            gate = jax.nn.sigmoid(o_scr[gr, h * M_DV:(h + 1) * M_DV])
            ha_ref[g, :, h * M_DV:(h + 1) * M_DV] = (y * gate).astype(BF16)
    for run in pending:
        run()


def _mlstm(h, sh, sc, mix_w, w_big, w_small, conv_w, gate_row, norm_w, bsz, seq):
    t, d = h.shape
    rows = min(M_CHUNK_ROWS, seq)
    nc = seq // rows
    ng = SEQ_GROUP if bsz % SEQ_GROUP == 0 else 1
    half = M_HEADS * M_DQK
    vcols = M_HEADS * M_DV
    wcols = M_GROUP_COLS
    assert COL_M_GROUP % wcols == 0
    resident = pl.Buffered(1)
    nxt = [pltpu.VMEM((ng * rows, 2 * half), F32), pltpu.VMEM((ng * rows, vcols), BF16),
           pltpu.VMEM((ng * rows, vcols), F32), pltpu.VMEM((ng * rows, SMALL_COLS), F32)]
    out = pl.pallas_call(
        _mlstm_body,
        grid=(bsz // ng, nc),
        in_specs=[pl.BlockSpec((ng, rows, d), lambda b, c: (b, 0, 0)),
                  pl.BlockSpec((ng, rows, d), lambda b, c: (b, jnp.minimum(c + 1, nc - 1), 0)),
                  pl.BlockSpec((ng, 1, d), lambda b, c: (b, 0, 0)),
                  pl.BlockSpec((ng, 1, d), lambda b, c: (b, 0, 0)),
                  pl.BlockSpec((1, d), lambda b, c: (0, 0)),
                  pl.BlockSpec((d, wcols), lambda b, c: (0, 0), pipeline_mode=resident),
                  pl.BlockSpec((d, SMALL_COLS), lambda b, c: (0, 0), pipeline_mode=resident),
                  pl.BlockSpec((M_CONV, 2 * half), lambda b, c: (0, 0)),
                  pl.BlockSpec((1, SMALL_COLS), lambda b, c: (0, 0)),
                  pl.BlockSpec((1, vcols), lambda b, c: (0, 0))],
        out_specs=pl.BlockSpec((ng, rows, vcols), lambda b, c: (b, c, 0)),
        out_shape=jax.ShapeDtypeStruct((bsz, seq, vcols), BF16),
        scratch_shapes=nxt + [pltpu.VMEM((ng, rows + CONV_HALO, 2 * half), F32)] + nxt[1:]
        + [pltpu.VMEM((ng * M_HEADS, M_DQK, M_DV), F32),
           pltpu.VMEM((ng * M_HEADS, 1, M_DQK), F32),
           pltpu.VMEM((ng * M_HEADS, 1, LANES), F32)],
        compiler_params=_cparams(("parallel", "arbitrary")),
        name="mlstm",
    )(h.reshape(bsz, seq, d), h.reshape(bsz, seq, d), sh, sc, mix_w, w_big, w_small, conv_w,
      gate_row, norm_w)
    return out.reshape(t, vcols)


def _gla_body(h0_ref, hn_ref, sh_ref, sc_ref, mw_ref, w_ref, ws_ref, aw_ref, ab_ref, nw_ref, hb_ref,
              qk_nxt, v_nxt, z_nxt, sm_nxt, qk_scr, v_scr, z_scr, sm_scr, st_scr):
    n_seq, rows = hn_ref.shape[0], hn_ref.shape[1]
    n_chunks = rows // G_CHUNK
    kcols = G_HEADS * G_DK
    vcols = G_HEADS * G_DV

    def projection_pieces(h_ref):
        u = jnp.concatenate(
            [_modulated_rmsnorm(h_ref[g], mw_ref[...], sc_ref[g], sh_ref[g]) for g in range(n_seq)],
            axis=0).astype(BF16)

        def piece(dst, w_lo, lo, width):
            def run():
                dst[:, lo:lo + width] = jnp.dot(u, w_ref[:, w_lo + lo:w_lo + lo + width],
                                                preferred_element_type=F32).astype(dst.dtype)
            return run

        def small_piece():
            sm_nxt[...] = jnp.dot(u, ws_ref[...], preferred_element_type=F32)

        pieces = [small_piece]
        for dst, w_lo in ((qk_nxt, 0), (v_nxt, 2 * kcols), (z_nxt, 2 * kcols + vcols)):
            for lo in range(0, vcols, PROJ_PIECE_COLS):
                pieces.append(piece(dst, w_lo, lo, PROJ_PIECE_COLS))
        return pieces

    @pl.when(pl.program_id(1) == 0)
    def _():
        st_scr[...] = jnp.zeros(st_scr.shape, F32)
        for run in projection_pieces(h0_ref):
            run()

    qk_scr[...] = qk_nxt[...]
    v_scr[...] = v_nxt[...]
    z_scr[...] = z_nxt[...]
    sm_scr[...] = sm_nxt[...]
    pending = projection_pieces(hn_ref)

    r_io = lax.broadcasted_iota(I32, (rows, rows), 0)
    c_io = lax.broadcasted_iota(I32, (rows, rows), 1)
    chunk_causal = jnp.logical_and(r_io >= c_io, r_io // G_CHUNK == c_io // G_CHUNK)
    chunk_tri = jnp.where(chunk_causal, 1.0, 0.0).astype(BF16)

    for g in range(n_seq):
        gr = slice(g * rows, (g + 1) * rows)
        for _ in range((len(pending) + n_seq - 1 - g) // (n_seq - g)):
            pending.pop(0)()
        q = qk_scr[gr, 0:kcols] * (G_DK ** -0.5)
        k = qk_scr[gr, kcols:2 * kcols]

        logits = jnp.dot(sm_scr[gr, :].astype(BF16), aw_ref[...], preferred_element_type=F32)
        la = _log_sigmoid(logits + ab_ref[...]) / G_TAU
        bc = _split3_dot(chunk_tri, la)
        gcs = [bc[(ci + 1) * G_CHUNK - 1:(ci + 1) * G_CHUNK, :] for ci in range(n_chunks)]
        gc_rows = jnp.concatenate([jnp.broadcast_to(gc, (G_CHUNK, kcols)) for gc in gcs], axis=0)

        q_in = (q * jnp.exp(bc)).astype(BF16)
        k_in = (k * jnp.exp(-bc)).astype(BF16)
        k_out = (k * jnp.exp(gc_rows - bc)).astype(BF16)

        for h in range(G_HEADS):
            ks = slice(h * G_DK, (h + 1) * G_DK)
            vs = slice(h * G_DV, (h + 1) * G_DV)
            v = v_scr[gr, vs]
            att = lax.dot_general(q_in[:, ks], k_in[:, ks], (((1,), (1,)), ((), ())),
                                  preferred_element_type=F32)
            att = jnp.where(chunk_causal, att, 0.0).astype(BF16)
            o_intra = jnp.dot(att, v, preferred_element_type=F32)
            st = st_scr[g * G_HEADS + h]
            outs = []
            for ci in range(n_chunks):
                rs = slice(ci * G_CHUNK, (ci + 1) * G_CHUNK)
                o_inter = lax.dot_general(q_in[rs, ks], st.astype(BF16),
                                          (((1,), (1,)), ((), ())), preferred_element_type=F32)
                outs.append(o_intra[rs, :] + o_inter)
                st = jnp.exp(gcs[ci][:, ks]) * st + lax.dot_general(
                    v[rs, :], k_out[rs, ks], (((0,), (0,)), ((), ())), preferred_element_type=F32)
            st_scr[g * G_HEADS + h] = st
            o = jnp.concatenate(outs, axis=0)
            y = o * lax.rsqrt(jnp.mean(o * o, axis=-1, keepdims=True) + EPS)
            y = y * nw_ref[:, vs]
            hb_ref[g, :, vs] = (y * _silu(z_scr[gr, vs])).astype(BF16)


def _gla(h, sh, sc, mix_w, w_big, w_small, alpha_full, alpha_b, norm_w, bsz, seq):
    t, d = h.shape
    rows = min(G_BLOCK_ROWS, seq)
    nb = seq // rows
    ng = SEQ_GROUP if bsz % SEQ_GROUP == 0 else 1
    kcols = G_HEADS * G_DK
    vcols = G_HEADS * G_DV
    wcols = G_GROUP_COLS
    assert COL_G_GROUP % wcols == 0
    resident = pl.Buffered(1)
    nxt = [pltpu.VMEM((ng * rows, 2 * kcols), F32), pltpu.VMEM((ng * rows, vcols), BF16),
           pltpu.VMEM((ng * rows, vcols), F32), pltpu.VMEM((ng * rows, SMALL_COLS), F32)]
    out = pl.pallas_call(
        _gla_body,
        grid=(bsz // ng, nb),
        in_specs=[pl.BlockSpec((ng, rows, d), lambda b, c: (b, 0, 0)),
                  pl.BlockSpec((ng, rows, d), lambda b, c: (b, jnp.minimum(c + 1, nb - 1), 0)),
                  pl.BlockSpec((ng, 1, d), lambda b, c: (b, 0, 0)),
                  pl.BlockSpec((ng, 1, d), lambda b, c: (b, 0, 0)),
                  pl.BlockSpec((1, d), lambda b, c: (0, 0)),
                  pl.BlockSpec((d, wcols), lambda b, c: (0, COL_G_GROUP // wcols),
                               pipeline_mode=resident),
                  pl.BlockSpec((d, SMALL_COLS), lambda b, c: (0, 0), pipeline_mode=resident),
                  pl.BlockSpec((SMALL_COLS, kcols), lambda b, c: (0, 0)),
                  pl.BlockSpec((1, kcols), lambda b, c: (0, 0)),
                  pl.BlockSpec((1, vcols), lambda b, c: (0, 0))],
        out_specs=pl.BlockSpec((ng, rows, vcols), lambda b, c: (b, c, 0)),
        out_shape=jax.ShapeDtypeStruct((bsz, seq, vcols), BF16),
        scratch_shapes=nxt + nxt + [pltpu.VMEM((ng * G_HEADS, G_DV, G_DK), F32)],
        compiler_params=_cparams(("parallel", "arbitrary")),
        name="gla",
    )(h.reshape(bsz, seq, d), h.reshape(bsz, seq, d), sh, sc, mix_w, w_big, w_small, alpha_full,
      alpha_b, norm_w)
    return out.reshape(t, vcols)


def _mixout_body(h_ref, sh_ref, sc_ref, mw_ref, wg_ref, ha_ref, hb_ref, gt_ref, wpa_ref, wpb_ref,
                 wo_ref, o_ref):
    d = h_ref.shape[1]
    h = h_ref[...]
    u = _modulated_rmsnorm(h, mw_ref[...], sc_ref[0], sh_ref[0]).astype(BF16)
    ga = jnp.dot(u, wg_ref[:, 0:d], preferred_element_type=F32)
    gb = jnp.dot(u, wg_ref[:, d:2 * d], preferred_element_type=F32)
    a = jnp.dot(ha_ref[...], wpa_ref[...], preferred_element_type=F32)
    b = jnp.dot(hb_ref[...], wpb_ref[...], preferred_element_type=F32)
    y = jax.nn.sigmoid(ga) * a + jax.nn.sigmoid(gb) * b
    o_ref[...] = h + gt_ref[0] * jnp.dot(y.astype(BF16), wo_ref[...], preferred_element_type=F32)


def _mixout(h, sh, sc, mix_w, w_big, ha, hb, gt, w_pa, w_pb, w_o, seq):
    t, d = h.shape
    tm = min(MIX_ROWS, seq)
    per_b = seq // tm
    assert GATE_GROUP_COLS == 2 * d and COL_GATE_GROUP % GATE_GROUP_COLS == 0
    resident = pl.Buffered(1)
    wspec = pl.BlockSpec((d, d), lambda i: (0, 0), pipeline_mode=resident)
    bspec = pl.BlockSpec((1, 1, d), lambda i: (i // per_b, 0, 0))
    return pl.pallas_call(
        _mixout_body,
        grid=(t // tm,),
        in_specs=[pl.BlockSpec((tm, d), lambda i: (i, 0)),
                  bspec, bspec,
                  pl.BlockSpec((1, d), lambda i: (0, 0)),
                  pl.BlockSpec((d, 2 * d), lambda i: (0, COL_GATE_GROUP // GATE_GROUP_COLS),
                               pipeline_mode=resident),
                  pl.BlockSpec((tm, d), lambda i: (i, 0)),
                  pl.BlockSpec((tm, d), lambda i: (i, 0)),
                  bspec,
                  wspec, wspec, wspec],
        out_specs=pl.BlockSpec((tm, d), lambda i: (i, 0)),
        out_shape=jax.ShapeDtypeStruct((t, d), F32),
        compiler_params=_cparams(("parallel",)),
        name="mix_out",
    )(h, sh, sc, mix_w, w_big, ha, hb, gt, w_pa, w_pb, w_o)


def _route_body(h_ref, sh_ref, sc_ref, nw_ref, rwt_ref, rb_ref,
                u_ref, eidx_ref, wts_ref, pos_ref, cnt_ref, carry_scr):
    tm = h_ref.shape[0]

    @pl.when(pl.program_id(0) == 0)
    def _():
        carry_scr[...] = jnp.zeros(carry_scr.shape, F32)

    u = _modulated_rmsnorm(h_ref[...], nw_ref[...], sc_ref[0], sh_ref[0])
    half = u.shape[1] // 2
    u_ref[...] = _pack_bf16_pair(u[:, :half], u[:, half:])
    logits = _split2_dot_nt(rwt_ref[...], u)
    scores = jax.nn.sigmoid(logits)
    sel = scores + rb_ref[...]

    neg = -jnp.inf
    sub_io = lax.broadcasted_iota(I32, (GROUP_SIZE, tm), 0)
    pieces = []
    for g in range(N_GROUPS):
        blk = sel[g * GROUP_SIZE:(g + 1) * GROUP_SIZE, :]
        m1 = jnp.max(blk, axis=0, keepdims=True)
        first = jnp.min(jnp.where(blk == m1, sub_io, GROUP_SIZE), axis=0, keepdims=True)
        m2 = jnp.max(jnp.where(sub_io == first, neg, blk), axis=0, keepdims=True)
        pieces.append(jnp.broadcast_to(m1 + m2, (GROUP_SIZE, tm)))
    gscore = jnp.concatenate(pieces, axis=0)

    e_io = lax.broadcasted_iota(I32, (N_EXPERTS, tm), 0)
    grp_io = e_io // GROUP_SIZE
    gmask = jnp.zeros((N_EXPERTS, tm), jnp.bool_)
    for _ in range(TOPK_GROUPS):
        mx = jnp.max(gscore, axis=0, keepdims=True)
        gi = jnp.min(jnp.where(gscore == mx, grp_io, N_GROUPS), axis=0, keepdims=True)
        hit = grp_io == gi
        gmask = jnp.logical_or(gmask, hit)
        gscore = jnp.where(hit, neg, gscore)

    cur = jnp.where(gmask, sel, neg)
    row_io = lax.broadcasted_iota(I32, (TOP_K, tm), 0)
    eidx = jnp.zeros((TOP_K, tm), I32)
    wraw = jnp.zeros((TOP_K, tm), F32)
    chosen = jnp.zeros((N_EXPERTS, tm), jnp.bool_)
    hits = []
    for kk in range(TOP_K):
        mx = jnp.max(cur, axis=0, keepdims=True)
        ei = jnp.min(jnp.where(cur == mx, e_io, N_EXPERTS), axis=0, keepdims=True)
        hit = e_io == ei
        hits.append(hit)
        sc_k = jnp.sum(jnp.where(hit, scores, 0.0), axis=0, keepdims=True)
        eidx = jnp.where(row_io == kk, ei, eidx)
        wraw = jnp.where(row_io == kk, sc_k, wraw)
        chosen = jnp.logical_or(chosen, hit)
        cur = jnp.where(hit, neg, cur)

    wsum = jnp.sum(wraw, axis=0, keepdims=True)
    wts_ref[...] = wraw / wsum * ROUTED_SCALE
    eidx_ref[...] = eidx

    chosen_f = jnp.where(chosen, 1.0, 0.0)
    r_io = lax.broadcasted_iota(I32, (tm, tm), 0)
    c_io = lax.broadcasted_iota(I32, (tm, tm), 1)
    strict_upper = jnp.where(r_io < c_io, 1.0, 0.0).astype(BF16)
    prefix = jnp.dot(chosen_f.astype(BF16), strict_upper, preferred_element_type=F32)
    rank = prefix + carry_scr[:, 0:1]
    pos = jnp.zeros((TOP_K, tm), F32)
    for kk in range(TOP_K):
        p_k = jnp.sum(jnp.where(hits[kk], rank, 0.0), axis=0, keepdims=True)
        pos = jnp.where(row_io == kk, p_k, pos)
    pos_ref[...] = pos.astype(I32)
    total = carry_scr[...] + jnp.sum(chosen_f, axis=1, keepdims=True)
    carry_scr[...] = total
    cnt_ref[...] = total.astype(I32)


def _route(h, sh, sc, nw, rw_t, rb_col, seq):
    t, d = h.shape
    tm = min(ROUTE_ROWS, seq)
    per_b = seq // tm
    kspec = pl.BlockSpec((TOP_K, tm), lambda i: (0, i))
    return pl.pallas_call(
        _route_body,
        grid=(t // tm,),
        in_specs=[pl.BlockSpec((tm, d), lambda i: (i, 0)),
                  pl.BlockSpec((1, 1, d), lambda i: (i // per_b, 0, 0)),
                  pl.BlockSpec((1, 1, d), lambda i: (i // per_b, 0, 0)),
                  pl.BlockSpec((1, d), lambda i: (0, 0)),
                  pl.BlockSpec((N_EXPERTS, d), lambda i: (0, 0)),
                  pl.BlockSpec((N_EXPERTS, 1), lambda i: (0, 0))],
        out_specs=[pl.BlockSpec((tm, d // 2), lambda i: (i, 0)), kspec, kspec, kspec,
                   pl.BlockSpec((N_EXPERTS, LANES), lambda i: (0, 0))],
        out_shape=[jax.ShapeDtypeStruct((t, d // 2), U32),
                   jax.ShapeDtypeStruct((TOP_K, t), I32),
                   jax.ShapeDtypeStruct((TOP_K, t), F32),
                   jax.ShapeDtypeStruct((TOP_K, t), I32),
                   jax.ShapeDtypeStruct((N_EXPERTS, LANES), I32)],
        scratch_shapes=[pltpu.VMEM((N_EXPERTS, LANES), F32)],
        compiler_params=_cparams(("arbitrary",)),
        name="moe_route",
    )(h, sh, sc, nw, rw_t, rb_col)


def _sc_worker_id():
    return lax.axis_index("s") * SC_CORES + lax.axis_index("c")


def _sc_scatter_rows(x, dest, n_rows):
    t, d = x.shape
    n_k = dest.shape[0]
    assert t % (SC_WORKERS * 2 * SC_CHUNK) == 0
    per_w = t // SC_WORKERS
    n_ch = per_w // SC_CHUNK
    dest4 = dest.reshape(n_k, SC_WORKERS, n_ch, SC_CHUNK).transpose(1, 2, 0, 3)
    mesh = plsc.VectorSubcoreMesh(core_axis_name="c", subcore_axis_name="s")

    @functools.partial(
        pl.kernel, mesh=mesh,
        out_type=jax.ShapeDtypeStruct((n_rows, d), x.dtype),
        scratch_types=[pltpu.VMEM((n_ch, n_k, SC_CHUNK), I32),
                       pltpu.VMEM((SC_CHUNK, d), x.dtype),
                       pltpu.VMEM((SC_CHUNK, d), x.dtype)] + [pltpu.SemaphoreType.DMA] * 4,
        name="moe_dispatch_sc",
    )
    def scatter_kernel(x_hbm, dest_hbm, out_hbm, idx_v, rows0, rows1, l0, l1, s0, s1):
        wid = _sc_worker_id()
        base = wid * per_w
        pltpu.sync_copy(dest_hbm.at[wid], idx_v)
        bufs = ((rows0, l0, s0), (rows1, l1, s1))

        def load(ci, b):
            rows, load_sem, _ = bufs[b]
            return pltpu.make_async_copy(x_hbm.at[pl.ds(base + ci * SC_CHUNK, SC_CHUNK)], rows,
                                         load_sem)

        def scatters(ci, b):
            rows, _, scatter_sem = bufs[b]
            return [pltpu.make_async_copy(rows, out_hbm.at[idx_v.at[ci, j]], scatter_sem)
                    for j in range(n_k)]

        load(0, 0).start()

        @pl.loop(0, n_ch, step=2)
        def _(ci):
            load(ci, 0).wait()
            for cp in scatters(ci, 0):
                cp.start()

            @pl.when(ci > 0)
            def _():
                for cp in scatters(ci - 1, 1):
                    cp.wait()

            load(ci + 1, 1).start()
            load(ci + 1, 1).wait()
            for cp in scatters(ci + 1, 1):
                cp.start()
            for cp in scatters(ci, 0):
                cp.wait()

            @pl.when(ci + 2 < n_ch)
            def _():
                load(ci + 2, 0).start()

        for cp in scatters(n_ch - 1, 1):
            cp.wait()

    return scatter_kernel(x, dest4)


def _sc_gather_rows(table, idx):
    n, d = idx.shape[0], table.shape[1]
    assert n % (SC_WORKERS * 2 * SC_CHUNK) == 0
    per_w = n // SC_WORKERS
    n_ch = per_w // SC_CHUNK
    idx3 = idx.reshape(SC_WORKERS, n_ch, SC_CHUNK)
    mesh = plsc.VectorSubcoreMesh(core_axis_name="c", subcore_axis_name="s")

    @functools.partial(
        pl.kernel, mesh=mesh,
        out_type=jax.ShapeDtypeStruct((n, d), table.dtype),
        scratch_types=[pltpu.VMEM((n_ch, SC_CHUNK), I32),
                       pltpu.VMEM((SC_CHUNK, d), table.dtype),
                       pltpu.VMEM((SC_CHUNK, d), table.dtype)] + [pltpu.SemaphoreType.DMA] * 4,
        name="moe_combine_sc",
    )
    def gather_kernel(table_hbm, idx_hbm, out_hbm, idx_v, rows0, rows1, g0, g1, w0, w1):
        wid = _sc_worker_id()
        base = wid * per_w
        pltpu.sync_copy(idx_hbm.at[wid], idx_v)
        bufs = ((rows0, g0, w0), (rows1, g1, w1))

        def gather(ci, b):
            rows, gather_sem, _ = bufs[b]
            return pltpu.make_async_copy(table_hbm.at[idx_v.at[ci]], rows, gather_sem)

        def write(ci, b):
            rows, _, write_sem = bufs[b]
            return pltpu.make_async_copy(rows, out_hbm.at[pl.ds(base + ci * SC_CHUNK, SC_CHUNK)],
                                         write_sem)

        gather(0, 0).start()

        @pl.loop(0, n_ch, step=2)
        def _(ci):
            gather(ci, 0).wait()
            write(ci, 0).start()

            @pl.when(ci > 0)
            def _():
                write(ci - 1, 1).wait()

            gather(ci + 1, 1).start()
            gather(ci + 1, 1).wait()
            write(ci + 1, 1).start()
            write(ci, 0).wait()

            @pl.when(ci + 2 < n_ch)
            def _():
                gather(ci + 2, 0).start()

        write(n_ch - 1, 1).wait()

    return gather_kernel(table, idx3)


def _expert_body(estart_ref, ecount_ref, x_hbm, w1_ref, w3_ref, w2_ref, y_hbm,
                 xbuf, ybuf, w1_scr, w3_scr, w2_scr, in_sem, out_sem):
    e = pl.program_id(0)
    start = estart_ref[e]
    count = ecount_ref[e]
    n_blk = (count + EXPERT_ROWS - 1) // EXPERT_ROWS
    half = xbuf.shape[2]

    def rows_at(b):
        return pl.ds(pl.multiple_of(start + b * EXPERT_ROWS, EXPERT_ROWS), EXPERT_ROWS)

    def read(b, slot):
        return pltpu.make_async_copy(x_hbm.at[rows_at(b)], xbuf.at[slot], in_sem.at[slot])

    def write(b, slot):
        return pltpu.make_async_copy(ybuf.at[slot], y_hbm.at[rows_at(b)], out_sem.at[slot])

    @pl.when(n_blk > 0)
    def _():
        read(0, 0).start()
        w1_scr[...] = w1_ref[0].astype(BF16)
        w3_scr[...] = w3_ref[0].astype(BF16)
        w2_scr[...] = w2_ref[0].astype(BF16)

    def block(b, carry):
        slot = b % 2
        read(b, slot).wait()

        @pl.when(b + 1 < n_blk)
        def _():
            read(b + 1, 1 - slot).start()

        @pl.when(b >= 2)
        def _():
            write(b - 2, slot).wait()

        valid = count - b * EXPERT_ROWS
        rows = lax.broadcasted_iota(I32, (EXPERT_ROWS, half), 0)
        lo, hi = _unpack_bf16_pair(jnp.where(rows < valid, xbuf[slot], 0))
        lo, hi = lo.astype(BF16), hi.astype(BF16)

        def proj(w_scr):
            return (jnp.dot(lo, w_scr[0:half, :], preferred_element_type=F32)
                    + jnp.dot(hi, w_scr[half:2 * half, :], preferred_element_type=F32))

        hid = _silu(proj(w1_scr)) * proj(w3_scr)
        y = jnp.dot(hid.astype(BF16), w2_scr[...], preferred_element_type=F32)
        ybuf[slot] = _pack_bf16_pair(y[:, :half], y[:, half:])
        write(b, slot).start()
        return carry

    lax.fori_loop(0, n_blk, block, 0)

    @pl.when(n_blk >= 2)
    def _():
        write(n_blk - 2, n_blk % 2).wait()

    @pl.when(n_blk >= 1)
    def _():
        write(n_blk - 1, (n_blk - 1) % 2).wait()


def _experts(xg, estart, ecount, w1, w3, w2, layer):
    n_rows, half = xg.shape
    d, de = w1.shape[-2:]
    grid_spec = pltpu.PrefetchScalarGridSpec(
        num_scalar_prefetch=2,
        grid=(N_EXPERTS,),
        in_specs=[pl.BlockSpec(memory_space=pl.ANY),
                  pl.BlockSpec((None, 1, d, de), lambda e, es, ec: (layer, e, 0, 0)),
                  pl.BlockSpec((None, 1, d, de), lambda e, es, ec: (layer, e, 0, 0)),
                  pl.BlockSpec((None, 1, de, d), lambda e, es, ec: (layer, e, 0, 0))],
        out_specs=pl.BlockSpec(memory_space=pl.ANY),
        scratch_shapes=[pltpu.VMEM((2, EXPERT_ROWS, half), U32), pltpu.VMEM((2, EXPERT_ROWS, half), U32),
                        pltpu.VMEM((d, de), BF16), pltpu.VMEM((d, de), BF16),
                        pltpu.VMEM((de, d), BF16),
                        pltpu.SemaphoreType.DMA((2,)), pltpu.SemaphoreType.DMA((2,))],
    )
    return pl.pallas_call(
        _expert_body,
        grid_spec=grid_spec,
        out_shape=jax.ShapeDtypeStruct((n_rows, half), U32),
        compiler_params=_cparams(("arbitrary",)),
        name="moe_experts",
    )(estart, ecount, xg, w1, w3, w2)


def _combine_body(h_ref, u_ref, yg_ref, wts_ref, gt_ref, s1_ref, s3_ref, s2_ref, fw_ref, o_ref, *,
                  final_norm):
    half = u_ref.shape[1]
    lo, hi = _unpack_bf16_pair(u_ref[...])
    lo, hi = lo.astype(BF16), hi.astype(BF16)

    def proj(w_ref):
        return (jnp.dot(lo, w_ref[0:half, :], preferred_element_type=F32)
                + jnp.dot(hi, w_ref[half:2 * half, :], preferred_element_type=F32))

    hid = _silu(proj(s1_ref)) * proj(s3_ref)
    shared = jnp.dot(hid.astype(BF16), s2_ref[...], preferred_element_type=F32)

    routed_lo = routed_hi = None
    for kk in range(TOP_K):
        y_lo, y_hi = _unpack_bf16_pair(yg_ref[kk])
        w = wts_ref[:, kk:kk + 1]
        routed_lo = y_lo * w if routed_lo is None else routed_lo + y_lo * w
        routed_hi = y_hi * w if routed_hi is None else routed_hi + y_hi * w
    gt = gt_ref[0]
    out_lo = h_ref[:, 0:half] + gt[:, 0:half] * (routed_lo + shared[:, 0:half])
    out_hi = (h_ref[:, half:2 * half]
              + gt[:, half:2 * half] * (routed_hi + shared[:, half:2 * half]))
    if final_norm:
        ssq = (jnp.sum(out_lo * out_lo, axis=-1, keepdims=True)
               + jnp.sum(out_hi * out_hi, axis=-1, keepdims=True))
        inv = lax.rsqrt(ssq / (2 * half) + EPS)
        out_lo = (out_lo * inv) * fw_ref[:, 0:half]
        out_hi = (out_hi * inv) * fw_ref[:, half:2 * half]
    o_ref[:, 0:half] = out_lo
    o_ref[:, half:2 * half] = out_hi


def _combine(h, u, yg, wts_tk, gt, s1, s3, s2, final_w, seq, final_norm, part, n_parts):
    t, d = h.shape
    half = d // 2
    tm = min(COMBINE_ROWS, seq)
    per_b = seq // tm
    ds_ = s1.shape[-1]
    steps = t // n_parts // tm
    off = part * steps
    return pl.pallas_call(
        functools.partial(_combine_body, final_norm=final_norm),
        grid=(steps,),
        in_specs=[pl.BlockSpec((tm, d), lambda i: (i + off, 0)),
                  pl.BlockSpec((tm, half), lambda i: (i + off, 0)),
                  pl.BlockSpec((TOP_K, tm, half), lambda i: (0, i, 0)),
                  pl.BlockSpec((tm, TOP_K), lambda i: (i + off, 0)),
                  pl.BlockSpec((1, 1, d), lambda i: ((i + off) // per_b, 0, 0)),
                  pl.BlockSpec((d, ds_), lambda i: (0, 0)),
                  pl.BlockSpec((d, ds_), lambda i: (0, 0)),
                  pl.BlockSpec((ds_, d), lambda i: (0, 0)),
                  pl.BlockSpec((1, d), lambda i: (0, 0))],
        out_specs=pl.BlockSpec((tm, d), lambda i: (i + off, 0)),
        out_shape=jax.ShapeDtypeStruct((t, d), F32),
        input_output_aliases={0: 0},
        compiler_params=_cparams(("parallel",)),
        name="moe_combine",
    )(h, u, yg, wts_tk, gt, s1, s3, s2, final_w)


def _split_w_in(w_in):
    sizes = (2 * M_HEADS * M_DQK, M_HEADS * M_DV, M_HEADS * M_DV, M_HEADS, M_HEADS,
             G_HEADS * G_DK, G_HEADS * G_DK, G_HEADS * G_DV, G_RANK, G_HEADS * G_DV,
             D_MODEL, D_MODEL)
    offs = [0]
    for n in sizes:
        offs.append(offs[-1] + n)
    w16 = w_in.astype(BF16)
    big = jnp.concatenate([w16[:, offs[0]:offs[3]], w16[:, offs[5]:offs[8]], w16[:, offs[9]:offs[12]]],
                          axis=1)
    pad = jnp.zeros((w_in.shape[0], SMALL_COLS - 2 * M_HEADS - G_RANK), BF16)
    small = jnp.concatenate([w16[:, offs[3]:offs[5]], w16[:, offs[8]:offs[9]], pad], axis=1)
    return big, small


def _moe_layout(counts, eidx, pos):
    padded = (counts + EXPERT_ROWS - 1) // EXPERT_ROWS * EXPERT_ROWS
    pstart = jnp.cumsum(padded) - padded
    experts = jnp.arange(N_EXPERTS, dtype=I32)
    dest = pos + jnp.sum(jnp.where(eidx[..., None] == experts, pstart, 0), axis=-1)
    return dest.astype(I32), pstart.astype(I32), counts.astype(I32)


def kernel(x, c, ada_w, ada_b, norm_mix_w, norm_moe_w, w_in, m_conv_w, m_gate_b, m_norm_w,
           g_alpha_w, g_alpha_b, g_norm_w, w_pa, w_pb, w_o, router_w, router_b,
           exp_w1, exp_w3, exp_w2, sh_w1, sh_w3, sh_w2, final_norm_w):
    bsz, seq, d = x.shape
    depth = ada_w.shape[0]
    t = bsz * seq
    n_rows = t * TOP_K + N_EXPERTS * EXPERT_ROWS

    ada = _ada(c, ada_w, ada_b).reshape(depth, bsz, 6, 1, d)
    h = x.reshape(t, d)
    for l in range(depth):
        sh1, sc1, gt1, sh2, sc2, gt2 = (ada[l, :, i] for i in range(6))

        w_big, w_small = _split_w_in(w_in[l])
        mix_w = norm_mix_w[l][None, :]
        gate_row = jnp.zeros((1, SMALL_COLS), F32)
        gate_row = gate_row.at[0, SMALL_I:SMALL_I + M_HEADS].set(m_gate_b[l, 0])
        gate_row = gate_row.at[0, SMALL_F:SMALL_F + M_HEADS].set(m_gate_b[l, 1])
        ha = _mlstm(h, sh1, sc1, mix_w, w_big, w_small, m_conv_w[l], gate_row, m_norm_w[l][None, :],
                    bsz, seq)
        alpha_full = jnp.zeros((SMALL_COLS, G_HEADS * G_DK), F32)
        alpha_full = alpha_full.at[SMALL_R:SMALL_R + G_RANK].set(g_alpha_w[l]).astype(BF16)
        hb = _gla(h, sh1, sc1, mix_w, w_big, w_small, alpha_full, g_alpha_b[l][None, :],
                  g_norm_w[l][None, :], bsz, seq)
        h = _mixout(h, sh1, sc1, mix_w, w_big, ha, hb, gt1, w_pa[l].astype(BF16),
                    w_pb[l].astype(BF16), w_o[l].astype(BF16), seq)

        u, eidx, wts, pos, cnt = _route(h, sh2, sc2, norm_moe_w[l][None, :],
                                        router_w[l].T, router_b[l][:, None], seq)
        dest, estart, ecount = _moe_layout(cnt[:, 0], eidx, pos)
        xg = _sc_scatter_rows(u, dest, n_rows)
        y = _experts(xg, estart, ecount, exp_w1, exp_w3, exp_w2, l)
        tp = t // COMBINE_PARTS
        for p in range(COMBINE_PARTS):
            dest_p = dest[:, p * tp:(p + 1) * tp].reshape(-1)
            yg = _sc_gather_rows(y, dest_p).reshape(TOP_K, tp, d // 2)
            h = _combine(h, u, yg, wts.T, gt2, sh_w1[l].astype(BF16), sh_w3[l].astype(BF16),
                         sh_w2[l].astype(BF16), final_norm_w[None, :], seq,
                         final_norm=(l == depth - 1), part=p, n_parts=COMBINE_PARTS)

    return h.reshape(bsz, seq, d)
```

```python
import functools

import jax
import jax.numpy as jnp
import numpy as np
from jax import lax
from jax.experimental import pallas as pl
from jax.experimental.pallas import tpu as pltpu
from jax.experimental.pallas import tpu_sc as plsc

F32 = jnp.float32
BF16 = jnp.bfloat16
I32 = jnp.int32
U32 = jnp.uint32
HI_MASK = np.uint32(0xFFFF0000)

LANES = 128

SC_CORES = 2
SC_SUBCORES = 16
SC_WORKERS = SC_CORES * SC_SUBCORES
SC_CHUNK = 64

D_MODEL = 1024
M_HEADS = 4
M_DQK = 128
M_DV = 256
M_CONV = 4
GATE_CAP = 15.0
G_HEADS = 4
G_DK = 128
G_DV = 256
G_RANK = 16
G_TAU = 16.0
G_CHUNK = 64
N_EXPERTS = 64
TOP_K = 8
N_GROUPS = 8
GROUP_SIZE = N_EXPERTS // N_GROUPS
TOPK_GROUPS = 4
ROUTED_SCALE = 2.5
EPS = 1e-6

M_CHUNK_ROWS = 256
G_BLOCK_ROWS = 256
SEQ_GROUP = 2
PROJ_PIECE_COLS = 256
MIX_ROWS = 512
ROUTE_ROWS = 1024
EXPERT_ROWS = 512
EXPERT_DMA_PIECES = 4
COMBINE_ROWS = 512
COMBINE_PARTS = 4
CONV_HALO = 8
VMEM_LIMIT = 48 * 1024 * 1024

M_GROUP_COLS = 2 * M_HEADS * M_DQK + 2 * M_HEADS * M_DV
G_GROUP_COLS = 2 * G_HEADS * G_DK + 2 * G_HEADS * G_DV
GATE_GROUP_COLS = 2 * D_MODEL
COL_M_GROUP = 0
COL_G_GROUP = COL_M_GROUP + M_GROUP_COLS
COL_GATE_GROUP = COL_G_GROUP + G_GROUP_COLS
SMALL_COLS = LANES
SMALL_I, SMALL_F, SMALL_R = 0, M_HEADS, 2 * M_HEADS


def _cparams(sem, vmem=VMEM_LIMIT):
    return pltpu.CompilerParams(dimension_semantics=sem, vmem_limit_bytes=vmem)


def _silu(x):
    return x * jax.nn.sigmoid(x)


def _log_sigmoid(x):
    return jnp.minimum(x, 0.0) - jnp.log1p(jnp.exp(-jnp.abs(x)))


def _modulated_rmsnorm(x, w, sc, sh):
    y = x * lax.rsqrt(jnp.mean(x * x, axis=-1, keepdims=True) + EPS)
    return (y * w) * (1.0 + sc) + sh


def _pack_bf16_pair(lo, hi):
    lo_bits = lax.bitcast_convert_type(lo.astype(BF16).astype(F32), U32)
    hi_bits = lax.bitcast_convert_type(hi.astype(BF16).astype(F32), U32)
    return (lo_bits >> 16) | (hi_bits & HI_MASK)


def _unpack_bf16_pair(packed):
    lo = lax.bitcast_convert_type(packed << 16, F32)
    hi = lax.bitcast_convert_type(packed & HI_MASK, F32)
    return lo, hi


def _lower_tri(n, dtype):
    r = lax.broadcasted_iota(I32, (n, n), 0)
    c = lax.broadcasted_iota(I32, (n, n), 1)
    return (r >= c).astype(dtype)


def _split3_dot(lhs01, x):
    hi = x.astype(BF16)
    r1 = x - hi.astype(F32)
    mid = r1.astype(BF16)
    lo = (r1 - mid.astype(F32)).astype(BF16)
    return (jnp.dot(lhs01, hi, preferred_element_type=F32)
            + jnp.dot(lhs01, mid, preferred_element_type=F32)
            + jnp.dot(lhs01, lo, preferred_element_type=F32))


def _split2_dot_nt(a, b):
    nt = (((1,), (1,)), ((), ()))
    a_hi = a.astype(BF16)
    a_lo = (a - a_hi.astype(F32)).astype(BF16)
    b_hi = b.astype(BF16)
    b_lo = (b - b_hi.astype(F32)).astype(BF16)
    return (lax.dot_general(a_hi, b_hi, nt, preferred_element_type=F32)
            + lax.dot_general(a_hi, b_lo, nt, preferred_element_type=F32)
            + lax.dot_general(a_lo, b_hi, nt, preferred_element_type=F32))


def _ada_body(c_ref, w_ref, b_ref, o_ref):
    cond = _silu(c_ref[...])
    o_ref[0] = jnp.dot(cond.astype(BF16), w_ref[0].astype(BF16),
                       preferred_element_type=F32) + b_ref[0]


def _ada(c, ada_w, ada_b):
    depth, d, six_d = ada_w.shape
    bsz = c.shape[0]
    nj = six_d // d
    return pl.pallas_call(
        _ada_body,
        grid=(depth, nj),
        in_specs=[pl.BlockSpec((bsz, d), lambda l, j: (0, 0)),
                  pl.BlockSpec((1, d, d), lambda l, j: (l, 0, j)),
                  pl.BlockSpec((1, 1, d), lambda l, j: (l, 0, j))],
        out_specs=pl.BlockSpec((1, bsz, d), lambda l, j: (l, 0, j)),
        out_shape=jax.ShapeDtypeStruct((depth, bsz, six_d), F32),
        compiler_params=_cparams(("parallel", "parallel")),
        name="ada_ln",
    )(c, ada_w, ada_b.reshape(depth, 1, six_d))


def _mlstm_body(h0_ref, hn_ref, sh_ref, sc_ref, mw_ref, w_ref, ws_ref, cw_ref, gb_ref, nw_ref,
                ha_ref, qk_nxt, v_nxt, o_nxt, sm_nxt, xe_scr, v_scr, o_scr, sm_scr,
                c_scr, n_scr, m_scr):
    n_seq, rows = hn_ref.shape[0], hn_ref.shape[1]
    half = M_HEADS * M_DQK
    vcols = M_HEADS * M_DV

    def projection_pieces(h_ref):
        u = jnp.concatenate(
            [_modulated_rmsnorm(h_ref[g], mw_ref[...], sc_ref[g], sh_ref[g]) for g in range(n_seq)],
            axis=0).astype(BF16)

        def piece(dst, w_lo, lo, width):
            def run():
                dst[:, lo:lo + width] = jnp.dot(u, w_ref[:, w_lo + lo:w_lo + lo + width],
                                                preferred_element_type=F32).astype(dst.dtype)
            return run

        def small_piece():
            sm_nxt[...] = jnp.dot(u, ws_ref[...], preferred_element_type=F32)

        pieces = [small_piece]
        for dst, w_lo in ((qk_nxt, 0), (v_nxt, 2 * half), (o_nxt, 2 * half + vcols)):
            for lo in range(0, vcols, PROJ_PIECE_COLS):
                pieces.append(piece(dst, w_lo, lo, PROJ_PIECE_COLS))
        return pieces

    @pl.when(pl.program_id(1) == 0)
    def _():
        xe_scr[:, 0:CONV_HALO, :] = jnp.zeros((n_seq, CONV_HALO, 2 * half), F32)
        c_scr[...] = jnp.zeros(c_scr.shape, F32)
        n_scr[...] = jnp.zeros(n_scr.shape, F32)
        m_scr[...] = jnp.zeros(m_scr.shape, F32)
        for run in projection_pieces(h0_ref):
            run()

    for g in range(n_seq):
        xe_scr[g, CONV_HALO:CONV_HALO + rows, :] = qk_nxt[g * rows:(g + 1) * rows, :]
    v_scr[...] = v_nxt[...]
    o_scr[...] = o_nxt[...]
    sm_scr[...] = sm_nxt[...]
    pending = projection_pieces(hn_ref)

    r_io = lax.broadcasted_iota(I32, (rows, rows), 0)
    c_io = lax.broadcasted_iota(I32, (rows, rows), 1)
    causal = r_io >= c_io
    tri = _lower_tri(rows, BF16)

    for g in range(n_seq):
        gr = slice(g * rows, (g + 1) * rows)
        conv = None
        for j in range(M_CONV):
            off = CONV_HALO - (M_CONV - 1) + j
            term = xe_scr[g, off:off + rows, :] * cw_ref[j:j + 1, :]
            conv = term if conv is None else conv + term
        qk = _silu(conv)
        xe_scr[g, 0:CONV_HALO, :] = xe_scr[g, rows:rows + CONV_HALO, :]

        capped = GATE_CAP * jnp.tanh((sm_scr[gr, :] + gb_ref[...]) / GATE_CAP)
        li_all = capped
        lf_all = _log_sigmoid(capped)
        b_all = _split3_dot(tri, lf_all)
        li_t = li_all.T
        b_t = b_all.T

        for h in range(M_HEADS):
            if pending:
                pending.pop(0)()
            sidx = g * M_HEADS + h
            q = (qk[:, h * M_DQK:(h + 1) * M_DQK] * (M_DQK ** -0.5)).astype(BF16)
            k = qk[:, half + h * M_DQK:half + (h + 1) * M_DQK]
            kb = k.astype(BF16)
            v = v_scr[gr, h * M_DV:(h + 1) * M_DV]
            li_c = li_all[:, SMALL_I + h:SMALL_I + h + 1]
            b_c = b_all[:, SMALL_F + h:SMALL_F + h + 1]
            li_r = li_t[SMALL_I + h:SMALL_I + h + 1, :]
            b_r = b_t[SMALL_F + h:SMALL_F + h + 1, :]
            gsum = b_c[rows - 1:rows, :]
            m_prev = m_scr[sidx][:, 0:1]
            c_prev = c_scr[sidx]
            n_prev = n_scr[sidx]

            d_mat = jnp.where(causal, b_c - b_r + li_r, -jnp.inf)
            m_inter = b_c + m_prev
            m_t = jnp.maximum(jnp.max(d_mat, axis=1, keepdims=True), m_inter)
            s = lax.dot_general(q, kb, (((1,), (1,)), ((), ())), preferred_element_type=F32)
            p = jnp.exp(d_mat - m_t) * s
            w_inter = jnp.exp(m_inter - m_t)
            num = (jnp.dot(p.astype(BF16), v, preferred_element_type=F32)
                   + w_inter * jnp.dot(q, c_prev.astype(BF16), preferred_element_type=F32))
            qn = jnp.sum(q.astype(F32) * n_prev, axis=1, keepdims=True)
            den = jnp.sum(p, axis=1, keepdims=True) + w_inter * qn
            hh = num / jnp.maximum(jnp.abs(den), jnp.exp(-m_t))

            if pending:
                pending.pop(0)()

            a_r = gsum - b_r + li_r
            a_c = gsum - b_c + li_c
            m_new = jnp.maximum(gsum + m_prev, jnp.max(a_r, axis=1, keepdims=True))
            decay = jnp.exp(gsum + m_prev - m_new)
            wk = jnp.exp(a_c - m_new) * k
            c_scr[sidx] = decay * c_prev + lax.dot_general(
                wk.astype(BF16), v, (((0,), (0,)), ((), ())), preferred_element_type=F32)
            n_scr[sidx] = decay * n_prev + jnp.sum(wk, axis=0, keepdims=True)
            m_scr[sidx] = jnp.broadcast_to(m_new, m_scr.shape[1:])

            y = hh * lax.rsqrt(jnp.mean(hh * hh, axis=-1, keepdims=True) + EPS)
            y = y * nw_ref[:, h * M_DV:(h + 1) * M_DV]
            gate = jax.nn.sigmoid(o_scr[gr, h * M_DV:(h + 1) * M_DV])
            ha_ref[g, :, h * M_DV:(h + 1) * M_DV] = (y * gate).astype(BF16)
    for run in pending:
        run()


def _mlstm(h, sh, sc, mix_w, w_big, w_small, conv_w, gate_row, norm_w, bsz, seq):
    t, d = h.shape
    rows = min(M_CHUNK_ROWS, seq)
    nc = seq // rows
    ng = SEQ_GROUP if bsz % SEQ_GROUP == 0 else 1
    half = M_HEADS * M_DQK
    vcols = M_HEADS * M_DV
    wcols = M_GROUP_COLS
    assert COL_M_GROUP % wcols == 0
    resident = pl.Buffered(1)
    nxt = [pltpu.VMEM((ng * rows, 2 * half), F32), pltpu.VMEM((ng * rows, vcols), BF16),
           pltpu.VMEM((ng * rows, vcols), F32), pltpu.VMEM((ng * rows, SMALL_COLS), F32)]
    out = pl.pallas_call(
        _mlstm_body,
        grid=(bsz // ng, nc),
        in_specs=[pl.BlockSpec((ng, rows, d), lambda b, c: (b, 0, 0)),
                  pl.BlockSpec((ng, rows, d), lambda b, c: (b, jnp.minimum(c + 1, nc - 1), 0)),
                  pl.BlockSpec((ng, 1, d), lambda b, c: (b, 0, 0)),
                  pl.BlockSpec((ng, 1, d), lambda b, c: (b, 0, 0)),
                  pl.BlockSpec((1, d), lambda b, c: (0, 0)),
                  pl.BlockSpec((d, wcols), lambda b, c: (0, 0), pipeline_mode=resident),
                  pl.BlockSpec((d, SMALL_COLS), lambda b, c: (0, 0), pipeline_mode=resident),
                  pl.BlockSpec((M_CONV, 2 * half), lambda b, c: (0, 0)),
                  pl.BlockSpec((1, SMALL_COLS), lambda b, c: (0, 0)),
                  pl.BlockSpec((1, vcols), lambda b, c: (0, 0))],
        out_specs=pl.BlockSpec((ng, rows, vcols), lambda b, c: (b, c, 0)),
        out_shape=jax.ShapeDtypeStruct((bsz, seq, vcols), BF16),
        scratch_shapes=nxt + [pltpu.VMEM((ng, rows + CONV_HALO, 2 * half), F32)] + nxt[1:]
        + [pltpu.VMEM((ng * M_HEADS, M_DQK, M_DV), F32),
           pltpu.VMEM((ng * M_HEADS, 1, M_DQK), F32),
           pltpu.VMEM((ng * M_HEADS, 1, LANES), F32)],
        compiler_params=_cparams(("parallel", "arbitrary")),
        name="mlstm",
    )(h.reshape(bsz, seq, d), h.reshape(bsz, seq, d), sh, sc, mix_w, w_big, w_small, conv_w,
      gate_row, norm_w)
    return out.reshape(t, vcols)


def _gla_body(h0_ref, hn_ref, sh_ref, sc_ref, mw_ref, w_ref, ws_ref, aw_ref, ab_ref, nw_ref, hb_ref,
              qk_nxt, v_nxt, z_nxt, sm_nxt, qk_scr, v_scr, z_scr, sm_scr, st_scr):
    n_seq, rows = hn_ref.shape[0], hn_ref.shape[1]
    n_chunks = rows // G_CHUNK
    kcols = G_HEADS * G_DK
    vcols = G_HEADS * G_DV

    def projection_pieces(h_ref):
        u = jnp.concatenate(
            [_modulated_rmsnorm(h_ref[g], mw_ref[...], sc_ref[g], sh_ref[g]) for g in range(n_seq)],
            axis=0).astype(BF16)

        def piece(dst, w_lo, lo, width):
            def run():
                dst[:, lo:lo + width] = jnp.dot(u, w_ref[:, w_lo + lo:w_lo + lo + width],
                                                preferred_element_type=F32).astype(dst.dtype)
            return run

        def small_piece():
            sm_nxt[...] = jnp.dot(u, ws_ref[...], preferred_element_type=F32)

        pieces = [small_piece]
        for dst, w_lo in ((qk_nxt, 0), (v_nxt, 2 * kcols), (z_nxt, 2 * kcols + vcols)):
            for lo in range(0, vcols, PROJ_PIECE_COLS):
                pieces.append(piece(dst, w_lo, lo, PROJ_PIECE_COLS))
        return pieces

    @pl.when(pl.program_id(1) == 0)
    def _():
        st_scr[...] = jnp.zeros(st_scr.shape, F32)
        for run in projection_pieces(h0_ref):
            run()

    qk_scr[...] = qk_nxt[...]
    v_scr[...] = v_nxt[...]
    z_scr[...] = z_nxt[...]
    sm_scr[...] = sm_nxt[...]
    pending = projection_pieces(hn_ref)

    r_io = lax.broadcasted_iota(I32, (rows, rows), 0)
    c_io = lax.broadcasted_iota(I32, (rows, rows), 1)
    chunk_causal = jnp.logical_and(r_io >= c_io, r_io // G_CHUNK == c_io // G_CHUNK)
    chunk_tri = jnp.where(chunk_causal, 1.0, 0.0).astype(BF16)

    for g in range(n_seq):
        gr = slice(g * rows, (g + 1) * rows)
        for _ in range((len(pending) + n_seq - 1 - g) // (n_seq - g)):
            pending.pop(0)()
        q = qk_scr[gr, 0:kcols] * (G_DK ** -0.5)
        k = qk_scr[gr, kcols:2 * kcols]

        logits = jnp.dot(sm_scr[gr, :].astype(BF16), aw_ref[...], preferred_element_type=F32)
        la = _log_sigmoid(logits + ab_ref[...]) / G_TAU
        bc = _split3_dot(chunk_tri, la)
        gcs = [bc[(ci + 1) * G_CHUNK - 1:(ci + 1) * G_CHUNK, :] for ci in range(n_chunks)]
        gc_rows = jnp.concatenate([jnp.broadcast_to(gc, (G_CHUNK, kcols)) for gc in gcs], axis=0)

        q_in = (q * jnp.exp(bc)).astype(BF16)
        k_in = (k * jnp.exp(-bc)).astype(BF16)
        k_out = (k * jnp.exp(gc_rows - bc)).astype(BF16)

        for h in range(G_HEADS):
            ks = slice(h * G_DK, (h + 1) * G_DK)
            vs = slice(h * G_DV, (h + 1) * G_DV)
            v = v_scr[gr, vs]
            att = lax.dot_general(q_in[:, ks], k_in[:, ks], (((1,), (1,)), ((), ())),
                                  preferred_element_type=F32)
            att = jnp.where(chunk_causal, att, 0.0).astype(BF16)
            o_intra = jnp.dot(att, v, preferred_element_type=F32)
            st = st_scr[g * G_HEADS + h]
            outs = []
            for ci in range(n_chunks):
                rs = slice(ci * G_CHUNK, (ci + 1) * G_CHUNK)
                o_inter = lax.dot_general(q_in[rs, ks], st.astype(BF16),
                                          (((1,), (1,)), ((), ())), preferred_element_type=F32)
                outs.append(o_intra[rs, :] + o_inter)
                st = jnp.exp(gcs[ci][:, ks]) * st + lax.dot_general(
                    v[rs, :], k_out[rs, ks], (((0,), (0,)), ((), ())), preferred_element_type=F32)
            st_scr[g * G_HEADS + h] = st
            o = jnp.concatenate(outs, axis=0)
            y = o * lax.rsqrt(jnp.mean(o * o, axis=-1, keepdims=True) + EPS)
            y = y * nw_ref[:, vs]
            hb_ref[g, :, vs] = (y * _silu(z_scr[gr, vs])).astype(BF16)


def _gla(h, sh, sc, mix_w, w_big, w_small, alpha_full, alpha_b, norm_w, bsz, seq):
    t, d = h.shape
    rows = min(G_BLOCK_ROWS, seq)
    nb = seq // rows
    ng = SEQ_GROUP if bsz % SEQ_GROUP == 0 else 1
    kcols = G_HEADS * G_DK
    vcols = G_HEADS * G_DV
    wcols = G_GROUP_COLS
    assert COL_G_GROUP % wcols == 0
    resident = pl.Buffered(1)
    nxt = [pltpu.VMEM((ng * rows, 2 * kcols), F32), pltpu.VMEM((ng * rows, vcols), BF16),
           pltpu.VMEM((ng * rows, vcols), F32), pltpu.VMEM((ng * rows, SMALL_COLS), F32)]
    out = pl.pallas_call(
        _gla_body,
        grid=(bsz // ng, nb),
        in_specs=[pl.BlockSpec((ng, rows, d), lambda b, c: (b, 0, 0)),
                  pl.BlockSpec((ng, rows, d), lambda b, c: (b, jnp.minimum(c + 1, nb - 1), 0)),
                  pl.BlockSpec((ng, 1, d), lambda b, c: (b, 0, 0)),
                  pl.BlockSpec((ng, 1, d), lambda b, c: (b, 0, 0)),
                  pl.BlockSpec((1, d), lambda b, c: (0, 0)),
                  pl.BlockSpec((d, wcols), lambda b, c: (0, COL_G_GROUP // wcols),
                               pipeline_mode=resident),
                  pl.BlockSpec((d, SMALL_COLS), lambda b, c: (0, 0), pipeline_mode=resident),
                  pl.BlockSpec((SMALL_COLS, kcols), lambda b, c: (0, 0)),
                  pl.BlockSpec((1, kcols), lambda b, c: (0, 0)),
                  pl.BlockSpec((1, vcols), lambda b, c: (0, 0))],
        out_specs=pl.BlockSpec((ng, rows, vcols), lambda b, c: (b, c, 0)),
        out_shape=jax.ShapeDtypeStruct((bsz, seq, vcols), BF16),
        scratch_shapes=nxt + nxt + [pltpu.VMEM((ng * G_HEADS, G_DV, G_DK), F32)],
        compiler_params=_cparams(("parallel", "arbitrary")),
        name="gla",
    )(h.reshape(bsz, seq, d), h.reshape(bsz, seq, d), sh, sc, mix_w, w_big, w_small, alpha_full,
      alpha_b, norm_w)
    return out.reshape(t, vcols)


def _mixout_body(h_ref, sh_ref, sc_ref, mw_ref, wg_ref, ha_ref, hb_ref, gt_ref, wpa_ref, wpb_ref,
                 wo_ref, o_ref):
    d = h_ref.shape[1]
    h = h_ref[...]
    u = _modulated_rmsnorm(h, mw_ref[...], sc_ref[0], sh_ref[0]).astype(BF16)
    ga = jnp.dot(u, wg_ref[:, 0:d], preferred_element_type=F32)
    gb = jnp.dot(u, wg_ref[:, d:2 * d], preferred_element_type=F32)
    a = jnp.dot(ha_ref[...], wpa_ref[...], preferred_element_type=F32)
    b = jnp.dot(hb_ref[...], wpb_ref[...], preferred_element_type=F32)
    y = jax.nn.sigmoid(ga) * a + jax.nn.sigmoid(gb) * b
    o_ref[...] = h + gt_ref[0] * jnp.dot(y.astype(BF16), wo_ref[...], preferred_element_type=F32)


def _mixout(h, sh, sc, mix_w, w_big, ha, hb, gt, w_pa, w_pb, w_o, seq):
    t, d = h.shape
    tm = min(MIX_ROWS, seq)
    per_b = seq // tm
    assert GATE_GROUP_COLS == 2 * d and COL_GATE_GROUP % GATE_GROUP_COLS == 0
    resident = pl.Buffered(1)
    wspec = pl.BlockSpec((d, d), lambda i: (0, 0), pipeline_mode=resident)
    bspec = pl.BlockSpec((1, 1, d), lambda i: (i // per_b, 0, 0))
    return pl.pallas_call(
        _mixout_body,
        grid=(t // tm,),
        in_specs=[pl.BlockSpec((tm, d), lambda i: (i, 0)),
                  bspec, bspec,
                  pl.BlockSpec((1, d), lambda i: (0, 0)),
                  pl.BlockSpec((d, 2 * d), lambda i: (0, COL_GATE_GROUP // GATE_GROUP_COLS),
                               pipeline_mode=resident),
                  pl.BlockSpec((tm, d), lambda i: (i, 0)),
                  pl.BlockSpec((tm, d), lambda i: (i, 0)),
                  bspec,
                  wspec, wspec, wspec],
        out_specs=pl.BlockSpec((tm, d), lambda i: (i, 0)),
        out_shape=jax.ShapeDtypeStruct((t, d), F32),
        compiler_params=_cparams(("parallel",)),
        name="mix_out",
    )(h, sh, sc, mix_w, w_big, ha, hb, gt, w_pa, w_pb, w_o)


def _route_body(h_ref, sh_ref, sc_ref, nw_ref, rwt_ref, rb_ref,
                u_ref, eidx_ref, wts_ref, pos_ref, cnt_ref, carry_scr):
    tm = h_ref.shape[0]

    @pl.when(pl.program_id(0) == 0)
    def _():
        carry_scr[...] = jnp.zeros(carry_scr.shape, F32)

    u = _modulated_rmsnorm(h_ref[...], nw_ref[...], sc_ref[0], sh_ref[0])
    half = u.shape[1] // 2
    u_ref[...] = _pack_bf16_pair(u[:, :half], u[:, half:])
    logits = _split2_dot_nt(rwt_ref[...], u)
    scores = jax.nn.sigmoid(logits)
    sel = scores + rb_ref[...]

    neg = -jnp.inf
    sub_io = lax.broadcasted_iota(I32, (GROUP_SIZE, tm), 0)
    pieces = []
    for g in range(N_GROUPS):
        blk = sel[g * GROUP_SIZE:(g + 1) * GROUP_SIZE, :]
        m1 = jnp.max(blk, axis=0, keepdims=True)
        first = jnp.min(jnp.where(blk == m1, sub_io, GROUP_SIZE), axis=0, keepdims=True)
        m2 = jnp.max(jnp.where(sub_io == first, neg, blk), axis=0, keepdims=True)
        pieces.append(jnp.broadcast_to(m1 + m2, (GROUP_SIZE, tm)))
    gscore = jnp.concatenate(pieces, axis=0)

    e_io = lax.broadcasted_iota(I32, (N_EXPERTS, tm), 0)
    grp_io = e_io // GROUP_SIZE
    gmask = jnp.zeros((N_EXPERTS, tm), jnp.bool_)
    for _ in range(TOPK_GROUPS):
        mx = jnp.max(gscore, axis=0, keepdims=True)
        gi = jnp.min(jnp.where(gscore == mx, grp_io, N_GROUPS), axis=0, keepdims=True)
        hit = grp_io == gi
        gmask = jnp.logical_or(gmask, hit)
        gscore = jnp.where(hit, neg, gscore)

    cur = jnp.where(gmask, sel, neg)
    row_io = lax.broadcasted_iota(I32, (TOP_K, tm), 0)
    eidx = jnp.zeros((TOP_K, tm), I32)
    wraw = jnp.zeros((TOP_K, tm), F32)
    chosen = jnp.zeros((N_EXPERTS, tm), jnp.bool_)
    hits = []
    for kk in range(TOP_K):
        mx = jnp.max(cur, axis=0, keepdims=True)
        ei = jnp.min(jnp.where(cur == mx, e_io, N_EXPERTS), axis=0, keepdims=True)
        hit = e_io == ei
        hits.append(hit)
        sc_k = jnp.sum(jnp.where(hit, scores, 0.0), axis=0, keepdims=True)
        eidx = jnp.where(row_io == kk, ei, eidx)
        wraw = jnp.where(row_io == kk, sc_k, wraw)
        chosen = jnp.logical_or(chosen, hit)
        cur = jnp.where(hit, neg, cur)

    wsum = jnp.sum(wraw, axis=0, keepdims=True)
    wts_ref[...] = wraw / wsum * ROUTED_SCALE
    eidx_ref[...] = eidx

    chosen_f = jnp.where(chosen, 1.0, 0.0)
    r_io = lax.broadcasted_iota(I32, (tm, tm), 0)
    c_io = lax.broadcasted_iota(I32, (tm, tm), 1)
    strict_upper = jnp.where(r_io < c_io, 1.0, 0.0).astype(BF16)
    prefix = jnp.dot(chosen_f.astype(BF16), strict_upper, preferred_element_type=F32)
    rank = prefix + carry_scr[:, 0:1]
    pos = jnp.zeros((TOP_K, tm), F32)
    for kk in range(TOP_K):
        p_k = jnp.sum(jnp.where(hits[kk], rank, 0.0), axis=0, keepdims=True)
        pos = jnp.where(row_io == kk, p_k, pos)
    pos_ref[...] = pos.astype(I32)
    total = carry_scr[...] + jnp.sum(chosen_f, axis=1, keepdims=True)
    carry_scr[...] = total
    cnt_ref[...] = total.astype(I32)


def _route(h, sh, sc, nw, rw_t, rb_col, seq):
    t, d = h.shape
    tm = min(ROUTE_ROWS, seq)
    per_b = seq // tm
    kspec = pl.BlockSpec((TOP_K, tm), lambda i: (0, i))
    return pl.pallas_call(
        _route_body,
        grid=(t // tm,),
        in_specs=[pl.BlockSpec((tm, d), lambda i: (i, 0)),
                  pl.BlockSpec((1, 1, d), lambda i: (i // per_b, 0, 0)),
                  pl.BlockSpec((1, 1, d), lambda i: (i // per_b, 0, 0)),
                  pl.BlockSpec((1, d), lambda i: (0, 0)),
                  pl.BlockSpec((N_EXPERTS, d), lambda i: (0, 0)),
                  pl.BlockSpec((N_EXPERTS, 1), lambda i: (0, 0))],
        out_specs=[pl.BlockSpec((tm, d // 2), lambda i: (i, 0)), kspec, kspec, kspec,
                   pl.BlockSpec((N_EXPERTS, LANES), lambda i: (0, 0))],
        out_shape=[jax.ShapeDtypeStruct((t, d // 2), U32),
                   jax.ShapeDtypeStruct((TOP_K, t), I32),
                   jax.ShapeDtypeStruct((TOP_K, t), F32),
                   jax.ShapeDtypeStruct((TOP_K, t), I32),
                   jax.ShapeDtypeStruct((N_EXPERTS, LANES), I32)],
        scratch_shapes=[pltpu.VMEM((N_EXPERTS, LANES), F32)],
        compiler_params=_cparams(("arbitrary",)),
        name="moe_route",
    )(h, sh, sc, nw, rw_t, rb_col)


def _sc_worker_id():
    return lax.axis_index("s") * SC_CORES + lax.axis_index("c")


def _sc_scatter_rows(x, dest, n_rows):
    t, d = x.shape
    n_k = dest.shape[0]
    assert t % (SC_WORKERS * 2 * SC_CHUNK) == 0
    per_w = t // SC_WORKERS
    n_ch = per_w // SC_CHUNK
    dest4 = dest.reshape(n_k, SC_WORKERS, n_ch, SC_CHUNK).transpose(1, 2, 0, 3)
    mesh = plsc.VectorSubcoreMesh(core_axis_name="c", subcore_axis_name="s")

    @functools.partial(
        pl.kernel, mesh=mesh,
        out_type=jax.ShapeDtypeStruct((n_rows, d), x.dtype),
        scratch_types=[pltpu.VMEM((n_ch, n_k, SC_CHUNK), I32),
                       pltpu.VMEM((SC_CHUNK, d), x.dtype),
                       pltpu.VMEM((SC_CHUNK, d), x.dtype)] + [pltpu.SemaphoreType.DMA] * 4,
        name="moe_dispatch_sc",
    )
    def scatter_kernel(x_hbm, dest_hbm, out_hbm, idx_v, rows0, rows1, l0, l1, s0, s1):
        wid = _sc_worker_id()
        base = wid * per_w
        pltpu.sync_copy(dest_hbm.at[wid], idx_v)
        bufs = ((rows0, l0, s0), (rows1, l1, s1))

        def load(ci, b):
            rows, load_sem, _ = bufs[b]
            return pltpu.make_async_copy(x_hbm.at[pl.ds(base + ci * SC_CHUNK, SC_CHUNK)], rows,
                                         load_sem)

        def scatters(ci, b):
            rows, _, scatter_sem = bufs[b]
            return [pltpu.make_async_copy(rows, out_hbm.at[idx_v.at[ci, j]], scatter_sem)
                    for j in range(n_k)]

        load(0, 0).start()

        @pl.loop(0, n_ch, step=2)
        def _(ci):
            load(ci, 0).wait()
            for cp in scatters(ci, 0):
                cp.start()

            @pl.when(ci > 0)
            def _():
                for cp in scatters(ci - 1, 1):
                    cp.wait()

            load(ci + 1, 1).start()
            load(ci + 1, 1).wait()
            for cp in scatters(ci + 1, 1):
                cp.start()
            for cp in scatters(ci, 0):
                cp.wait()

            @pl.when(ci + 2 < n_ch)
            def _():
                load(ci + 2, 0).start()

        for cp in scatters(n_ch - 1, 1):
            cp.wait()

    return scatter_kernel(x, dest4)


def _sc_gather_rows(table, idx):
    n, d = idx.shape[0], table.shape[1]
    assert n % (SC_WORKERS * 2 * SC_CHUNK) == 0
    per_w = n // SC_WORKERS
    n_ch = per_w // SC_CHUNK
    idx3 = idx.reshape(SC_WORKERS, n_ch, SC_CHUNK)
    mesh = plsc.VectorSubcoreMesh(core_axis_name="c", subcore_axis_name="s")

    @functools.partial(
        pl.kernel, mesh=mesh,
        out_type=jax.ShapeDtypeStruct((n, d), table.dtype),
        scratch_types=[pltpu.VMEM((n_ch, SC_CHUNK), I32),
                       pltpu.VMEM((SC_CHUNK, d), table.dtype),
                       pltpu.VMEM((SC_CHUNK, d), table.dtype)] + [pltpu.SemaphoreType.DMA] * 4,
        name="moe_combine_sc",
    )
    def gather_kernel(table_hbm, idx_hbm, out_hbm, idx_v, rows0, rows1, g0, g1, w0, w1):
        wid = _sc_worker_id()
        base = wid * per_w
        pltpu.sync_copy(idx_hbm.at[wid], idx_v)
        bufs = ((rows0, g0, w0), (rows1, g1, w1))

        def gather(ci, b):
            rows, gather_sem, _ = bufs[b]
            return pltpu.make_async_copy(table_hbm.at[idx_v.at[ci]], rows, gather_sem)

        def write(ci, b):
            rows, _, write_sem = bufs[b]
            return pltpu.make_async_copy(rows, out_hbm.at[pl.ds(base + ci * SC_CHUNK, SC_CHUNK)],
                                         write_sem)

        gather(0, 0).start()

        @pl.loop(0, n_ch, step=2)
        def _(ci):
            gather(ci, 0).wait()
            write(ci, 0).start()

            @pl.when(ci > 0)
            def _():
                write(ci - 1, 1).wait()

            gather(ci + 1, 1).start()
            gather(ci + 1, 1).wait()
            write(ci + 1, 1).start()
            write(ci, 0).wait()

            @pl.when(ci + 2 < n_ch)
            def _():
                gather(ci + 2, 0).start()

        write(n_ch - 1, 1).wait()

    return gather_kernel(table, idx3)


def _expert_body(estart_ref, ecount_ref, x_hbm, w1_ref, w3_ref, w2_ref, y_hbm,
                 xbuf, ybuf, w1_scr, w3_scr, w2_scr, in_sem, out_sem):
    e = pl.program_id(0)
    start = estart_ref[e]
    count = ecount_ref[e]
    n_blk = (count + EXPERT_ROWS - 1) // EXPERT_ROWS
    half = xbuf.shape[2]

    piece = EXPERT_ROWS // EXPERT_DMA_PIECES

    class _Pieces:
        def __init__(self, copies):
            self.copies = copies

        def start(self):
            for cp in self.copies:
                cp.start()

        def wait(self):
            for cp in self.copies:
                cp.wait()

    def hbm_rows(b, p):
        return pl.ds(pl.multiple_of(start + b * EXPERT_ROWS + p * piece, piece), piece)

    def read(b, slot):
        return _Pieces([pltpu.make_async_copy(x_hbm.at[hbm_rows(b, p)],
                                              xbuf.at[slot, pl.ds(p * piece, piece)], in_sem.at[slot])
                        for p in range(EXPERT_DMA_PIECES)])

    def write(b, slot):
        return _Pieces([pltpu.make_async_copy(ybuf.at[slot, pl.ds(p * piece, piece)],
                                              y_hbm.at[hbm_rows(b, p)], out_sem.at[slot])
                        for p in range(EXPERT_DMA_PIECES)])

    @pl.when(n_blk > 0)
    def _():
        read(0, 0).start()
        w1_scr[...] = w1_ref[0].astype(BF16)
        w3_scr[...] = w3_ref[0].astype(BF16)
        w2_scr[...] = w2_ref[0].astype(BF16)

    def block(b, carry):
        slot = b % 2
        read(b, slot).wait()

        @pl.when(b + 1 < n_blk)
        def _():
            read(b + 1, 1 - slot).start()

        @pl.when(b >= 2)
        def _():
            write(b - 2, slot).wait()

        valid = count - b * EXPERT_ROWS
        rows = lax.broadcasted_iota(I32, (EXPERT_ROWS, half), 0)
        lo, hi = _unpack_bf16_pair(jnp.where(rows < valid, xbuf[slot], 0))
        lo, hi = lo.astype(BF16), hi.astype(BF16)

        def proj(w_scr):
            return (jnp.dot(lo, w_scr[0:half, :], preferred_element_type=F32)
                    + jnp.dot(hi, w_scr[half:2 * half, :], preferred_element_type=F32))

        hid = _silu(proj(w1_scr)) * proj(w3_scr)
        y = jnp.dot(hid.astype(BF16), w2_scr[...], preferred_element_type=F32)
        ybuf[slot] = _pack_bf16_pair(y[:, :half], y[:, half:])
        write(b, slot).start()
        return carry

    lax.fori_loop(0, n_blk, block, 0)

    @pl.when(n_blk >= 2)
    def _():
        write(n_blk - 2, n_blk % 2).wait()

    @pl.when(n_blk >= 1)
    def _():
        write(n_blk - 1, (n_blk - 1) % 2).wait()


def _experts(xg, estart, ecount, w1, w3, w2, layer):
    n_rows, half = xg.shape
    d, de = w1.shape[-2:]
    grid_spec = pltpu.PrefetchScalarGridSpec(
        num_scalar_prefetch=2,
        grid=(N_EXPERTS,),
        in_specs=[pl.BlockSpec(memory_space=pl.ANY),
                  pl.BlockSpec((None, 1, d, de), lambda e, es, ec: (layer, e, 0, 0)),
                  pl.BlockSpec((None, 1, d, de), lambda e, es, ec: (layer, e, 0, 0)),
                  pl.BlockSpec((None, 1, de, d), lambda e, es, ec: (layer, e, 0, 0))],
        out_specs=pl.BlockSpec(memory_space=pl.ANY),
        scratch_shapes=[pltpu.VMEM((2, EXPERT_ROWS, half), U32), pltpu.VMEM((2, EXPERT_ROWS, half), U32),
                        pltpu.VMEM((d, de), BF16), pltpu.VMEM((d, de), BF16),
                        pltpu.VMEM((de, d), BF16),
                        pltpu.SemaphoreType.DMA((2,)), pltpu.SemaphoreType.DMA((2,))],
    )
    return pl.pallas_call(
        _expert_body,
        grid_spec=grid_spec,
        out_shape=jax.ShapeDtypeStruct((n_rows, half), U32),
        compiler_params=_cparams(("arbitrary",)),
        name="moe_experts",
    )(estart, ecount, xg, w1, w3, w2)


def _combine_body(h_ref, u_ref, yg_ref, wts_ref, gt_ref, s1_ref, s3_ref, s2_ref, fw_ref, o_ref, *,
                  final_norm):
    half = u_ref.shape[1]
    lo, hi = _unpack_bf16_pair(u_ref[...])
    lo, hi = lo.astype(BF16), hi.astype(BF16)

    def proj(w_ref):
        return (jnp.dot(lo, w_ref[0:half, :], preferred_element_type=F32)
                + jnp.dot(hi, w_ref[half:2 * half, :], preferred_element_type=F32))

    hid = _silu(proj(s1_ref)) * proj(s3_ref)
    shared = jnp.dot(hid.astype(BF16), s2_ref[...], preferred_element_type=F32)

    routed_lo = routed_hi = None
    for kk in range(TOP_K):
        y_lo, y_hi = _unpack_bf16_pair(yg_ref[kk])
        w = wts_ref[:, kk:kk + 1]
        routed_lo = y_lo * w if routed_lo is None else routed_lo + y_lo * w
        routed_hi = y_hi * w if routed_hi is None else routed_hi + y_hi * w
    gt = gt_ref[0]
    out_lo = h_ref[:, 0:half] + gt[:, 0:half] * (routed_lo + shared[:, 0:half])
    out_hi = (h_ref[:, half:2 * half]
              + gt[:, half:2 * half] * (routed_hi + shared[:, half:2 * half]))
    if final_norm:
        ssq = (jnp.sum(out_lo * out_lo, axis=-1, keepdims=True)
               + jnp.sum(out_hi * out_hi, axis=-1, keepdims=True))
        inv = lax.rsqrt(ssq / (2 * half) + EPS)
        out_lo = (out_lo * inv) * fw_ref[:, 0:half]
        out_hi = (out_hi * inv) * fw_ref[:, half:2 * half]
    o_ref[:, 0:half] = out_lo
    o_ref[:, half:2 * half] = out_hi


def _combine(h, u, yg, wts_tk, gt, s1, s3, s2, final_w, seq, final_norm, part, n_parts):
    t, d = h.shape
    half = d // 2
    tm = min(COMBINE_ROWS, seq)
    per_b = seq // tm
    ds_ = s1.shape[-1]
    steps = t // n_parts // tm
    off = part * steps
    return pl.pallas_call(
        functools.partial(_combine_body, final_norm=final_norm),
        grid=(steps,),
        in_specs=[pl.BlockSpec((tm, d), lambda i: (i + off, 0)),
                  pl.BlockSpec((tm, half), lambda i: (i + off, 0)),
                  pl.BlockSpec((TOP_K, tm, half), lambda i: (0, i, 0)),
                  pl.BlockSpec((tm, TOP_K), lambda i: (i + off, 0)),
                  pl.BlockSpec((1, 1, d), lambda i: ((i + off) // per_b, 0, 0)),
                  pl.BlockSpec((d, ds_), lambda i: (0, 0)),
                  pl.BlockSpec((d, ds_), lambda i: (0, 0)),
                  pl.BlockSpec((ds_, d), lambda i: (0, 0)),
                  pl.BlockSpec((1, d), lambda i: (0, 0))],
        out_specs=pl.BlockSpec((tm, d), lambda i: (i + off, 0)),
        out_shape=jax.ShapeDtypeStruct((t, d), F32),
        input_output_aliases={0: 0},
        compiler_params=_cparams(("parallel",)),
        name="moe_combine",
    )(h, u, yg, wts_tk, gt, s1, s3, s2, final_w)


def _split_w_in(w_in):
    sizes = (2 * M_HEADS * M_DQK, M_HEADS * M_DV, M_HEADS * M_DV, M_HEADS, M_HEADS,
             G_HEADS * G_DK, G_HEADS * G_DK, G_HEADS * G_DV, G_RANK, G_HEADS * G_DV,
             D_MODEL, D_MODEL)
    offs = [0]
    for n in sizes:
        offs.append(offs[-1] + n)
    w16 = w_in.astype(BF16)
    big = jnp.concatenate([w16[:, offs[0]:offs[3]], w16[:, offs[5]:offs[8]], w16[:, offs[9]:offs[12]]],
                          axis=1)
    pad = jnp.zeros((w_in.shape[0], SMALL_COLS - 2 * M_HEADS - G_RANK), BF16)
    small = jnp.concatenate([w16[:, offs[3]:offs[5]], w16[:, offs[8]:offs[9]], pad], axis=1)
    return big, small


def _moe_layout(counts, eidx, pos):
    padded = (counts + EXPERT_ROWS - 1) // EXPERT_ROWS * EXPERT_ROWS
    pstart = jnp.cumsum(padded) - padded
    experts = jnp.arange(N_EXPERTS, dtype=I32)
    dest = pos + jnp.sum(jnp.where(eidx[..., None] == experts, pstart, 0), axis=-1)
    return dest.astype(I32), pstart.astype(I32), counts.astype(I32)


def kernel(x, c, ada_w, ada_b, norm_mix_w, norm_moe_w, w_in, m_conv_w, m_gate_b, m_norm_w,
           g_alpha_w, g_alpha_b, g_norm_w, w_pa, w_pb, w_o, router_w, router_b,
           exp_w1, exp_w3, exp_w2, sh_w1, sh_w3, sh_w2, final_norm_w):
    bsz, seq, d = x.shape
    depth = ada_w.shape[0]
    t = bsz * seq
    n_rows = t * TOP_K + N_EXPERTS * EXPERT_ROWS

    ada = _ada(c, ada_w, ada_b).reshape(depth, bsz, 6, 1, d)
    h = x.reshape(t, d)
    for l in range(depth):
        sh1, sc1, gt1, sh2, sc2, gt2 = (ada[l, :, i] for i in range(6))

        w_big, w_small = _split_w_in(w_in[l])
        mix_w = norm_mix_w[l][None, :]
        gate_row = jnp.zeros((1, SMALL_COLS), F32)
        gate_row = gate_row.at[0, SMALL_I:SMALL_I + M_HEADS].set(m_gate_b[l, 0])
        gate_row = gate_row.at[0, SMALL_F:SMALL_F + M_HEADS].set(m_gate_b[l, 1])
        ha = _mlstm(h, sh1, sc1, mix_w, w_big, w_small, m_conv_w[l], gate_row, m_norm_w[l][None, :],
                    bsz, seq)
        alpha_full = jnp.zeros((SMALL_COLS, G_HEADS * G_DK), F32)
        alpha_full = alpha_full.at[SMALL_R:SMALL_R + G_RANK].set(g_alpha_w[l]).astype(BF16)
        hb = _gla(h, sh1, sc1, mix_w, w_big, w_small, alpha_full, g_alpha_b[l][None, :],
                  g_norm_w[l][None, :], bsz, seq)
        h = _mixout(h, sh1, sc1, mix_w, w_big, ha, hb, gt1, w_pa[l].astype(BF16),
                    w_pb[l].astype(BF16), w_o[l].astype(BF16), seq)

        u, eidx, wts, pos, cnt = _route(h, sh2, sc2, norm_moe_w[l][None, :],
                                        router_w[l].T, router_b[l][:, None], seq)
        dest, estart, ecount = _moe_layout(cnt[:, 0], eidx, pos)
        xg = _sc_scatter_rows(u, dest, n_rows)
        y = _experts(xg, estart, ecount, exp_w1, exp_w3, exp_w2, l)
        tp = t // COMBINE_PARTS
        for p in range(COMBINE_PARTS):
            dest_p = dest[:, p * tp:(p + 1) * tp].reshape(-1)
            yg = _sc_gather_rows(y, dest_p).reshape(TOP_K, tp, d // 2)
            h = _combine(h, u, yg, wts.T, gt2, sh_w1[l].astype(BF16), sh_w3[l].astype(BF16),
                         sh_w2[l].astype(BF16), final_norm_w[None, :], seq,
                         final_norm=(l == depth - 1), part=p, n_parts=COMBINE_PARTS)

    return h.reshape(bsz, seq, d)
```

```python
import functools

import jax
import jax.numpy as jnp
import numpy as np
from jax import lax
from jax.experimental import pallas as pl
from jax.experimental.pallas import tpu as pltpu
from jax.experimental.pallas import tpu_sc as plsc

F32 = jnp.float32
BF16 = jnp.bfloat16
I32 = jnp.int32
U32 = jnp.uint32
HI_MASK = np.uint32(0xFFFF0000)

LANES = 128

SC_CORES = 2
SC_SUBCORES = 16
SC_WORKERS = SC_CORES * SC_SUBCORES
SC_CHUNK = 64

D_MODEL = 1024
M_HEADS = 4
M_DQK = 128
M_DV = 256
M_CONV = 4
GATE_CAP = 15.0
G_HEADS = 4
G_DK = 128
G_DV = 256
G_RANK = 16
G_TAU = 16.0
G_CHUNK = 64
N_EXPERTS = 64
TOP_K = 8
N_GROUPS = 8
GROUP_SIZE = N_EXPERTS // N_GROUPS
TOPK_GROUPS = 4
ROUTED_SCALE = 2.5
EPS = 1e-6

M_CHUNK_ROWS = 256
G_BLOCK_ROWS = 256
SEQ_GROUP = 2
PROJ_PIECE_COLS = 256
MIX_ROWS = 512
ROUTE_ROWS = 1024
EXPERT_ROWS = 512
EXPERT_SLOTS = 4
COMBINE_ROWS = 512
COMBINE_PARTS = 4
CONV_HALO = 8
VMEM_LIMIT = 48 * 1024 * 1024

M_GROUP_COLS = 2 * M_HEADS * M_DQK + 2 * M_HEADS * M_DV
G_GROUP_COLS = 2 * G_HEADS * G_DK + 2 * G_HEADS * G_DV
GATE_GROUP_COLS = 2 * D_MODEL
COL_M_GROUP = 0
COL_G_GROUP = COL_M_GROUP + M_GROUP_COLS
COL_GATE_GROUP = COL_G_GROUP + G_GROUP_COLS
SMALL_COLS = LANES
SMALL_I, SMALL_F, SMALL_R = 0, M_HEADS, 2 * M_HEADS


def _cparams(sem, vmem=VMEM_LIMIT):
    return pltpu.CompilerParams(dimension_semantics=sem, vmem_limit_bytes=vmem)


def _silu(x):
    return x * jax.nn.sigmoid(x)


def _log_sigmoid(x):
    return jnp.minimum(x, 0.0) - jnp.log1p(jnp.exp(-jnp.abs(x)))


def _modulated_rmsnorm(x, w, sc, sh):
    y = x * lax.rsqrt(jnp.mean(x * x, axis=-1, keepdims=True) + EPS)
    return (y * w) * (1.0 + sc) + sh


def _pack_bf16_pair(lo, hi):
    lo_bits = lax.bitcast_convert_type(lo.astype(BF16).astype(F32), U32)
    hi_bits = lax.bitcast_convert_type(hi.astype(BF16).astype(F32), U32)
    return (lo_bits >> 16) | (hi_bits & HI_MASK)


def _unpack_bf16_pair(packed):
    lo = lax.bitcast_convert_type(packed << 16, F32)
    hi = lax.bitcast_convert_type(packed & HI_MASK, F32)
    return lo, hi


def _lower_tri(n, dtype):
    r = lax.broadcasted_iota(I32, (n, n), 0)
    c = lax.broadcasted_iota(I32, (n, n), 1)
    return (r >= c).astype(dtype)


def _split3_dot(lhs01, x):
    hi = x.astype(BF16)
    r1 = x - hi.astype(F32)
    mid = r1.astype(BF16)
    lo = (r1 - mid.astype(F32)).astype(BF16)
    return (jnp.dot(lhs01, hi, preferred_element_type=F32)
            + jnp.dot(lhs01, mid, preferred_element_type=F32)
            + jnp.dot(lhs01, lo, preferred_element_type=F32))


def _split2_dot_nt(a, b):
    nt = (((1,), (1,)), ((), ()))
    a_hi = a.astype(BF16)
    a_lo = (a - a_hi.astype(F32)).astype(BF16)
    b_hi = b.astype(BF16)
    b_lo = (b - b_hi.astype(F32)).astype(BF16)
    return (lax.dot_general(a_hi, b_hi, nt, preferred_element_type=F32)
            + lax.dot_general(a_hi, b_lo, nt, preferred_element_type=F32)
            + lax.dot_general(a_lo, b_hi, nt, preferred_element_type=F32))


def _ada_body(c_ref, w_ref, b_ref, o_ref):
    cond = _silu(c_ref[...])
    o_ref[0] = jnp.dot(cond.astype(BF16), w_ref[0].astype(BF16),
                       preferred_element_type=F32) + b_ref[0]


def _ada(c, ada_w, ada_b):
    depth, d, six_d = ada_w.shape
    bsz = c.shape[0]
    nj = six_d // d
    return pl.pallas_call(
        _ada_body,
        grid=(depth, nj),
        in_specs=[pl.BlockSpec((bsz, d), lambda l, j: (0, 0)),
                  pl.BlockSpec((1, d, d), lambda l, j: (l, 0, j)),
                  pl.BlockSpec((1, 1, d), lambda l, j: (l, 0, j))],
        out_specs=pl.BlockSpec((1, bsz, d), lambda l, j: (l, 0, j)),
        out_shape=jax.ShapeDtypeStruct((depth, bsz, six_d), F32),
        compiler_params=_cparams(("parallel", "parallel")),
        name="ada_ln",
    )(c, ada_w, ada_b.reshape(depth, 1, six_d))


def _mlstm_body(h0_ref, hn_ref, sh_ref, sc_ref, mw_ref, w_ref, ws_ref, cw_ref, gb_ref, nw_ref,
                ha_ref, qk_nxt, v_nxt, o_nxt, sm_nxt, xe_scr, v_scr, o_scr, sm_scr,
                c_scr, n_scr, m_scr):
    n_seq, rows = hn_ref.shape[0], hn_ref.shape[1]
    half = M_HEADS * M_DQK
    vcols = M_HEADS * M_DV

    def projection_pieces(h_ref):
        u = jnp.concatenate(
            [_modulated_rmsnorm(h_ref[g], mw_ref[...], sc_ref[g], sh_ref[g]) for g in range(n_seq)],
            axis=0).astype(BF16)

        def piece(dst, w_lo, lo, width):
            def run():
                dst[:, lo:lo + width] = jnp.dot(u, w_ref[:, w_lo + lo:w_lo + lo + width],
                                                preferred_element_type=F32).astype(dst.dtype)
            return run

        def small_piece():
            sm_nxt[...] = jnp.dot(u, ws_ref[...], preferred_element_type=F32)

        pieces = [small_piece]
        for dst, w_lo in ((qk_nxt, 0), (v_nxt, 2 * half), (o_nxt, 2 * half + vcols)):
            for lo in range(0, vcols, PROJ_PIECE_COLS):
                pieces.append(piece(dst, w_lo, lo, PROJ_PIECE_COLS))
        return pieces

    @pl.when(pl.program_id(1) == 0)
    def _():
        xe_scr[:, 0:CONV_HALO, :] = jnp.zeros((n_seq, CONV_HALO, 2 * half), F32)
        c_scr[...] = jnp.zeros(c_scr.shape, F32)
        n_scr[...] = jnp.zeros(n_scr.shape, F32)
        m_scr[...] = jnp.zeros(m_scr.shape, F32)
        for run in projection_pieces(h0_ref):
            run()

    for g in range(n_seq):
        xe_scr[g, CONV_HALO:CONV_HALO + rows, :] = qk_nxt[g * rows:(g + 1) * rows, :]
    v_scr[...] = v_nxt[...]
    o_scr[...] = o_nxt[...]
    sm_scr[...] = sm_nxt[...]
    pending = projection_pieces(hn_ref)

    r_io = lax.broadcasted_iota(I32, (rows, rows), 0)
    c_io = lax.broadcasted_iota(I32, (rows, rows), 1)
    causal = r_io >= c_io
    tri = _lower_tri(rows, BF16)

    for g in range(n_seq):
        gr = slice(g * rows, (g + 1) * rows)
        conv = None
        for j in range(M_CONV):
            off = CONV_HALO - (M_CONV - 1) + j
            term = xe_scr[g, off:off + rows, :] * cw_ref[j:j + 1, :]
            conv = term if conv is None else conv + term
        qk = _silu(conv)
        xe_scr[g, 0:CONV_HALO, :] = xe_scr[g, rows:rows + CONV_HALO, :]

        capped = GATE_CAP * jnp.tanh((sm_scr[gr, :] + gb_ref[...]) / GATE_CAP)
        li_all = capped
        lf_all = _log_sigmoid(capped)
        b_all = _split3_dot(tri, lf_all)
        li_t = li_all.T
        b_t = b_all.T

        for h in range(M_HEADS):
            if pending:
                pending.pop(0)()
            sidx = g * M_HEADS + h
            q = (qk[:, h * M_DQK:(h + 1) * M_DQK] * (M_DQK ** -0.5)).astype(BF16)
            k = qk[:, half + h * M_DQK:half + (h + 1) * M_DQK]
            kb = k.astype(BF16)
            v = v_scr[gr, h * M_DV:(h + 1) * M_DV]
            li_c = li_all[:, SMALL_I + h:SMALL_I + h + 1]
            b_c = b_all[:, SMALL_F + h:SMALL_F + h + 1]
            li_r = li_t[SMALL_I + h:SMALL_I + h + 1, :]
            b_r = b_t[SMALL_F + h:SMALL_F + h + 1, :]
            gsum = b_c[rows - 1:rows, :]
            m_prev = m_scr[sidx][:, 0:1]
            c_prev = c_scr[sidx]
            n_prev = n_scr[sidx]

            d_mat = jnp.where(causal, b_c - b_r + li_r, -jnp.inf)
            m_inter = b_c + m_prev
            m_t = jnp.maximum(jnp.max(d_mat, axis=1, keepdims=True), m_inter)
            s = lax.dot_general(q, kb, (((1,), (1,)), ((), ())), preferred_element_type=F32)
            p = jnp.exp(d_mat - m_t) * s
            w_inter = jnp.exp(m_inter - m_t)
            num = (jnp.dot(p.astype(BF16), v, preferred_element_type=F32)
                   + w_inter * jnp.dot(q, c_prev.astype(BF16), preferred_element_type=F32))
            qn = jnp.sum(q.astype(F32) * n_prev, axis=1, keepdims=True)
            den = jnp.sum(p, axis=1, keepdims=True) + w_inter * qn
            hh = num / jnp.maximum(jnp.abs(den), jnp.exp(-m_t))

            if pending:
                pending.pop(0)()

            a_r = gsum - b_r + li_r
            a_c = gsum - b_c + li_c
            m_new = jnp.maximum(gsum + m_prev, jnp.max(a_r, axis=1, keepdims=True))
            decay = jnp.exp(gsum + m_prev - m_new)
            wk = jnp.exp(a_c - m_new) * k
            c_scr[sidx] = decay * c_prev + lax.dot_general(
                wk.astype(BF16), v, (((0,), (0,)), ((), ())), preferred_element_type=F32)
            n_scr[sidx] = decay * n_prev + jnp.sum(wk, axis=0, keepdims=True)
            m_scr[sidx] = jnp.broadcast_to(m_new, m_scr.shape[1:])

            y = hh * lax.rsqrt(jnp.mean(hh * hh, axis=-1, keepdims=True) + EPS)
            y = y * nw_ref[:, h * M_DV:(h + 1) * M_DV]
            gate = jax.nn.sigmoid(o_scr[gr, h * M_DV:(h + 1) * M_DV])
            ha_ref[g, :, h * M_DV:(h + 1) * M_DV] = (y * gate).astype(BF16)
    for run in pending:
        run()


def _mlstm(h, sh, sc, mix_w, w_big, w_small, conv_w, gate_row, norm_w, bsz, seq):
    t, d = h.shape
    rows = min(M_CHUNK_ROWS, seq)
    nc = seq // rows
    ng = SEQ_GROUP if bsz % SEQ_GROUP == 0 else 1
    half = M_HEADS * M_DQK
    vcols = M_HEADS * M_DV
    wcols = M_GROUP_COLS
    assert COL_M_GROUP % wcols == 0
    resident = pl.Buffered(1)
    nxt = [pltpu.VMEM((ng * rows, 2 * half), F32), pltpu.VMEM((ng * rows, vcols), BF16),
           pltpu.VMEM((ng * rows, vcols), F32), pltpu.VMEM((ng * rows, SMALL_COLS), F32)]
    out = pl.pallas_call(
        _mlstm_body,
        grid=(bsz // ng, nc),
        in_specs=[pl.BlockSpec((ng, rows, d), lambda b, c: (b, 0, 0)),
                  pl.BlockSpec((ng, rows, d), lambda b, c: (b, jnp.minimum(c + 1, nc - 1), 0)),
                  pl.BlockSpec((ng, 1, d), lambda b, c: (b, 0, 0)),
                  pl.BlockSpec((ng, 1, d), lambda b, c: (b, 0, 0)),
                  pl.BlockSpec((1, d), lambda b, c: (0, 0)),
                  pl.BlockSpec((d, wcols), lambda b, c: (0, 0), pipeline_mode=resident),
                  pl.BlockSpec((d, SMALL_COLS), lambda b, c: (0, 0), pipeline_mode=resident),
                  pl.BlockSpec((M_CONV, 2 * half), lambda b, c: (0, 0)),
                  pl.BlockSpec((1, SMALL_COLS), lambda b, c: (0, 0)),
                  pl.BlockSpec((1, vcols), lambda b, c: (0, 0))],
        out_specs=pl.BlockSpec((ng, rows, vcols), lambda b, c: (b, c, 0)),
        out_shape=jax.ShapeDtypeStruct((bsz, seq, vcols), BF16),
        scratch_shapes=nxt + [pltpu.VMEM((ng, rows + CONV_HALO, 2 * half), F32)] + nxt[1:]
        + [pltpu.VMEM((ng * M_HEADS, M_DQK, M_DV), F32),
           pltpu.VMEM((ng * M_HEADS, 1, M_DQK), F32),
           pltpu.VMEM((ng * M_HEADS, 1, LANES), F32)],
        compiler_params=_cparams(("parallel", "arbitrary")),
        name="mlstm",
    )(h.reshape(bsz, seq, d), h.reshape(bsz, seq, d), sh, sc, mix_w, w_big, w_small, conv_w,
      gate_row, norm_w)
    return out.reshape(t, vcols)


def _gla_body(h0_ref, hn_ref, sh_ref, sc_ref, mw_ref, w_ref, ws_ref, aw_ref, ab_ref, nw_ref, hb_ref,
              qk_nxt, v_nxt, z_nxt, sm_nxt, qk_scr, v_scr, z_scr, sm_scr, st_scr):
    n_seq, rows = hn_ref.shape[0], hn_ref.shape[1]
    n_chunks = rows // G_CHUNK
    kcols = G_HEADS * G_DK
    vcols = G_HEADS * G_DV

    def projection_pieces(h_ref):
        u = jnp.concatenate(
            [_modulated_rmsnorm(h_ref[g], mw_ref[...], sc_ref[g], sh_ref[g]) for g in range(n_seq)],
            axis=0).astype(BF16)

        def piece(dst, w_lo, lo, width):
            def run():
                dst[:, lo:lo + width] = jnp.dot(u, w_ref[:, w_lo + lo:w_lo + lo + width],
                                                preferred_element_type=F32).astype(dst.dtype)
            return run

        def small_piece():
            sm_nxt[...] = jnp.dot(u, ws_ref[...], preferred_element_type=F32)

        pieces = [small_piece]
        for dst, w_lo in ((qk_nxt, 0), (v_nxt, 2 * kcols), (z_nxt, 2 * kcols + vcols)):
            for lo in range(0, vcols, PROJ_PIECE_COLS):
                pieces.append(piece(dst, w_lo, lo, PROJ_PIECE_COLS))
        return pieces

    @pl.when(pl.program_id(1) == 0)
    def _():
        st_scr[...] = jnp.zeros(st_scr.shape, F32)
        for run in projection_pieces(h0_ref):
            run()

    qk_scr[...] = qk_nxt[...]
    v_scr[...] = v_nxt[...]
    z_scr[...] = z_nxt[...]
    sm_scr[...] = sm_nxt[...]
    pending = projection_pieces(hn_ref)

    r_io = lax.broadcasted_iota(I32, (rows, rows), 0)
    c_io = lax.broadcasted_iota(I32, (rows, rows), 1)
    chunk_causal = jnp.logical_and(r_io >= c_io, r_io // G_CHUNK == c_io // G_CHUNK)
    chunk_tri = jnp.where(chunk_causal, 1.0, 0.0).astype(BF16)

    for g in range(n_seq):
        gr = slice(g * rows, (g + 1) * rows)
        for _ in range((len(pending) + n_seq - 1 - g) // (n_seq - g)):
            pending.pop(0)()
        q = qk_scr[gr, 0:kcols] * (G_DK ** -0.5)
        k = qk_scr[gr, kcols:2 * kcols]

        logits = jnp.dot(sm_scr[gr, :].astype(BF16), aw_ref[...], preferred_element_type=F32)
        la = _log_sigmoid(logits + ab_ref[...]) / G_TAU
        bc = _split3_dot(chunk_tri, la)
        gcs = [bc[(ci + 1) * G_CHUNK - 1:(ci + 1) * G_CHUNK, :] for ci in range(n_chunks)]
        gc_rows = jnp.concatenate([jnp.broadcast_to(gc, (G_CHUNK, kcols)) for gc in gcs], axis=0)

        q_in = (q * jnp.exp(bc)).astype(BF16)
        k_in = (k * jnp.exp(-bc)).astype(BF16)
        k_out = (k * jnp.exp(gc_rows - bc)).astype(BF16)

        for h in range(G_HEADS):
            ks = slice(h * G_DK, (h + 1) * G_DK)
            vs = slice(h * G_DV, (h + 1) * G_DV)
            v = v_scr[gr, vs]
            att = lax.dot_general(q_in[:, ks], k_in[:, ks], (((1,), (1,)), ((), ())),
                                  preferred_element_type=F32)
            att = jnp.where(chunk_causal, att, 0.0).astype(BF16)
            o_intra = jnp.dot(att, v, preferred_element_type=F32)
            st = st_scr[g * G_HEADS + h]
            outs = []
            for ci in range(n_chunks):
                rs = slice(ci * G_CHUNK, (ci + 1) * G_CHUNK)
                o_inter = lax.dot_general(q_in[rs, ks], st.astype(BF16),
                                          (((1,), (1,)), ((), ())), preferred_element_type=F32)
                outs.append(o_intra[rs, :] + o_inter)
                st = jnp.exp(gcs[ci][:, ks]) * st + lax.dot_general(
                    v[rs, :], k_out[rs, ks], (((0,), (0,)), ((), ())), preferred_element_type=F32)
            st_scr[g * G_HEADS + h] = st
            o = jnp.concatenate(outs, axis=0)
            y = o * lax.rsqrt(jnp.mean(o * o, axis=-1, keepdims=True) + EPS)
            y = y * nw_ref[:, vs]
            hb_ref[g, :, vs] = (y * _silu(z_scr[gr, vs])).astype(BF16)


def _gla(h, sh, sc, mix_w, w_big, w_small, alpha_full, alpha_b, norm_w, bsz, seq):
    t, d = h.shape
    rows = min(G_BLOCK_ROWS, seq)
    nb = seq // rows
    ng = SEQ_GROUP if bsz % SEQ_GROUP == 0 else 1
    kcols = G_HEADS * G_DK
    vcols = G_HEADS * G_DV
    wcols = G_GROUP_COLS
    assert COL_G_GROUP % wcols == 0
    resident = pl.Buffered(1)
    nxt = [pltpu.VMEM((ng * rows, 2 * kcols), F32), pltpu.VMEM((ng * rows, vcols), BF16),
           pltpu.VMEM((ng * rows, vcols), F32), pltpu.VMEM((ng * rows, SMALL_COLS), F32)]
    out = pl.pallas_call(
        _gla_body,
        grid=(bsz // ng, nb),
        in_specs=[pl.BlockSpec((ng, rows, d), lambda b, c: (b, 0, 0)),
                  pl.BlockSpec((ng, rows, d), lambda b, c: (b, jnp.minimum(c + 1, nb - 1), 0)),
                  pl.BlockSpec((ng, 1, d), lambda b, c: (b, 0, 0)),
                  pl.BlockSpec((ng, 1, d), lambda b, c: (b, 0, 0)),
                  pl.BlockSpec((1, d), lambda b, c: (0, 0)),
                  pl.BlockSpec((d, wcols), lambda b, c: (0, COL_G_GROUP // wcols),
                               pipeline_mode=resident),
                  pl.BlockSpec((d, SMALL_COLS), lambda b, c: (0, 0), pipeline_mode=resident),
                  pl.BlockSpec((SMALL_COLS, kcols), lambda b, c: (0, 0)),
                  pl.BlockSpec((1, kcols), lambda b, c: (0, 0)),
                  pl.BlockSpec((1, vcols), lambda b, c: (0, 0))],
        out_specs=pl.BlockSpec((ng, rows, vcols), lambda b, c: (b, c, 0)),
        out_shape=jax.ShapeDtypeStruct((bsz, seq, vcols), BF16),
        scratch_shapes=nxt + nxt + [pltpu.VMEM((ng * G_HEADS, G_DV, G_DK), F32)],
        compiler_params=_cparams(("parallel", "arbitrary")),
        name="gla",
    )(h.reshape(bsz, seq, d), h.reshape(bsz, seq, d), sh, sc, mix_w, w_big, w_small, alpha_full,
      alpha_b, norm_w)
    return out.reshape(t, vcols)


def _mixout_body(h_ref, sh_ref, sc_ref, mw_ref, wg_ref, ha_ref, hb_ref, gt_ref, wpa_ref, wpb_ref,
                 wo_ref, o_ref):
    d = h_ref.shape[1]
    h = h_ref[...]
    u = _modulated_rmsnorm(h, mw_ref[...], sc_ref[0], sh_ref[0]).astype(BF16)
    ga = jnp.dot(u, wg_ref[:, 0:d], preferred_element_type=F32)
    gb = jnp.dot(u, wg_ref[:, d:2 * d], preferred_element_type=F32)
    a = jnp.dot(ha_ref[...], wpa_ref[...], preferred_element_type=F32)
    b = jnp.dot(hb_ref[...], wpb_ref[...], preferred_element_type=F32)
    y = jax.nn.sigmoid(ga) * a + jax.nn.sigmoid(gb) * b
    o_ref[...] = h + gt_ref[0] * jnp.dot(y.astype(BF16), wo_ref[...], preferred_element_type=F32)


def _mixout(h, sh, sc, mix_w, w_big, ha, hb, gt, w_pa, w_pb, w_o, seq):
    t, d = h.shape
    tm = min(MIX_ROWS, seq)
    per_b = seq // tm
    assert GATE_GROUP_COLS == 2 * d and COL_GATE_GROUP % GATE_GROUP_COLS == 0
    resident = pl.Buffered(1)
    wspec = pl.BlockSpec((d, d), lambda i: (0, 0), pipeline_mode=resident)
    bspec = pl.BlockSpec((1, 1, d), lambda i: (i // per_b, 0, 0))
    return pl.pallas_call(
        _mixout_body,
        grid=(t // tm,),
        in_specs=[pl.BlockSpec((tm, d), lambda i: (i, 0)),
                  bspec, bspec,
                  pl.BlockSpec((1, d), lambda i: (0, 0)),
                  pl.BlockSpec((d, 2 * d), lambda i: (0, COL_GATE_GROUP // GATE_GROUP_COLS),
                               pipeline_mode=resident),
                  pl.BlockSpec((tm, d), lambda i: (i, 0)),
                  pl.BlockSpec((tm, d), lambda i: (i, 0)),
                  bspec,
                  wspec, wspec, wspec],
        out_specs=pl.BlockSpec((tm, d), lambda i: (i, 0)),
        out_shape=jax.ShapeDtypeStruct((t, d), F32),
        compiler_params=_cparams(("parallel",)),
        name="mix_out",
    )(h, sh, sc, mix_w, w_big, ha, hb, gt, w_pa, w_pb, w_o)


def _route_body(h_ref, sh_ref, sc_ref, nw_ref, rwt_ref, rb_ref,
                u_ref, eidx_ref, wts_ref, pos_ref, cnt_ref, carry_scr):
    tm = h_ref.shape[0]

    @pl.when(pl.program_id(0) == 0)
    def _():
        carry_scr[...] = jnp.zeros(carry_scr.shape, F32)

    u = _modulated_rmsnorm(h_ref[...], nw_ref[...], sc_ref[0], sh_ref[0])
    half = u.shape[1] // 2
    u_ref[...] = _pack_bf16_pair(u[:, :half], u[:, half:])
    logits = _split2_dot_nt(rwt_ref[...], u)
    scores = jax.nn.sigmoid(logits)
    sel = scores + rb_ref[...]

    neg = -jnp.inf
    sub_io = lax.broadcasted_iota(I32, (GROUP_SIZE, tm), 0)
    pieces = []
    for g in range(N_GROUPS):
        blk = sel[g * GROUP_SIZE:(g + 1) * GROUP_SIZE, :]
        m1 = jnp.max(blk, axis=0, keepdims=True)
        first = jnp.min(jnp.where(blk == m1, sub_io, GROUP_SIZE), axis=0, keepdims=True)
        m2 = jnp.max(jnp.where(sub_io == first, neg, blk), axis=0, keepdims=True)
        pieces.append(jnp.broadcast_to(m1 + m2, (GROUP_SIZE, tm)))
    gscore = jnp.concatenate(pieces, axis=0)

    e_io = lax.broadcasted_iota(I32, (N_EXPERTS, tm), 0)
    grp_io = e_io // GROUP_SIZE
    gmask = jnp.zeros((N_EXPERTS, tm), jnp.bool_)
    for _ in range(TOPK_GROUPS):
        mx = jnp.max(gscore, axis=0, keepdims=True)
        gi = jnp.min(jnp.where(gscore == mx, grp_io, N_GROUPS), axis=0, keepdims=True)
        hit = grp_io == gi
        gmask = jnp.logical_or(gmask, hit)
        gscore = jnp.where(hit, neg, gscore)

    cur = jnp.where(gmask, sel, neg)
    row_io = lax.broadcasted_iota(I32, (TOP_K, tm), 0)
    eidx = jnp.zeros((TOP_K, tm), I32)
    wraw = jnp.zeros((TOP_K, tm), F32)
    chosen = jnp.zeros((N_EXPERTS, tm), jnp.bool_)
    hits = []
    for kk in range(TOP_K):
        mx = jnp.max(cur, axis=0, keepdims=True)
        ei = jnp.min(jnp.where(cur == mx, e_io, N_EXPERTS), axis=0, keepdims=True)
        hit = e_io == ei
        hits.append(hit)
        sc_k = jnp.sum(jnp.where(hit, scores, 0.0), axis=0, keepdims=True)
        eidx = jnp.where(row_io == kk, ei, eidx)
        wraw = jnp.where(row_io == kk, sc_k, wraw)
        chosen = jnp.logical_or(chosen, hit)
        cur = jnp.where(hit, neg, cur)

    wsum = jnp.sum(wraw, axis=0, keepdims=True)
    wts_ref[...] = wraw / wsum * ROUTED_SCALE
    eidx_ref[...] = eidx

    chosen_f = jnp.where(chosen, 1.0, 0.0)
    r_io = lax.broadcasted_iota(I32, (tm, tm), 0)
    c_io = lax.broadcasted_iota(I32, (tm, tm), 1)
    strict_upper = jnp.where(r_io < c_io, 1.0, 0.0).astype(BF16)
    prefix = jnp.dot(chosen_f.astype(BF16), strict_upper, preferred_element_type=F32)
    rank = prefix + carry_scr[:, 0:1]
    pos = jnp.zeros((TOP_K, tm), F32)
    for kk in range(TOP_K):
        p_k = jnp.sum(jnp.where(hits[kk], rank, 0.0), axis=0, keepdims=True)
        pos = jnp.where(row_io == kk, p_k, pos)
    pos_ref[...] = pos.astype(I32)
    total = carry_scr[...] + jnp.sum(chosen_f, axis=1, keepdims=True)
    carry_scr[...] = total
    cnt_ref[...] = total.astype(I32)


def _route(h, sh, sc, nw, rw_t, rb_col, seq):
    t, d = h.shape
    tm = min(ROUTE_ROWS, seq)
    per_b = seq // tm
    kspec = pl.BlockSpec((TOP_K, tm), lambda i: (0, i))
    return pl.pallas_call(
        _route_body,
        grid=(t // tm,),
        in_specs=[pl.BlockSpec((tm, d), lambda i: (i, 0)),
                  pl.BlockSpec((1, 1, d), lambda i: (i // per_b, 0, 0)),
                  pl.BlockSpec((1, 1, d), lambda i: (i // per_b, 0, 0)),
                  pl.BlockSpec((1, d), lambda i: (0, 0)),
                  pl.BlockSpec((N_EXPERTS, d), lambda i: (0, 0)),
                  pl.BlockSpec((N_EXPERTS, 1), lambda i: (0, 0))],
        out_specs=[pl.BlockSpec((tm, d // 2), lambda i: (i, 0)), kspec, kspec, kspec,
                   pl.BlockSpec((N_EXPERTS, LANES), lambda i: (0, 0))],
        out_shape=[jax.ShapeDtypeStruct((t, d // 2), U32),
                   jax.ShapeDtypeStruct((TOP_K, t), I32),
                   jax.ShapeDtypeStruct((TOP_K, t), F32),
                   jax.ShapeDtypeStruct((TOP_K, t), I32),
                   jax.ShapeDtypeStruct((N_EXPERTS, LANES), I32)],
        scratch_shapes=[pltpu.VMEM((N_EXPERTS, LANES), F32)],
        compiler_params=_cparams(("arbitrary",)),
        name="moe_route",
    )(h, sh, sc, nw, rw_t, rb_col)


def _sc_worker_id():
    return lax.axis_index("s") * SC_CORES + lax.axis_index("c")


def _sc_scatter_rows(x, dest, n_rows):
    t, d = x.shape
    n_k = dest.shape[0]
    assert t % (SC_WORKERS * 2 * SC_CHUNK) == 0
    per_w = t // SC_WORKERS
    n_ch = per_w // SC_CHUNK
    dest4 = dest.reshape(n_k, SC_WORKERS, n_ch, SC_CHUNK).transpose(1, 2, 0, 3)
    mesh = plsc.VectorSubcoreMesh(core_axis_name="c", subcore_axis_name="s")

    @functools.partial(
        pl.kernel, mesh=mesh,
        out_type=jax.ShapeDtypeStruct((n_rows, d), x.dtype),
        scratch_types=[pltpu.VMEM((n_ch, n_k, SC_CHUNK), I32),
                       pltpu.VMEM((SC_CHUNK, d), x.dtype),
                       pltpu.VMEM((SC_CHUNK, d), x.dtype)] + [pltpu.SemaphoreType.DMA] * 4,
        name="moe_dispatch_sc",
    )
    def scatter_kernel(x_hbm, dest_hbm, out_hbm, idx_v, rows0, rows1, l0, l1, s0, s1):
        wid = _sc_worker_id()
        base = wid * per_w
        pltpu.sync_copy(dest_hbm.at[wid], idx_v)
        bufs = ((rows0, l0, s0), (rows1, l1, s1))

        def load(ci, b):
            rows, load_sem, _ = bufs[b]
            return pltpu.make_async_copy(x_hbm.at[pl.ds(base + ci * SC_CHUNK, SC_CHUNK)], rows,
                                         load_sem)

        def scatters(ci, b):
            rows, _, scatter_sem = bufs[b]
            return [pltpu.make_async_copy(rows, out_hbm.at[idx_v.at[ci, j]], scatter_sem)
                    for j in range(n_k)]

        load(0, 0).start()

        @pl.loop(0, n_ch, step=2)
        def _(ci):
            load(ci, 0).wait()
            for cp in scatters(ci, 0):
                cp.start()

            @pl.when(ci > 0)
            def _():
                for cp in scatters(ci - 1, 1):
                    cp.wait()

            load(ci + 1, 1).start()
            load(ci + 1, 1).wait()
            for cp in scatters(ci + 1, 1):
                cp.start()
            for cp in scatters(ci, 0):
                cp.wait()

            @pl.when(ci + 2 < n_ch)
            def _():
                load(ci + 2, 0).start()

        for cp in scatters(n_ch - 1, 1):
            cp.wait()

    return scatter_kernel(x, dest4)


def _sc_gather_rows(table, idx):
    n, d = idx.shape[0], table.shape[1]
    assert n % (SC_WORKERS * 2 * SC_CHUNK) == 0
    per_w = n // SC_WORKERS
    n_ch = per_w // SC_CHUNK
    idx3 = idx.reshape(SC_WORKERS, n_ch, SC_CHUNK)
    mesh = plsc.VectorSubcoreMesh(core_axis_name="c", subcore_axis_name="s")

    @functools.partial(
        pl.kernel, mesh=mesh,
        out_type=jax.ShapeDtypeStruct((n, d), table.dtype),
        scratch_types=[pltpu.VMEM((n_ch, SC_CHUNK), I32),
                       pltpu.VMEM((SC_CHUNK, d), table.dtype),
                       pltpu.VMEM((SC_CHUNK, d), table.dtype)] + [pltpu.SemaphoreType.DMA] * 4,
        name="moe_combine_sc",
    )
    def gather_kernel(table_hbm, idx_hbm, out_hbm, idx_v, rows0, rows1, g0, g1, w0, w1):
        wid = _sc_worker_id()
        base = wid * per_w
        pltpu.sync_copy(idx_hbm.at[wid], idx_v)
        bufs = ((rows0, g0, w0), (rows1, g1, w1))

        def gather(ci, b):
            rows, gather_sem, _ = bufs[b]
            return pltpu.make_async_copy(table_hbm.at[idx_v.at[ci]], rows, gather_sem)

        def write(ci, b):
            rows, _, write_sem = bufs[b]
            return pltpu.make_async_copy(rows, out_hbm.at[pl.ds(base + ci * SC_CHUNK, SC_CHUNK)],
                                         write_sem)

        gather(0, 0).start()

        @pl.loop(0, n_ch, step=2)
        def _(ci):
            gather(ci, 0).wait()
            write(ci, 0).start()

            @pl.when(ci > 0)
            def _():
                write(ci - 1, 1).wait()

            gather(ci + 1, 1).start()
            gather(ci + 1, 1).wait()
            write(ci + 1, 1).start()
            write(ci, 0).wait()

            @pl.when(ci + 2 < n_ch)
            def _():
                gather(ci + 2, 0).start()

        write(n_ch - 1, 1).wait()

    return gather_kernel(table, idx3)


def _expert_body(estart_ref, ecount_ref, x_hbm, w1_ref, w3_ref, w2_ref, y_hbm,
                 xbuf, ybuf, w1_scr, w3_scr, w2_scr, in_sem, out_sem):
    e = pl.program_id(0)
    start = estart_ref[e]
    count = ecount_ref[e]
    n_blk = (count + EXPERT_ROWS - 1) // EXPERT_ROWS
    half = xbuf.shape[2]

    def rows_at(b):
        return pl.ds(pl.multiple_of(start + b * EXPERT_ROWS, EXPERT_ROWS), EXPERT_ROWS)

    def read(b, slot):
        return pltpu.make_async_copy(x_hbm.at[rows_at(b)], xbuf.at[slot], in_sem.at[slot])

    def write(b, slot):
        return pltpu.make_async_copy(ybuf.at[slot], y_hbm.at[rows_at(b)], out_sem.at[slot])

    n_slots = xbuf.shape[0]
    lookahead = n_slots - 1

    for b0 in range(lookahead):
        @pl.when(b0 < n_blk)
        def _(b0=b0):
            read(b0, b0).start()

    @pl.when(n_blk > 0)
    def _():
        w1_scr[...] = w1_ref[0].astype(BF16)
        w3_scr[...] = w3_ref[0].astype(BF16)
        w2_scr[...] = w2_ref[0].astype(BF16)

    def block(b, carry):
        slot = b % n_slots
        read(b, slot).wait()

        @pl.when(b + lookahead < n_blk)
        def _():
            read(b + lookahead, (b + lookahead) % n_slots).start()

        @pl.when(b >= n_slots)
        def _():
            write(b - n_slots, slot).wait()

        valid = count - b * EXPERT_ROWS
        rows = lax.broadcasted_iota(I32, (EXPERT_ROWS, half), 0)
        lo, hi = _unpack_bf16_pair(jnp.where(rows < valid, xbuf[slot], 0))
        lo, hi = lo.astype(BF16), hi.astype(BF16)

        def proj(w_scr):
            return (jnp.dot(lo, w_scr[0:half, :], preferred_element_type=F32)
                    + jnp.dot(hi, w_scr[half:2 * half, :], preferred_element_type=F32))

        hid = _silu(proj(w1_scr)) * proj(w3_scr)
        y = jnp.dot(hid.astype(BF16), w2_scr[...], preferred_element_type=F32)
        ybuf[slot] = _pack_bf16_pair(y[:, :half], y[:, half:])
        write(b, slot).start()
        return carry

    lax.fori_loop(0, n_blk, block, 0)

    for back in range(n_slots, 0, -1):
        @pl.when(n_blk >= back)
        def _(back=back):
            write(n_blk - back, (n_blk - back) % n_slots).wait()


def _experts(xg, estart, ecount, w1, w3, w2, layer):
    n_rows, half = xg.shape
    d, de = w1.shape[-2:]
    grid_spec = pltpu.PrefetchScalarGridSpec(
        num_scalar_prefetch=2,
        grid=(N_EXPERTS,),
        in_specs=[pl.BlockSpec(memory_space=pl.ANY),
                  pl.BlockSpec((None, 1, d, de), lambda e, es, ec: (layer, e, 0, 0)),
                  pl.BlockSpec((None, 1, d, de), lambda e, es, ec: (layer, e, 0, 0)),
                  pl.BlockSpec((None, 1, de, d), lambda e, es, ec: (layer, e, 0, 0))],
        out_specs=pl.BlockSpec(memory_space=pl.ANY),
        scratch_shapes=[pltpu.VMEM((EXPERT_SLOTS, EXPERT_ROWS, half), U32),
                        pltpu.VMEM((EXPERT_SLOTS, EXPERT_ROWS, half), U32),
                        pltpu.VMEM((d, de), BF16), pltpu.VMEM((d, de), BF16),
                        pltpu.VMEM((de, d), BF16),
                        pltpu.SemaphoreType.DMA((EXPERT_SLOTS,)),
                        pltpu.SemaphoreType.DMA((EXPERT_SLOTS,))],
    )
    return pl.pallas_call(
        _expert_body,
        grid_spec=grid_spec,
        out_shape=jax.ShapeDtypeStruct((n_rows, half), U32),
        compiler_params=_cparams(("arbitrary",)),
        name="moe_experts",
    )(estart, ecount, xg, w1, w3, w2)


def _combine_body(h_ref, u_ref, yg_ref, wts_ref, gt_ref, s1_ref, s3_ref, s2_ref, fw_ref, o_ref, *,
                  final_norm):
    half = u_ref.shape[1]
    lo, hi = _unpack_bf16_pair(u_ref[...])
    lo, hi = lo.astype(BF16), hi.astype(BF16)

    def proj(w_ref):
        return (jnp.dot(lo, w_ref[0:half, :], preferred_element_type=F32)
                + jnp.dot(hi, w_ref[half:2 * half, :], preferred_element_type=F32))

    hid = _silu(proj(s1_ref)) * proj(s3_ref)
    shared = jnp.dot(hid.astype(BF16), s2_ref[...], preferred_element_type=F32)

    routed_lo = routed_hi = None
    for kk in range(TOP_K):
        y_lo, y_hi = _unpack_bf16_pair(yg_ref[kk])
        w = wts_ref[:, kk:kk + 1]
        routed_lo = y_lo * w if routed_lo is None else routed_lo + y_lo * w
        routed_hi = y_hi * w if routed_hi is None else routed_hi + y_hi * w
    gt = gt_ref[0]
    out_lo = h_ref[:, 0:half] + gt[:, 0:half] * (routed_lo + shared[:, 0:half])
    out_hi = (h_ref[:, half:2 * half]
              + gt[:, half:2 * half] * (routed_hi + shared[:, half:2 * half]))
    if final_norm:
        ssq = (jnp.sum(out_lo * out_lo, axis=-1, keepdims=True)
               + jnp.sum(out_hi * out_hi, axis=-1, keepdims=True))
        inv = lax.rsqrt(ssq / (2 * half) + EPS)
        out_lo = (out_lo * inv) * fw_ref[:, 0:half]
        out_hi = (out_hi * inv) * fw_ref[:, half:2 * half]
    o_ref[:, 0:half] = out_lo
    o_ref[:, half:2 * half] = out_hi


def _combine(h, u, yg, wts_tk, gt, s1, s3, s2, final_w, seq, final_norm, part, n_parts):
    t, d = h.shape
    half = d // 2
    tm = min(COMBINE_ROWS, seq)
    per_b = seq // tm
    ds_ = s1.shape[-1]
    steps = t // n_parts // tm
    off = part * steps
    return pl.pallas_call(
        functools.partial(_combine_body, final_norm=final_norm),
        grid=(steps,),
        in_specs=[pl.BlockSpec((tm, d), lambda i: (i + off, 0)),
                  pl.BlockSpec((tm, half), lambda i: (i + off, 0)),
                  pl.BlockSpec((TOP_K, tm, half), lambda i: (0, i, 0)),
                  pl.BlockSpec((tm, TOP_K), lambda i: (i + off, 0)),
                  pl.BlockSpec((1, 1, d), lambda i: ((i + off) // per_b, 0, 0)),
                  pl.BlockSpec((d, ds_), lambda i: (0, 0)),
                  pl.BlockSpec((d, ds_), lambda i: (0, 0)),
                  pl.BlockSpec((ds_, d), lambda i: (0, 0)),
                  pl.BlockSpec((1, d), lambda i: (0, 0))],
        out_specs=pl.BlockSpec((tm, d), lambda i: (i + off, 0)),
        out_shape=jax.ShapeDtypeStruct((t, d), F32),
        input_output_aliases={0: 0},
        compiler_params=_cparams(("parallel",)),
        name="moe_combine",
    )(h, u, yg, wts_tk, gt, s1, s3, s2, final_w)


def _split_w_in(w_in):
    sizes = (2 * M_HEADS * M_DQK, M_HEADS * M_DV, M_HEADS * M_DV, M_HEADS, M_HEADS,
             G_HEADS * G_DK, G_HEADS * G_DK, G_HEADS * G_DV, G_RANK, G_HEADS * G_DV,
             D_MODEL, D_MODEL)
    offs = [0]
    for n in sizes:
        offs.append(offs[-1] + n)
    w16 = w_in.astype(BF16)
    big = jnp.concatenate([w16[:, offs[0]:offs[3]], w16[:, offs[5]:offs[8]], w16[:, offs[9]:offs[12]]],
                          axis=1)
    pad = jnp.zeros((w_in.shape[0], SMALL_COLS - 2 * M_HEADS - G_RANK), BF16)
    small = jnp.concatenate([w16[:, offs[3]:offs[5]], w16[:, offs[8]:offs[9]], pad], axis=1)
    return big, small


def _moe_layout(counts, eidx, pos):
    padded = (counts + EXPERT_ROWS - 1) // EXPERT_ROWS * EXPERT_ROWS
    pstart = jnp.cumsum(padded) - padded
    experts = jnp.arange(N_EXPERTS, dtype=I32)
    dest = pos + jnp.sum(jnp.where(eidx[..., None] == experts, pstart, 0), axis=-1)
    return dest.astype(I32), pstart.astype(I32), counts.astype(I32)


def kernel(x, c, ada_w, ada_b, norm_mix_w, norm_moe_w, w_in, m_conv_w, m_gate_b, m_norm_w,
           g_alpha_w, g_alpha_b, g_norm_w, w_pa, w_pb, w_o, router_w, router_b,
           exp_w1, exp_w3, exp_w2, sh_w1, sh_w3, sh_w2, final_norm_w):
    bsz, seq, d = x.shape
    depth = ada_w.shape[0]
    t = bsz * seq
    n_rows = t * TOP_K + N_EXPERTS * EXPERT_ROWS

    ada = _ada(c, ada_w, ada_b).reshape(depth, bsz, 6, 1, d)
    h = x.reshape(t, d)
    for l in range(depth):
        sh1, sc1, gt1, sh2, sc2, gt2 = (ada[l, :, i] for i in range(6))

        w_big, w_small = _split_w_in(w_in[l])
        mix_w = norm_mix_w[l][None, :]
        gate_row = jnp.zeros((1, SMALL_COLS), F32)
        gate_row = gate_row.at[0, SMALL_I:SMALL_I + M_HEADS].set(m_gate_b[l, 0])
        gate_row = gate_row.at[0, SMALL_F:SMALL_F + M_HEADS].set(m_gate_b[l, 1])
        ha = _mlstm(h, sh1, sc1, mix_w, w_big, w_small, m_conv_w[l], gate_row, m_norm_w[l][None, :],
                    bsz, seq)
        alpha_full = jnp.zeros((SMALL_COLS, G_HEADS * G_DK), F32)
        alpha_full = alpha_full.at[SMALL_R:SMALL_R + G_RANK].set(g_alpha_w[l]).astype(BF16)
        hb = _gla(h, sh1, sc1, mix_w, w_big, w_small, alpha_full, g_alpha_b[l][None, :],
                  g_norm_w[l][None, :], bsz, seq)
        h = _mixout(h, sh1, sc1, mix_w, w_big, ha, hb, gt1, w_pa[l].astype(BF16),
                    w_pb[l].astype(BF16), w_o[l].astype(BF16), seq)

        u, eidx, wts, pos, cnt = _route(h, sh2, sc2, norm_moe_w[l][None, :],
                                        router_w[l].T, router_b[l][:, None], seq)
        dest, estart, ecount = _moe_layout(cnt[:, 0], eidx, pos)
        xg = _sc_scatter_rows(u, dest, n_rows)
        y = _experts(xg, estart, ecount, exp_w1, exp_w3, exp_w2, l)
        tp = t // COMBINE_PARTS
        for p in range(COMBINE_PARTS):
            dest_p = dest[:, p * tp:(p + 1) * tp].reshape(-1)
            yg = _sc_gather_rows(y, dest_p).reshape(TOP_K, tp, d // 2)
            h = _combine(h, u, yg, wts.T, gt2, sh_w1[l].astype(BF16), sh_w3[l].astype(BF16),
                         sh_w2[l].astype(BF16), final_norm_w[None, :], seq,
                         final_norm=(l == depth - 1), part=p, n_parts=COMBINE_PARTS)

    return h.reshape(bsz, seq, d)
```

```python
import functools

import jax
import jax.numpy as jnp
import numpy as np
from jax import lax
from jax.experimental import pallas as pl
from jax.experimental.pallas import tpu as pltpu
from jax.experimental.pallas import tpu_sc as plsc

F32 = jnp.float32
BF16 = jnp.bfloat16
I32 = jnp.int32
U32 = jnp.uint32
HI_MASK = np.uint32(0xFFFF0000)

LANES = 128

SC_CORES = 2
SC_SUBCORES = 16
SC_WORKERS = SC_CORES * SC_SUBCORES
SC_CHUNK = 64

D_MODEL = 1024
M_HEADS = 4
M_DQK = 128
M_DV = 256
M_CONV = 4
GATE_CAP = 15.0
G_HEADS = 4
G_DK = 128
G_DV = 256
G_RANK = 16
G_TAU = 16.0
G_CHUNK = 64
N_EXPERTS = 64
TOP_K = 8
N_GROUPS = 8
GROUP_SIZE = N_EXPERTS // N_GROUPS
TOPK_GROUPS = 4
ROUTED_SCALE = 2.5
EPS = 1e-6

M_CHUNK_ROWS = 256
G_BLOCK_ROWS = 256
SEQ_GROUP = 2
PROJ_PIECE_COLS = 256
MIX_ROWS = 512
ROUTE_ROWS = 1024
EXPERT_ROWS = 1024
EXPERT_SLOTS = 3
COMBINE_ROWS = 512
COMBINE_PARTS = 4
CONV_HALO = 8
VMEM_LIMIT = 48 * 1024 * 1024

M_GROUP_COLS = 2 * M_HEADS * M_DQK + 2 * M_HEADS * M_DV
G_GROUP_COLS = 2 * G_HEADS * G_DK + 2 * G_HEADS * G_DV
GATE_GROUP_COLS = 2 * D_MODEL
COL_M_GROUP = 0
COL_G_GROUP = COL_M_GROUP + M_GROUP_COLS
COL_GATE_GROUP = COL_G_GROUP + G_GROUP_COLS
SMALL_COLS = LANES
SMALL_I, SMALL_F, SMALL_R = 0, M_HEADS, 2 * M_HEADS


def _cparams(sem, vmem=VMEM_LIMIT):
    return pltpu.CompilerParams(dimension_semantics=sem, vmem_limit_bytes=vmem)


def _silu(x):
    return x * jax.nn.sigmoid(x)


def _log_sigmoid(x):
    return jnp.minimum(x, 0.0) - jnp.log1p(jnp.exp(-jnp.abs(x)))


def _modulated_rmsnorm(x, w, sc, sh):
    y = x * lax.rsqrt(jnp.mean(x * x, axis=-1, keepdims=True) + EPS)
    return (y * w) * (1.0 + sc) + sh


def _pack_bf16_pair(lo, hi):
    lo_bits = lax.bitcast_convert_type(lo.astype(BF16).astype(F32), U32)
    hi_bits = lax.bitcast_convert_type(hi.astype(BF16).astype(F32), U32)
    return (lo_bits >> 16) | (hi_bits & HI_MASK)


def _unpack_bf16_pair(packed):
    lo = lax.bitcast_convert_type(packed << 16, F32)
    hi = lax.bitcast_convert_type(packed & HI_MASK, F32)
    return lo, hi


def _lower_tri(n, dtype):
    r = lax.broadcasted_iota(I32, (n, n), 0)
    c = lax.broadcasted_iota(I32, (n, n), 1)
    return (r >= c).astype(dtype)


def _split3_dot(lhs01, x):
    hi = x.astype(BF16)
    r1 = x - hi.astype(F32)
    mid = r1.astype(BF16)
    lo = (r1 - mid.astype(F32)).astype(BF16)
    return (jnp.dot(lhs01, hi, preferred_element_type=F32)
            + jnp.dot(lhs01, mid, preferred_element_type=F32)
            + jnp.dot(lhs01, lo, preferred_element_type=F32))


def _split2_dot_nt(a, b):
    nt = (((1,), (1,)), ((), ()))
    a_hi = a.astype(BF16)
    a_lo = (a - a_hi.astype(F32)).astype(BF16)
    b_hi = b.astype(BF16)
    b_lo = (b - b_hi.astype(F32)).astype(BF16)
    return (lax.dot_general(a_hi, b_hi, nt, preferred_element_type=F32)
            + lax.dot_general(a_hi, b_lo, nt, preferred_element_type=F32)
            + lax.dot_general(a_lo, b_hi, nt, preferred_element_type=F32))


def _ada_body(c_ref, w_ref, b_ref, o_ref):
    cond = _silu(c_ref[...])
    o_ref[0] = jnp.dot(cond.astype(BF16), w_ref[0].astype(BF16),
                       preferred_element_type=F32) + b_ref[0]


def _ada(c, ada_w, ada_b):
    depth, d, six_d = ada_w.shape
    bsz = c.shape[0]
    nj = six_d // d
    return pl.pallas_call(
        _ada_body,
        grid=(depth, nj),
        in_specs=[pl.BlockSpec((bsz, d), lambda l, j: (0, 0)),
                  pl.BlockSpec((1, d, d), lambda l, j: (l, 0, j)),
                  pl.BlockSpec((1, 1, d), lambda l, j: (l, 0, j))],
        out_specs=pl.BlockSpec((1, bsz, d), lambda l, j: (l, 0, j)),
        out_shape=jax.ShapeDtypeStruct((depth, bsz, six_d), F32),
        compiler_params=_cparams(("parallel", "parallel")),
        name="ada_ln",
    )(c, ada_w, ada_b.reshape(depth, 1, six_d))


def _mlstm_body(h0_ref, hn_ref, sh_ref, sc_ref, mw_ref, w_ref, ws_ref, cw_ref, gb_ref, nw_ref,
                ha_ref, qk_nxt, v_nxt, o_nxt, sm_nxt, xe_scr, v_scr, o_scr, sm_scr,
                c_scr, n_scr, m_scr):
    n_seq, rows = hn_ref.shape[0], hn_ref.shape[1]
    half = M_HEADS * M_DQK
    vcols = M_HEADS * M_DV

    def projection_pieces(h_ref):
        u = jnp.concatenate(
            [_modulated_rmsnorm(h_ref[g], mw_ref[...], sc_ref[g], sh_ref[g]) for g in range(n_seq)],
            axis=0).astype(BF16)

        def piece(dst, w_lo, lo, width):
            def run():
                dst[:, lo:lo + width] = jnp.dot(u, w_ref[:, w_lo + lo:w_lo + lo + width],
                                                preferred_element_type=F32).astype(dst.dtype)
            return run

        def small_piece():
            sm_nxt[...] = jnp.dot(u, ws_ref[...], preferred_element_type=F32)

        pieces = [small_piece]
        for dst, w_lo in ((qk_nxt, 0), (v_nxt, 2 * half), (o_nxt, 2 * half + vcols)):
            for lo in range(0, vcols, PROJ_PIECE_COLS):
                pieces.append(piece(dst, w_lo, lo, PROJ_PIECE_COLS))
        return pieces

    @pl.when(pl.program_id(1) == 0)
    def _():
        xe_scr[:, 0:CONV_HALO, :] = jnp.zeros((n_seq, CONV_HALO, 2 * half), F32)
        c_scr[...] = jnp.zeros(c_scr.shape, F32)
        n_scr[...] = jnp.zeros(n_scr.shape, F32)
        m_scr[...] = jnp.zeros(m_scr.shape, F32)
        for run in projection_pieces(h0_ref):
            run()

    for g in range(n_seq):
        xe_scr[g, CONV_HALO:CONV_HALO + rows, :] = qk_nxt[g * rows:(g + 1) * rows, :]
    v_scr[...] = v_nxt[...]
    o_scr[...] = o_nxt[...]
    sm_scr[...] = sm_nxt[...]
    pending = projection_pieces(hn_ref)

    r_io = lax.broadcasted_iota(I32, (rows, rows), 0)
    c_io = lax.broadcasted_iota(I32, (rows, rows), 1)
    causal = r_io >= c_io
    tri = _lower_tri(rows, BF16)

    for g in range(n_seq):
        gr = slice(g * rows, (g + 1) * rows)
        conv = None
        for j in range(M_CONV):
            off = CONV_HALO - (M_CONV - 1) + j
            term = xe_scr[g, off:off + rows, :] * cw_ref[j:j + 1, :]
            conv = term if conv is None else conv + term
        qk = _silu(conv)
        xe_scr[g, 0:CONV_HALO, :] = xe_scr[g, rows:rows + CONV_HALO, :]

        capped = GATE_CAP * jnp.tanh((sm_scr[gr, :] + gb_ref[...]) / GATE_CAP)
        li_all = capped
        lf_all = _log_sigmoid(capped)
        b_all = _split3_dot(tri, lf_all)
        li_t = li_all.T
        b_t = b_all.T

        for h in range(M_HEADS):
            if pending:
                pending.pop(0)()
            sidx = g * M_HEADS + h
            q = (qk[:, h * M_DQK:(h + 1) * M_DQK] * (M_DQK ** -0.5)).astype(BF16)
            k = qk[:, half + h * M_DQK:half + (h + 1) * M_DQK]
            kb = k.astype(BF16)
            v = v_scr[gr, h * M_DV:(h + 1) * M_DV]
            li_c = li_all[:, SMALL_I + h:SMALL_I + h + 1]
            b_c = b_all[:, SMALL_F + h:SMALL_F + h + 1]
            li_r = li_t[SMALL_I + h:SMALL_I + h + 1, :]
            b_r = b_t[SMALL_F + h:SMALL_F + h + 1, :]
            gsum = b_c[rows - 1:rows, :]
            m_prev = m_scr[sidx][:, 0:1]
            c_prev = c_scr[sidx]
            n_prev = n_scr[sidx]

            d_mat = jnp.where(causal, b_c - b_r + li_r, -jnp.inf)
            m_inter = b_c + m_prev
            m_t = jnp.maximum(jnp.max(d_mat, axis=1, keepdims=True), m_inter)
            s = lax.dot_general(q, kb, (((1,), (1,)), ((), ())), preferred_element_type=F32)
            p = jnp.exp(d_mat - m_t) * s
            w_inter = jnp.exp(m_inter - m_t)
            num = (jnp.dot(p.astype(BF16), v, preferred_element_type=F32)
                   + w_inter * jnp.dot(q, c_prev.astype(BF16), preferred_element_type=F32))
            qn = jnp.sum(q.astype(F32) * n_prev, axis=1, keepdims=True)
            den = jnp.sum(p, axis=1, keepdims=True) + w_inter * qn
            hh = num / jnp.maximum(jnp.abs(den), jnp.exp(-m_t))

            if pending:
                pending.pop(0)()

            a_r = gsum - b_r + li_r
            a_c = gsum - b_c + li_c
            m_new = jnp.maximum(gsum + m_prev, jnp.max(a_r, axis=1, keepdims=True))
            decay = jnp.exp(gsum + m_prev - m_new)
            wk = jnp.exp(a_c - m_new) * k
            c_scr[sidx] = decay * c_prev + lax.dot_general(
                wk.astype(BF16), v, (((0,), (0,)), ((), ())), preferred_element_type=F32)
            n_scr[sidx] = decay * n_prev + jnp.sum(wk, axis=0, keepdims=True)
            m_scr[sidx] = jnp.broadcast_to(m_new, m_scr.shape[1:])

            y = hh * lax.rsqrt(jnp.mean(hh * hh, axis=-1, keepdims=True) + EPS)
            y = y * nw_ref[:, h * M_DV:(h + 1) * M_DV]
            gate = jax.nn.sigmoid(o_scr[gr, h * M_DV:(h + 1) * M_DV])
            ha_ref[g, :, h * M_DV:(h + 1) * M_DV] = (y * gate).astype(BF16)
    for run in pending:
        run()


def _mlstm(h, sh, sc, mix_w, w_big, w_small, conv_w, gate_row, norm_w, bsz, seq):
    t, d = h.shape
    rows = min(M_CHUNK_ROWS, seq)
    nc = seq // rows
    ng = SEQ_GROUP if bsz % SEQ_GROUP == 0 else 1
    half = M_HEADS * M_DQK
    vcols = M_HEADS * M_DV
    wcols = M_GROUP_COLS
    assert COL_M_GROUP % wcols == 0
    resident = pl.Buffered(1)
    nxt = [pltpu.VMEM((ng * rows, 2 * half), F32), pltpu.VMEM((ng * rows, vcols), BF16),
           pltpu.VMEM((ng * rows, vcols), F32), pltpu.VMEM((ng * rows, SMALL_COLS), F32)]
    out = pl.pallas_call(
        _mlstm_body,
        grid=(bsz // ng, nc),
        in_specs=[pl.BlockSpec((ng, rows, d), lambda b, c: (b, 0, 0)),
                  pl.BlockSpec((ng, rows, d), lambda b, c: (b, jnp.minimum(c + 1, nc - 1), 0)),
                  pl.BlockSpec((ng, 1, d), lambda b, c: (b, 0, 0)),
                  pl.BlockSpec((ng, 1, d), lambda b, c: (b, 0, 0)),
                  pl.BlockSpec((1, d), lambda b, c: (0, 0)),
                  pl.BlockSpec((d, wcols), lambda b, c: (0, 0), pipeline_mode=resident),
                  pl.BlockSpec((d, SMALL_COLS), lambda b, c: (0, 0), pipeline_mode=resident),
                  pl.BlockSpec((M_CONV, 2 * half), lambda b, c: (0, 0)),
                  pl.BlockSpec((1, SMALL_COLS), lambda b, c: (0, 0)),
                  pl.BlockSpec((1, vcols), lambda b, c: (0, 0))],
        out_specs=pl.BlockSpec((ng, rows, vcols), lambda b, c: (b, c, 0)),
        out_shape=jax.ShapeDtypeStruct((bsz, seq, vcols), BF16),
        scratch_shapes=nxt + [pltpu.VMEM((ng, rows + CONV_HALO, 2 * half), F32)] + nxt[1:]
        + [pltpu.VMEM((ng * M_HEADS, M_DQK, M_DV), F32),
           pltpu.VMEM((ng * M_HEADS, 1, M_DQK), F32),
           pltpu.VMEM((ng * M_HEADS, 1, LANES), F32)],
        compiler_params=_cparams(("parallel", "arbitrary")),
        name="mlstm",
    )(h.reshape(bsz, seq, d), h.reshape(bsz, seq, d), sh, sc, mix_w, w_big, w_small, conv_w,
      gate_row, norm_w)
    return out.reshape(t, vcols)


def _gla_body(h0_ref, hn_ref, sh_ref, sc_ref, mw_ref, w_ref, ws_ref, aw_ref, ab_ref, nw_ref, hb_ref,
              qk_nxt, v_nxt, z_nxt, sm_nxt, qk_scr, v_scr, z_scr, sm_scr, st_scr):
    n_seq, rows = hn_ref.shape[0], hn_ref.shape[1]
    n_chunks = rows // G_CHUNK
    kcols = G_HEADS * G_DK
    vcols = G_HEADS * G_DV

    def projection_pieces(h_ref):
        u = jnp.concatenate(
            [_modulated_rmsnorm(h_ref[g], mw_ref[...], sc_ref[g], sh_ref[g]) for g in range(n_seq)],
            axis=0).astype(BF16)

        def piece(dst, w_lo, lo, width):
            def run():
                dst[:, lo:lo + width] = jnp.dot(u, w_ref[:, w_lo + lo:w_lo + lo + width],
                                                preferred_element_type=F32).astype(dst.dtype)
            return run

        def small_piece():
            sm_nxt[...] = jnp.dot(u, ws_ref[...], preferred_element_type=F32)

        pieces = [small_piece]
        for dst, w_lo in ((qk_nxt, 0), (v_nxt, 2 * kcols), (z_nxt, 2 * kcols + vcols)):
            for lo in range(0, vcols, PROJ_PIECE_COLS):
                pieces.append(piece(dst, w_lo, lo, PROJ_PIECE_COLS))
        return pieces

    @pl.when(pl.program_id(1) == 0)
    def _():
        st_scr[...] = jnp.zeros(st_scr.shape, F32)
        for run in projection_pieces(h0_ref):
            run()

    qk_scr[...] = qk_nxt[...]
    v_scr[...] = v_nxt[...]
    z_scr[...] = z_nxt[...]
    sm_scr[...] = sm_nxt[...]
    pending = projection_pieces(hn_ref)

    r_io = lax.broadcasted_iota(I32, (rows, rows), 0)
    c_io = lax.broadcasted_iota(I32, (rows, rows), 1)
    chunk_causal = jnp.logical_and(r_io >= c_io, r_io // G_CHUNK == c_io // G_CHUNK)
    chunk_tri = jnp.where(chunk_causal, 1.0, 0.0).astype(BF16)

    for g in range(n_seq):
        gr = slice(g * rows, (g + 1) * rows)
        for _ in range((len(pending) + n_seq - 1 - g) // (n_seq - g)):
            pending.pop(0)()
        q = qk_scr[gr, 0:kcols] * (G_DK ** -0.5)
        k = qk_scr[gr, kcols:2 * kcols]

        logits = jnp.dot(sm_scr[gr, :].astype(BF16), aw_ref[...], preferred_element_type=F32)
        la = _log_sigmoid(logits + ab_ref[...]) / G_TAU
        bc = _split3_dot(chunk_tri, la)
        gcs = [bc[(ci + 1) * G_CHUNK - 1:(ci + 1) * G_CHUNK, :] for ci in range(n_chunks)]
        gc_rows = jnp.concatenate([jnp.broadcast_to(gc, (G_CHUNK, kcols)) for gc in gcs], axis=0)

        q_in = (q * jnp.exp(bc)).astype(BF16)
        k_in = (k * jnp.exp(-bc)).astype(BF16)
        k_out = (k * jnp.exp(gc_rows - bc)).astype(BF16)

        for h in range(G_HEADS):
            ks = slice(h * G_DK, (h + 1) * G_DK)
            vs = slice(h * G_DV, (h + 1) * G_DV)
            v = v_scr[gr, vs]
            att = lax.dot_general(q_in[:, ks], k_in[:, ks], (((1,), (1,)), ((), ())),
                                  preferred_element_type=F32)
            att = jnp.where(chunk_causal, att, 0.0).astype(BF16)
            o_intra = jnp.dot(att, v, preferred_element_type=F32)
            st = st_scr[g * G_HEADS + h]
            outs = []
            for ci in range(n_chunks):
                rs = slice(ci * G_CHUNK, (ci + 1) * G_CHUNK)
                o_inter = lax.dot_general(q_in[rs, ks], st.astype(BF16),
                                          (((1,), (1,)), ((), ())), preferred_element_type=F32)
                outs.append(o_intra[rs, :] + o_inter)
                st = jnp.exp(gcs[ci][:, ks]) * st + lax.dot_general(
                    v[rs, :], k_out[rs, ks], (((0,), (0,)), ((), ())), preferred_element_type=F32)
            st_scr[g * G_HEADS + h] = st
            o = jnp.concatenate(outs, axis=0)
            y = o * lax.rsqrt(jnp.mean(o * o, axis=-1, keepdims=True) + EPS)
            y = y * nw_ref[:, vs]
            hb_ref[g, :, vs] = (y * _silu(z_scr[gr, vs])).astype(BF16)


def _gla(h, sh, sc, mix_w, w_big, w_small, alpha_full, alpha_b, norm_w, bsz, seq):
    t, d = h.shape
    rows = min(G_BLOCK_ROWS, seq)
    nb = seq // rows
    ng = SEQ_GROUP if bsz % SEQ_GROUP == 0 else 1
    kcols = G_HEADS * G_DK
    vcols = G_HEADS * G_DV
    wcols = G_GROUP_COLS
    assert COL_G_GROUP % wcols == 0
    resident = pl.Buffered(1)
    nxt = [pltpu.VMEM((ng * rows, 2 * kcols), F32), pltpu.VMEM((ng * rows, vcols), BF16),
           pltpu.VMEM((ng * rows, vcols), F32), pltpu.VMEM((ng * rows, SMALL_COLS), F32)]
    out = pl.pallas_call(
        _gla_body,
        grid=(bsz // ng, nb),
        in_specs=[pl.BlockSpec((ng, rows, d), lambda b, c: (b, 0, 0)),
                  pl.BlockSpec((ng, rows, d), lambda b, c: (b, jnp.minimum(c + 1, nb - 1), 0)),
                  pl.BlockSpec((ng, 1, d), lambda b, c: (b, 0, 0)),
                  pl.BlockSpec((ng, 1, d), lambda b, c: (b, 0, 0)),
                  pl.BlockSpec((1, d), lambda b, c: (0, 0)),
                  pl.BlockSpec((d, wcols), lambda b, c: (0, COL_G_GROUP // wcols),
                               pipeline_mode=resident),
                  pl.BlockSpec((d, SMALL_COLS), lambda b, c: (0, 0), pipeline_mode=resident),
                  pl.BlockSpec((SMALL_COLS, kcols), lambda b, c: (0, 0)),
                  pl.BlockSpec((1, kcols), lambda b, c: (0, 0)),
                  pl.BlockSpec((1, vcols), lambda b, c: (0, 0))],
        out_specs=pl.BlockSpec((ng, rows, vcols), lambda b, c: (b, c, 0)),
        out_shape=jax.ShapeDtypeStruct((bsz, seq, vcols), BF16),
        scratch_shapes=nxt + nxt + [pltpu.VMEM((ng * G_HEADS, G_DV, G_DK), F32)],
        compiler_params=_cparams(("parallel", "arbitrary")),
        name="gla",
    )(h.reshape(bsz, seq, d), h.reshape(bsz, seq, d), sh, sc, mix_w, w_big, w_small, alpha_full,
      alpha_b, norm_w)
    return out.reshape(t, vcols)


def _mixout_body(h_ref, sh_ref, sc_ref, mw_ref, wg_ref, ha_ref, hb_ref, gt_ref, wpa_ref, wpb_ref,
                 wo_ref, o_ref):
    d = h_ref.shape[1]
    h = h_ref[...]
    u = _modulated_rmsnorm(h, mw_ref[...], sc_ref[0], sh_ref[0]).astype(BF16)
    ga = jnp.dot(u, wg_ref[:, 0:d], preferred_element_type=F32)
    gb = jnp.dot(u, wg_ref[:, d:2 * d], preferred_element_type=F32)
    a = jnp.dot(ha_ref[...], wpa_ref[...], preferred_element_type=F32)
    b = jnp.dot(hb_ref[...], wpb_ref[...], preferred_element_type=F32)
    y = jax.nn.sigmoid(ga) * a + jax.nn.sigmoid(gb) * b
    o_ref[...] = h + gt_ref[0] * jnp.dot(y.astype(BF16), wo_ref[...], preferred_element_type=F32)


def _mixout(h, sh, sc, mix_w, w_big, ha, hb, gt, w_pa, w_pb, w_o, seq):
    t, d = h.shape
    tm = min(MIX_ROWS, seq)
    per_b = seq // tm
    assert GATE_GROUP_COLS == 2 * d and COL_GATE_GROUP % GATE_GROUP_COLS == 0
    resident = pl.Buffered(1)
    wspec = pl.BlockSpec((d, d), lambda i: (0, 0), pipeline_mode=resident)
    bspec = pl.BlockSpec((1, 1, d), lambda i: (i // per_b, 0, 0))
    return pl.pallas_call(
        _mixout_body,
        grid=(t // tm,),
        in_specs=[pl.BlockSpec((tm, d), lambda i: (i, 0)),
                  bspec, bspec,
                  pl.BlockSpec((1, d), lambda i: (0, 0)),
                  pl.BlockSpec((d, 2 * d), lambda i: (0, COL_GATE_GROUP // GATE_GROUP_COLS),
                               pipeline_mode=resident),
                  pl.BlockSpec((tm, d), lambda i: (i, 0)),
                  pl.BlockSpec((tm, d), lambda i: (i, 0)),
                  bspec,
                  wspec, wspec, wspec],
        out_specs=pl.BlockSpec((tm, d), lambda i: (i, 0)),
        out_shape=jax.ShapeDtypeStruct((t, d), F32),
        compiler_params=_cparams(("parallel",)),
        name="mix_out",
    )(h, sh, sc, mix_w, w_big, ha, hb, gt, w_pa, w_pb, w_o)


def _route_body(h_ref, sh_ref, sc_ref, nw_ref, rwt_ref, rb_ref,
                u_ref, eidx_ref, wts_ref, pos_ref, cnt_ref, carry_scr):
    tm = h_ref.shape[0]

    @pl.when(pl.program_id(0) == 0)
    def _():
        carry_scr[...] = jnp.zeros(carry_scr.shape, F32)

    u = _modulated_rmsnorm(h_ref[...], nw_ref[...], sc_ref[0], sh_ref[0])
    half = u.shape[1] // 2
    u_ref[...] = _pack_bf16_pair(u[:, :half], u[:, half:])
    logits = _split2_dot_nt(rwt_ref[...], u)
    scores = jax.nn.sigmoid(logits)
    sel = scores + rb_ref[...]

    neg = -jnp.inf
    sub_io = lax.broadcasted_iota(I32, (GROUP_SIZE, tm), 0)
    pieces = []
    for g in range(N_GROUPS):
        blk = sel[g * GROUP_SIZE:(g + 1) * GROUP_SIZE, :]
        m1 = jnp.max(blk, axis=0, keepdims=True)
        first = jnp.min(jnp.where(blk == m1, sub_io, GROUP_SIZE), axis=0, keepdims=True)
        m2 = jnp.max(jnp.where(sub_io == first, neg, blk), axis=0, keepdims=True)
        pieces.append(jnp.broadcast_to(m1 + m2, (GROUP_SIZE, tm)))
    gscore = jnp.concatenate(pieces, axis=0)

    e_io = lax.broadcasted_iota(I32, (N_EXPERTS, tm), 0)
    grp_io = e_io // GROUP_SIZE
    gmask = jnp.zeros((N_EXPERTS, tm), jnp.bool_)
    for _ in range(TOPK_GROUPS):
        mx = jnp.max(gscore, axis=0, keepdims=True)
        gi = jnp.min(jnp.where(gscore == mx, grp_io, N_GROUPS), axis=0, keepdims=True)
        hit = grp_io == gi
        gmask = jnp.logical_or(gmask, hit)
        gscore = jnp.where(hit, neg, gscore)

    cur = jnp.where(gmask, sel, neg)
    row_io = lax.broadcasted_iota(I32, (TOP_K, tm), 0)
    eidx = jnp.zeros((TOP_K, tm), I32)
    wraw = jnp.zeros((TOP_K, tm), F32)
    chosen = jnp.zeros((N_EXPERTS, tm), jnp.bool_)
    hits = []
    for kk in range(TOP_K):
        mx = jnp.max(cur, axis=0, keepdims=True)
        ei = jnp.min(jnp.where(cur == mx, e_io, N_EXPERTS), axis=0, keepdims=True)
        hit = e_io == ei
        hits.append(hit)
        sc_k = jnp.sum(jnp.where(hit, scores, 0.0), axis=0, keepdims=True)
        eidx = jnp.where(row_io == kk, ei, eidx)
        wraw = jnp.where(row_io == kk, sc_k, wraw)
        chosen = jnp.logical_or(chosen, hit)
        cur = jnp.where(hit, neg, cur)

    wsum = jnp.sum(wraw, axis=0, keepdims=True)
    wts_ref[...] = wraw / wsum * ROUTED_SCALE
    eidx_ref[...] = eidx

    chosen_f = jnp.where(chosen, 1.0, 0.0)
    r_io = lax.broadcasted_iota(I32, (tm, tm), 0)
    c_io = lax.broadcasted_iota(I32, (tm, tm), 1)
    strict_upper = jnp.where(r_io < c_io, 1.0, 0.0).astype(BF16)
    prefix = jnp.dot(chosen_f.astype(BF16), strict_upper, preferred_element_type=F32)
    rank = prefix + carry_scr[:, 0:1]
    pos = jnp.zeros((TOP_K, tm), F32)
    for kk in range(TOP_K):
        p_k = jnp.sum(jnp.where(hits[kk], rank, 0.0), axis=0, keepdims=True)
        pos = jnp.where(row_io == kk, p_k, pos)
    pos_ref[...] = pos.astype(I32)
    total = carry_scr[...] + jnp.sum(chosen_f, axis=1, keepdims=True)
    carry_scr[...] = total
    cnt_ref[...] = total.astype(I32)


def _route(h, sh, sc, nw, rw_t, rb_col, seq):
    t, d = h.shape
    tm = min(ROUTE_ROWS, seq)
    per_b = seq // tm
    kspec = pl.BlockSpec((TOP_K, tm), lambda i: (0, i))
    return pl.pallas_call(
        _route_body,
        grid=(t // tm,),
        in_specs=[pl.BlockSpec((tm, d), lambda i: (i, 0)),
                  pl.BlockSpec((1, 1, d), lambda i: (i // per_b, 0, 0)),
                  pl.BlockSpec((1, 1, d), lambda i: (i // per_b, 0, 0)),
                  pl.BlockSpec((1, d), lambda i: (0, 0)),
                  pl.BlockSpec((N_EXPERTS, d), lambda i: (0, 0)),
                  pl.BlockSpec((N_EXPERTS, 1), lambda i: (0, 0))],
        out_specs=[pl.BlockSpec((tm, d // 2), lambda i: (i, 0)), kspec, kspec, kspec,
                   pl.BlockSpec((N_EXPERTS, LANES), lambda i: (0, 0))],
        out_shape=[jax.ShapeDtypeStruct((t, d // 2), U32),
                   jax.ShapeDtypeStruct((TOP_K, t), I32),
                   jax.ShapeDtypeStruct((TOP_K, t), F32),
                   jax.ShapeDtypeStruct((TOP_K, t), I32),
                   jax.ShapeDtypeStruct((N_EXPERTS, LANES), I32)],
        scratch_shapes=[pltpu.VMEM((N_EXPERTS, LANES), F32)],
        compiler_params=_cparams(("arbitrary",)),
        name="moe_route",
    )(h, sh, sc, nw, rw_t, rb_col)


def _sc_worker_id():
    return lax.axis_index("s") * SC_CORES + lax.axis_index("c")


def _sc_scatter_rows(x, dest, n_rows):
    t, d = x.shape
    n_k = dest.shape[0]
    assert t % (SC_WORKERS * 2 * SC_CHUNK) == 0
    per_w = t // SC_WORKERS
    n_ch = per_w // SC_CHUNK
    dest4 = dest.reshape(n_k, SC_WORKERS, n_ch, SC_CHUNK).transpose(1, 2, 0, 3)
    mesh = plsc.VectorSubcoreMesh(core_axis_name="c", subcore_axis_name="s")

    @functools.partial(
        pl.kernel, mesh=mesh,
        out_type=jax.ShapeDtypeStruct((n_rows, d), x.dtype),
        scratch_types=[pltpu.VMEM((n_ch, n_k, SC_CHUNK), I32),
                       pltpu.VMEM((SC_CHUNK, d), x.dtype),
                       pltpu.VMEM((SC_CHUNK, d), x.dtype)] + [pltpu.SemaphoreType.DMA] * 4,
        name="moe_dispatch_sc",
    )
    def scatter_kernel(x_hbm, dest_hbm, out_hbm, idx_v, rows0, rows1, l0, l1, s0, s1):
        wid = _sc_worker_id()
        base = wid * per_w
        pltpu.sync_copy(dest_hbm.at[wid], idx_v)
        bufs = ((rows0, l0, s0), (rows1, l1, s1))

        def load(ci, b):
            rows, load_sem, _ = bufs[b]
            return pltpu.make_async_copy(x_hbm.at[pl.ds(base + ci * SC_CHUNK, SC_CHUNK)], rows,
                                         load_sem)

        def scatters(ci, b):
            rows, _, scatter_sem = bufs[b]
            return [pltpu.make_async_copy(rows, out_hbm.at[idx_v.at[ci, j]], scatter_sem)
                    for j in range(n_k)]

        load(0, 0).start()

        @pl.loop(0, n_ch, step=2)
        def _(ci):
            load(ci, 0).wait()
            for cp in scatters(ci, 0):
                cp.start()

            @pl.when(ci > 0)
            def _():
                for cp in scatters(ci - 1, 1):
                    cp.wait()

            load(ci + 1, 1).start()
            load(ci + 1, 1).wait()
            for cp in scatters(ci + 1, 1):
                cp.start()
            for cp in scatters(ci, 0):
                cp.wait()

            @pl.when(ci + 2 < n_ch)
            def _():
                load(ci + 2, 0).start()

        for cp in scatters(n_ch - 1, 1):
            cp.wait()

    return scatter_kernel(x, dest4)


def _sc_gather_rows(table, idx):
    n, d = idx.shape[0], table.shape[1]
    assert n % (SC_WORKERS * 2 * SC_CHUNK) == 0
    per_w = n // SC_WORKERS
    n_ch = per_w // SC_CHUNK
    idx3 = idx.reshape(SC_WORKERS, n_ch, SC_CHUNK)
    mesh = plsc.VectorSubcoreMesh(core_axis_name="c", subcore_axis_name="s")

    @functools.partial(
        pl.kernel, mesh=mesh,
        out_type=jax.ShapeDtypeStruct((n, d), table.dtype),
        scratch_types=[pltpu.VMEM((n_ch, SC_CHUNK), I32),
                       pltpu.VMEM((SC_CHUNK, d), table.dtype),
                       pltpu.VMEM((SC_CHUNK, d), table.dtype)] + [pltpu.SemaphoreType.DMA] * 4,
        name="moe_combine_sc",
    )
    def gather_kernel(table_hbm, idx_hbm, out_hbm, idx_v, rows0, rows1, g0, g1, w0, w1):
        wid = _sc_worker_id()
        base = wid * per_w
        pltpu.sync_copy(idx_hbm.at[wid], idx_v)
        bufs = ((rows0, g0, w0), (rows1, g1, w1))

        def gather(ci, b):
            rows, gather_sem, _ = bufs[b]
            return pltpu.make_async_copy(table_hbm.at[idx_v.at[ci]], rows, gather_sem)

        def write(ci, b):
            rows, _, write_sem = bufs[b]
            return pltpu.make_async_copy(rows, out_hbm.at[pl.ds(base + ci * SC_CHUNK, SC_CHUNK)],
                                         write_sem)

        gather(0, 0).start()

        @pl.loop(0, n_ch, step=2)
        def _(ci):
            gather(ci, 0).wait()
            write(ci, 0).start()

            @pl.when(ci > 0)
            def _():
                write(ci - 1, 1).wait()

            gather(ci + 1, 1).start()
            gather(ci + 1, 1).wait()
            write(ci + 1, 1).start()
            write(ci, 0).wait()

            @pl.when(ci + 2 < n_ch)
            def _():
                gather(ci + 2, 0).start()

        write(n_ch - 1, 1).wait()

    return gather_kernel(table, idx3)


def _expert_body(estart_ref, ecount_ref, x_hbm, w1_ref, w3_ref, w2_ref, y_hbm,
                 xbuf, ybuf, w1_scr, w3_scr, w2_scr, in_sem, out_sem):
    e = pl.program_id(0)
    start = estart_ref[e]
    count = ecount_ref[e]
    n_blk = (count + EXPERT_ROWS - 1) // EXPERT_ROWS
    half = xbuf.shape[2]

    def rows_at(b):
        return pl.ds(pl.multiple_of(start + b * EXPERT_ROWS, EXPERT_ROWS), EXPERT_ROWS)

    def read(b, slot):
        return pltpu.make_async_copy(x_hbm.at[rows_at(b)], xbuf.at[slot], in_sem.at[slot])

    def write(b, slot):
        return pltpu.make_async_copy(ybuf.at[slot], y_hbm.at[rows_at(b)], out_sem.at[slot])

    n_slots = xbuf.shape[0]
    lookahead = n_slots - 1

    for b0 in range(lookahead):
        @pl.when(b0 < n_blk)
        def _(b0=b0):
            read(b0, b0).start()

    @pl.when(n_blk > 0)
    def _():
        w1_scr[...] = w1_ref[0].astype(BF16)
        w3_scr[...] = w3_ref[0].astype(BF16)
        w2_scr[...] = w2_ref[0].astype(BF16)

    def block(b, carry):
        slot = b % n_slots
        read(b, slot).wait()

        @pl.when(b + lookahead < n_blk)
        def _():
            read(b + lookahead, (b + lookahead) % n_slots).start()

        @pl.when(b >= n_slots)
        def _():
            write(b - n_slots, slot).wait()

        valid = count - b * EXPERT_ROWS
        rows = lax.broadcasted_iota(I32, (EXPERT_ROWS, half), 0)
        lo, hi = _unpack_bf16_pair(jnp.where(rows < valid, xbuf[slot], 0))
        lo, hi = lo.astype(BF16), hi.astype(BF16)

        def proj(w_scr):
            return (jnp.dot(lo, w_scr[0:half, :], preferred_element_type=F32)
                    + jnp.dot(hi, w_scr[half:2 * half, :], preferred_element_type=F32))

        hid = _silu(proj(w1_scr)) * proj(w3_scr)
        y = jnp.dot(hid.astype(BF16), w2_scr[...], preferred_element_type=F32)
        ybuf[slot] = _pack_bf16_pair(y[:, :half], y[:, half:])
        write(b, slot).start()
        return carry

    lax.fori_loop(0, n_blk, block, 0)

    for back in range(n_slots, 0, -1):
        @pl.when(n_blk >= back)
        def _(back=back):
            write(n_blk - back, (n_blk - back) % n_slots).wait()


def _experts(xg, estart, ecount, w1, w3, w2, layer):
    n_rows, half = xg.shape
    d, de = w1.shape[-2:]
    grid_spec = pltpu.PrefetchScalarGridSpec(
        num_scalar_prefetch=2,
        grid=(N_EXPERTS,),
        in_specs=[pl.BlockSpec(memory_space=pl.ANY),
                  pl.BlockSpec((None, 1, d, de), lambda e, es, ec: (layer, e, 0, 0)),
                  pl.BlockSpec((None, 1, d, de), lambda e, es, ec: (layer, e, 0, 0)),
                  pl.BlockSpec((None, 1, de, d), lambda e, es, ec: (layer, e, 0, 0))],
        out_specs=pl.BlockSpec(memory_space=pl.ANY),
        scratch_shapes=[pltpu.VMEM((EXPERT_SLOTS, EXPERT_ROWS, half), U32),
                        pltpu.VMEM((EXPERT_SLOTS, EXPERT_ROWS, half), U32),
                        pltpu.VMEM((d, de), BF16), pltpu.VMEM((d, de), BF16),
                        pltpu.VMEM((de, d), BF16),
                        pltpu.SemaphoreType.DMA((EXPERT_SLOTS,)),
                        pltpu.SemaphoreType.DMA((EXPERT_SLOTS,))],
    )
    return pl.pallas_call(
        _expert_body,
        grid_spec=grid_spec,
        out_shape=jax.ShapeDtypeStruct((n_rows, half), U32),
        compiler_params=_cparams(("arbitrary",)),
        name="moe_experts",
    )(estart, ecount, xg, w1, w3, w2)


def _combine_body(h_ref, u_ref, yg_ref, wts_ref, gt_ref, s1_ref, s3_ref, s2_ref, fw_ref, o_ref, *,
                  final_norm):
    half = u_ref.shape[1]
    lo, hi = _unpack_bf16_pair(u_ref[...])
    lo, hi = lo.astype(BF16), hi.astype(BF16)

    def proj(w_ref):
        return (jnp.dot(lo, w_ref[0:half, :], preferred_element_type=F32)
                + jnp.dot(hi, w_ref[half:2 * half, :], preferred_element_type=F32))

    hid = _silu(proj(s1_ref)) * proj(s3_ref)
    shared = jnp.dot(hid.astype(BF16), s2_ref[...], preferred_element_type=F32)

    routed_lo = routed_hi = None
    for kk in range(TOP_K):
        y_lo, y_hi = _unpack_bf16_pair(yg_ref[kk])
        w = wts_ref[:, kk:kk + 1]
        routed_lo = y_lo * w if routed_lo is None else routed_lo + y_lo * w
        routed_hi = y_hi * w if routed_hi is None else routed_hi + y_hi * w
    gt = gt_ref[0]
    out_lo = h_ref[:, 0:half] + gt[:, 0:half] * (routed_lo + shared[:, 0:half])
    out_hi = (h_ref[:, half:2 * half]
              + gt[:, half:2 * half] * (routed_hi + shared[:, half:2 * half]))
    if final_norm:
        ssq = (jnp.sum(out_lo * out_lo, axis=-1, keepdims=True)
               + jnp.sum(out_hi * out_hi, axis=-1, keepdims=True))
        inv = lax.rsqrt(ssq / (2 * half) + EPS)
        out_lo = (out_lo * inv) * fw_ref[:, 0:half]
        out_hi = (out_hi * inv) * fw_ref[:, half:2 * half]
    o_ref[:, 0:half] = out_lo
    o_ref[:, half:2 * half] = out_hi


def _combine(h, u, yg, wts_tk, gt, s1, s3, s2, final_w, seq, final_norm, part, n_parts):
    t, d = h.shape
    half = d // 2
    tm = min(COMBINE_ROWS, seq)
    per_b = seq // tm
    ds_ = s1.shape[-1]
    steps = t // n_parts // tm
    off = part * steps
    return pl.pallas_call(
        functools.partial(_combine_body, final_norm=final_norm),
        grid=(steps,),
        in_specs=[pl.BlockSpec((tm, d), lambda i: (i + off, 0)),
                  pl.BlockSpec((tm, half), lambda i: (i + off, 0)),
                  pl.BlockSpec((TOP_K, tm, half), lambda i: (0, i, 0)),
                  pl.BlockSpec((tm, TOP_K), lambda i: (i + off, 0)),
                  pl.BlockSpec((1, 1, d), lambda i: ((i + off) // per_b, 0, 0)),
                  pl.BlockSpec((d, ds_), lambda i: (0, 0)),
                  pl.BlockSpec((d, ds_), lambda i: (0, 0)),
                  pl.BlockSpec((ds_, d), lambda i: (0, 0)),
                  pl.BlockSpec((1, d), lambda i: (0, 0))],
        out_specs=pl.BlockSpec((tm, d), lambda i: (i + off, 0)),
        out_shape=jax.ShapeDtypeStruct((t, d), F32),
        input_output_aliases={0: 0},
        compiler_params=_cparams(("parallel",)),
        name="moe_combine",
    )(h, u, yg, wts_tk, gt, s1, s3, s2, final_w)


def _split_w_in(w_in):
    sizes = (2 * M_HEADS * M_DQK, M_HEADS * M_DV, M_HEADS * M_DV, M_HEADS, M_HEADS,
             G_HEADS * G_DK, G_HEADS * G_DK, G_HEADS * G_DV, G_RANK, G_HEADS * G_DV,
             D_MODEL, D_MODEL)
    offs = [0]
    for n in sizes:
        offs.append(offs[-1] + n)
    w16 = w_in.astype(BF16)
    big = jnp.concatenate([w16[:, offs[0]:offs[3]], w16[:, offs[5]:offs[8]], w16[:, offs[9]:offs[12]]],
                          axis=1)
    pad = jnp.zeros((w_in.shape[0], SMALL_COLS - 2 * M_HEADS - G_RANK), BF16)
    small = jnp.concatenate([w16[:, offs[3]:offs[5]], w16[:, offs[8]:offs[9]], pad], axis=1)
    return big, small


def _moe_layout(counts, eidx, pos):
    padded = (counts + EXPERT_ROWS - 1) // EXPERT_ROWS * EXPERT_ROWS
    pstart = jnp.cumsum(padded) - padded
    experts = jnp.arange(N_EXPERTS, dtype=I32)
    dest = pos + jnp.sum(jnp.where(eidx[..., None] == experts, pstart, 0), axis=-1)
    return dest.astype(I32), pstart.astype(I32), counts.astype(I32)


def kernel(x, c, ada_w, ada_b, norm_mix_w, norm_moe_w, w_in, m_conv_w, m_gate_b, m_norm_w,
           g_alpha_w, g_alpha_b, g_norm_w, w_pa, w_pb, w_o, router_w, router_b,
           exp_w1, exp_w3, exp_w2, sh_w1, sh_w3, sh_w2, final_norm_w):
    bsz, seq, d = x.shape
    depth = ada_w.shape[0]
    t = bsz * seq
    n_rows = t * TOP_K + N_EXPERTS * EXPERT_ROWS

    ada = _ada(c, ada_w, ada_b).reshape(depth, bsz, 6, 1, d)
    h = x.reshape(t, d)
    for l in range(depth):
        sh1, sc1, gt1, sh2, sc2, gt2 = (ada[l, :, i] for i in range(6))

        w_big, w_small = _split_w_in(w_in[l])
        mix_w = norm_mix_w[l][None, :]
        gate_row = jnp.zeros((1, SMALL_COLS), F32)
        gate_row = gate_row.at[0, SMALL_I:SMALL_I + M_HEADS].set(m_gate_b[l, 0])
        gate_row = gate_row.at[0, SMALL_F:SMALL_F + M_HEADS].set(m_gate_b[l, 1])
        ha = _mlstm(h, sh1, sc1, mix_w, w_big, w_small, m_conv_w[l], gate_row, m_norm_w[l][None, :],
                    bsz, seq)
        alpha_full = jnp.zeros((SMALL_COLS, G_HEADS * G_DK), F32)
        alpha_full = alpha_full.at[SMALL_R:SMALL_R + G_RANK].set(g_alpha_w[l]).astype(BF16)
        hb = _gla(h, sh1, sc1, mix_w, w_big, w_small, alpha_full, g_alpha_b[l][None, :],
                  g_norm_w[l][None, :], bsz, seq)
        h = _mixout(h, sh1, sc1, mix_w, w_big, ha, hb, gt1, w_pa[l].astype(BF16),
                    w_pb[l].astype(BF16), w_o[l].astype(BF16), seq)

        u, eidx, wts, pos, cnt = _route(h, sh2, sc2, norm_moe_w[l][None, :],
                                        router_w[l].T, router_b[l][:, None], seq)
        dest, estart, ecount = _moe_layout(cnt[:, 0], eidx, pos)
        xg = _sc_scatter_rows(u, dest, n_rows)
        y = _experts(xg, estart, ecount, exp_w1, exp_w3, exp_w2, l)
        tp = t // COMBINE_PARTS
        for p in range(COMBINE_PARTS):
            dest_p = dest[:, p * tp:(p + 1) * tp].reshape(-1)
            yg = _sc_gather_rows(y, dest_p).reshape(TOP_K, tp, d // 2)
            h = _combine(h, u, yg, wts.T, gt2, sh_w1[l].astype(BF16), sh_w3[l].astype(BF16),
                         sh_w2[l].astype(BF16), final_norm_w[None, :], seq,
                         final_norm=(l == depth - 1), part=p, n_parts=COMBINE_PARTS)

    return h.reshape(bsz, seq, d)
```

```python
import functools

import jax
import jax.numpy as jnp
import numpy as np
from jax import lax
from jax.experimental import pallas as pl
from jax.experimental.pallas import tpu as pltpu
from jax.experimental.pallas import tpu_sc as plsc

F32 = jnp.float32
BF16 = jnp.bfloat16
I32 = jnp.int32
U32 = jnp.uint32
HI_MASK = np.uint32(0xFFFF0000)

LANES = 128

SC_CORES = 2
SC_SUBCORES = 16
SC_WORKERS = SC_CORES * SC_SUBCORES
SC_CHUNK = 64

D_MODEL = 1024
M_HEADS = 4
M_DQK = 128
M_DV = 256
M_CONV = 4
GATE_CAP = 15.0
G_HEADS = 4
G_DK = 128
G_DV = 256
G_RANK = 16
G_TAU = 16.0
G_CHUNK = 64
N_EXPERTS = 64
TOP_K = 8
N_GROUPS = 8
GROUP_SIZE = N_EXPERTS // N_GROUPS
TOPK_GROUPS = 4
ROUTED_SCALE = 2.5
EPS = 1e-6

M_CHUNK_ROWS = 256
G_BLOCK_ROWS = 256
SEQ_GROUP = 2
PROJ_PIECE_COLS = 256
MIX_ROWS = 512
ROUTE_ROWS = 1024
EXPERT_ROWS = 1024
COMBINE_ROWS = 512
COMBINE_PARTS = 4
CONV_HALO = 8
VMEM_LIMIT = 48 * 1024 * 1024

M_GROUP_COLS = 2 * M_HEADS * M_DQK + 2 * M_HEADS * M_DV
G_GROUP_COLS = 2 * G_HEADS * G_DK + 2 * G_HEADS * G_DV
GATE_GROUP_COLS = 2 * D_MODEL
COL_M_GROUP = 0
COL_G_GROUP = COL_M_GROUP + M_GROUP_COLS
COL_GATE_GROUP = COL_G_GROUP + G_GROUP_COLS
SMALL_COLS = LANES
SMALL_I, SMALL_F, SMALL_R = 0, M_HEADS, 2 * M_HEADS


def _cparams(sem, vmem=VMEM_LIMIT):
    return pltpu.CompilerParams(dimension_semantics=sem, vmem_limit_bytes=vmem)


def _silu(x):
    return x * jax.nn.sigmoid(x)


def _log_sigmoid(x):
    return jnp.minimum(x, 0.0) - jnp.log1p(jnp.exp(-jnp.abs(x)))


def _modulated_rmsnorm(x, w, sc, sh):
    y = x * lax.rsqrt(jnp.mean(x * x, axis=-1, keepdims=True) + EPS)
    return (y * w) * (1.0 + sc) + sh


def _pack_bf16_pair(lo, hi):
    lo_bits = lax.bitcast_convert_type(lo.astype(BF16).astype(F32), U32)
    hi_bits = lax.bitcast_convert_type(hi.astype(BF16).astype(F32), U32)
    return (lo_bits >> 16) | (hi_bits & HI_MASK)


def _unpack_bf16_pair(packed):
    lo = lax.bitcast_convert_type(packed << 16, F32)
    hi = lax.bitcast_convert_type(packed & HI_MASK, F32)
    return lo, hi


def _lower_tri(n, dtype):
    r = lax.broadcasted_iota(I32, (n, n), 0)
    c = lax.broadcasted_iota(I32, (n, n), 1)
    return (r >= c).astype(dtype)


def _split3_dot(lhs01, x):
    hi = x.astype(BF16)
    r1 = x - hi.astype(F32)
    mid = r1.astype(BF16)
    lo = (r1 - mid.astype(F32)).astype(BF16)
    return (jnp.dot(lhs01, hi, preferred_element_type=F32)
            + jnp.dot(lhs01, mid, preferred_element_type=F32)
            + jnp.dot(lhs01, lo, preferred_element_type=F32))


def _split2_dot_nt(a, b):
    nt = (((1,), (1,)), ((), ()))
    a_hi = a.astype(BF16)
    a_lo = (a - a_hi.astype(F32)).astype(BF16)
    b_hi = b.astype(BF16)
    b_lo = (b - b_hi.astype(F32)).astype(BF16)
    return (lax.dot_general(a_hi, b_hi, nt, preferred_element_type=F32)
            + lax.dot_general(a_hi, b_lo, nt, preferred_element_type=F32)
            + lax.dot_general(a_lo, b_hi, nt, preferred_element_type=F32))


def _ada_body(c_ref, w_ref, b_ref, o_ref):
    cond = _silu(c_ref[...])
    o_ref[0] = jnp.dot(cond.astype(BF16), w_ref[0].astype(BF16),
                       preferred_element_type=F32) + b_ref[0]


def _ada(c, ada_w, ada_b):
    depth, d, six_d = ada_w.shape
    bsz = c.shape[0]
    nj = six_d // d
    return pl.pallas_call(
        _ada_body,
        grid=(depth, nj),
        in_specs=[pl.BlockSpec((bsz, d), lambda l, j: (0, 0)),
                  pl.BlockSpec((1, d, d), lambda l, j: (l, 0, j)),
                  pl.BlockSpec((1, 1, d), lambda l, j: (l, 0, j))],
        out_specs=pl.BlockSpec((1, bsz, d), lambda l, j: (l, 0, j)),
        out_shape=jax.ShapeDtypeStruct((depth, bsz, six_d), F32),
        compiler_params=_cparams(("parallel", "parallel")),
        name="ada_ln",
    )(c, ada_w, ada_b.reshape(depth, 1, six_d))


def _mlstm_body(h0_ref, hn_ref, sh_ref, sc_ref, mw_ref, w_ref, ws_ref, cw_ref, gb_ref, nw_ref,
                ha_ref, qk_nxt, v_nxt, o_nxt, sm_nxt, xe_scr, v_scr, o_scr, sm_scr,
                c_scr, n_scr, m_scr):
    n_seq, rows = hn_ref.shape[0], hn_ref.shape[1]
    half = M_HEADS * M_DQK
    vcols = M_HEADS * M_DV

    def projection_pieces(h_ref):
        u = jnp.concatenate(
            [_modulated_rmsnorm(h_ref[g], mw_ref[...], sc_ref[g], sh_ref[g]) for g in range(n_seq)],
            axis=0).astype(BF16)

        def piece(dst, w_lo, lo, width):
            def run():
                dst[:, lo:lo + width] = jnp.dot(u, w_ref[:, w_lo + lo:w_lo + lo + width],
                                                preferred_element_type=F32).astype(dst.dtype)
            return run

        def small_piece():
            sm_nxt[...] = jnp.dot(u, ws_ref[...], preferred_element_type=F32)

        pieces = [small_piece]
        for dst, w_lo in ((qk_nxt, 0), (v_nxt, 2 * half), (o_nxt, 2 * half + vcols)):
            for lo in range(0, vcols, PROJ_PIECE_COLS):
                pieces.append(piece(dst, w_lo, lo, PROJ_PIECE_COLS))
        return pieces

    @pl.when(pl.program_id(1) == 0)
    def _():
        xe_scr[:, 0:CONV_HALO, :] = jnp.zeros((n_seq, CONV_HALO, 2 * half), F32)
        c_scr[...] = jnp.zeros(c_scr.shape, F32)
        n_scr[...] = jnp.zeros(n_scr.shape, F32)
        m_scr[...] = jnp.zeros(m_scr.shape, F32)
        for run in projection_pieces(h0_ref):
            run()

    for g in range(n_seq):
        xe_scr[g, CONV_HALO:CONV_HALO + rows, :] = qk_nxt[g * rows:(g + 1) * rows, :]
    v_scr[...] = v_nxt[...]
    o_scr[...] = o_nxt[...]
    sm_scr[...] = sm_nxt[...]
    pending = projection_pieces(hn_ref)

    r_io = lax.broadcasted_iota(I32, (rows, rows), 0)
    c_io = lax.broadcasted_iota(I32, (rows, rows), 1)
    causal = r_io >= c_io
    tri = _lower_tri(rows, BF16)

    for g in range(n_seq):
        gr = slice(g * rows, (g + 1) * rows)
        conv = None
        for j in range(M_CONV):
            off = CONV_HALO - (M_CONV - 1) + j
            term = xe_scr[g, off:off + rows, :] * cw_ref[j:j + 1, :]
            conv = term if conv is None else conv + term
        qk = _silu(conv)
        xe_scr[g, 0:CONV_HALO, :] = xe_scr[g, rows:rows + CONV_HALO, :]

        capped = GATE_CAP * jnp.tanh((sm_scr[gr, :] + gb_ref[...]) / GATE_CAP)
        li_all = capped
        lf_all = _log_sigmoid(capped)
        b_all = _split3_dot(tri, lf_all)
        li_t = li_all.T
        b_t = b_all.T

        for h in range(M_HEADS):
            if pending:
                pending.pop(0)()
            sidx = g * M_HEADS + h
            q = (qk[:, h * M_DQK:(h + 1) * M_DQK] * (M_DQK ** -0.5)).astype(BF16)
            k = qk[:, half + h * M_DQK:half + (h + 1) * M_DQK]
            kb = k.astype(BF16)
            v = v_scr[gr, h * M_DV:(h + 1) * M_DV]
            li_c = li_all[:, SMALL_I + h:SMALL_I + h + 1]
            b_c = b_all[:, SMALL_F + h:SMALL_F + h + 1]
            li_r = li_t[SMALL_I + h:SMALL_I + h + 1, :]
            b_r = b_t[SMALL_F + h:SMALL_F + h + 1, :]
            gsum = b_c[rows - 1:rows, :]
            m_prev = m_scr[sidx][:, 0:1]
            c_prev = c_scr[sidx]
            n_prev = n_scr[sidx]

            d_mat = jnp.where(causal, b_c - b_r + li_r, -jnp.inf)
            m_inter = b_c + m_prev
            m_t = jnp.maximum(jnp.max(d_mat, axis=1, keepdims=True), m_inter)
            s = lax.dot_general(q, kb, (((1,), (1,)), ((), ())), preferred_element_type=F32)
            p = jnp.exp(d_mat - m_t) * s
            w_inter = jnp.exp(m_inter - m_t)
            num = (jnp.dot(p.astype(BF16), v, preferred_element_type=F32)
                   + w_inter * jnp.dot(q, c_prev.astype(BF16), preferred_element_type=F32))
            qn = jnp.sum(q.astype(F32) * n_prev, axis=1, keepdims=True)
            den = jnp.sum(p, axis=1, keepdims=True) + w_inter * qn
            hh = num / jnp.maximum(jnp.abs(den), jnp.exp(-m_t))

            if pending:
                pending.pop(0)()

            a_r = gsum - b_r + li_r
            a_c = gsum - b_c + li_c
            m_new = jnp.maximum(gsum + m_prev, jnp.max(a_r, axis=1, keepdims=True))
            decay = jnp.exp(gsum + m_prev - m_new)
            wk = jnp.exp(a_c - m_new) * k
            c_scr[sidx] = decay * c_prev + lax.dot_general(
                wk.astype(BF16), v, (((0,), (0,)), ((), ())), preferred_element_type=F32)
            n_scr[sidx] = decay * n_prev + jnp.sum(wk, axis=0, keepdims=True)
            m_scr[sidx] = jnp.broadcast_to(m_new, m_scr.shape[1:])

            y = hh * lax.rsqrt(jnp.mean(hh * hh, axis=-1, keepdims=True) + EPS)
            y = y * nw_ref[:, h * M_DV:(h + 1) * M_DV]
            gate = jax.nn.sigmoid(o_scr[gr, h * M_DV:(h + 1) * M_DV])
            ha_ref[g, :, h * M_DV:(h + 1) * M_DV] = (y * gate).astype(BF16)
    for run in pending:
        run()


def _mlstm(h, sh, sc, mix_w, w_big, w_small, conv_w, gate_row, norm_w, bsz, seq):
    t, d = h.shape
    rows = min(M_CHUNK_ROWS, seq)
    nc = seq // rows
    ng = SEQ_GROUP if bsz % SEQ_GROUP == 0 else 1
    half = M_HEADS * M_DQK
    vcols = M_HEADS * M_DV
    wcols = M_GROUP_COLS
    assert COL_M_GROUP % wcols == 0
    resident = pl.Buffered(1)
    nxt = [pltpu.VMEM((ng * rows, 2 * half), F32), pltpu.VMEM((ng * rows, vcols), BF16),
           pltpu.VMEM((ng * rows, vcols), F32), pltpu.VMEM((ng * rows, SMALL_COLS), F32)]
    out = pl.pallas_call(
        _mlstm_body,
        grid=(bsz // ng, nc),
        in_specs=[pl.BlockSpec((ng, rows, d), lambda b, c: (b, 0, 0)),
                  pl.BlockSpec((ng, rows, d), lambda b, c: (b, jnp.minimum(c + 1, nc - 1), 0)),
                  pl.BlockSpec((ng, 1, d), lambda b, c: (b, 0, 0)),
                  pl.BlockSpec((ng, 1, d), lambda b, c: (b, 0, 0)),
                  pl.BlockSpec((1, d), lambda b, c: (0, 0)),
                  pl.BlockSpec((d, wcols), lambda b, c: (0, 0), pipeline_mode=resident),
                  pl.BlockSpec((d, SMALL_COLS), lambda b, c: (0, 0), pipeline_mode=resident),
                  pl.BlockSpec((M_CONV, 2 * half), lambda b, c: (0, 0)),
                  pl.BlockSpec((1, SMALL_COLS), lambda b, c: (0, 0)),
                  pl.BlockSpec((1, vcols), lambda b, c: (0, 0))],
        out_specs=pl.BlockSpec((ng, rows, vcols), lambda b, c: (b, c, 0)),
        out_shape=jax.ShapeDtypeStruct((bsz, seq, vcols), BF16),
        scratch_shapes=nxt + [pltpu.VMEM((ng, rows + CONV_HALO, 2 * half), F32)] + nxt[1:]
        + [pltpu.VMEM((ng * M_HEADS, M_DQK, M_DV), F32),
           pltpu.VMEM((ng * M_HEADS, 1, M_DQK), F32),
           pltpu.VMEM((ng * M_HEADS, 1, LANES), F32)],
        compiler_params=_cparams(("parallel", "arbitrary")),
        name="mlstm",
    )(h.reshape(bsz, seq, d), h.reshape(bsz, seq, d), sh, sc, mix_w, w_big, w_small, conv_w,
      gate_row, norm_w)
    return out.reshape(t, vcols)


def _gla_body(h0_ref, hn_ref, sh_ref, sc_ref, mw_ref, w_ref, ws_ref, aw_ref, ab_ref, nw_ref, hb_ref,
              qk_nxt, v_nxt, z_nxt, sm_nxt, qk_scr, v_scr, z_scr, sm_scr, st_scr):
    n_seq, rows = hn_ref.shape[0], hn_ref.shape[1]
    n_chunks = rows // G_CHUNK
    kcols = G_HEADS * G_DK
    vcols = G_HEADS * G_DV

    def projection_pieces(h_ref):
        u = jnp.concatenate(
            [_modulated_rmsnorm(h_ref[g], mw_ref[...], sc_ref[g], sh_ref[g]) for g in range(n_seq)],
            axis=0).astype(BF16)

        def piece(dst, w_lo, lo, width):
            def run():
                dst[:, lo:lo + width] = jnp.dot(u, w_ref[:, w_lo + lo:w_lo + lo + width],
                                                preferred_element_type=F32).astype(dst.dtype)
            return run

        def small_piece():
            sm_nxt[...] = jnp.dot(u, ws_ref[...], preferred_element_type=F32)

        pieces = [small_piece]
        for dst, w_lo in ((qk_nxt, 0), (v_nxt, 2 * kcols), (z_nxt, 2 * kcols + vcols)):
            for lo in range(0, vcols, PROJ_PIECE_COLS):
                pieces.append(piece(dst, w_lo, lo, PROJ_PIECE_COLS))
        return pieces

    @pl.when(pl.program_id(1) == 0)
    def _():
        st_scr[...] = jnp.zeros(st_scr.shape, F32)
        for run in projection_pieces(h0_ref):
            run()

    qk_scr[...] = qk_nxt[...]
    v_scr[...] = v_nxt[...]
    z_scr[...] = z_nxt[...]
    sm_scr[...] = sm_nxt[...]
    pending = projection_pieces(hn_ref)

    r_io = lax.broadcasted_iota(I32, (rows, rows), 0)
    c_io = lax.broadcasted_iota(I32, (rows, rows), 1)
    chunk_causal = jnp.logical_and(r_io >= c_io, r_io // G_CHUNK == c_io // G_CHUNK)
    chunk_tri = jnp.where(chunk_causal, 1.0, 0.0).astype(BF16)

    for g in range(n_seq):
        gr = slice(g * rows, (g + 1) * rows)
        for _ in range((len(pending) + n_seq - 1 - g) // (n_seq - g)):
            pending.pop(0)()
        q = qk_scr[gr, 0:kcols] * (G_DK ** -0.5)
        k = qk_scr[gr, kcols:2 * kcols]

        logits = jnp.dot(sm_scr[gr, :].astype(BF16), aw_ref[...], preferred_element_type=F32)
        la = _log_sigmoid(logits + ab_ref[...]) / G_TAU
        bc = _split3_dot(chunk_tri, la)
        gcs = [bc[(ci + 1) * G_CHUNK - 1:(ci + 1) * G_CHUNK, :] for ci in range(n_chunks)]
        gc_rows = jnp.concatenate([jnp.broadcast_to(gc, (G_CHUNK, kcols)) for gc in gcs], axis=0)

        q_in = (q * jnp.exp(bc)).astype(BF16)
        k_in = (k * jnp.exp(-bc)).astype(BF16)
        k_out = (k * jnp.exp(gc_rows - bc)).astype(BF16)

        for h in range(G_HEADS):
            ks = slice(h * G_DK, (h + 1) * G_DK)
            vs = slice(h * G_DV, (h + 1) * G_DV)
            v = v_scr[gr, vs]
            att = lax.dot_general(q_in[:, ks], k_in[:, ks], (((1,), (1,)), ((), ())),
                                  preferred_element_type=F32)
            att = jnp.where(chunk_causal, att, 0.0).astype(BF16)
            o_intra = jnp.dot(att, v, preferred_element_type=F32)
            st = st_scr[g * G_HEADS + h]
            outs = []
            for ci in range(n_chunks):
                rs = slice(ci * G_CHUNK, (ci + 1) * G_CHUNK)
                o_inter = lax.dot_general(q_in[rs, ks], st.astype(BF16),
                                          (((1,), (1,)), ((), ())), preferred_element_type=F32)
                outs.append(o_intra[rs, :] + o_inter)
                st = jnp.exp(gcs[ci][:, ks]) * st + lax.dot_general(
                    v[rs, :], k_out[rs, ks], (((0,), (0,)), ((), ())), preferred_element_type=F32)
            st_scr[g * G_HEADS + h] = st
            o = jnp.concatenate(outs, axis=0)
            y = o * lax.rsqrt(jnp.mean(o * o, axis=-1, keepdims=True) + EPS)
            y = y * nw_ref[:, vs]
            hb_ref[g, :, vs] = (y * _silu(z_scr[gr, vs])).astype(BF16)


def _gla(h, sh, sc, mix_w, w_big, w_small, alpha_full, alpha_b, norm_w, bsz, seq):
    t, d = h.shape
    rows = min(G_BLOCK_ROWS, seq)
    nb = seq // rows
    ng = SEQ_GROUP if bsz % SEQ_GROUP == 0 else 1
    kcols = G_HEADS * G_DK
    vcols = G_HEADS * G_DV
    wcols = G_GROUP_COLS
    assert COL_G_GROUP % wcols == 0
    resident = pl.Buffered(1)
    nxt = [pltpu.VMEM((ng * rows, 2 * kcols), F32), pltpu.VMEM((ng * rows, vcols), BF16),
           pltpu.VMEM((ng * rows, vcols), F32), pltpu.VMEM((ng * rows, SMALL_COLS), F32)]
    out = pl.pallas_call(
        _gla_body,
        grid=(bsz // ng, nb),
        in_specs=[pl.BlockSpec((ng, rows, d), lambda b, c: (b, 0, 0)),
                  pl.BlockSpec((ng, rows, d), lambda b, c: (b, jnp.minimum(c + 1, nb - 1), 0)),
                  pl.BlockSpec((ng, 1, d), lambda b, c: (b, 0, 0)),
                  pl.BlockSpec((ng, 1, d), lambda b, c: (b, 0, 0)),
                  pl.BlockSpec((1, d), lambda b, c: (0, 0)),
                  pl.BlockSpec((d, wcols), lambda b, c: (0, COL_G_GROUP // wcols),
                               pipeline_mode=resident),
                  pl.BlockSpec((d, SMALL_COLS), lambda b, c: (0, 0), pipeline_mode=resident),
                  pl.BlockSpec((SMALL_COLS, kcols), lambda b, c: (0, 0)),
                  pl.BlockSpec((1, kcols), lambda b, c: (0, 0)),
                  pl.BlockSpec((1, vcols), lambda b, c: (0, 0))],
        out_specs=pl.BlockSpec((ng, rows, vcols), lambda b, c: (b, c, 0)),
        out_shape=jax.ShapeDtypeStruct((bsz, seq, vcols), BF16),
        scratch_shapes=nxt + nxt + [pltpu.VMEM((ng * G_HEADS, G_DV, G_DK), F32)],
        compiler_params=_cparams(("parallel", "arbitrary")),
        name="gla",
    )(h.reshape(bsz, seq, d), h.reshape(bsz, seq, d), sh, sc, mix_w, w_big, w_small, alpha_full,
      alpha_b, norm_w)
    return out.reshape(t, vcols)


def _mixout_body(h_ref, sh_ref, sc_ref, mw_ref, wg_ref, ha_ref, hb_ref, gt_ref, wpa_ref, wpb_ref,
                 wo_ref, o_ref):
    d = h_ref.shape[1]
    h = h_ref[...]
    u = _modulated_rmsnorm(h, mw_ref[...], sc_ref[0], sh_ref[0]).astype(BF16)
    ga = jnp.dot(u, wg_ref[:, 0:d], preferred_element_type=F32)
    gb = jnp.dot(u, wg_ref[:, d:2 * d], preferred_element_type=F32)
    a = jnp.dot(ha_ref[...], wpa_ref[...], preferred_element_type=F32)
    b = jnp.dot(hb_ref[...], wpb_ref[...], preferred_element_type=F32)
    y = jax.nn.sigmoid(ga) * a + jax.nn.sigmoid(gb) * b
    o_ref[...] = h + gt_ref[0] * jnp.dot(y.astype(BF16), wo_ref[...], preferred_element_type=F32)


def _mixout(h, sh, sc, mix_w, w_big, ha, hb, gt, w_pa, w_pb, w_o, seq):
    t, d = h.shape
    tm = min(MIX_ROWS, seq)
    per_b = seq // tm
    assert GATE_GROUP_COLS == 2 * d and COL_GATE_GROUP % GATE_GROUP_COLS == 0
    resident = pl.Buffered(1)
    wspec = pl.BlockSpec((d, d), lambda i: (0, 0), pipeline_mode=resident)
    bspec = pl.BlockSpec((1, 1, d), lambda i: (i // per_b, 0, 0))
    return pl.pallas_call(
        _mixout_body,
        grid=(t // tm,),
        in_specs=[pl.BlockSpec((tm, d), lambda i: (i, 0)),
                  bspec, bspec,
                  pl.BlockSpec((1, d), lambda i: (0, 0)),
                  pl.BlockSpec((d, 2 * d), lambda i: (0, COL_GATE_GROUP // GATE_GROUP_COLS),
                               pipeline_mode=resident),
                  pl.BlockSpec((tm, d), lambda i: (i, 0)),
                  pl.BlockSpec((tm, d), lambda i: (i, 0)),
                  bspec,
                  wspec, wspec, wspec],
        out_specs=pl.BlockSpec((tm, d), lambda i: (i, 0)),
        out_shape=jax.ShapeDtypeStruct((t, d), F32),
        compiler_params=_cparams(("parallel",)),
        name="mix_out",
    )(h, sh, sc, mix_w, w_big, ha, hb, gt, w_pa, w_pb, w_o)


def _route_body(h_ref, sh_ref, sc_ref, nw_ref, rwt_ref, rb_ref,
                u_ref, eidx_ref, wts_ref, pos_ref, cnt_ref, carry_scr):
    tm = h_ref.shape[0]

    @pl.when(pl.program_id(0) == 0)
    def _():
        carry_scr[...] = jnp.zeros(carry_scr.shape, F32)

    u = _modulated_rmsnorm(h_ref[...], nw_ref[...], sc_ref[0], sh_ref[0])
    half = u.shape[1] // 2
    u_ref[...] = _pack_bf16_pair(u[:, :half], u[:, half:])
    logits = _split2_dot_nt(rwt_ref[...], u)
    scores = jax.nn.sigmoid(logits)
    sel = scores + rb_ref[...]

    neg = -jnp.inf
    sub_io = lax.broadcasted_iota(I32, (GROUP_SIZE, tm), 0)
    pieces = []
    for g in range(N_GROUPS):
        blk = sel[g * GROUP_SIZE:(g + 1) * GROUP_SIZE, :]
        m1 = jnp.max(blk, axis=0, keepdims=True)
        first = jnp.min(jnp.where(blk == m1, sub_io, GROUP_SIZE), axis=0, keepdims=True)
        m2 = jnp.max(jnp.where(sub_io == first, neg, blk), axis=0, keepdims=True)
        pieces.append(jnp.broadcast_to(m1 + m2, (GROUP_SIZE, tm)))
    gscore = jnp.concatenate(pieces, axis=0)

    e_io = lax.broadcasted_iota(I32, (N_EXPERTS, tm), 0)
    grp_io = e_io // GROUP_SIZE
    gmask = jnp.zeros((N_EXPERTS, tm), jnp.bool_)
    for _ in range(TOPK_GROUPS):
        mx = jnp.max(gscore, axis=0, keepdims=True)
        gi = jnp.min(jnp.where(gscore == mx, grp_io, N_GROUPS), axis=0, keepdims=True)
        hit = grp_io == gi
        gmask = jnp.logical_or(gmask, hit)
        gscore = jnp.where(hit, neg, gscore)

    cur = jnp.where(gmask, sel, neg)
    row_io = lax.broadcasted_iota(I32, (TOP_K, tm), 0)
    eidx = jnp.zeros((TOP_K, tm), I32)
    wraw = jnp.zeros((TOP_K, tm), F32)
    chosen = jnp.zeros((N_EXPERTS, tm), jnp.bool_)
    hits = []
    for kk in range(TOP_K):
        mx = jnp.max(cur, axis=0, keepdims=True)
        ei = jnp.min(jnp.where(cur == mx, e_io, N_EXPERTS), axis=0, keepdims=True)
        hit = e_io == ei
        hits.append(hit)
        sc_k = jnp.sum(jnp.where(hit, scores, 0.0), axis=0, keepdims=True)
        eidx = jnp.where(row_io == kk, ei, eidx)
        wraw = jnp.where(row_io == kk, sc_k, wraw)
        chosen = jnp.logical_or(chosen, hit)
        cur = jnp.where(hit, neg, cur)

    wsum = jnp.sum(wraw, axis=0, keepdims=True)
    wts_ref[...] = wraw / wsum * ROUTED_SCALE
    eidx_ref[...] = eidx

    chosen_f = jnp.where(chosen, 1.0, 0.0)
    r_io = lax.broadcasted_iota(I32, (tm, tm), 0)
    c_io = lax.broadcasted_iota(I32, (tm, tm), 1)
    strict_upper = jnp.where(r_io < c_io, 1.0, 0.0).astype(BF16)
    prefix = jnp.dot(chosen_f.astype(BF16), strict_upper, preferred_element_type=F32)
    rank = prefix + carry_scr[:, 0:1]
    pos = jnp.zeros((TOP_K, tm), F32)
    for kk in range(TOP_K):
        p_k = jnp.sum(jnp.where(hits[kk], rank, 0.0), axis=0, keepdims=True)
        pos = jnp.where(row_io == kk, p_k, pos)
    pos_ref[...] = pos.astype(I32)
    total = carry_scr[...] + jnp.sum(chosen_f, axis=1, keepdims=True)
    carry_scr[...] = total
    cnt_ref[...] = total.astype(I32)


def _route(h, sh, sc, nw, rw_t, rb_col, seq):
    t, d = h.shape
    tm = min(ROUTE_ROWS, seq)
    per_b = seq // tm
    kspec = pl.BlockSpec((TOP_K, tm), lambda i: (0, i))
    return pl.pallas_call(
        _route_body,
        grid=(t // tm,),
        in_specs=[pl.BlockSpec((tm, d), lambda i: (i, 0)),
                  pl.BlockSpec((1, 1, d), lambda i: (i // per_b, 0, 0)),
                  pl.BlockSpec((1, 1, d), lambda i: (i // per_b, 0, 0)),
                  pl.BlockSpec((1, d), lambda i: (0, 0)),
                  pl.BlockSpec((N_EXPERTS, d), lambda i: (0, 0)),
                  pl.BlockSpec((N_EXPERTS, 1), lambda i: (0, 0))],
        out_specs=[pl.BlockSpec((tm, d // 2), lambda i: (i, 0)), kspec, kspec, kspec,
                   pl.BlockSpec((N_EXPERTS, LANES), lambda i: (0, 0))],
        out_shape=[jax.ShapeDtypeStruct((t, d // 2), U32),
                   jax.ShapeDtypeStruct((TOP_K, t), I32),
                   jax.ShapeDtypeStruct((TOP_K, t), F32),
                   jax.ShapeDtypeStruct((TOP_K, t), I32),
                   jax.ShapeDtypeStruct((N_EXPERTS, LANES), I32)],
        scratch_shapes=[pltpu.VMEM((N_EXPERTS, LANES), F32)],
        compiler_params=_cparams(("arbitrary",)),
        name="moe_route",
    )(h, sh, sc, nw, rw_t, rb_col)


def _sc_worker_id():
    return lax.axis_index("s") * SC_CORES + lax.axis_index("c")


def _sc_scatter_rows(x, dest, n_rows):
    t, d = x.shape
    n_k = dest.shape[0]
    assert t % (SC_WORKERS * 2 * SC_CHUNK) == 0
    per_w = t // SC_WORKERS
    n_ch = per_w // SC_CHUNK
    dest4 = dest.reshape(n_k, SC_WORKERS, n_ch, SC_CHUNK).transpose(1, 2, 0, 3)
    mesh = plsc.VectorSubcoreMesh(core_axis_name="c", subcore_axis_name="s")

    @functools.partial(
        pl.kernel, mesh=mesh,
        out_type=jax.ShapeDtypeStruct((n_rows, d), x.dtype),
        scratch_types=[pltpu.VMEM((n_ch, n_k, SC_CHUNK), I32),
                       pltpu.VMEM((SC_CHUNK, d), x.dtype),
                       pltpu.VMEM((SC_CHUNK, d), x.dtype)] + [pltpu.SemaphoreType.DMA] * 4,
        name="moe_dispatch_sc",
    )
    def scatter_kernel(x_hbm, dest_hbm, out_hbm, idx_v, rows0, rows1, l0, l1, s0, s1):
        wid = _sc_worker_id()
        base = wid * per_w
        pltpu.sync_copy(dest_hbm.at[wid], idx_v)
        bufs = ((rows0, l0, s0), (rows1, l1, s1))

        def load(ci, b):
            rows, load_sem, _ = bufs[b]
            return pltpu.make_async_copy(x_hbm.at[pl.ds(base + ci * SC_CHUNK, SC_CHUNK)], rows,
                                         load_sem)

        def scatters(ci, b):
            rows, _, scatter_sem = bufs[b]
            return [pltpu.make_async_copy(rows, out_hbm.at[idx_v.at[ci, j]], scatter_sem)
                    for j in range(n_k)]

        load(0, 0).start()

        @pl.loop(0, n_ch, step=2)
        def _(ci):
            load(ci, 0).wait()
            for cp in scatters(ci, 0):
                cp.start()

            @pl.when(ci > 0)
            def _():
                for cp in scatters(ci - 1, 1):
                    cp.wait()

            load(ci + 1, 1).start()
            load(ci + 1, 1).wait()
            for cp in scatters(ci + 1, 1):
                cp.start()
            for cp in scatters(ci, 0):
                cp.wait()

            @pl.when(ci + 2 < n_ch)
            def _():
                load(ci + 2, 0).start()

        for cp in scatters(n_ch - 1, 1):
            cp.wait()

    return scatter_kernel(x, dest4)


def _sc_gather_rows(table, idx):
    n, d = idx.shape[0], table.shape[1]
    assert n % (SC_WORKERS * 2 * SC_CHUNK) == 0
    per_w = n // SC_WORKERS
    n_ch = per_w // SC_CHUNK
    idx3 = idx.reshape(SC_WORKERS, n_ch, SC_CHUNK)
    mesh = plsc.VectorSubcoreMesh(core_axis_name="c", subcore_axis_name="s")

    @functools.partial(
        pl.kernel, mesh=mesh,
        out_type=jax.ShapeDtypeStruct((n, d), table.dtype),
        scratch_types=[pltpu.VMEM((n_ch, SC_CHUNK), I32),
                       pltpu.VMEM((SC_CHUNK, d), table.dtype),
                       pltpu.VMEM((SC_CHUNK, d), table.dtype)] + [pltpu.SemaphoreType.DMA] * 4,
        name="moe_combine_sc",
    )
    def gather_kernel(table_hbm, idx_hbm, out_hbm, idx_v, rows0, rows1, g0, g1, w0, w1):
        wid = _sc_worker_id()
        base = wid * per_w
        pltpu.sync_copy(idx_hbm.at[wid], idx_v)
        bufs = ((rows0, g0, w0), (rows1, g1, w1))

        def gather(ci, b):
            rows, gather_sem, _ = bufs[b]
            return pltpu.make_async_copy(table_hbm.at[idx_v.at[ci]], rows, gather_sem)

        def write(ci, b):
            rows, _, write_sem = bufs[b]
            return pltpu.make_async_copy(rows, out_hbm.at[pl.ds(base + ci * SC_CHUNK, SC_CHUNK)],
                                         write_sem)

        gather(0, 0).start()

        @pl.loop(0, n_ch, step=2)
        def _(ci):
            gather(ci, 0).wait()
            write(ci, 0).start()

            @pl.when(ci > 0)
            def _():
                write(ci - 1, 1).wait()

            gather(ci + 1, 1).start()
            gather(ci + 1, 1).wait()
            write(ci + 1, 1).start()
            write(ci, 0).wait()

            @pl.when(ci + 2 < n_ch)
            def _():
                gather(ci + 2, 0).start()

        write(n_ch - 1, 1).wait()

    return gather_kernel(table, idx3)


def _expert_body(blk_e_ref, blk_first_ref, blk_valid_ref, x_ref, w1_ref, w3_ref, w2_ref, y_ref,
                 w1_scr, w3_scr, w2_scr):
    del blk_e_ref
    j = pl.program_id(0)
    valid = blk_valid_ref[j]

    @pl.when(blk_first_ref[j] == 1)
    def _():
        w1_scr[...] = w1_ref[0].astype(BF16)
        w3_scr[...] = w3_ref[0].astype(BF16)
        w2_scr[...] = w2_ref[0].astype(BF16)

    @pl.when(valid > 0)
    def _():
        half = x_ref.shape[1]
        rows = lax.broadcasted_iota(I32, x_ref.shape, 0)
        lo, hi = _unpack_bf16_pair(jnp.where(rows < valid, x_ref[...], 0))
        lo, hi = lo.astype(BF16), hi.astype(BF16)

        def proj(w_scr):
            return (jnp.dot(lo, w_scr[0:half, :], preferred_element_type=F32)
                    + jnp.dot(hi, w_scr[half:2 * half, :], preferred_element_type=F32))

        hid = _silu(proj(w1_scr)) * proj(w3_scr)
        y = jnp.dot(hid.astype(BF16), w2_scr[...], preferred_element_type=F32)
        y_ref[...] = _pack_bf16_pair(y[:, :half], y[:, half:])

    @pl.when(valid == 0)
    def _():
        y_ref[...] = jnp.zeros(y_ref.shape, U32)


def _experts(xg, blk_e, blk_first, blk_valid, w1, w3, w2, layer):
    n_rows, half = xg.shape
    n_blocks = n_rows // EXPERT_ROWS
    d, de = w1.shape[-2:]
    grid_spec = pltpu.PrefetchScalarGridSpec(
        num_scalar_prefetch=3,
        grid=(n_blocks,),
        in_specs=[pl.BlockSpec((EXPERT_ROWS, half), lambda j, be, bf, bv: (j, 0)),
                  pl.BlockSpec((None, 1, d, de), lambda j, be, bf, bv: (layer, be[j], 0, 0)),
                  pl.BlockSpec((None, 1, d, de), lambda j, be, bf, bv: (layer, be[j], 0, 0)),
                  pl.BlockSpec((None, 1, de, d), lambda j, be, bf, bv: (layer, be[j], 0, 0))],
        out_specs=pl.BlockSpec((EXPERT_ROWS, half), lambda j, be, bf, bv: (j, 0)),
        scratch_shapes=[pltpu.VMEM((d, de), BF16), pltpu.VMEM((d, de), BF16),
                        pltpu.VMEM((de, d), BF16)],
    )
    return pl.pallas_call(
        _expert_body,
        grid_spec=grid_spec,
        out_shape=jax.ShapeDtypeStruct((n_rows, half), U32),
        compiler_params=_cparams(("arbitrary",)),
        name="moe_experts",
    )(blk_e, blk_first, blk_valid, xg, w1, w3, w2)


def _combine_body(h_ref, u_ref, yg_ref, wts_ref, gt_ref, s1_ref, s3_ref, s2_ref, fw_ref, o_ref, *,
                  final_norm):
    half = u_ref.shape[1]
    lo, hi = _unpack_bf16_pair(u_ref[...])
    lo, hi = lo.astype(BF16), hi.astype(BF16)

    def proj(w_ref):
        return (jnp.dot(lo, w_ref[0:half, :], preferred_element_type=F32)
                + jnp.dot(hi, w_ref[half:2 * half, :], preferred_element_type=F32))

    hid = _silu(proj(s1_ref)) * proj(s3_ref)
    shared = jnp.dot(hid.astype(BF16), s2_ref[...], preferred_element_type=F32)

    routed_lo = routed_hi = None
    for kk in range(TOP_K):
        y_lo, y_hi = _unpack_bf16_pair(yg_ref[kk])
        w = wts_ref[:, kk:kk + 1]
        routed_lo = y_lo * w if routed_lo is None else routed_lo + y_lo * w
        routed_hi = y_hi * w if routed_hi is None else routed_hi + y_hi * w
    gt = gt_ref[0]
    out_lo = h_ref[:, 0:half] + gt[:, 0:half] * (routed_lo + shared[:, 0:half])
    out_hi = (h_ref[:, half:2 * half]
              + gt[:, half:2 * half] * (routed_hi + shared[:, half:2 * half]))
    if final_norm:
        ssq = (jnp.sum(out_lo * out_lo, axis=-1, keepdims=True)
               + jnp.sum(out_hi * out_hi, axis=-1, keepdims=True))
        inv = lax.rsqrt(ssq / (2 * half) + EPS)
        out_lo = (out_lo * inv) * fw_ref[:, 0:half]
        out_hi = (out_hi * inv) * fw_ref[:, half:2 * half]
    o_ref[:, 0:half] = out_lo
    o_ref[:, half:2 * half] = out_hi


def _combine(h, u, yg, wts_tk, gt, s1, s3, s2, final_w, seq, final_norm, part, n_parts):
    t, d = h.shape
    half = d // 2
    tm = min(COMBINE_ROWS, seq)
    per_b = seq // tm
    ds_ = s1.shape[-1]
    steps = t // n_parts // tm
    off = part * steps
    return pl.pallas_call(
        functools.partial(_combine_body, final_norm=final_norm),
        grid=(steps,),
        in_specs=[pl.BlockSpec((tm, d), lambda i: (i + off, 0)),
                  pl.BlockSpec((tm, half), lambda i: (i + off, 0)),
                  pl.BlockSpec((TOP_K, tm, half), lambda i: (0, i, 0)),
                  pl.BlockSpec((tm, TOP_K), lambda i: (i + off, 0)),
                  pl.BlockSpec((1, 1, d), lambda i: ((i + off) // per_b, 0, 0)),
                  pl.BlockSpec((d, ds_), lambda i: (0, 0)),
                  pl.BlockSpec((d, ds_), lambda i: (0, 0)),
                  pl.BlockSpec((ds_, d), lambda i: (0, 0)),
                  pl.BlockSpec((1, d), lambda i: (0, 0))],
        out_specs=pl.BlockSpec((tm, d), lambda i: (i + off, 0)),
        out_shape=jax.ShapeDtypeStruct((t, d), F32),
        input_output_aliases={0: 0},
        compiler_params=_cparams(("parallel",)),
        name="moe_combine",
    )(h, u, yg, wts_tk, gt, s1, s3, s2, final_w)


def _split_w_in(w_in):
    sizes = (2 * M_HEADS * M_DQK, M_HEADS * M_DV, M_HEADS * M_DV, M_HEADS, M_HEADS,
             G_HEADS * G_DK, G_HEADS * G_DK, G_HEADS * G_DV, G_RANK, G_HEADS * G_DV,
             D_MODEL, D_MODEL)
    offs = [0]
    for n in sizes:
        offs.append(offs[-1] + n)
    w16 = w_in.astype(BF16)
    big = jnp.concatenate([w16[:, offs[0]:offs[3]], w16[:, offs[5]:offs[8]], w16[:, offs[9]:offs[12]]],
                          axis=1)
    pad = jnp.zeros((w_in.shape[0], SMALL_COLS - 2 * M_HEADS - G_RANK), BF16)
    small = jnp.concatenate([w16[:, offs[3]:offs[5]], w16[:, offs[8]:offs[9]], pad], axis=1)
    return big, small


def _prepare_mixer_weights(w_in, w_pa, w_pb, w_o, layer, tie=None):
    if tie is not None:
        w_in, w_pa, w_pb, w_o, _ = lax.optimization_barrier((w_in, w_pa, w_pb, w_o, tie))
    w_big, w_small = _split_w_in(w_in[layer])
    return (w_big, w_small, w_pa[layer].astype(BF16), w_pb[layer].astype(BF16),
            w_o[layer].astype(BF16))


def _moe_layout(counts, eidx, pos, n_blocks):
    padded = (counts + EXPERT_ROWS - 1) // EXPERT_ROWS * EXPERT_ROWS
    pend = jnp.cumsum(padded)
    pstart = pend - padded
    experts = jnp.arange(N_EXPERTS, dtype=I32)
    dest = pos + jnp.sum(jnp.where(eidx[..., None] == experts, pstart, 0), axis=-1)
    blk_start = jnp.arange(n_blocks, dtype=I32) * EXPERT_ROWS
    owner = jnp.sum((pend[None, :] <= blk_start[:, None]).astype(I32), axis=1)
    blk_e = jnp.minimum(owner, N_EXPERTS - 1)
    prev = jnp.concatenate([jnp.full((1,), -1, I32), blk_e[:-1]])
    blk_first = (blk_e != prev).astype(I32)
    own = blk_e[:, None] == experts[None, :]
    rows_left = jnp.sum(jnp.where(own, (pstart + counts)[None, :], 0), axis=1) - blk_start
    blk_valid = jnp.clip(jnp.where(owner < N_EXPERTS, rows_left, 0), 0, EXPERT_ROWS)
    return dest.astype(I32), blk_e.astype(I32), blk_first, blk_valid.astype(I32)


def kernel(x, c, ada_w, ada_b, norm_mix_w, norm_moe_w, w_in, m_conv_w, m_gate_b, m_norm_w,
           g_alpha_w, g_alpha_b, g_norm_w, w_pa, w_pb, w_o, router_w, router_b,
           exp_w1, exp_w3, exp_w2, sh_w1, sh_w3, sh_w2, final_norm_w):
    bsz, seq, d = x.shape
    depth = ada_w.shape[0]
    t = bsz * seq
    n_rows = t * TOP_K + N_EXPERTS * EXPERT_ROWS
    n_blocks = n_rows // EXPERT_ROWS

    ada = _ada(c, ada_w, ada_b).reshape(depth, bsz, 6, 1, d)
    h = x.reshape(t, d)
    prepared = _prepare_mixer_weights(w_in, w_pa, w_pb, w_o, 0)
    for l in range(depth):
        sh1, sc1, gt1, sh2, sc2, gt2 = (ada[l, :, i] for i in range(6))

        w_big, w_small, w_pa_l, w_pb_l, w_o_l = prepared
        mix_w = norm_mix_w[l][None, :]
        gate_row = jnp.zeros((1, SMALL_COLS), F32)
        gate_row = gate_row.at[0, SMALL_I:SMALL_I + M_HEADS].set(m_gate_b[l, 0])
        gate_row = gate_row.at[0, SMALL_F:SMALL_F + M_HEADS].set(m_gate_b[l, 1])
        ha = _mlstm(h, sh1, sc1, mix_w, w_big, w_small, m_conv_w[l], gate_row, m_norm_w[l][None, :],
                    bsz, seq)
        alpha_full = jnp.zeros((SMALL_COLS, G_HEADS * G_DK), F32)
        alpha_full = alpha_full.at[SMALL_R:SMALL_R + G_RANK].set(g_alpha_w[l]).astype(BF16)
        hb = _gla(h, sh1, sc1, mix_w, w_big, w_small, alpha_full, g_alpha_b[l][None, :],
                  g_norm_w[l][None, :], bsz, seq)
        h = _mixout(h, sh1, sc1, mix_w, w_big, ha, hb, gt1, w_pa_l, w_pb_l, w_o_l, seq)

        u, eidx, wts, pos, cnt = _route(h, sh2, sc2, norm_moe_w[l][None, :],
                                        router_w[l].T, router_b[l][:, None], seq)
        dest, blk_e, blk_first, blk_valid = _moe_layout(cnt[:, 0], eidx, pos, n_blocks)
        xg = _sc_scatter_rows(u, dest, n_rows)
        if l + 1 < depth:
            prepared = _prepare_mixer_weights(w_in, w_pa, w_pb, w_o, l + 1, tie=dest)
        y = _experts(xg, blk_e, blk_first, blk_valid, exp_w1, exp_w3, exp_w2, l)
        tp = t // COMBINE_PARTS
        for p in range(COMBINE_PARTS):
            dest_p = dest[:, p * tp:(p + 1) * tp].reshape(-1)
            yg = _sc_gather_rows(y, dest_p).reshape(TOP_K, tp, d // 2)
            h = _combine(h, u, yg, wts.T, gt2, sh_w1[l].astype(BF16), sh_w3[l].astype(BF16),
                         sh_w2[l].astype(BF16), final_norm_w[None, :], seq,
                         final_norm=(l == depth - 1), part=p, n_parts=COMBINE_PARTS)

    return h.reshape(bsz, seq, d)
```

```python
import functools

import jax
import jax.numpy as jnp
import numpy as np
from jax import lax
from jax.experimental import pallas as pl
from jax.experimental.pallas import tpu as pltpu
from jax.experimental.pallas import tpu_sc as plsc

F32 = jnp.float32
BF16 = jnp.bfloat16
I32 = jnp.int32
U32 = jnp.uint32
HI_MASK = np.uint32(0xFFFF0000)

LANES = 128

SC_CORES = 2
SC_SUBCORES = 16
SC_WORKERS = SC_CORES * SC_SUBCORES
SC_CHUNK = 64

D_MODEL = 1024
M_HEADS = 4
M_DQK = 128
M_DV = 256
M_CONV = 4
GATE_CAP = 15.0
G_HEADS = 4
G_DK = 128
G_DV = 256
G_RANK = 16
G_TAU = 16.0
G_CHUNK = 64
N_EXPERTS = 64
TOP_K = 8
N_GROUPS = 8
GROUP_SIZE = N_EXPERTS // N_GROUPS
TOPK_GROUPS = 4
ROUTED_SCALE = 2.5
EPS = 1e-6

M_CHUNK_ROWS = 256
G_BLOCK_ROWS = 256
SEQ_GROUP = 2
PROJ_PIECE_COLS = 256
MIX_ROWS = 512
ROUTE_ROWS = 1024
EXPERT_ROWS = 1024
COMBINE_ROWS = 512
COMBINE_PARTS = 4
CONV_HALO = 8
VMEM_LIMIT = 48 * 1024 * 1024

M_GROUP_COLS = 2 * M_HEADS * M_DQK + 2 * M_HEADS * M_DV
G_GROUP_COLS = 2 * G_HEADS * G_DK + 2 * G_HEADS * G_DV
GATE_GROUP_COLS = 2 * D_MODEL
COL_M_GROUP = 0
COL_G_GROUP = COL_M_GROUP + M_GROUP_COLS
COL_GATE_GROUP = COL_G_GROUP + G_GROUP_COLS
SMALL_COLS = LANES
SMALL_I, SMALL_F, SMALL_R = 0, M_HEADS, 2 * M_HEADS


def _cparams(sem, vmem=VMEM_LIMIT):
    return pltpu.CompilerParams(dimension_semantics=sem, vmem_limit_bytes=vmem)


def _silu(x):
    return x * jax.nn.sigmoid(x)


def _log_sigmoid(x):
    return jnp.minimum(x, 0.0) - jnp.log1p(jnp.exp(-jnp.abs(x)))


def _modulated_rmsnorm(x, w, sc, sh):
    y = x * lax.rsqrt(jnp.mean(x * x, axis=-1, keepdims=True) + EPS)
    return (y * w) * (1.0 + sc) + sh


def _pack_bf16_pair(lo, hi):
    lo_bits = lax.bitcast_convert_type(lo.astype(BF16).astype(F32), U32)
    hi_bits = lax.bitcast_convert_type(hi.astype(BF16).astype(F32), U32)
    return (lo_bits >> 16) | (hi_bits & HI_MASK)


def _unpack_bf16_pair(packed):
    lo = lax.bitcast_convert_type(packed << 16, F32)
    hi = lax.bitcast_convert_type(packed & HI_MASK, F32)
    return lo, hi


def _lower_tri(n, dtype):
    r = lax.broadcasted_iota(I32, (n, n), 0)
    c = lax.broadcasted_iota(I32, (n, n), 1)
    return (r >= c).astype(dtype)


def _split3_dot(lhs01, x):
    hi = x.astype(BF16)
    r1 = x - hi.astype(F32)
    mid = r1.astype(BF16)
    lo = (r1 - mid.astype(F32)).astype(BF16)
    return (jnp.dot(lhs01, hi, preferred_element_type=F32)
            + jnp.dot(lhs01, mid, preferred_element_type=F32)
            + jnp.dot(lhs01, lo, preferred_element_type=F32))


def _split2_dot_nt(a, b):
    nt = (((1,), (1,)), ((), ()))
    a_hi = a.astype(BF16)
    a_lo = (a - a_hi.astype(F32)).astype(BF16)
    b_hi = b.astype(BF16)
    b_lo = (b - b_hi.astype(F32)).astype(BF16)
    return (lax.dot_general(a_hi, b_hi, nt, preferred_element_type=F32)
            + lax.dot_general(a_hi, b_lo, nt, preferred_element_type=F32)
            + lax.dot_general(a_lo, b_hi, nt, preferred_element_type=F32))


def _ada_body(c_ref, w_ref, b_ref, o_ref):
    cond = _silu(c_ref[...])
    o_ref[0] = jnp.dot(cond.astype(BF16), w_ref[0].astype(BF16),
                       preferred_element_type=F32) + b_ref[0]


def _ada(c, ada_w, ada_b):
    depth, d, six_d = ada_w.shape
    bsz = c.shape[0]
    nj = six_d // d
    return pl.pallas_call(
        _ada_body,
        grid=(depth, nj),
        in_specs=[pl.BlockSpec((bsz, d), lambda l, j: (0, 0)),
                  pl.BlockSpec((1, d, d), lambda l, j: (l, 0, j)),
                  pl.BlockSpec((1, 1, d), lambda l, j: (l, 0, j))],
        out_specs=pl.BlockSpec((1, bsz, d), lambda l, j: (l, 0, j)),
        out_shape=jax.ShapeDtypeStruct((depth, bsz, six_d), F32),
        compiler_params=_cparams(("parallel", "parallel")),
        name="ada_ln",
    )(c, ada_w, ada_b.reshape(depth, 1, six_d))


def _mlstm_body(h0_ref, hn_ref, sh_ref, sc_ref, mw_ref, w_ref, ws_ref, cw_ref, gb_ref, nw_ref,
                ha_ref, qk_nxt, v_nxt, o_nxt, sm_nxt, xe_scr, v_scr, o_scr, sm_scr,
                c_scr, n_scr, m_scr):
    n_seq, rows = hn_ref.shape[0], hn_ref.shape[1]
    half = M_HEADS * M_DQK
    vcols = M_HEADS * M_DV

    def projection_pieces(h_ref):
        u = jnp.concatenate(
            [_modulated_rmsnorm(h_ref[g], mw_ref[...], sc_ref[g], sh_ref[g]) for g in range(n_seq)],
            axis=0).astype(BF16)

        def piece(dst, w_lo, lo, width):
            def run():
                dst[:, lo:lo + width] = jnp.dot(u, w_ref[:, w_lo + lo:w_lo + lo + width],
                                                preferred_element_type=F32).astype(dst.dtype)
            return run

        def small_piece():
            sm_nxt[...] = jnp.dot(u, ws_ref[...], preferred_element_type=F32)

        pieces = [small_piece]
        for dst, w_lo in ((qk_nxt, 0), (v_nxt, 2 * half), (o_nxt, 2 * half + vcols)):
            for lo in range(0, vcols, PROJ_PIECE_COLS):
                pieces.append(piece(dst, w_lo, lo, PROJ_PIECE_COLS))
        return pieces

    @pl.when(pl.program_id(1) == 0)
    def _():
        xe_scr[:, 0:CONV_HALO, :] = jnp.zeros((n_seq, CONV_HALO, 2 * half), F32)
        c_scr[...] = jnp.zeros(c_scr.shape, F32)
        n_scr[...] = jnp.zeros(n_scr.shape, F32)
        m_scr[...] = jnp.zeros(m_scr.shape, F32)
        for run in projection_pieces(h0_ref):
            run()

    for g in range(n_seq):
        xe_scr[g, CONV_HALO:CONV_HALO + rows, :] = qk_nxt[g * rows:(g + 1) * rows, :]
    v_scr[...] = v_nxt[...]
    o_scr[...] = o_nxt[...]
    sm_scr[...] = sm_nxt[...]
    pending = projection_pieces(hn_ref)

    r_io = lax.broadcasted_iota(I32, (rows, rows), 0)
    c_io = lax.broadcasted_iota(I32, (rows, rows), 1)
    causal = r_io >= c_io
    tri = _lower_tri(rows, BF16)

    for g in range(n_seq):
        gr = slice(g * rows, (g + 1) * rows)
        conv = None
        for j in range(M_CONV):
            off = CONV_HALO - (M_CONV - 1) + j
            term = xe_scr[g, off:off + rows, :] * cw_ref[j:j + 1, :]
            conv = term if conv is None else conv + term
        qk = _silu(conv)
        xe_scr[g, 0:CONV_HALO, :] = xe_scr[g, rows:rows + CONV_HALO, :]

        capped = GATE_CAP * jnp.tanh((sm_scr[gr, :] + gb_ref[...]) / GATE_CAP)
        li_all = capped
        lf_all = _log_sigmoid(capped)
        b_all = _split3_dot(tri, lf_all)
        li_t = li_all.T
        b_t = b_all.T

        for h in range(M_HEADS):
            if pending:
                pending.pop(0)()
            sidx = g * M_HEADS + h
            q = (qk[:, h * M_DQK:(h + 1) * M_DQK] * (M_DQK ** -0.5)).astype(BF16)
            k = qk[:, half + h * M_DQK:half + (h + 1) * M_DQK]
            kb = k.astype(BF16)
            v = v_scr[gr, h * M_DV:(h + 1) * M_DV]
            li_c = li_all[:, SMALL_I + h:SMALL_I + h + 1]
            b_c = b_all[:, SMALL_F + h:SMALL_F + h + 1]
            li_r = li_t[SMALL_I + h:SMALL_I + h + 1, :]
            b_r = b_t[SMALL_F + h:SMALL_F + h + 1, :]
            gsum = b_c[rows - 1:rows, :]
            m_prev = m_scr[sidx][:, 0:1]
            c_prev = c_scr[sidx]
            n_prev = n_scr[sidx]

            d_mat = jnp.where(causal, b_c - b_r + li_r, -jnp.inf)
            m_inter = b_c + m_prev
            m_t = jnp.maximum(jnp.max(d_mat, axis=1, keepdims=True), m_inter)
            s = lax.dot_general(q, kb, (((1,), (1,)), ((), ())), preferred_element_type=F32)
            p = jnp.exp(d_mat - m_t) * s
            w_inter = jnp.exp(m_inter - m_t)
            num = (jnp.dot(p.astype(BF16), v, preferred_element_type=F32)
                   + w_inter * jnp.dot(q, c_prev.astype(BF16), preferred_element_type=F32))
            qn = jnp.sum(q.astype(F32) * n_prev, axis=1, keepdims=True)
            den = jnp.sum(p, axis=1, keepdims=True) + w_inter * qn
            hh = num / jnp.maximum(jnp.abs(den), jnp.exp(-m_t))

            if pending:
                pending.pop(0)()

            a_r = gsum - b_r + li_r
            a_c = gsum - b_c + li_c
            m_new = jnp.maximum(gsum + m_prev, jnp.max(a_r, axis=1, keepdims=True))
            decay = jnp.exp(gsum + m_prev - m_new)
            wk = jnp.exp(a_c - m_new) * k
            c_scr[sidx] = decay * c_prev + lax.dot_general(
                wk.astype(BF16), v, (((0,), (0,)), ((), ())), preferred_element_type=F32)
            n_scr[sidx] = decay * n_prev + jnp.sum(wk, axis=0, keepdims=True)
            m_scr[sidx] = jnp.broadcast_to(m_new, m_scr.shape[1:])

            y = hh * lax.rsqrt(jnp.mean(hh * hh, axis=-1, keepdims=True) + EPS)
            y = y * nw_ref[:, h * M_DV:(h + 1) * M_DV]
            gate = jax.nn.sigmoid(o_scr[gr, h * M_DV:(h + 1) * M_DV])
            ha_ref[g, :, h * M_DV:(h + 1) * M_DV] = (y * gate).astype(BF16)
    for run in pending:
        run()


def _mlstm(h, sh, sc, mix_w, w_big, w_small, conv_w, gate_row, norm_w, bsz, seq):
    t, d = h.shape
    rows = min(M_CHUNK_ROWS, seq)
    nc = seq // rows
    ng = SEQ_GROUP if bsz % SEQ_GROUP == 0 else 1
    half = M_HEADS * M_DQK
    vcols = M_HEADS * M_DV
    wcols = M_GROUP_COLS
    assert COL_M_GROUP % wcols == 0
    resident = pl.Buffered(1)
    nxt = [pltpu.VMEM((ng * rows, 2 * half), F32), pltpu.VMEM((ng * rows, vcols), BF16),
           pltpu.VMEM((ng * rows, vcols), F32), pltpu.VMEM((ng * rows, SMALL_COLS), F32)]
    out = pl.pallas_call(
        _mlstm_body,
        grid=(bsz // ng, nc),
        in_specs=[pl.BlockSpec((ng, rows, d), lambda b, c: (b, 0, 0)),
                  pl.BlockSpec((ng, rows, d), lambda b, c: (b, jnp.minimum(c + 1, nc - 1), 0)),
                  pl.BlockSpec((ng, 1, d), lambda b, c: (b, 0, 0)),
                  pl.BlockSpec((ng, 1, d), lambda b, c: (b, 0, 0)),
                  pl.BlockSpec((1, d), lambda b, c: (0, 0)),
                  pl.BlockSpec((d, wcols), lambda b, c: (0, 0), pipeline_mode=resident),
                  pl.BlockSpec((d, SMALL_COLS), lambda b, c: (0, 0), pipeline_mode=resident),
                  pl.BlockSpec((M_CONV, 2 * half), lambda b, c: (0, 0)),
                  pl.BlockSpec((1, SMALL_COLS), lambda b, c: (0, 0)),
                  pl.BlockSpec((1, vcols), lambda b, c: (0, 0))],
        out_specs=pl.BlockSpec((ng, rows, vcols), lambda b, c: (b, c, 0)),
        out_shape=jax.ShapeDtypeStruct((bsz, seq, vcols), BF16),
        scratch_shapes=nxt + [pltpu.VMEM((ng, rows + CONV_HALO, 2 * half), F32)] + nxt[1:]
        + [pltpu.VMEM((ng * M_HEADS, M_DQK, M_DV), F32),
           pltpu.VMEM((ng * M_HEADS, 1, M_DQK), F32),
           pltpu.VMEM((ng * M_HEADS, 1, LANES), F32)],
        compiler_params=_cparams(("parallel", "arbitrary")),
        name="mlstm",
    )(h.reshape(bsz, seq, d), h.reshape(bsz, seq, d), sh, sc, mix_w, w_big, w_small, conv_w,
      gate_row, norm_w)
    return out.reshape(t, vcols)


def _gla_body(h0_ref, hn_ref, sh_ref, sc_ref, mw_ref, w_ref, ws_ref, aw_ref, ab_ref, nw_ref, hb_ref,
              qk_nxt, v_nxt, z_nxt, sm_nxt, qk_scr, v_scr, z_scr, sm_scr, st_scr):
    n_seq, rows = hn_ref.shape[0], hn_ref.shape[1]
    n_chunks = rows // G_CHUNK
    kcols = G_HEADS * G_DK
    vcols = G_HEADS * G_DV

    def projection_pieces(h_ref):
        u = jnp.concatenate(
            [_modulated_rmsnorm(h_ref[g], mw_ref[...], sc_ref[g], sh_ref[g]) for g in range(n_seq)],
            axis=0).astype(BF16)

        def piece(dst, w_lo, lo, width):
            def run():
                dst[:, lo:lo + width] = jnp.dot(u, w_ref[:, w_lo + lo:w_lo + lo + width],
                                                preferred_element_type=F32).astype(dst.dtype)
            return run

        def small_piece():
            sm_nxt[...] = jnp.dot(u, ws_ref[...], preferred_element_type=F32)

        pieces = [small_piece]
        for dst, w_lo in ((qk_nxt, 0), (v_nxt, 2 * kcols), (z_nxt, 2 * kcols + vcols)):
            for lo in range(0, vcols, PROJ_PIECE_COLS):
                pieces.append(piece(dst, w_lo, lo, PROJ_PIECE_COLS))
        return pieces

    @pl.when(pl.program_id(1) == 0)
    def _():
        st_scr[...] = jnp.zeros(st_scr.shape, F32)
        for run in projection_pieces(h0_ref):
            run()

    qk_scr[...] = qk_nxt[...]
    v_scr[...] = v_nxt[...]
    z_scr[...] = z_nxt[...]
    sm_scr[...] = sm_nxt[...]
    pending = projection_pieces(hn_ref)

    r_io = lax.broadcasted_iota(I32, (rows, rows), 0)
    c_io = lax.broadcasted_iota(I32, (rows, rows), 1)
    chunk_causal = jnp.logical_and(r_io >= c_io, r_io // G_CHUNK == c_io // G_CHUNK)
    chunk_tri = jnp.where(chunk_causal, 1.0, 0.0).astype(BF16)

    for g in range(n_seq):
        gr = slice(g * rows, (g + 1) * rows)
        for _ in range((len(pending) + n_seq - 1 - g) // (n_seq - g)):
            pending.pop(0)()
        q = qk_scr[gr, 0:kcols] * (G_DK ** -0.5)
        k = qk_scr[gr, kcols:2 * kcols]

        logits = jnp.dot(sm_scr[gr, :].astype(BF16), aw_ref[...], preferred_element_type=F32)
        la = _log_sigmoid(logits + ab_ref[...]) / G_TAU
        bc = _split3_dot(chunk_tri, la)
        gcs = [bc[(ci + 1) * G_CHUNK - 1:(ci + 1) * G_CHUNK, :] for ci in range(n_chunks)]
        gc_rows = jnp.concatenate([jnp.broadcast_to(gc, (G_CHUNK, kcols)) for gc in gcs], axis=0)

        q_in = (q * jnp.exp(bc)).astype(BF16)
        k_in = (k * jnp.exp(-bc)).astype(BF16)
        k_out = (k * jnp.exp(gc_rows - bc)).astype(BF16)

        for h in range(G_HEADS):
            ks = slice(h * G_DK, (h + 1) * G_DK)
            vs = slice(h * G_DV, (h + 1) * G_DV)
            v = v_scr[gr, vs]
            att = lax.dot_general(q_in[:, ks], k_in[:, ks], (((1,), (1,)), ((), ())),
                                  preferred_element_type=F32)
            att = jnp.where(chunk_causal, att, 0.0).astype(BF16)
            o_intra = jnp.dot(att, v, preferred_element_type=F32)
            st = st_scr[g * G_HEADS + h]
            outs = []
            for ci in range(n_chunks):
                rs = slice(ci * G_CHUNK, (ci + 1) * G_CHUNK)
                o_inter = lax.dot_general(q_in[rs, ks], st.astype(BF16),
                                          (((1,), (1,)), ((), ())), preferred_element_type=F32)
                outs.append(o_intra[rs, :] + o_inter)
                st = jnp.exp(gcs[ci][:, ks]) * st + lax.dot_general(
                    v[rs, :], k_out[rs, ks], (((0,), (0,)), ((), ())), preferred_element_type=F32)
            st_scr[g * G_HEADS + h] = st
            o = jnp.concatenate(outs, axis=0)
            y = o * lax.rsqrt(jnp.mean(o * o, axis=-1, keepdims=True) + EPS)
            y = y * nw_ref[:, vs]
            hb_ref[g, :, vs] = (y * _silu(z_scr[gr, vs])).astype(BF16)


def _gla(h, sh, sc, mix_w, w_big, w_small, alpha_full, alpha_b, norm_w, bsz, seq):
    t, d = h.shape
    rows = min(G_BLOCK_ROWS, seq)
    nb = seq // rows
    ng = SEQ_GROUP if bsz % SEQ_GROUP == 0 else 1
    kcols = G_HEADS * G_DK
    vcols = G_HEADS * G_DV
    wcols = G_GROUP_COLS
    assert COL_G_GROUP % wcols == 0
    resident = pl.Buffered(1)
    nxt = [pltpu.VMEM((ng * rows, 2 * kcols), F32), pltpu.VMEM((ng * rows, vcols), BF16),
           pltpu.VMEM((ng * rows, vcols), F32), pltpu.VMEM((ng * rows, SMALL_COLS), F32)]
    out = pl.pallas_call(
        _gla_body,
        grid=(bsz // ng, nb),
        in_specs=[pl.BlockSpec((ng, rows, d), lambda b, c: (b, 0, 0)),
                  pl.BlockSpec((ng, rows, d), lambda b, c: (b, jnp.minimum(c + 1, nb - 1), 0)),
                  pl.BlockSpec((ng, 1, d), lambda b, c: (b, 0, 0)),
                  pl.BlockSpec((ng, 1, d), lambda b, c: (b, 0, 0)),
                  pl.BlockSpec((1, d), lambda b, c: (0, 0)),
                  pl.BlockSpec((d, wcols), lambda b, c: (0, COL_G_GROUP // wcols),
                               pipeline_mode=resident),
                  pl.BlockSpec((d, SMALL_COLS), lambda b, c: (0, 0), pipeline_mode=resident),
                  pl.BlockSpec((SMALL_COLS, kcols), lambda b, c: (0, 0)),
                  pl.BlockSpec((1, kcols), lambda b, c: (0, 0)),
                  pl.BlockSpec((1, vcols), lambda b, c: (0, 0))],
        out_specs=pl.BlockSpec((ng, rows, vcols), lambda b, c: (b, c, 0)),
        out_shape=jax.ShapeDtypeStruct((bsz, seq, vcols), BF16),
        scratch_shapes=nxt + nxt + [pltpu.VMEM((ng * G_HEADS, G_DV, G_DK), F32)],
        compiler_params=_cparams(("parallel", "arbitrary")),
        name="gla",
    )(h.reshape(bsz, seq, d), h.reshape(bsz, seq, d), sh, sc, mix_w, w_big, w_small, alpha_full,
      alpha_b, norm_w)
    return out.reshape(t, vcols)


def _mixout_body(h_ref, sh_ref, sc_ref, mw_ref, wg_ref, ha_ref, hb_ref, gt_ref, wpa_ref, wpb_ref,
                 wo_ref, sh2_ref, sc2_ref, nw2_ref, rwt_ref, o_ref, u_ref, logits_ref):
    d = h_ref.shape[1]
    h = h_ref[...]
    u = _modulated_rmsnorm(h, mw_ref[...], sc_ref[0], sh_ref[0]).astype(BF16)
    ga = jnp.dot(u, wg_ref[:, 0:d], preferred_element_type=F32)
    gb = jnp.dot(u, wg_ref[:, d:2 * d], preferred_element_type=F32)
    a = jnp.dot(ha_ref[...], wpa_ref[...], preferred_element_type=F32)
    b = jnp.dot(hb_ref[...], wpb_ref[...], preferred_element_type=F32)
    y = jax.nn.sigmoid(ga) * a + jax.nn.sigmoid(gb) * b
    h_new = h + gt_ref[0] * jnp.dot(y.astype(BF16), wo_ref[...], preferred_element_type=F32)
    o_ref[...] = h_new
    u2 = _modulated_rmsnorm(h_new, nw2_ref[...], sc2_ref[0], sh2_ref[0])
    u_ref[...] = _pack_bf16_pair(u2[:, :d // 2], u2[:, d // 2:])
    logits_ref[...] = _split2_dot_nt(rwt_ref[...], u2)


def _mixout(h, sh, sc, mix_w, w_big, ha, hb, gt, w_pa, w_pb, w_o, sh2, sc2, moe_w, rw_t, seq):
    t, d = h.shape
    tm = min(MIX_ROWS, seq)
    per_b = seq // tm
    assert GATE_GROUP_COLS == 2 * d and COL_GATE_GROUP % GATE_GROUP_COLS == 0
    resident = pl.Buffered(1)
    wspec = pl.BlockSpec((d, d), lambda i: (0, 0), pipeline_mode=resident)
    bspec = pl.BlockSpec((1, 1, d), lambda i: (i // per_b, 0, 0))
    return pl.pallas_call(
        _mixout_body,
        grid=(t // tm,),
        in_specs=[pl.BlockSpec((tm, d), lambda i: (i, 0)),
                  bspec, bspec,
                  pl.BlockSpec((1, d), lambda i: (0, 0)),
                  pl.BlockSpec((d, 2 * d), lambda i: (0, COL_GATE_GROUP // GATE_GROUP_COLS),
                               pipeline_mode=resident),
                  pl.BlockSpec((tm, d), lambda i: (i, 0)),
                  pl.BlockSpec((tm, d), lambda i: (i, 0)),
                  bspec,
                  wspec, wspec, wspec,
                  bspec, bspec,
                  pl.BlockSpec((1, d), lambda i: (0, 0)),
                  pl.BlockSpec((N_EXPERTS, d), lambda i: (0, 0))],
        out_specs=[pl.BlockSpec((tm, d), lambda i: (i, 0)),
                   pl.BlockSpec((tm, d // 2), lambda i: (i, 0)),
                   pl.BlockSpec((N_EXPERTS, tm), lambda i: (0, i))],
        out_shape=[jax.ShapeDtypeStruct((t, d), F32),
                   jax.ShapeDtypeStruct((t, d // 2), U32),
                   jax.ShapeDtypeStruct((N_EXPERTS, t), F32)],
        compiler_params=_cparams(("parallel",)),
        name="mix_out",
    )(h, sh, sc, mix_w, w_big, ha, hb, gt, w_pa, w_pb, w_o, sh2, sc2, moe_w, rw_t)


def _route_body(logits_ref, rb_ref, eidx_ref, wts_ref, pos_ref, cnt_ref, carry_scr):
    tm = logits_ref.shape[1]

    @pl.when(pl.program_id(0) == 0)
    def _():
        carry_scr[...] = jnp.zeros(carry_scr.shape, F32)

    scores = jax.nn.sigmoid(logits_ref[...])
    sel = scores + rb_ref[...]

    neg = -jnp.inf
    sub_io = lax.broadcasted_iota(I32, (GROUP_SIZE, tm), 0)
    pieces = []
    for g in range(N_GROUPS):
        blk = sel[g * GROUP_SIZE:(g + 1) * GROUP_SIZE, :]
        m1 = jnp.max(blk, axis=0, keepdims=True)
        first = jnp.min(jnp.where(blk == m1, sub_io, GROUP_SIZE), axis=0, keepdims=True)
        m2 = jnp.max(jnp.where(sub_io == first, neg, blk), axis=0, keepdims=True)
        pieces.append(jnp.broadcast_to(m1 + m2, (GROUP_SIZE, tm)))
    gscore = jnp.concatenate(pieces, axis=0)

    e_io = lax.broadcasted_iota(I32, (N_EXPERTS, tm), 0)
    grp_io = e_io // GROUP_SIZE
    gmask = jnp.zeros((N_EXPERTS, tm), jnp.bool_)
    for _ in range(TOPK_GROUPS):
        mx = jnp.max(gscore, axis=0, keepdims=True)
        gi = jnp.min(jnp.where(gscore == mx, grp_io, N_GROUPS), axis=0, keepdims=True)
        hit = grp_io == gi
        gmask = jnp.logical_or(gmask, hit)
        gscore = jnp.where(hit, neg, gscore)

    cur = jnp.where(gmask, sel, neg)
    row_io = lax.broadcasted_iota(I32, (TOP_K, tm), 0)
    eidx = jnp.zeros((TOP_K, tm), I32)
    wraw = jnp.zeros((TOP_K, tm), F32)
    chosen = jnp.zeros((N_EXPERTS, tm), jnp.bool_)
    hits = []
    for kk in range(TOP_K):
        mx = jnp.max(cur, axis=0, keepdims=True)
        ei = jnp.min(jnp.where(cur == mx, e_io, N_EXPERTS), axis=0, keepdims=True)
        hit = e_io == ei
        hits.append(hit)
        sc_k = jnp.sum(jnp.where(hit, scores, 0.0), axis=0, keepdims=True)
        eidx = jnp.where(row_io == kk, ei, eidx)
        wraw = jnp.where(row_io == kk, sc_k, wraw)
        chosen = jnp.logical_or(chosen, hit)
        cur = jnp.where(hit, neg, cur)

    wsum = jnp.sum(wraw, axis=0, keepdims=True)
    wts_ref[...] = wraw / wsum * ROUTED_SCALE
    eidx_ref[...] = eidx

    chosen_f = jnp.where(chosen, 1.0, 0.0)
    r_io = lax.broadcasted_iota(I32, (tm, tm), 0)
    c_io = lax.broadcasted_iota(I32, (tm, tm), 1)
    strict_upper = jnp.where(r_io < c_io, 1.0, 0.0).astype(BF16)
    prefix = jnp.dot(chosen_f.astype(BF16), strict_upper, preferred_element_type=F32)
    rank = prefix + carry_scr[:, 0:1]
    pos = jnp.zeros((TOP_K, tm), F32)
    for kk in range(TOP_K):
        p_k = jnp.sum(jnp.where(hits[kk], rank, 0.0), axis=0, keepdims=True)
        pos = jnp.where(row_io == kk, p_k, pos)
    pos_ref[...] = pos.astype(I32)
    total = carry_scr[...] + jnp.sum(chosen_f, axis=1, keepdims=True)
    carry_scr[...] = total
    cnt_ref[...] = total.astype(I32)


def _route(logits_t, rb_col):
    t = logits_t.shape[1]
    tm = min(ROUTE_ROWS, t)
    kspec = pl.BlockSpec((TOP_K, tm), lambda i: (0, i))
    return pl.pallas_call(
        _route_body,
        grid=(t // tm,),
        in_specs=[pl.BlockSpec((N_EXPERTS, tm), lambda i: (0, i)),
                  pl.BlockSpec((N_EXPERTS, 1), lambda i: (0, 0))],
        out_specs=[kspec, kspec, kspec,
                   pl.BlockSpec((N_EXPERTS, LANES), lambda i: (0, 0))],
        out_shape=[jax.ShapeDtypeStruct((TOP_K, t), I32),
                   jax.ShapeDtypeStruct((TOP_K, t), F32),
                   jax.ShapeDtypeStruct((TOP_K, t), I32),
                   jax.ShapeDtypeStruct((N_EXPERTS, LANES), I32)],
        scratch_shapes=[pltpu.VMEM((N_EXPERTS, LANES), F32)],
        compiler_params=_cparams(("arbitrary",)),
        name="moe_route",
    )(logits_t, rb_col)


def _sc_worker_id():
    return lax.axis_index("s") * SC_CORES + lax.axis_index("c")


def _sc_scatter_rows(x, dest, n_rows):
    t, d = x.shape
    n_k = dest.shape[0]
    assert t % (SC_WORKERS * 2 * SC_CHUNK) == 0
    per_w = t // SC_WORKERS
    n_ch = per_w // SC_CHUNK
    dest4 = dest.reshape(n_k, SC_WORKERS, n_ch, SC_CHUNK).transpose(1, 2, 0, 3)
    mesh = plsc.VectorSubcoreMesh(core_axis_name="c", subcore_axis_name="s")

    @functools.partial(
        pl.kernel, mesh=mesh,
        out_type=jax.ShapeDtypeStruct((n_rows, d), x.dtype),
        scratch_types=[pltpu.VMEM((n_ch, n_k, SC_CHUNK), I32),
                       pltpu.VMEM((SC_CHUNK, d), x.dtype),
                       pltpu.VMEM((SC_CHUNK, d), x.dtype)] + [pltpu.SemaphoreType.DMA] * 4,
        name="moe_dispatch_sc",
    )
    def scatter_kernel(x_hbm, dest_hbm, out_hbm, idx_v, rows0, rows1, l0, l1, s0, s1):
        wid = _sc_worker_id()
        base = wid * per_w
        pltpu.sync_copy(dest_hbm.at[wid], idx_v)
        bufs = ((rows0, l0, s0), (rows1, l1, s1))

        def load(ci, b):
            rows, load_sem, _ = bufs[b]
            return pltpu.make_async_copy(x_hbm.at[pl.ds(base + ci * SC_CHUNK, SC_CHUNK)], rows,
                                         load_sem)

        def scatters(ci, b):
            rows, _, scatter_sem = bufs[b]
            return [pltpu.make_async_copy(rows, out_hbm.at[idx_v.at[ci, j]], scatter_sem)
                    for j in range(n_k)]

        load(0, 0).start()

        @pl.loop(0, n_ch, step=2)
        def _(ci):
            load(ci, 0).wait()
            for cp in scatters(ci, 0):
                cp.start()

            @pl.when(ci > 0)
            def _():
                for cp in scatters(ci - 1, 1):
                    cp.wait()

            load(ci + 1, 1).start()
            load(ci + 1, 1).wait()
            for cp in scatters(ci + 1, 1):
                cp.start()
            for cp in scatters(ci, 0):
                cp.wait()

            @pl.when(ci + 2 < n_ch)
            def _():
                load(ci + 2, 0).start()

        for cp in scatters(n_ch - 1, 1):
            cp.wait()

    return scatter_kernel(x, dest4)


def _sc_gather_rows(table, idx):
    n, d = idx.shape[0], table.shape[1]
    assert n % (SC_WORKERS * 2 * SC_CHUNK) == 0
    per_w = n // SC_WORKERS
    n_ch = per_w // SC_CHUNK
    idx3 = idx.reshape(SC_WORKERS, n_ch, SC_CHUNK)
    mesh = plsc.VectorSubcoreMesh(core_axis_name="c", subcore_axis_name="s")

    @functools.partial(
        pl.kernel, mesh=mesh,
        out_type=jax.ShapeDtypeStruct((n, d), table.dtype),
        scratch_types=[pltpu.VMEM((n_ch, SC_CHUNK), I32),
                       pltpu.VMEM((SC_CHUNK, d), table.dtype),
                       pltpu.VMEM((SC_CHUNK, d), table.dtype)] + [pltpu.SemaphoreType.DMA] * 4,
        name="moe_combine_sc",
    )
    def gather_kernel(table_hbm, idx_hbm, out_hbm, idx_v, rows0, rows1, g0, g1, w0, w1):
        wid = _sc_worker_id()
        base = wid * per_w
        pltpu.sync_copy(idx_hbm.at[wid], idx_v)
        bufs = ((rows0, g0, w0), (rows1, g1, w1))

        def gather(ci, b):
            rows, gather_sem, _ = bufs[b]
            return pltpu.make_async_copy(table_hbm.at[idx_v.at[ci]], rows, gather_sem)

        def write(ci, b):
            rows, _, write_sem = bufs[b]
            return pltpu.make_async_copy(rows, out_hbm.at[pl.ds(base + ci * SC_CHUNK, SC_CHUNK)],
                                         write_sem)

        gather(0, 0).start()

        @pl.loop(0, n_ch, step=2)
        def _(ci):
            gather(ci, 0).wait()
            write(ci, 0).start()

            @pl.when(ci > 0)
            def _():
                write(ci - 1, 1).wait()

            gather(ci + 1, 1).start()
            gather(ci + 1, 1).wait()
            write(ci + 1, 1).start()
            write(ci, 0).wait()

            @pl.when(ci + 2 < n_ch)
            def _():
                gather(ci + 2, 0).start()

        write(n_ch - 1, 1).wait()

    return gather_kernel(table, idx3)


def _expert_body(blk_e_ref, blk_first_ref, blk_valid_ref, x_ref, w1_ref, w3_ref, w2_ref, y_ref,
                 w1_scr, w3_scr, w2_scr):
    del blk_e_ref
    j = pl.program_id(0)
    valid = blk_valid_ref[j]

    @pl.when(blk_first_ref[j] == 1)
    def _():
        w1_scr[...] = w1_ref[0].astype(BF16)
        w3_scr[...] = w3_ref[0].astype(BF16)
        w2_scr[...] = w2_ref[0].astype(BF16)

    @pl.when(valid > 0)
    def _():
        half = x_ref.shape[1]
        rows = lax.broadcasted_iota(I32, x_ref.shape, 0)
        lo, hi = _unpack_bf16_pair(jnp.where(rows < valid, x_ref[...], 0))
        lo, hi = lo.astype(BF16), hi.astype(BF16)

        def proj(w_scr):
            return (jnp.dot(lo, w_scr[0:half, :], preferred_element_type=F32)
                    + jnp.dot(hi, w_scr[half:2 * half, :], preferred_element_type=F32))

        hid = _silu(proj(w1_scr)) * proj(w3_scr)
        y = jnp.dot(hid.astype(BF16), w2_scr[...], preferred_element_type=F32)
        y_ref[...] = _pack_bf16_pair(y[:, :half], y[:, half:])

    @pl.when(valid == 0)
    def _():
        y_ref[...] = jnp.zeros(y_ref.shape, U32)


def _experts(xg, blk_e, blk_first, blk_valid, w1, w3, w2, layer):
    n_rows, half = xg.shape
    n_blocks = n_rows // EXPERT_ROWS
    d, de = w1.shape[-2:]
    grid_spec = pltpu.PrefetchScalarGridSpec(
        num_scalar_prefetch=3,
        grid=(n_blocks,),
        in_specs=[pl.BlockSpec((EXPERT_ROWS, half), lambda j, be, bf, bv: (j, 0)),
                  pl.BlockSpec((None, 1, d, de), lambda j, be, bf, bv: (layer, be[j], 0, 0)),
                  pl.BlockSpec((None, 1, d, de), lambda j, be, bf, bv: (layer, be[j], 0, 0)),
                  pl.BlockSpec((None, 1, de, d), lambda j, be, bf, bv: (layer, be[j], 0, 0))],
        out_specs=pl.BlockSpec((EXPERT_ROWS, half), lambda j, be, bf, bv: (j, 0)),
        scratch_shapes=[pltpu.VMEM((d, de), BF16), pltpu.VMEM((d, de), BF16),
                        pltpu.VMEM((de, d), BF16)],
    )
    return pl.pallas_call(
        _expert_body,
        grid_spec=grid_spec,
        out_shape=jax.ShapeDtypeStruct((n_rows, half), U32),
        compiler_params=_cparams(("arbitrary",)),
        name="moe_experts",
    )(blk_e, blk_first, blk_valid, xg, w1, w3, w2)


def _combine_body(h_ref, u_ref, yg_ref, wts_ref, gt_ref, s1_ref, s3_ref, s2_ref, fw_ref, o_ref, *,
                  final_norm):
    half = u_ref.shape[1]
    lo, hi = _unpack_bf16_pair(u_ref[...])
    lo, hi = lo.astype(BF16), hi.astype(BF16)

    def proj(w_ref):
        return (jnp.dot(lo, w_ref[0:half, :], preferred_element_type=F32)
                + jnp.dot(hi, w_ref[half:2 * half, :], preferred_element_type=F32))

    hid = _silu(proj(s1_ref)) * proj(s3_ref)
    shared = jnp.dot(hid.astype(BF16), s2_ref[...], preferred_element_type=F32)

    routed_lo = routed_hi = None
    for kk in range(TOP_K):
        y_lo, y_hi = _unpack_bf16_pair(yg_ref[kk])
        w = wts_ref[:, kk:kk + 1]
        routed_lo = y_lo * w if routed_lo is None else routed_lo + y_lo * w
        routed_hi = y_hi * w if routed_hi is None else routed_hi + y_hi * w
    gt = gt_ref[0]
    out_lo = h_ref[:, 0:half] + gt[:, 0:half] * (routed_lo + shared[:, 0:half])
    out_hi = (h_ref[:, half:2 * half]
              + gt[:, half:2 * half] * (routed_hi + shared[:, half:2 * half]))
    if final_norm:
        ssq = (jnp.sum(out_lo * out_lo, axis=-1, keepdims=True)
               + jnp.sum(out_hi * out_hi, axis=-1, keepdims=True))
        inv = lax.rsqrt(ssq / (2 * half) + EPS)
        out_lo = (out_lo * inv) * fw_ref[:, 0:half]
        out_hi = (out_hi * inv) * fw_ref[:, half:2 * half]
    o_ref[:, 0:half] = out_lo
    o_ref[:, half:2 * half] = out_hi


def _combine(h, u, yg, wts_tk, gt, s1, s3, s2, final_w, seq, final_norm, part, n_parts):
    t, d = h.shape
    half = d // 2
    tm = min(COMBINE_ROWS, seq)
    per_b = seq // tm
    ds_ = s1.shape[-1]
    steps = t // n_parts // tm
    off = part * steps
    return pl.pallas_call(
        functools.partial(_combine_body, final_norm=final_norm),
        grid=(steps,),
        in_specs=[pl.BlockSpec((tm, d), lambda i: (i + off, 0)),
                  pl.BlockSpec((tm, half), lambda i: (i + off, 0)),
                  pl.BlockSpec((TOP_K, tm, half), lambda i: (0, i, 0)),
                  pl.BlockSpec((tm, TOP_K), lambda i: (i + off, 0)),
                  pl.BlockSpec((1, 1, d), lambda i: ((i + off) // per_b, 0, 0)),
                  pl.BlockSpec((d, ds_), lambda i: (0, 0)),
                  pl.BlockSpec((d, ds_), lambda i: (0, 0)),
                  pl.BlockSpec((ds_, d), lambda i: (0, 0)),
                  pl.BlockSpec((1, d), lambda i: (0, 0))],
        out_specs=pl.BlockSpec((tm, d), lambda i: (i + off, 0)),
        out_shape=jax.ShapeDtypeStruct((t, d), F32),
        input_output_aliases={0: 0},
        compiler_params=_cparams(("parallel",)),
        name="moe_combine",
    )(h, u, yg, wts_tk, gt, s1, s3, s2, final_w)


def _split_w_in(w_in):
    sizes = (2 * M_HEADS * M_DQK, M_HEADS * M_DV, M_HEADS * M_DV, M_HEADS, M_HEADS,
             G_HEADS * G_DK, G_HEADS * G_DK, G_HEADS * G_DV, G_RANK, G_HEADS * G_DV,
             D_MODEL, D_MODEL)
    offs = [0]
    for n in sizes:
        offs.append(offs[-1] + n)
    w16 = w_in.astype(BF16)
    big = jnp.concatenate([w16[:, offs[0]:offs[3]], w16[:, offs[5]:offs[8]], w16[:, offs[9]:offs[12]]],
                          axis=1)
    pad = jnp.zeros((w_in.shape[0], SMALL_COLS - 2 * M_HEADS - G_RANK), BF16)
    small = jnp.concatenate([w16[:, offs[3]:offs[5]], w16[:, offs[8]:offs[9]], pad], axis=1)
    return big, small


def _prepare_mixer_weights(w_in, w_pa, w_pb, w_o, layer, tie=None):
    if tie is not None:
        w_in, w_pa, w_pb, w_o, _ = lax.optimization_barrier((w_in, w_pa, w_pb, w_o, tie))
    w_big, w_small = _split_w_in(w_in[layer])
    return (w_big, w_small, w_pa[layer].astype(BF16), w_pb[layer].astype(BF16),
            w_o[layer].astype(BF16))


def _moe_layout(counts, eidx, pos, n_blocks):
    padded = (counts + EXPERT_ROWS - 1) // EXPERT_ROWS * EXPERT_ROWS
    pend = jnp.cumsum(padded)
    pstart = pend - padded
    experts = jnp.arange(N_EXPERTS, dtype=I32)
    dest = pos + jnp.sum(jnp.where(eidx[..., None] == experts, pstart, 0), axis=-1)
    blk_start = jnp.arange(n_blocks, dtype=I32) * EXPERT_ROWS
    owner = jnp.sum((pend[None, :] <= blk_start[:, None]).astype(I32), axis=1)
    blk_e = jnp.minimum(owner, N_EXPERTS - 1)
    prev = jnp.concatenate([jnp.full((1,), -1, I32), blk_e[:-1]])
    blk_first = (blk_e != prev).astype(I32)
    own = blk_e[:, None] == experts[None, :]
    rows_left = jnp.sum(jnp.where(own, (pstart + counts)[None, :], 0), axis=1) - blk_start
    blk_valid = jnp.clip(jnp.where(owner < N_EXPERTS, rows_left, 0), 0, EXPERT_ROWS)
    return dest.astype(I32), blk_e.astype(I32), blk_first, blk_valid.astype(I32)


def kernel(x, c, ada_w, ada_b, norm_mix_w, norm_moe_w, w_in, m_conv_w, m_gate_b, m_norm_w,
           g_alpha_w, g_alpha_b, g_norm_w, w_pa, w_pb, w_o, router_w, router_b,
           exp_w1, exp_w3, exp_w2, sh_w1, sh_w3, sh_w2, final_norm_w):
    bsz, seq, d = x.shape
    depth = ada_w.shape[0]
    t = bsz * seq
    n_rows = t * TOP_K + N_EXPERTS * EXPERT_ROWS
    n_blocks = n_rows // EXPERT_ROWS

    ada = _ada(c, ada_w, ada_b).reshape(depth, bsz, 6, 1, d)
    h = x.reshape(t, d)
    prepared = _prepare_mixer_weights(w_in, w_pa, w_pb, w_o, 0)
    for l in range(depth):
        sh1, sc1, gt1, sh2, sc2, gt2 = (ada[l, :, i] for i in range(6))

        w_big, w_small, w_pa_l, w_pb_l, w_o_l = prepared
        mix_w = norm_mix_w[l][None, :]
        gate_row = jnp.zeros((1, SMALL_COLS), F32)
        gate_row = gate_row.at[0, SMALL_I:SMALL_I + M_HEADS].set(m_gate_b[l, 0])
        gate_row = gate_row.at[0, SMALL_F:SMALL_F + M_HEADS].set(m_gate_b[l, 1])
        ha = _mlstm(h, sh1, sc1, mix_w, w_big, w_small, m_conv_w[l], gate_row, m_norm_w[l][None, :],
                    bsz, seq)
        alpha_full = jnp.zeros((SMALL_COLS, G_HEADS * G_DK), F32)
        alpha_full = alpha_full.at[SMALL_R:SMALL_R + G_RANK].set(g_alpha_w[l]).astype(BF16)
        hb = _gla(h, sh1, sc1, mix_w, w_big, w_small, alpha_full, g_alpha_b[l][None, :],
                  g_norm_w[l][None, :], bsz, seq)
        h, u, logits_t = _mixout(h, sh1, sc1, mix_w, w_big, ha, hb, gt1, w_pa_l, w_pb_l, w_o_l,
                                 sh2, sc2, norm_moe_w[l][None, :], router_w[l].T, seq)

        eidx, wts, pos, cnt = _route(logits_t, router_b[l][:, None])
        dest, blk_e, blk_first, blk_valid = _moe_layout(cnt[:, 0], eidx, pos, n_blocks)
        xg = _sc_scatter_rows(u, dest, n_rows)
        if l + 1 < depth:
            prepared = _prepare_mixer_weights(w_in, w_pa, w_pb, w_o, l + 1, tie=dest)
        y = _experts(xg, blk_e, blk_first, blk_valid, exp_w1, exp_w3, exp_w2, l)
        tp = t // COMBINE_PARTS
        for p in range(COMBINE_PARTS):
            dest_p = dest[:, p * tp:(p + 1) * tp].reshape(-1)
            yg = _sc_gather_rows(y, dest_p).reshape(TOP_K, tp, d // 2)
            h = _combine(h, u, yg, wts.T, gt2, sh_w1[l].astype(BF16), sh_w3[l].astype(BF16),
                         sh_w2[l].astype(BF16), final_norm_w[None, :], seq,
                         final_norm=(l == depth - 1), part=p, n_parts=COMBINE_PARTS)

    return h.reshape(bsz, seq, d)
```

```python
import functools

import jax
import jax.numpy as jnp
import numpy as np
from jax import lax
from jax.experimental import pallas as pl
from jax.experimental.pallas import tpu as pltpu
from jax.experimental.pallas import tpu_sc as plsc

F32 = jnp.float32
BF16 = jnp.bfloat16
I32 = jnp.int32
U32 = jnp.uint32
HI_MASK = np.uint32(0xFFFF0000)

LANES = 128

SC_CORES = 2
SC_SUBCORES = 16
SC_WORKERS = SC_CORES * SC_SUBCORES
SC_CHUNK = 64

D_MODEL = 1024
M_HEADS = 4
M_DQK = 128
M_DV = 256
M_CONV = 4
GATE_CAP = 15.0
G_HEADS = 4
G_DK = 128
G_DV = 256
G_RANK = 16
G_TAU = 16.0
G_CHUNK = 64
N_EXPERTS = 64
TOP_K = 8
N_GROUPS = 8
GROUP_SIZE = N_EXPERTS // N_GROUPS
TOPK_GROUPS = 4
ROUTED_SCALE = 2.5
EPS = 1e-6

M_CHUNK_ROWS = 256
G_BLOCK_ROWS = 256
SEQ_GROUP = 2
PROJ_PIECE_COLS = 256
MIX_ROWS = 512
ROUTE_ROWS = 1024
EXPERT_ROWS = 1024
COMBINE_ROWS = 512
COMBINE_PARTS = 8
CONV_HALO = 8
VMEM_LIMIT = 48 * 1024 * 1024

M_GROUP_COLS = 2 * M_HEADS * M_DQK + 2 * M_HEADS * M_DV
G_GROUP_COLS = 2 * G_HEADS * G_DK + 2 * G_HEADS * G_DV
GATE_GROUP_COLS = 2 * D_MODEL
COL_M_GROUP = 0
COL_G_GROUP = COL_M_GROUP + M_GROUP_COLS
COL_GATE_GROUP = COL_G_GROUP + G_GROUP_COLS
SMALL_COLS = LANES
SMALL_I, SMALL_F, SMALL_R = 0, M_HEADS, 2 * M_HEADS


def _cparams(sem, vmem=VMEM_LIMIT):
    return pltpu.CompilerParams(dimension_semantics=sem, vmem_limit_bytes=vmem)


def _silu(x):
    return x * jax.nn.sigmoid(x)


def _log_sigmoid(x):
    return jnp.minimum(x, 0.0) - jnp.log1p(jnp.exp(-jnp.abs(x)))


def _modulated_rmsnorm(x, w, sc, sh):
    y = x * lax.rsqrt(jnp.mean(x * x, axis=-1, keepdims=True) + EPS)
    return (y * w) * (1.0 + sc) + sh


def _pack_bf16_pair(lo, hi):
    lo_bits = lax.bitcast_convert_type(lo.astype(BF16).astype(F32), U32)
    hi_bits = lax.bitcast_convert_type(hi.astype(BF16).astype(F32), U32)
    return (lo_bits >> 16) | (hi_bits & HI_MASK)


def _unpack_bf16_pair(packed):
    lo = lax.bitcast_convert_type(packed << 16, F32)
    hi = lax.bitcast_convert_type(packed & HI_MASK, F32)
    return lo, hi


def _lower_tri(n, dtype):
    r = lax.broadcasted_iota(I32, (n, n), 0)
    c = lax.broadcasted_iota(I32, (n, n), 1)
    return (r >= c).astype(dtype)


def _split3_dot(lhs01, x):
    hi = x.astype(BF16)
    r1 = x - hi.astype(F32)
    mid = r1.astype(BF16)
    lo = (r1 - mid.astype(F32)).astype(BF16)
    return (jnp.dot(lhs01, hi, preferred_element_type=F32)
            + jnp.dot(lhs01, mid, preferred_element_type=F32)
            + jnp.dot(lhs01, lo, preferred_element_type=F32))


def _split2_dot_nt(a, b):
    nt = (((1,), (1,)), ((), ()))
    a_hi = a.astype(BF16)
    a_lo = (a - a_hi.astype(F32)).astype(BF16)
    b_hi = b.astype(BF16)
    b_lo = (b - b_hi.astype(F32)).astype(BF16)
    return (lax.dot_general(a_hi, b_hi, nt, preferred_element_type=F32)
            + lax.dot_general(a_hi, b_lo, nt, preferred_element_type=F32)
            + lax.dot_general(a_lo, b_hi, nt, preferred_element_type=F32))


def _ada_body(c_ref, w_ref, b_ref, o_ref):
    cond = _silu(c_ref[...])
    o_ref[0] = jnp.dot(cond.astype(BF16), w_ref[0].astype(BF16),
                       preferred_element_type=F32) + b_ref[0]


def _ada(c, ada_w, ada_b):
    depth, d, six_d = ada_w.shape
    bsz = c.shape[0]
    nj = six_d // d
    return pl.pallas_call(
        _ada_body,
        grid=(depth, nj),
        in_specs=[pl.BlockSpec((bsz, d), lambda l, j: (0, 0)),
                  pl.BlockSpec((1, d, d), lambda l, j: (l, 0, j)),
                  pl.BlockSpec((1, 1, d), lambda l, j: (l, 0, j))],
        out_specs=pl.BlockSpec((1, bsz, d), lambda l, j: (l, 0, j)),
        out_shape=jax.ShapeDtypeStruct((depth, bsz, six_d), F32),
        compiler_params=_cparams(("parallel", "parallel")),
        name="ada_ln",
    )(c, ada_w, ada_b.reshape(depth, 1, six_d))


def _mlstm_body(h0_ref, hn_ref, sh_ref, sc_ref, mw_ref, w_ref, ws_ref, cw_ref, gb_ref, nw_ref,
                ha_ref, qk_nxt, v_nxt, o_nxt, sm_nxt, xe_scr, v_scr, o_scr, sm_scr,
                c_scr, n_scr, m_scr):
    n_seq, rows = hn_ref.shape[0], hn_ref.shape[1]
    half = M_HEADS * M_DQK
    vcols = M_HEADS * M_DV

    def projection_pieces(h_ref):
        u = jnp.concatenate(
            [_modulated_rmsnorm(h_ref[g], mw_ref[...], sc_ref[g], sh_ref[g]) for g in range(n_seq)],
            axis=0).astype(BF16)

        def piece(dst, w_lo, lo, width):
            def run():
                dst[:, lo:lo + width] = jnp.dot(u, w_ref[:, w_lo + lo:w_lo + lo + width],
                                                preferred_element_type=F32).astype(dst.dtype)
            return run

        def small_piece():
            sm_nxt[...] = jnp.dot(u, ws_ref[...], preferred_element_type=F32)

        pieces = [small_piece]
        for dst, w_lo in ((qk_nxt, 0), (v_nxt, 2 * half), (o_nxt, 2 * half + vcols)):
            for lo in range(0, vcols, PROJ_PIECE_COLS):
                pieces.append(piece(dst, w_lo, lo, PROJ_PIECE_COLS))
        return pieces

    @pl.when(pl.program_id(1) == 0)
    def _():
        xe_scr[:, 0:CONV_HALO, :] = jnp.zeros((n_seq, CONV_HALO, 2 * half), F32)
        c_scr[...] = jnp.zeros(c_scr.shape, F32)
        n_scr[...] = jnp.zeros(n_scr.shape, F32)
        m_scr[...] = jnp.zeros(m_scr.shape, F32)
        for run in projection_pieces(h0_ref):
            run()

    for g in range(n_seq):
        xe_scr[g, CONV_HALO:CONV_HALO + rows, :] = qk_nxt[g * rows:(g + 1) * rows, :]
    v_scr[...] = v_nxt[...]
    o_scr[...] = o_nxt[...]
    sm_scr[...] = sm_nxt[...]
    pending = projection_pieces(hn_ref)

    r_io = lax.broadcasted_iota(I32, (rows, rows), 0)
    c_io = lax.broadcasted_iota(I32, (rows, rows), 1)
    causal = r_io >= c_io
    tri = _lower_tri(rows, BF16)

    for g in range(n_seq):
        gr = slice(g * rows, (g + 1) * rows)
        conv = None
        for j in range(M_CONV):
            off = CONV_HALO - (M_CONV - 1) + j
            term = xe_scr[g, off:off + rows, :] * cw_ref[j:j + 1, :]
            conv = term if conv is None else conv + term
        qk = _silu(conv)
        xe_scr[g, 0:CONV_HALO, :] = xe_scr[g, rows:rows + CONV_HALO, :]

        capped = GATE_CAP * jnp.tanh((sm_scr[gr, :] + gb_ref[...]) / GATE_CAP)
        li_all = capped
        lf_all = _log_sigmoid(capped)
        b_all = _split3_dot(tri, lf_all)
        li_t = li_all.T
        b_t = b_all.T

        for h in range(M_HEADS):
            if pending:
                pending.pop(0)()
            sidx = g * M_HEADS + h
            q = (qk[:, h * M_DQK:(h + 1) * M_DQK] * (M_DQK ** -0.5)).astype(BF16)
            k = qk[:, half + h * M_DQK:half + (h + 1) * M_DQK]
            kb = k.astype(BF16)
            v = v_scr[gr, h * M_DV:(h + 1) * M_DV]
            li_c = li_all[:, SMALL_I + h:SMALL_I + h + 1]
            b_c = b_all[:, SMALL_F + h:SMALL_F + h + 1]
            li_r = li_t[SMALL_I + h:SMALL_I + h + 1, :]
            b_r = b_t[SMALL_F + h:SMALL_F + h + 1, :]
            gsum = b_c[rows - 1:rows, :]
            m_prev = m_scr[sidx][:, 0:1]
            c_prev = c_scr[sidx]
            n_prev = n_scr[sidx]

            d_mat = jnp.where(causal, b_c - b_r + li_r, -jnp.inf)
            m_inter = b_c + m_prev
            m_t = jnp.maximum(jnp.max(d_mat, axis=1, keepdims=True), m_inter)
            s = lax.dot_general(q, kb, (((1,), (1,)), ((), ())), preferred_element_type=F32)
            p = jnp.exp(d_mat - m_t) * s
            w_inter = jnp.exp(m_inter - m_t)
            num = (jnp.dot(p.astype(BF16), v, preferred_element_type=F32)
                   + w_inter * jnp.dot(q, c_prev.astype(BF16), preferred_element_type=F32))
            qn = jnp.sum(q.astype(F32) * n_prev, axis=1, keepdims=True)
            den = jnp.sum(p, axis=1, keepdims=True) + w_inter * qn
            hh = num / jnp.maximum(jnp.abs(den), jnp.exp(-m_t))

            if pending:
                pending.pop(0)()

            a_r = gsum - b_r + li_r
            a_c = gsum - b_c + li_c
            m_new = jnp.maximum(gsum + m_prev, jnp.max(a_r, axis=1, keepdims=True))
            decay = jnp.exp(gsum + m_prev - m_new)
            wk = jnp.exp(a_c - m_new) * k
            c_scr[sidx] = decay * c_prev + lax.dot_general(
                wk.astype(BF16), v, (((0,), (0,)), ((), ())), preferred_element_type=F32)
            n_scr[sidx] = decay * n_prev + jnp.sum(wk, axis=0, keepdims=True)
            m_scr[sidx] = jnp.broadcast_to(m_new, m_scr.shape[1:])

            y = hh * lax.rsqrt(jnp.mean(hh * hh, axis=-1, keepdims=True) + EPS)
            y = y * nw_ref[:, h * M_DV:(h + 1) * M_DV]
            gate = jax.nn.sigmoid(o_scr[gr, h * M_DV:(h + 1) * M_DV])
            ha_ref[g, :, h * M_DV:(h + 1) * M_DV] = (y * gate).astype(BF16)
    for run in pending:
        run()


def _mlstm(h, sh, sc, mix_w, w_big, w_small, conv_w, gate_row, norm_w, bsz, seq):
    t, d = h.shape
    rows = min(M_CHUNK_ROWS, seq)
    nc = seq // rows
    ng = SEQ_GROUP if bsz % SEQ_GROUP == 0 else 1
    half = M_HEADS * M_DQK
    vcols = M_HEADS * M_DV
    wcols = M_GROUP_COLS
    assert COL_M_GROUP % wcols == 0
    resident = pl.Buffered(1)
    nxt = [pltpu.VMEM((ng * rows, 2 * half), F32), pltpu.VMEM((ng * rows, vcols), BF16),
           pltpu.VMEM((ng * rows, vcols), F32), pltpu.VMEM((ng * rows, SMALL_COLS), F32)]
    out = pl.pallas_call(
        _mlstm_body,
        grid=(bsz // ng, nc),
        in_specs=[pl.BlockSpec((ng, rows, d), lambda b, c: (b, 0, 0)),
                  pl.BlockSpec((ng, rows, d), lambda b, c: (b, jnp.minimum(c + 1, nc - 1), 0)),
                  pl.BlockSpec((ng, 1, d), lambda b, c: (b, 0, 0)),
                  pl.BlockSpec((ng, 1, d), lambda b, c: (b, 0, 0)),
                  pl.BlockSpec((1, d), lambda b, c: (0, 0)),
                  pl.BlockSpec((d, wcols), lambda b, c: (0, 0), pipeline_mode=resident),
                  pl.BlockSpec((d, SMALL_COLS), lambda b, c: (0, 0), pipeline_mode=resident),
                  pl.BlockSpec((M_CONV, 2 * half), lambda b, c: (0, 0)),
                  pl.BlockSpec((1, SMALL_COLS), lambda b, c: (0, 0)),
                  pl.BlockSpec((1, vcols), lambda b, c: (0, 0))],
        out_specs=pl.BlockSpec((ng, rows, vcols), lambda b, c: (b, c, 0)),
        out_shape=jax.ShapeDtypeStruct((bsz, seq, vcols), BF16),
        scratch_shapes=nxt + [pltpu.VMEM((ng, rows + CONV_HALO, 2 * half), F32)] + nxt[1:]
        + [pltpu.VMEM((ng * M_HEADS, M_DQK, M_DV), F32),
           pltpu.VMEM((ng * M_HEADS, 1, M_DQK), F32),
           pltpu.VMEM((ng * M_HEADS, 1, LANES), F32)],
        compiler_params=_cparams(("parallel", "arbitrary")),
        name="mlstm",
    )(h.reshape(bsz, seq, d), h.reshape(bsz, seq, d), sh, sc, mix_w, w_big, w_small, conv_w,
      gate_row, norm_w)
    return out.reshape(t, vcols)


def _gla_body(h0_ref, hn_ref, sh_ref, sc_ref, mw_ref, w_ref, ws_ref, aw_ref, ab_ref, nw_ref, hb_ref,
              qk_nxt, v_nxt, z_nxt, sm_nxt, qk_scr, v_scr, z_scr, sm_scr, st_scr):
    n_seq, rows = hn_ref.shape[0], hn_ref.shape[1]
    n_chunks = rows // G_CHUNK
    kcols = G_HEADS * G_DK
    vcols = G_HEADS * G_DV

    def projection_pieces(h_ref):
        u = jnp.concatenate(
            [_modulated_rmsnorm(h_ref[g], mw_ref[...], sc_ref[g], sh_ref[g]) for g in range(n_seq)],
            axis=0).astype(BF16)

        def piece(dst, w_lo, lo, width):
            def run():
                dst[:, lo:lo + width] = jnp.dot(u, w_ref[:, w_lo + lo:w_lo + lo + width],
                                                preferred_element_type=F32).astype(dst.dtype)
            return run

        def small_piece():
            sm_nxt[...] = jnp.dot(u, ws_ref[...], preferred_element_type=F32)

        pieces = [small_piece]
        for dst, w_lo in ((qk_nxt, 0), (v_nxt, 2 * kcols), (z_nxt, 2 * kcols + vcols)):
            for lo in range(0, vcols, PROJ_PIECE_COLS):
                pieces.append(piece(dst, w_lo, lo, PROJ_PIECE_COLS))
        return pieces

    @pl.when(pl.program_id(1) == 0)
    def _():
        st_scr[...] = jnp.zeros(st_scr.shape, F32)
        for run in projection_pieces(h0_ref):
            run()

    qk_scr[...] = qk_nxt[...]
    v_scr[...] = v_nxt[...]
    z_scr[...] = z_nxt[...]
    sm_scr[...] = sm_nxt[...]
    pending = projection_pieces(hn_ref)

    r_io = lax.broadcasted_iota(I32, (rows, rows), 0)
    c_io = lax.broadcasted_iota(I32, (rows, rows), 1)
    chunk_causal = jnp.logical_and(r_io >= c_io, r_io // G_CHUNK == c_io // G_CHUNK)
    chunk_tri = jnp.where(chunk_causal, 1.0, 0.0).astype(BF16)

    for g in range(n_seq):
        gr = slice(g * rows, (g + 1) * rows)
        for _ in range((len(pending) + n_seq - 1 - g) // (n_seq - g)):
            pending.pop(0)()
        q = qk_scr[gr, 0:kcols] * (G_DK ** -0.5)
        k = qk_scr[gr, kcols:2 * kcols]

        logits = jnp.dot(sm_scr[gr, :].astype(BF16), aw_ref[...], preferred_element_type=F32)
        la = _log_sigmoid(logits + ab_ref[...]) / G_TAU
        bc = _split3_dot(chunk_tri, la)
        gcs = [bc[(ci + 1) * G_CHUNK - 1:(ci + 1) * G_CHUNK, :] for ci in range(n_chunks)]
        gc_rows = jnp.concatenate([jnp.broadcast_to(gc, (G_CHUNK, kcols)) for gc in gcs], axis=0)

        q_in = (q * jnp.exp(bc)).astype(BF16)
        k_in = (k * jnp.exp(-bc)).astype(BF16)
        k_out = (k * jnp.exp(gc_rows - bc)).astype(BF16)

        for h in range(G_HEADS):
            ks = slice(h * G_DK, (h + 1) * G_DK)
            vs = slice(h * G_DV, (h + 1) * G_DV)
            v = v_scr[gr, vs]
            att = lax.dot_general(q_in[:, ks], k_in[:, ks], (((1,), (1,)), ((), ())),
                                  preferred_element_type=F32)
            att = jnp.where(chunk_causal, att, 0.0).astype(BF16)
            o_intra = jnp.dot(att, v, preferred_element_type=F32)
            st = st_scr[g * G_HEADS + h]
            outs = []
            for ci in range(n_chunks):
                rs = slice(ci * G_CHUNK, (ci + 1) * G_CHUNK)
                o_inter = lax.dot_general(q_in[rs, ks], st.astype(BF16),
                                          (((1,), (1,)), ((), ())), preferred_element_type=F32)
                outs.append(o_intra[rs, :] + o_inter)
                st = jnp.exp(gcs[ci][:, ks]) * st + lax.dot_general(
                    v[rs, :], k_out[rs, ks], (((0,), (0,)), ((), ())), preferred_element_type=F32)
            st_scr[g * G_HEADS + h] = st
            o = jnp.concatenate(outs, axis=0)
            y = o * lax.rsqrt(jnp.mean(o * o, axis=-1, keepdims=True) + EPS)
            y = y * nw_ref[:, vs]
            hb_ref[g, :, vs] = (y * _silu(z_scr[gr, vs])).astype(BF16)


def _gla(h, sh, sc, mix_w, w_big, w_small, alpha_full, alpha_b, norm_w, bsz, seq):
    t, d = h.shape
    rows = min(G_BLOCK_ROWS, seq)
    nb = seq // rows
    ng = SEQ_GROUP if bsz % SEQ_GROUP == 0 else 1
    kcols = G_HEADS * G_DK
    vcols = G_HEADS * G_DV
    wcols = G_GROUP_COLS
    assert COL_G_GROUP % wcols == 0
    resident = pl.Buffered(1)
    nxt = [pltpu.VMEM((ng * rows, 2 * kcols), F32), pltpu.VMEM((ng * rows, vcols), BF16),
           pltpu.VMEM((ng * rows, vcols), F32), pltpu.VMEM((ng * rows, SMALL_COLS), F32)]
    out = pl.pallas_call(
        _gla_body,
        grid=(bsz // ng, nb),
        in_specs=[pl.BlockSpec((ng, rows, d), lambda b, c: (b, 0, 0)),
                  pl.BlockSpec((ng, rows, d), lambda b, c: (b, jnp.minimum(c + 1, nb - 1), 0)),
                  pl.BlockSpec((ng, 1, d), lambda b, c: (b, 0, 0)),
                  pl.BlockSpec((ng, 1, d), lambda b, c: (b, 0, 0)),
                  pl.BlockSpec((1, d), lambda b, c: (0, 0)),
                  pl.BlockSpec((d, wcols), lambda b, c: (0, COL_G_GROUP // wcols),
                               pipeline_mode=resident),
                  pl.BlockSpec((d, SMALL_COLS), lambda b, c: (0, 0), pipeline_mode=resident),
                  pl.BlockSpec((SMALL_COLS, kcols), lambda b, c: (0, 0)),
                  pl.BlockSpec((1, kcols), lambda b, c: (0, 0)),
                  pl.BlockSpec((1, vcols), lambda b, c: (0, 0))],
        out_specs=pl.BlockSpec((ng, rows, vcols), lambda b, c: (b, c, 0)),
        out_shape=jax.ShapeDtypeStruct((bsz, seq, vcols), BF16),
        scratch_shapes=nxt + nxt + [pltpu.VMEM((ng * G_HEADS, G_DV, G_DK), F32)],
        compiler_params=_cparams(("parallel", "arbitrary")),
        name="gla",
    )(h.reshape(bsz, seq, d), h.reshape(bsz, seq, d), sh, sc, mix_w, w_big, w_small, alpha_full,
      alpha_b, norm_w)
    return out.reshape(t, vcols)


def _mixout_body(h_ref, sh_ref, sc_ref, mw_ref, wg_ref, ha_ref, hb_ref, gt_ref, wpa_ref, wpb_ref,
                 wo_ref, o_ref):
    d = h_ref.shape[1]
    h = h_ref[...]
    u = _modulated_rmsnorm(h, mw_ref[...], sc_ref[0], sh_ref[0]).astype(BF16)
    ga = jnp.dot(u, wg_ref[:, 0:d], preferred_element_type=F32)
    gb = jnp.dot(u, wg_ref[:, d:2 * d], preferred_element_type=F32)
    a = jnp.dot(ha_ref[...], wpa_ref[...], preferred_element_type=F32)
    b = jnp.dot(hb_ref[...], wpb_ref[...], preferred_element_type=F32)
    y = jax.nn.sigmoid(ga) * a + jax.nn.sigmoid(gb) * b
    o_ref[...] = h + gt_ref[0] * jnp.dot(y.astype(BF16), wo_ref[...], preferred_element_type=F32)


def _mixout(h, sh, sc, mix_w, w_big, ha, hb, gt, w_pa, w_pb, w_o, seq):
    t, d = h.shape
    tm = min(MIX_ROWS, seq)
    per_b = seq // tm
    assert GATE_GROUP_COLS == 2 * d and COL_GATE_GROUP % GATE_GROUP_COLS == 0
    resident = pl.Buffered(1)
    wspec = pl.BlockSpec((d, d), lambda i: (0, 0), pipeline_mode=resident)
    bspec = pl.BlockSpec((1, 1, d), lambda i: (i // per_b, 0, 0))
    return pl.pallas_call(
        _mixout_body,
        grid=(t // tm,),
        in_specs=[pl.BlockSpec((tm, d), lambda i: (i, 0)),
                  bspec, bspec,
                  pl.BlockSpec((1, d), lambda i: (0, 0)),
                  pl.BlockSpec((d, 2 * d), lambda i: (0, COL_GATE_GROUP // GATE_GROUP_COLS),
                               pipeline_mode=resident),
                  pl.BlockSpec((tm, d), lambda i: (i, 0)),
                  pl.BlockSpec((tm, d), lambda i: (i, 0)),
                  bspec,
                  wspec, wspec, wspec],
        out_specs=pl.BlockSpec((tm, d), lambda i: (i, 0)),
        out_shape=jax.ShapeDtypeStruct((t, d), F32),
        compiler_params=_cparams(("parallel",)),
        name="mix_out",
    )(h, sh, sc, mix_w, w_big, ha, hb, gt, w_pa, w_pb, w_o)


def _route_body(h_ref, sh_ref, sc_ref, nw_ref, rwt_ref, rb_ref,
                u_ref, eidx_ref, wts_ref, pos_ref, cnt_ref, carry_scr):
    tm = h_ref.shape[0]

    @pl.when(pl.program_id(0) == 0)
    def _():
        carry_scr[...] = jnp.zeros(carry_scr.shape, F32)

    u = _modulated_rmsnorm(h_ref[...], nw_ref[...], sc_ref[0], sh_ref[0])
    half = u.shape[1] // 2
    u_ref[...] = _pack_bf16_pair(u[:, :half], u[:, half:])
    logits = _split2_dot_nt(rwt_ref[...], u)
    scores = jax.nn.sigmoid(logits)
    sel = scores + rb_ref[...]

    neg = -jnp.inf
    sub_io = lax.broadcasted_iota(I32, (GROUP_SIZE, tm), 0)
    pieces = []
    for g in range(N_GROUPS):
        blk = sel[g * GROUP_SIZE:(g + 1) * GROUP_SIZE, :]
        m1 = jnp.max(blk, axis=0, keepdims=True)
        first = jnp.min(jnp.where(blk == m1, sub_io, GROUP_SIZE), axis=0, keepdims=True)
        m2 = jnp.max(jnp.where(sub_io == first, neg, blk), axis=0, keepdims=True)
        pieces.append(jnp.broadcast_to(m1 + m2, (GROUP_SIZE, tm)))
    gscore = jnp.concatenate(pieces, axis=0)

    e_io = lax.broadcasted_iota(I32, (N_EXPERTS, tm), 0)
    grp_io = e_io // GROUP_SIZE
    gmask = jnp.zeros((N_EXPERTS, tm), jnp.bool_)
    for _ in range(TOPK_GROUPS):
        mx = jnp.max(gscore, axis=0, keepdims=True)
        gi = jnp.min(jnp.where(gscore == mx, grp_io, N_GROUPS), axis=0, keepdims=True)
        hit = grp_io == gi
        gmask = jnp.logical_or(gmask, hit)
        gscore = jnp.where(hit, neg, gscore)

    cur = jnp.where(gmask, sel, neg)
    row_io = lax.broadcasted_iota(I32, (TOP_K, tm), 0)
    eidx = jnp.zeros((TOP_K, tm), I32)
    wraw = jnp.zeros((TOP_K, tm), F32)
    chosen = jnp.zeros((N_EXPERTS, tm), jnp.bool_)
    hits = []
    for kk in range(TOP_K):
        mx = jnp.max(cur, axis=0, keepdims=True)
        ei = jnp.min(jnp.where(cur == mx, e_io, N_EXPERTS), axis=0, keepdims=True)
        hit = e_io == ei
        hits.append(hit)
        sc_k = jnp.sum(jnp.where(hit, scores, 0.0), axis=0, keepdims=True)
        eidx = jnp.where(row_io == kk, ei, eidx)
        wraw = jnp.where(row_io == kk, sc_k, wraw)
        chosen = jnp.logical_or(chosen, hit)
        cur = jnp.where(hit, neg, cur)

    wsum = jnp.sum(wraw, axis=0, keepdims=True)
    wts_ref[...] = wraw / wsum * ROUTED_SCALE
    eidx_ref[...] = eidx

    chosen_f = jnp.where(chosen, 1.0, 0.0)
    r_io = lax.broadcasted_iota(I32, (tm, tm), 0)
    c_io = lax.broadcasted_iota(I32, (tm, tm), 1)
    strict_upper = jnp.where(r_io < c_io, 1.0, 0.0).astype(BF16)
    prefix = jnp.dot(chosen_f.astype(BF16), strict_upper, preferred_element_type=F32)
    rank = prefix + carry_scr[:, 0:1]
    pos = jnp.zeros((TOP_K, tm), F32)
    for kk in range(TOP_K):
        p_k = jnp.sum(jnp.where(hits[kk], rank, 0.0), axis=0, keepdims=True)
        pos = jnp.where(row_io == kk, p_k, pos)
    pos_ref[...] = pos.astype(I32)
    total = carry_scr[...] + jnp.sum(chosen_f, axis=1, keepdims=True)
    carry_scr[...] = total
    cnt_ref[...] = total.astype(I32)


def _route(h, sh, sc, nw, rw_t, rb_col, seq):
    t, d = h.shape
    tm = min(ROUTE_ROWS, seq)
    per_b = seq // tm
    kspec = pl.BlockSpec((TOP_K, tm), lambda i: (0, i))
    return pl.pallas_call(
        _route_body,
        grid=(t // tm,),
        in_specs=[pl.BlockSpec((tm, d), lambda i: (i, 0)),
                  pl.BlockSpec((1, 1, d), lambda i: (i // per_b, 0, 0)),
                  pl.BlockSpec((1, 1, d), lambda i: (i // per_b, 0, 0)),
                  pl.BlockSpec((1, d), lambda i: (0, 0)),
                  pl.BlockSpec((N_EXPERTS, d), lambda i: (0, 0)),
                  pl.BlockSpec((N_EXPERTS, 1), lambda i: (0, 0))],
        out_specs=[pl.BlockSpec((tm, d // 2), lambda i: (i, 0)), kspec, kspec, kspec,
                   pl.BlockSpec((N_EXPERTS, LANES), lambda i: (0, 0))],
        out_shape=[jax.ShapeDtypeStruct((t, d // 2), U32),
                   jax.ShapeDtypeStruct((TOP_K, t), I32),
                   jax.ShapeDtypeStruct((TOP_K, t), F32),
                   jax.ShapeDtypeStruct((TOP_K, t), I32),
                   jax.ShapeDtypeStruct((N_EXPERTS, LANES), I32)],
        scratch_shapes=[pltpu.VMEM((N_EXPERTS, LANES), F32)],
        compiler_params=_cparams(("arbitrary",)),
        name="moe_route",
    )(h, sh, sc, nw, rw_t, rb_col)


def _sc_worker_id():
    return lax.axis_index("s") * SC_CORES + lax.axis_index("c")


def _sc_scatter_rows(x, dest, n_rows):
    t, d = x.shape
    n_k = dest.shape[0]
    assert t % (SC_WORKERS * 2 * SC_CHUNK) == 0
    per_w = t // SC_WORKERS
    n_ch = per_w // SC_CHUNK
    dest4 = dest.reshape(n_k, SC_WORKERS, n_ch, SC_CHUNK).transpose(1, 2, 0, 3)
    mesh = plsc.VectorSubcoreMesh(core_axis_name="c", subcore_axis_name="s")

    @functools.partial(
        pl.kernel, mesh=mesh,
        out_type=jax.ShapeDtypeStruct((n_rows, d), x.dtype),
        scratch_types=[pltpu.VMEM((n_ch, n_k, SC_CHUNK), I32),
                       pltpu.VMEM((SC_CHUNK, d), x.dtype),
                       pltpu.VMEM((SC_CHUNK, d), x.dtype)] + [pltpu.SemaphoreType.DMA] * 4,
        name="moe_dispatch_sc",
    )
    def scatter_kernel(x_hbm, dest_hbm, out_hbm, idx_v, rows0, rows1, l0, l1, s0, s1):
        wid = _sc_worker_id()
        base = wid * per_w
        pltpu.sync_copy(dest_hbm.at[wid], idx_v)
        bufs = ((rows0, l0, s0), (rows1, l1, s1))

        def load(ci, b):
            rows, load_sem, _ = bufs[b]
            return pltpu.make_async_copy(x_hbm.at[pl.ds(base + ci * SC_CHUNK, SC_CHUNK)], rows,
                                         load_sem)

        def scatters(ci, b):
            rows, _, scatter_sem = bufs[b]
            return [pltpu.make_async_copy(rows, out_hbm.at[idx_v.at[ci, j]], scatter_sem)
                    for j in range(n_k)]

        load(0, 0).start()

        @pl.loop(0, n_ch, step=2)
        def _(ci):
            load(ci, 0).wait()
            for cp in scatters(ci, 0):
                cp.start()

            @pl.when(ci > 0)
            def _():
                for cp in scatters(ci - 1, 1):
                    cp.wait()

            load(ci + 1, 1).start()
            load(ci + 1, 1).wait()
            for cp in scatters(ci + 1, 1):
                cp.start()
            for cp in scatters(ci, 0):
                cp.wait()

            @pl.when(ci + 2 < n_ch)
            def _():
                load(ci + 2, 0).start()

        for cp in scatters(n_ch - 1, 1):
            cp.wait()

    return scatter_kernel(x, dest4)


def _sc_gather_rows(table, idx):
    n, d = idx.shape[0], table.shape[1]
    assert n % (SC_WORKERS * 2 * SC_CHUNK) == 0
    per_w = n // SC_WORKERS
    n_ch = per_w // SC_CHUNK
    idx3 = idx.reshape(SC_WORKERS, n_ch, SC_CHUNK)
    mesh = plsc.VectorSubcoreMesh(core_axis_name="c", subcore_axis_name="s")

    @functools.partial(
        pl.kernel, mesh=mesh,
        out_type=jax.ShapeDtypeStruct((n, d), table.dtype),
        scratch_types=[pltpu.VMEM((n_ch, SC_CHUNK), I32),
                       pltpu.VMEM((SC_CHUNK, d), table.dtype),
                       pltpu.VMEM((SC_CHUNK, d), table.dtype)] + [pltpu.SemaphoreType.DMA] * 4,
        name="moe_combine_sc",
    )
    def gather_kernel(table_hbm, idx_hbm, out_hbm, idx_v, rows0, rows1, g0, g1, w0, w1):
        wid = _sc_worker_id()
        base = wid * per_w
        pltpu.sync_copy(idx_hbm.at[wid], idx_v)
        bufs = ((rows0, g0, w0), (rows1, g1, w1))

        def gather(ci, b):
            rows, gather_sem, _ = bufs[b]
            return pltpu.make_async_copy(table_hbm.at[idx_v.at[ci]], rows, gather_sem)

        def write(ci, b):
            rows, _, write_sem = bufs[b]
            return pltpu.make_async_copy(rows, out_hbm.at[pl.ds(base + ci * SC_CHUNK, SC_CHUNK)],
                                         write_sem)

        gather(0, 0).start()

        @pl.loop(0, n_ch, step=2)
        def _(ci):
            gather(ci, 0).wait()
            write(ci, 0).start()

            @pl.when(ci > 0)
            def _():
                write(ci - 1, 1).wait()

            gather(ci + 1, 1).start()
            gather(ci + 1, 1).wait()
            write(ci + 1, 1).start()
            write(ci, 0).wait()

            @pl.when(ci + 2 < n_ch)
            def _():
                gather(ci + 2, 0).start()

        write(n_ch - 1, 1).wait()

    return gather_kernel(table, idx3)


def _expert_body(blk_e_ref, blk_first_ref, blk_valid_ref, x_ref, w1_ref, w3_ref, w2_ref, y_ref,
                 w1_scr, w3_scr, w2_scr):
    del blk_e_ref
    j = pl.program_id(0)
    valid = blk_valid_ref[j]

    @pl.when(blk_first_ref[j] == 1)
    def _():
        w1_scr[...] = w1_ref[0].astype(BF16)
        w3_scr[...] = w3_ref[0].astype(BF16)
        w2_scr[...] = w2_ref[0].astype(BF16)

    @pl.when(valid > 0)
    def _():
        half = x_ref.shape[1]
        rows = lax.broadcasted_iota(I32, x_ref.shape, 0)
        lo, hi = _unpack_bf16_pair(jnp.where(rows < valid, x_ref[...], 0))
        lo, hi = lo.astype(BF16), hi.astype(BF16)

        def proj(w_scr):
            return (jnp.dot(lo, w_scr[0:half, :], preferred_element_type=F32)
                    + jnp.dot(hi, w_scr[half:2 * half, :], preferred_element_type=F32))

        hid = _silu(proj(w1_scr)) * proj(w3_scr)
        y = jnp.dot(hid.astype(BF16), w2_scr[...], preferred_element_type=F32)
        y_ref[...] = _pack_bf16_pair(y[:, :half], y[:, half:])

    @pl.when(valid == 0)
    def _():
        y_ref[...] = jnp.zeros(y_ref.shape, U32)


def _experts(xg, blk_e, blk_first, blk_valid, w1, w3, w2, layer):
    n_rows, half = xg.shape
    n_blocks = n_rows // EXPERT_ROWS
    d, de = w1.shape[-2:]
    grid_spec = pltpu.PrefetchScalarGridSpec(
        num_scalar_prefetch=3,
        grid=(n_blocks,),
        in_specs=[pl.BlockSpec((EXPERT_ROWS, half), lambda j, be, bf, bv: (j, 0)),
                  pl.BlockSpec((None, 1, d, de), lambda j, be, bf, bv: (layer, be[j], 0, 0)),
                  pl.BlockSpec((None, 1, d, de), lambda j, be, bf, bv: (layer, be[j], 0, 0)),
                  pl.BlockSpec((None, 1, de, d), lambda j, be, bf, bv: (layer, be[j], 0, 0))],
        out_specs=pl.BlockSpec((EXPERT_ROWS, half), lambda j, be, bf, bv: (j, 0)),
        scratch_shapes=[pltpu.VMEM((d, de), BF16), pltpu.VMEM((d, de), BF16),
                        pltpu.VMEM((de, d), BF16)],
    )
    return pl.pallas_call(
        _expert_body,
        grid_spec=grid_spec,
        out_shape=jax.ShapeDtypeStruct((n_rows, half), U32),
        compiler_params=_cparams(("arbitrary",)),
        name="moe_experts",
    )(blk_e, blk_first, blk_valid, xg, w1, w3, w2)


def _combine_body(h_ref, u_ref, yg_ref, wts_ref, gt_ref, s1_ref, s3_ref, s2_ref, fw_ref, o_ref, *,
                  final_norm):
    half = u_ref.shape[1]
    lo, hi = _unpack_bf16_pair(u_ref[...])
    lo, hi = lo.astype(BF16), hi.astype(BF16)

    def proj(w_ref):
        return (jnp.dot(lo, w_ref[0:half, :], preferred_element_type=F32)
                + jnp.dot(hi, w_ref[half:2 * half, :], preferred_element_type=F32))

    hid = _silu(proj(s1_ref)) * proj(s3_ref)
    shared = jnp.dot(hid.astype(BF16), s2_ref[...], preferred_element_type=F32)

    routed_lo = routed_hi = None
    for kk in range(TOP_K):
        y_lo, y_hi = _unpack_bf16_pair(yg_ref[kk])
        w = wts_ref[:, kk:kk + 1]
        routed_lo = y_lo * w if routed_lo is None else routed_lo + y_lo * w
        routed_hi = y_hi * w if routed_hi is None else routed_hi + y_hi * w
    gt = gt_ref[0]
    out_lo = h_ref[:, 0:half] + gt[:, 0:half] * (routed_lo + shared[:, 0:half])
    out_hi = (h_ref[:, half:2 * half]
              + gt[:, half:2 * half] * (routed_hi + shared[:, half:2 * half]))
    if final_norm:
        ssq = (jnp.sum(out_lo * out_lo, axis=-1, keepdims=True)
               + jnp.sum(out_hi * out_hi, axis=-1, keepdims=True))
        inv = lax.rsqrt(ssq / (2 * half) + EPS)
        out_lo = (out_lo * inv) * fw_ref[:, 0:half]
        out_hi = (out_hi * inv) * fw_ref[:, half:2 * half]
    o_ref[:, 0:half] = out_lo
    o_ref[:, half:2 * half] = out_hi


def _combine(h, u, yg, wts_tk, gt, s1, s3, s2, final_w, seq, final_norm, part, n_parts):
    t, d = h.shape
    half = d // 2
    tm = min(COMBINE_ROWS, seq)
    per_b = seq // tm
    ds_ = s1.shape[-1]
    steps = t // n_parts // tm
    off = part * steps
    return pl.pallas_call(
        functools.partial(_combine_body, final_norm=final_norm),
        grid=(steps,),
        in_specs=[pl.BlockSpec((tm, d), lambda i: (i + off, 0)),
                  pl.BlockSpec((tm, half), lambda i: (i + off, 0)),
                  pl.BlockSpec((TOP_K, tm, half), lambda i: (0, i, 0)),
                  pl.BlockSpec((tm, TOP_K), lambda i: (i + off, 0)),
                  pl.BlockSpec((1, 1, d), lambda i: ((i + off) // per_b, 0, 0)),
                  pl.BlockSpec((d, ds_), lambda i: (0, 0)),
                  pl.BlockSpec((d, ds_), lambda i: (0, 0)),
                  pl.BlockSpec((ds_, d), lambda i: (0, 0)),
                  pl.BlockSpec((1, d), lambda i: (0, 0))],
        out_specs=pl.BlockSpec((tm, d), lambda i: (i + off, 0)),
        out_shape=jax.ShapeDtypeStruct((t, d), F32),
        input_output_aliases={0: 0},
        compiler_params=_cparams(("parallel",)),
        name="moe_combine",
    )(h, u, yg, wts_tk, gt, s1, s3, s2, final_w)


def _split_w_in(w_in):
    sizes = (2 * M_HEADS * M_DQK, M_HEADS * M_DV, M_HEADS * M_DV, M_HEADS, M_HEADS,
             G_HEADS * G_DK, G_HEADS * G_DK, G_HEADS * G_DV, G_RANK, G_HEADS * G_DV,
             D_MODEL, D_MODEL)
    offs = [0]
    for n in sizes:
        offs.append(offs[-1] + n)
    w16 = w_in.astype(BF16)
    big = jnp.concatenate([w16[:, offs[0]:offs[3]], w16[:, offs[5]:offs[8]], w16[:, offs[9]:offs[12]]],
                          axis=1)
    pad = jnp.zeros((w_in.shape[0], SMALL_COLS - 2 * M_HEADS - G_RANK), BF16)
    small = jnp.concatenate([w16[:, offs[3]:offs[5]], w16[:, offs[8]:offs[9]], pad], axis=1)
    return big, small


def _prepare_mixer_weights(w_in, w_pa, w_pb, w_o, layer, tie=None):
    if tie is not None:
        w_in, w_pa, w_pb, w_o, _ = lax.optimization_barrier((w_in, w_pa, w_pb, w_o, tie))
    w_big, w_small = _split_w_in(w_in[layer])
    return (w_big, w_small, w_pa[layer].astype(BF16), w_pb[layer].astype(BF16),
            w_o[layer].astype(BF16))


def _moe_layout(counts, eidx, pos, n_blocks):
    padded = (counts + EXPERT_ROWS - 1) // EXPERT_ROWS * EXPERT_ROWS
    pend = jnp.cumsum(padded)
    pstart = pend - padded
    experts = jnp.arange(N_EXPERTS, dtype=I32)
    dest = pos + jnp.sum(jnp.where(eidx[..., None] == experts, pstart, 0), axis=-1)
    blk_start = jnp.arange(n_blocks, dtype=I32) * EXPERT_ROWS
    owner = jnp.sum((pend[None, :] <= blk_start[:, None]).astype(I32), axis=1)
    blk_e = jnp.minimum(owner, N_EXPERTS - 1)
    prev = jnp.concatenate([jnp.full((1,), -1, I32), blk_e[:-1]])
    blk_first = (blk_e != prev).astype(I32)
    own = blk_e[:, None] == experts[None, :]
    rows_left = jnp.sum(jnp.where(own, (pstart + counts)[None, :], 0), axis=1) - blk_start
    blk_valid = jnp.clip(jnp.where(owner < N_EXPERTS, rows_left, 0), 0, EXPERT_ROWS)
    return dest.astype(I32), blk_e.astype(I32), blk_first, blk_valid.astype(I32)


def kernel(x, c, ada_w, ada_b, norm_mix_w, norm_moe_w, w_in, m_conv_w, m_gate_b, m_norm_w,
           g_alpha_w, g_alpha_b, g_norm_w, w_pa, w_pb, w_o, router_w, router_b,
           exp_w1, exp_w3, exp_w2, sh_w1, sh_w3, sh_w2, final_norm_w):
    bsz, seq, d = x.shape
    depth = ada_w.shape[0]
    t = bsz * seq
    n_rows = t * TOP_K + N_EXPERTS * EXPERT_ROWS
    n_blocks = n_rows // EXPERT_ROWS

    ada = _ada(c, ada_w, ada_b).reshape(depth, bsz, 6, 1, d)
    h = x.reshape(t, d)
    prepared = _prepare_mixer_weights(w_in, w_pa, w_pb, w_o, 0)
    for l in range(depth):
        sh1, sc1, gt1, sh2, sc2, gt2 = (ada[l, :, i] for i in range(6))

        w_big, w_small, w_pa_l, w_pb_l, w_o_l = prepared
        mix_w = norm_mix_w[l][None, :]
        gate_row = jnp.zeros((1, SMALL_COLS), F32)
        gate_row = gate_row.at[0, SMALL_I:SMALL_I + M_HEADS].set(m_gate_b[l, 0])
        gate_row = gate_row.at[0, SMALL_F:SMALL_F + M_HEADS].set(m_gate_b[l, 1])
        ha = _mlstm(h, sh1, sc1, mix_w, w_big, w_small, m_conv_w[l], gate_row, m_norm_w[l][None, :],
                    bsz, seq)
        alpha_full = jnp.zeros((SMALL_COLS, G_HEADS * G_DK), F32)
        alpha_full = alpha_full.at[SMALL_R:SMALL_R + G_RANK].set(g_alpha_w[l]).astype(BF16)
        hb = _gla(h, sh1, sc1, mix_w, w_big, w_small, alpha_full, g_alpha_b[l][None, :],
                  g_norm_w[l][None, :], bsz, seq)
        h = _mixout(h, sh1, sc1, mix_w, w_big, ha, hb, gt1, w_pa_l, w_pb_l, w_o_l, seq)

        u, eidx, wts, pos, cnt = _route(h, sh2, sc2, norm_moe_w[l][None, :],
                                        router_w[l].T, router_b[l][:, None], seq)
        dest, blk_e, blk_first, blk_valid = _moe_layout(cnt[:, 0], eidx, pos, n_blocks)
        xg = _sc_scatter_rows(u, dest, n_rows)
        if l + 1 < depth:
            prepared = _prepare_mixer_weights(w_in, w_pa, w_pb, w_o, l + 1, tie=dest)
        y = _experts(xg, blk_e, blk_first, blk_valid, exp_w1, exp_w3, exp_w2, l)
        tp = t // COMBINE_PARTS
        for p in range(COMBINE_PARTS):
            dest_p = dest[:, p * tp:(p + 1) * tp].reshape(-1)
            yg = _sc_gather_rows(y, dest_p).reshape(TOP_K, tp, d // 2)
            h = _combine(h, u, yg, wts.T, gt2, sh_w1[l].astype(BF16), sh_w3[l].astype(BF16),
                         sh_w2[l].astype(BF16), final_norm_w[None, :], seq,
                         final_norm=(l == depth - 1), part=p, n_parts=COMBINE_PARTS)

    return h.reshape(bsz, seq, d)
```

```python
import functools

import jax
import jax.numpy as jnp
import numpy as np
from jax import lax
from jax.experimental import pallas as pl
from jax.experimental.pallas import tpu as pltpu
from jax.experimental.pallas import tpu_sc as plsc

F32 = jnp.float32
BF16 = jnp.bfloat16
I32 = jnp.int32
U32 = jnp.uint32
HI_MASK = np.uint32(0xFFFF0000)

LANES = 128

SC_CORES = 2
SC_SUBCORES = 16
SC_WORKERS = SC_CORES * SC_SUBCORES
SC_CHUNK = 64

D_MODEL = 1024
M_HEADS = 4
M_DQK = 128
M_DV = 256
M_CONV = 4
GATE_CAP = 15.0
G_HEADS = 4
G_DK = 128
G_DV = 256
G_RANK = 16
G_TAU = 16.0
G_CHUNK = 64
N_EXPERTS = 64
TOP_K = 8
N_GROUPS = 8
GROUP_SIZE = N_EXPERTS // N_GROUPS
TOPK_GROUPS = 4
ROUTED_SCALE = 2.5
EPS = 1e-6

M_CHUNK_ROWS = 256
G_BLOCK_ROWS = 256
SEQ_GROUP = 2
PROJ_PIECE_COLS = 256
MIX_ROWS = 512
ROUTE_ROWS = 1024
EXPERT_ROWS = 1024
COMBINE_ROWS = 512
COMBINE_PARTS = 2
CONV_HALO = 8
VMEM_LIMIT = 48 * 1024 * 1024

M_GROUP_COLS = 2 * M_HEADS * M_DQK + 2 * M_HEADS * M_DV
G_GROUP_COLS = 2 * G_HEADS * G_DK + 2 * G_HEADS * G_DV
GATE_GROUP_COLS = 2 * D_MODEL
COL_M_GROUP = 0
COL_G_GROUP = COL_M_GROUP + M_GROUP_COLS
COL_GATE_GROUP = COL_G_GROUP + G_GROUP_COLS
SMALL_COLS = LANES
SMALL_I, SMALL_F, SMALL_R = 0, M_HEADS, 2 * M_HEADS


def _cparams(sem, vmem=VMEM_LIMIT):
    return pltpu.CompilerParams(dimension_semantics=sem, vmem_limit_bytes=vmem)


def _silu(x):
    return x * jax.nn.sigmoid(x)


def _log_sigmoid(x):
    return jnp.minimum(x, 0.0) - jnp.log1p(jnp.exp(-jnp.abs(x)))


def _modulated_rmsnorm(x, w, sc, sh):
    y = x * lax.rsqrt(jnp.mean(x * x, axis=-1, keepdims=True) + EPS)
    return (y * w) * (1.0 + sc) + sh


def _pack_bf16_pair(lo, hi):
    lo_bits = lax.bitcast_convert_type(lo.astype(BF16).astype(F32), U32)
    hi_bits = lax.bitcast_convert_type(hi.astype(BF16).astype(F32), U32)
    return (lo_bits >> 16) | (hi_bits & HI_MASK)


def _unpack_bf16_pair(packed):
    lo = lax.bitcast_convert_type(packed << 16, F32)
    hi = lax.bitcast_convert_type(packed & HI_MASK, F32)
    return lo, hi


def _lower_tri(n, dtype):
    r = lax.broadcasted_iota(I32, (n, n), 0)
    c = lax.broadcasted_iota(I32, (n, n), 1)
    return (r >= c).astype(dtype)


def _split3_dot(lhs01, x):
    hi = x.astype(BF16)
    r1 = x - hi.astype(F32)
    mid = r1.astype(BF16)
    lo = (r1 - mid.astype(F32)).astype(BF16)
    return (jnp.dot(lhs01, hi, preferred_element_type=F32)
            + jnp.dot(lhs01, mid, preferred_element_type=F32)
            + jnp.dot(lhs01, lo, preferred_element_type=F32))


def _split2_dot_nt(a, b):
    nt = (((1,), (1,)), ((), ()))
    a_hi = a.astype(BF16)
    a_lo = (a - a_hi.astype(F32)).astype(BF16)
    b_hi = b.astype(BF16)
    b_lo = (b - b_hi.astype(F32)).astype(BF16)
    return (lax.dot_general(a_hi, b_hi, nt, preferred_element_type=F32)
            + lax.dot_general(a_hi, b_lo, nt, preferred_element_type=F32)
            + lax.dot_general(a_lo, b_hi, nt, preferred_element_type=F32))


def _ada_body(c_ref, w_ref, b_ref, o_ref):
    cond = _silu(c_ref[...])
    o_ref[0] = jnp.dot(cond.astype(BF16), w_ref[0].astype(BF16),
                       preferred_element_type=F32) + b_ref[0]


def _ada(c, ada_w, ada_b):
    depth, d, six_d = ada_w.shape
    bsz = c.shape[0]
    nj = six_d // d
    return pl.pallas_call(
        _ada_body,
        grid=(depth, nj),
        in_specs=[pl.BlockSpec((bsz, d), lambda l, j: (0, 0)),
                  pl.BlockSpec((1, d, d), lambda l, j: (l, 0, j)),
                  pl.BlockSpec((1, 1, d), lambda l, j: (l, 0, j))],
        out_specs=pl.BlockSpec((1, bsz, d), lambda l, j: (l, 0, j)),
        out_shape=jax.ShapeDtypeStruct((depth, bsz, six_d), F32),
        compiler_params=_cparams(("parallel", "parallel")),
        name="ada_ln",
    )(c, ada_w, ada_b.reshape(depth, 1, six_d))


def _mlstm_body(h0_ref, hn_ref, sh_ref, sc_ref, mw_ref, w_ref, ws_ref, cw_ref, gb_ref, nw_ref,
                ha_ref, qk_nxt, v_nxt, o_nxt, sm_nxt, xe_scr, v_scr, o_scr, sm_scr,
                c_scr, n_scr, m_scr):
    n_seq, rows = hn_ref.shape[0], hn_ref.shape[1]
    half = M_HEADS * M_DQK
    vcols = M_HEADS * M_DV

    def projection_pieces(h_ref):
        u = jnp.concatenate(
            [_modulated_rmsnorm(h_ref[g], mw_ref[...], sc_ref[g], sh_ref[g]) for g in range(n_seq)],
            axis=0).astype(BF16)

        def piece(dst, w_lo, lo, width):
            def run():
                dst[:, lo:lo + width] = jnp.dot(u, w_ref[:, w_lo + lo:w_lo + lo + width],
                                                preferred_element_type=F32).astype(dst.dtype)
            return run

        def small_piece():
            sm_nxt[...] = jnp.dot(u, ws_ref[...], preferred_element_type=F32)

        pieces = [small_piece]
        for dst, w_lo in ((qk_nxt, 0), (v_nxt, 2 * half), (o_nxt, 2 * half + vcols)):
            for lo in range(0, vcols, PROJ_PIECE_COLS):
                pieces.append(piece(dst, w_lo, lo, PROJ_PIECE_COLS))
        return pieces

    @pl.when(pl.program_id(1) == 0)
    def _():
        xe_scr[:, 0:CONV_HALO, :] = jnp.zeros((n_seq, CONV_HALO, 2 * half), F32)
        c_scr[...] = jnp.zeros(c_scr.shape, F32)
        n_scr[...] = jnp.zeros(n_scr.shape, F32)
        m_scr[...] = jnp.zeros(m_scr.shape, F32)
        for run in projection_pieces(h0_ref):
            run()

    for g in range(n_seq):
        xe_scr[g, CONV_HALO:CONV_HALO + rows, :] = qk_nxt[g * rows:(g + 1) * rows, :]
    v_scr[...] = v_nxt[...]
    o_scr[...] = o_nxt[...]
    sm_scr[...] = sm_nxt[...]
    pending = projection_pieces(hn_ref)

    r_io = lax.broadcasted_iota(I32, (rows, rows), 0)
    c_io = lax.broadcasted_iota(I32, (rows, rows), 1)
    causal = r_io >= c_io
    tri = _lower_tri(rows, BF16)

    for g in range(n_seq):
        gr = slice(g * rows, (g + 1) * rows)
        conv = None
        for j in range(M_CONV):
            off = CONV_HALO - (M_CONV - 1) + j
            term = xe_scr[g, off:off + rows, :] * cw_ref[j:j + 1, :]
            conv = term if conv is None else conv + term
        qk = _silu(conv)
        xe_scr[g, 0:CONV_HALO, :] = xe_scr[g, rows:rows + CONV_HALO, :]

        capped = GATE_CAP * jnp.tanh((sm_scr[gr, :] + gb_ref[...]) / GATE_CAP)
        li_all = capped
        lf_all = _log_sigmoid(capped)
        b_all = _split3_dot(tri, lf_all)
        li_t = li_all.T
        b_t = b_all.T

        for h in range(M_HEADS):
            if pending:
                pending.pop(0)()
            sidx = g * M_HEADS + h
            q = (qk[:, h * M_DQK:(h + 1) * M_DQK] * (M_DQK ** -0.5)).astype(BF16)
            k = qk[:, half + h * M_DQK:half + (h + 1) * M_DQK]
            kb = k.astype(BF16)
            v = v_scr[gr, h * M_DV:(h + 1) * M_DV]
            li_c = li_all[:, SMALL_I + h:SMALL_I + h + 1]
            b_c = b_all[:, SMALL_F + h:SMALL_F + h + 1]
            li_r = li_t[SMALL_I + h:SMALL_I + h + 1, :]
            b_r = b_t[SMALL_F + h:SMALL_F + h + 1, :]
            gsum = b_c[rows - 1:rows, :]
            m_prev = m_scr[sidx][:, 0:1]
            c_prev = c_scr[sidx]
            n_prev = n_scr[sidx]

            d_mat = jnp.where(causal, b_c - b_r + li_r, -jnp.inf)
            m_inter = b_c + m_prev
            m_t = jnp.maximum(jnp.max(d_mat, axis=1, keepdims=True), m_inter)
            s = lax.dot_general(q, kb, (((1,), (1,)), ((), ())), preferred_element_type=F32)
            p = jnp.exp(d_mat - m_t) * s
            w_inter = jnp.exp(m_inter - m_t)
            num = (jnp.dot(p.astype(BF16), v, preferred_element_type=F32)
                   + w_inter * jnp.dot(q, c_prev.astype(BF16), preferred_element_type=F32))
            qn = jnp.sum(q.astype(F32) * n_prev, axis=1, keepdims=True)
            den = jnp.sum(p, axis=1, keepdims=True) + w_inter * qn
            hh = num / jnp.maximum(jnp.abs(den), jnp.exp(-m_t))

            if pending:
                pending.pop(0)()

            a_r = gsum - b_r + li_r
            a_c = gsum - b_c + li_c
            m_new = jnp.maximum(gsum + m_prev, jnp.max(a_r, axis=1, keepdims=True))
            decay = jnp.exp(gsum + m_prev - m_new)
            wk = jnp.exp(a_c - m_new) * k
            c_scr[sidx] = decay * c_prev + lax.dot_general(
                wk.astype(BF16), v, (((0,), (0,)), ((), ())), preferred_element_type=F32)
            n_scr[sidx] = decay * n_prev + jnp.sum(wk, axis=0, keepdims=True)
            m_scr[sidx] = jnp.broadcast_to(m_new, m_scr.shape[1:])

            y = hh * lax.rsqrt(jnp.mean(hh * hh, axis=-1, keepdims=True) + EPS)
            y = y * nw_ref[:, h * M_DV:(h + 1) * M_DV]
            gate = jax.nn.sigmoid(o_scr[gr, h * M_DV:(h + 1) * M_DV])
            ha_ref[g, :, h * M_DV:(h + 1) * M_DV] = (y * gate).astype(BF16)
    for run in pending:
        run()


def _mlstm(h, sh, sc, mix_w, w_big, w_small, conv_w, gate_row, norm_w, bsz, seq):
    t, d = h.shape
    rows = min(M_CHUNK_ROWS, seq)
    nc = seq // rows
    ng = SEQ_GROUP if bsz % SEQ_GROUP == 0 else 1
    half = M_HEADS * M_DQK
    vcols = M_HEADS * M_DV
    wcols = M_GROUP_COLS
    assert COL_M_GROUP % wcols == 0
    resident = pl.Buffered(1)
    nxt = [pltpu.VMEM((ng * rows, 2 * half), F32), pltpu.VMEM((ng * rows, vcols), BF16),
           pltpu.VMEM((ng * rows, vcols), F32), pltpu.VMEM((ng * rows, SMALL_COLS), F32)]
    out = pl.pallas_call(
        _mlstm_body,
        grid=(bsz // ng, nc),
        in_specs=[pl.BlockSpec((ng, rows, d), lambda b, c: (b, 0, 0)),
                  pl.BlockSpec((ng, rows, d), lambda b, c: (b, jnp.minimum(c + 1, nc - 1), 0)),
                  pl.BlockSpec((ng, 1, d), lambda b, c: (b, 0, 0)),
                  pl.BlockSpec((ng, 1, d), lambda b, c: (b, 0, 0)),
                  pl.BlockSpec((1, d), lambda b, c: (0, 0)),
                  pl.BlockSpec((d, wcols), lambda b, c: (0, 0), pipeline_mode=resident),
                  pl.BlockSpec((d, SMALL_COLS), lambda b, c: (0, 0), pipeline_mode=resident),
                  pl.BlockSpec((M_CONV, 2 * half), lambda b, c: (0, 0)),
                  pl.BlockSpec((1, SMALL_COLS), lambda b, c: (0, 0)),
                  pl.BlockSpec((1, vcols), lambda b, c: (0, 0))],
        out_specs=pl.BlockSpec((ng, rows, vcols), lambda b, c: (b, c, 0)),
        out_shape=jax.ShapeDtypeStruct((bsz, seq, vcols), BF16),
        scratch_shapes=nxt + [pltpu.VMEM((ng, rows + CONV_HALO, 2 * half), F32)] + nxt[1:]
        + [pltpu.VMEM((ng * M_HEADS, M_DQK, M_DV), F32),
           pltpu.VMEM((ng * M_HEADS, 1, M_DQK), F32),
           pltpu.VMEM((ng * M_HEADS, 1, LANES), F32)],
        compiler_params=_cparams(("parallel", "arbitrary")),
        name="mlstm",
    )(h.reshape(bsz, seq, d), h.reshape(bsz, seq, d), sh, sc, mix_w, w_big, w_small, conv_w,
      gate_row, norm_w)
    return out.reshape(t, vcols)


def _gla_body(h0_ref, hn_ref, sh_ref, sc_ref, mw_ref, w_ref, ws_ref, aw_ref, ab_ref, nw_ref, hb_ref,
              qk_nxt, v_nxt, z_nxt, sm_nxt, qk_scr, v_scr, z_scr, sm_scr, st_scr):
    n_seq, rows = hn_ref.shape[0], hn_ref.shape[1]
    n_chunks = rows // G_CHUNK
    kcols = G_HEADS * G_DK
    vcols = G_HEADS * G_DV

    def projection_pieces(h_ref):
        u = jnp.concatenate(
            [_modulated_rmsnorm(h_ref[g], mw_ref[...], sc_ref[g], sh_ref[g]) for g in range(n_seq)],
            axis=0).astype(BF16)

        def piece(dst, w_lo, lo, width):
            def run():
                dst[:, lo:lo + width] = jnp.dot(u, w_ref[:, w_lo + lo:w_lo + lo + width],
                                                preferred_element_type=F32).astype(dst.dtype)
            return run

        def small_piece():
            sm_nxt[...] = jnp.dot(u, ws_ref[...], preferred_element_type=F32)

        pieces = [small_piece]
        for dst, w_lo in ((qk_nxt, 0), (v_nxt, 2 * kcols), (z_nxt, 2 * kcols + vcols)):
            for lo in range(0, vcols, PROJ_PIECE_COLS):
                pieces.append(piece(dst, w_lo, lo, PROJ_PIECE_COLS))
        return pieces

    @pl.when(pl.program_id(1) == 0)
    def _():
        st_scr[...] = jnp.zeros(st_scr.shape, F32)
        for run in projection_pieces(h0_ref):
            run()

    qk_scr[...] = qk_nxt[...]
    v_scr[...] = v_nxt[...]
    z_scr[...] = z_nxt[...]
    sm_scr[...] = sm_nxt[...]
    pending = projection_pieces(hn_ref)

    r_io = lax.broadcasted_iota(I32, (rows, rows), 0)
    c_io = lax.broadcasted_iota(I32, (rows, rows), 1)
    chunk_causal = jnp.logical_and(r_io >= c_io, r_io // G_CHUNK == c_io // G_CHUNK)
    chunk_tri = jnp.where(chunk_causal, 1.0, 0.0).astype(BF16)

    for g in range(n_seq):
        gr = slice(g * rows, (g + 1) * rows)
        for _ in range((len(pending) + n_seq - 1 - g) // (n_seq - g)):
            pending.pop(0)()
        q = qk_scr[gr, 0:kcols] * (G_DK ** -0.5)
        k = qk_scr[gr, kcols:2 * kcols]

        logits = jnp.dot(sm_scr[gr, :].astype(BF16), aw_ref[...], preferred_element_type=F32)
        la = _log_sigmoid(logits + ab_ref[...]) / G_TAU
        bc = _split3_dot(chunk_tri, la)
        gcs = [bc[(ci + 1) * G_CHUNK - 1:(ci + 1) * G_CHUNK, :] for ci in range(n_chunks)]
        gc_rows = jnp.concatenate([jnp.broadcast_to(gc, (G_CHUNK, kcols)) for gc in gcs], axis=0)

        q_in = (q * jnp.exp(bc)).astype(BF16)
        k_in = (k * jnp.exp(-bc)).astype(BF16)
        k_out = (k * jnp.exp(gc_rows - bc)).astype(BF16)

        for h in range(G_HEADS):
            ks = slice(h * G_DK, (h + 1) * G_DK)
            vs = slice(h * G_DV, (h + 1) * G_DV)
            v = v_scr[gr, vs]
            att = lax.dot_general(q_in[:, ks], k_in[:, ks], (((1,), (1,)), ((), ())),
                                  preferred_element_type=F32)
            att = jnp.where(chunk_causal, att, 0.0).astype(BF16)
            o_intra = jnp.dot(att, v, preferred_element_type=F32)
            st = st_scr[g * G_HEADS + h]
            outs = []
            for ci in range(n_chunks):
                rs = slice(ci * G_CHUNK, (ci + 1) * G_CHUNK)
                o_inter = lax.dot_general(q_in[rs, ks], st.astype(BF16),
                                          (((1,), (1,)), ((), ())), preferred_element_type=F32)
                outs.append(o_intra[rs, :] + o_inter)
                st = jnp.exp(gcs[ci][:, ks]) * st + lax.dot_general(
                    v[rs, :], k_out[rs, ks], (((0,), (0,)), ((), ())), preferred_element_type=F32)
            st_scr[g * G_HEADS + h] = st
            o = jnp.concatenate(outs, axis=0)
            y = o * lax.rsqrt(jnp.mean(o * o, axis=-1, keepdims=True) + EPS)
            y = y * nw_ref[:, vs]
            hb_ref[g, :, vs] = (y * _silu(z_scr[gr, vs])).astype(BF16)


def _gla(h, sh, sc, mix_w, w_big, w_small, alpha_full, alpha_b, norm_w, bsz, seq):
    t, d = h.shape
    rows = min(G_BLOCK_ROWS, seq)
    nb = seq // rows
    ng = SEQ_GROUP if bsz % SEQ_GROUP == 0 else 1
    kcols = G_HEADS * G_DK
    vcols = G_HEADS * G_DV
    wcols = G_GROUP_COLS
    assert COL_G_GROUP % wcols == 0
    resident = pl.Buffered(1)
    nxt = [pltpu.VMEM((ng * rows, 2 * kcols), F32), pltpu.VMEM((ng * rows, vcols), BF16),
           pltpu.VMEM((ng * rows, vcols), F32), pltpu.VMEM((ng * rows, SMALL_COLS), F32)]
    out = pl.pallas_call(
        _gla_body,
        grid=(bsz // ng, nb),
        in_specs=[pl.BlockSpec((ng, rows, d), lambda b, c: (b, 0, 0)),
                  pl.BlockSpec((ng, rows, d), lambda b, c: (b, jnp.minimum(c + 1, nb - 1), 0)),
                  pl.BlockSpec((ng, 1, d), lambda b, c: (b, 0, 0)),
                  pl.BlockSpec((ng, 1, d), lambda b, c: (b, 0, 0)),
                  pl.BlockSpec((1, d), lambda b, c: (0, 0)),
                  pl.BlockSpec((d, wcols), lambda b, c: (0, COL_G_GROUP // wcols),
                               pipeline_mode=resident),
                  pl.BlockSpec((d, SMALL_COLS), lambda b, c: (0, 0), pipeline_mode=resident),
                  pl.BlockSpec((SMALL_COLS, kcols), lambda b, c: (0, 0)),
                  pl.BlockSpec((1, kcols), lambda b, c: (0, 0)),
                  pl.BlockSpec((1, vcols), lambda b, c: (0, 0))],
        out_specs=pl.BlockSpec((ng, rows, vcols), lambda b, c: (b, c, 0)),
        out_shape=jax.ShapeDtypeStruct((bsz, seq, vcols), BF16),
        scratch_shapes=nxt + nxt + [pltpu.VMEM((ng * G_HEADS, G_DV, G_DK), F32)],
        compiler_params=_cparams(("parallel", "arbitrary")),
        name="gla",
    )(h.reshape(bsz, seq, d), h.reshape(bsz, seq, d), sh, sc, mix_w, w_big, w_small, alpha_full,
      alpha_b, norm_w)
    return out.reshape(t, vcols)


def _mixout_body(h_ref, sh_ref, sc_ref, mw_ref, wg_ref, ha_ref, hb_ref, gt_ref, wpa_ref, wpb_ref,
                 wo_ref, o_ref):
    d = h_ref.shape[1]
    h = h_ref[...]
    u = _modulated_rmsnorm(h, mw_ref[...], sc_ref[0], sh_ref[0]).astype(BF16)
    ga = jnp.dot(u, wg_ref[:, 0:d], preferred_element_type=F32)
    gb = jnp.dot(u, wg_ref[:, d:2 * d], preferred_element_type=F32)
    a = jnp.dot(ha_ref[...], wpa_ref[...], preferred_element_type=F32)
    b = jnp.dot(hb_ref[...], wpb_ref[...], preferred_element_type=F32)
    y = jax.nn.sigmoid(ga) * a + jax.nn.sigmoid(gb) * b
    o_ref[...] = h + gt_ref[0] * jnp.dot(y.astype(BF16), wo_ref[...], preferred_element_type=F32)


def _mixout(h, sh, sc, mix_w, w_big, ha, hb, gt, w_pa, w_pb, w_o, seq):
    t, d = h.shape
    tm = min(MIX_ROWS, seq)
    per_b = seq // tm
    assert GATE_GROUP_COLS == 2 * d and COL_GATE_GROUP % GATE_GROUP_COLS == 0
    resident = pl.Buffered(1)
    wspec = pl.BlockSpec((d, d), lambda i: (0, 0), pipeline_mode=resident)
    bspec = pl.BlockSpec((1, 1, d), lambda i: (i // per_b, 0, 0))
    return pl.pallas_call(
        _mixout_body,
        grid=(t // tm,),
        in_specs=[pl.BlockSpec((tm, d), lambda i: (i, 0)),
                  bspec, bspec,
                  pl.BlockSpec((1, d), lambda i: (0, 0)),
                  pl.BlockSpec((d, 2 * d), lambda i: (0, COL_GATE_GROUP // GATE_GROUP_COLS),
                               pipeline_mode=resident),
                  pl.BlockSpec((tm, d), lambda i: (i, 0)),
                  pl.BlockSpec((tm, d), lambda i: (i, 0)),
                  bspec,
                  wspec, wspec, wspec],
        out_specs=pl.BlockSpec((tm, d), lambda i: (i, 0)),
        out_shape=jax.ShapeDtypeStruct((t, d), F32),
        compiler_params=_cparams(("parallel",)),
        name="mix_out",
    )(h, sh, sc, mix_w, w_big, ha, hb, gt, w_pa, w_pb, w_o)


def _route_body(h_ref, sh_ref, sc_ref, nw_ref, rwt_ref, rb_ref,
                u_ref, eidx_ref, wts_ref, pos_ref, cnt_ref, carry_scr):
    tm = h_ref.shape[0]

    @pl.when(pl.program_id(0) == 0)
    def _():
        carry_scr[...] = jnp.zeros(carry_scr.shape, F32)

    u = _modulated_rmsnorm(h_ref[...], nw_ref[...], sc_ref[0], sh_ref[0])
    half = u.shape[1] // 2
    u_ref[...] = _pack_bf16_pair(u[:, :half], u[:, half:])
    logits = _split2_dot_nt(rwt_ref[...], u)
    scores = jax.nn.sigmoid(logits)
    sel = scores + rb_ref[...]

    neg = -jnp.inf
    sub_io = lax.broadcasted_iota(I32, (GROUP_SIZE, tm), 0)
    pieces = []
    for g in range(N_GROUPS):
        blk = sel[g * GROUP_SIZE:(g + 1) * GROUP_SIZE, :]
        m1 = jnp.max(blk, axis=0, keepdims=True)
        first = jnp.min(jnp.where(blk == m1, sub_io, GROUP_SIZE), axis=0, keepdims=True)
        m2 = jnp.max(jnp.where(sub_io == first, neg, blk), axis=0, keepdims=True)
        pieces.append(jnp.broadcast_to(m1 + m2, (GROUP_SIZE, tm)))
    gscore = jnp.concatenate(pieces, axis=0)

    e_io = lax.broadcasted_iota(I32, (N_EXPERTS, tm), 0)
    grp_io = e_io // GROUP_SIZE
    gmask = jnp.zeros((N_EXPERTS, tm), jnp.bool_)
    for _ in range(TOPK_GROUPS):
        mx = jnp.max(gscore, axis=0, keepdims=True)
        gi = jnp.min(jnp.where(gscore == mx, grp_io, N_GROUPS), axis=0, keepdims=True)
        hit = grp_io == gi
        gmask = jnp.logical_or(gmask, hit)
        gscore = jnp.where(hit, neg, gscore)

    cur = jnp.where(gmask, sel, neg)
    row_io = lax.broadcasted_iota(I32, (TOP_K, tm), 0)
    eidx = jnp.zeros((TOP_K, tm), I32)
    wraw = jnp.zeros((TOP_K, tm), F32)
    chosen = jnp.zeros((N_EXPERTS, tm), jnp.bool_)
    hits = []
    for kk in range(TOP_K):
        mx = jnp.max(cur, axis=0, keepdims=True)
        ei = jnp.min(jnp.where(cur == mx, e_io, N_EXPERTS), axis=0, keepdims=True)
        hit = e_io == ei
        hits.append(hit)
        sc_k = jnp.sum(jnp.where(hit, scores, 0.0), axis=0, keepdims=True)
        eidx = jnp.where(row_io == kk, ei, eidx)
        wraw = jnp.where(row_io == kk, sc_k, wraw)
        chosen = jnp.logical_or(chosen, hit)
        cur = jnp.where(hit, neg, cur)

    wsum = jnp.sum(wraw, axis=0, keepdims=True)
    wts_ref[...] = wraw / wsum * ROUTED_SCALE
    eidx_ref[...] = eidx

    chosen_f = jnp.where(chosen, 1.0, 0.0)
    r_io = lax.broadcasted_iota(I32, (tm, tm), 0)
    c_io = lax.broadcasted_iota(I32, (tm, tm), 1)
    strict_upper = jnp.where(r_io < c_io, 1.0, 0.0).astype(BF16)
    prefix = jnp.dot(chosen_f.astype(BF16), strict_upper, preferred_element_type=F32)
    rank = prefix + carry_scr[:, 0:1]
    pos = jnp.zeros((TOP_K, tm), F32)
    for kk in range(TOP_K):
        p_k = jnp.sum(jnp.where(hits[kk], rank, 0.0), axis=0, keepdims=True)
        pos = jnp.where(row_io == kk, p_k, pos)
    pos_ref[...] = pos.astype(I32)
    total = carry_scr[...] + jnp.sum(chosen_f, axis=1, keepdims=True)
    carry_scr[...] = total
    cnt_ref[...] = total.astype(I32)


def _route(h, sh, sc, nw, rw_t, rb_col, seq):
    t, d = h.shape
    tm = min(ROUTE_ROWS, seq)
    per_b = seq // tm
    kspec = pl.BlockSpec((TOP_K, tm), lambda i: (0, i))
    return pl.pallas_call(
        _route_body,
        grid=(t // tm,),
        in_specs=[pl.BlockSpec((tm, d), lambda i: (i, 0)),
                  pl.BlockSpec((1, 1, d), lambda i: (i // per_b, 0, 0)),
                  pl.BlockSpec((1, 1, d), lambda i: (i // per_b, 0, 0)),
                  pl.BlockSpec((1, d), lambda i: (0, 0)),
                  pl.BlockSpec((N_EXPERTS, d), lambda i: (0, 0)),
                  pl.BlockSpec((N_EXPERTS, 1), lambda i: (0, 0))],
        out_specs=[pl.BlockSpec((tm, d // 2), lambda i: (i, 0)), kspec, kspec, kspec,
                   pl.BlockSpec((N_EXPERTS, LANES), lambda i: (0, 0))],
        out_shape=[jax.ShapeDtypeStruct((t, d // 2), U32),
                   jax.ShapeDtypeStruct((TOP_K, t), I32),
                   jax.ShapeDtypeStruct((TOP_K, t), F32),
                   jax.ShapeDtypeStruct((TOP_K, t), I32),
                   jax.ShapeDtypeStruct((N_EXPERTS, LANES), I32)],
        scratch_shapes=[pltpu.VMEM((N_EXPERTS, LANES), F32)],
        compiler_params=_cparams(("arbitrary",)),
        name="moe_route",
    )(h, sh, sc, nw, rw_t, rb_col)


def _sc_worker_id():
    return lax.axis_index("s") * SC_CORES + lax.axis_index("c")


def _sc_scatter_rows(x, dest, n_rows):
    t, d = x.shape
    n_k = dest.shape[0]
    assert t % (SC_WORKERS * 2 * SC_CHUNK) == 0
    per_w = t // SC_WORKERS
    n_ch = per_w // SC_CHUNK
    dest4 = dest.reshape(n_k, SC_WORKERS, n_ch, SC_CHUNK).transpose(1, 2, 0, 3)
    mesh = plsc.VectorSubcoreMesh(core_axis_name="c", subcore_axis_name="s")

    @functools.partial(
        pl.kernel, mesh=mesh,
        out_type=jax.ShapeDtypeStruct((n_rows, d), x.dtype),
        scratch_types=[pltpu.VMEM((n_ch, n_k, SC_CHUNK), I32),
                       pltpu.VMEM((SC_CHUNK, d), x.dtype),
                       pltpu.VMEM((SC_CHUNK, d), x.dtype)] + [pltpu.SemaphoreType.DMA] * 4,
        name="moe_dispatch_sc",
    )
    def scatter_kernel(x_hbm, dest_hbm, out_hbm, idx_v, rows0, rows1, l0, l1, s0, s1):
        wid = _sc_worker_id()
        base = wid * per_w
        pltpu.sync_copy(dest_hbm.at[wid], idx_v)
        bufs = ((rows0, l0, s0), (rows1, l1, s1))

        def load(ci, b):
            rows, load_sem, _ = bufs[b]
            return pltpu.make_async_copy(x_hbm.at[pl.ds(base + ci * SC_CHUNK, SC_CHUNK)], rows,
                                         load_sem)

        def scatters(ci, b):
            rows, _, scatter_sem = bufs[b]
            return [pltpu.make_async_copy(rows, out_hbm.at[idx_v.at[ci, j]], scatter_sem)
                    for j in range(n_k)]

        load(0, 0).start()

        @pl.loop(0, n_ch, step=2)
        def _(ci):
            load(ci, 0).wait()
            for cp in scatters(ci, 0):
                cp.start()

            @pl.when(ci > 0)
            def _():
                for cp in scatters(ci - 1, 1):
                    cp.wait()

            load(ci + 1, 1).start()
            load(ci + 1, 1).wait()
            for cp in scatters(ci + 1, 1):
                cp.start()
            for cp in scatters(ci, 0):
                cp.wait()

            @pl.when(ci + 2 < n_ch)
            def _():
                load(ci + 2, 0).start()

        for cp in scatters(n_ch - 1, 1):
            cp.wait()

    return scatter_kernel(x, dest4)


def _sc_gather_rows(table, idx):
    n, d = idx.shape[0], table.shape[1]
    assert n % (SC_WORKERS * 2 * SC_CHUNK) == 0
    per_w = n // SC_WORKERS
    n_ch = per_w // SC_CHUNK
    idx3 = idx.reshape(SC_WORKERS, n_ch, SC_CHUNK)
    mesh = plsc.VectorSubcoreMesh(core_axis_name="c", subcore_axis_name="s")

    @functools.partial(
        pl.kernel, mesh=mesh,
        out_type=jax.ShapeDtypeStruct((n, d), table.dtype),
        scratch_types=[pltpu.VMEM((n_ch, SC_CHUNK), I32),
                       pltpu.VMEM((SC_CHUNK, d), table.dtype),
                       pltpu.VMEM((SC_CHUNK, d), table.dtype)] + [pltpu.SemaphoreType.DMA] * 4,
        name="moe_combine_sc",
    )
    def gather_kernel(table_hbm, idx_hbm, out_hbm, idx_v, rows0, rows1, g0, g1, w0, w1):
        wid = _sc_worker_id()
        base = wid * per_w
        pltpu.sync_copy(idx_hbm.at[wid], idx_v)
        bufs = ((rows0, g0, w0), (rows1, g1, w1))

        def gather(ci, b):
            rows, gather_sem, _ = bufs[b]
            return pltpu.make_async_copy(table_hbm.at[idx_v.at[ci]], rows, gather_sem)

        def write(ci, b):
            rows, _, write_sem = bufs[b]
            return pltpu.make_async_copy(rows, out_hbm.at[pl.ds(base + ci * SC_CHUNK, SC_CHUNK)],
                                         write_sem)

        gather(0, 0).start()

        @pl.loop(0, n_ch, step=2)
        def _(ci):
            gather(ci, 0).wait()
            write(ci, 0).start()

            @pl.when(ci > 0)
            def _():
                write(ci - 1, 1).wait()

            gather(ci + 1, 1).start()
            gather(ci + 1, 1).wait()
            write(ci + 1, 1).start()
            write(ci, 0).wait()

            @pl.when(ci + 2 < n_ch)
            def _():
                gather(ci + 2, 0).start()

        write(n_ch - 1, 1).wait()

    return gather_kernel(table, idx3)


def _expert_body(blk_e_ref, blk_first_ref, blk_valid_ref, x_ref, w1_ref, w3_ref, w2_ref, y_ref,
                 w1_scr, w3_scr, w2_scr):
    del blk_e_ref
    j = pl.program_id(0)
    valid = blk_valid_ref[j]

    @pl.when(blk_first_ref[j] == 1)
    def _():
        w1_scr[...] = w1_ref[0].astype(BF16)
        w3_scr[...] = w3_ref[0].astype(BF16)
        w2_scr[...] = w2_ref[0].astype(BF16)

    @pl.when(valid > 0)
    def _():
        half = x_ref.shape[1]
        rows = lax.broadcasted_iota(I32, x_ref.shape, 0)
        lo, hi = _unpack_bf16_pair(jnp.where(rows < valid, x_ref[...], 0))
        lo, hi = lo.astype(BF16), hi.astype(BF16)

        def proj(w_scr):
            return (jnp.dot(lo, w_scr[0:half, :], preferred_element_type=F32)
                    + jnp.dot(hi, w_scr[half:2 * half, :], preferred_element_type=F32))

        hid = _silu(proj(w1_scr)) * proj(w3_scr)
        y = jnp.dot(hid.astype(BF16), w2_scr[...], preferred_element_type=F32)
        y_ref[...] = _pack_bf16_pair(y[:, :half], y[:, half:])

    @pl.when(valid == 0)
    def _():
        y_ref[...] = jnp.zeros(y_ref.shape, U32)


def _experts(xg, blk_e, blk_first, blk_valid, w1, w3, w2, layer):
    n_rows, half = xg.shape
    n_blocks = n_rows // EXPERT_ROWS
    d, de = w1.shape[-2:]
    grid_spec = pltpu.PrefetchScalarGridSpec(
        num_scalar_prefetch=3,
        grid=(n_blocks,),
        in_specs=[pl.BlockSpec((EXPERT_ROWS, half), lambda j, be, bf, bv: (j, 0)),
                  pl.BlockSpec((None, 1, d, de), lambda j, be, bf, bv: (layer, be[j], 0, 0)),
                  pl.BlockSpec((None, 1, d, de), lambda j, be, bf, bv: (layer, be[j], 0, 0)),
                  pl.BlockSpec((None, 1, de, d), lambda j, be, bf, bv: (layer, be[j], 0, 0))],
        out_specs=pl.BlockSpec((EXPERT_ROWS, half), lambda j, be, bf, bv: (j, 0)),
        scratch_shapes=[pltpu.VMEM((d, de), BF16), pltpu.VMEM((d, de), BF16),
                        pltpu.VMEM((de, d), BF16)],
    )
    return pl.pallas_call(
        _expert_body,
        grid_spec=grid_spec,
        out_shape=jax.ShapeDtypeStruct((n_rows, half), U32),
        compiler_params=_cparams(("arbitrary",)),
        name="moe_experts",
    )(blk_e, blk_first, blk_valid, xg, w1, w3, w2)


def _combine_body(h_ref, u_ref, yg_ref, wts_ref, gt_ref, s1_ref, s3_ref, s2_ref, fw_ref, o_ref, *,
                  final_norm):
    half = u_ref.shape[1]
    lo, hi = _unpack_bf16_pair(u_ref[...])
    lo, hi = lo.astype(BF16), hi.astype(BF16)

    def proj(w_ref):
        return (jnp.dot(lo, w_ref[0:half, :], preferred_element_type=F32)
                + jnp.dot(hi, w_ref[half:2 * half, :], preferred_element_type=F32))

    hid = _silu(proj(s1_ref)) * proj(s3_ref)
    shared = jnp.dot(hid.astype(BF16), s2_ref[...], preferred_element_type=F32)

    routed_lo = routed_hi = None
    for kk in range(TOP_K):
        y_lo, y_hi = _unpack_bf16_pair(yg_ref[kk])
        w = wts_ref[:, kk:kk + 1]
        routed_lo = y_lo * w if routed_lo is None else routed_lo + y_lo * w
        routed_hi = y_hi * w if routed_hi is None else routed_hi + y_hi * w
    gt = gt_ref[0]
    out_lo = h_ref[:, 0:half] + gt[:, 0:half] * (routed_lo + shared[:, 0:half])
    out_hi = (h_ref[:, half:2 * half]
              + gt[:, half:2 * half] * (routed_hi + shared[:, half:2 * half]))
    if final_norm:
        ssq = (jnp.sum(out_lo * out_lo, axis=-1, keepdims=True)
               + jnp.sum(out_hi * out_hi, axis=-1, keepdims=True))
        inv = lax.rsqrt(ssq / (2 * half) + EPS)
        out_lo = (out_lo * inv) * fw_ref[:, 0:half]
        out_hi = (out_hi * inv) * fw_ref[:, half:2 * half]
    o_ref[:, 0:half] = out_lo
    o_ref[:, half:2 * half] = out_hi


def _combine(h, u, yg, wts_tk, gt, s1, s3, s2, final_w, seq, final_norm, part, n_parts):
    t, d = h.shape
    half = d // 2
    tm = min(COMBINE_ROWS, seq)
    per_b = seq // tm
    ds_ = s1.shape[-1]
    steps = t // n_parts // tm
    off = part * steps
    return pl.pallas_call(
        functools.partial(_combine_body, final_norm=final_norm),
        grid=(steps,),
        in_specs=[pl.BlockSpec((tm, d), lambda i: (i + off, 0)),
                  pl.BlockSpec((tm, half), lambda i: (i + off, 0)),
                  pl.BlockSpec((TOP_K, tm, half), lambda i: (0, i, 0)),
                  pl.BlockSpec((tm, TOP_K), lambda i: (i + off, 0)),
                  pl.BlockSpec((1, 1, d), lambda i: ((i + off) // per_b, 0, 0)),
                  pl.BlockSpec((d, ds_), lambda i: (0, 0)),
                  pl.BlockSpec((d, ds_), lambda i: (0, 0)),
                  pl.BlockSpec((ds_, d), lambda i: (0, 0)),
                  pl.BlockSpec((1, d), lambda i: (0, 0))],
        out_specs=pl.BlockSpec((tm, d), lambda i: (i + off, 0)),
        out_shape=jax.ShapeDtypeStruct((t, d), F32),
        input_output_aliases={0: 0},
        compiler_params=_cparams(("parallel",)),
        name="moe_combine",
    )(h, u, yg, wts_tk, gt, s1, s3, s2, final_w)


def _split_w_in(w_in):
    sizes = (2 * M_HEADS * M_DQK, M_HEADS * M_DV, M_HEADS * M_DV, M_HEADS, M_HEADS,
             G_HEADS * G_DK, G_HEADS * G_DK, G_HEADS * G_DV, G_RANK, G_HEADS * G_DV,
             D_MODEL, D_MODEL)
    offs = [0]
    for n in sizes:
        offs.append(offs[-1] + n)
    w16 = w_in.astype(BF16)
    big = jnp.concatenate([w16[:, offs[0]:offs[3]], w16[:, offs[5]:offs[8]], w16[:, offs[9]:offs[12]]],
                          axis=1)
    pad = jnp.zeros((w_in.shape[0], SMALL_COLS - 2 * M_HEADS - G_RANK), BF16)
    small = jnp.concatenate([w16[:, offs[3]:offs[5]], w16[:, offs[8]:offs[9]], pad], axis=1)
    return big, small


def _prepare_mixer_weights(w_in, w_pa, w_pb, w_o, layer, tie=None):
    if tie is not None:
        w_in, w_pa, w_pb, w_o, _ = lax.optimization_barrier((w_in, w_pa, w_pb, w_o, tie))
    w_big, w_small = _split_w_in(w_in[layer])
    return (w_big, w_small, w_pa[layer].astype(BF16), w_pb[layer].astype(BF16),
            w_o[layer].astype(BF16))


def _moe_layout(counts, eidx, pos, n_blocks):
    padded = (counts + EXPERT_ROWS - 1) // EXPERT_ROWS * EXPERT_ROWS
    pend = jnp.cumsum(padded)
    pstart = pend - padded
    experts = jnp.arange(N_EXPERTS, dtype=I32)
    dest = pos + jnp.sum(jnp.where(eidx[..., None] == experts, pstart, 0), axis=-1)
    blk_start = jnp.arange(n_blocks, dtype=I32) * EXPERT_ROWS
    owner = jnp.sum((pend[None, :] <= blk_start[:, None]).astype(I32), axis=1)
    blk_e = jnp.minimum(owner, N_EXPERTS - 1)
    prev = jnp.concatenate([jnp.full((1,), -1, I32), blk_e[:-1]])
    blk_first = (blk_e != prev).astype(I32)
    own = blk_e[:, None] == experts[None, :]
    rows_left = jnp.sum(jnp.where(own, (pstart + counts)[None, :], 0), axis=1) - blk_start
    blk_valid = jnp.clip(jnp.where(owner < N_EXPERTS, rows_left, 0), 0, EXPERT_ROWS)
    return dest.astype(I32), blk_e.astype(I32), blk_first, blk_valid.astype(I32)


def kernel(x, c, ada_w, ada_b, norm_mix_w, norm_moe_w, w_in, m_conv_w, m_gate_b, m_norm_w,
           g_alpha_w, g_alpha_b, g_norm_w, w_pa, w_pb, w_o, router_w, router_b,
           exp_w1, exp_w3, exp_w2, sh_w1, sh_w3, sh_w2, final_norm_w):
    bsz, seq, d = x.shape
    depth = ada_w.shape[0]
    t = bsz * seq
    n_rows = t * TOP_K + N_EXPERTS * EXPERT_ROWS
    n_blocks = n_rows // EXPERT_ROWS

    ada = _ada(c, ada_w, ada_b).reshape(depth, bsz, 6, 1, d)
    h = x.reshape(t, d)
    prepared = _prepare_mixer_weights(w_in, w_pa, w_pb, w_o, 0)
    for l in range(depth):
        sh1, sc1, gt1, sh2, sc2, gt2 = (ada[l, :, i] for i in range(6))

        w_big, w_small, w_pa_l, w_pb_l, w_o_l = prepared
        mix_w = norm_mix_w[l][None, :]
        gate_row = jnp.zeros((1, SMALL_COLS), F32)
        gate_row = gate_row.at[0, SMALL_I:SMALL_I + M_HEADS].set(m_gate_b[l, 0])
        gate_row = gate_row.at[0, SMALL_F:SMALL_F + M_HEADS].set(m_gate_b[l, 1])
        ha = _mlstm(h, sh1, sc1, mix_w, w_big, w_small, m_conv_w[l], gate_row, m_norm_w[l][None, :],
                    bsz, seq)
        alpha_full = jnp.zeros((SMALL_COLS, G_HEADS * G_DK), F32)
        alpha_full = alpha_full.at[SMALL_R:SMALL_R + G_RANK].set(g_alpha_w[l]).astype(BF16)
        hb = _gla(h, sh1, sc1, mix_w, w_big, w_small, alpha_full, g_alpha_b[l][None, :],
                  g_norm_w[l][None, :], bsz, seq)
        h = _mixout(h, sh1, sc1, mix_w, w_big, ha, hb, gt1, w_pa_l, w_pb_l, w_o_l, seq)

        u, eidx, wts, pos, cnt = _route(h, sh2, sc2, norm_moe_w[l][None, :],
                                        router_w[l].T, router_b[l][:, None], seq)
        dest, blk_e, blk_first, blk_valid = _moe_layout(cnt[:, 0], eidx, pos, n_blocks)
        xg = _sc_scatter_rows(u, dest, n_rows)
        if l + 1 < depth:
            prepared = _prepare_mixer_weights(w_in, w_pa, w_pb, w_o, l + 1, tie=dest)
        y = _experts(xg, blk_e, blk_first, blk_valid, exp_w1, exp_w3, exp_w2, l)
        tp = t // COMBINE_PARTS
        for p in range(COMBINE_PARTS):
            dest_p = dest[:, p * tp:(p + 1) * tp].reshape(-1)
            yg = _sc_gather_rows(y, dest_p).reshape(TOP_K, tp, d // 2)
            h = _combine(h, u, yg, wts.T, gt2, sh_w1[l].astype(BF16), sh_w3[l].astype(BF16),
                         sh_w2[l].astype(BF16), final_norm_w[None, :], seq,
                         final_norm=(l == depth - 1), part=p, n_parts=COMBINE_PARTS)

    return h.reshape(bsz, seq, d)
```
